```python
import jax, jax.numpy as jnp
from jax import lax
import numpy as np

D_MODEL = 1024
BATCH = 2
SEQ = 8192
DEPTH = 2
DEC_BATCH = 128
DEC_SEQ = 8
PAST_LEN = 2048
PAGE_SIZE = 128

HEAD_DIM = 64
NSA_HEADS = 4
NSA_BLOCK = 64
NSA_TOPK = 16
NSA_WINDOW = 512
NSA_ROWS = 4
MLSTM_HEADS = 4
MLSTM_CHUNK = 64
FOX_HEADS = 4
Q_BLOCK = 128
N_BRANCH = 3
BRANCH_WIDTH = NSA_HEADS * HEAD_DIM
N_GROUPS = 4
EXPERTS_PER_GROUP = 4
N_EXPERTS = N_GROUPS * EXPERTS_PER_GROUP
TOP_K_IN_GROUP = 2
D_EXPERT = 256
FORGET_BIAS = 3.0
LN_EPS = 1e-5
NEG_INF = -1e30
TINY = 1e-30
FORCED_SCORE = 1e9
ATTN_SCALE = HEAD_DIM ** -0.5

IN_SPLITS = (
    NSA_HEADS * HEAD_DIM,
    6 * HEAD_DIM,
    NSA_HEADS * 3,
    MLSTM_HEADS * HEAD_DIM,
    MLSTM_HEADS * HEAD_DIM,
    MLSTM_HEADS * HEAD_DIM,
    MLSTM_HEADS * HEAD_DIM,
    MLSTM_HEADS,
    MLSTM_HEADS,
    FOX_HEADS * HEAD_DIM,
    FOX_HEADS * HEAD_DIM,
    FOX_HEADS * HEAD_DIM,
    FOX_HEADS,
    N_BRANCH * D_MODEL,
)
IDX_ML_F = 8
IDX_FOX_F = 12

kernel_name = 'hybrid_nsa_mlstm_fox_hmoe_step'


def _split_points():
    return [int(v) for v in np.cumsum(IN_SPLITS)[:-1]]


def _layer_norm(x, w, b):
    xf = x.astype(jnp.float32)
    mu = jnp.mean(xf, axis=-1, keepdims=True)
    var = jnp.mean(jnp.square(xf - mu), axis=-1, keepdims=True)
    y = (xf - mu) * lax.rsqrt(var + LN_EPS) * w.astype(jnp.float32) + b.astype(jnp.float32)
    return y.astype(x.dtype)


def _masked_softmax(s, mask):
    s = jnp.where(mask, s.astype(jnp.float32), NEG_INF)
    p = jnp.exp(s - jnp.max(s, axis=-1, keepdims=True)) * mask
    return p / jnp.maximum(jnp.sum(p, axis=-1, keepdims=True), TINY)


def _alibi_slopes(n):
    return jnp.exp2(-8.0 * jnp.arange(1, n + 1, dtype=jnp.float32) / n)


def _project(x, w_in, b_in):
    B, L, _ = x.shape
    f32 = jnp.float32
    z = x @ w_in + b_in
    (nq, nkv, ng, mq, mk, mv, mo, mi, mf, fq, fk, fv, ff, gm) = jnp.split(z, _split_points(), axis=-1)
    heads = lambda t, h: t.reshape(B, L, h, HEAD_DIM)
    rows = nkv.reshape(B, L, 6, HEAD_DIM)
    return dict(
        nsa_q=heads(nq, NSA_HEADS),
        nsa_rows=rows[:, :, :NSA_ROWS],
        nsa_win=rows[:, :, NSA_ROWS:],
        nsa_g=jax.nn.sigmoid(ng.astype(f32)).reshape(B, L, NSA_HEADS, 3),
        ml_q=heads(mq, MLSTM_HEADS).astype(f32),
        ml_k=heads(mk, MLSTM_HEADS).astype(f32) * ATTN_SCALE,
        ml_v=heads(mv, MLSTM_HEADS).astype(f32),
        ml_o=jax.nn.sigmoid(mo.astype(f32)).reshape(B, L, MLSTM_HEADS, HEAD_DIM),
        ml_i=mi.astype(f32),
        ml_lf=jax.nn.log_sigmoid(mf.astype(f32)),
        fox_q=heads(fq, FOX_HEADS),
        fox_k=heads(fk, FOX_HEADS),
        fox_v=heads(fv, FOX_HEADS),
        fox_lf=jax.nn.log_sigmoid(ff.astype(f32)),
        merge=jax.nn.sigmoid(gm.astype(f32)).reshape(B, L, N_BRANCH, D_MODEL))


def _nsa_compress(k, w):
    B, L, _ = k.shape
    n_cb = L // NSA_BLOCK
    blocks = k[:, :n_cb * NSA_BLOCK].reshape(B, n_cb, NSA_BLOCK, HEAD_DIM)
    return jnp.einsum('bjpd,pde->bje', blocks, w)


def _nsa_cmp_attend(q, kc, vc, pos_q, slopes):
    n_cb = kc.shape[1]
    blk_end = jnp.arange(n_cb) * NSA_BLOCK + NSA_BLOCK - 1
    dist = pos_q[:, None] - blk_end[None, :]
    s = jnp.einsum('bqhd,bjd->bhqj', q, kc) * ATTN_SCALE - slopes[:, None, None] * dist.astype(jnp.float32)
    p = _masked_softmax(s, dist >= 0)
    o = jnp.einsum('bhqj,bjd->bqhd', p.astype(vc.dtype), vc)
    return o, jnp.sum(p, axis=1)


def _nsa_select_blocks(imp, pos_q, n_sb):
    n_cb = imp.shape[-1]
    imp = jnp.pad(imp, ((0, 0), (0, 0), (0, n_sb - n_cb)))
    j = jnp.arange(n_sb)[None, :]
    cur = (pos_q // NSA_BLOCK)[:, None]
    score = jnp.where(j > cur, NEG_INF, jnp.where((j == cur) | (j == 0), FORCED_SCORE, imp))
    vals, idx = lax.top_k(score, min(NSA_TOPK, n_sb))
    return idx, vals > 0.5 * NEG_INF


def _nsa_sel_attend(q, kb, vb, idx, valid, pos_q, slopes):
    B, Q = q.shape[:2]
    bi = jnp.arange(B)[:, None, None]
    kg = kb[bi, idx].reshape(B, Q, -1, HEAD_DIM)
    vg = vb[bi, idx].reshape(B, Q, -1, HEAD_DIM)
    pos_k = (idx[..., None] * NSA_BLOCK + jnp.arange(NSA_BLOCK)).reshape(B, Q, -1)
    ok = jnp.repeat(valid, NSA_BLOCK, axis=-1) & (pos_k <= pos_q[None, :, None])
    dist = (pos_q[None, :, None] - pos_k).astype(jnp.float32)
    s = jnp.einsum('bqhd,bqkd->bhqk', q, kg) * ATTN_SCALE - slopes[None, :, None, None] * dist[:, None]
    p = _masked_softmax(s, ok[:, None])
    return jnp.einsum('bhqk,bqkd->bqhd', p.astype(vg.dtype), vg)


def _nsa_cmp_sel(q, rows, pos_q, w_ck, w_cv, slopes):
    B, Q = q.shape[:2]
    L = rows.shape[1]
    kc = _nsa_compress(rows[:, :, 0], w_ck)
    vc = _nsa_compress(rows[:, :, 1], w_cv)
    o_cmp, imp = _nsa_cmp_attend(q, kc, vc, pos_q, slopes)
    n_sb = -(-L // NSA_BLOCK)
    idx, valid = _nsa_select_blocks(imp, pos_q, n_sb)
    pad = ((0, 0), (0, n_sb * NSA_BLOCK - L), (0, 0))
    kb = jnp.pad(rows[:, :, 2], pad).reshape(B, n_sb, NSA_BLOCK, HEAD_DIM)
    vb = jnp.pad(rows[:, :, 3], pad).reshape(B, n_sb, NSA_BLOCK, HEAD_DIM)
    qb = Q_BLOCK if Q % Q_BLOCK == 0 else Q
    nqb = Q // qb

    def block(args):
        qc, ic, okc, pc = args
        return _nsa_sel_attend(qc, kb, vb, ic, okc, pc, slopes)

    xs = (q.reshape(B, nqb, qb, NSA_HEADS, HEAD_DIM).swapaxes(0, 1),
          idx.reshape(B, nqb, qb, -1).swapaxes(0, 1),
          valid.reshape(B, nqb, qb, -1).swapaxes(0, 1),
          pos_q.reshape(nqb, qb))
    o_sel = lax.map(block, xs).swapaxes(0, 1).reshape(B, Q, NSA_HEADS, HEAD_DIM)
    return o_cmp, o_sel


def _nsa_win_attend(q, k, v, pos_q, pos_k, slopes):
    dist = pos_q[:, None] - pos_k[None, :]
    ok = (dist >= 0) & (dist < NSA_WINDOW) & (pos_k[None, :] >= 0)
    s = jnp.einsum('bqhd,bkd->bhqk', q, k) * ATTN_SCALE - slopes[:, None, None] * dist.astype(jnp.float32)
    p = _masked_softmax(s, ok)
    return jnp.einsum('bhqk,bkd->bqhd', p.astype(v.dtype), v)


def _nsa_win_prompt(q, k, v, slopes):
    B, S = q.shape[:2]
    nqb = S // Q_BLOCK
    nb = NSA_WINDOW // Q_BLOCK
    pad = ((0, 0), (NSA_WINDOW, 0), (0, 0))
    kp = jnp.pad(k, pad).reshape(B, nqb + nb, Q_BLOCK, HEAD_DIM)
    vp = jnp.pad(v, pad).reshape(B, nqb + nb, Q_BLOCK, HEAD_DIM)
    band = jnp.arange(nqb)[:, None] + jnp.arange(nb + 1)[None, :]
    kband = kp[:, band].reshape(B, nqb, (nb + 1) * Q_BLOCK, HEAD_DIM)
    vband = vp[:, band].reshape(B, nqb, (nb + 1) * Q_BLOCK, HEAD_DIM)
    qblk = q.reshape(B, nqb, Q_BLOCK, NSA_HEADS, HEAD_DIM)
    pos_q = jnp.arange(S).reshape(nqb, Q_BLOCK)
    pos_k = (jnp.arange(nqb) * Q_BLOCK - NSA_WINDOW)[:, None] + jnp.arange((nb + 1) * Q_BLOCK)[None, :]
    o = jax.vmap(_nsa_win_attend, in_axes=(1, 1, 1, 0, 0, None), out_axes=1)(qblk, kband, vband, pos_q, pos_k, slopes)
    return o.reshape(B, S, NSA_HEADS, HEAD_DIM)


def _nsa_combine(g, o_cmp, o_sel, o_win):
    return jnp.einsum('blhc,cblhd->blhd', g, jnp.stack([o_cmp, o_sel, o_win]))


def _mlstm_chunk(carry, xs):
    c, n, m = carry
    q, k, v, ig, lf = xs
    L = q.shape[1]
    b = jnp.cumsum(lf, axis=1)
    causal = jnp.tril(jnp.ones((L, L), bool))[None, :, :, None]
    dmat = jnp.where(causal, b[:, :, None, :] - b[:, None, :, :] + ig[:, None, :, :], NEG_INF)
    a = b + m[:, None, :]
    m_t = jnp.maximum(a, jnp.max(dmat, axis=2))
    wq = jnp.exp(dmat - m_t[:, :, None, :]) * jnp.einsum('bthd,bshd->btsh', q, k)
    inter = jnp.exp(a - m_t)
    num = inter[..., None] * jnp.einsum('bthd,bhde->bthe', q, c) + jnp.einsum('btsh,bshe->bthe', wq, v)
    den = inter * jnp.einsum('bthd,bhd->bth', q, n) + jnp.sum(wq, axis=2)
    h = num / jnp.maximum(jnp.abs(den), jnp.exp(-m_t))[..., None]
    bl = b[:, -1]
    g = bl[:, None, :] - b + ig
    m_new = jnp.maximum(bl + m, jnp.max(g, axis=1))
    ws = jnp.exp(g - m_new[:, None, :])
    decay = jnp.exp(bl + m - m_new)
    c_new = decay[..., None, None] * c + jnp.einsum('bsh,bshd,bshe->bhde', ws, k, v)
    n_new = decay[..., None] * n + jnp.einsum('bsh,bshd->bhd', ws, k)
    return (c_new, n_new, m_new), h


def _mlstm_prompt(q, k, v, ig, lf):
    B, S, H, _ = q.shape
    ch = MLSTM_CHUNK if S % MLSTM_CHUNK == 0 else S
    nc = S // ch
    to_chunks = lambda t: t.reshape((B, nc, ch) + t.shape[2:]).swapaxes(0, 1)
    f32 = jnp.float32
    init = (jnp.zeros((B, H, HEAD_DIM, HEAD_DIM), f32), jnp.zeros((B, H, HEAD_DIM), f32), jnp.zeros((B, H), f32))
    state, h = lax.scan(_mlstm_chunk, init, tuple(to_chunks(t) for t in (q, k, v, ig, lf)))
    return h.swapaxes(0, 1).reshape(B, S, H, HEAD_DIM), state


def _mlstm_readout(h, o_gate, norm_w):
    h = o_gate * h
    mu = jnp.mean(h, axis=-1, keepdims=True)
    var = jnp.mean(jnp.square(h - mu), axis=-1, keepdims=True)
    return (h - mu) * lax.rsqrt(var + LN_EPS) * norm_w.astype(jnp.float32).reshape(MLSTM_HEADS, HEAD_DIM)


def _fox_attend(q, k, v, fq, fk, pos_q, pos_k):
    s = jnp.einsum('bqhd,bkhd->bhqk', q, k) * ATTN_SCALE
    s = s + (fq.transpose(0, 2, 1)[..., None] - fk.transpose(0, 2, 1)[:, :, None, :])
    p = _masked_softmax(s, pos_k[None, :] <= pos_q[:, None])
    return jnp.einsum('bhqk,bkhd->bqhd', p.astype(v.dtype), v)


def _fox_prompt(q, k, v, F):
    B, S = q.shape[:2]
    qb = Q_BLOCK if S % Q_BLOCK == 0 else S
    nqb = S // qb
    pos = jnp.arange(S)

    def block(args):
        qc, fc, pc = args
        return _fox_attend(qc, k, v, fc, F, pc, pos)

    xs = (q.reshape(B, nqb, qb, FOX_HEADS, HEAD_DIM).swapaxes(0, 1),
          F.reshape(B, nqb, qb, FOX_HEADS).swapaxes(0, 1),
          pos.reshape(nqb, qb))
    return lax.map(block, xs).swapaxes(0, 1).reshape(B, S, FOX_HEADS, HEAD_DIM)


def _merge(gates, o_nsa, o_ml, o_fox, w_branch, w_out, dtype):
    B, L = o_nsa.shape[:2]
    br = jnp.stack([o.reshape(B, L, BRANCH_WIDTH).astype(dtype) for o in (o_nsa, o_ml, o_fox)], axis=2)
    proj = jnp.einsum('blmc,mcd->blmd', br, w_branch)
    y = jnp.einsum('blmd,blmd->bld', gates, proj).astype(dtype)
    return y @ w_out


def _moe(x, w_group, b_group, w_expert, b_expert, w_gate, w_up, w_down):
    B, L, D = x.shape
    t = x.reshape(B * L, D)
    pg = jax.nn.softmax((t @ w_group + b_group).astype(jnp.float32), axis=-1)
    g_val, g_idx = lax.top_k(pg, 1)
    le = (t @ w_expert + b_expert).astype(jnp.float32).reshape(-1, N_GROUPS, EXPERTS_PER_GROUP)
    le = jnp.take_along_axis(le, g_idx[:, :, None], axis=1)[:, 0]
    e_val, e_idx = lax.top_k(jax.nn.softmax(le, axis=-1), TOP_K_IN_GROUP)
    e_w = g_val * e_val / jnp.sum(e_val, axis=-1, keepdims=True)
    gate = jnp.sum(jax.nn.one_hot(g_idx * EXPERTS_PER_GROUP + e_idx, N_EXPERTS, dtype=jnp.float32) * e_w[..., None], axis=1)
    h = jax.nn.silu(jnp.einsum('td,edf->tef', t, w_gate)) * jnp.einsum('td,edf->tef', t, w_up)
    h = h * gate[:, :, None].astype(h.dtype)
    return jnp.einsum('tef,efd->td', h, w_down).reshape(B, L, D)


def _mixers_prompt(x, w_in, b_in, w_ck, w_cv, ml_norm_w, w_branch, w_out, slopes):
    B, S, _ = x.shape
    pr = _project(x, w_in, b_in)
    pos = jnp.arange(S)
    o_cmp, o_sel = _nsa_cmp_sel(pr['nsa_q'], pr['nsa_rows'], pos, w_ck, w_cv, slopes)
    o_win = _nsa_win_prompt(pr['nsa_q'], pr['nsa_win'][:, :, 0], pr['nsa_win'][:, :, 1], slopes)
    o_nsa = _nsa_combine(pr['nsa_g'], o_cmp, o_sel, o_win)
    h, (c, n, m) = _mlstm_prompt(pr['ml_q'], pr['ml_k'], pr['ml_v'], pr['ml_i'], pr['ml_lf'])
    o_ml = _mlstm_readout(h, pr['ml_o'], ml_norm_w)
    o_fox = _fox_prompt(pr['fox_q'], pr['fox_k'], pr['fox_v'], jnp.cumsum(pr['fox_lf'], axis=1))
    out = _merge(pr['merge'], o_nsa, o_ml, o_fox, w_branch, w_out, x.dtype)
    w_keep = min(NSA_WINDOW, S)
    new = (pr['nsa_rows'], pr['nsa_win'][:, S - w_keep:], jnp.stack([pr['fox_k'], pr['fox_v']], axis=2),
           pr['fox_lf'], c, n, m)
    return out, new


def _mixers_sample(x, c_nsa, c_win, c_fox_kv, c_fox_lf, s_c, s_n, s_m, page_table,
                   w_in, b_in, w_ck, w_cv, ml_norm_w, w_branch, w_out, slopes):
    DB, T, _ = x.shape
    past = page_table.shape[1] * PAGE_SIZE
    f32 = jnp.float32
    pr = _project(x, w_in, b_in)
    pos_q = past + jnp.arange(T)
    rows_past = c_nsa[page_table].reshape(DB, past, NSA_ROWS, HEAD_DIM)
    rows_all = jnp.concatenate([rows_past, pr['nsa_rows'].astype(rows_past.dtype)], axis=1)
    o_cmp, o_sel = _nsa_cmp_sel(pr['nsa_q'], rows_all, pos_q, w_ck, w_cv, slopes)
    wb = c_win.shape[1]
    win_all = jnp.concatenate([c_win, pr['nsa_win'].astype(c_win.dtype)], axis=1)
    pos_k = past - wb + jnp.arange(wb + T)
    o_win = _nsa_win_attend(pr['nsa_q'], win_all[:, :, 0], win_all[:, :, 1], pos_q, pos_k, slopes)
    o_nsa = _nsa_combine(pr['nsa_g'], o_cmp, o_sel, o_win)
    (c, n, m), h = _mlstm_chunk((s_c.astype(f32), s_n.astype(f32), s_m.astype(f32)),
                                (pr['ml_q'], pr['ml_k'], pr['ml_v'], pr['ml_i'], pr['ml_lf']))
    o_ml = _mlstm_readout(h, pr['ml_o'], ml_norm_w)
    kv_past = c_fox_kv[page_table].reshape(DB, past, 2, FOX_HEADS, HEAD_DIM)
    k_all = jnp.concatenate([kv_past[:, :, 0], pr['fox_k'].astype(kv_past.dtype)], axis=1)
    v_all = jnp.concatenate([kv_past[:, :, 1], pr['fox_v'].astype(kv_past.dtype)], axis=1)
    lf_all = jnp.concatenate([c_fox_lf[page_table].reshape(DB, past, FOX_HEADS).astype(f32), pr['fox_lf']], axis=1)
    F = jnp.cumsum(lf_all, axis=1)
    o_fox = _fox_attend(pr['fox_q'], k_all, v_all, F[:, past:], F, pos_q, jnp.arange(past + T))
    out = _merge(pr['merge'], o_nsa, o_ml, o_fox, w_branch, w_out, x.dtype)
    new = (pr['nsa_rows'], win_all[:, T:], jnp.stack([pr['fox_k'], pr['fox_v']], axis=2),
           pr['fox_lf'], c, n, m)
    return out, new


def _stack_layers(states):
    return tuple(jnp.stack(list(a)) for a in zip(*states))


def setup_inputs(seed: int = 0) -> dict:
    key = jax.random.key(seed)
    keys = iter(jax.random.split(key, 40))
    f32 = jnp.float32

    def nrm(shape, scale):
        return jax.random.normal(next(keys), shape, f32) * scale

    beta = (8.0 * DEPTH) ** -0.25
    n_pages = PAST_LEN // PAGE_SIZE
    n_used = DEC_BATCH * n_pages
    n_phys = n_used + max(n_used // 4, 1)
    page_table = jax.random.permutation(next(keys), n_phys)[:n_used].reshape(DEC_BATCH, n_pages).astype(jnp.int32)
    win_buf = min(NSA_WINDOW, PAST_LEN)
    n_in = sum(IN_SPLITS)
    starts = np.concatenate([[0], np.cumsum(IN_SPLITS)])
    offset = np.zeros((n_in,), np.float32)
    offset[starts[IDX_ML_F]:starts[IDX_ML_F + 1]] = FORGET_BIAS
    offset[starts[IDX_FOX_F]:starts[IDX_FOX_F + 1]] = FORGET_BIAS
    return {
        'x_prompt': nrm((BATCH, SEQ, D_MODEL), 1.0),
        'x_sample': nrm((DEC_BATCH, DEC_SEQ, D_MODEL), 1.0),
        'cache_nsa': nrm((DEPTH, n_phys, PAGE_SIZE, NSA_ROWS, HEAD_DIM), 1.0),
        'cache_nsa_win': nrm((DEPTH, DEC_BATCH, win_buf, 2, HEAD_DIM), 1.0),
        'cache_fox_kv': nrm((DEPTH, n_phys, PAGE_SIZE, 2, FOX_HEADS, HEAD_DIM), 1.0),
        'cache_fox_logf': jax.nn.log_sigmoid(FORGET_BIAS + nrm((DEPTH, n_phys, PAGE_SIZE, FOX_HEADS), 1.0)),
        'state_mlstm_c': nrm((DEPTH, DEC_BATCH, MLSTM_HEADS, HEAD_DIM, HEAD_DIM), 1.0),
        'state_mlstm_n': nrm((DEPTH, DEC_BATCH, MLSTM_HEADS, HEAD_DIM), 1.0),
        'state_mlstm_m': nrm((DEPTH, DEC_BATCH, MLSTM_HEADS), 1.0),
        'page_table': page_table,
        'ln_in_w': 1.0 + nrm((D_MODEL,), 0.02),
        'ln_in_b': nrm((D_MODEL,), 0.02),
        'w_in': nrm((DEPTH, D_MODEL, n_in), D_MODEL ** -0.5),
        'b_in': nrm((DEPTH, n_in), 0.02) + jnp.asarray(offset),
        'nsa_w_ck': nrm((DEPTH, NSA_BLOCK, HEAD_DIM, HEAD_DIM), (NSA_BLOCK * HEAD_DIM) ** -0.5),
        'nsa_w_cv': nrm((DEPTH, NSA_BLOCK, HEAD_DIM, HEAD_DIM), (NSA_BLOCK * HEAD_DIM) ** -0.5),
        'mlstm_norm_w': 1.0 + nrm((DEPTH, MLSTM_HEADS * HEAD_DIM), 0.02),
        'w_branch': nrm((DEPTH, N_BRANCH, BRANCH_WIDTH, D_MODEL), beta * BRANCH_WIDTH ** -0.5),
        'w_out': nrm((DEPTH, D_MODEL, D_MODEL), beta * D_MODEL ** -0.5),
        'ln1_w': 1.0 + nrm((DEPTH, D_MODEL), 0.02),
        'ln1_b': nrm((DEPTH, D_MODEL), 0.02),
        'moe_w_group': nrm((DEPTH, D_MODEL, N_GROUPS), D_MODEL ** -0.5),
        'moe_b_group': nrm((DEPTH, N_GROUPS), 0.01),
        'moe_w_expert': nrm((DEPTH, D_MODEL, N_EXPERTS), D_MODEL ** -0.5),
        'moe_b_expert': nrm((DEPTH, N_EXPERTS), 0.01),
        'moe_w_gate': nrm((DEPTH, N_EXPERTS, D_MODEL, D_EXPERT), D_MODEL ** -0.5),
        'moe_w_up': nrm((DEPTH, N_EXPERTS, D_MODEL, D_EXPERT), D_MODEL ** -0.5),
        'moe_w_down': nrm((DEPTH, N_EXPERTS, D_EXPERT, D_MODEL), beta * D_EXPERT ** -0.5),
        'ln2_w': 1.0 + nrm((DEPTH, D_MODEL), 0.02),
        'ln2_b': nrm((DEPTH, D_MODEL), 0.02),
    }


def reference(x_prompt, x_sample, cache_nsa, cache_nsa_win, cache_fox_kv, cache_fox_logf,
              state_mlstm_c, state_mlstm_n, state_mlstm_m, page_table,
              ln_in_w, ln_in_b, w_in, b_in, nsa_w_ck, nsa_w_cv, mlstm_norm_w, w_branch, w_out,
              ln1_w, ln1_b, moe_w_group, moe_b_group, moe_w_expert, moe_b_expert,
              moe_w_gate, moe_w_up, moe_w_down, ln2_w, ln2_b):
    alpha = (2.0 * DEPTH) ** 0.25
    slopes = _alibi_slopes(NSA_HEADS)
    xp = _layer_norm(x_prompt, ln_in_w, ln_in_b)
    xs = _layer_norm(x_sample, ln_in_w, ln_in_b)
    new_p, new_s = [], []
    for l in range(DEPTH):
        mix_w = (w_in[l], b_in[l], nsa_w_ck[l], nsa_w_cv[l], mlstm_norm_w[l], w_branch[l], w_out[l])
        moe_w = (moe_w_group[l], moe_b_group[l], moe_w_expert[l], moe_b_expert[l],
                 moe_w_gate[l], moe_w_up[l], moe_w_down[l])
        mix_p, st_p = _mixers_prompt(xp, *mix_w, slopes)
        xp = _layer_norm(alpha * xp + mix_p, ln1_w[l], ln1_b[l])
        xp = _layer_norm(alpha * xp + _moe(xp, *moe_w), ln2_w[l], ln2_b[l])
        mix_s, st_s = _mixers_sample(xs, cache_nsa[l], cache_nsa_win[l], cache_fox_kv[l], cache_fox_logf[l],
                                     state_mlstm_c[l], state_mlstm_n[l], state_mlstm_m[l], page_table,
                                     *mix_w, slopes)
        xs = _layer_norm(alpha * xs + mix_s, ln1_w[l], ln1_b[l])
        xs = _layer_norm(alpha * xs + _moe(xs, *moe_w), ln2_w[l], ln2_b[l])
        new_p.append(st_p)
        new_s.append(st_s)
    (p_nsa_rows, p_nsa_win, p_fox_kv, p_fox_logf, p_mlstm_c, p_mlstm_n, p_mlstm_m) = _stack_layers(new_p)
    (s_nsa_rows, s_nsa_win, s_fox_kv, s_fox_logf, s_mlstm_c, s_mlstm_n, s_mlstm_m) = _stack_layers(new_s)
    return (xp, xs, p_nsa_rows, p_nsa_win, p_fox_kv, p_fox_logf, p_mlstm_c, p_mlstm_n, p_mlstm_m,
            s_nsa_rows, s_nsa_win, s_fox_kv, s_fox_logf, s_mlstm_c, s_mlstm_n, s_mlstm_m)
```

```python
import functools

import jax
import jax.numpy as jnp
import numpy as np
from jax import lax
from jax.experimental import pallas as pl
from jax.experimental.pallas import tpu as pltpu

D_MODEL = 1024
BATCH = 2
SEQ = 8192
DEPTH = 2
DEC_BATCH = 128
DEC_SEQ = 8
PAST_LEN = 2048
PAGE_SIZE = 128

HEAD_DIM = 64
NSA_HEADS = 4
NSA_BLOCK = 64
NSA_TOPK = 16
NSA_WINDOW = 512
NSA_ROWS = 4
MLSTM_HEADS = 4
MLSTM_CHUNK = 64
FOX_HEADS = 4
Q_BLOCK = 128
N_BRANCH = 3
BRANCH_WIDTH = NSA_HEADS * HEAD_DIM
N_GROUPS = 4
EXPERTS_PER_GROUP = 4
N_EXPERTS = N_GROUPS * EXPERTS_PER_GROUP
TOP_K_IN_GROUP = 2
D_EXPERT = 256
LN_EPS = 1e-5
NEG_INF = -1e30
TINY = 1e-30
FORCED_SCORE = 1e9
ATTN_SCALE = HEAD_DIM ** -0.5

IN_SPLITS = (
    NSA_HEADS * HEAD_DIM,
    6 * HEAD_DIM,
    NSA_HEADS * 3,
    MLSTM_HEADS * HEAD_DIM,
    MLSTM_HEADS * HEAD_DIM,
    MLSTM_HEADS * HEAD_DIM,
    MLSTM_HEADS * HEAD_DIM,
    MLSTM_HEADS,
    MLSTM_HEADS,
    FOX_HEADS * HEAD_DIM,
    FOX_HEADS * HEAD_DIM,
    FOX_HEADS * HEAD_DIM,
    FOX_HEADS,
    N_BRANCH * D_MODEL,
)


def _linear_kernel(x_ref, w_ref, b_ref, o_ref):
    x = x_ref[...].astype(jnp.bfloat16)
    o_ref[...] = jnp.dot(x, w_ref[...], preferred_element_type=jnp.float32) + b_ref[...]


def _pallas_linear(x, w, b, tm=512, tn=512):
    T, K = x.shape
    N = w.shape[1]
    n_pad = -(-N // tn) * tn
    wp = jnp.pad(w.astype(jnp.bfloat16), ((0, 0), (0, n_pad - N)))
    bp = jnp.pad(b.astype(jnp.float32), (0, n_pad - N)).reshape(1, n_pad)
    out = pl.pallas_call(
        _linear_kernel,
        grid=(T // tm, n_pad // tn),
        in_specs=[pl.BlockSpec((tm, K), lambda i, j: (i, 0)),
                  pl.BlockSpec((K, tn), lambda i, j: (0, j)),
                  pl.BlockSpec((1, tn), lambda i, j: (0, j))],
        out_specs=pl.BlockSpec((tm, tn), lambda i, j: (i, j)),
        out_shape=jax.ShapeDtypeStruct((T, n_pad), jnp.float32),
        name="linear",
    )(x, wp, bp)
    return out[:, :N]


def _split_points():
    return [int(v) for v in np.cumsum(IN_SPLITS)[:-1]]


def _layer_norm(x, w, b):
    xf = x.astype(jnp.float32)
    mu = jnp.mean(xf, axis=-1, keepdims=True)
    var = jnp.mean(jnp.square(xf - mu), axis=-1, keepdims=True)
    y = (xf - mu) * lax.rsqrt(var + LN_EPS) * w.astype(jnp.float32) + b.astype(jnp.float32)
    return y.astype(x.dtype)


def _masked_softmax(s, mask):
    s = jnp.where(mask, s.astype(jnp.float32), NEG_INF)
    p = jnp.exp(s - jnp.max(s, axis=-1, keepdims=True)) * mask
    return p / jnp.maximum(jnp.sum(p, axis=-1, keepdims=True), TINY)


def _alibi_slopes(n):
    return jnp.exp2(-8.0 * jnp.arange(1, n + 1, dtype=jnp.float32) / n)


def _project(x, w_in, b_in):
    B, L, _ = x.shape
    f32 = jnp.float32
    if L == DEC_SEQ:
        z = _pallas_linear(x.reshape(B * L, -1), w_in, b_in).reshape(B, L, -1)
    else:
        z = x @ w_in + b_in
    (nq, nkv, ng, mq, mk, mv, mo, mi, mf, fq, fk, fv, ff, gm) = jnp.split(z, _split_points(), axis=-1)
    heads = lambda t, h: t.reshape(B, L, h, HEAD_DIM)
    rows = nkv.reshape(B, L, 6, HEAD_DIM)
    return dict(
        nsa_q=heads(nq, NSA_HEADS),
        nsa_rows=rows[:, :, :NSA_ROWS],
        nsa_win=rows[:, :, NSA_ROWS:],
        nsa_g=jax.nn.sigmoid(ng.astype(f32)).reshape(B, L, NSA_HEADS, 3),
        ml_q=heads(mq, MLSTM_HEADS).astype(f32),
        ml_k=heads(mk, MLSTM_HEADS).astype(f32) * ATTN_SCALE,
        ml_v=heads(mv, MLSTM_HEADS).astype(f32),
        ml_o=jax.nn.sigmoid(mo.astype(f32)).reshape(B, L, MLSTM_HEADS, HEAD_DIM),
        ml_i=mi.astype(f32),
        ml_lf=jax.nn.log_sigmoid(mf.astype(f32)),
        fox_q=heads(fq, FOX_HEADS),
        fox_k=heads(fk, FOX_HEADS),
        fox_v=heads(fv, FOX_HEADS),
        fox_lf=jax.nn.log_sigmoid(ff.astype(f32)),
        merge=jax.nn.sigmoid(gm.astype(f32)).reshape(B, L, N_BRANCH, D_MODEL))


def _nsa_compress(k, w):
    B, L, _ = k.shape
    n_cb = L // NSA_BLOCK
    blocks = k[:, :n_cb * NSA_BLOCK].reshape(B, n_cb, NSA_BLOCK, HEAD_DIM)
    return jnp.einsum('bjpd,pde->bje', blocks, w)


def _nsa_cmp_attend(q, kc, vc, pos_q, slopes):
    n_cb = kc.shape[1]
    blk_end = jnp.arange(n_cb) * NSA_BLOCK + NSA_BLOCK - 1
    dist = pos_q[:, None] - blk_end[None, :]
    s = jnp.einsum('bqhd,bjd->bhqj', q, kc) * ATTN_SCALE - slopes[:, None, None] * dist.astype(jnp.float32)
    p = _masked_softmax(s, dist >= 0)
    o = jnp.einsum('bhqj,bjd->bqhd', p.astype(vc.dtype), vc)
    return o, jnp.sum(p, axis=1)


def _nsa_select_blocks(imp, pos_q, n_sb):
    n_cb = imp.shape[-1]
    imp = jnp.pad(imp, ((0, 0), (0, 0), (0, n_sb - n_cb)))
    j = jnp.arange(n_sb)[None, :]
    cur = (pos_q // NSA_BLOCK)[:, None]
    score = jnp.where(j > cur, NEG_INF, jnp.where((j == cur) | (j == 0), FORCED_SCORE, imp))
    vals, idx = lax.top_k(score, min(NSA_TOPK, n_sb))
    return idx, vals > 0.5 * NEG_INF


def _nsa_sel_attend(q, kb, vb, idx, valid, pos_q, slopes):
    B, Q = q.shape[:2]
    bi = jnp.arange(B)[:, None, None]
    kg = kb[bi, idx].reshape(B, Q, -1, HEAD_DIM)
    vg = vb[bi, idx].reshape(B, Q, -1, HEAD_DIM)
    pos_k = (idx[..., None] * NSA_BLOCK + jnp.arange(NSA_BLOCK)).reshape(B, Q, -1)
    ok = jnp.repeat(valid, NSA_BLOCK, axis=-1) & (pos_k <= pos_q[None, :, None])
    dist = (pos_q[None, :, None] - pos_k).astype(jnp.float32)
    s = jnp.einsum('bqhd,bqkd->bhqk', q, kg) * ATTN_SCALE - slopes[None, :, None, None] * dist[:, None]
    p = _masked_softmax(s, ok[:, None])
    return jnp.einsum('bhqk,bqkd->bqhd', p.astype(vg.dtype), vg)


def _nsa_cmp_sel(q, rows, pos_q, w_ck, w_cv, slopes):
    B, Q = q.shape[:2]
    L = rows.shape[1]
    kc = _nsa_compress(rows[:, :, 0], w_ck)
    vc = _nsa_compress(rows[:, :, 1], w_cv)
    o_cmp, imp = _nsa_cmp_attend(q, kc, vc, pos_q, slopes)
    n_sb = -(-L // NSA_BLOCK)
    idx, valid = _nsa_select_blocks(imp, pos_q, n_sb)
    pad = ((0, 0), (0, n_sb * NSA_BLOCK - L), (0, 0))
    kb = jnp.pad(rows[:, :, 2], pad).reshape(B, n_sb, NSA_BLOCK, HEAD_DIM)
    vb = jnp.pad(rows[:, :, 3], pad).reshape(B, n_sb, NSA_BLOCK, HEAD_DIM)
    qb = Q_BLOCK if Q % Q_BLOCK == 0 else Q
    nqb = Q // qb

    def block(args):
        qc, ic, okc, pc = args
        return _nsa_sel_attend(qc, kb, vb, ic, okc, pc, slopes)

    xs = (q.reshape(B, nqb, qb, NSA_HEADS, HEAD_DIM).swapaxes(0, 1),
          idx.reshape(B, nqb, qb, -1).swapaxes(0, 1),
          valid.reshape(B, nqb, qb, -1).swapaxes(0, 1),
          pos_q.reshape(nqb, qb))
    o_sel = lax.map(block, xs).swapaxes(0, 1).reshape(B, Q, NSA_HEADS, HEAD_DIM)
    return o_cmp, o_sel


def _nsa_win_attend(q, k, v, pos_q, pos_k, slopes):
    dist = pos_q[:, None] - pos_k[None, :]
    ok = (dist >= 0) & (dist < NSA_WINDOW) & (pos_k[None, :] >= 0)
    s = jnp.einsum('bqhd,bkd->bhqk', q, k) * ATTN_SCALE - slopes[:, None, None] * dist.astype(jnp.float32)
    p = _masked_softmax(s, ok)
    return jnp.einsum('bhqk,bkd->bqhd', p.astype(v.dtype), v)


def _nsa_win_prompt(q, k, v, slopes):
    B, S = q.shape[:2]
    nqb = S // Q_BLOCK
    nb = NSA_WINDOW // Q_BLOCK
    pad = ((0, 0), (NSA_WINDOW, 0), (0, 0))
    kp = jnp.pad(k, pad).reshape(B, nqb + nb, Q_BLOCK, HEAD_DIM)
    vp = jnp.pad(v, pad).reshape(B, nqb + nb, Q_BLOCK, HEAD_DIM)
    band = jnp.arange(nqb)[:, None] + jnp.arange(nb + 1)[None, :]
    kband = kp[:, band].reshape(B, nqb, (nb + 1) * Q_BLOCK, HEAD_DIM)
    vband = vp[:, band].reshape(B, nqb, (nb + 1) * Q_BLOCK, HEAD_DIM)
    qblk = q.reshape(B, nqb, Q_BLOCK, NSA_HEADS, HEAD_DIM)
    pos_q = jnp.arange(S).reshape(nqb, Q_BLOCK)
    pos_k = (jnp.arange(nqb) * Q_BLOCK - NSA_WINDOW)[:, None] + jnp.arange((nb + 1) * Q_BLOCK)[None, :]
    o = jax.vmap(_nsa_win_attend, in_axes=(1, 1, 1, 0, 0, None), out_axes=1)(qblk, kband, vband, pos_q, pos_k, slopes)
    return o.reshape(B, S, NSA_HEADS, HEAD_DIM)


def _nsa_combine(g, o_cmp, o_sel, o_win):
    return jnp.einsum('blhc,cblhd->blhd', g, jnp.stack([o_cmp, o_sel, o_win]))


def _mlstm_chunk(carry, xs):
    c, n, m = carry
    q, k, v, ig, lf = xs
    L = q.shape[1]
    b = jnp.cumsum(lf, axis=1)
    causal = jnp.tril(jnp.ones((L, L), bool))[None, :, :, None]
    dmat = jnp.where(causal, b[:, :, None, :] - b[:, None, :, :] + ig[:, None, :, :], NEG_INF)
    a = b + m[:, None, :]
    m_t = jnp.maximum(a, jnp.max(dmat, axis=2))
    wq = jnp.exp(dmat - m_t[:, :, None, :]) * jnp.einsum('bthd,bshd->btsh', q, k)
    inter = jnp.exp(a - m_t)
    num = inter[..., None] * jnp.einsum('bthd,bhde->bthe', q, c) + jnp.einsum('btsh,bshe->bthe', wq, v)
    den = inter * jnp.einsum('bthd,bhd->bth', q, n) + jnp.sum(wq, axis=2)
    h = num / jnp.maximum(jnp.abs(den), jnp.exp(-m_t))[..., None]
    bl = b[:, -1]
    g = bl[:, None, :] - b + ig
    m_new = jnp.maximum(bl + m, jnp.max(g, axis=1))
    ws = jnp.exp(g - m_new[:, None, :])
    decay = jnp.exp(bl + m - m_new)
    c_new = decay[..., None, None] * c + jnp.einsum('bsh,bshd,bshe->bhde', ws, k, v)
    n_new = decay[..., None] * n + jnp.einsum('bsh,bshd->bhd', ws, k)
    return (c_new, n_new, m_new), h


def _mlstm_prompt(q, k, v, ig, lf):
    B, S, H, _ = q.shape
    ch = MLSTM_CHUNK if S % MLSTM_CHUNK == 0 else S
    nc = S // ch
    to_chunks = lambda t: t.reshape((B, nc, ch) + t.shape[2:]).swapaxes(0, 1)
    f32 = jnp.float32
    init = (jnp.zeros((B, H, HEAD_DIM, HEAD_DIM), f32), jnp.zeros((B, H, HEAD_DIM), f32), jnp.zeros((B, H), f32))
    state, h = lax.scan(_mlstm_chunk, init, tuple(to_chunks(t) for t in (q, k, v, ig, lf)))
    return h.swapaxes(0, 1).reshape(B, S, H, HEAD_DIM), state


def _mlstm_readout(h, o_gate, norm_w):
    h = o_gate * h
    mu = jnp.mean(h, axis=-1, keepdims=True)
    var = jnp.mean(jnp.square(h - mu), axis=-1, keepdims=True)
    return (h - mu) * lax.rsqrt(var + LN_EPS) * norm_w.astype(jnp.float32).reshape(MLSTM_HEADS, HEAD_DIM)


def _fox_attend(q, k, v, fq, fk, pos_q, pos_k):
    s = jnp.einsum('bqhd,bkhd->bhqk', q, k) * ATTN_SCALE
    s = s + (fq.transpose(0, 2, 1)[..., None] - fk.transpose(0, 2, 1)[:, :, None, :])
    p = _masked_softmax(s, pos_k[None, :] <= pos_q[:, None])
    return jnp.einsum('bhqk,bkhd->bqhd', p.astype(v.dtype), v)


def _fox_prompt(q, k, v, F):
    B, S = q.shape[:2]
    qb = Q_BLOCK if S % Q_BLOCK == 0 else S
    nqb = S // qb
    pos = jnp.arange(S)

    def block(args):
        qc, fc, pc = args
        return _fox_attend(qc, k, v, fc, F, pc, pos)

    xs = (q.reshape(B, nqb, qb, FOX_HEADS, HEAD_DIM).swapaxes(0, 1),
          F.reshape(B, nqb, qb, FOX_HEADS).swapaxes(0, 1),
          pos.reshape(nqb, qb))
    return lax.map(block, xs).swapaxes(0, 1).reshape(B, S, FOX_HEADS, HEAD_DIM)


def _merge(gates, o_nsa, o_ml, o_fox, w_branch, w_out, dtype):
    B, L = o_nsa.shape[:2]
    br = jnp.stack([o.reshape(B, L, BRANCH_WIDTH).astype(dtype) for o in (o_nsa, o_ml, o_fox)], axis=2)
    proj = jnp.einsum('blmc,mcd->blmd', br, w_branch)
    y = jnp.einsum('blmd,blmd->bld', gates, proj).astype(dtype)
    return y @ w_out


def _moe(x, w_group, b_group, w_expert, b_expert, w_gate, w_up, w_down):
    B, L, D = x.shape
    t = x.reshape(B * L, D)
    pg = jax.nn.softmax((t @ w_group + b_group).astype(jnp.float32), axis=-1)
    g_val, g_idx = lax.top_k(pg, 1)
    le = (t @ w_expert + b_expert).astype(jnp.float32).reshape(-1, N_GROUPS, EXPERTS_PER_GROUP)
    le = jnp.take_along_axis(le, g_idx[:, :, None], axis=1)[:, 0]
    e_val, e_idx = lax.top_k(jax.nn.softmax(le, axis=-1), TOP_K_IN_GROUP)
    e_w = g_val * e_val / jnp.sum(e_val, axis=-1, keepdims=True)
    gate = jnp.sum(jax.nn.one_hot(g_idx * EXPERTS_PER_GROUP + e_idx, N_EXPERTS, dtype=jnp.float32) * e_w[..., None], axis=1)
    h = jax.nn.silu(jnp.einsum('td,edf->tef', t, w_gate)) * jnp.einsum('td,edf->tef', t, w_up)
    h = h * gate[:, :, None].astype(h.dtype)
    return jnp.einsum('tef,efd->td', h, w_down).reshape(B, L, D)


def _mixers_prompt(x, w_in, b_in, w_ck, w_cv, ml_norm_w, w_branch, w_out, slopes):
    B, S, _ = x.shape
    pr = _project(x, w_in, b_in)
    pos = jnp.arange(S)
    o_cmp, o_sel = _nsa_cmp_sel(pr['nsa_q'], pr['nsa_rows'], pos, w_ck, w_cv, slopes)
    o_win = _nsa_win_prompt(pr['nsa_q'], pr['nsa_win'][:, :, 0], pr['nsa_win'][:, :, 1], slopes)
    o_nsa = _nsa_combine(pr['nsa_g'], o_cmp, o_sel, o_win)
    h, (c, n, m) = _mlstm_prompt(pr['ml_q'], pr['ml_k'], pr['ml_v'], pr['ml_i'], pr['ml_lf'])
    o_ml = _mlstm_readout(h, pr['ml_o'], ml_norm_w)
    o_fox = _fox_prompt(pr['fox_q'], pr['fox_k'], pr['fox_v'], jnp.cumsum(pr['fox_lf'], axis=1))
    out = _merge(pr['merge'], o_nsa, o_ml, o_fox, w_branch, w_out, x.dtype)
    w_keep = min(NSA_WINDOW, S)
    new = (pr['nsa_rows'], pr['nsa_win'][:, S - w_keep:], jnp.stack([pr['fox_k'], pr['fox_v']], axis=2),
           pr['fox_lf'], c, n, m)
    return out, new


def _mixers_sample(x, c_nsa, c_win, c_fox_kv, c_fox_lf, s_c, s_n, s_m, page_table,
                   w_in, b_in, w_ck, w_cv, ml_norm_w, w_branch, w_out, slopes):
    DB, T, _ = x.shape
    past = page_table.shape[1] * PAGE_SIZE
    f32 = jnp.float32
    pr = _project(x, w_in, b_in)
    pos_q = past + jnp.arange(T)
    rows_past = c_nsa[page_table].reshape(DB, past, NSA_ROWS, HEAD_DIM)
    rows_all = jnp.concatenate([rows_past, pr['nsa_rows'].astype(rows_past.dtype)], axis=1)
    o_cmp, o_sel = _nsa_cmp_sel(pr['nsa_q'], rows_all, pos_q, w_ck, w_cv, slopes)
    wb = c_win.shape[1]
    win_all = jnp.concatenate([c_win, pr['nsa_win'].astype(c_win.dtype)], axis=1)
    pos_k = past - wb + jnp.arange(wb + T)
    o_win = _nsa_win_attend(pr['nsa_q'], win_all[:, :, 0], win_all[:, :, 1], pos_q, pos_k, slopes)
    o_nsa = _nsa_combine(pr['nsa_g'], o_cmp, o_sel, o_win)
    (c, n, m), h = _mlstm_chunk((s_c.astype(f32), s_n.astype(f32), s_m.astype(f32)),
                                (pr['ml_q'], pr['ml_k'], pr['ml_v'], pr['ml_i'], pr['ml_lf']))
    o_ml = _mlstm_readout(h, pr['ml_o'], ml_norm_w)
    kv_past = c_fox_kv[page_table].reshape(DB, past, 2, FOX_HEADS, HEAD_DIM)
    k_all = jnp.concatenate([kv_past[:, :, 0], pr['fox_k'].astype(kv_past.dtype)], axis=1)
    v_all = jnp.concatenate([kv_past[:, :, 1], pr['fox_v'].astype(kv_past.dtype)], axis=1)
    lf_all = jnp.concatenate([c_fox_lf[page_table].reshape(DB, past, FOX_HEADS).astype(f32), pr['fox_lf']], axis=1)
    F = jnp.cumsum(lf_all, axis=1)
    o_fox = _fox_attend(pr['fox_q'], k_all, v_all, F[:, past:], F, pos_q, jnp.arange(past + T))
    out = _merge(pr['merge'], o_nsa, o_ml, o_fox, w_branch, w_out, x.dtype)
    new = (pr['nsa_rows'], win_all[:, T:], jnp.stack([pr['fox_k'], pr['fox_v']], axis=2),
           pr['fox_lf'], c, n, m)
    return out, new


def _stack_layers(states):
    return tuple(jnp.stack(list(a)) for a in zip(*states))


def kernel(x_prompt, x_sample, cache_nsa, cache_nsa_win, cache_fox_kv, cache_fox_logf,
           state_mlstm_c, state_mlstm_n, state_mlstm_m, page_table,
           ln_in_w, ln_in_b, w_in, b_in, nsa_w_ck, nsa_w_cv, mlstm_norm_w, w_branch, w_out,
           ln1_w, ln1_b, moe_w_group, moe_b_group, moe_w_expert, moe_b_expert,
           moe_w_gate, moe_w_up, moe_w_down, ln2_w, ln2_b):
    alpha = (2.0 * DEPTH) ** 0.25
    slopes = _alibi_slopes(NSA_HEADS)
    xp = _layer_norm(x_prompt, ln_in_w, ln_in_b)
    xs = _layer_norm(x_sample, ln_in_w, ln_in_b)
    new_p, new_s = [], []
    for l in range(DEPTH):
        mix_w = (w_in[l], b_in[l], nsa_w_ck[l], nsa_w_cv[l], mlstm_norm_w[l], w_branch[l], w_out[l])
        moe_w = (moe_w_group[l], moe_b_group[l], moe_w_expert[l], moe_b_expert[l],
                 moe_w_gate[l], moe_w_up[l], moe_w_down[l])
        mix_p, st_p = _mixers_prompt(xp, *mix_w, slopes)
        xp = _layer_norm(alpha * xp + mix_p, ln1_w[l], ln1_b[l])
        xp = _layer_norm(alpha * xp + _moe(xp, *moe_w), ln2_w[l], ln2_b[l])
        mix_s, st_s = _mixers_sample(xs, cache_nsa[l], cache_nsa_win[l], cache_fox_kv[l], cache_fox_logf[l],
                                     state_mlstm_c[l], state_mlstm_n[l], state_mlstm_m[l], page_table,
                                     *mix_w, slopes)
        xs = _layer_norm(alpha * xs + mix_s, ln1_w[l], ln1_b[l])
        xs = _layer_norm(alpha * xs + _moe(xs, *moe_w), ln2_w[l], ln2_b[l])
        new_p.append(st_p)
        new_s.append(st_s)
    (p_nsa_rows, p_nsa_win, p_fox_kv, p_fox_logf, p_mlstm_c, p_mlstm_n, p_mlstm_m) = _stack_layers(new_p)
    (s_nsa_rows, s_nsa_win, s_fox_kv, s_fox_logf, s_mlstm_c, s_mlstm_n, s_mlstm_m) = _stack_layers(new_s)
    return (xp, xs, p_nsa_rows, p_nsa_win, p_fox_kv, p_fox_logf, p_mlstm_c, p_mlstm_n, p_mlstm_m,
            s_nsa_rows, s_nsa_win, s_fox_kv, s_fox_logf, s_mlstm_c, s_mlstm_n, s_mlstm_m)
```

```python
import functools

import jax
import jax.numpy as jnp
import numpy as np
from jax import lax
from jax.experimental import pallas as pl
from jax.experimental.pallas import tpu as pltpu

D_MODEL = 1024
BATCH = 2
SEQ = 8192
DEPTH = 2
DEC_BATCH = 128
DEC_SEQ = 8
PAST_LEN = 2048
PAGE_SIZE = 128

HEAD_DIM = 64
NSA_HEADS = 4
NSA_BLOCK = 64
NSA_TOPK = 16
NSA_WINDOW = 512
NSA_ROWS = 4
MLSTM_HEADS = 4
MLSTM_CHUNK = 64
FOX_HEADS = 4
Q_BLOCK = 128
N_BRANCH = 3
BRANCH_WIDTH = NSA_HEADS * HEAD_DIM
N_GROUPS = 4
EXPERTS_PER_GROUP = 4
N_EXPERTS = N_GROUPS * EXPERTS_PER_GROUP
TOP_K_IN_GROUP = 2
D_EXPERT = 256
LN_EPS = 1e-5
NEG_INF = -1e30
TINY = 1e-30
FORCED_SCORE = 1e9
ATTN_SCALE = HEAD_DIM ** -0.5

IN_SPLITS = (
    NSA_HEADS * HEAD_DIM,
    6 * HEAD_DIM,
    NSA_HEADS * 3,
    MLSTM_HEADS * HEAD_DIM,
    MLSTM_HEADS * HEAD_DIM,
    MLSTM_HEADS * HEAD_DIM,
    MLSTM_HEADS * HEAD_DIM,
    MLSTM_HEADS,
    MLSTM_HEADS,
    FOX_HEADS * HEAD_DIM,
    FOX_HEADS * HEAD_DIM,
    FOX_HEADS * HEAD_DIM,
    FOX_HEADS,
    N_BRANCH * D_MODEL,
)


def _linear_kernel(x_ref, w_ref, b_ref, o_ref):
    x = x_ref[...].astype(jnp.bfloat16)
    o_ref[...] = jnp.dot(x, w_ref[...], preferred_element_type=jnp.float32) + b_ref[...]


def _pallas_linear(x, w, b, tm=512, tn=512):
    T, K = x.shape
    N = w.shape[1]
    n_pad = -(-N // tn) * tn
    wp = jnp.pad(w.astype(jnp.bfloat16), ((0, 0), (0, n_pad - N)))
    bp = jnp.pad(b.astype(jnp.float32), (0, n_pad - N)).reshape(1, n_pad)
    out = pl.pallas_call(
        _linear_kernel,
        grid=(T // tm, n_pad // tn),
        in_specs=[pl.BlockSpec((tm, K), lambda i, j: (i, 0)),
                  pl.BlockSpec((K, tn), lambda i, j: (0, j)),
                  pl.BlockSpec((1, tn), lambda i, j: (0, j))],
        out_specs=pl.BlockSpec((tm, tn), lambda i, j: (i, j)),
        out_shape=jax.ShapeDtypeStruct((T, n_pad), jnp.float32),
        name="linear",
    )(x, wp, bp)
    return out[:, :N]


VMEM_LIMIT_BYTES = 48 * 1024 * 1024
NSA_SLOPES = tuple(2.0 ** (-8.0 * (h + 1) / NSA_HEADS) for h in range(NSA_HEADS))
_NT = (((1,), (1,)), ((), ()))
_HI = lax.Precision.HIGHEST


def _sigmoid(x):
    return 1.0 / (1.0 + jnp.exp(-x))


def _tile_rows(x, n):
    return jnp.concatenate([x] * n, axis=0)


def _compress_kernel(x_ref, w_ref, o_ref, acc_ref):
    k = pl.program_id(1)

    @pl.when(k == 0)
    def _():
        acc_ref[...] = jnp.zeros_like(acc_ref)

    acc_ref[...] += jnp.dot(x_ref[...], w_ref[...], preferred_element_type=jnp.float32, precision=_HI)

    @pl.when(k == pl.num_programs(1) - 1)
    def _():
        o_ref[...] = acc_ref[...]


def _compress_weights(w_ck, w_cv):
    z = jnp.zeros_like(w_ck)
    wk = jnp.stack([w_ck, z, z, z], axis=1)
    wv = jnp.stack([z, w_cv, z, z], axis=1)
    return jnp.concatenate([wk, wv], axis=-1).reshape(NSA_BLOCK * NSA_ROWS * HEAD_DIM, 2 * HEAD_DIM)


def _nsa_compress_blocks(blocks, w_big, tm=256, tk=2048):
    n, kdim = blocks.shape
    tm = min(tm, n)
    return pl.pallas_call(
        _compress_kernel,
        grid=(n // tm, kdim // tk),
        in_specs=[pl.BlockSpec((tm, tk), lambda i, k: (i, k)),
                  pl.BlockSpec((tk, 2 * HEAD_DIM), lambda i, k: (k, 0))],
        out_specs=pl.BlockSpec((tm, 2 * HEAD_DIM), lambda i, k: (i, 0)),
        out_shape=jax.ShapeDtypeStruct((n, 2 * HEAD_DIM), jnp.float32),
        scratch_shapes=[pltpu.VMEM((tm, 2 * HEAD_DIM), jnp.float32)],
        name="nsa_compress",
    )(blocks, w_big)


def _softmax_step(carry, s, v):
    m, l, acc = carry
    m_new = jnp.maximum(m, jnp.max(s, axis=-1, keepdims=True))
    p = jnp.exp(s - m_new)
    a = jnp.exp(m - m_new)
    l = a * l + jnp.sum(p, axis=-1, keepdims=True)
    acc = a * acc + jnp.dot(p.astype(jnp.bfloat16), v, preferred_element_type=jnp.float32)
    return m_new, l, acc


def _select_blocks(imp, cur, n_pick):
    q, n_sb = imp.shape
    jq = lax.broadcasted_iota(jnp.int32, (q, n_sb), 1)
    jf = jq.astype(jnp.float32)
    score = jnp.where(jq > cur, NEG_INF, jnp.where((jq == cur) | (jq == 0), FORCED_SCORE, imp))

    def pick(_, carry):
        work, sel = carry
        mx = jnp.max(work, axis=-1, keepdims=True)
        first = jnp.min(jnp.where(work == mx, jf, float(n_sb)), axis=-1, keepdims=True)
        hit = jf == first
        sel = jnp.where(hit & (mx > 0.5 * NEG_INF), 1.0, sel)
        work = jnp.where(hit, -jnp.inf, work)
        return work, sel

    _, sel = lax.fori_loop(0, n_pick, pick, (score, jnp.zeros((q, n_sb), jnp.float32)))
    return sel


def _nsa_prompt_kernel(qs_ref, qf_ref, kc_ref, vc_ref, ksel_ref, vsel_ref, kwin_ref, vwin_ref, g_ref,
                       o_ref, *, tq, tk, tw):
    i = pl.program_id(1)
    f32, bf16 = jnp.float32, jnp.bfloat16
    H = NSA_HEADS
    R = H * tq
    qs = qs_ref[0].reshape(R, HEAD_DIM)
    qf = qf_ref[0].reshape(R, HEAD_DIM)
    row = lax.broadcasted_iota(jnp.int32, (R, 1), 0)
    head = row // tq
    slope = jnp.where(head == 0, NSA_SLOPES[0], jnp.where(head == 1, NSA_SLOPES[1],
                      jnp.where(head == 2, NSA_SLOPES[2], NSA_SLOPES[3]))).astype(f32)
    posq = i * tq + (row - head * tq)

    n_cb = kc_ref.shape[1]
    sc = lax.dot_general(qf, kc_ref[0], _NT, precision=_HI, preferred_element_type=f32) * ATTN_SCALE
    jb = lax.broadcasted_iota(jnp.int32, (R, n_cb), 1)
    distc = posq - (jb * NSA_BLOCK + NSA_BLOCK - 1)
    okc = distc >= 0
    sc = jnp.where(okc, sc - slope * distc.astype(f32), NEG_INF)
    pc = jnp.exp(sc - jnp.max(sc, axis=-1, keepdims=True)) * okc.astype(f32)
    pc = pc / jnp.maximum(jnp.sum(pc, axis=-1, keepdims=True), TINY)
    o_cmp = jnp.dot(pc.astype(bf16), vc_ref[0].astype(bf16), preferred_element_type=f32)
    imp = pc[0:tq] + pc[tq:2 * tq] + pc[2 * tq:3 * tq] + pc[3 * tq:4 * tq]

    pq = i * tq + lax.broadcasted_iota(jnp.int32, (tq, 1), 0)
    msel = _select_blocks(imp, pq // NSA_BLOCK, NSA_TOPK).astype(bf16)

    rowpos = i * tq + lax.broadcasted_iota(jnp.int32, (tq, 1), 0)
    init = (jnp.full((R, 1), NEG_INF, f32), jnp.zeros((R, 1), f32), jnp.zeros((R, HEAD_DIM), f32))

    def sel_body(j, carry):
        k0 = pl.multiple_of(j * tk, tk)
        k = ksel_ref[0, pl.ds(k0, tk), :]
        v = vsel_ref[0, pl.ds(k0, tk), :]
        s = lax.dot_general(qs, k, _NT, preferred_element_type=f32)
        eb = (lax.broadcasted_iota(jnp.int32, (n_cb, tk), 0)
              == j * (tk // NSA_BLOCK) + lax.broadcasted_iota(jnp.int32, (n_cb, tk), 1) // NSA_BLOCK)
        mexp = jnp.dot(msel, eb.astype(bf16), preferred_element_type=f32)
        d = rowpos - (k0 + lax.broadcasted_iota(jnp.int32, (tq, tk), 1))
        ok = (mexp > 0.5) & (d >= 0)
        s = jnp.where(_tile_rows(ok, H), s - slope * _tile_rows(d.astype(f32), H), NEG_INF)
        return _softmax_step(carry, s, v)

    n_sel = (i * tq + tq - 1) // tk + 1
    _, l_sel, a_sel = lax.fori_loop(0, n_sel, sel_body, init)
    o_sel = a_sel / l_sel

    def win_body(j, carry):
        k0 = pl.multiple_of(j * tw, tw)
        k = kwin_ref[0, pl.ds(k0, tw), :]
        v = vwin_ref[0, pl.ds(k0, tw), :]
        s = lax.dot_general(qs, k, _NT, preferred_element_type=f32)
        d = rowpos - (k0 + lax.broadcasted_iota(jnp.int32, (tq, tw), 1))
        ok = (d >= 0) & (d < NSA_WINDOW)
        s = jnp.where(_tile_rows(ok, H), s - slope * _tile_rows(d.astype(f32), H), NEG_INF)
        return _softmax_step(carry, s, v)

    lo = jnp.maximum((i * tq - NSA_WINDOW + 1) // tw, 0)
    hi = (i * tq + tq - 1) // tw + 1
    _, l_win, a_win = lax.fori_loop(lo, hi, win_body, init)
    o_win = a_win / l_win

    g = _sigmoid(g_ref[...])
    gate = lambda c: jnp.concatenate([g[:, 3 * h + c:3 * h + c + 1] for h in range(H)], axis=0)
    o = gate(0) * o_cmp + gate(1) * o_sel + gate(2) * o_win
    o_ref[0] = o.reshape(H, tq, HEAD_DIM)


def _heads_major(x, dtype):
    B, S, _ = x.shape
    return x.reshape(B, S, -1, HEAD_DIM).transpose(0, 2, 1, 3).astype(dtype)


def _nsa_prompt(nq, rows, win, gates, w_ck, w_cv, tq=128, tk=256, tw=128):
    B, S, _ = nq.shape
    bf16 = jnp.bfloat16
    kcvc = _nsa_compress_blocks(rows.reshape(B * S // NSA_BLOCK, -1), _compress_weights(w_ck, w_cv))
    kcvc = kcvc.reshape(B, S // NSA_BLOCK, 2 * HEAD_DIM)
    kc, vc = kcvc[..., :HEAD_DIM], kcvc[..., HEAD_DIM:]
    qs = _heads_major(nq * ATTN_SCALE, bf16)
    qf = _heads_major(nq, jnp.float32)
    col = lambda a, c: a[..., c * HEAD_DIM:(c + 1) * HEAD_DIM].astype(bf16)
    full = lambda n: pl.BlockSpec((1, n, HEAD_DIM), lambda b, i: (b, 0, 0))
    qspec = pl.BlockSpec((1, NSA_HEADS, tq, HEAD_DIM), lambda b, i: (b, 0, i, 0))
    o = pl.pallas_call(
        functools.partial(_nsa_prompt_kernel, tq=tq, tk=tk, tw=tw),
        grid=(B, S // tq),
        in_specs=[qspec, qspec, full(S // NSA_BLOCK), full(S // NSA_BLOCK), full(S), full(S), full(S), full(S),
                  pl.BlockSpec((None, tq, 128), lambda b, i: (b, i, 0))],
        out_specs=qspec,
        out_shape=jax.ShapeDtypeStruct((B, NSA_HEADS, S, HEAD_DIM), jnp.float32),
        compiler_params=pltpu.CompilerParams(dimension_semantics=("parallel", "arbitrary"),
                                             vmem_limit_bytes=VMEM_LIMIT_BYTES),
        name="nsa_prompt",
    )(qs, qf, kc, vc, col(rows, 2), col(rows, 3), col(win, 0), col(win, 1), gates)
    return o.transpose(0, 2, 1, 3).reshape(B, S, BRANCH_WIDTH)


def _fox_prompt_kernel(q_ref, k_ref, v_ref, fq_ref, fk_ref, o_ref, *, t):
    i = pl.program_id(1)
    f32, bf16 = jnp.float32, jnp.bfloat16
    lane = lax.broadcasted_iota(jnp.int32, (1, 2 * HEAD_DIM), 1)
    low = lane < HEAD_DIM
    rc = lax.broadcasted_iota(jnp.int32, (t, t), 0) - lax.broadcasted_iota(jnp.int32, (t, t), 1)
    for hp in range(FOX_HEADS // 2):
        cols = slice(hp * 2 * HEAD_DIM, (hp + 1) * 2 * HEAD_DIM)
        q2 = q_ref[0, :, cols]
        qa = jnp.where(low, q2, jnp.zeros_like(q2))
        qb = jnp.where(low, jnp.zeros_like(q2), q2)
        fqa = fq_ref[0, :, 2 * hp:2 * hp + 1]
        fqb = fq_ref[0, :, 2 * hp + 1:2 * hp + 2]

        def body(j, carry, diagonal):
            ma, la, mb, lb, acc = carry
            k0 = pl.multiple_of(j * t, t)
            k2 = k_ref[0, pl.ds(k0, t), cols]
            v2 = v_ref[0, pl.ds(k0, t), cols]
            fk = fk_ref[0, j]
            sa = lax.dot_general(qa, k2, _NT, preferred_element_type=f32) + (fqa - fk[2 * hp:2 * hp + 1])
            sb = lax.dot_general(qb, k2, _NT, preferred_element_type=f32) + (fqb - fk[2 * hp + 1:2 * hp + 2])
            if diagonal:
                sa = jnp.where(rc >= 0, sa, NEG_INF)
                sb = jnp.where(rc >= 0, sb, NEG_INF)
            ma_n = jnp.maximum(ma, jnp.max(sa, axis=-1, keepdims=True))
            mb_n = jnp.maximum(mb, jnp.max(sb, axis=-1, keepdims=True))
            pa = jnp.exp(sa - ma_n)
            pb = jnp.exp(sb - mb_n)
            aa = jnp.exp(ma - ma_n)
            ab = jnp.exp(mb - mb_n)
            la = aa * la + jnp.sum(pa, axis=-1, keepdims=True)
            lb = ab * lb + jnp.sum(pb, axis=-1, keepdims=True)
            pva = jnp.dot(pa.astype(bf16), v2, preferred_element_type=f32)
            pvb = jnp.dot(pb.astype(bf16), v2, preferred_element_type=f32)
            acc = jnp.where(low, aa * acc + pva, ab * acc + pvb)
            return ma_n, la, mb_n, lb, acc

        neg = jnp.full((t, 1), NEG_INF, f32)
        zero = jnp.zeros((t, 1), f32)
        carry = (neg, zero, neg, zero, jnp.zeros((t, 2 * HEAD_DIM), f32))
        carry = lax.fori_loop(0, i, functools.partial(body, diagonal=False), carry)
        _, la, _, lb, acc = body(i, carry, True)
        o_ref[0, :, cols] = acc / jnp.where(low, la, lb)


def _fox_prompt_attn(q, k, v, F, t=256):
    B, S, W = q.shape
    bf16 = jnp.bfloat16
    fk = jnp.pad(F.transpose(0, 2, 1), ((0, 0), (0, 8 - FOX_HEADS), (0, 0)))
    fk = fk.reshape(B, 8, S // t, t).transpose(0, 2, 1, 3)
    full = pl.BlockSpec((1, S, W), lambda b, i: (b, 0, 0))
    return pl.pallas_call(
        functools.partial(_fox_prompt_kernel, t=t),
        grid=(B, S // t),
        in_specs=[pl.BlockSpec((1, t, W), lambda b, i: (b, i, 0)), full, full,
                  pl.BlockSpec((1, t, FOX_HEADS), lambda b, i: (b, i, 0)),
                  pl.BlockSpec((1, S // t, 8, t), lambda b, i: (b, 0, 0, 0))],
        out_specs=pl.BlockSpec((1, t, W), lambda b, i: (b, i, 0)),
        out_shape=jax.ShapeDtypeStruct((B, S, W), jnp.float32),
        compiler_params=pltpu.CompilerParams(dimension_semantics=("parallel", "arbitrary"),
                                             vmem_limit_bytes=VMEM_LIMIT_BYTES),
        name="fox_prompt",
    )((q * ATTN_SCALE).astype(bf16), k.astype(bf16), v.astype(bf16), F, fk)


def _copy_kernel(pt_ref, x_ref, o_ref):
    o_ref[...] = x_ref[...]


def _gather_pages(cache, page_table):
    db, n_pages = page_table.shape
    _, page, w = cache.shape
    out = pl.pallas_call(
        _copy_kernel,
        grid_spec=pltpu.PrefetchScalarGridSpec(
            num_scalar_prefetch=1, grid=(db * n_pages,),
            in_specs=[pl.BlockSpec((1, page, w), lambda i, pt: (pt[i], 0, 0))],
            out_specs=pl.BlockSpec((1, page, w), lambda i, pt: (i, 0, 0))),
        out_shape=jax.ShapeDtypeStruct((db * n_pages, page, w), cache.dtype),
        name="gather_pages",
    )(page_table.reshape(-1), cache)
    return out.reshape(db, n_pages * page, w)


def _split_points():
    return [int(v) for v in np.cumsum(IN_SPLITS)[:-1]]


def _layer_norm(x, w, b):
    xf = x.astype(jnp.float32)
    mu = jnp.mean(xf, axis=-1, keepdims=True)
    var = jnp.mean(jnp.square(xf - mu), axis=-1, keepdims=True)
    y = (xf - mu) * lax.rsqrt(var + LN_EPS) * w.astype(jnp.float32) + b.astype(jnp.float32)
    return y.astype(x.dtype)


def _masked_softmax(s, mask):
    s = jnp.where(mask, s.astype(jnp.float32), NEG_INF)
    p = jnp.exp(s - jnp.max(s, axis=-1, keepdims=True)) * mask
    return p / jnp.maximum(jnp.sum(p, axis=-1, keepdims=True), TINY)


def _alibi_slopes(n):
    return jnp.exp2(-8.0 * jnp.arange(1, n + 1, dtype=jnp.float32) / n)


def _project(x, w_in, b_in):
    B, L, _ = x.shape
    f32 = jnp.float32
    z = _pallas_linear(x.reshape(B * L, -1), w_in, b_in).reshape(B, L, -1)
    (nq, nkv, ng, mq, mk, mv, mo, mi, mf, fq, fk, fv, ff, gm) = jnp.split(z, _split_points(), axis=-1)
    heads = lambda t, h: t.reshape(B, L, h, HEAD_DIM)
    rows = nkv.reshape(B, L, 6, HEAD_DIM)
    return dict(
        nsa_q=heads(nq, NSA_HEADS),
        nsa_rows=rows[:, :, :NSA_ROWS],
        nsa_win=rows[:, :, NSA_ROWS:],
        nsa_g=jax.nn.sigmoid(ng.astype(f32)).reshape(B, L, NSA_HEADS, 3),
        nsa_g_raw=ng,
        ml_q=heads(mq, MLSTM_HEADS).astype(f32),
        ml_k=heads(mk, MLSTM_HEADS).astype(f32) * ATTN_SCALE,
        ml_v=heads(mv, MLSTM_HEADS).astype(f32),
        ml_o=jax.nn.sigmoid(mo.astype(f32)).reshape(B, L, MLSTM_HEADS, HEAD_DIM),
        ml_i=mi.astype(f32),
        ml_lf=jax.nn.log_sigmoid(mf.astype(f32)),
        fox_q=heads(fq, FOX_HEADS),
        fox_k=heads(fk, FOX_HEADS),
        fox_v=heads(fv, FOX_HEADS),
        fox_lf=jax.nn.log_sigmoid(ff.astype(f32)),
        merge=jax.nn.sigmoid(gm.astype(f32)).reshape(B, L, N_BRANCH, D_MODEL))


def _nsa_compress(k, w):
    B, L, _ = k.shape
    n_cb = L // NSA_BLOCK
    blocks = k[:, :n_cb * NSA_BLOCK].reshape(B, n_cb, NSA_BLOCK, HEAD_DIM)
    return jnp.einsum('bjpd,pde->bje', blocks, w)


def _nsa_cmp_attend(q, kc, vc, pos_q, slopes):
    n_cb = kc.shape[1]
    blk_end = jnp.arange(n_cb) * NSA_BLOCK + NSA_BLOCK - 1
    dist = pos_q[:, None] - blk_end[None, :]
    s = jnp.einsum('bqhd,bjd->bhqj', q, kc) * ATTN_SCALE - slopes[:, None, None] * dist.astype(jnp.float32)
    p = _masked_softmax(s, dist >= 0)
    o = jnp.einsum('bhqj,bjd->bqhd', p.astype(vc.dtype), vc)
    return o, jnp.sum(p, axis=1)


def _nsa_select_blocks(imp, pos_q, n_sb):
    n_cb = imp.shape[-1]
    imp = jnp.pad(imp, ((0, 0), (0, 0), (0, n_sb - n_cb)))
    j = jnp.arange(n_sb)[None, :]
    cur = (pos_q // NSA_BLOCK)[:, None]
    score = jnp.where(j > cur, NEG_INF, jnp.where((j == cur) | (j == 0), FORCED_SCORE, imp))
    vals, idx = lax.top_k(score, min(NSA_TOPK, n_sb))
    return idx, vals > 0.5 * NEG_INF


def _nsa_sel_attend(q, kb, vb, idx, valid, pos_q, slopes):
    B, Q = q.shape[:2]
    bi = jnp.arange(B)[:, None, None]
    kg = kb[bi, idx].reshape(B, Q, -1, HEAD_DIM)
    vg = vb[bi, idx].reshape(B, Q, -1, HEAD_DIM)
    pos_k = (idx[..., None] * NSA_BLOCK + jnp.arange(NSA_BLOCK)).reshape(B, Q, -1)
    ok = jnp.repeat(valid, NSA_BLOCK, axis=-1) & (pos_k <= pos_q[None, :, None])
    dist = (pos_q[None, :, None] - pos_k).astype(jnp.float32)
    s = jnp.einsum('bqhd,bqkd->bhqk', q, kg) * ATTN_SCALE - slopes[None, :, None, None] * dist[:, None]
    p = _masked_softmax(s, ok[:, None])
    return jnp.einsum('bhqk,bqkd->bqhd', p.astype(vg.dtype), vg)


def _nsa_cmp_sel(q, rows, pos_q, w_ck, w_cv, slopes):
    B, Q = q.shape[:2]
    L = rows.shape[1]
    kc = _nsa_compress(rows[:, :, 0], w_ck)
    vc = _nsa_compress(rows[:, :, 1], w_cv)
    o_cmp, imp = _nsa_cmp_attend(q, kc, vc, pos_q, slopes)
    n_sb = -(-L // NSA_BLOCK)
    idx, valid = _nsa_select_blocks(imp, pos_q, n_sb)
    pad = ((0, 0), (0, n_sb * NSA_BLOCK - L), (0, 0))
    kb = jnp.pad(rows[:, :, 2], pad).reshape(B, n_sb, NSA_BLOCK, HEAD_DIM)
    vb = jnp.pad(rows[:, :, 3], pad).reshape(B, n_sb, NSA_BLOCK, HEAD_DIM)
    qb = Q_BLOCK if Q % Q_BLOCK == 0 else Q
    nqb = Q // qb

    def block(args):
        qc, ic, okc, pc = args
        return _nsa_sel_attend(qc, kb, vb, ic, okc, pc, slopes)

    xs = (q.reshape(B, nqb, qb, NSA_HEADS, HEAD_DIM).swapaxes(0, 1),
          idx.reshape(B, nqb, qb, -1).swapaxes(0, 1),
          valid.reshape(B, nqb, qb, -1).swapaxes(0, 1),
          pos_q.reshape(nqb, qb))
    o_sel = lax.map(block, xs).swapaxes(0, 1).reshape(B, Q, NSA_HEADS, HEAD_DIM)
    return o_cmp, o_sel


def _nsa_win_attend(q, k, v, pos_q, pos_k, slopes):
    dist = pos_q[:, None] - pos_k[None, :]
    ok = (dist >= 0) & (dist < NSA_WINDOW) & (pos_k[None, :] >= 0)
    s = jnp.einsum('bqhd,bkd->bhqk', q, k) * ATTN_SCALE - slopes[:, None, None] * dist.astype(jnp.float32)
    p = _masked_softmax(s, ok)
    return jnp.einsum('bhqk,bkd->bqhd', p.astype(v.dtype), v)


def _nsa_win_prompt(q, k, v, slopes):
    B, S = q.shape[:2]
    nqb = S // Q_BLOCK
    nb = NSA_WINDOW // Q_BLOCK
    pad = ((0, 0), (NSA_WINDOW, 0), (0, 0))
    kp = jnp.pad(k, pad).reshape(B, nqb + nb, Q_BLOCK, HEAD_DIM)
    vp = jnp.pad(v, pad).reshape(B, nqb + nb, Q_BLOCK, HEAD_DIM)
    band = jnp.arange(nqb)[:, None] + jnp.arange(nb + 1)[None, :]
    kband = kp[:, band].reshape(B, nqb, (nb + 1) * Q_BLOCK, HEAD_DIM)
    vband = vp[:, band].reshape(B, nqb, (nb + 1) * Q_BLOCK, HEAD_DIM)
    qblk = q.reshape(B, nqb, Q_BLOCK, NSA_HEADS, HEAD_DIM)
    pos_q = jnp.arange(S).reshape(nqb, Q_BLOCK)
    pos_k = (jnp.arange(nqb) * Q_BLOCK - NSA_WINDOW)[:, None] + jnp.arange((nb + 1) * Q_BLOCK)[None, :]
    o = jax.vmap(_nsa_win_attend, in_axes=(1, 1, 1, 0, 0, None), out_axes=1)(qblk, kband, vband, pos_q, pos_k, slopes)
    return o.reshape(B, S, NSA_HEADS, HEAD_DIM)


def _nsa_combine(g, o_cmp, o_sel, o_win):
    return jnp.einsum('blhc,cblhd->blhd', g, jnp.stack([o_cmp, o_sel, o_win]))


def _mlstm_chunk(carry, xs):
    c, n, m = carry
    q, k, v, ig, lf = xs
    L = q.shape[1]
    b = jnp.cumsum(lf, axis=1)
    causal = jnp.tril(jnp.ones((L, L), bool))[None, :, :, None]
    dmat = jnp.where(causal, b[:, :, None, :] - b[:, None, :, :] + ig[:, None, :, :], NEG_INF)
    a = b + m[:, None, :]
    m_t = jnp.maximum(a, jnp.max(dmat, axis=2))
    wq = jnp.exp(dmat - m_t[:, :, None, :]) * jnp.einsum('bthd,bshd->btsh', q, k)
    inter = jnp.exp(a - m_t)
    num = inter[..., None] * jnp.einsum('bthd,bhde->bthe', q, c) + jnp.einsum('btsh,bshe->bthe', wq, v)
    den = inter * jnp.einsum('bthd,bhd->bth', q, n) + jnp.sum(wq, axis=2)
    h = num / jnp.maximum(jnp.abs(den), jnp.exp(-m_t))[..., None]
    bl = b[:, -1]
    g = bl[:, None, :] - b + ig
    m_new = jnp.maximum(bl + m, jnp.max(g, axis=1))
    ws = jnp.exp(g - m_new[:, None, :])
    decay = jnp.exp(bl + m - m_new)
    c_new = decay[..., None, None] * c + jnp.einsum('bsh,bshd,bshe->bhde', ws, k, v)
    n_new = decay[..., None] * n + jnp.einsum('bsh,bshd->bhd', ws, k)
    return (c_new, n_new, m_new), h


def _mlstm_prompt(q, k, v, ig, lf):
    B, S, H, _ = q.shape
    ch = MLSTM_CHUNK if S % MLSTM_CHUNK == 0 else S
    nc = S // ch
    to_chunks = lambda t: t.reshape((B, nc, ch) + t.shape[2:]).swapaxes(0, 1)
    f32 = jnp.float32
    init = (jnp.zeros((B, H, HEAD_DIM, HEAD_DIM), f32), jnp.zeros((B, H, HEAD_DIM), f32), jnp.zeros((B, H), f32))
    state, h = lax.scan(_mlstm_chunk, init, tuple(to_chunks(t) for t in (q, k, v, ig, lf)))
    return h.swapaxes(0, 1).reshape(B, S, H, HEAD_DIM), state


def _mlstm_readout(h, o_gate, norm_w):
    h = o_gate * h
    mu = jnp.mean(h, axis=-1, keepdims=True)
    var = jnp.mean(jnp.square(h - mu), axis=-1, keepdims=True)
    return (h - mu) * lax.rsqrt(var + LN_EPS) * norm_w.astype(jnp.float32).reshape(MLSTM_HEADS, HEAD_DIM)


def _fox_attend(q, k, v, fq, fk, pos_q, pos_k):
    s = jnp.einsum('bqhd,bkhd->bhqk', q, k) * ATTN_SCALE
    s = s + (fq.transpose(0, 2, 1)[..., None] - fk.transpose(0, 2, 1)[:, :, None, :])
    p = _masked_softmax(s, pos_k[None, :] <= pos_q[:, None])
    return jnp.einsum('bhqk,bkhd->bqhd', p.astype(v.dtype), v)


def _fox_prompt(q, k, v, F):
    B, S = q.shape[:2]
    qb = Q_BLOCK if S % Q_BLOCK == 0 else S
    nqb = S // qb
    pos = jnp.arange(S)

    def block(args):
        qc, fc, pc = args
        return _fox_attend(qc, k, v, fc, F, pc, pos)

    xs = (q.reshape(B, nqb, qb, FOX_HEADS, HEAD_DIM).swapaxes(0, 1),
          F.reshape(B, nqb, qb, FOX_HEADS).swapaxes(0, 1),
          pos.reshape(nqb, qb))
    return lax.map(block, xs).swapaxes(0, 1).reshape(B, S, FOX_HEADS, HEAD_DIM)


def _merge(gates, o_nsa, o_ml, o_fox, w_branch, w_out, dtype):
    B, L = o_nsa.shape[:2]
    br = jnp.stack([o.reshape(B, L, BRANCH_WIDTH).astype(dtype) for o in (o_nsa, o_ml, o_fox)], axis=2)
    proj = jnp.einsum('blmc,mcd->blmd', br, w_branch)
    y = jnp.einsum('blmd,blmd->bld', gates, proj).astype(dtype)
    return y @ w_out


def _moe(x, w_group, b_group, w_expert, b_expert, w_gate, w_up, w_down):
    B, L, D = x.shape
    t = x.reshape(B * L, D)
    pg = jax.nn.softmax((t @ w_group + b_group).astype(jnp.float32), axis=-1)
    g_val, g_idx = lax.top_k(pg, 1)
    le = (t @ w_expert + b_expert).astype(jnp.float32).reshape(-1, N_GROUPS, EXPERTS_PER_GROUP)
    le = jnp.take_along_axis(le, g_idx[:, :, None], axis=1)[:, 0]
    e_val, e_idx = lax.top_k(jax.nn.softmax(le, axis=-1), TOP_K_IN_GROUP)
    e_w = g_val * e_val / jnp.sum(e_val, axis=-1, keepdims=True)
    gate = jnp.sum(jax.nn.one_hot(g_idx * EXPERTS_PER_GROUP + e_idx, N_EXPERTS, dtype=jnp.float32) * e_w[..., None], axis=1)
    h = jax.nn.silu(jnp.einsum('td,edf->tef', t, w_gate)) * jnp.einsum('td,edf->tef', t, w_up)
    h = h * gate[:, :, None].astype(h.dtype)
    return jnp.einsum('tef,efd->td', h, w_down).reshape(B, L, D)


def _mixers_prompt(x, w_in, b_in, w_ck, w_cv, ml_norm_w, w_branch, w_out, slopes):
    B, S, _ = x.shape
    pr = _project(x, w_in, b_in)
    pos = jnp.arange(S)
    o_nsa = _nsa_prompt(pr['nsa_q'].reshape(B, S, -1), pr['nsa_rows'].reshape(B, S, -1),
                        pr['nsa_win'].reshape(B, S, -1), jnp.pad(pr['nsa_g_raw'], ((0, 0), (0, 0), (0, 116))),
                        w_ck, w_cv).reshape(B, S, NSA_HEADS, HEAD_DIM)
    h, (c, n, m) = _mlstm_prompt(pr['ml_q'], pr['ml_k'], pr['ml_v'], pr['ml_i'], pr['ml_lf'])
    o_ml = _mlstm_readout(h, pr['ml_o'], ml_norm_w)
    flat = lambda t: t.reshape(B, S, -1)
    o_fox = _fox_prompt_attn(flat(pr['fox_q']), flat(pr['fox_k']), flat(pr['fox_v']),
                             jnp.cumsum(pr['fox_lf'], axis=1)).reshape(B, S, FOX_HEADS, HEAD_DIM)
    out = _merge(pr['merge'], o_nsa, o_ml, o_fox, w_branch, w_out, x.dtype)
    w_keep = min(NSA_WINDOW, S)
    new = (pr['nsa_rows'], pr['nsa_win'][:, S - w_keep:], jnp.stack([pr['fox_k'], pr['fox_v']], axis=2),
           pr['fox_lf'], c, n, m)
    return out, new


def _mixers_sample(x, c_nsa, c_win, c_fox_kv, c_fox_lf, s_c, s_n, s_m, page_table,
                   w_in, b_in, w_ck, w_cv, ml_norm_w, w_branch, w_out, slopes):
    DB, T, _ = x.shape
    past = page_table.shape[1] * PAGE_SIZE
    f32 = jnp.float32
    pr = _project(x, w_in, b_in)
    pos_q = past + jnp.arange(T)
    rows_past = _gather_pages(c_nsa.reshape(-1, PAGE_SIZE, NSA_ROWS * HEAD_DIM), page_table)
    rows_past = rows_past.reshape(DB, past, NSA_ROWS, HEAD_DIM)
    rows_all = jnp.concatenate([rows_past, pr['nsa_rows'].astype(rows_past.dtype)], axis=1)
    o_cmp, o_sel = _nsa_cmp_sel(pr['nsa_q'], rows_all, pos_q, w_ck, w_cv, slopes)
    wb = c_win.shape[1]
    win_all = jnp.concatenate([c_win, pr['nsa_win'].astype(c_win.dtype)], axis=1)
    pos_k = past - wb + jnp.arange(wb + T)
    o_win = _nsa_win_attend(pr['nsa_q'], win_all[:, :, 0], win_all[:, :, 1], pos_q, pos_k, slopes)
    o_nsa = _nsa_combine(pr['nsa_g'], o_cmp, o_sel, o_win)
    (c, n, m), h = _mlstm_chunk((s_c.astype(f32), s_n.astype(f32), s_m.astype(f32)),
                                (pr['ml_q'], pr['ml_k'], pr['ml_v'], pr['ml_i'], pr['ml_lf']))
    o_ml = _mlstm_readout(h, pr['ml_o'], ml_norm_w)
    kv_past = _gather_pages(c_fox_kv.reshape(-1, PAGE_SIZE, 2 * FOX_HEADS * HEAD_DIM), page_table)
    kv_past = kv_past.reshape(DB, past, 2, FOX_HEADS, HEAD_DIM)
    k_all = jnp.concatenate([kv_past[:, :, 0], pr['fox_k'].astype(kv_past.dtype)], axis=1)
    v_all = jnp.concatenate([kv_past[:, :, 1], pr['fox_v'].astype(kv_past.dtype)], axis=1)
    lf_all = jnp.concatenate([c_fox_lf[page_table].reshape(DB, past, FOX_HEADS).astype(f32), pr['fox_lf']], axis=1)
    F = jnp.cumsum(lf_all, axis=1)
    o_fox = _fox_attend(pr['fox_q'], k_all, v_all, F[:, past:], F, pos_q, jnp.arange(past + T))
    out = _merge(pr['merge'], o_nsa, o_ml, o_fox, w_branch, w_out, x.dtype)
    new = (pr['nsa_rows'], win_all[:, T:], jnp.stack([pr['fox_k'], pr['fox_v']], axis=2),
           pr['fox_lf'], c, n, m)
    return out, new


def _stack_layers(states):
    return tuple(jnp.stack(list(a)) for a in zip(*states))


def kernel(x_prompt, x_sample, cache_nsa, cache_nsa_win, cache_fox_kv, cache_fox_logf,
           state_mlstm_c, state_mlstm_n, state_mlstm_m, page_table,
           ln_in_w, ln_in_b, w_in, b_in, nsa_w_ck, nsa_w_cv, mlstm_norm_w, w_branch, w_out,
           ln1_w, ln1_b, moe_w_group, moe_b_group, moe_w_expert, moe_b_expert,
           moe_w_gate, moe_w_up, moe_w_down, ln2_w, ln2_b):
    alpha = (2.0 * DEPTH) ** 0.25
    slopes = _alibi_slopes(NSA_HEADS)
    xp = _layer_norm(x_prompt, ln_in_w, ln_in_b)
    xs = _layer_norm(x_sample, ln_in_w, ln_in_b)
    new_p, new_s = [], []
    for l in range(DEPTH):
        mix_w = (w_in[l], b_in[l], nsa_w_ck[l], nsa_w_cv[l], mlstm_norm_w[l], w_branch[l], w_out[l])
        moe_w = (moe_w_group[l], moe_b_group[l], moe_w_expert[l], moe_b_expert[l],
                 moe_w_gate[l], moe_w_up[l], moe_w_down[l])
        mix_p, st_p = _mixers_prompt(xp, *mix_w, slopes)
        xp = _layer_norm(alpha * xp + mix_p, ln1_w[l], ln1_b[l])
        xp = _layer_norm(alpha * xp + _moe(xp, *moe_w), ln2_w[l], ln2_b[l])
        mix_s, st_s = _mixers_sample(xs, cache_nsa[l], cache_nsa_win[l], cache_fox_kv[l], cache_fox_logf[l],
                                     state_mlstm_c[l], state_mlstm_n[l], state_mlstm_m[l], page_table,
                                     *mix_w, slopes)
        xs = _layer_norm(alpha * xs + mix_s, ln1_w[l], ln1_b[l])
        xs = _layer_norm(alpha * xs + _moe(xs, *moe_w), ln2_w[l], ln2_b[l])
        new_p.append(st_p)
        new_s.append(st_s)
    (p_nsa_rows, p_nsa_win, p_fox_kv, p_fox_logf, p_mlstm_c, p_mlstm_n, p_mlstm_m) = _stack_layers(new_p)
    (s_nsa_rows, s_nsa_win, s_fox_kv, s_fox_logf, s_mlstm_c, s_mlstm_n, s_mlstm_m) = _stack_layers(new_s)
    return (xp, xs, p_nsa_rows, p_nsa_win, p_fox_kv, p_fox_logf, p_mlstm_c, p_mlstm_n, p_mlstm_m,
            s_nsa_rows, s_nsa_win, s_fox_kv, s_fox_logf, s_mlstm_c, s_mlstm_n, s_mlstm_m)
```

```python
import functools

import jax
import jax.numpy as jnp
import numpy as np
from jax import lax
from jax.experimental import pallas as pl
from jax.experimental.pallas import tpu as pltpu

D_MODEL = 1024
BATCH = 2
SEQ = 8192
DEPTH = 2
DEC_BATCH = 128
DEC_SEQ = 8
PAST_LEN = 2048
PAGE_SIZE = 128

HEAD_DIM = 64
NSA_HEADS = 4
NSA_BLOCK = 64
NSA_TOPK = 16
NSA_WINDOW = 512
NSA_ROWS = 4
MLSTM_HEADS = 4
MLSTM_CHUNK = 64
FOX_HEADS = 4
Q_BLOCK = 128
N_BRANCH = 3
BRANCH_WIDTH = NSA_HEADS * HEAD_DIM
N_GROUPS = 4
EXPERTS_PER_GROUP = 4
N_EXPERTS = N_GROUPS * EXPERTS_PER_GROUP
TOP_K_IN_GROUP = 2
D_EXPERT = 256
LN_EPS = 1e-5
NEG_INF = -1e30
TINY = 1e-30
FORCED_SCORE = 1e9
ATTN_SCALE = HEAD_DIM ** -0.5

IN_SPLITS = (
    NSA_HEADS * HEAD_DIM,
    6 * HEAD_DIM,
    NSA_HEADS * 3,
    MLSTM_HEADS * HEAD_DIM,
    MLSTM_HEADS * HEAD_DIM,
    MLSTM_HEADS * HEAD_DIM,
    MLSTM_HEADS * HEAD_DIM,
    MLSTM_HEADS,
    MLSTM_HEADS,
    FOX_HEADS * HEAD_DIM,
    FOX_HEADS * HEAD_DIM,
    FOX_HEADS * HEAD_DIM,
    FOX_HEADS,
    N_BRANCH * D_MODEL,
)


def _linear_kernel(x_ref, w_ref, b_ref, o_ref):
    x = x_ref[...].astype(jnp.bfloat16)
    o_ref[...] = jnp.dot(x, w_ref[...], preferred_element_type=jnp.float32) + b_ref[...]


def _pallas_linear(x, w, b, tm=512, tn=512):
    T, K = x.shape
    N = w.shape[1]
    n_pad = -(-N // tn) * tn
    wp = jnp.pad(w.astype(jnp.bfloat16), ((0, 0), (0, n_pad - N)))
    bp = jnp.pad(b.astype(jnp.float32), (0, n_pad - N)).reshape(1, n_pad)
    out = pl.pallas_call(
        _linear_kernel,
        grid=(T // tm, n_pad // tn),
        in_specs=[pl.BlockSpec((tm, K), lambda i, j: (i, 0)),
                  pl.BlockSpec((K, tn), lambda i, j: (0, j)),
                  pl.BlockSpec((1, tn), lambda i, j: (0, j))],
        out_specs=pl.BlockSpec((tm, tn), lambda i, j: (i, j)),
        out_shape=jax.ShapeDtypeStruct((T, n_pad), jnp.float32),
        name="linear",
    )(x, wp, bp)
    return out[:, :N]


VMEM_LIMIT_BYTES = 48 * 1024 * 1024
NSA_SLOPES = tuple(2.0 ** (-8.0 * (h + 1) / NSA_HEADS) for h in range(NSA_HEADS))
_NT = (((1,), (1,)), ((), ()))
_HI = lax.Precision.HIGHEST


def _sigmoid(x):
    return 1.0 / (1.0 + jnp.exp(-x))


def _tile_rows(x, n):
    return jnp.concatenate([x] * n, axis=0)


def _compress_kernel(x_ref, w_ref, o_ref, acc_ref):
    k = pl.program_id(1)

    @pl.when(k == 0)
    def _():
        acc_ref[...] = jnp.zeros_like(acc_ref)

    acc_ref[...] += jnp.dot(x_ref[...], w_ref[...], preferred_element_type=jnp.float32, precision=_HI)

    @pl.when(k == pl.num_programs(1) - 1)
    def _():
        o_ref[...] = acc_ref[...]


def _compress_weights(w_ck, w_cv):
    z = jnp.zeros_like(w_ck)
    wk = jnp.stack([w_ck, z, z, z], axis=1)
    wv = jnp.stack([z, w_cv, z, z], axis=1)
    return jnp.concatenate([wk, wv], axis=-1).reshape(NSA_BLOCK * NSA_ROWS * HEAD_DIM, 2 * HEAD_DIM)


def _nsa_compress_blocks(blocks, w_big, tm=256, tk=2048):
    n, kdim = blocks.shape
    tm = min(tm, n)
    return pl.pallas_call(
        _compress_kernel,
        grid=(n // tm, kdim // tk),
        in_specs=[pl.BlockSpec((tm, tk), lambda i, k: (i, k)),
                  pl.BlockSpec((tk, 2 * HEAD_DIM), lambda i, k: (k, 0))],
        out_specs=pl.BlockSpec((tm, 2 * HEAD_DIM), lambda i, k: (i, 0)),
        out_shape=jax.ShapeDtypeStruct((n, 2 * HEAD_DIM), jnp.float32),
        scratch_shapes=[pltpu.VMEM((tm, 2 * HEAD_DIM), jnp.float32)],
        name="nsa_compress",
    )(blocks, w_big)


def _softmax_step(carry, s, v):
    m, l, acc = carry
    m_new = jnp.maximum(m, jnp.max(s, axis=-1, keepdims=True))
    p = jnp.exp(s - m_new)
    a = jnp.exp(m - m_new)
    l = a * l + jnp.sum(p, axis=-1, keepdims=True)
    acc = a * acc + jnp.dot(p.astype(jnp.bfloat16), v, preferred_element_type=jnp.float32)
    return m_new, l, acc


def _select_blocks(imp, cur, n_pick):
    q, n_sb = imp.shape
    jq = lax.broadcasted_iota(jnp.int32, (q, n_sb), 1)
    jf = jq.astype(jnp.float32)
    score = jnp.where(jq > cur, NEG_INF, jnp.where((jq == cur) | (jq == 0), FORCED_SCORE, imp))

    def pick(_, carry):
        work, sel = carry
        mx = jnp.max(work, axis=-1, keepdims=True)
        first = jnp.min(jnp.where(work == mx, jf, float(n_sb)), axis=-1, keepdims=True)
        hit = jf == first
        sel = jnp.where(hit & (mx > 0.5 * NEG_INF), 1.0, sel)
        work = jnp.where(hit, -jnp.inf, work)
        return work, sel

    _, sel = lax.fori_loop(0, n_pick, pick, (score, jnp.zeros((q, n_sb), jnp.float32)))
    return sel


def _nsa_prompt_kernel(qs_ref, qf_ref, kc_ref, vc_ref, ksel_ref, vsel_ref, kwin_ref, vwin_ref, g_ref,
                       o_ref, *, tq, tk, tw):
    i = pl.program_id(1)
    f32, bf16 = jnp.float32, jnp.bfloat16
    H = NSA_HEADS
    R = H * tq
    qs = qs_ref[0].reshape(R, HEAD_DIM)
    qf = qf_ref[0].reshape(R, HEAD_DIM)
    row = lax.broadcasted_iota(jnp.int32, (R, 1), 0)
    head = row // tq
    slope = jnp.where(head == 0, NSA_SLOPES[0], jnp.where(head == 1, NSA_SLOPES[1],
                      jnp.where(head == 2, NSA_SLOPES[2], NSA_SLOPES[3]))).astype(f32)
    posq = i * tq + (row - head * tq)

    n_cb = kc_ref.shape[1]
    sc = lax.dot_general(qf, kc_ref[0], _NT, precision=_HI, preferred_element_type=f32) * ATTN_SCALE
    jb = lax.broadcasted_iota(jnp.int32, (R, n_cb), 1)
    distc = posq - (jb * NSA_BLOCK + NSA_BLOCK - 1)
    okc = distc >= 0
    sc = jnp.where(okc, sc - slope * distc.astype(f32), NEG_INF)
    pc = jnp.exp(sc - jnp.max(sc, axis=-1, keepdims=True)) * okc.astype(f32)
    pc = pc / jnp.maximum(jnp.sum(pc, axis=-1, keepdims=True), TINY)
    o_cmp = jnp.dot(pc.astype(bf16), vc_ref[0].astype(bf16), preferred_element_type=f32)
    imp = pc[0:tq] + pc[tq:2 * tq] + pc[2 * tq:3 * tq] + pc[3 * tq:4 * tq]

    pq = i * tq + lax.broadcasted_iota(jnp.int32, (tq, 1), 0)
    msel = _select_blocks(imp, pq // NSA_BLOCK, NSA_TOPK).astype(bf16)

    rowpos = i * tq + lax.broadcasted_iota(jnp.int32, (tq, 1), 0)
    init = (jnp.full((R, 1), NEG_INF, f32), jnp.zeros((R, 1), f32), jnp.zeros((R, HEAD_DIM), f32))

    def sel_body(j, carry):
        k0 = pl.multiple_of(j * tk, tk)
        k = ksel_ref[0, pl.ds(k0, tk), :]
        v = vsel_ref[0, pl.ds(k0, tk), :]
        s = lax.dot_general(qs, k, _NT, preferred_element_type=f32)
        eb = (lax.broadcasted_iota(jnp.int32, (n_cb, tk), 0)
              == j * (tk // NSA_BLOCK) + lax.broadcasted_iota(jnp.int32, (n_cb, tk), 1) // NSA_BLOCK)
        mexp = jnp.dot(msel, eb.astype(bf16), preferred_element_type=f32)
        d = rowpos - (k0 + lax.broadcasted_iota(jnp.int32, (tq, tk), 1))
        ok = (mexp > 0.5) & (d >= 0)
        s = jnp.where(_tile_rows(ok, H), s - slope * _tile_rows(d.astype(f32), H), NEG_INF)
        return _softmax_step(carry, s, v)

    n_sel = (i * tq + tq - 1) // tk + 1
    _, l_sel, a_sel = lax.fori_loop(0, n_sel, sel_body, init)
    o_sel = a_sel / l_sel

    def win_body(j, carry):
        k0 = pl.multiple_of(j * tw, tw)
        k = kwin_ref[0, pl.ds(k0, tw), :]
        v = vwin_ref[0, pl.ds(k0, tw), :]
        s = lax.dot_general(qs, k, _NT, preferred_element_type=f32)
        d = rowpos - (k0 + lax.broadcasted_iota(jnp.int32, (tq, tw), 1))
        ok = (d >= 0) & (d < NSA_WINDOW)
        s = jnp.where(_tile_rows(ok, H), s - slope * _tile_rows(d.astype(f32), H), NEG_INF)
        return _softmax_step(carry, s, v)

    lo = jnp.maximum((i * tq - NSA_WINDOW + 1) // tw, 0)
    hi = (i * tq + tq - 1) // tw + 1
    _, l_win, a_win = lax.fori_loop(lo, hi, win_body, init)
    o_win = a_win / l_win

    g = _sigmoid(g_ref[...])
    gate = lambda c: jnp.concatenate([g[:, 3 * h + c:3 * h + c + 1] for h in range(H)], axis=0)
    o = gate(0) * o_cmp + gate(1) * o_sel + gate(2) * o_win
    o_ref[0] = o.reshape(H, tq, HEAD_DIM)


def _heads_major(x, dtype):
    B, S, _ = x.shape
    return x.reshape(B, S, -1, HEAD_DIM).transpose(0, 2, 1, 3).astype(dtype)


def _nsa_prompt(nq, rows, win, gates, w_ck, w_cv, tq=128, tk=256, tw=128):
    B, S, _ = nq.shape
    bf16 = jnp.bfloat16
    kcvc = _nsa_compress_blocks(rows.reshape(B * S // NSA_BLOCK, -1), _compress_weights(w_ck, w_cv))
    kcvc = kcvc.reshape(B, S // NSA_BLOCK, 2 * HEAD_DIM)
    kc, vc = kcvc[..., :HEAD_DIM], kcvc[..., HEAD_DIM:]
    qs = _heads_major(nq * ATTN_SCALE, bf16)
    qf = _heads_major(nq, jnp.float32)
    col = lambda a, c: a[..., c * HEAD_DIM:(c + 1) * HEAD_DIM].astype(bf16)
    full = lambda n: pl.BlockSpec((1, n, HEAD_DIM), lambda b, i: (b, 0, 0))
    qspec = pl.BlockSpec((1, NSA_HEADS, tq, HEAD_DIM), lambda b, i: (b, 0, i, 0))
    o = pl.pallas_call(
        functools.partial(_nsa_prompt_kernel, tq=tq, tk=tk, tw=tw),
        grid=(B, S // tq),
        in_specs=[qspec, qspec, full(S // NSA_BLOCK), full(S // NSA_BLOCK), full(S), full(S), full(S), full(S),
                  pl.BlockSpec((None, tq, 128), lambda b, i: (b, i, 0))],
        out_specs=qspec,
        out_shape=jax.ShapeDtypeStruct((B, NSA_HEADS, S, HEAD_DIM), jnp.float32),
        compiler_params=pltpu.CompilerParams(dimension_semantics=("parallel", "arbitrary"),
                                             vmem_limit_bytes=VMEM_LIMIT_BYTES),
        name="nsa_prompt",
    )(qs, qf, kc, vc, col(rows, 2), col(rows, 3), col(win, 0), col(win, 1), gates)
    return o.transpose(0, 2, 1, 3).reshape(B, S, BRANCH_WIDTH)


def _fox_prompt_kernel(q_ref, k_ref, v_ref, fq_ref, fk_ref, o_ref, *, t):
    i = pl.program_id(1)
    f32, bf16 = jnp.float32, jnp.bfloat16
    lane = lax.broadcasted_iota(jnp.int32, (1, 2 * HEAD_DIM), 1)
    low = lane < HEAD_DIM
    rc = lax.broadcasted_iota(jnp.int32, (t, t), 0) - lax.broadcasted_iota(jnp.int32, (t, t), 1)
    for hp in range(FOX_HEADS // 2):
        cols = slice(hp * 2 * HEAD_DIM, (hp + 1) * 2 * HEAD_DIM)
        q2 = q_ref[0, :, cols]
        qa = jnp.where(low, q2, jnp.zeros_like(q2))
        qb = jnp.where(low, jnp.zeros_like(q2), q2)
        fqa = fq_ref[0, :, 2 * hp:2 * hp + 1]
        fqb = fq_ref[0, :, 2 * hp + 1:2 * hp + 2]

        def body(j, carry, diagonal):
            ma, la, mb, lb, acc = carry
            k0 = pl.multiple_of(j * t, t)
            k2 = k_ref[0, pl.ds(k0, t), cols]
            v2 = v_ref[0, pl.ds(k0, t), cols]
            fk = fk_ref[0, j]
            sa = lax.dot_general(qa, k2, _NT, preferred_element_type=f32) + (fqa - fk[2 * hp:2 * hp + 1])
            sb = lax.dot_general(qb, k2, _NT, preferred_element_type=f32) + (fqb - fk[2 * hp + 1:2 * hp + 2])
            if diagonal:
                sa = jnp.where(rc >= 0, sa, NEG_INF)
                sb = jnp.where(rc >= 0, sb, NEG_INF)
            ma_n = jnp.maximum(ma, jnp.max(sa, axis=-1, keepdims=True))
            mb_n = jnp.maximum(mb, jnp.max(sb, axis=-1, keepdims=True))
            pa = jnp.exp(sa - ma_n)
            pb = jnp.exp(sb - mb_n)
            aa = jnp.exp(ma - ma_n)
            ab = jnp.exp(mb - mb_n)
            la = aa * la + jnp.sum(pa, axis=-1, keepdims=True)
            lb = ab * lb + jnp.sum(pb, axis=-1, keepdims=True)
            pva = jnp.dot(pa.astype(bf16), v2, preferred_element_type=f32)
            pvb = jnp.dot(pb.astype(bf16), v2, preferred_element_type=f32)
            acc = jnp.where(low, aa * acc + pva, ab * acc + pvb)
            return ma_n, la, mb_n, lb, acc

        neg = jnp.full((t, 1), NEG_INF, f32)
        zero = jnp.zeros((t, 1), f32)
        carry = (neg, zero, neg, zero, jnp.zeros((t, 2 * HEAD_DIM), f32))
        carry = lax.fori_loop(0, i, functools.partial(body, diagonal=False), carry)
        _, la, _, lb, acc = body(i, carry, True)
        o_ref[0, :, cols] = acc / jnp.where(low, la, lb)


def _fox_prompt_attn(q, k, v, F, t=256):
    B, S, W = q.shape
    bf16 = jnp.bfloat16
    fk = jnp.pad(F.transpose(0, 2, 1), ((0, 0), (0, 8 - FOX_HEADS), (0, 0)))
    fk = fk.reshape(B, 8, S // t, t).transpose(0, 2, 1, 3)
    full = pl.BlockSpec((1, S, W), lambda b, i: (b, 0, 0))
    return pl.pallas_call(
        functools.partial(_fox_prompt_kernel, t=t),
        grid=(B, S // t),
        in_specs=[pl.BlockSpec((1, t, W), lambda b, i: (b, i, 0)), full, full,
                  pl.BlockSpec((1, t, FOX_HEADS), lambda b, i: (b, i, 0)),
                  pl.BlockSpec((1, S // t, 8, t), lambda b, i: (b, 0, 0, 0))],
        out_specs=pl.BlockSpec((1, t, W), lambda b, i: (b, i, 0)),
        out_shape=jax.ShapeDtypeStruct((B, S, W), jnp.float32),
        compiler_params=pltpu.CompilerParams(dimension_semantics=("parallel", "arbitrary"),
                                             vmem_limit_bytes=VMEM_LIMIT_BYTES),
        name="fox_prompt",
    )((q * ATTN_SCALE).astype(bf16), k.astype(bf16), v.astype(bf16), F, fk)


def _log_sigmoid(x):
    return jnp.minimum(x, 0.0) - jnp.log1p(jnp.exp(-jnp.abs(x)))


def _mlstm_kernel(q_ref, k_ref, kt_ref, v_ref, og_ref, gc_ref, gr_ref, nw_ref, cn0_ref, m0_ref,
                  h_ref, cn_ref, m_ref, cn_s, m_s, *, nb, L):
    c = pl.program_id(1)
    f32 = jnp.float32
    W = 2 * HEAD_DIM
    n_pairs = MLSTM_HEADS // 2

    @pl.when(c == 0)
    def _():
        cn_s[...] = cn0_ref[...]
        m_s[...] = m0_ref[...]

    lane = lax.broadcasted_iota(jnp.int32, (1, W), 1)
    low = lane < HEAD_DIM
    ti = lax.broadcasted_iota(jnp.int32, (L, L), 0)
    si = lax.broadcasted_iota(jnp.int32, (L, L), 1)
    causal = si <= ti
    tri = causal.astype(f32)
    tri_t = (ti <= si).astype(f32)
    srow = lax.broadcasted_iota(jnp.int32, (W, 2 * W), 0)
    slane = lax.broadcasted_iota(jnp.int32, (W, 2 * W), 1)
    top = srow < HEAD_DIM
    keep_a = top & ((slane < HEAD_DIM) | (slane == W))
    keep_b = (~top) & (((slane >= HEAD_DIM) & (slane < W)) | (slane == W + 1))
    lane_w = lax.broadcasted_iota(jnp.int32, (1, W), 1)
    mdt = k_ref.dtype

    for b in range(nb):
        gcol = gc_ref[b]
        grow = gr_ref[b]
        bcol = jnp.dot(tri, _log_sigmoid(gcol), precision=_HI, preferred_element_type=f32)
        brow = jnp.dot(_log_sigmoid(grow), tri_t, precision=_HI, preferred_element_type=f32)
        for hp in range(n_pairs):
            cols = slice(hp * W, (hp + 1) * W)
            q2 = q_ref[b, :, cols]
            k2 = k_ref[b, :, cols]
            v2 = v_ref[b, :, cols]
            kt2 = kt_ref[b, cols, :]
            cn = cn_s[b, hp]
            r = jnp.dot(q2, cn.astype(mdt), preferred_element_type=f32)
            zq = jnp.zeros_like(q2)
            per_head = []
            for x in range(2):
                h = 2 * hp + x
                qx = jnp.where(low, q2, zq) if x == 0 else jnp.where(low, zq, q2)
                b_c = bcol[:, MLSTM_HEADS + h:MLSTM_HEADS + h + 1]
                b_r = brow[MLSTM_HEADS + h:MLSTM_HEADS + h + 1, :]
                ig_c = gcol[:, h:h + 1]
                ig_r = grow[h:h + 1, :]
                m_prev = m_s[b, 0:1, h:h + 1]
                dmat = jnp.where(causal, b_c - b_r + ig_r, NEG_INF)
                a_c = b_c + m_prev
                m_t = jnp.maximum(a_c, jnp.max(dmat, axis=-1, keepdims=True))
                wq = jnp.exp(dmat - m_t) * lax.dot_general(qx, k2, _NT, preferred_element_type=f32)
                inter = jnp.exp(a_c - m_t)
                wv = jnp.dot(wq.astype(mdt), v2, preferred_element_type=f32)
                den = inter * r[:, W + x:W + x + 1] + jnp.sum(wq, axis=-1, keepdims=True)
                den = jnp.maximum(jnp.abs(den), jnp.exp(-m_t))
                bl = b_c[L - 1:L, :]
                g_c = bl - b_c + ig_c
                m_new = jnp.maximum(bl + m_prev, jnp.max(g_c, axis=0, keepdims=True))
                ws = jnp.exp(g_c - m_new)
                decay = jnp.exp(bl + m_prev - m_new)
                aug = jnp.concatenate([v2.astype(f32) * ws, jnp.where(lane_w == x, ws, 0.0)], axis=1)
                u = jnp.dot(kt2, aug.astype(mdt), preferred_element_type=f32)
                per_head.append((inter, wv, den, decay, u))
                m_s[b, 0:1, h:h + 1] = m_new
            (ia, wva, dena, deca, ua), (ib, wvb, denb, decb, ub) = per_head
            num = jnp.where(low, ia * r[:, :W] + wva, ib * r[:, :W] + wvb)
            hid = num / jnp.where(low, dena, denb)
            cn_s[b, hp] = (jnp.where(top, deca, decb) * cn + jnp.where(keep_a, ua, 0.0)
                           + jnp.where(keep_b, ub, 0.0))
            hid = _sigmoid(og_ref[b, :, cols]) * hid
            mean = lambda t: jnp.where(low, jnp.sum(jnp.where(low, t, 0.0), axis=-1, keepdims=True),
                                       jnp.sum(jnp.where(low, 0.0, t), axis=-1, keepdims=True)) / HEAD_DIM
            mu = mean(hid)
            var = mean(jnp.square(hid - mu))
            h_ref[b, :, cols] = (hid - mu) * lax.rsqrt(var + LN_EPS) * nw_ref[:, cols]

    @pl.when(c == pl.num_programs(1) - 1)
    def _():
        cn_ref[...] = cn_s[...]
        m_ref[...] = m_s[...]


def _mlstm_state_pack(c, n, m):
    Bx = c.shape[0]
    W = 2 * HEAD_DIM
    cn = jnp.zeros((Bx, MLSTM_HEADS // 2, W, 2 * W), jnp.float32)
    for hp in range(MLSTM_HEADS // 2):
        for x in range(2):
            r0 = x * HEAD_DIM
            cn = cn.at[:, hp, r0:r0 + HEAD_DIM, r0:r0 + HEAD_DIM].set(c[:, 2 * hp + x])
            cn = cn.at[:, hp, r0:r0 + HEAD_DIM, W + x].set(n[:, 2 * hp + x])
    m8 = jnp.zeros((Bx, 8, 128), jnp.float32).at[:, 0, :MLSTM_HEADS].set(m)
    return cn, m8


def _mlstm_state_unpack(cn, m8):
    W = 2 * HEAD_DIM
    c = jnp.stack([cn[:, h // 2, (h % 2) * HEAD_DIM:(h % 2 + 1) * HEAD_DIM,
                      (h % 2) * HEAD_DIM:(h % 2 + 1) * HEAD_DIM] for h in range(MLSTM_HEADS)], axis=1)
    n = jnp.stack([cn[:, h // 2, (h % 2) * HEAD_DIM:(h % 2 + 1) * HEAD_DIM, W + h % 2]
                   for h in range(MLSTM_HEADS)], axis=1)
    return c, n, m8[:, 0, :MLSTM_HEADS]


def _mlstm(q, k, v, og, mi, mf, norm_w, c0, n0, m0, L, nb, mxu_dtype):
    Bx, S, W4 = q.shape
    f32 = jnp.float32
    gates = jnp.concatenate([mi, mf], axis=-1).astype(f32)
    gcol = jnp.pad(gates, ((0, 0), (0, 0), (0, 128 - 2 * MLSTM_HEADS)))
    grow = gates.transpose(0, 2, 1)
    cn0, m8 = _mlstm_state_pack(c0.astype(f32), n0.astype(f32), m0.astype(f32))
    tok = lambda w: pl.BlockSpec((nb, L, w), lambda b, c: (b, c, 0))
    st_cn = pl.BlockSpec((nb, MLSTM_HEADS // 2, 128, 256), lambda b, c: (b, 0, 0, 0))
    st_m = pl.BlockSpec((nb, 8, 128), lambda b, c: (b, 0, 0))
    h, cn, m8 = pl.pallas_call(
        functools.partial(_mlstm_kernel, nb=nb, L=L),
        grid=(Bx // nb, S // L),
        in_specs=[tok(W4), tok(W4), pl.BlockSpec((nb, W4, L), lambda b, c: (b, 0, c)), tok(W4), tok(W4),
                  tok(128), pl.BlockSpec((nb, 8, L), lambda b, c: (b, 0, c)),
                  pl.BlockSpec((1, W4), lambda b, c: (0, 0)), st_cn, st_m],
        out_specs=[tok(W4), st_cn, st_m],
        out_shape=[jax.ShapeDtypeStruct((Bx, S, W4), f32),
                   jax.ShapeDtypeStruct(cn0.shape, f32), jax.ShapeDtypeStruct(m8.shape, f32)],
        scratch_shapes=[pltpu.VMEM((nb, MLSTM_HEADS // 2, 128, 256), f32), pltpu.VMEM((nb, 8, 128), f32)],
        compiler_params=pltpu.CompilerParams(dimension_semantics=("parallel", "arbitrary"),
                                             vmem_limit_bytes=VMEM_LIMIT_BYTES),
        name="mlstm",
    )(q.astype(mxu_dtype), k.astype(mxu_dtype), k.astype(mxu_dtype).transpose(0, 2, 1), v.astype(mxu_dtype),
      og.astype(f32), gcol, grow, norm_w.astype(f32).reshape(1, W4), cn0, m8)
    return h, _mlstm_state_unpack(cn, m8)


def _ln(x, w, b):
    mu = jnp.mean(x, axis=-1, keepdims=True)
    var = jnp.mean(jnp.square(x - mu), axis=-1, keepdims=True)
    return (x - mu) * lax.rsqrt(var + LN_EPS) * w + b


def _ln_kernel(x_ref, w_ref, b_ref, o_ref):
    o_ref[...] = _ln(x_ref[...], w_ref[...], b_ref[...])


def _layer_norm_rows(x, w, b, tm=1024):
    T, D = x.shape
    vec = pl.BlockSpec((1, D), lambda i: (0, 0))
    return pl.pallas_call(
        _ln_kernel, grid=(T // tm,),
        in_specs=[pl.BlockSpec((tm, D), lambda i: (i, 0)), vec, vec],
        out_specs=pl.BlockSpec((tm, D), lambda i: (i, 0)),
        out_shape=jax.ShapeDtypeStruct((T, D), jnp.float32),
        name="layer_norm",
    )(x, w.reshape(1, D), b.reshape(1, D))


Z_GM = 0
Z_NQ = 3072
Z_ROWS = Z_NQ + 256
Z_MQ = 3584
Z_MK = Z_MQ + 256
Z_MV = 4096
Z_MO = Z_MV + 256
Z_FQ = 4608
Z_WIN = Z_FQ + 256
Z_SMALL = Z_WIN + 128
Z_FK = 5120
Z_FV = Z_FK + 256
Z_WIDTH = 5632
Z_TILE = 512
SMALL_MI, SMALL_MF, SMALL_FF = 12, 16, 20


def _z_column_order():
    starts = np.concatenate([[0], np.cumsum(IN_SPLITS)])
    seg = lambda i, lo=0, hi=None: np.arange(starts[i] + lo, starts[i] + (IN_SPLITS[i] if hi is None else hi))
    pad = lambda n: np.full((n,), -1)
    order = np.concatenate([
        seg(13), seg(0), seg(1, 0, 256), seg(3), seg(4), seg(5), seg(6),
        seg(9), seg(1, 256, 384), seg(2), seg(7), seg(8), seg(12), pad(128 - 24), seg(10), seg(11)])
    assert order.shape == (Z_WIDTH,)
    return order


def _permute_in_proj(w_in, b_in):
    order = _z_column_order()
    valid = jnp.asarray(order >= 0)
    idx = jnp.asarray(np.maximum(order, 0))
    w = jnp.where(valid[None, :], w_in[:, idx], 0.0)
    b = jnp.where(valid, b_in[idx], 0.0)
    return w, b


def _proj_kernel(x_ref, w_ref, wlo_ref, b_ref, o_ref, xh_s, xl_s, *, hi_tile):
    j = pl.program_id(1)
    f32 = jnp.float32

    @pl.when(j == 0)
    def _():
        x = x_ref[...]
        xh = x.astype(jnp.bfloat16)
        xh_s[...] = xh
        xl_s[...] = (x - xh.astype(f32)).astype(jnp.bfloat16)

    @pl.when(j != hi_tile)
    def _():
        o_ref[...] = jnp.dot(xh_s[...], w_ref[...], preferred_element_type=f32) + b_ref[...]

    @pl.when(j == hi_tile)
    def _():
        acc = jnp.dot(xh_s[...], wlo_ref[...], preferred_element_type=f32)
        acc += jnp.dot(xl_s[...], w_ref[...], preferred_element_type=f32)
        acc += jnp.dot(xh_s[...], w_ref[...], preferred_element_type=f32)
        o_ref[...] = acc + b_ref[...]


def _in_projection(x, w_in, b_in, tm=1024):
    T, D = x.shape
    w, b = _permute_in_proj(w_in, b_in)
    wh = w.astype(jnp.bfloat16)
    hi_tile = Z_NQ // Z_TILE
    wlo = (w[:, Z_NQ:Z_NQ + Z_TILE] - wh[:, Z_NQ:Z_NQ + Z_TILE].astype(jnp.float32)).astype(jnp.bfloat16)
    return pl.pallas_call(
        functools.partial(_proj_kernel, hi_tile=hi_tile),
        grid=(T // tm, Z_WIDTH // Z_TILE),
        in_specs=[pl.BlockSpec((tm, D), lambda i, j: (i, 0)),
                  pl.BlockSpec((D, Z_TILE), lambda i, j: (0, j)),
                  pl.BlockSpec((D, Z_TILE), lambda i, j: (0, 0)),
                  pl.BlockSpec((1, Z_TILE), lambda i, j: (0, j))],
        out_specs=pl.BlockSpec((tm, Z_TILE), lambda i, j: (i, j)),
        out_shape=jax.ShapeDtypeStruct((T, Z_WIDTH), jnp.float32),
        scratch_shapes=[pltpu.VMEM((tm, D), jnp.bfloat16), pltpu.VMEM((tm, D), jnp.bfloat16)],
        compiler_params=pltpu.CompilerParams(dimension_semantics=("parallel", "arbitrary"),
                                             vmem_limit_bytes=VMEM_LIMIT_BYTES),
        name="in_projection",
    )(x, wh, wlo, b.reshape(1, Z_WIDTH))


def _merge_kernel(x_ref, gm_ref, on_ref, om_ref, of_ref, wb_ref, wo_ref, lw_ref, lb_ref, o_ref, *, alpha):
    f32, bf16 = jnp.float32, jnp.bfloat16
    y = None
    for m, br in enumerate((on_ref, om_ref, of_ref)):
        proj = jnp.dot(br[...].astype(bf16), wb_ref[m], preferred_element_type=f32)
        term = _sigmoid(gm_ref[:, m * D_MODEL:(m + 1) * D_MODEL]) * proj
        y = term if y is None else y + term
    mix = jnp.dot(y.astype(bf16), wo_ref[...], preferred_element_type=f32)
    o_ref[...] = _ln(alpha * x_ref[...] + mix, lw_ref[...], lb_ref[...])


def _merge_ln(x, z, o_nsa, o_ml, o_fox, w_branch, w_out, ln_w, ln_b, alpha, tm=512):
    T, D = x.shape
    tok = lambda w: pl.BlockSpec((tm, w), lambda i: (i, 0))
    vec = pl.BlockSpec((1, D), lambda i: (0, 0))
    return pl.pallas_call(
        functools.partial(_merge_kernel, alpha=alpha),
        grid=(T // tm,),
        in_specs=[tok(D), tok(N_BRANCH * D), tok(BRANCH_WIDTH), tok(BRANCH_WIDTH), tok(BRANCH_WIDTH),
                  pl.BlockSpec((N_BRANCH, BRANCH_WIDTH, D), lambda i: (0, 0, 0)),
                  pl.BlockSpec((D, D), lambda i: (0, 0)), vec, vec],
        out_specs=tok(D),
        out_shape=jax.ShapeDtypeStruct((T, D), jnp.float32),
        compiler_params=pltpu.CompilerParams(dimension_semantics=("parallel",),
                                             vmem_limit_bytes=VMEM_LIMIT_BYTES),
        name="merge_ln",
    )(x, z, o_nsa, o_ml, o_fox, w_branch.astype(jnp.bfloat16), w_out.astype(jnp.bfloat16),
      ln_w.reshape(1, D), ln_b.reshape(1, D))


def _route(logits):
    tm = logits.shape[0]
    lane = lax.broadcasted_iota(jnp.int32, (tm, 128), 1)
    lanef = lane.astype(jnp.float32)
    big = 1e9
    is_g = lane < N_GROUPS
    lg = jnp.where(is_g, logits, -jnp.inf)
    eg = jnp.exp(lg - jnp.max(lg, axis=-1, keepdims=True))
    pg = eg / jnp.sum(eg, axis=-1, keepdims=True)
    g_val = jnp.max(pg, axis=-1, keepdims=True)
    g_idx = jnp.min(jnp.where(is_g & (pg == g_val), lanef, big), axis=-1, keepdims=True)
    e_lo = N_GROUPS + EXPERTS_PER_GROUP * g_idx
    in_grp = (lanef >= e_lo) & (lanef < e_lo + EXPERTS_PER_GROUP)
    le = jnp.where(in_grp, logits, -jnp.inf)
    ee = jnp.exp(le - jnp.max(le, axis=-1, keepdims=True))
    pe = ee / jnp.sum(ee, axis=-1, keepdims=True)
    v1 = jnp.max(pe, axis=-1, keepdims=True)
    i1 = jnp.min(jnp.where(in_grp & (pe == v1), lanef, big), axis=-1, keepdims=True)
    rest = in_grp & (lanef != i1)
    pe2 = jnp.where(rest, pe, -1.0)
    v2 = jnp.max(pe2, axis=-1, keepdims=True)
    i2 = jnp.min(jnp.where(rest & (pe2 == v2), lanef, big), axis=-1, keepdims=True)
    tot = v1 + v2
    return jnp.where(lanef == i1, g_val * v1 / tot, jnp.where(lanef == i2, g_val * v2 / tot, 0.0))


def _moe_kernel(x_ref, wr_ref, br_ref, wgu_ref, wd_ref, lw_ref, lb_ref, o_ref, xb_s, gate_s, acc_s, *, alpha):
    e = pl.program_id(1)
    f32, bf16 = jnp.float32, jnp.bfloat16

    @pl.when(e == 0)
    def _():
        x = x_ref[...]
        xb_s[...] = x.astype(bf16)
        logits = jnp.dot(x, wr_ref[...], precision=_HI, preferred_element_type=f32) + br_ref[...]
        gate_s[...] = _route(logits)
        acc_s[...] = jnp.zeros_like(acc_s)

    lane = lax.broadcasted_iota(jnp.int32, gate_s.shape, 1)
    gate = jnp.sum(jnp.where(lane == N_GROUPS + e, gate_s[...], 0.0), axis=-1, keepdims=True)
    gu = jnp.dot(xb_s[...], wgu_ref[0], preferred_element_type=f32)
    g, u = gu[:, :D_EXPERT], gu[:, D_EXPERT:]
    h = (g * _sigmoid(g)) * u * gate
    acc_s[...] += jnp.dot(h.astype(bf16), wd_ref[0], preferred_element_type=f32)

    @pl.when(e == pl.num_programs(1) - 1)
    def _():
        o_ref[...] = _ln(alpha * x_ref[...] + acc_s[...], lw_ref[...], lb_ref[...])


def _moe_ln(x, w_group, b_group, w_expert, b_expert, w_gate, w_up, w_down, ln_w, ln_b, alpha, tm=1024):
    T, D = x.shape
    bf16 = jnp.bfloat16
    n_r = N_GROUPS + N_EXPERTS
    wr = jnp.pad(jnp.concatenate([w_group, w_expert], axis=1), ((0, 0), (0, 128 - n_r)))
    br = jnp.pad(jnp.concatenate([b_group, b_expert]), (0, 128 - n_r)).reshape(1, 128)
    wgu = jnp.concatenate([w_gate, w_up], axis=-1).astype(bf16)
    vec = pl.BlockSpec((1, D), lambda i, e: (0, 0))
    return pl.pallas_call(
        functools.partial(_moe_kernel, alpha=alpha),
        grid=(T // tm, N_EXPERTS),
        in_specs=[pl.BlockSpec((tm, D), lambda i, e: (i, 0)),
                  pl.BlockSpec((D, 128), lambda i, e: (0, 0)),
                  pl.BlockSpec((1, 128), lambda i, e: (0, 0)),
                  pl.BlockSpec((1, D, 2 * D_EXPERT), lambda i, e: (e, 0, 0)),
                  pl.BlockSpec((1, D_EXPERT, D), lambda i, e: (e, 0, 0)), vec, vec],
        out_specs=pl.BlockSpec((tm, D), lambda i, e: (i, 0)),
        out_shape=jax.ShapeDtypeStruct((T, D), jnp.float32),
        scratch_shapes=[pltpu.VMEM((tm, D), bf16), pltpu.VMEM((tm, 128), jnp.float32),
                        pltpu.VMEM((tm, D), jnp.float32)],
        compiler_params=pltpu.CompilerParams(dimension_semantics=("parallel", "arbitrary"),
                                             vmem_limit_bytes=VMEM_LIMIT_BYTES),
        name="moe_ln",
    )(x, wr, br, wgu, w_down.astype(bf16), ln_w.reshape(1, D), ln_b.reshape(1, D))


N_PAGES = PAST_LEN // PAGE_SIZE
NEW_PAD = 128
DEC_KEYS = PAST_LEN + NEW_PAD
DEC_ROWS = NSA_HEADS * DEC_SEQ


def _page_specs(width):
    def spec(p):
        return pl.BlockSpec((1, PAGE_SIZE, width), lambda b, pt: (pt[b * N_PAGES + p], 0, 0))
    return [spec(p) for p in range(N_PAGES)]


def _softmax_rows(s):
    m = jnp.max(s, axis=-1, keepdims=True)
    p = jnp.exp(s - m)
    return p, jnp.sum(p, axis=-1, keepdims=True)


def _fox_decode_kernel(pt_ref, qbd_ref, knew_ref, vnew_ref, fk_ref, fq_ref, *refs):
    pages, o_ref = refs[:N_PAGES], refs[N_PAGES]
    f32, bf16 = jnp.float32, jnp.bfloat16
    W = FOX_HEADS * HEAD_DIM
    qbd = qbd_ref[0]
    s = [lax.dot_general(qbd, pg[0, :, :W].astype(bf16), _NT, preferred_element_type=f32) for pg in pages]
    s.append(lax.dot_general(qbd, knew_ref[0], _NT, preferred_element_type=f32))
    s = jnp.concatenate(s, axis=1)
    rowh = lax.broadcasted_iota(jnp.int32, (DEC_ROWS, 1), 0) // DEC_SEQ
    fk = fk_ref[0]
    fk_rows = jnp.where(rowh == 0, fk[0:1], jnp.where(rowh == 1, fk[1:2], jnp.where(rowh == 2, fk[2:3], fk[3:4])))
    col = lax.broadcasted_iota(jnp.int32, (DEC_ROWS, DEC_KEYS), 1)
    t = lax.broadcasted_iota(jnp.int32, (DEC_ROWS, DEC_KEYS), 0) % DEC_SEQ
    ok = (col < PAST_LEN) | (col - PAST_LEN <= t)
    s = jnp.where(ok, s + (fq_ref[0] - fk_rows), NEG_INF)
    p, l = _softmax_rows(s)
    pb = p.astype(bf16)
    o = jnp.dot(pb[:, PAST_LEN:], vnew_ref[0], preferred_element_type=f32)
    for i, pg in enumerate(pages):
        o += jnp.dot(pb[:, i * PAGE_SIZE:(i + 1) * PAGE_SIZE], pg[0, :, W:].astype(bf16), preferred_element_type=f32)
    o = o / l
    lane_h = lax.broadcasted_iota(jnp.int32, (DEC_ROWS, W), 1) // HEAD_DIM
    o = jnp.where(lane_h == rowh, o, 0.0)
    o_ref[0] = o[0:8] + o[8:16] + o[16:24] + o[24:32]


def _pad_new(x):
    return jnp.pad(x, ((0, 0), (0, NEW_PAD - DEC_SEQ), (0, 0))).astype(jnp.bfloat16)


def _fox_decode(q, k_new, v_new, lf_new, cache_kv, cache_lf, page_table):
    DB = q.shape[0]
    f32, bf16 = jnp.float32, jnp.bfloat16
    W = FOX_HEADS * HEAD_DIM
    eye = jnp.eye(FOX_HEADS, dtype=f32)
    qh = (q * ATTN_SCALE).reshape(DB, DEC_SEQ, FOX_HEADS, HEAD_DIM).transpose(0, 2, 1, 3)
    qbd = (qh[:, :, :, None, :] * eye[None, :, None, :, None]).reshape(DB, DEC_ROWS, W).astype(bf16)
    lf_all = jnp.concatenate([cache_lf[page_table].reshape(DB, PAST_LEN, FOX_HEADS).astype(f32), lf_new], axis=1)
    F = jnp.cumsum(lf_all, axis=1)
    fk = jnp.pad(F.transpose(0, 2, 1), ((0, 0), (0, 8 - FOX_HEADS), (0, DEC_KEYS - PAST_LEN - DEC_SEQ)))
    fq = F[:, PAST_LEN:].transpose(0, 2, 1).reshape(DB, DEC_ROWS, 1)
    per_seq = lambda r, w: pl.BlockSpec((1, r, w), lambda b, pt: (b, 0, 0))
    pages = cache_kv.reshape(-1, PAGE_SIZE, 2 * W)
    return pl.pallas_call(
        _fox_decode_kernel,
        grid_spec=pltpu.PrefetchScalarGridSpec(
            num_scalar_prefetch=1, grid=(DB,),
            in_specs=[per_seq(DEC_ROWS, W), per_seq(NEW_PAD, W), per_seq(NEW_PAD, W), per_seq(8, DEC_KEYS),
                      per_seq(DEC_ROWS, 1)] + _page_specs(2 * W),
            out_specs=per_seq(DEC_SEQ, W)),
        out_shape=jax.ShapeDtypeStruct((DB, DEC_SEQ, W), f32),
        compiler_params=pltpu.CompilerParams(dimension_semantics=("parallel",), vmem_limit_bytes=VMEM_LIMIT_BYTES),
        name="fox_decode",
    )(page_table.reshape(-1), qbd, _pad_new(k_new), _pad_new(v_new), fk, fq, *([pages] * N_PAGES))


def _nsa_decode_kernel(pt_ref, qs_ref, qf_ref, kcvc_ref, rnew_ref, wbuf_ref, wnew_ref, g_ref, e_ref, *refs):
    pages, o_ref = refs[:N_PAGES], refs[N_PAGES]
    f32, bf16 = jnp.float32, jnp.bfloat16
    H, T, HD = NSA_HEADS, DEC_SEQ, HEAD_DIM
    qs = qs_ref[0]
    row = lax.broadcasted_iota(jnp.int32, (DEC_ROWS, 1), 0)
    head = row // T
    slope = jnp.where(head == 0, NSA_SLOPES[0], jnp.where(head == 1, NSA_SLOPES[1],
                      jnp.where(head == 2, NSA_SLOPES[2], NSA_SLOPES[3]))).astype(f32)
    posq = PAST_LEN + row % T

    kcvc = kcvc_ref[0]
    n_cb = kcvc.shape[0]
    sc = lax.dot_general(qf_ref[0], kcvc, _NT, precision=_HI, preferred_element_type=f32) * ATTN_SCALE
    jb = lax.broadcasted_iota(jnp.int32, (DEC_ROWS, n_cb), 1)
    distc = posq - (jb * NSA_BLOCK + NSA_BLOCK - 1)
    okc = distc >= 0
    sc = jnp.where(okc, sc - slope * distc.astype(f32), NEG_INF)
    pc = jnp.exp(sc - jnp.max(sc, axis=-1, keepdims=True)) * okc.astype(f32)
    pc = pc / jnp.maximum(jnp.sum(pc, axis=-1, keepdims=True), TINY)
    o_cmp = jnp.dot(pc.astype(bf16), kcvc.astype(bf16), preferred_element_type=f32)[:, HD:]
    imp = pc[0:T] + pc[T:2 * T] + pc[2 * T:3 * T] + pc[3 * T:4 * T]
    imp = jnp.concatenate([imp, jnp.zeros((T, 128 - n_cb), f32)], axis=1)
    cur = (PAST_LEN + lax.broadcasted_iota(jnp.int32, (T, 1), 0)) // NSA_BLOCK
    msel = _select_blocks(imp, cur, NSA_TOPK).astype(bf16)

    mexp = jnp.dot(msel, e_ref[...], preferred_element_type=f32)
    col = lax.broadcasted_iota(jnp.int32, (T, DEC_KEYS), 1)
    tq = lax.broadcasted_iota(jnp.int32, (T, DEC_KEYS), 0)
    d = PAST_LEN + tq - col
    ok = (mexp > 0.5) & (d >= 0)
    kv = [pg[0, :, 2 * HD:].astype(bf16) for pg in pages] + [rnew_ref[0]]
    s = jnp.concatenate([lax.dot_general(qs, x, _NT, preferred_element_type=f32) for x in kv], axis=1)
    s = jnp.where(_tile_rows(ok, H), s - slope * _tile_rows(d.astype(f32), H), NEG_INF)
    p, l = _softmax_rows(s)
    pb = p.astype(bf16)
    acc = jnp.zeros((DEC_ROWS, 2 * HD), f32)
    for i, x in enumerate(kv):
        acc += jnp.dot(pb[:, i * PAGE_SIZE:(i + 1) * PAGE_SIZE], x, preferred_element_type=f32)
    o_sel = acc[:, HD:] / l

    wb = wbuf_ref.shape[1]
    kvw = [wbuf_ref[0].astype(bf16), wnew_ref[0]]
    sw = jnp.concatenate([lax.dot_general(qs, x, _NT, preferred_element_type=f32) for x in kvw], axis=1)
    colw = lax.broadcasted_iota(jnp.int32, (T, wb + NEW_PAD), 1)
    tw = lax.broadcasted_iota(jnp.int32, (T, wb + NEW_PAD), 0)
    dw = wb + tw - colw
    okw = (dw >= 0) & (dw < NSA_WINDOW)
    sw = jnp.where(_tile_rows(okw, H), sw - slope * _tile_rows(dw.astype(f32), H), NEG_INF)
    pw, lw = _softmax_rows(sw)
    pwb = pw.astype(bf16)
    accw = (jnp.dot(pwb[:, :wb], kvw[0], preferred_element_type=f32)
            + jnp.dot(pwb[:, wb:], kvw[1], preferred_element_type=f32))
    o_win = accw[:, HD:] / lw

    g = _sigmoid(g_ref[0])
    gate = lambda c: jnp.concatenate([g[:, 3 * h + c:3 * h + c + 1] for h in range(H)], axis=0)
    o_ref[0] = gate(0) * o_cmp + gate(1) * o_sel + gate(2) * o_win


def _nsa_decode(nq, rows_new, win_new, gates, cache_rows, cache_win, page_table, w_ck, w_cv):
    DB = nq.shape[0]
    f32, bf16 = jnp.float32, jnp.bfloat16
    HD = HEAD_DIM
    n_phys = cache_rows.shape[0]
    kcvc = _nsa_compress_blocks(cache_rows.reshape(n_phys * (PAGE_SIZE // NSA_BLOCK), -1),
                                _compress_weights(w_ck, w_cv))
    kcvc = kcvc.reshape(n_phys, -1)[page_table].reshape(DB, PAST_LEN // NSA_BLOCK, 2 * HD)
    stack = lambda x: x.reshape(DB, DEC_SEQ, NSA_HEADS, HD).transpose(0, 2, 1, 3).reshape(DB, DEC_ROWS, HD)
    pad_q = lambda x: jnp.pad(x, ((0, 0), (0, 0), (0, HD)))
    qs = pad_q(stack(nq * ATTN_SCALE)).astype(bf16)
    qf = pad_q(stack(nq)).astype(f32)
    colk = np.arange(DEC_KEYS)
    e = (np.arange(128)[:, None] == colk[None, :] // NSA_BLOCK) & (colk[None, :] < PAST_LEN + NSA_BLOCK)
    wb = cache_win.shape[1]
    per_seq = lambda r, w: pl.BlockSpec((1, r, w), lambda b, pt: (b, 0, 0))
    o = pl.pallas_call(
        _nsa_decode_kernel,
        grid_spec=pltpu.PrefetchScalarGridSpec(
            num_scalar_prefetch=1, grid=(DB,),
            in_specs=[per_seq(DEC_ROWS, 2 * HD), per_seq(DEC_ROWS, 2 * HD), per_seq(PAST_LEN // NSA_BLOCK, 2 * HD),
                      per_seq(NEW_PAD, 2 * HD), per_seq(wb, 2 * HD), per_seq(NEW_PAD, 2 * HD), per_seq(DEC_SEQ, 128),
                      pl.BlockSpec((128, DEC_KEYS), lambda b, pt: (0, 0))] + _page_specs(NSA_ROWS * HD),
            out_specs=per_seq(DEC_ROWS, HD)),
        out_shape=jax.ShapeDtypeStruct((DB, DEC_ROWS, HD), f32),
        compiler_params=pltpu.CompilerParams(dimension_semantics=("parallel",), vmem_limit_bytes=VMEM_LIMIT_BYTES),
        name="nsa_decode",
    )(page_table.reshape(-1), qs, qf, kcvc, _pad_new(rows_new[..., 2 * HD:]), cache_win.reshape(DB, wb, 2 * HD),
      _pad_new(win_new), gates, jnp.asarray(e, bf16),
      *([cache_rows.reshape(n_phys, PAGE_SIZE, NSA_ROWS * HD)] * N_PAGES))
    return o.reshape(DB, NSA_HEADS, DEC_SEQ, HD).transpose(0, 2, 1, 3).reshape(DB, DEC_SEQ, BRANCH_WIDTH)


def _copy_kernel(pt_ref, x_ref, o_ref):
    o_ref[...] = x_ref[...]


def _gather_pages(cache, page_table):
    db, n_pages = page_table.shape
    _, page, w = cache.shape
    out = pl.pallas_call(
        _copy_kernel,
        grid_spec=pltpu.PrefetchScalarGridSpec(
            num_scalar_prefetch=1, grid=(db * n_pages,),
            in_specs=[pl.BlockSpec((1, page, w), lambda i, pt: (pt[i], 0, 0))],
            out_specs=pl.BlockSpec((1, page, w), lambda i, pt: (i, 0, 0))),
        out_shape=jax.ShapeDtypeStruct((db * n_pages, page, w), cache.dtype),
        name="gather_pages",
    )(page_table.reshape(-1), cache)
    return out.reshape(db, n_pages * page, w)


def _split_points():
    return [int(v) for v in np.cumsum(IN_SPLITS)[:-1]]


def _layer_norm(x, w, b):
    xf = x.astype(jnp.float32)
    mu = jnp.mean(xf, axis=-1, keepdims=True)
    var = jnp.mean(jnp.square(xf - mu), axis=-1, keepdims=True)
    y = (xf - mu) * lax.rsqrt(var + LN_EPS) * w.astype(jnp.float32) + b.astype(jnp.float32)
    return y.astype(x.dtype)


def _masked_softmax(s, mask):
    s = jnp.where(mask, s.astype(jnp.float32), NEG_INF)
    p = jnp.exp(s - jnp.max(s, axis=-1, keepdims=True)) * mask
    return p / jnp.maximum(jnp.sum(p, axis=-1, keepdims=True), TINY)


def _alibi_slopes(n):
    return jnp.exp2(-8.0 * jnp.arange(1, n + 1, dtype=jnp.float32) / n)


def _project(x, w_in, b_in):
    B, L, _ = x.shape
    f32 = jnp.float32
    z = _pallas_linear(x.reshape(B * L, -1), w_in, b_in).reshape(B, L, -1)
    (nq, nkv, ng, mq, mk, mv, mo, mi, mf, fq, fk, fv, ff, gm) = jnp.split(z, _split_points(), axis=-1)
    heads = lambda t, h: t.reshape(B, L, h, HEAD_DIM)
    rows = nkv.reshape(B, L, 6, HEAD_DIM)
    return dict(
        nsa_q=heads(nq, NSA_HEADS),
        nsa_rows=rows[:, :, :NSA_ROWS],
        nsa_win=rows[:, :, NSA_ROWS:],
        nsa_g=jax.nn.sigmoid(ng.astype(f32)).reshape(B, L, NSA_HEADS, 3),
        nsa_g_raw=ng,
        ml_q=heads(mq, MLSTM_HEADS).astype(f32),
        ml_k=heads(mk, MLSTM_HEADS).astype(f32) * ATTN_SCALE,
        ml_v=heads(mv, MLSTM_HEADS).astype(f32),
        ml_o=jax.nn.sigmoid(mo.astype(f32)).reshape(B, L, MLSTM_HEADS, HEAD_DIM),
        ml_i=mi.astype(f32),
        ml_lf=jax.nn.log_sigmoid(mf.astype(f32)),
        fox_q=heads(fq, FOX_HEADS),
        fox_k=heads(fk, FOX_HEADS),
        fox_v=heads(fv, FOX_HEADS),
        fox_lf=jax.nn.log_sigmoid(ff.astype(f32)),
        merge=jax.nn.sigmoid(gm.astype(f32)).reshape(B, L, N_BRANCH, D_MODEL))


def _nsa_compress(k, w):
    B, L, _ = k.shape
    n_cb = L // NSA_BLOCK
    blocks = k[:, :n_cb * NSA_BLOCK].reshape(B, n_cb, NSA_BLOCK, HEAD_DIM)
    return jnp.einsum('bjpd,pde->bje', blocks, w)


def _nsa_cmp_attend(q, kc, vc, pos_q, slopes):
    n_cb = kc.shape[1]
    blk_end = jnp.arange(n_cb) * NSA_BLOCK + NSA_BLOCK - 1
    dist = pos_q[:, None] - blk_end[None, :]
    s = jnp.einsum('bqhd,bjd->bhqj', q, kc) * ATTN_SCALE - slopes[:, None, None] * dist.astype(jnp.float32)
    p = _masked_softmax(s, dist >= 0)
    o = jnp.einsum('bhqj,bjd->bqhd', p.astype(vc.dtype), vc)
    return o, jnp.sum(p, axis=1)


def _nsa_select_blocks(imp, pos_q, n_sb):
    n_cb = imp.shape[-1]
    imp = jnp.pad(imp, ((0, 0), (0, 0), (0, n_sb - n_cb)))
    j = jnp.arange(n_sb)[None, :]
    cur = (pos_q // NSA_BLOCK)[:, None]
    score = jnp.where(j > cur, NEG_INF, jnp.where((j == cur) | (j == 0), FORCED_SCORE, imp))
    vals, idx = lax.top_k(score, min(NSA_TOPK, n_sb))
    return idx, vals > 0.5 * NEG_INF


def _nsa_sel_attend(q, kb, vb, idx, valid, pos_q, slopes):
    B, Q = q.shape[:2]
    bi = jnp.arange(B)[:, None, None]
    kg = kb[bi, idx].reshape(B, Q, -1, HEAD_DIM)
    vg = vb[bi, idx].reshape(B, Q, -1, HEAD_DIM)
    pos_k = (idx[..., None] * NSA_BLOCK + jnp.arange(NSA_BLOCK)).reshape(B, Q, -1)
    ok = jnp.repeat(valid, NSA_BLOCK, axis=-1) & (pos_k <= pos_q[None, :, None])
    dist = (pos_q[None, :, None] - pos_k).astype(jnp.float32)
    s = jnp.einsum('bqhd,bqkd->bhqk', q, kg) * ATTN_SCALE - slopes[None, :, None, None] * dist[:, None]
    p = _masked_softmax(s, ok[:, None])
    return jnp.einsum('bhqk,bqkd->bqhd', p.astype(vg.dtype), vg)


def _nsa_cmp_sel(q, rows, pos_q, w_ck, w_cv, slopes):
    B, Q = q.shape[:2]
    L = rows.shape[1]
    kc = _nsa_compress(rows[:, :, 0], w_ck)
    vc = _nsa_compress(rows[:, :, 1], w_cv)
    o_cmp, imp = _nsa_cmp_attend(q, kc, vc, pos_q, slopes)
    n_sb = -(-L // NSA_BLOCK)
    idx, valid = _nsa_select_blocks(imp, pos_q, n_sb)
    pad = ((0, 0), (0, n_sb * NSA_BLOCK - L), (0, 0))
    kb = jnp.pad(rows[:, :, 2], pad).reshape(B, n_sb, NSA_BLOCK, HEAD_DIM)
    vb = jnp.pad(rows[:, :, 3], pad).reshape(B, n_sb, NSA_BLOCK, HEAD_DIM)
    qb = Q_BLOCK if Q % Q_BLOCK == 0 else Q
    nqb = Q // qb

    def block(args):
        qc, ic, okc, pc = args
        return _nsa_sel_attend(qc, kb, vb, ic, okc, pc, slopes)

    xs = (q.reshape(B, nqb, qb, NSA_HEADS, HEAD_DIM).swapaxes(0, 1),
          idx.reshape(B, nqb, qb, -1).swapaxes(0, 1),
          valid.reshape(B, nqb, qb, -1).swapaxes(0, 1),
          pos_q.reshape(nqb, qb))
    o_sel = lax.map(block, xs).swapaxes(0, 1).reshape(B, Q, NSA_HEADS, HEAD_DIM)
    return o_cmp, o_sel


def _nsa_win_attend(q, k, v, pos_q, pos_k, slopes):
    dist = pos_q[:, None] - pos_k[None, :]
    ok = (dist >= 0) & (dist < NSA_WINDOW) & (pos_k[None, :] >= 0)
    s = jnp.einsum('bqhd,bkd->bhqk', q, k) * ATTN_SCALE - slopes[:, None, None] * dist.astype(jnp.float32)
    p = _masked_softmax(s, ok)
    return jnp.einsum('bhqk,bkd->bqhd', p.astype(v.dtype), v)


def _nsa_win_prompt(q, k, v, slopes):
    B, S = q.shape[:2]
    nqb = S // Q_BLOCK
    nb = NSA_WINDOW // Q_BLOCK
    pad = ((0, 0), (NSA_WINDOW, 0), (0, 0))
    kp = jnp.pad(k, pad).reshape(B, nqb + nb, Q_BLOCK, HEAD_DIM)
    vp = jnp.pad(v, pad).reshape(B, nqb + nb, Q_BLOCK, HEAD_DIM)
    band = jnp.arange(nqb)[:, None] + jnp.arange(nb + 1)[None, :]
    kband = kp[:, band].reshape(B, nqb, (nb + 1) * Q_BLOCK, HEAD_DIM)
    vband = vp[:, band].reshape(B, nqb, (nb + 1) * Q_BLOCK, HEAD_DIM)
    qblk = q.reshape(B, nqb, Q_BLOCK, NSA_HEADS, HEAD_DIM)
    pos_q = jnp.arange(S).reshape(nqb, Q_BLOCK)
    pos_k = (jnp.arange(nqb) * Q_BLOCK - NSA_WINDOW)[:, None] + jnp.arange((nb + 1) * Q_BLOCK)[None, :]
    o = jax.vmap(_nsa_win_attend, in_axes=(1, 1, 1, 0, 0, None), out_axes=1)(qblk, kband, vband, pos_q, pos_k, slopes)
    return o.reshape(B, S, NSA_HEADS, HEAD_DIM)


def _nsa_combine(g, o_cmp, o_sel, o_win):
    return jnp.einsum('blhc,cblhd->blhd', g, jnp.stack([o_cmp, o_sel, o_win]))


def _mlstm_chunk(carry, xs):
    c, n, m = carry
    q, k, v, ig, lf = xs
    L = q.shape[1]
    b = jnp.cumsum(lf, axis=1)
    causal = jnp.tril(jnp.ones((L, L), bool))[None, :, :, None]
    dmat = jnp.where(causal, b[:, :, None, :] - b[:, None, :, :] + ig[:, None, :, :], NEG_INF)
    a = b + m[:, None, :]
    m_t = jnp.maximum(a, jnp.max(dmat, axis=2))
    wq = jnp.exp(dmat - m_t[:, :, None, :]) * jnp.einsum('bthd,bshd->btsh', q, k)
    inter = jnp.exp(a - m_t)
    num = inter[..., None] * jnp.einsum('bthd,bhde->bthe', q, c) + jnp.einsum('btsh,bshe->bthe', wq, v)
    den = inter * jnp.einsum('bthd,bhd->bth', q, n) + jnp.sum(wq, axis=2)
    h = num / jnp.maximum(jnp.abs(den), jnp.exp(-m_t))[..., None]
    bl = b[:, -1]
    g = bl[:, None, :] - b + ig
    m_new = jnp.maximum(bl + m, jnp.max(g, axis=1))
    ws = jnp.exp(g - m_new[:, None, :])
    decay = jnp.exp(bl + m - m_new)
    c_new = decay[..., None, None] * c + jnp.einsum('bsh,bshd,bshe->bhde', ws, k, v)
    n_new = decay[..., None] * n + jnp.einsum('bsh,bshd->bhd', ws, k)
    return (c_new, n_new, m_new), h


def _mlstm_prompt(q, k, v, ig, lf):
    B, S, H, _ = q.shape
    ch = MLSTM_CHUNK if S % MLSTM_CHUNK == 0 else S
    nc = S // ch
    to_chunks = lambda t: t.reshape((B, nc, ch) + t.shape[2:]).swapaxes(0, 1)
    f32 = jnp.float32
    init = (jnp.zeros((B, H, HEAD_DIM, HEAD_DIM), f32), jnp.zeros((B, H, HEAD_DIM), f32), jnp.zeros((B, H), f32))
    state, h = lax.scan(_mlstm_chunk, init, tuple(to_chunks(t) for t in (q, k, v, ig, lf)))
    return h.swapaxes(0, 1).reshape(B, S, H, HEAD_DIM), state


def _mlstm_readout(h, o_gate, norm_w):
    h = o_gate * h
    mu = jnp.mean(h, axis=-1, keepdims=True)
    var = jnp.mean(jnp.square(h - mu), axis=-1, keepdims=True)
    return (h - mu) * lax.rsqrt(var + LN_EPS) * norm_w.astype(jnp.float32).reshape(MLSTM_HEADS, HEAD_DIM)


def _fox_attend(q, k, v, fq, fk, pos_q, pos_k):
    s = jnp.einsum('bqhd,bkhd->bhqk', q, k) * ATTN_SCALE
    s = s + (fq.transpose(0, 2, 1)[..., None] - fk.transpose(0, 2, 1)[:, :, None, :])
    p = _masked_softmax(s, pos_k[None, :] <= pos_q[:, None])
    return jnp.einsum('bhqk,bkhd->bqhd', p.astype(v.dtype), v)


def _fox_prompt(q, k, v, F):
    B, S = q.shape[:2]
    qb = Q_BLOCK if S % Q_BLOCK == 0 else S
    nqb = S // qb
    pos = jnp.arange(S)

    def block(args):
        qc, fc, pc = args
        return _fox_attend(qc, k, v, fc, F, pc, pos)

    xs = (q.reshape(B, nqb, qb, FOX_HEADS, HEAD_DIM).swapaxes(0, 1),
          F.reshape(B, nqb, qb, FOX_HEADS).swapaxes(0, 1),
          pos.reshape(nqb, qb))
    return lax.map(block, xs).swapaxes(0, 1).reshape(B, S, FOX_HEADS, HEAD_DIM)


def _merge(gates, o_nsa, o_ml, o_fox, w_branch, w_out, dtype):
    B, L = o_nsa.shape[:2]
    br = jnp.stack([o.reshape(B, L, BRANCH_WIDTH).astype(dtype) for o in (o_nsa, o_ml, o_fox)], axis=2)
    proj = jnp.einsum('blmc,mcd->blmd', br, w_branch)
    y = jnp.einsum('blmd,blmd->bld', gates, proj).astype(dtype)
    return y @ w_out


def _moe(x, w_group, b_group, w_expert, b_expert, w_gate, w_up, w_down):
    B, L, D = x.shape
    t = x.reshape(B * L, D)
    pg = jax.nn.softmax((t @ w_group + b_group).astype(jnp.float32), axis=-1)
    g_val, g_idx = lax.top_k(pg, 1)
    le = (t @ w_expert + b_expert).astype(jnp.float32).reshape(-1, N_GROUPS, EXPERTS_PER_GROUP)
    le = jnp.take_along_axis(le, g_idx[:, :, None], axis=1)[:, 0]
    e_val, e_idx = lax.top_k(jax.nn.softmax(le, axis=-1), TOP_K_IN_GROUP)
    e_w = g_val * e_val / jnp.sum(e_val, axis=-1, keepdims=True)
    gate = jnp.sum(jax.nn.one_hot(g_idx * EXPERTS_PER_GROUP + e_idx, N_EXPERTS, dtype=jnp.float32) * e_w[..., None], axis=1)
    h = jax.nn.silu(jnp.einsum('td,edf->tef', t, w_gate)) * jnp.einsum('td,edf->tef', t, w_up)
    h = h * gate[:, :, None].astype(h.dtype)
    return jnp.einsum('tef,efd->td', h, w_down).reshape(B, L, D)


def _mixers_prompt(x, w_in, b_in, w_ck, w_cv, ml_norm_w, w_branch, w_out, slopes):
    B, S, _ = x.shape
    pr = _project(x, w_in, b_in)
    pos = jnp.arange(S)
    o_nsa = _nsa_prompt(pr['nsa_q'].reshape(B, S, -1), pr['nsa_rows'].reshape(B, S, -1),
                        pr['nsa_win'].reshape(B, S, -1), jnp.pad(pr['nsa_g_raw'], ((0, 0), (0, 0), (0, 116))),
                        w_ck, w_cv).reshape(B, S, NSA_HEADS, HEAD_DIM)
    h, (c, n, m) = _mlstm_prompt(pr['ml_q'], pr['ml_k'], pr['ml_v'], pr['ml_i'], pr['ml_lf'])
    o_ml = _mlstm_readout(h, pr['ml_o'], ml_norm_w)
    flat = lambda t: t.reshape(B, S, -1)
    o_fox = _fox_prompt_attn(flat(pr['fox_q']), flat(pr['fox_k']), flat(pr['fox_v']),
                             jnp.cumsum(pr['fox_lf'], axis=1)).reshape(B, S, FOX_HEADS, HEAD_DIM)
    out = _merge(pr['merge'], o_nsa, o_ml, o_fox, w_branch, w_out, x.dtype)
    w_keep = min(NSA_WINDOW, S)
    new = (pr['nsa_rows'], pr['nsa_win'][:, S - w_keep:], jnp.stack([pr['fox_k'], pr['fox_v']], axis=2),
           pr['fox_lf'], c, n, m)
    return out, new


def _mixers_sample(x, c_nsa, c_win, c_fox_kv, c_fox_lf, s_c, s_n, s_m, page_table,
                   w_in, b_in, w_ck, w_cv, ml_norm_w, w_branch, w_out, slopes):
    DB, T, _ = x.shape
    past = page_table.shape[1] * PAGE_SIZE
    f32 = jnp.float32
    pr = _project(x, w_in, b_in)
    pos_q = past + jnp.arange(T)
    rows_past = _gather_pages(c_nsa.reshape(-1, PAGE_SIZE, NSA_ROWS * HEAD_DIM), page_table)
    rows_past = rows_past.reshape(DB, past, NSA_ROWS, HEAD_DIM)
    rows_all = jnp.concatenate([rows_past, pr['nsa_rows'].astype(rows_past.dtype)], axis=1)
    o_cmp, o_sel = _nsa_cmp_sel(pr['nsa_q'], rows_all, pos_q, w_ck, w_cv, slopes)
    wb = c_win.shape[1]
    win_all = jnp.concatenate([c_win, pr['nsa_win'].astype(c_win.dtype)], axis=1)
    pos_k = past - wb + jnp.arange(wb + T)
    o_win = _nsa_win_attend(pr['nsa_q'], win_all[:, :, 0], win_all[:, :, 1], pos_q, pos_k, slopes)
    o_nsa = _nsa_combine(pr['nsa_g'], o_cmp, o_sel, o_win)
    (c, n, m), h = _mlstm_chunk((s_c.astype(f32), s_n.astype(f32), s_m.astype(f32)),
                                (pr['ml_q'], pr['ml_k'], pr['ml_v'], pr['ml_i'], pr['ml_lf']))
    o_ml = _mlstm_readout(h, pr['ml_o'], ml_norm_w)
    kv_past = _gather_pages(c_fox_kv.reshape(-1, PAGE_SIZE, 2 * FOX_HEADS * HEAD_DIM), page_table)
    kv_past = kv_past.reshape(DB, past, 2, FOX_HEADS, HEAD_DIM)
    k_all = jnp.concatenate([kv_past[:, :, 0], pr['fox_k'].astype(kv_past.dtype)], axis=1)
    v_all = jnp.concatenate([kv_past[:, :, 1], pr['fox_v'].astype(kv_past.dtype)], axis=1)
    lf_all = jnp.concatenate([c_fox_lf[page_table].reshape(DB, past, FOX_HEADS).astype(f32), pr['fox_lf']], axis=1)
    F = jnp.cumsum(lf_all, axis=1)
    o_fox = _fox_attend(pr['fox_q'], k_all, v_all, F[:, past:], F, pos_q, jnp.arange(past + T))
    out = _merge(pr['merge'], o_nsa, o_ml, o_fox, w_branch, w_out, x.dtype)
    new = (pr['nsa_rows'], win_all[:, T:], jnp.stack([pr['fox_k'], pr['fox_v']], axis=2),
           pr['fox_lf'], c, n, m)
    return out, new


def _stack_layers(states):
    return tuple(jnp.stack(list(a)) for a in zip(*states))


def kernel(x_prompt, x_sample, cache_nsa, cache_nsa_win, cache_fox_kv, cache_fox_logf,
           state_mlstm_c, state_mlstm_n, state_mlstm_m, page_table,
           ln_in_w, ln_in_b, w_in, b_in, nsa_w_ck, nsa_w_cv, mlstm_norm_w, w_branch, w_out,
           ln1_w, ln1_b, moe_w_group, moe_b_group, moe_w_expert, moe_b_expert,
           moe_w_gate, moe_w_up, moe_w_down, ln2_w, ln2_b):
    f32, bf16 = jnp.float32, jnp.bfloat16
    alpha = (2.0 * DEPTH) ** 0.25
    B, S, D = x_prompt.shape
    DB, T, _ = x_sample.shape
    TP, TS = B * S, DB * T
    HD = HEAD_DIM
    x = jnp.concatenate([x_prompt.reshape(TP, D), x_sample.reshape(TS, D)], axis=0)
    x = _layer_norm_rows(x, ln_in_w, ln_in_b)
    cat = lambda p, s: jnp.concatenate([p.reshape(TP, -1), s.reshape(TS, -1)], axis=0)
    cols = lambda a, c0, w: a[..., c0:c0 + w]
    zero_state = (jnp.zeros((B, MLSTM_HEADS, HD, HD), f32), jnp.zeros((B, MLSTM_HEADS, HD), f32),
                  jnp.zeros((B, MLSTM_HEADS), f32))
    new_p, new_s = [], []
    for l in range(DEPTH):
        z = _in_projection(x, w_in[l], b_in[l])
        zp = z[:TP].reshape(B, S, Z_WIDTH)
        zs = z[TP:].reshape(DB, T, Z_WIDTH)
        small_p, small_s = cols(zp, Z_SMALL, 128), cols(zs, Z_SMALL, 128)

        o_nsa_p = _nsa_prompt(cols(zp, Z_NQ, 256), cols(zp, Z_ROWS, 256), cols(zp, Z_WIN, 128), small_p,
                              nsa_w_ck[l], nsa_w_cv[l])
        o_nsa_s = _nsa_decode(cols(zs, Z_NQ, 256), cols(zs, Z_ROWS, 256), cols(zs, Z_WIN, 128), small_s,
                              cache_nsa[l], cache_nsa_win[l], page_table, nsa_w_ck[l], nsa_w_cv[l])

        def mlstm(zz, small, state, L, nb, dt):
            return _mlstm(cols(zz, Z_MQ, 256), cols(zz, Z_MK, 256) * ATTN_SCALE, cols(zz, Z_MV, 256),
                          cols(zz, Z_MO, 256), cols(small, SMALL_MI, MLSTM_HEADS), cols(small, SMALL_MF, MLSTM_HEADS),
                          mlstm_norm_w[l], *state, L, nb, dt)

        o_ml_p, st_ml_p = mlstm(zp, small_p, zero_state, 128, B, bf16)
        o_ml_s, st_ml_s = mlstm(zs, small_s, (state_mlstm_c[l], state_mlstm_n[l], state_mlstm_m[l]), T, 4, f32)

        lf_p = jax.nn.log_sigmoid(cols(small_p, SMALL_FF, FOX_HEADS))
        lf_s = jax.nn.log_sigmoid(cols(small_s, SMALL_FF, FOX_HEADS))
        o_fox_p = _fox_prompt_attn(cols(zp, Z_FQ, 256), cols(zp, Z_FK, 256), cols(zp, Z_FV, 256),
                                   jnp.cumsum(lf_p, axis=1))
        o_fox_s = _fox_decode(cols(zs, Z_FQ, 256), cols(zs, Z_FK, 256), cols(zs, Z_FV, 256), lf_s,
                              cache_fox_kv[l], cache_fox_logf[l], page_table)

        x = _merge_ln(x, z, cat(o_nsa_p, o_nsa_s), cat(o_ml_p, o_ml_s), cat(o_fox_p, o_fox_s),
                      w_branch[l], w_out[l], ln1_w[l], ln1_b[l], alpha)
        x = _moe_ln(x, moe_w_group[l], moe_b_group[l], moe_w_expert[l], moe_b_expert[l],
                    moe_w_gate[l], moe_w_up[l], moe_w_down[l], ln2_w[l], ln2_b[l], alpha)

        w_keep = min(NSA_WINDOW, S)
        new_p.append((cols(zp, Z_ROWS, 256).reshape(B, S, NSA_ROWS, HD),
                      cols(zp, Z_WIN, 128)[:, S - w_keep:].reshape(B, w_keep, 2, HD),
                      cols(zp, Z_FK, 512).reshape(B, S, 2, FOX_HEADS, HD), lf_p) + tuple(st_ml_p))
        win_new = cols(zs, Z_WIN, 128).reshape(DB, T, 2, HD).astype(cache_nsa_win.dtype)
        new_s.append((cols(zs, Z_ROWS, 256).reshape(DB, T, NSA_ROWS, HD),
                      jnp.concatenate([cache_nsa_win[l][:, T:], win_new], axis=1),
                      cols(zs, Z_FK, 512).reshape(DB, T, 2, FOX_HEADS, HD), lf_s) + tuple(st_ml_s))
    (p_nsa_rows, p_nsa_win, p_fox_kv, p_fox_logf, p_mlstm_c, p_mlstm_n, p_mlstm_m) = _stack_layers(new_p)
    (s_nsa_rows, s_nsa_win, s_fox_kv, s_fox_logf, s_mlstm_c, s_mlstm_n, s_mlstm_m) = _stack_layers(new_s)
    return (x[:TP].reshape(B, S, D), x[TP:].reshape(DB, T, D),
            p_nsa_rows, p_nsa_win, p_fox_kv, p_fox_logf, p_mlstm_c, p_mlstm_n, p_mlstm_m,
            s_nsa_rows, s_nsa_win, s_fox_kv, s_fox_logf, s_mlstm_c, s_mlstm_n, s_mlstm_m)
```

```python
import functools

import jax
import jax.numpy as jnp
import numpy as np
from jax import lax
from jax.experimental import pallas as pl
from jax.experimental.pallas import tpu as pltpu

D_MODEL = 1024
BATCH = 2
SEQ = 8192
DEPTH = 2
DEC_BATCH = 128
DEC_SEQ = 8
PAST_LEN = 2048
PAGE_SIZE = 128

HEAD_DIM = 64
NSA_HEADS = 4
NSA_BLOCK = 64
NSA_TOPK = 16
NSA_WINDOW = 512
NSA_ROWS = 4
MLSTM_HEADS = 4
MLSTM_CHUNK = 64
FOX_HEADS = 4
Q_BLOCK = 128
N_BRANCH = 3
BRANCH_WIDTH = NSA_HEADS * HEAD_DIM
N_GROUPS = 4
EXPERTS_PER_GROUP = 4
N_EXPERTS = N_GROUPS * EXPERTS_PER_GROUP
TOP_K_IN_GROUP = 2
D_EXPERT = 256
LN_EPS = 1e-5
NEG_INF = -1e30
TINY = 1e-30
FORCED_SCORE = 1e9
ATTN_SCALE = HEAD_DIM ** -0.5

IN_SPLITS = (
    NSA_HEADS * HEAD_DIM,
    6 * HEAD_DIM,
    NSA_HEADS * 3,
    MLSTM_HEADS * HEAD_DIM,
    MLSTM_HEADS * HEAD_DIM,
    MLSTM_HEADS * HEAD_DIM,
    MLSTM_HEADS * HEAD_DIM,
    MLSTM_HEADS,
    MLSTM_HEADS,
    FOX_HEADS * HEAD_DIM,
    FOX_HEADS * HEAD_DIM,
    FOX_HEADS * HEAD_DIM,
    FOX_HEADS,
    N_BRANCH * D_MODEL,
)


def _linear_kernel(x_ref, w_ref, b_ref, o_ref):
    x = x_ref[...].astype(jnp.bfloat16)
    o_ref[...] = jnp.dot(x, w_ref[...], preferred_element_type=jnp.float32) + b_ref[...]


def _pallas_linear(x, w, b, tm=512, tn=512):
    T, K = x.shape
    N = w.shape[1]
    n_pad = -(-N // tn) * tn
    wp = jnp.pad(w.astype(jnp.bfloat16), ((0, 0), (0, n_pad - N)))
    bp = jnp.pad(b.astype(jnp.float32), (0, n_pad - N)).reshape(1, n_pad)
    out = pl.pallas_call(
        _linear_kernel,
        grid=(T // tm, n_pad // tn),
        in_specs=[pl.BlockSpec((tm, K), lambda i, j: (i, 0)),
                  pl.BlockSpec((K, tn), lambda i, j: (0, j)),
                  pl.BlockSpec((1, tn), lambda i, j: (0, j))],
        out_specs=pl.BlockSpec((tm, tn), lambda i, j: (i, j)),
        out_shape=jax.ShapeDtypeStruct((T, n_pad), jnp.float32),
        name="linear",
    )(x, wp, bp)
    return out[:, :N]


VMEM_LIMIT_BYTES = 48 * 1024 * 1024
NSA_SLOPES = tuple(2.0 ** (-8.0 * (h + 1) / NSA_HEADS) for h in range(NSA_HEADS))
_NT = (((1,), (1,)), ((), ()))
_HI = lax.Precision.HIGHEST


def _sigmoid(x):
    return 1.0 / (1.0 + jnp.exp(-x))


def _tile_rows(x, n):
    return jnp.concatenate([x] * n, axis=0)


def _compress_kernel(x_ref, w_ref, o_ref, acc_ref):
    k = pl.program_id(1)

    @pl.when(k == 0)
    def _():
        acc_ref[...] = jnp.zeros_like(acc_ref)

    acc_ref[...] += jnp.dot(x_ref[...], w_ref[...], preferred_element_type=jnp.float32, precision=_HI)

    @pl.when(k == pl.num_programs(1) - 1)
    def _():
        o_ref[...] = acc_ref[...]


def _compress_weights(w_ck, w_cv):
    z = jnp.zeros_like(w_ck)
    wk = jnp.stack([w_ck, z, z, z], axis=1)
    wv = jnp.stack([z, w_cv, z, z], axis=1)
    return jnp.concatenate([wk, wv], axis=-1).reshape(NSA_BLOCK * NSA_ROWS * HEAD_DIM, 2 * HEAD_DIM)


def _nsa_compress_blocks(blocks, w_big, tm=256, tk=2048):
    n, kdim = blocks.shape
    tm = min(tm, n)
    return pl.pallas_call(
        _compress_kernel,
        grid=(n // tm, kdim // tk),
        in_specs=[pl.BlockSpec((tm, tk), lambda i, k: (i, k)),
                  pl.BlockSpec((tk, 2 * HEAD_DIM), lambda i, k: (k, 0))],
        out_specs=pl.BlockSpec((tm, 2 * HEAD_DIM), lambda i, k: (i, 0)),
        out_shape=jax.ShapeDtypeStruct((n, 2 * HEAD_DIM), jnp.float32),
        scratch_shapes=[pltpu.VMEM((tm, 2 * HEAD_DIM), jnp.float32)],
        name="nsa_compress",
    )(blocks, w_big)


def _softmax_step(carry, s, v):
    m, l, acc = carry
    m_new = jnp.maximum(m, jnp.max(s, axis=-1, keepdims=True))
    p = jnp.exp(s - m_new)
    a = jnp.exp(m - m_new)
    l = a * l + jnp.sum(p, axis=-1, keepdims=True)
    acc = a * acc + jnp.dot(p.astype(jnp.bfloat16), v, preferred_element_type=jnp.float32)
    return m_new, l, acc


def _select_blocks(imp, cur, n_pick):
    q, n_sb = imp.shape
    jq = lax.broadcasted_iota(jnp.int32, (q, n_sb), 1)
    jf = jq.astype(jnp.float32)
    score = jnp.where(jq > cur, NEG_INF, jnp.where((jq == cur) | (jq == 0), FORCED_SCORE, imp))

    def pick(_, carry):
        work, sel = carry
        mx = jnp.max(work, axis=-1, keepdims=True)
        first = jnp.min(jnp.where(work == mx, jf, float(n_sb)), axis=-1, keepdims=True)
        hit = jf == first
        sel = jnp.where(hit & (mx > 0.5 * NEG_INF), 1.0, sel)
        work = jnp.where(hit, -jnp.inf, work)
        return work, sel

    _, sel = lax.fori_loop(0, n_pick, pick, (score, jnp.zeros((q, n_sb), jnp.float32)))
    return sel


def _nsa_prompt_kernel(qs_ref, qf_ref, kc_ref, vc_ref, ksel_ref, vsel_ref, kwin_ref, vwin_ref, g_ref,
                       o_ref, *, tq, tk, tw):
    i = pl.program_id(1)
    f32, bf16 = jnp.float32, jnp.bfloat16
    H = NSA_HEADS
    R = H * tq
    qs = qs_ref[0].reshape(R, HEAD_DIM)
    qf = qf_ref[0].reshape(R, HEAD_DIM)
    row = lax.broadcasted_iota(jnp.int32, (R, 1), 0)
    head = row // tq
    slope = jnp.where(head == 0, NSA_SLOPES[0], jnp.where(head == 1, NSA_SLOPES[1],
                      jnp.where(head == 2, NSA_SLOPES[2], NSA_SLOPES[3]))).astype(f32)
    posq = i * tq + (row - head * tq)

    n_cb = kc_ref.shape[1]
    sc = lax.dot_general(qf, kc_ref[0], _NT, precision=_HI, preferred_element_type=f32) * ATTN_SCALE
    jb = lax.broadcasted_iota(jnp.int32, (R, n_cb), 1)
    distc = posq - (jb * NSA_BLOCK + NSA_BLOCK - 1)
    okc = distc >= 0
    sc = jnp.where(okc, sc - slope * distc.astype(f32), NEG_INF)
    pc = jnp.exp(sc - jnp.max(sc, axis=-1, keepdims=True)) * okc.astype(f32)
    pc = pc / jnp.maximum(jnp.sum(pc, axis=-1, keepdims=True), TINY)
    o_cmp = jnp.dot(pc.astype(bf16), vc_ref[0].astype(bf16), preferred_element_type=f32)
    imp = pc[0:tq] + pc[tq:2 * tq] + pc[2 * tq:3 * tq] + pc[3 * tq:4 * tq]

    pq = i * tq + lax.broadcasted_iota(jnp.int32, (tq, 1), 0)
    msel = _select_blocks(imp, pq // NSA_BLOCK, NSA_TOPK).astype(bf16)

    rowpos = i * tq + lax.broadcasted_iota(jnp.int32, (tq, 1), 0)
    init = (jnp.full((R, 1), NEG_INF, f32), jnp.zeros((R, 1), f32), jnp.zeros((R, HEAD_DIM), f32))

    def sel_body(j, carry):
        k0 = pl.multiple_of(j * tk, tk)
        k = ksel_ref[0, pl.ds(k0, tk), :]
        v = vsel_ref[0, pl.ds(k0, tk), :]
        s = lax.dot_general(qs, k, _NT, preferred_element_type=f32)
        eb = (lax.broadcasted_iota(jnp.int32, (n_cb, tk), 0)
              == j * (tk // NSA_BLOCK) + lax.broadcasted_iota(jnp.int32, (n_cb, tk), 1) // NSA_BLOCK)
        mexp = jnp.dot(msel, eb.astype(bf16), preferred_element_type=f32)
        d = rowpos - (k0 + lax.broadcasted_iota(jnp.int32, (tq, tk), 1))
        ok = (mexp > 0.5) & (d >= 0)
        s = jnp.where(_tile_rows(ok, H), s - slope * _tile_rows(d.astype(f32), H), NEG_INF)
        return _softmax_step(carry, s, v)

    n_sel = (i * tq + tq - 1) // tk + 1
    _, l_sel, a_sel = lax.fori_loop(0, n_sel, sel_body, init)
    o_sel = a_sel / l_sel

    def win_body(j, carry):
        k0 = pl.multiple_of(j * tw, tw)
        k = kwin_ref[0, pl.ds(k0, tw), :]
        v = vwin_ref[0, pl.ds(k0, tw), :]
        s = lax.dot_general(qs, k, _NT, preferred_element_type=f32)
        d = rowpos - (k0 + lax.broadcasted_iota(jnp.int32, (tq, tw), 1))
        ok = (d >= 0) & (d < NSA_WINDOW)
        s = jnp.where(_tile_rows(ok, H), s - slope * _tile_rows(d.astype(f32), H), NEG_INF)
        return _softmax_step(carry, s, v)

    lo = jnp.maximum((i * tq - NSA_WINDOW + 1) // tw, 0)
    hi = (i * tq + tq - 1) // tw + 1
    _, l_win, a_win = lax.fori_loop(lo, hi, win_body, init)
    o_win = a_win / l_win

    g = _sigmoid(g_ref[...])
    gate = lambda c: jnp.concatenate([g[:, 3 * h + c:3 * h + c + 1] for h in range(H)], axis=0)
    o = gate(0) * o_cmp + gate(1) * o_sel + gate(2) * o_win
    o_ref[0] = o.reshape(H, tq, HEAD_DIM)


def _heads_major(x, dtype):
    B, S, _ = x.shape
    return x.reshape(B, S, -1, HEAD_DIM).transpose(0, 2, 1, 3).astype(dtype)


def _nsa_prompt(nq, rows, win, gates, w_ck, w_cv, tq=128, tk=256, tw=128):
    B, S, _ = nq.shape
    bf16 = jnp.bfloat16
    kcvc = _nsa_compress_blocks(rows.reshape(B * S // NSA_BLOCK, -1), _compress_weights(w_ck, w_cv))
    kcvc = kcvc.reshape(B, S // NSA_BLOCK, 2 * HEAD_DIM)
    kc, vc = kcvc[..., :HEAD_DIM], kcvc[..., HEAD_DIM:]
    qs = _heads_major(nq * ATTN_SCALE, bf16)
    qf = _heads_major(nq, jnp.float32)
    col = lambda a, c: a[..., c * HEAD_DIM:(c + 1) * HEAD_DIM].astype(bf16)
    full = lambda n: pl.BlockSpec((1, n, HEAD_DIM), lambda b, i: (b, 0, 0))
    qspec = pl.BlockSpec((1, NSA_HEADS, tq, HEAD_DIM), lambda b, i: (b, 0, i, 0))
    o = pl.pallas_call(
        functools.partial(_nsa_prompt_kernel, tq=tq, tk=tk, tw=tw),
        grid=(B, S // tq),
        in_specs=[qspec, qspec, full(S // NSA_BLOCK), full(S // NSA_BLOCK), full(S), full(S), full(S), full(S),
                  pl.BlockSpec((None, tq, 128), lambda b, i: (b, i, 0))],
        out_specs=qspec,
        out_shape=jax.ShapeDtypeStruct((B, NSA_HEADS, S, HEAD_DIM), jnp.float32),
        compiler_params=pltpu.CompilerParams(dimension_semantics=("parallel", "arbitrary"),
                                             vmem_limit_bytes=VMEM_LIMIT_BYTES),
        name="nsa_prompt",
    )(qs, qf, kc, vc, col(rows, 2), col(rows, 3), col(win, 0), col(win, 1), gates)
    return o.transpose(0, 2, 1, 3).reshape(B, S, BRANCH_WIDTH)


def _fox_prompt_kernel(q_ref, k_ref, v_ref, fq_ref, fk_ref, o_ref, *, t):
    i = pl.program_id(1)
    f32, bf16 = jnp.float32, jnp.bfloat16
    lane = lax.broadcasted_iota(jnp.int32, (1, 2 * HEAD_DIM), 1)
    low = lane < HEAD_DIM
    rc = lax.broadcasted_iota(jnp.int32, (t, t), 0) - lax.broadcasted_iota(jnp.int32, (t, t), 1)
    for hp in range(FOX_HEADS // 2):
        cols = slice(hp * 2 * HEAD_DIM, (hp + 1) * 2 * HEAD_DIM)
        q2 = q_ref[0, :, cols]
        qa = jnp.where(low, q2, jnp.zeros_like(q2))
        qb = jnp.where(low, jnp.zeros_like(q2), q2)
        fqa = fq_ref[0, :, 2 * hp:2 * hp + 1]
        fqb = fq_ref[0, :, 2 * hp + 1:2 * hp + 2]

        def body(j, carry, diagonal):
            ma, la, mb, lb, acc = carry
            k0 = pl.multiple_of(j * t, t)
            k2 = k_ref[0, pl.ds(k0, t), cols]
            v2 = v_ref[0, pl.ds(k0, t), cols]
            fk = fk_ref[0, j]
            sa = lax.dot_general(qa, k2, _NT, preferred_element_type=f32) + (fqa - fk[2 * hp:2 * hp + 1])
            sb = lax.dot_general(qb, k2, _NT, preferred_element_type=f32) + (fqb - fk[2 * hp + 1:2 * hp + 2])
            if diagonal:
                sa = jnp.where(rc >= 0, sa, NEG_INF)
                sb = jnp.where(rc >= 0, sb, NEG_INF)
            ma_n = jnp.maximum(ma, jnp.max(sa, axis=-1, keepdims=True))
            mb_n = jnp.maximum(mb, jnp.max(sb, axis=-1, keepdims=True))
            pa = jnp.exp(sa - ma_n)
            pb = jnp.exp(sb - mb_n)
            aa = jnp.exp(ma - ma_n)
            ab = jnp.exp(mb - mb_n)
            la = aa * la + jnp.sum(pa, axis=-1, keepdims=True)
            lb = ab * lb + jnp.sum(pb, axis=-1, keepdims=True)
            pva = jnp.dot(pa.astype(bf16), v2, preferred_element_type=f32)
            pvb = jnp.dot(pb.astype(bf16), v2, preferred_element_type=f32)
            acc = jnp.where(low, aa * acc + pva, ab * acc + pvb)
            return ma_n, la, mb_n, lb, acc

        neg = jnp.full((t, 1), NEG_INF, f32)
        zero = jnp.zeros((t, 1), f32)
        carry = (neg, zero, neg, zero, jnp.zeros((t, 2 * HEAD_DIM), f32))
        carry = lax.fori_loop(0, i, functools.partial(body, diagonal=False), carry)
        _, la, _, lb, acc = body(i, carry, True)
        o_ref[0, :, cols] = acc / jnp.where(low, la, lb)


def _fox_prompt_attn(q, k, v, F, t=256):
    B, S, W = q.shape
    bf16 = jnp.bfloat16
    fk = jnp.pad(F.transpose(0, 2, 1), ((0, 0), (0, 8 - FOX_HEADS), (0, 0)))
    fk = fk.reshape(B, 8, S // t, t).transpose(0, 2, 1, 3)
    full = pl.BlockSpec((1, S, W), lambda b, i: (b, 0, 0))
    return pl.pallas_call(
        functools.partial(_fox_prompt_kernel, t=t),
        grid=(B, S // t),
        in_specs=[pl.BlockSpec((1, t, W), lambda b, i: (b, i, 0)), full, full,
                  pl.BlockSpec((1, t, FOX_HEADS), lambda b, i: (b, i, 0)),
                  pl.BlockSpec((1, S // t, 8, t), lambda b, i: (b, 0, 0, 0))],
        out_specs=pl.BlockSpec((1, t, W), lambda b, i: (b, i, 0)),
        out_shape=jax.ShapeDtypeStruct((B, S, W), jnp.float32),
        compiler_params=pltpu.CompilerParams(dimension_semantics=("parallel", "arbitrary"),
                                             vmem_limit_bytes=VMEM_LIMIT_BYTES),
        name="fox_prompt",
    )((q * ATTN_SCALE).astype(bf16), k.astype(bf16), v.astype(bf16), F, fk)


def _log_sigmoid(x):
    return jnp.minimum(x, 0.0) - jnp.log1p(jnp.exp(-jnp.abs(x)))


def _mlstm_kernel(q_ref, k_ref, kt_ref, v_ref, og_ref, gc_ref, gr_ref, nw_ref, cn0_ref, m0_ref,
                  h_ref, cn_ref, m_ref, cn_s, m_s, *, nb, L):
    c = pl.program_id(1)
    f32 = jnp.float32
    W = 2 * HEAD_DIM
    n_pairs = MLSTM_HEADS // 2

    @pl.when(c == 0)
    def _():
        cn_s[...] = cn0_ref[...]
        m_s[...] = m0_ref[...]

    lane = lax.broadcasted_iota(jnp.int32, (1, W), 1)
    low = lane < HEAD_DIM
    ti = lax.broadcasted_iota(jnp.int32, (L, L), 0)
    si = lax.broadcasted_iota(jnp.int32, (L, L), 1)
    causal = si <= ti
    tri = causal.astype(f32)
    tri_t = (ti <= si).astype(f32)
    srow = lax.broadcasted_iota(jnp.int32, (W, 2 * W), 0)
    slane = lax.broadcasted_iota(jnp.int32, (W, 2 * W), 1)
    top = srow < HEAD_DIM
    keep_a = top & ((slane < HEAD_DIM) | (slane == W))
    keep_b = (~top) & (((slane >= HEAD_DIM) & (slane < W)) | (slane == W + 1))
    lane_w = lax.broadcasted_iota(jnp.int32, (1, W), 1)
    mdt = k_ref.dtype

    for b in range(nb):
        gcol = gc_ref[b]
        grow = gr_ref[b]
        bcol = jnp.dot(tri, _log_sigmoid(gcol), precision=_HI, preferred_element_type=f32)
        brow = jnp.dot(_log_sigmoid(grow), tri_t, precision=_HI, preferred_element_type=f32)
        for hp in range(n_pairs):
            cols = slice(hp * W, (hp + 1) * W)
            q2 = q_ref[b, :, cols]
            k2 = k_ref[b, :, cols]
            v2 = v_ref[b, :, cols]
            kt2 = kt_ref[b, cols, :]
            cn = cn_s[b, hp]
            r = jnp.dot(q2, cn.astype(mdt), preferred_element_type=f32)
            zq = jnp.zeros_like(q2)
            per_head = []
            for x in range(2):
                h = 2 * hp + x
                qx = jnp.where(low, q2, zq) if x == 0 else jnp.where(low, zq, q2)
                b_c = bcol[:, MLSTM_HEADS + h:MLSTM_HEADS + h + 1]
                b_r = brow[MLSTM_HEADS + h:MLSTM_HEADS + h + 1, :]
                ig_c = gcol[:, h:h + 1]
                ig_r = grow[h:h + 1, :]
                m_prev = m_s[b, 0:1, h:h + 1]
                dmat = jnp.where(causal, b_c - b_r + ig_r, NEG_INF)
                a_c = b_c + m_prev
                m_t = jnp.maximum(a_c, jnp.max(dmat, axis=-1, keepdims=True))
                wq = jnp.exp(dmat - m_t) * lax.dot_general(qx, k2, _NT, preferred_element_type=f32)
                inter = jnp.exp(a_c - m_t)
                wv = jnp.dot(wq.astype(mdt), v2, preferred_element_type=f32)
                den = inter * r[:, W + x:W + x + 1] + jnp.sum(wq, axis=-1, keepdims=True)
                den = jnp.maximum(jnp.abs(den), jnp.exp(-m_t))
                bl = b_c[L - 1:L, :]
                g_c = bl - b_c + ig_c
                m_new = jnp.maximum(bl + m_prev, jnp.max(g_c, axis=0, keepdims=True))
                ws = jnp.exp(g_c - m_new)
                decay = jnp.exp(bl + m_prev - m_new)
                aug = jnp.concatenate([v2.astype(f32) * ws, jnp.where(lane_w == x, ws, 0.0)], axis=1)
                u = jnp.dot(kt2, aug.astype(mdt), preferred_element_type=f32)
                per_head.append((inter, wv, den, decay, u))
                m_s[b, 0:1, h:h + 1] = m_new
            (ia, wva, dena, deca, ua), (ib, wvb, denb, decb, ub) = per_head
            num = jnp.where(low, ia * r[:, :W] + wva, ib * r[:, :W] + wvb)
            hid = num / jnp.where(low, dena, denb)
            cn_s[b, hp] = (jnp.where(top, deca, decb) * cn + jnp.where(keep_a, ua, 0.0)
                           + jnp.where(keep_b, ub, 0.0))
            hid = _sigmoid(og_ref[b, :, cols]) * hid
            mean = lambda t: jnp.where(low, jnp.sum(jnp.where(low, t, 0.0), axis=-1, keepdims=True),
                                       jnp.sum(jnp.where(low, 0.0, t), axis=-1, keepdims=True)) / HEAD_DIM
            mu = mean(hid)
            var = mean(jnp.square(hid - mu))
            h_ref[b, :, cols] = (hid - mu) * lax.rsqrt(var + LN_EPS) * nw_ref[:, cols]

    @pl.when(c == pl.num_programs(1) - 1)
    def _():
        cn_ref[...] = cn_s[...]
        m_ref[...] = m_s[...]


def _mlstm_state_pack(c, n, m):
    Bx = c.shape[0]
    W = 2 * HEAD_DIM
    cn = jnp.zeros((Bx, MLSTM_HEADS // 2, W, 2 * W), jnp.float32)
    for hp in range(MLSTM_HEADS // 2):
        for x in range(2):
            r0 = x * HEAD_DIM
            cn = cn.at[:, hp, r0:r0 + HEAD_DIM, r0:r0 + HEAD_DIM].set(c[:, 2 * hp + x])
            cn = cn.at[:, hp, r0:r0 + HEAD_DIM, W + x].set(n[:, 2 * hp + x])
    m8 = jnp.zeros((Bx, 8, 128), jnp.float32).at[:, 0, :MLSTM_HEADS].set(m)
    return cn, m8


def _mlstm_state_unpack(cn, m8):
    W = 2 * HEAD_DIM
    c = jnp.stack([cn[:, h // 2, (h % 2) * HEAD_DIM:(h % 2 + 1) * HEAD_DIM,
                      (h % 2) * HEAD_DIM:(h % 2 + 1) * HEAD_DIM] for h in range(MLSTM_HEADS)], axis=1)
    n = jnp.stack([cn[:, h // 2, (h % 2) * HEAD_DIM:(h % 2 + 1) * HEAD_DIM, W + h % 2]
                   for h in range(MLSTM_HEADS)], axis=1)
    return c, n, m8[:, 0, :MLSTM_HEADS]


def _mlstm(q, k, v, og, mi, mf, norm_w, c0, n0, m0, L, nb, mxu_dtype):
    Bx, S, W4 = q.shape
    f32 = jnp.float32
    gates = jnp.concatenate([mi, mf], axis=-1).astype(f32)
    gcol = jnp.pad(gates, ((0, 0), (0, 0), (0, 128 - 2 * MLSTM_HEADS)))
    grow = gates.transpose(0, 2, 1)
    cn0, m8 = _mlstm_state_pack(c0.astype(f32), n0.astype(f32), m0.astype(f32))
    tok = lambda w: pl.BlockSpec((nb, L, w), lambda b, c: (b, c, 0))
    st_cn = pl.BlockSpec((nb, MLSTM_HEADS // 2, 128, 256), lambda b, c: (b, 0, 0, 0))
    st_m = pl.BlockSpec((nb, 8, 128), lambda b, c: (b, 0, 0))
    h, cn, m8 = pl.pallas_call(
        functools.partial(_mlstm_kernel, nb=nb, L=L),
        grid=(Bx // nb, S // L),
        in_specs=[tok(W4), tok(W4), pl.BlockSpec((nb, W4, L), lambda b, c: (b, 0, c)), tok(W4), tok(W4),
                  tok(128), pl.BlockSpec((nb, 8, L), lambda b, c: (b, 0, c)),
                  pl.BlockSpec((1, W4), lambda b, c: (0, 0)), st_cn, st_m],
        out_specs=[tok(W4), st_cn, st_m],
        out_shape=[jax.ShapeDtypeStruct((Bx, S, W4), f32),
                   jax.ShapeDtypeStruct(cn0.shape, f32), jax.ShapeDtypeStruct(m8.shape, f32)],
        scratch_shapes=[pltpu.VMEM((nb, MLSTM_HEADS // 2, 128, 256), f32), pltpu.VMEM((nb, 8, 128), f32)],
        compiler_params=pltpu.CompilerParams(dimension_semantics=("parallel", "arbitrary"),
                                             vmem_limit_bytes=VMEM_LIMIT_BYTES),
        name="mlstm",
    )(q.astype(mxu_dtype), k.astype(mxu_dtype), k.astype(mxu_dtype).transpose(0, 2, 1), v.astype(mxu_dtype),
      og.astype(f32), gcol, grow, norm_w.astype(f32).reshape(1, W4), cn0, m8)
    return h, _mlstm_state_unpack(cn, m8)


def _ln(x, w, b):
    mu = jnp.mean(x, axis=-1, keepdims=True)
    var = jnp.mean(jnp.square(x - mu), axis=-1, keepdims=True)
    return (x - mu) * lax.rsqrt(var + LN_EPS) * w + b


def _ln_kernel(x_ref, w_ref, b_ref, o_ref):
    o_ref[...] = _ln(x_ref[...], w_ref[...], b_ref[...])


def _layer_norm_rows(x, w, b, tm=1024):
    T, D = x.shape
    vec = pl.BlockSpec((1, D), lambda i: (0, 0))
    return pl.pallas_call(
        _ln_kernel, grid=(T // tm,),
        in_specs=[pl.BlockSpec((tm, D), lambda i: (i, 0)), vec, vec],
        out_specs=pl.BlockSpec((tm, D), lambda i: (i, 0)),
        out_shape=jax.ShapeDtypeStruct((T, D), jnp.float32),
        name="layer_norm",
    )(x, w.reshape(1, D), b.reshape(1, D))


Z_GM = 0
Z_NQ = 3072
Z_ROWS = Z_NQ + 256
Z_MQ = 3584
Z_MK = Z_MQ + 256
Z_MV = 4096
Z_MO = Z_MV + 256
Z_FQ = 4608
Z_WIN = Z_FQ + 256
Z_SMALL = Z_WIN + 128
Z_FK = 5120
Z_FV = Z_FK + 256
Z_WIDTH = 5632
Z_TILE = 512
SMALL_MI, SMALL_MF, SMALL_FF = 12, 16, 20


def _z_column_order():
    starts = np.concatenate([[0], np.cumsum(IN_SPLITS)])
    seg = lambda i, lo=0, hi=None: np.arange(starts[i] + lo, starts[i] + (IN_SPLITS[i] if hi is None else hi))
    pad = lambda n: np.full((n,), -1)
    order = np.concatenate([
        seg(13), seg(0), seg(1, 0, 256), seg(3), seg(4), seg(5), seg(6),
        seg(9), seg(1, 256, 384), seg(2), seg(7), seg(8), seg(12), pad(128 - 24), seg(10), seg(11)])
    assert order.shape == (Z_WIDTH,)
    return order


def _permute_in_proj(w_in, b_in):
    order = _z_column_order()
    valid = jnp.asarray(order >= 0)
    idx = jnp.asarray(np.maximum(order, 0))
    w = jnp.where(valid[None, :], w_in[:, idx], 0.0)
    b = jnp.where(valid, b_in[idx], 0.0)
    return w, b


def _proj_kernel(x_ref, w_ref, wlo_ref, b_ref, o_ref, xh_s, xl_s, *, hi_tile):
    j = pl.program_id(1)
    f32 = jnp.float32

    @pl.when(j == 0)
    def _():
        x = x_ref[...]
        xh = x.astype(jnp.bfloat16)
        xh_s[...] = xh
        xl_s[...] = (x - xh.astype(f32)).astype(jnp.bfloat16)

    @pl.when(j != hi_tile)
    def _():
        o_ref[...] = jnp.dot(xh_s[...], w_ref[...], preferred_element_type=f32) + b_ref[...]

    @pl.when(j == hi_tile)
    def _():
        acc = jnp.dot(xh_s[...], wlo_ref[...], preferred_element_type=f32)
        acc += jnp.dot(xl_s[...], w_ref[...], preferred_element_type=f32)
        acc += jnp.dot(xh_s[...], w_ref[...], preferred_element_type=f32)
        o_ref[...] = acc + b_ref[...]


def _in_projection(x, w_in, b_in, tm=1024):
    T, D = x.shape
    w, b = _permute_in_proj(w_in, b_in)
    wh = w.astype(jnp.bfloat16)
    hi_tile = Z_NQ // Z_TILE
    wlo = (w[:, Z_NQ:Z_NQ + Z_TILE] - wh[:, Z_NQ:Z_NQ + Z_TILE].astype(jnp.float32)).astype(jnp.bfloat16)
    return pl.pallas_call(
        functools.partial(_proj_kernel, hi_tile=hi_tile),
        grid=(T // tm, Z_WIDTH // Z_TILE),
        in_specs=[pl.BlockSpec((tm, D), lambda i, j: (i, 0)),
                  pl.BlockSpec((D, Z_TILE), lambda i, j: (0, j)),
                  pl.BlockSpec((D, Z_TILE), lambda i, j: (0, 0)),
                  pl.BlockSpec((1, Z_TILE), lambda i, j: (0, j))],
        out_specs=pl.BlockSpec((tm, Z_TILE), lambda i, j: (i, j)),
        out_shape=jax.ShapeDtypeStruct((T, Z_WIDTH), jnp.float32),
        scratch_shapes=[pltpu.VMEM((tm, D), jnp.bfloat16), pltpu.VMEM((tm, D), jnp.bfloat16)],
        compiler_params=pltpu.CompilerParams(dimension_semantics=("parallel", "arbitrary"),
                                             vmem_limit_bytes=VMEM_LIMIT_BYTES),
        name="in_projection",
    )(x, wh, wlo, b.reshape(1, Z_WIDTH))


def _merge_kernel(x_ref, gm_ref, on_ref, om_ref, of_ref, wb_ref, wo_ref, lw_ref, lb_ref, o_ref, *, alpha):
    f32, bf16 = jnp.float32, jnp.bfloat16
    y = None
    for m, br in enumerate((on_ref, om_ref, of_ref)):
        proj = jnp.dot(br[...].astype(bf16), wb_ref[m], preferred_element_type=f32)
        term = _sigmoid(gm_ref[:, m * D_MODEL:(m + 1) * D_MODEL]) * proj
        y = term if y is None else y + term
    mix = jnp.dot(y.astype(bf16), wo_ref[...], preferred_element_type=f32)
    o_ref[...] = _ln(alpha * x_ref[...] + mix, lw_ref[...], lb_ref[...])


def _merge_ln(x, z, o_nsa, o_ml, o_fox, w_branch, w_out, ln_w, ln_b, alpha, tm=512):
    T, D = x.shape
    tok = lambda w: pl.BlockSpec((tm, w), lambda i: (i, 0))
    vec = pl.BlockSpec((1, D), lambda i: (0, 0))
    return pl.pallas_call(
        functools.partial(_merge_kernel, alpha=alpha),
        grid=(T // tm,),
        in_specs=[tok(D), tok(N_BRANCH * D), tok(BRANCH_WIDTH), tok(BRANCH_WIDTH), tok(BRANCH_WIDTH),
                  pl.BlockSpec((N_BRANCH, BRANCH_WIDTH, D), lambda i: (0, 0, 0)),
                  pl.BlockSpec((D, D), lambda i: (0, 0)), vec, vec],
        out_specs=tok(D),
        out_shape=jax.ShapeDtypeStruct((T, D), jnp.float32),
        compiler_params=pltpu.CompilerParams(dimension_semantics=("parallel",),
                                             vmem_limit_bytes=VMEM_LIMIT_BYTES),
        name="merge_ln",
    )(x, z, o_nsa, o_ml, o_fox, w_branch.astype(jnp.bfloat16), w_out.astype(jnp.bfloat16),
      ln_w.reshape(1, D), ln_b.reshape(1, D))


def _route(logits):
    tm = logits.shape[0]
    lane = lax.broadcasted_iota(jnp.int32, (tm, 128), 1)
    lanef = lane.astype(jnp.float32)
    big = 1e9
    is_g = lane < N_GROUPS
    lg = jnp.where(is_g, logits, -jnp.inf)
    eg = jnp.exp(lg - jnp.max(lg, axis=-1, keepdims=True))
    pg = eg / jnp.sum(eg, axis=-1, keepdims=True)
    g_val = jnp.max(pg, axis=-1, keepdims=True)
    g_idx = jnp.min(jnp.where(is_g & (pg == g_val), lanef, big), axis=-1, keepdims=True)
    e_lo = N_GROUPS + EXPERTS_PER_GROUP * g_idx
    in_grp = (lanef >= e_lo) & (lanef < e_lo + EXPERTS_PER_GROUP)
    le = jnp.where(in_grp, logits, -jnp.inf)
    ee = jnp.exp(le - jnp.max(le, axis=-1, keepdims=True))
    pe = ee / jnp.sum(ee, axis=-1, keepdims=True)
    v1 = jnp.max(pe, axis=-1, keepdims=True)
    i1 = jnp.min(jnp.where(in_grp & (pe == v1), lanef, big), axis=-1, keepdims=True)
    rest = in_grp & (lanef != i1)
    pe2 = jnp.where(rest, pe, -1.0)
    v2 = jnp.max(pe2, axis=-1, keepdims=True)
    i2 = jnp.min(jnp.where(rest & (pe2 == v2), lanef, big), axis=-1, keepdims=True)
    tot = v1 + v2
    return jnp.where(lanef == i1, g_val * v1 / tot, jnp.where(lanef == i2, g_val * v2 / tot, 0.0))


def _moe_kernel(x_ref, wr_ref, br_ref, wgu_ref, wd_ref, lw_ref, lb_ref, o_ref, xb_s, gate_s, acc_s, *, alpha):
    e = pl.program_id(1)
    f32, bf16 = jnp.float32, jnp.bfloat16

    @pl.when(e == 0)
    def _():
        x = x_ref[...]
        xb_s[...] = x.astype(bf16)
        logits = jnp.dot(x, wr_ref[...], precision=_HI, preferred_element_type=f32) + br_ref[...]
        gate_s[...] = _route(logits)
        acc_s[...] = jnp.zeros_like(acc_s)

    lane = lax.broadcasted_iota(jnp.int32, gate_s.shape, 1)
    gate = jnp.sum(jnp.where(lane == N_GROUPS + e, gate_s[...], 0.0), axis=-1, keepdims=True)
    gu = jnp.dot(xb_s[...], wgu_ref[0], preferred_element_type=f32)
    g, u = gu[:, :D_EXPERT], gu[:, D_EXPERT:]
    h = (g * _sigmoid(g)) * u * gate
    acc_s[...] += jnp.dot(h.astype(bf16), wd_ref[0], preferred_element_type=f32)

    @pl.when(e == pl.num_programs(1) - 1)
    def _():
        o_ref[...] = _ln(alpha * x_ref[...] + acc_s[...], lw_ref[...], lb_ref[...])


def _moe_ln(x, w_group, b_group, w_expert, b_expert, w_gate, w_up, w_down, ln_w, ln_b, alpha, tm=1024):
    T, D = x.shape
    bf16 = jnp.bfloat16
    n_r = N_GROUPS + N_EXPERTS
    wr = jnp.pad(jnp.concatenate([w_group, w_expert], axis=1), ((0, 0), (0, 128 - n_r)))
    br = jnp.pad(jnp.concatenate([b_group, b_expert]), (0, 128 - n_r)).reshape(1, 128)
    wgu = jnp.concatenate([w_gate, w_up], axis=-1).astype(bf16)
    vec = pl.BlockSpec((1, D), lambda i, e: (0, 0))
    return pl.pallas_call(
        functools.partial(_moe_kernel, alpha=alpha),
        grid=(T // tm, N_EXPERTS),
        in_specs=[pl.BlockSpec((tm, D), lambda i, e: (i, 0)),
                  pl.BlockSpec((D, 128), lambda i, e: (0, 0)),
                  pl.BlockSpec((1, 128), lambda i, e: (0, 0)),
                  pl.BlockSpec((1, D, 2 * D_EXPERT), lambda i, e: (e, 0, 0)),
                  pl.BlockSpec((1, D_EXPERT, D), lambda i, e: (e, 0, 0)), vec, vec],
        out_specs=pl.BlockSpec((tm, D), lambda i, e: (i, 0)),
        out_shape=jax.ShapeDtypeStruct((T, D), jnp.float32),
        scratch_shapes=[pltpu.VMEM((tm, D), bf16), pltpu.VMEM((tm, 128), jnp.float32),
                        pltpu.VMEM((tm, D), jnp.float32)],
        compiler_params=pltpu.CompilerParams(dimension_semantics=("parallel", "arbitrary"),
                                             vmem_limit_bytes=VMEM_LIMIT_BYTES),
        name="moe_ln",
    )(x, wr, br, wgu, w_down.astype(bf16), ln_w.reshape(1, D), ln_b.reshape(1, D))


N_PAGES = PAST_LEN // PAGE_SIZE
NEW_PAD = 128
DEC_KEYS = PAST_LEN + NEW_PAD
DEC_ROWS = NSA_HEADS * DEC_SEQ


def _pages_token_minor(cache):
    nd = cache.ndim
    t = cache.transpose((0, 1) + tuple(range(3, nd)) + (2,))
    return t.reshape(cache.shape[0] * cache.shape[1], -1, cache.shape[2])


def _page_specs(rows, row_block, layer, n_phys):
    def spec(p):
        return pl.BlockSpec((1, rows, PAGE_SIZE),
                            lambda b, pt: (layer * n_phys + pt[b * N_PAGES + p], row_block, 0))
    return [spec(p) for p in range(N_PAGES)]


def _softmax_rows(s):
    m = jnp.max(s, axis=-1, keepdims=True)
    p = jnp.exp(s - m)
    return p, jnp.sum(p, axis=-1, keepdims=True)


def _fox_decode_kernel(pt_ref, qbd_ref, knew_ref, vnew_ref, fk_ref, fq_ref, *refs):
    pages, o_ref = refs[:N_PAGES], refs[N_PAGES]
    f32, bf16 = jnp.float32, jnp.bfloat16
    W = FOX_HEADS * HEAD_DIM
    qbd = qbd_ref[0]
    s = [jnp.dot(qbd, pg[0, :W, :].astype(bf16), preferred_element_type=f32) for pg in pages]
    s.append(jnp.dot(qbd, knew_ref[0], preferred_element_type=f32))
    s = jnp.concatenate(s, axis=1)
    rowh = lax.broadcasted_iota(jnp.int32, (DEC_ROWS, 1), 0) // DEC_SEQ
    fk = fk_ref[0]
    fk_rows = jnp.where(rowh == 0, fk[0:1], jnp.where(rowh == 1, fk[1:2], jnp.where(rowh == 2, fk[2:3], fk[3:4])))
    col = lax.broadcasted_iota(jnp.int32, (DEC_ROWS, DEC_KEYS), 1)
    t = lax.broadcasted_iota(jnp.int32, (DEC_ROWS, DEC_KEYS), 0) % DEC_SEQ
    ok = (col < PAST_LEN) | (col - PAST_LEN <= t)
    s = jnp.where(ok, s + (fq_ref[0] - fk_rows), NEG_INF)
    p, l = _softmax_rows(s)
    pb = p.astype(bf16)
    o = lax.dot_general(pb[:, PAST_LEN:], vnew_ref[0], _NT, preferred_element_type=f32)
    for i, pg in enumerate(pages):
        o += lax.dot_general(pb[:, i * PAGE_SIZE:(i + 1) * PAGE_SIZE], pg[0, W:, :].astype(bf16), _NT,
                             preferred_element_type=f32)
    o = o / l
    lane_h = lax.broadcasted_iota(jnp.int32, (DEC_ROWS, W), 1) // HEAD_DIM
    o = jnp.where(lane_h == rowh, o, 0.0)
    o_ref[0] = o[0:8] + o[8:16] + o[16:24] + o[24:32]


def _pad_new_t(x):
    return jnp.pad(x.transpose(0, 2, 1), ((0, 0), (0, 0), (0, NEW_PAD - DEC_SEQ))).astype(jnp.bfloat16)


def _fox_decode(q, k_new, v_new, lf_new, cache_kv, cache_lf, page_table, layer):
    DB = q.shape[0]
    f32, bf16 = jnp.float32, jnp.bfloat16
    W = FOX_HEADS * HEAD_DIM
    eye = jnp.eye(FOX_HEADS, dtype=f32)
    qh = (q * ATTN_SCALE).reshape(DB, DEC_SEQ, FOX_HEADS, HEAD_DIM).transpose(0, 2, 1, 3)
    qbd = (qh[:, :, :, None, :] * eye[None, :, None, :, None]).reshape(DB, DEC_ROWS, W).astype(bf16)
    lf_all = jnp.concatenate([cache_lf[layer][page_table].reshape(DB, PAST_LEN, FOX_HEADS).astype(f32), lf_new], axis=1)
    F = jnp.cumsum(lf_all, axis=1)
    fk = jnp.pad(F.transpose(0, 2, 1), ((0, 0), (0, 8 - FOX_HEADS), (0, DEC_KEYS - PAST_LEN - DEC_SEQ)))
    fq = F[:, PAST_LEN:].transpose(0, 2, 1).reshape(DB, DEC_ROWS, 1)
    per_seq = lambda r, w: pl.BlockSpec((1, r, w), lambda b, pt: (b, 0, 0))
    pages = _pages_token_minor(cache_kv)
    return pl.pallas_call(
        _fox_decode_kernel,
        grid_spec=pltpu.PrefetchScalarGridSpec(
            num_scalar_prefetch=1, grid=(DB,),
            in_specs=[per_seq(DEC_ROWS, W), per_seq(W, NEW_PAD), per_seq(W, NEW_PAD), per_seq(8, DEC_KEYS),
                      per_seq(DEC_ROWS, 1)] + _page_specs(2 * W, 0, layer, cache_kv.shape[1]),
            out_specs=per_seq(DEC_SEQ, W)),
        out_shape=jax.ShapeDtypeStruct((DB, DEC_SEQ, W), f32),
        compiler_params=pltpu.CompilerParams(dimension_semantics=("parallel",), vmem_limit_bytes=VMEM_LIMIT_BYTES),
        name="fox_decode",
    )(page_table.reshape(-1), qbd, _pad_new_t(k_new), _pad_new_t(v_new), fk, fq, *([pages] * N_PAGES))


def _nsa_decode_kernel(pt_ref, qs_ref, qf_ref, kcvc_ref, rnew_ref, wbuf_ref, wnew_ref, g_ref, e_ref, *refs):
    pages, o_ref = refs[:N_PAGES], refs[N_PAGES]
    f32, bf16 = jnp.float32, jnp.bfloat16
    H, T, HD = NSA_HEADS, DEC_SEQ, HEAD_DIM
    qs = qs_ref[0]
    row = lax.broadcasted_iota(jnp.int32, (DEC_ROWS, 1), 0)
    head = row // T
    slope = jnp.where(head == 0, NSA_SLOPES[0], jnp.where(head == 1, NSA_SLOPES[1],
                      jnp.where(head == 2, NSA_SLOPES[2], NSA_SLOPES[3]))).astype(f32)
    posq = PAST_LEN + row % T

    kcvc = kcvc_ref[0]
    n_cb = kcvc.shape[0]
    sc = lax.dot_general(qf_ref[0], kcvc, _NT, precision=_HI, preferred_element_type=f32) * ATTN_SCALE
    jb = lax.broadcasted_iota(jnp.int32, (DEC_ROWS, n_cb), 1)
    distc = posq - (jb * NSA_BLOCK + NSA_BLOCK - 1)
    okc = distc >= 0
    sc = jnp.where(okc, sc - slope * distc.astype(f32), NEG_INF)
    pc = jnp.exp(sc - jnp.max(sc, axis=-1, keepdims=True)) * okc.astype(f32)
    pc = pc / jnp.maximum(jnp.sum(pc, axis=-1, keepdims=True), TINY)
    o_cmp = jnp.dot(pc.astype(bf16), kcvc.astype(bf16), preferred_element_type=f32)[:, HD:]
    imp = pc[0:T] + pc[T:2 * T] + pc[2 * T:3 * T] + pc[3 * T:4 * T]
    imp = jnp.concatenate([imp, jnp.zeros((T, 128 - n_cb), f32)], axis=1)
    cur = (PAST_LEN + lax.broadcasted_iota(jnp.int32, (T, 1), 0)) // NSA_BLOCK
    msel = _select_blocks(imp, cur, NSA_TOPK).astype(bf16)

    mexp = jnp.dot(msel, e_ref[...], preferred_element_type=f32)
    col = lax.broadcasted_iota(jnp.int32, (T, DEC_KEYS), 1)
    tq = lax.broadcasted_iota(jnp.int32, (T, DEC_KEYS), 0)
    d = PAST_LEN + tq - col
    ok = (mexp > 0.5) & (d >= 0)
    kv = [pg[0].astype(bf16) for pg in pages] + [rnew_ref[0]]
    s = jnp.concatenate([jnp.dot(qs, x[:HD], preferred_element_type=f32) for x in kv], axis=1)
    s = jnp.where(_tile_rows(ok, H), s - slope * _tile_rows(d.astype(f32), H), NEG_INF)
    p, l = _softmax_rows(s)
    pb = p.astype(bf16)
    acc = jnp.zeros((DEC_ROWS, HD), f32)
    for i, x in enumerate(kv):
        acc += lax.dot_general(pb[:, i * PAGE_SIZE:(i + 1) * PAGE_SIZE], x[HD:], _NT, preferred_element_type=f32)
    o_sel = acc / l

    wb = wbuf_ref.shape[2]
    kvw = [wbuf_ref[0].astype(bf16), wnew_ref[0]]
    sw = jnp.concatenate([jnp.dot(qs, x[:HD], preferred_element_type=f32) for x in kvw], axis=1)
    colw = lax.broadcasted_iota(jnp.int32, (T, wb + NEW_PAD), 1)
    tw = lax.broadcasted_iota(jnp.int32, (T, wb + NEW_PAD), 0)
    dw = wb + tw - colw
    okw = (dw >= 0) & (dw < NSA_WINDOW)
    sw = jnp.where(_tile_rows(okw, H), sw - slope * _tile_rows(dw.astype(f32), H), NEG_INF)
    pw, lw = _softmax_rows(sw)
    pwb = pw.astype(bf16)
    accw = (lax.dot_general(pwb[:, :wb], kvw[0][HD:], _NT, preferred_element_type=f32)
            + lax.dot_general(pwb[:, wb:], kvw[1][HD:], _NT, preferred_element_type=f32))
    o_win = accw / lw

    g = _sigmoid(g_ref[0])
    gate = lambda c: jnp.concatenate([g[:, 3 * h + c:3 * h + c + 1] for h in range(H)], axis=0)
    o_ref[0] = gate(0) * o_cmp + gate(1) * o_sel + gate(2) * o_win


def _compress_pages_kernel(x_ref, wk_ref, wv_ref, o_ref):
    tm = x_ref.shape[0]
    f32 = jnp.float32
    acc_k = jnp.zeros((tm, 2 * HEAD_DIM), f32)
    acc_v = jnp.zeros((tm, 2 * HEAD_DIM), f32)
    for d in range(HEAD_DIM):
        acc_k += jnp.dot(x_ref[:, d, :], wk_ref[d], precision=_HI, preferred_element_type=f32)
        acc_v += jnp.dot(x_ref[:, HEAD_DIM + d, :], wv_ref[d], preferred_element_type=f32)
    o_ref[...] = jnp.concatenate([acc_k, acc_v], axis=1)


def _nsa_compress_pages(pages, w_ck, w_cv, layer, n_phys, tm=64):
    eye = jnp.eye(PAGE_SIZE // NSA_BLOCK, dtype=jnp.float32)
    big = lambda w: jnp.einsum('pde,bc->dbpce', w, eye).reshape(HEAD_DIM, PAGE_SIZE, 2 * HEAD_DIM)
    wspec = pl.BlockSpec((HEAD_DIM, PAGE_SIZE, 2 * HEAD_DIM), lambda i: (0, 0, 0))
    return pl.pallas_call(
        _compress_pages_kernel,
        grid=(n_phys // tm,),
        in_specs=[pl.BlockSpec((tm, 2 * HEAD_DIM, PAGE_SIZE), lambda i: (layer * (n_phys // tm) + i, 0, 0)),
                  wspec, wspec],
        out_specs=pl.BlockSpec((tm, 4 * HEAD_DIM), lambda i: (i, 0)),
        out_shape=jax.ShapeDtypeStruct((n_phys, 4 * HEAD_DIM), jnp.float32),
        compiler_params=pltpu.CompilerParams(dimension_semantics=("parallel",), vmem_limit_bytes=VMEM_LIMIT_BYTES),
        name="nsa_compress_pages",
    )(pages, big(w_ck), big(w_cv))


def _nsa_decode(nq, rows_new, win_new, gates, cache_rows, cache_win, page_table, w_ck, w_cv, layer):
    DB = nq.shape[0]
    f32, bf16 = jnp.float32, jnp.bfloat16
    HD = HEAD_DIM
    n_phys = cache_rows.shape[1]
    n_blk = PAGE_SIZE // NSA_BLOCK
    pages = _pages_token_minor(cache_rows)
    kcvc = _nsa_compress_pages(pages, w_ck, w_cv, layer, n_phys)[page_table]
    kcvc = kcvc.reshape(DB, N_PAGES, 2, n_blk, HD).transpose(0, 1, 3, 2, 4).reshape(DB, N_PAGES * n_blk, 2 * HD)
    stack = lambda x: x.reshape(DB, DEC_SEQ, NSA_HEADS, HD).transpose(0, 2, 1, 3).reshape(DB, DEC_ROWS, HD)
    qs = stack(nq * ATTN_SCALE).astype(bf16)
    qf = jnp.pad(stack(nq), ((0, 0), (0, 0), (0, HD))).astype(f32)
    win_t = cache_win.transpose(0, 1, 3, 4, 2).reshape(cache_win.shape[0] * DB, 2 * HD, cache_win.shape[2])
    colk = np.arange(DEC_KEYS)
    e = (np.arange(128)[:, None] == colk[None, :] // NSA_BLOCK) & (colk[None, :] < PAST_LEN + NSA_BLOCK)
    wb = cache_win.shape[2]
    per_seq = lambda r, w: pl.BlockSpec((1, r, w), lambda b, pt: (b, 0, 0))
    o = pl.pallas_call(
        _nsa_decode_kernel,
        grid_spec=pltpu.PrefetchScalarGridSpec(
            num_scalar_prefetch=1, grid=(DB,),
            in_specs=[per_seq(DEC_ROWS, HD), per_seq(DEC_ROWS, 2 * HD), per_seq(PAST_LEN // NSA_BLOCK, 2 * HD),
                      per_seq(2 * HD, NEW_PAD),
                      pl.BlockSpec((1, 2 * HD, wb), lambda b, pt: (layer * DB + b, 0, 0)),
                      per_seq(2 * HD, NEW_PAD), per_seq(DEC_SEQ, 128),
                      pl.BlockSpec((128, DEC_KEYS), lambda b, pt: (0, 0))]
                     + _page_specs(2 * HD, 1, layer, n_phys),
            out_specs=per_seq(DEC_ROWS, HD)),
        out_shape=jax.ShapeDtypeStruct((DB, DEC_ROWS, HD), f32),
        compiler_params=pltpu.CompilerParams(dimension_semantics=("parallel",), vmem_limit_bytes=VMEM_LIMIT_BYTES),
        name="nsa_decode",
    )(page_table.reshape(-1), qs, qf, kcvc, _pad_new_t(rows_new[..., 2 * HD:]), win_t,
      _pad_new_t(win_new), gates, jnp.asarray(e, bf16), *([pages] * N_PAGES))
    return o.reshape(DB, NSA_HEADS, DEC_SEQ, HD).transpose(0, 2, 1, 3).reshape(DB, DEC_SEQ, BRANCH_WIDTH)


def _copy_kernel(pt_ref, x_ref, o_ref):
    o_ref[...] = x_ref[...]


def _gather_pages(cache, page_table):
    db, n_pages = page_table.shape
    _, page, w = cache.shape
    out = pl.pallas_call(
        _copy_kernel,
        grid_spec=pltpu.PrefetchScalarGridSpec(
            num_scalar_prefetch=1, grid=(db * n_pages,),
            in_specs=[pl.BlockSpec((1, page, w), lambda i, pt: (pt[i], 0, 0))],
            out_specs=pl.BlockSpec((1, page, w), lambda i, pt: (i, 0, 0))),
        out_shape=jax.ShapeDtypeStruct((db * n_pages, page, w), cache.dtype),
        name="gather_pages",
    )(page_table.reshape(-1), cache)
    return out.reshape(db, n_pages * page, w)


def _split_points():
    return [int(v) for v in np.cumsum(IN_SPLITS)[:-1]]


def _layer_norm(x, w, b):
    xf = x.astype(jnp.float32)
    mu = jnp.mean(xf, axis=-1, keepdims=True)
    var = jnp.mean(jnp.square(xf - mu), axis=-1, keepdims=True)
    y = (xf - mu) * lax.rsqrt(var + LN_EPS) * w.astype(jnp.float32) + b.astype(jnp.float32)
    return y.astype(x.dtype)


def _masked_softmax(s, mask):
    s = jnp.where(mask, s.astype(jnp.float32), NEG_INF)
    p = jnp.exp(s - jnp.max(s, axis=-1, keepdims=True)) * mask
    return p / jnp.maximum(jnp.sum(p, axis=-1, keepdims=True), TINY)


def _alibi_slopes(n):
    return jnp.exp2(-8.0 * jnp.arange(1, n + 1, dtype=jnp.float32) / n)


def _project(x, w_in, b_in):
    B, L, _ = x.shape
    f32 = jnp.float32
    z = _pallas_linear(x.reshape(B * L, -1), w_in, b_in).reshape(B, L, -1)
    (nq, nkv, ng, mq, mk, mv, mo, mi, mf, fq, fk, fv, ff, gm) = jnp.split(z, _split_points(), axis=-1)
    heads = lambda t, h: t.reshape(B, L, h, HEAD_DIM)
    rows = nkv.reshape(B, L, 6, HEAD_DIM)
    return dict(
        nsa_q=heads(nq, NSA_HEADS),
        nsa_rows=rows[:, :, :NSA_ROWS],
        nsa_win=rows[:, :, NSA_ROWS:],
        nsa_g=jax.nn.sigmoid(ng.astype(f32)).reshape(B, L, NSA_HEADS, 3),
        nsa_g_raw=ng,
        ml_q=heads(mq, MLSTM_HEADS).astype(f32),
        ml_k=heads(mk, MLSTM_HEADS).astype(f32) * ATTN_SCALE,
        ml_v=heads(mv, MLSTM_HEADS).astype(f32),
        ml_o=jax.nn.sigmoid(mo.astype(f32)).reshape(B, L, MLSTM_HEADS, HEAD_DIM),
        ml_i=mi.astype(f32),
        ml_lf=jax.nn.log_sigmoid(mf.astype(f32)),
        fox_q=heads(fq, FOX_HEADS),
        fox_k=heads(fk, FOX_HEADS),
        fox_v=heads(fv, FOX_HEADS),
        fox_lf=jax.nn.log_sigmoid(ff.astype(f32)),
        merge=jax.nn.sigmoid(gm.astype(f32)).reshape(B, L, N_BRANCH, D_MODEL))


def _nsa_compress(k, w):
    B, L, _ = k.shape
    n_cb = L // NSA_BLOCK
    blocks = k[:, :n_cb * NSA_BLOCK].reshape(B, n_cb, NSA_BLOCK, HEAD_DIM)
    return jnp.einsum('bjpd,pde->bje', blocks, w)


def _nsa_cmp_attend(q, kc, vc, pos_q, slopes):
    n_cb = kc.shape[1]
    blk_end = jnp.arange(n_cb) * NSA_BLOCK + NSA_BLOCK - 1
    dist = pos_q[:, None] - blk_end[None, :]
    s = jnp.einsum('bqhd,bjd->bhqj', q, kc) * ATTN_SCALE - slopes[:, None, None] * dist.astype(jnp.float32)
    p = _masked_softmax(s, dist >= 0)
    o = jnp.einsum('bhqj,bjd->bqhd', p.astype(vc.dtype), vc)
    return o, jnp.sum(p, axis=1)


def _nsa_select_blocks(imp, pos_q, n_sb):
    n_cb = imp.shape[-1]
    imp = jnp.pad(imp, ((0, 0), (0, 0), (0, n_sb - n_cb)))
    j = jnp.arange(n_sb)[None, :]
    cur = (pos_q // NSA_BLOCK)[:, None]
    score = jnp.where(j > cur, NEG_INF, jnp.where((j == cur) | (j == 0), FORCED_SCORE, imp))
    vals, idx = lax.top_k(score, min(NSA_TOPK, n_sb))
    return idx, vals > 0.5 * NEG_INF


def _nsa_sel_attend(q, kb, vb, idx, valid, pos_q, slopes):
    B, Q = q.shape[:2]
    bi = jnp.arange(B)[:, None, None]
    kg = kb[bi, idx].reshape(B, Q, -1, HEAD_DIM)
    vg = vb[bi, idx].reshape(B, Q, -1, HEAD_DIM)
    pos_k = (idx[..., None] * NSA_BLOCK + jnp.arange(NSA_BLOCK)).reshape(B, Q, -1)
    ok = jnp.repeat(valid, NSA_BLOCK, axis=-1) & (pos_k <= pos_q[None, :, None])
    dist = (pos_q[None, :, None] - pos_k).astype(jnp.float32)
    s = jnp.einsum('bqhd,bqkd->bhqk', q, kg) * ATTN_SCALE - slopes[None, :, None, None] * dist[:, None]
    p = _masked_softmax(s, ok[:, None])
    return jnp.einsum('bhqk,bqkd->bqhd', p.astype(vg.dtype), vg)


def _nsa_cmp_sel(q, rows, pos_q, w_ck, w_cv, slopes):
    B, Q = q.shape[:2]
    L = rows.shape[1]
    kc = _nsa_compress(rows[:, :, 0], w_ck)
    vc = _nsa_compress(rows[:, :, 1], w_cv)
    o_cmp, imp = _nsa_cmp_attend(q, kc, vc, pos_q, slopes)
    n_sb = -(-L // NSA_BLOCK)
    idx, valid = _nsa_select_blocks(imp, pos_q, n_sb)
    pad = ((0, 0), (0, n_sb * NSA_BLOCK - L), (0, 0))
    kb = jnp.pad(rows[:, :, 2], pad).reshape(B, n_sb, NSA_BLOCK, HEAD_DIM)
    vb = jnp.pad(rows[:, :, 3], pad).reshape(B, n_sb, NSA_BLOCK, HEAD_DIM)
    qb = Q_BLOCK if Q % Q_BLOCK == 0 else Q
    nqb = Q // qb

    def block(args):
        qc, ic, okc, pc = args
        return _nsa_sel_attend(qc, kb, vb, ic, okc, pc, slopes)

    xs = (q.reshape(B, nqb, qb, NSA_HEADS, HEAD_DIM).swapaxes(0, 1),
          idx.reshape(B, nqb, qb, -1).swapaxes(0, 1),
          valid.reshape(B, nqb, qb, -1).swapaxes(0, 1),
          pos_q.reshape(nqb, qb))
    o_sel = lax.map(block, xs).swapaxes(0, 1).reshape(B, Q, NSA_HEADS, HEAD_DIM)
    return o_cmp, o_sel


def _nsa_win_attend(q, k, v, pos_q, pos_k, slopes):
    dist = pos_q[:, None] - pos_k[None, :]
    ok = (dist >= 0) & (dist < NSA_WINDOW) & (pos_k[None, :] >= 0)
    s = jnp.einsum('bqhd,bkd->bhqk', q, k) * ATTN_SCALE - slopes[:, None, None] * dist.astype(jnp.float32)
    p = _masked_softmax(s, ok)
    return jnp.einsum('bhqk,bkd->bqhd', p.astype(v.dtype), v)


def _nsa_win_prompt(q, k, v, slopes):
    B, S = q.shape[:2]
    nqb = S // Q_BLOCK
    nb = NSA_WINDOW // Q_BLOCK
    pad = ((0, 0), (NSA_WINDOW, 0), (0, 0))
    kp = jnp.pad(k, pad).reshape(B, nqb + nb, Q_BLOCK, HEAD_DIM)
    vp = jnp.pad(v, pad).reshape(B, nqb + nb, Q_BLOCK, HEAD_DIM)
    band = jnp.arange(nqb)[:, None] + jnp.arange(nb + 1)[None, :]
    kband = kp[:, band].reshape(B, nqb, (nb + 1) * Q_BLOCK, HEAD_DIM)
    vband = vp[:, band].reshape(B, nqb, (nb + 1) * Q_BLOCK, HEAD_DIM)
    qblk = q.reshape(B, nqb, Q_BLOCK, NSA_HEADS, HEAD_DIM)
    pos_q = jnp.arange(S).reshape(nqb, Q_BLOCK)
    pos_k = (jnp.arange(nqb) * Q_BLOCK - NSA_WINDOW)[:, None] + jnp.arange((nb + 1) * Q_BLOCK)[None, :]
    o = jax.vmap(_nsa_win_attend, in_axes=(1, 1, 1, 0, 0, None), out_axes=1)(qblk, kband, vband, pos_q, pos_k, slopes)
    return o.reshape(B, S, NSA_HEADS, HEAD_DIM)


def _nsa_combine(g, o_cmp, o_sel, o_win):
    return jnp.einsum('blhc,cblhd->blhd', g, jnp.stack([o_cmp, o_sel, o_win]))


def _mlstm_chunk(carry, xs):
    c, n, m = carry
    q, k, v, ig, lf = xs
    L = q.shape[1]
    b = jnp.cumsum(lf, axis=1)
    causal = jnp.tril(jnp.ones((L, L), bool))[None, :, :, None]
    dmat = jnp.where(causal, b[:, :, None, :] - b[:, None, :, :] + ig[:, None, :, :], NEG_INF)
    a = b + m[:, None, :]
    m_t = jnp.maximum(a, jnp.max(dmat, axis=2))
    wq = jnp.exp(dmat - m_t[:, :, None, :]) * jnp.einsum('bthd,bshd->btsh', q, k)
    inter = jnp.exp(a - m_t)
    num = inter[..., None] * jnp.einsum('bthd,bhde->bthe', q, c) + jnp.einsum('btsh,bshe->bthe', wq, v)
    den = inter * jnp.einsum('bthd,bhd->bth', q, n) + jnp.sum(wq, axis=2)
    h = num / jnp.maximum(jnp.abs(den), jnp.exp(-m_t))[..., None]
    bl = b[:, -1]
    g = bl[:, None, :] - b + ig
    m_new = jnp.maximum(bl + m, jnp.max(g, axis=1))
    ws = jnp.exp(g - m_new[:, None, :])
    decay = jnp.exp(bl + m - m_new)
    c_new = decay[..., None, None] * c + jnp.einsum('bsh,bshd,bshe->bhde', ws, k, v)
    n_new = decay[..., None] * n + jnp.einsum('bsh,bshd->bhd', ws, k)
    return (c_new, n_new, m_new), h


def _mlstm_prompt(q, k, v, ig, lf):
    B, S, H, _ = q.shape
    ch = MLSTM_CHUNK if S % MLSTM_CHUNK == 0 else S
    nc = S // ch
    to_chunks = lambda t: t.reshape((B, nc, ch) + t.shape[2:]).swapaxes(0, 1)
    f32 = jnp.float32
    init = (jnp.zeros((B, H, HEAD_DIM, HEAD_DIM), f32), jnp.zeros((B, H, HEAD_DIM), f32), jnp.zeros((B, H), f32))
    state, h = lax.scan(_mlstm_chunk, init, tuple(to_chunks(t) for t in (q, k, v, ig, lf)))
    return h.swapaxes(0, 1).reshape(B, S, H, HEAD_DIM), state


def _mlstm_readout(h, o_gate, norm_w):
    h = o_gate * h
    mu = jnp.mean(h, axis=-1, keepdims=True)
    var = jnp.mean(jnp.square(h - mu), axis=-1, keepdims=True)
    return (h - mu) * lax.rsqrt(var + LN_EPS) * norm_w.astype(jnp.float32).reshape(MLSTM_HEADS, HEAD_DIM)


def _fox_attend(q, k, v, fq, fk, pos_q, pos_k):
    s = jnp.einsum('bqhd,bkhd->bhqk', q, k) * ATTN_SCALE
    s = s + (fq.transpose(0, 2, 1)[..., None] - fk.transpose(0, 2, 1)[:, :, None, :])
    p = _masked_softmax(s, pos_k[None, :] <= pos_q[:, None])
    return jnp.einsum('bhqk,bkhd->bqhd', p.astype(v.dtype), v)


def _fox_prompt(q, k, v, F):
    B, S = q.shape[:2]
    qb = Q_BLOCK if S % Q_BLOCK == 0 else S
    nqb = S // qb
    pos = jnp.arange(S)

    def block(args):
        qc, fc, pc = args
        return _fox_attend(qc, k, v, fc, F, pc, pos)

    xs = (q.reshape(B, nqb, qb, FOX_HEADS, HEAD_DIM).swapaxes(0, 1),
          F.reshape(B, nqb, qb, FOX_HEADS).swapaxes(0, 1),
          pos.reshape(nqb, qb))
    return lax.map(block, xs).swapaxes(0, 1).reshape(B, S, FOX_HEADS, HEAD_DIM)


def _merge(gates, o_nsa, o_ml, o_fox, w_branch, w_out, dtype):
    B, L = o_nsa.shape[:2]
    br = jnp.stack([o.reshape(B, L, BRANCH_WIDTH).astype(dtype) for o in (o_nsa, o_ml, o_fox)], axis=2)
    proj = jnp.einsum('blmc,mcd->blmd', br, w_branch)
    y = jnp.einsum('blmd,blmd->bld', gates, proj).astype(dtype)
    return y @ w_out


def _moe(x, w_group, b_group, w_expert, b_expert, w_gate, w_up, w_down):
    B, L, D = x.shape
    t = x.reshape(B * L, D)
    pg = jax.nn.softmax((t @ w_group + b_group).astype(jnp.float32), axis=-1)
    g_val, g_idx = lax.top_k(pg, 1)
    le = (t @ w_expert + b_expert).astype(jnp.float32).reshape(-1, N_GROUPS, EXPERTS_PER_GROUP)
    le = jnp.take_along_axis(le, g_idx[:, :, None], axis=1)[:, 0]
    e_val, e_idx = lax.top_k(jax.nn.softmax(le, axis=-1), TOP_K_IN_GROUP)
    e_w = g_val * e_val / jnp.sum(e_val, axis=-1, keepdims=True)
    gate = jnp.sum(jax.nn.one_hot(g_idx * EXPERTS_PER_GROUP + e_idx, N_EXPERTS, dtype=jnp.float32) * e_w[..., None], axis=1)
    h = jax.nn.silu(jnp.einsum('td,edf->tef', t, w_gate)) * jnp.einsum('td,edf->tef', t, w_up)
    h = h * gate[:, :, None].astype(h.dtype)
    return jnp.einsum('tef,efd->td', h, w_down).reshape(B, L, D)


def _mixers_prompt(x, w_in, b_in, w_ck, w_cv, ml_norm_w, w_branch, w_out, slopes):
    B, S, _ = x.shape
    pr = _project(x, w_in, b_in)
    pos = jnp.arange(S)
    o_nsa = _nsa_prompt(pr['nsa_q'].reshape(B, S, -1), pr['nsa_rows'].reshape(B, S, -1),
                        pr['nsa_win'].reshape(B, S, -1), jnp.pad(pr['nsa_g_raw'], ((0, 0), (0, 0), (0, 116))),
                        w_ck, w_cv).reshape(B, S, NSA_HEADS, HEAD_DIM)
    h, (c, n, m) = _mlstm_prompt(pr['ml_q'], pr['ml_k'], pr['ml_v'], pr['ml_i'], pr['ml_lf'])
    o_ml = _mlstm_readout(h, pr['ml_o'], ml_norm_w)
    flat = lambda t: t.reshape(B, S, -1)
    o_fox = _fox_prompt_attn(flat(pr['fox_q']), flat(pr['fox_k']), flat(pr['fox_v']),
                             jnp.cumsum(pr['fox_lf'], axis=1)).reshape(B, S, FOX_HEADS, HEAD_DIM)
    out = _merge(pr['merge'], o_nsa, o_ml, o_fox, w_branch, w_out, x.dtype)
    w_keep = min(NSA_WINDOW, S)
    new = (pr['nsa_rows'], pr['nsa_win'][:, S - w_keep:], jnp.stack([pr['fox_k'], pr['fox_v']], axis=2),
           pr['fox_lf'], c, n, m)
    return out, new


def _mixers_sample(x, c_nsa, c_win, c_fox_kv, c_fox_lf, s_c, s_n, s_m, page_table,
                   w_in, b_in, w_ck, w_cv, ml_norm_w, w_branch, w_out, slopes):
    DB, T, _ = x.shape
    past = page_table.shape[1] * PAGE_SIZE
    f32 = jnp.float32
    pr = _project(x, w_in, b_in)
    pos_q = past + jnp.arange(T)
    rows_past = _gather_pages(c_nsa.reshape(-1, PAGE_SIZE, NSA_ROWS * HEAD_DIM), page_table)
    rows_past = rows_past.reshape(DB, past, NSA_ROWS, HEAD_DIM)
    rows_all = jnp.concatenate([rows_past, pr['nsa_rows'].astype(rows_past.dtype)], axis=1)
    o_cmp, o_sel = _nsa_cmp_sel(pr['nsa_q'], rows_all, pos_q, w_ck, w_cv, slopes)
    wb = c_win.shape[1]
    win_all = jnp.concatenate([c_win, pr['nsa_win'].astype(c_win.dtype)], axis=1)
    pos_k = past - wb + jnp.arange(wb + T)
    o_win = _nsa_win_attend(pr['nsa_q'], win_all[:, :, 0], win_all[:, :, 1], pos_q, pos_k, slopes)
    o_nsa = _nsa_combine(pr['nsa_g'], o_cmp, o_sel, o_win)
    (c, n, m), h = _mlstm_chunk((s_c.astype(f32), s_n.astype(f32), s_m.astype(f32)),
                                (pr['ml_q'], pr['ml_k'], pr['ml_v'], pr['ml_i'], pr['ml_lf']))
    o_ml = _mlstm_readout(h, pr['ml_o'], ml_norm_w)
    kv_past = _gather_pages(c_fox_kv.reshape(-1, PAGE_SIZE, 2 * FOX_HEADS * HEAD_DIM), page_table)
    kv_past = kv_past.reshape(DB, past, 2, FOX_HEADS, HEAD_DIM)
    k_all = jnp.concatenate([kv_past[:, :, 0], pr['fox_k'].astype(kv_past.dtype)], axis=1)
    v_all = jnp.concatenate([kv_past[:, :, 1], pr['fox_v'].astype(kv_past.dtype)], axis=1)
    lf_all = jnp.concatenate([c_fox_lf[page_table].reshape(DB, past, FOX_HEADS).astype(f32), pr['fox_lf']], axis=1)
    F = jnp.cumsum(lf_all, axis=1)
    o_fox = _fox_attend(pr['fox_q'], k_all, v_all, F[:, past:], F, pos_q, jnp.arange(past + T))
    out = _merge(pr['merge'], o_nsa, o_ml, o_fox, w_branch, w_out, x.dtype)
    new = (pr['nsa_rows'], win_all[:, T:], jnp.stack([pr['fox_k'], pr['fox_v']], axis=2),
           pr['fox_lf'], c, n, m)
    return out, new


def _stack_layers(states):
    return tuple(jnp.stack(list(a)) for a in zip(*states))


def kernel(x_prompt, x_sample, cache_nsa, cache_nsa_win, cache_fox_kv, cache_fox_logf,
           state_mlstm_c, state_mlstm_n, state_mlstm_m, page_table,
           ln_in_w, ln_in_b, w_in, b_in, nsa_w_ck, nsa_w_cv, mlstm_norm_w, w_branch, w_out,
           ln1_w, ln1_b, moe_w_group, moe_b_group, moe_w_expert, moe_b_expert,
           moe_w_gate, moe_w_up, moe_w_down, ln2_w, ln2_b):
    f32, bf16 = jnp.float32, jnp.bfloat16
    alpha = (2.0 * DEPTH) ** 0.25
    B, S, D = x_prompt.shape
    DB, T, _ = x_sample.shape
    TP, TS = B * S, DB * T
    HD = HEAD_DIM
    x = jnp.concatenate([x_prompt.reshape(TP, D), x_sample.reshape(TS, D)], axis=0)
    x = _layer_norm_rows(x, ln_in_w, ln_in_b)
    cat = lambda p, s: jnp.concatenate([p.reshape(TP, -1), s.reshape(TS, -1)], axis=0)
    cols = lambda a, c0, w: a[..., c0:c0 + w]
    zero_state = (jnp.zeros((B, MLSTM_HEADS, HD, HD), f32), jnp.zeros((B, MLSTM_HEADS, HD), f32),
                  jnp.zeros((B, MLSTM_HEADS), f32))
    new_p, new_s = [], []
    for l in range(DEPTH):
        z = _in_projection(x, w_in[l], b_in[l])
        zp = z[:TP].reshape(B, S, Z_WIDTH)
        zs = z[TP:].reshape(DB, T, Z_WIDTH)
        small_p, small_s = cols(zp, Z_SMALL, 128), cols(zs, Z_SMALL, 128)

        o_nsa_p = _nsa_prompt(cols(zp, Z_NQ, 256), cols(zp, Z_ROWS, 256), cols(zp, Z_WIN, 128), small_p,
                              nsa_w_ck[l], nsa_w_cv[l])
        o_nsa_s = _nsa_decode(cols(zs, Z_NQ, 256), cols(zs, Z_ROWS, 256), cols(zs, Z_WIN, 128), small_s,
                              cache_nsa, cache_nsa_win, page_table, nsa_w_ck[l], nsa_w_cv[l], l)

        def mlstm(zz, small, state, L, nb, dt):
            return _mlstm(cols(zz, Z_MQ, 256), cols(zz, Z_MK, 256) * ATTN_SCALE, cols(zz, Z_MV, 256),
                          cols(zz, Z_MO, 256), cols(small, SMALL_MI, MLSTM_HEADS), cols(small, SMALL_MF, MLSTM_HEADS),
                          mlstm_norm_w[l], *state, L, nb, dt)

        o_ml_p, st_ml_p = mlstm(zp, small_p, zero_state, 128, B, bf16)
        o_ml_s, st_ml_s = mlstm(zs, small_s, (state_mlstm_c[l], state_mlstm_n[l], state_mlstm_m[l]), T, 4, f32)

        lf_p = jax.nn.log_sigmoid(cols(small_p, SMALL_FF, FOX_HEADS))
        lf_s = jax.nn.log_sigmoid(cols(small_s, SMALL_FF, FOX_HEADS))
        o_fox_p = _fox_prompt_attn(cols(zp, Z_FQ, 256), cols(zp, Z_FK, 256), cols(zp, Z_FV, 256),
                                   jnp.cumsum(lf_p, axis=1))
        o_fox_s = _fox_decode(cols(zs, Z_FQ, 256), cols(zs, Z_FK, 256), cols(zs, Z_FV, 256), lf_s,
                              cache_fox_kv, cache_fox_logf, page_table, l)

        x = _merge_ln(x, z, cat(o_nsa_p, o_nsa_s), cat(o_ml_p, o_ml_s), cat(o_fox_p, o_fox_s),
                      w_branch[l], w_out[l], ln1_w[l], ln1_b[l], alpha)
        x = _moe_ln(x, moe_w_group[l], moe_b_group[l], moe_w_expert[l], moe_b_expert[l],
                    moe_w_gate[l], moe_w_up[l], moe_w_down[l], ln2_w[l], ln2_b[l], alpha)

        w_keep = min(NSA_WINDOW, S)
        new_p.append((cols(zp, Z_ROWS, 256).reshape(B, S, NSA_ROWS, HD),
                      cols(zp, Z_WIN, 128)[:, S - w_keep:].reshape(B, w_keep, 2, HD),
                      cols(zp, Z_FK, 512).reshape(B, S, 2, FOX_HEADS, HD), lf_p) + tuple(st_ml_p))
        win_new = cols(zs, Z_WIN, 128).reshape(DB, T, 2, HD).astype(cache_nsa_win.dtype)
        new_s.append((cols(zs, Z_ROWS, 256).reshape(DB, T, NSA_ROWS, HD),
                      jnp.concatenate([cache_nsa_win[l][:, T:], win_new], axis=1),
                      cols(zs, Z_FK, 512).reshape(DB, T, 2, FOX_HEADS, HD), lf_s) + tuple(st_ml_s))
    (p_nsa_rows, p_nsa_win, p_fox_kv, p_fox_logf, p_mlstm_c, p_mlstm_n, p_mlstm_m) = _stack_layers(new_p)
    (s_nsa_rows, s_nsa_win, s_fox_kv, s_fox_logf, s_mlstm_c, s_mlstm_n, s_mlstm_m) = _stack_layers(new_s)
    return (x[:TP].reshape(B, S, D), x[TP:].reshape(DB, T, D),
            p_nsa_rows, p_nsa_win, p_fox_kv, p_fox_logf, p_mlstm_c, p_mlstm_n, p_mlstm_m,
            s_nsa_rows, s_nsa_win, s_fox_kv, s_fox_logf, s_mlstm_c, s_mlstm_n, s_mlstm_m)
```

```python
import functools

import jax
import jax.numpy as jnp
import numpy as np
from jax import lax
from jax.experimental import pallas as pl
from jax.experimental.pallas import tpu as pltpu

D_MODEL = 1024
BATCH = 2
SEQ = 8192
DEPTH = 2
DEC_BATCH = 128
DEC_SEQ = 8
PAST_LEN = 2048
PAGE_SIZE = 128

HEAD_DIM = 64
NSA_HEADS = 4
NSA_BLOCK = 64
NSA_TOPK = 16
NSA_WINDOW = 512
NSA_ROWS = 4
MLSTM_HEADS = 4
MLSTM_CHUNK = 64
FOX_HEADS = 4
Q_BLOCK = 128
N_BRANCH = 3
BRANCH_WIDTH = NSA_HEADS * HEAD_DIM
N_GROUPS = 4
EXPERTS_PER_GROUP = 4
N_EXPERTS = N_GROUPS * EXPERTS_PER_GROUP
TOP_K_IN_GROUP = 2
D_EXPERT = 256
LN_EPS = 1e-5
NEG_INF = -1e30
TINY = 1e-30
FORCED_SCORE = 1e9
ATTN_SCALE = HEAD_DIM ** -0.5

IN_SPLITS = (
    NSA_HEADS * HEAD_DIM,
    6 * HEAD_DIM,
    NSA_HEADS * 3,
    MLSTM_HEADS * HEAD_DIM,
    MLSTM_HEADS * HEAD_DIM,
    MLSTM_HEADS * HEAD_DIM,
    MLSTM_HEADS * HEAD_DIM,
    MLSTM_HEADS,
    MLSTM_HEADS,
    FOX_HEADS * HEAD_DIM,
    FOX_HEADS * HEAD_DIM,
    FOX_HEADS * HEAD_DIM,
    FOX_HEADS,
    N_BRANCH * D_MODEL,
)


def _linear_kernel(x_ref, w_ref, b_ref, o_ref):
    x = x_ref[...].astype(jnp.bfloat16)
    o_ref[...] = jnp.dot(x, w_ref[...], preferred_element_type=jnp.float32) + b_ref[...]


def _pallas_linear(x, w, b, tm=512, tn=512):
    T, K = x.shape
    N = w.shape[1]
    n_pad = -(-N // tn) * tn
    wp = jnp.pad(w.astype(jnp.bfloat16), ((0, 0), (0, n_pad - N)))
    bp = jnp.pad(b.astype(jnp.float32), (0, n_pad - N)).reshape(1, n_pad)
    out = pl.pallas_call(
        _linear_kernel,
        grid=(T // tm, n_pad // tn),
        in_specs=[pl.BlockSpec((tm, K), lambda i, j: (i, 0)),
                  pl.BlockSpec((K, tn), lambda i, j: (0, j)),
                  pl.BlockSpec((1, tn), lambda i, j: (0, j))],
        out_specs=pl.BlockSpec((tm, tn), lambda i, j: (i, j)),
        out_shape=jax.ShapeDtypeStruct((T, n_pad), jnp.float32),
        name="linear",
    )(x, wp, bp)
    return out[:, :N]


VMEM_LIMIT_BYTES = 48 * 1024 * 1024
NSA_SLOPES = tuple(2.0 ** (-8.0 * (h + 1) / NSA_HEADS) for h in range(NSA_HEADS))
_NT = (((1,), (1,)), ((), ()))
_HI = lax.Precision.HIGHEST


def _sigmoid(x):
    return 1.0 / (1.0 + jnp.exp(-x))


def _tile_rows(x, n):
    return jnp.concatenate([x] * n, axis=0)


def _compress_kernel(x_ref, w_ref, o_ref, acc_ref):
    k = pl.program_id(1)

    @pl.when(k == 0)
    def _():
        acc_ref[...] = jnp.zeros_like(acc_ref)

    acc_ref[...] += jnp.dot(x_ref[...], w_ref[...], preferred_element_type=jnp.float32, precision=_HI)

    @pl.when(k == pl.num_programs(1) - 1)
    def _():
        o_ref[...] = acc_ref[...]


def _compress_weights(w_ck, w_cv):
    z = jnp.zeros_like(w_ck)
    wk = jnp.stack([w_ck, z, z, z], axis=1)
    wv = jnp.stack([z, w_cv, z, z], axis=1)
    return jnp.concatenate([wk, wv], axis=-1).reshape(NSA_BLOCK * NSA_ROWS * HEAD_DIM, 2 * HEAD_DIM)


def _nsa_compress_blocks(blocks, w_big, tm=256, tk=2048):
    n, kdim = blocks.shape
    tm = min(tm, n)
    return pl.pallas_call(
        _compress_kernel,
        grid=(n // tm, kdim // tk),
        in_specs=[pl.BlockSpec((tm, tk), lambda i, k: (i, k)),
                  pl.BlockSpec((tk, 2 * HEAD_DIM), lambda i, k: (k, 0))],
        out_specs=pl.BlockSpec((tm, 2 * HEAD_DIM), lambda i, k: (i, 0)),
        out_shape=jax.ShapeDtypeStruct((n, 2 * HEAD_DIM), jnp.float32),
        scratch_shapes=[pltpu.VMEM((tm, 2 * HEAD_DIM), jnp.float32)],
        name="nsa_compress",
    )(blocks, w_big)


def _softmax_step(carry, s, v):
    m, l, acc = carry
    m_new = jnp.maximum(m, jnp.max(s, axis=-1, keepdims=True))
    p = jnp.exp(s - m_new)
    a = jnp.exp(m - m_new)
    l = a * l + jnp.sum(p, axis=-1, keepdims=True)
    acc = a * acc + jnp.dot(p.astype(jnp.bfloat16), v, preferred_element_type=jnp.float32)
    return m_new, l, acc


def _select_blocks(imp, cur, n_pick):
    q, n_sb = imp.shape
    jq = lax.broadcasted_iota(jnp.int32, (q, n_sb), 1)
    jf = jq.astype(jnp.float32)
    score = jnp.where(jq > cur, NEG_INF, jnp.where((jq == cur) | (jq == 0), FORCED_SCORE, imp))

    valid = jq <= cur

    def pick(_, carry):
        work, sel = carry
        hit = jq == jnp.argmax(work, axis=-1, keepdims=True).astype(jnp.int32)
        sel = jnp.where(hit & valid, 1.0, sel)
        work = jnp.where(hit, -jnp.inf, work)
        return work, sel

    _, sel = lax.fori_loop(0, n_pick, pick, (score, jnp.zeros((q, n_sb), jnp.float32)))
    return sel


def _nsa_prompt_kernel(qs_ref, qf_ref, kc_ref, vc_ref, ksel_ref, vsel_ref, kwin_ref, vwin_ref, g_ref,
                       o_ref, flag_ref, *, tq, tk, tw):
    i = pl.program_id(1)
    f32, bf16 = jnp.float32, jnp.bfloat16
    H = NSA_HEADS
    R = H * tq
    qs = qs_ref[0].reshape(R, HEAD_DIM)
    qf = qf_ref[0].reshape(R, HEAD_DIM)
    row = lax.broadcasted_iota(jnp.int32, (R, 1), 0)
    head = row // tq
    slope = jnp.where(head == 0, NSA_SLOPES[0], jnp.where(head == 1, NSA_SLOPES[1],
                      jnp.where(head == 2, NSA_SLOPES[2], NSA_SLOPES[3]))).astype(f32)
    posq = i * tq + (row - head * tq)

    n_cb = kc_ref.shape[1]
    sc = lax.dot_general(qf, kc_ref[0], _NT, precision=_HI, preferred_element_type=f32) * ATTN_SCALE
    jb = lax.broadcasted_iota(jnp.int32, (R, n_cb), 1)
    distc = posq - (jb * NSA_BLOCK + NSA_BLOCK - 1)
    okc = distc >= 0
    sc = jnp.where(okc, sc - slope * distc.astype(f32), NEG_INF)
    pc = jnp.exp(sc - jnp.max(sc, axis=-1, keepdims=True)) * okc.astype(f32)
    pc = pc / jnp.maximum(jnp.sum(pc, axis=-1, keepdims=True), TINY)
    o_cmp = jnp.dot(pc.astype(bf16), vc_ref[0].astype(bf16), preferred_element_type=f32)
    imp = pc[0:tq] + pc[tq:2 * tq] + pc[2 * tq:3 * tq] + pc[3 * tq:4 * tq]

    pq = i * tq + lax.broadcasted_iota(jnp.int32, (tq, 1), 0)
    msel = _select_blocks(imp, pq // NSA_BLOCK, NSA_TOPK)

    bpt = tk // NSA_BLOCK
    blk_any = jnp.max(msel, axis=0, keepdims=True)
    for j in range(n_cb // bpt):
        flag_ref[j] = (jnp.max(blk_any[:, j * bpt:(j + 1) * bpt]) > 0.5).astype(jnp.int32)
    msel = msel.astype(bf16)

    rowpos = i * tq + lax.broadcasted_iota(jnp.int32, (tq, 1), 0)

    def attend(carry, s, v1):
        m, acc = carry
        m_new = jnp.maximum(m, jnp.max(s, axis=-1, keepdims=True))
        p = jnp.exp(s - m_new).astype(bf16)
        return m_new, jnp.exp(m - m_new) * acc + jnp.dot(p, v1, preferred_element_type=f32)

    init = (jnp.full((R, 1), NEG_INF, f32), jnp.zeros((R, 2 * HEAD_DIM), f32))

    def sel_tile(j, carry):
        k0 = pl.multiple_of(j * tk, tk)
        k = ksel_ref[0, pl.ds(k0, tk), :]
        s = lax.dot_general(qs, k, _NT, preferred_element_type=f32)
        eb = (lax.broadcasted_iota(jnp.int32, (n_cb, tk), 0)
              == j * bpt + lax.broadcasted_iota(jnp.int32, (n_cb, tk), 1) // NSA_BLOCK)
        mexp = jnp.dot(msel, eb.astype(bf16), preferred_element_type=f32)
        d = rowpos - (k0 + lax.broadcasted_iota(jnp.int32, (tq, tk), 1))
        ok = (mexp > 0.5) & (d >= 0)
        s = jnp.where(_tile_rows(ok, H), s - slope * _tile_rows(d.astype(f32), H), NEG_INF)
        return attend(carry, s, vsel_ref[0, pl.ds(k0, tk), :])

    def sel_body(j, carry):
        return lax.cond(flag_ref[j] > 0, lambda c: sel_tile(j, c), lambda c: c, carry)

    n_sel = (i * tq + tq - 1) // tk + 1
    _, a_sel = lax.fori_loop(0, n_sel, sel_body, init)
    o_sel = a_sel[:, :HEAD_DIM] / a_sel[:, HEAD_DIM:]

    nw = NSA_WINDOW + tq
    w0 = pl.multiple_of(jnp.maximum(i * tq - NSA_WINDOW, 0), tw)
    sw = lax.dot_general(qs, kwin_ref[0, pl.ds(w0, nw), :], _NT, preferred_element_type=f32)
    dw = rowpos - (w0 + lax.broadcasted_iota(jnp.int32, (tq, nw), 1))
    okw = (dw >= 0) & (dw < NSA_WINDOW)
    sw = jnp.where(_tile_rows(okw, H), sw - slope * _tile_rows(dw.astype(f32), H), NEG_INF)
    _, a_win = attend(init, sw, vwin_ref[0, pl.ds(w0, nw), :])
    o_win = a_win[:, :HEAD_DIM] / a_win[:, HEAD_DIM:]

    g = _sigmoid(g_ref[...])
    gate = lambda c: jnp.concatenate([g[:, 3 * h + c:3 * h + c + 1] for h in range(H)], axis=0)
    o = gate(0) * o_cmp + gate(1) * o_sel + gate(2) * o_win
    o_ref[0] = o.reshape(H, tq, HEAD_DIM)


def _heads_major(x, dtype):
    B, S, _ = x.shape
    return x.reshape(B, S, -1, HEAD_DIM).transpose(0, 2, 1, 3).astype(dtype)


def _nsa_prompt(nq, rows, win, gates, w_ck, w_cv, tq=256, tk=256, tw=128):
    B, S, _ = nq.shape
    bf16 = jnp.bfloat16
    kcvc = _nsa_compress_blocks(rows.reshape(B * S // NSA_BLOCK, -1), _compress_weights(w_ck, w_cv))
    kcvc = kcvc.reshape(B, S // NSA_BLOCK, 2 * HEAD_DIM)
    kc, vc = kcvc[..., :HEAD_DIM], kcvc[..., HEAD_DIM:]
    qs = _heads_major(nq * ATTN_SCALE, bf16)
    qf = _heads_major(nq, jnp.float32)
    col = lambda a, c: a[..., c * HEAD_DIM:(c + 1) * HEAD_DIM].astype(bf16)
    with_ones = lambda v: jnp.concatenate([v, jnp.ones_like(v)], axis=-1)
    full = lambda n, w: pl.BlockSpec((1, n, w), lambda b, i: (b, 0, 0))
    qspec = pl.BlockSpec((1, NSA_HEADS, tq, HEAD_DIM), lambda b, i: (b, 0, i, 0))
    n_cb = S // NSA_BLOCK
    o = pl.pallas_call(
        functools.partial(_nsa_prompt_kernel, tq=tq, tk=tk, tw=tw),
        grid=(B, S // tq),
        in_specs=[qspec, qspec, full(n_cb, HEAD_DIM), full(n_cb, HEAD_DIM), full(S, HEAD_DIM),
                  full(S, 2 * HEAD_DIM), full(S, HEAD_DIM), full(S, 2 * HEAD_DIM),
                  pl.BlockSpec((None, tq, 128), lambda b, i: (b, i, 0))],
        out_specs=qspec,
        out_shape=jax.ShapeDtypeStruct((B, NSA_HEADS, S, HEAD_DIM), jnp.float32),
        scratch_shapes=[pltpu.SMEM((S // tk,), jnp.int32)],
        compiler_params=pltpu.CompilerParams(dimension_semantics=("parallel", "arbitrary"),
                                             vmem_limit_bytes=VMEM_LIMIT_BYTES),
        name="nsa_prompt",
    )(qs, qf, kc, vc, col(rows, 2), with_ones(col(rows, 3)), col(win, 0), with_ones(col(win, 1)), gates)
    return o.transpose(0, 2, 1, 3).reshape(B, S, BRANCH_WIDTH)


def _fox_prompt_kernel(q_ref, k_ref, v_ref, fq_ref, fk_ref, o_ref, *, tq, t):
    i = pl.program_id(1)
    f32, bf16 = jnp.float32, jnp.bfloat16
    lane = lax.broadcasted_iota(jnp.int32, (1, 2 * HEAD_DIM), 1)
    low = lane < HEAD_DIM
    rc = lax.broadcasted_iota(jnp.int32, (tq, t), 0) - lax.broadcasted_iota(jnp.int32, (tq, t), 1)
    n_pairs = FOX_HEADS // 2
    pair_cols = [slice(hp * 2 * HEAD_DIM, (hp + 1) * 2 * HEAD_DIM) for hp in range(n_pairs)]
    q_h, fq_h = [], []
    for hp in range(n_pairs):
        q2 = q_ref[0, :, pair_cols[hp]]
        zq = jnp.zeros_like(q2)
        q_h += [jnp.where(low, q2, zq), jnp.where(low, zq, q2)]
        fq_h += [fq_ref[0, :, 2 * hp:2 * hp + 1], fq_ref[0, :, 2 * hp + 1:2 * hp + 2]]

    def body(j, carry, diag_offset):
        k0 = pl.multiple_of(j * t, t)
        fk = fk_ref[0, j]
        out = []
        for hp in range(n_pairs):
            k2 = k_ref[0, pl.ds(k0, t), pair_cols[hp]]
            v2 = v_ref[0, pl.ds(k0, t), pair_cols[hp]]
            one = jnp.ones_like(v2)
            v1 = (jnp.where(low, v2, one), jnp.where(low, one, v2))
            for x in range(2):
                h = 2 * hp + x
                m, acc = carry[h]
                s = lax.dot_general(q_h[h], k2, _NT, preferred_element_type=f32) + (fq_h[h] - fk[h:h + 1])
                if diag_offset is not None:
                    s = jnp.where(rc >= diag_offset, s, NEG_INF)
                m_new = jnp.maximum(m, jnp.max(s, axis=-1, keepdims=True))
                p = jnp.exp(s - m_new).astype(bf16)
                acc = jnp.exp(m - m_new) * acc + jnp.dot(p, v1[x], preferred_element_type=f32)
                out.append((m_new, acc))
        return tuple(out)

    init = tuple((jnp.full((tq, 1), NEG_INF, f32), jnp.zeros((tq, 2 * HEAD_DIM), f32)) for _ in range(FOX_HEADS))
    n_full = i * (tq // t)
    carry = lax.fori_loop(0, n_full, functools.partial(body, diag_offset=None), init)
    for jj in range(tq // t):
        carry = body(n_full + jj, carry, jj * t)
    for hp in range(n_pairs):
        acc_a, acc_b = carry[2 * hp][1], carry[2 * hp + 1][1]
        den = pltpu.roll(jnp.where(low, acc_b, acc_a), HEAD_DIM, axis=1)
        o_ref[0, :, pair_cols[hp]] = jnp.where(low, acc_a, acc_b) / den


def _fox_prompt_attn(q, k, v, F, tq=1024, t=256):
    B, S, W = q.shape
    bf16 = jnp.bfloat16
    fk = jnp.pad(F.transpose(0, 2, 1), ((0, 0), (0, 8 - FOX_HEADS), (0, 0)))
    fk = fk.reshape(B, 8, S // t, t).transpose(0, 2, 1, 3)
    full = pl.BlockSpec((1, S, W), lambda b, i: (b, 0, 0))
    return pl.pallas_call(
        functools.partial(_fox_prompt_kernel, tq=tq, t=t),
        grid=(B, S // tq),
        in_specs=[pl.BlockSpec((1, tq, W), lambda b, i: (b, i, 0)), full, full,
                  pl.BlockSpec((1, tq, FOX_HEADS), lambda b, i: (b, i, 0)),
                  pl.BlockSpec((1, S // t, 8, t), lambda b, i: (b, 0, 0, 0))],
        out_specs=pl.BlockSpec((1, tq, W), lambda b, i: (b, i, 0)),
        out_shape=jax.ShapeDtypeStruct((B, S, W), jnp.float32),
        compiler_params=pltpu.CompilerParams(dimension_semantics=("parallel", "arbitrary"),
                                             vmem_limit_bytes=VMEM_LIMIT_BYTES),
        name="fox_prompt",
    )((q * ATTN_SCALE).astype(bf16), k.astype(bf16), v.astype(bf16), F, fk)


def _log_sigmoid(x):
    return jnp.minimum(x, 0.0) - jnp.log1p(jnp.exp(-jnp.abs(x)))


def _mlstm_kernel(q_ref, k_ref, kt_ref, v_ref, og_ref, gc_ref, gr_ref, nw_ref, cn0_ref, m0_ref,
                  h_ref, cn_ref, m_ref, cn_s, m_s, *, nb, L):
    c = pl.program_id(1)
    f32 = jnp.float32
    W = 2 * HEAD_DIM
    n_pairs = MLSTM_HEADS // 2

    @pl.when(c == 0)
    def _():
        cn_s[...] = cn0_ref[...]
        m_s[...] = m0_ref[...]

    lane = lax.broadcasted_iota(jnp.int32, (1, W), 1)
    low = lane < HEAD_DIM
    ti = lax.broadcasted_iota(jnp.int32, (L, L), 0)
    si = lax.broadcasted_iota(jnp.int32, (L, L), 1)
    causal = si <= ti
    tri = causal.astype(f32)
    tri_t = (ti <= si).astype(f32)
    srow = lax.broadcasted_iota(jnp.int32, (W, 2 * W), 0)
    slane = lax.broadcasted_iota(jnp.int32, (W, 2 * W), 1)
    top = srow < HEAD_DIM
    keep_a = top & ((slane < HEAD_DIM) | (slane == W))
    keep_b = (~top) & (((slane >= HEAD_DIM) & (slane < W)) | (slane == W + 1))
    lane_w = lax.broadcasted_iota(jnp.int32, (1, W), 1)
    mdt = k_ref.dtype

    for b in range(nb):
        gcol = gc_ref[b]
        grow = gr_ref[b]
        bcol = jnp.dot(tri, _log_sigmoid(gcol), precision=_HI, preferred_element_type=f32)
        brow = jnp.dot(_log_sigmoid(grow), tri_t, precision=_HI, preferred_element_type=f32)
        for hp in range(n_pairs):
            cols = slice(hp * W, (hp + 1) * W)
            q2 = q_ref[b, :, cols]
            k2 = k_ref[b, :, cols]
            v2 = v_ref[b, :, cols]
            kt2 = kt_ref[b, cols, :]
            cn = cn_s[b, hp]
            r = jnp.dot(q2, cn.astype(mdt), preferred_element_type=f32)
            zq = jnp.zeros_like(q2)
            per_head = []
            for x in range(2):
                h = 2 * hp + x
                qx = jnp.where(low, q2, zq) if x == 0 else jnp.where(low, zq, q2)
                b_c = bcol[:, MLSTM_HEADS + h:MLSTM_HEADS + h + 1]
                b_r = brow[MLSTM_HEADS + h:MLSTM_HEADS + h + 1, :]
                ig_c = gcol[:, h:h + 1]
                ig_r = grow[h:h + 1, :]
                m_prev = m_s[b, 0:1, h:h + 1]
                dmat = jnp.where(causal, b_c - b_r + ig_r, NEG_INF)
                a_c = b_c + m_prev
                m_t = jnp.maximum(a_c, jnp.max(dmat, axis=-1, keepdims=True))
                wq = jnp.exp(dmat - m_t) * lax.dot_general(qx, k2, _NT, preferred_element_type=f32)
                inter = jnp.exp(a_c - m_t)
                wv = jnp.dot(wq.astype(mdt), v2, preferred_element_type=f32)
                den = inter * r[:, W + x:W + x + 1] + jnp.sum(wq, axis=-1, keepdims=True)
                den = jnp.maximum(jnp.abs(den), jnp.exp(-m_t))
                bl = b_c[L - 1:L, :]
                g_c = bl - b_c + ig_c
                m_new = jnp.maximum(bl + m_prev, jnp.max(g_c, axis=0, keepdims=True))
                ws = jnp.exp(g_c - m_new)
                decay = jnp.exp(bl + m_prev - m_new)
                aug = jnp.concatenate([v2.astype(f32) * ws, jnp.where(lane_w == x, ws, 0.0)], axis=1)
                u = jnp.dot(kt2, aug.astype(mdt), preferred_element_type=f32)
                per_head.append((inter, wv, den, decay, u))
                m_s[b, 0:1, h:h + 1] = m_new
            (ia, wva, dena, deca, ua), (ib, wvb, denb, decb, ub) = per_head
            num = jnp.where(low, ia * r[:, :W] + wva, ib * r[:, :W] + wvb)
            hid = num / jnp.where(low, dena, denb)
            cn_s[b, hp] = (jnp.where(top, deca, decb) * cn + jnp.where(keep_a, ua, 0.0)
                           + jnp.where(keep_b, ub, 0.0))
            hid = _sigmoid(og_ref[b, :, cols]) * hid
            mean = lambda t: jnp.where(low, jnp.sum(jnp.where(low, t, 0.0), axis=-1, keepdims=True),
                                       jnp.sum(jnp.where(low, 0.0, t), axis=-1, keepdims=True)) / HEAD_DIM
            mu = mean(hid)
            var = mean(jnp.square(hid - mu))
            h_ref[b, :, cols] = (hid - mu) * lax.rsqrt(var + LN_EPS) * nw_ref[:, cols]

    @pl.when(c == pl.num_programs(1) - 1)
    def _():
        cn_ref[...] = cn_s[...]
        m_ref[...] = m_s[...]


def _mlstm_state_pack(c, n, m):
    Bx = c.shape[0]
    W = 2 * HEAD_DIM
    cn = jnp.zeros((Bx, MLSTM_HEADS // 2, W, 2 * W), jnp.float32)
    for hp in range(MLSTM_HEADS // 2):
        for x in range(2):
            r0 = x * HEAD_DIM
            cn = cn.at[:, hp, r0:r0 + HEAD_DIM, r0:r0 + HEAD_DIM].set(c[:, 2 * hp + x])
            cn = cn.at[:, hp, r0:r0 + HEAD_DIM, W + x].set(n[:, 2 * hp + x])
    m8 = jnp.zeros((Bx, 8, 128), jnp.float32).at[:, 0, :MLSTM_HEADS].set(m)
    return cn, m8


def _mlstm_state_unpack(cn, m8):
    W = 2 * HEAD_DIM
    c = jnp.stack([cn[:, h // 2, (h % 2) * HEAD_DIM:(h % 2 + 1) * HEAD_DIM,
                      (h % 2) * HEAD_DIM:(h % 2 + 1) * HEAD_DIM] for h in range(MLSTM_HEADS)], axis=1)
    n = jnp.stack([cn[:, h // 2, (h % 2) * HEAD_DIM:(h % 2 + 1) * HEAD_DIM, W + h % 2]
                   for h in range(MLSTM_HEADS)], axis=1)
    return c, n, m8[:, 0, :MLSTM_HEADS]


def _mlstm(q, k, v, og, mi, mf, norm_w, c0, n0, m0, L, nb, mxu_dtype):
    Bx, S, W4 = q.shape
    f32 = jnp.float32
    gates = jnp.concatenate([mi, mf], axis=-1).astype(f32)
    gcol = jnp.pad(gates, ((0, 0), (0, 0), (0, 128 - 2 * MLSTM_HEADS)))
    grow = gates.transpose(0, 2, 1)
    cn0, m8 = _mlstm_state_pack(c0.astype(f32), n0.astype(f32), m0.astype(f32))
    tok = lambda w: pl.BlockSpec((nb, L, w), lambda b, c: (b, c, 0))
    st_cn = pl.BlockSpec((nb, MLSTM_HEADS // 2, 128, 256), lambda b, c: (b, 0, 0, 0))
    st_m = pl.BlockSpec((nb, 8, 128), lambda b, c: (b, 0, 0))
    h, cn, m8 = pl.pallas_call(
        functools.partial(_mlstm_kernel, nb=nb, L=L),
        grid=(Bx // nb, S // L),
        in_specs=[tok(W4), tok(W4), pl.BlockSpec((nb, W4, L), lambda b, c: (b, 0, c)), tok(W4), tok(W4),
                  tok(128), pl.BlockSpec((nb, 8, L), lambda b, c: (b, 0, c)),
                  pl.BlockSpec((1, W4), lambda b, c: (0, 0)), st_cn, st_m],
        out_specs=[tok(W4), st_cn, st_m],
        out_shape=[jax.ShapeDtypeStruct((Bx, S, W4), f32),
                   jax.ShapeDtypeStruct(cn0.shape, f32), jax.ShapeDtypeStruct(m8.shape, f32)],
        scratch_shapes=[pltpu.VMEM((nb, MLSTM_HEADS // 2, 128, 256), f32), pltpu.VMEM((nb, 8, 128), f32)],
        compiler_params=pltpu.CompilerParams(dimension_semantics=("parallel", "arbitrary"),
                                             vmem_limit_bytes=VMEM_LIMIT_BYTES),
        name="mlstm",
    )(q.astype(mxu_dtype), k.astype(mxu_dtype), k.astype(mxu_dtype).transpose(0, 2, 1), v.astype(mxu_dtype),
      og.astype(f32), gcol, grow, norm_w.astype(f32).reshape(1, W4), cn0, m8)
    return h, _mlstm_state_unpack(cn, m8)


def _ln(x, w, b):
    mu = jnp.mean(x, axis=-1, keepdims=True)
    var = jnp.mean(jnp.square(x - mu), axis=-1, keepdims=True)
    return (x - mu) * lax.rsqrt(var + LN_EPS) * w + b


def _ln_kernel(x_ref, w_ref, b_ref, o_ref):
    o_ref[...] = _ln(x_ref[...], w_ref[...], b_ref[...])


def _layer_norm_rows(x, w, b, tm=1024):
    T, D = x.shape
    vec = pl.BlockSpec((1, D), lambda i: (0, 0))
    return pl.pallas_call(
        _ln_kernel, grid=(T // tm,),
        in_specs=[pl.BlockSpec((tm, D), lambda i: (i, 0)), vec, vec],
        out_specs=pl.BlockSpec((tm, D), lambda i: (i, 0)),
        out_shape=jax.ShapeDtypeStruct((T, D), jnp.float32),
        name="layer_norm",
    )(x, w.reshape(1, D), b.reshape(1, D))


Z_GM = 0
Z_NQ = 3072
Z_ROWS = Z_NQ + 256
Z_MQ = 3584
Z_MK = Z_MQ + 256
Z_MV = 4096
Z_MO = Z_MV + 256
Z_FQ = 4608
Z_WIN = Z_FQ + 256
Z_SMALL = Z_WIN + 128
Z_FK = 5120
Z_FV = Z_FK + 256
Z_WIDTH = 5632
Z_TILE = 512
SMALL_MI, SMALL_MF, SMALL_FF = 12, 16, 20


def _z_column_order():
    starts = np.concatenate([[0], np.cumsum(IN_SPLITS)])
    seg = lambda i, lo=0, hi=None: np.arange(starts[i] + lo, starts[i] + (IN_SPLITS[i] if hi is None else hi))
    pad = lambda n: np.full((n,), -1)
    order = np.concatenate([
        seg(13), seg(0), seg(1, 0, 256), seg(3), seg(4), seg(5), seg(6),
        seg(9), seg(1, 256, 384), seg(2), seg(7), seg(8), seg(12), pad(128 - 24), seg(10), seg(11)])
    assert order.shape == (Z_WIDTH,)
    return order


def _permute_in_proj(w_in, b_in):
    order = _z_column_order()
    valid = jnp.asarray(order >= 0)
    idx = jnp.asarray(np.maximum(order, 0))
    w = jnp.where(valid[None, :], w_in[:, idx], 0.0)
    b = jnp.where(valid, b_in[idx], 0.0)
    return w, b


def _proj_kernel(x_ref, w_ref, wlo_ref, b_ref, o_ref, xh_s, xl_s, *, hi_tile):
    j = pl.program_id(1)
    f32 = jnp.float32

    @pl.when(j == 0)
    def _():
        x = x_ref[...]
        xh = x.astype(jnp.bfloat16)
        xh_s[...] = xh
        xl_s[...] = (x - xh.astype(f32)).astype(jnp.bfloat16)

    @pl.when(j != hi_tile)
    def _():
        o_ref[...] = jnp.dot(xh_s[...], w_ref[...], preferred_element_type=f32) + b_ref[...]

    @pl.when(j == hi_tile)
    def _():
        acc = jnp.dot(xh_s[...], wlo_ref[...], preferred_element_type=f32)
        acc += jnp.dot(xl_s[...], w_ref[...], preferred_element_type=f32)
        acc += jnp.dot(xh_s[...], w_ref[...], preferred_element_type=f32)
        o_ref[...] = acc + b_ref[...]


def _in_projection(x, w_in, b_in, tm=1024):
    T, D = x.shape
    w, b = _permute_in_proj(w_in, b_in)
    wh = w.astype(jnp.bfloat16)
    hi_tile = Z_NQ // Z_TILE
    wlo = (w[:, Z_NQ:Z_NQ + Z_TILE] - wh[:, Z_NQ:Z_NQ + Z_TILE].astype(jnp.float32)).astype(jnp.bfloat16)
    return pl.pallas_call(
        functools.partial(_proj_kernel, hi_tile=hi_tile),
        grid=(T // tm, Z_WIDTH // Z_TILE),
        in_specs=[pl.BlockSpec((tm, D), lambda i, j: (i, 0)),
                  pl.BlockSpec((D, Z_TILE), lambda i, j: (0, j)),
                  pl.BlockSpec((D, Z_TILE), lambda i, j: (0, 0)),
                  pl.BlockSpec((1, Z_TILE), lambda i, j: (0, j))],
        out_specs=pl.BlockSpec((tm, Z_TILE), lambda i, j: (i, j)),
        out_shape=jax.ShapeDtypeStruct((T, Z_WIDTH), jnp.float32),
        scratch_shapes=[pltpu.VMEM((tm, D), jnp.bfloat16), pltpu.VMEM((tm, D), jnp.bfloat16)],
        compiler_params=pltpu.CompilerParams(dimension_semantics=("parallel", "arbitrary"),
                                             vmem_limit_bytes=VMEM_LIMIT_BYTES),
        name="in_projection",
    )(x, wh, wlo, b.reshape(1, Z_WIDTH))


def _merge_kernel(x_ref, gm_ref, on_ref, om_ref, of_ref, wb_ref, wo_ref, lw_ref, lb_ref, o_ref, *, alpha):
    f32, bf16 = jnp.float32, jnp.bfloat16
    y = None
    for m, br in enumerate((on_ref, om_ref, of_ref)):
        proj = jnp.dot(br[...].astype(bf16), wb_ref[m], preferred_element_type=f32)
        term = _sigmoid(gm_ref[:, m * D_MODEL:(m + 1) * D_MODEL]) * proj
        y = term if y is None else y + term
    mix = jnp.dot(y.astype(bf16), wo_ref[...], preferred_element_type=f32)
    o_ref[...] = _ln(alpha * x_ref[...] + mix, lw_ref[...], lb_ref[...])


def _merge_ln(x, z, o_nsa, o_ml, o_fox, w_branch, w_out, ln_w, ln_b, alpha, tm=512):
    T, D = x.shape
    tok = lambda w: pl.BlockSpec((tm, w), lambda i: (i, 0))
    vec = pl.BlockSpec((1, D), lambda i: (0, 0))
    return pl.pallas_call(
        functools.partial(_merge_kernel, alpha=alpha),
        grid=(T // tm,),
        in_specs=[tok(D), tok(N_BRANCH * D), tok(BRANCH_WIDTH), tok(BRANCH_WIDTH), tok(BRANCH_WIDTH),
                  pl.BlockSpec((N_BRANCH, BRANCH_WIDTH, D), lambda i: (0, 0, 0)),
                  pl.BlockSpec((D, D), lambda i: (0, 0)), vec, vec],
        out_specs=tok(D),
        out_shape=jax.ShapeDtypeStruct((T, D), jnp.float32),
        compiler_params=pltpu.CompilerParams(dimension_semantics=("parallel",),
                                             vmem_limit_bytes=VMEM_LIMIT_BYTES),
        name="merge_ln",
    )(x, z, o_nsa, o_ml, o_fox, w_branch.astype(jnp.bfloat16), w_out.astype(jnp.bfloat16),
      ln_w.reshape(1, D), ln_b.reshape(1, D))


def _route(logits):
    tm = logits.shape[0]
    lane = lax.broadcasted_iota(jnp.int32, (tm, 128), 1)
    lanef = lane.astype(jnp.float32)
    big = 1e9
    is_g = lane < N_GROUPS
    lg = jnp.where(is_g, logits, -jnp.inf)
    eg = jnp.exp(lg - jnp.max(lg, axis=-1, keepdims=True))
    pg = eg / jnp.sum(eg, axis=-1, keepdims=True)
    g_val = jnp.max(pg, axis=-1, keepdims=True)
    g_idx = jnp.min(jnp.where(is_g & (pg == g_val), lanef, big), axis=-1, keepdims=True)
    e_lo = N_GROUPS + EXPERTS_PER_GROUP * g_idx
    in_grp = (lanef >= e_lo) & (lanef < e_lo + EXPERTS_PER_GROUP)
    le = jnp.where(in_grp, logits, -jnp.inf)
    ee = jnp.exp(le - jnp.max(le, axis=-1, keepdims=True))
    pe = ee / jnp.sum(ee, axis=-1, keepdims=True)
    v1 = jnp.max(pe, axis=-1, keepdims=True)
    i1 = jnp.min(jnp.where(in_grp & (pe == v1), lanef, big), axis=-1, keepdims=True)
    rest = in_grp & (lanef != i1)
    pe2 = jnp.where(rest, pe, -1.0)
    v2 = jnp.max(pe2, axis=-1, keepdims=True)
    i2 = jnp.min(jnp.where(rest & (pe2 == v2), lanef, big), axis=-1, keepdims=True)
    tot = v1 + v2
    return jnp.where(lanef == i1, g_val * v1 / tot, jnp.where(lanef == i2, g_val * v2 / tot, 0.0))


def _moe_kernel(x_ref, wr_ref, br_ref, wgu_ref, wd_ref, lw_ref, lb_ref, o_ref, xb_s, gate_s, acc_s, *, alpha):
    e = pl.program_id(1)
    f32, bf16 = jnp.float32, jnp.bfloat16

    @pl.when(e == 0)
    def _():
        x = x_ref[...]
        xb_s[...] = x.astype(bf16)
        logits = jnp.dot(x, wr_ref[...], precision=_HI, preferred_element_type=f32) + br_ref[...]
        gate_s[...] = _route(logits)
        acc_s[...] = jnp.zeros_like(acc_s)

    lane = lax.broadcasted_iota(jnp.int32, gate_s.shape, 1)
    gate = jnp.sum(jnp.where(lane == N_GROUPS + e, gate_s[...], 0.0), axis=-1, keepdims=True)
    gu = jnp.dot(xb_s[...], wgu_ref[0], preferred_element_type=f32)
    g, u = gu[:, :D_EXPERT], gu[:, D_EXPERT:]
    h = (g * _sigmoid(g)) * u * gate
    acc_s[...] += jnp.dot(h.astype(bf16), wd_ref[0], preferred_element_type=f32)

    @pl.when(e == pl.num_programs(1) - 1)
    def _():
        o_ref[...] = _ln(alpha * x_ref[...] + acc_s[...], lw_ref[...], lb_ref[...])


def _moe_ln(x, w_group, b_group, w_expert, b_expert, w_gate, w_up, w_down, ln_w, ln_b, alpha, tm=1024):
    T, D = x.shape
    bf16 = jnp.bfloat16
    n_r = N_GROUPS + N_EXPERTS
    wr = jnp.pad(jnp.concatenate([w_group, w_expert], axis=1), ((0, 0), (0, 128 - n_r)))
    br = jnp.pad(jnp.concatenate([b_group, b_expert]), (0, 128 - n_r)).reshape(1, 128)
    wgu = jnp.concatenate([w_gate, w_up], axis=-1).astype(bf16)
    vec = pl.BlockSpec((1, D), lambda i, e: (0, 0))
    return pl.pallas_call(
        functools.partial(_moe_kernel, alpha=alpha),
        grid=(T // tm, N_EXPERTS),
        in_specs=[pl.BlockSpec((tm, D), lambda i, e: (i, 0)),
                  pl.BlockSpec((D, 128), lambda i, e: (0, 0)),
                  pl.BlockSpec((1, 128), lambda i, e: (0, 0)),
                  pl.BlockSpec((1, D, 2 * D_EXPERT), lambda i, e: (e, 0, 0)),
                  pl.BlockSpec((1, D_EXPERT, D), lambda i, e: (e, 0, 0)), vec, vec],
        out_specs=pl.BlockSpec((tm, D), lambda i, e: (i, 0)),
        out_shape=jax.ShapeDtypeStruct((T, D), jnp.float32),
        scratch_shapes=[pltpu.VMEM((tm, D), bf16), pltpu.VMEM((tm, 128), jnp.float32),
                        pltpu.VMEM((tm, D), jnp.float32)],
        compiler_params=pltpu.CompilerParams(dimension_semantics=("parallel", "arbitrary"),
                                             vmem_limit_bytes=VMEM_LIMIT_BYTES),
        name="moe_ln",
    )(x, wr, br, wgu, w_down.astype(bf16), ln_w.reshape(1, D), ln_b.reshape(1, D))


N_PAGES = PAST_LEN // PAGE_SIZE
NEW_PAD = 128
DEC_KEYS = PAST_LEN + NEW_PAD
DEC_ROWS = NSA_HEADS * DEC_SEQ


def _pages_token_minor(cache):
    nd = cache.ndim
    t = cache.transpose((0, 1) + tuple(range(3, nd)) + (2,))
    return t.reshape(cache.shape[0] * cache.shape[1], -1, cache.shape[2])


def _page_specs(rows, row_block, layer, n_phys):
    def spec(p):
        return pl.BlockSpec((1, rows, PAGE_SIZE),
                            lambda b, pt: (layer * n_phys + pt[b * N_PAGES + p], row_block, 0))
    return [spec(p) for p in range(N_PAGES)]


def _softmax_rows(s):
    m = jnp.max(s, axis=-1, keepdims=True)
    p = jnp.exp(s - m)
    return p, jnp.sum(p, axis=-1, keepdims=True)


def _fox_decode_kernel(pt_ref, qbd_ref, knew_ref, vnew_ref, fk_ref, fq_ref, *refs):
    pages, o_ref = refs[:N_PAGES], refs[N_PAGES]
    f32, bf16 = jnp.float32, jnp.bfloat16
    W = FOX_HEADS * HEAD_DIM
    qbd = qbd_ref[0]
    s = [jnp.dot(qbd, pg[0, :W, :].astype(bf16), preferred_element_type=f32) for pg in pages]
    s.append(jnp.dot(qbd, knew_ref[0], preferred_element_type=f32))
    s = jnp.concatenate(s, axis=1)
    rowh = lax.broadcasted_iota(jnp.int32, (DEC_ROWS, 1), 0) // DEC_SEQ
    fk = fk_ref[0]
    fk_rows = jnp.where(rowh == 0, fk[0:1], jnp.where(rowh == 1, fk[1:2], jnp.where(rowh == 2, fk[2:3], fk[3:4])))
    col = lax.broadcasted_iota(jnp.int32, (DEC_ROWS, DEC_KEYS), 1)
    t = lax.broadcasted_iota(jnp.int32, (DEC_ROWS, DEC_KEYS), 0) % DEC_SEQ
    ok = (col < PAST_LEN) | (col - PAST_LEN <= t)
    s = jnp.where(ok, s + (fq_ref[0] - fk_rows), NEG_INF)
    p, l = _softmax_rows(s)
    pb = p.astype(bf16)
    o = lax.dot_general(pb[:, PAST_LEN:], vnew_ref[0], _NT, preferred_element_type=f32)
    for i, pg in enumerate(pages):
        o += lax.dot_general(pb[:, i * PAGE_SIZE:(i + 1) * PAGE_SIZE], pg[0, W:, :].astype(bf16), _NT,
                             preferred_element_type=f32)
    o = o / l
    lane_h = lax.broadcasted_iota(jnp.int32, (DEC_ROWS, W), 1) // HEAD_DIM
    o = jnp.where(lane_h == rowh, o, 0.0)
    o_ref[0] = o[0:8] + o[8:16] + o[16:24] + o[24:32]


def _pad_new_t(x):
    return jnp.pad(x.transpose(0, 2, 1), ((0, 0), (0, 0), (0, NEW_PAD - DEC_SEQ))).astype(jnp.bfloat16)


def _fox_decode(q, k_new, v_new, lf_new, cache_kv, cache_lf, page_table, layer):
    DB = q.shape[0]
    f32, bf16 = jnp.float32, jnp.bfloat16
    W = FOX_HEADS * HEAD_DIM
    eye = jnp.eye(FOX_HEADS, dtype=f32)
    qh = (q * ATTN_SCALE).reshape(DB, DEC_SEQ, FOX_HEADS, HEAD_DIM).transpose(0, 2, 1, 3)
    qbd = (qh[:, :, :, None, :] * eye[None, :, None, :, None]).reshape(DB, DEC_ROWS, W).astype(bf16)
    lf_all = jnp.concatenate([cache_lf[layer][page_table].reshape(DB, PAST_LEN, FOX_HEADS).astype(f32), lf_new], axis=1)
    F = jnp.cumsum(lf_all, axis=1)
    fk = jnp.pad(F.transpose(0, 2, 1), ((0, 0), (0, 8 - FOX_HEADS), (0, DEC_KEYS - PAST_LEN - DEC_SEQ)))
    fq = F[:, PAST_LEN:].transpose(0, 2, 1).reshape(DB, DEC_ROWS, 1)
    per_seq = lambda r, w: pl.BlockSpec((1, r, w), lambda b, pt: (b, 0, 0))
    pages = _pages_token_minor(cache_kv)
    return pl.pallas_call(
        _fox_decode_kernel,
        grid_spec=pltpu.PrefetchScalarGridSpec(
            num_scalar_prefetch=1, grid=(DB,),
            in_specs=[per_seq(DEC_ROWS, W), per_seq(W, NEW_PAD), per_seq(W, NEW_PAD), per_seq(8, DEC_KEYS),
                      per_seq(DEC_ROWS, 1)] + _page_specs(2 * W, 0, layer, cache_kv.shape[1]),
            out_specs=per_seq(DEC_SEQ, W)),
        out_shape=jax.ShapeDtypeStruct((DB, DEC_SEQ, W), f32),
        compiler_params=pltpu.CompilerParams(dimension_semantics=("parallel",), vmem_limit_bytes=VMEM_LIMIT_BYTES),
        name="fox_decode",
    )(page_table.reshape(-1), qbd, _pad_new_t(k_new), _pad_new_t(v_new), fk, fq, *([pages] * N_PAGES))


def _nsa_decode_kernel(pt_ref, qs_ref, qf_ref, kcvc_ref, rnew_ref, wbuf_ref, wnew_ref, g_ref, e_ref, *refs):
    pages, o_ref = refs[:N_PAGES], refs[N_PAGES]
    f32, bf16 = jnp.float32, jnp.bfloat16
    H, T, HD = NSA_HEADS, DEC_SEQ, HEAD_DIM
    qs = qs_ref[0]
    row = lax.broadcasted_iota(jnp.int32, (DEC_ROWS, 1), 0)
    head = row // T
    slope = jnp.where(head == 0, NSA_SLOPES[0], jnp.where(head == 1, NSA_SLOPES[1],
                      jnp.where(head == 2, NSA_SLOPES[2], NSA_SLOPES[3]))).astype(f32)
    posq = PAST_LEN + row % T

    kcvc = kcvc_ref[0]
    n_cb = kcvc.shape[0]
    sc = lax.dot_general(qf_ref[0], kcvc, _NT, precision=_HI, preferred_element_type=f32) * ATTN_SCALE
    jb = lax.broadcasted_iota(jnp.int32, (DEC_ROWS, n_cb), 1)
    distc = posq - (jb * NSA_BLOCK + NSA_BLOCK - 1)
    okc = distc >= 0
    sc = jnp.where(okc, sc - slope * distc.astype(f32), NEG_INF)
    pc = jnp.exp(sc - jnp.max(sc, axis=-1, keepdims=True)) * okc.astype(f32)
    pc = pc / jnp.maximum(jnp.sum(pc, axis=-1, keepdims=True), TINY)
    o_cmp = jnp.dot(pc.astype(bf16), kcvc.astype(bf16), preferred_element_type=f32)[:, HD:]
    imp = pc[0:T] + pc[T:2 * T] + pc[2 * T:3 * T] + pc[3 * T:4 * T]
    imp = jnp.concatenate([imp, jnp.zeros((T, 128 - n_cb), f32)], axis=1)
    cur = (PAST_LEN + lax.broadcasted_iota(jnp.int32, (T, 1), 0)) // NSA_BLOCK
    msel = _select_blocks(imp, cur, NSA_TOPK).astype(bf16)

    mexp = jnp.dot(msel, e_ref[...], preferred_element_type=f32)
    col = lax.broadcasted_iota(jnp.int32, (T, DEC_KEYS), 1)
    tq = lax.broadcasted_iota(jnp.int32, (T, DEC_KEYS), 0)
    d = PAST_LEN + tq - col
    ok = (mexp > 0.5) & (d >= 0)
    kv = [pg[0].astype(bf16) for pg in pages] + [rnew_ref[0]]
    s = jnp.concatenate([jnp.dot(qs, x[:HD], preferred_element_type=f32) for x in kv], axis=1)
    s = jnp.where(_tile_rows(ok, H), s - slope * _tile_rows(d.astype(f32), H), NEG_INF)
    p, l = _softmax_rows(s)
    pb = p.astype(bf16)
    acc = jnp.zeros((DEC_ROWS, HD), f32)
    for i, x in enumerate(kv):
        acc += lax.dot_general(pb[:, i * PAGE_SIZE:(i + 1) * PAGE_SIZE], x[HD:], _NT, preferred_element_type=f32)
    o_sel = acc / l

    wb = wbuf_ref.shape[2]
    kvw = [wbuf_ref[0].astype(bf16), wnew_ref[0]]
    sw = jnp.concatenate([jnp.dot(qs, x[:HD], preferred_element_type=f32) for x in kvw], axis=1)
    colw = lax.broadcasted_iota(jnp.int32, (T, wb + NEW_PAD), 1)
    tw = lax.broadcasted_iota(jnp.int32, (T, wb + NEW_PAD), 0)
    dw = wb + tw - colw
    okw = (dw >= 0) & (dw < NSA_WINDOW)
    sw = jnp.where(_tile_rows(okw, H), sw - slope * _tile_rows(dw.astype(f32), H), NEG_INF)
    pw, lw = _softmax_rows(sw)
    pwb = pw.astype(bf16)
    accw = (lax.dot_general(pwb[:, :wb], kvw[0][HD:], _NT, preferred_element_type=f32)
            + lax.dot_general(pwb[:, wb:], kvw[1][HD:], _NT, preferred_element_type=f32))
    o_win = accw / lw

    g = _sigmoid(g_ref[0])
    gate = lambda c: jnp.concatenate([g[:, 3 * h + c:3 * h + c + 1] for h in range(H)], axis=0)
    o_ref[0] = gate(0) * o_cmp + gate(1) * o_sel + gate(2) * o_win


def _compress_pages_kernel(x_ref, wk_ref, wv_ref, o_ref):
    tm = x_ref.shape[0]
    f32 = jnp.float32
    acc_k = jnp.zeros((tm, 2 * HEAD_DIM), f32)
    acc_v = jnp.zeros((tm, 2 * HEAD_DIM), f32)
    for d in range(HEAD_DIM):
        acc_k += jnp.dot(x_ref[:, d, :], wk_ref[d], precision=_HI, preferred_element_type=f32)
        acc_v += jnp.dot(x_ref[:, HEAD_DIM + d, :], wv_ref[d], preferred_element_type=f32)
    o_ref[...] = jnp.concatenate([acc_k, acc_v], axis=1)


def _nsa_compress_pages(pages, w_ck, w_cv, layer, n_phys, tm=64):
    eye = jnp.eye(PAGE_SIZE // NSA_BLOCK, dtype=jnp.float32)
    big = lambda w: jnp.einsum('pde,bc->dbpce', w, eye).reshape(HEAD_DIM, PAGE_SIZE, 2 * HEAD_DIM)
    wspec = pl.BlockSpec((HEAD_DIM, PAGE_SIZE, 2 * HEAD_DIM), lambda i: (0, 0, 0))
    return pl.pallas_call(
        _compress_pages_kernel,
        grid=(n_phys // tm,),
        in_specs=[pl.BlockSpec((tm, 2 * HEAD_DIM, PAGE_SIZE), lambda i: (layer * (n_phys // tm) + i, 0, 0)),
                  wspec, wspec],
        out_specs=pl.BlockSpec((tm, 4 * HEAD_DIM), lambda i: (i, 0)),
        out_shape=jax.ShapeDtypeStruct((n_phys, 4 * HEAD_DIM), jnp.float32),
        compiler_params=pltpu.CompilerParams(dimension_semantics=("parallel",), vmem_limit_bytes=VMEM_LIMIT_BYTES),
        name="nsa_compress_pages",
    )(pages, big(w_ck), big(w_cv))


def _nsa_decode(nq, rows_new, win_new, gates, cache_rows, cache_win, page_table, w_ck, w_cv, layer):
    DB = nq.shape[0]
    f32, bf16 = jnp.float32, jnp.bfloat16
    HD = HEAD_DIM
    n_phys = cache_rows.shape[1]
    n_blk = PAGE_SIZE // NSA_BLOCK
    pages = _pages_token_minor(cache_rows)
    kcvc = _nsa_compress_pages(pages, w_ck, w_cv, layer, n_phys)[page_table]
    kcvc = kcvc.reshape(DB, N_PAGES, 2, n_blk, HD).transpose(0, 1, 3, 2, 4).reshape(DB, N_PAGES * n_blk, 2 * HD)
    stack = lambda x: x.reshape(DB, DEC_SEQ, NSA_HEADS, HD).transpose(0, 2, 1, 3).reshape(DB, DEC_ROWS, HD)
    qs = stack(nq * ATTN_SCALE).astype(bf16)
    qf = jnp.pad(stack(nq), ((0, 0), (0, 0), (0, HD))).astype(f32)
    win_t = cache_win.transpose(0, 1, 3, 4, 2).reshape(cache_win.shape[0] * DB, 2 * HD, cache_win.shape[2])
    colk = np.arange(DEC_KEYS)
    e = (np.arange(128)[:, None] == colk[None, :] // NSA_BLOCK) & (colk[None, :] < PAST_LEN + NSA_BLOCK)
    wb = cache_win.shape[2]
    per_seq = lambda r, w: pl.BlockSpec((1, r, w), lambda b, pt: (b, 0, 0))
    o = pl.pallas_call(
        _nsa_decode_kernel,
        grid_spec=pltpu.PrefetchScalarGridSpec(
            num_scalar_prefetch=1, grid=(DB,),
            in_specs=[per_seq(DEC_ROWS, HD), per_seq(DEC_ROWS, 2 * HD), per_seq(PAST_LEN // NSA_BLOCK, 2 * HD),
                      per_seq(2 * HD, NEW_PAD),
                      pl.BlockSpec((1, 2 * HD, wb), lambda b, pt: (layer * DB + b, 0, 0)),
                      per_seq(2 * HD, NEW_PAD), per_seq(DEC_SEQ, 128),
                      pl.BlockSpec((128, DEC_KEYS), lambda b, pt: (0, 0))]
                     + _page_specs(2 * HD, 1, layer, n_phys),
            out_specs=per_seq(DEC_ROWS, HD)),
        out_shape=jax.ShapeDtypeStruct((DB, DEC_ROWS, HD), f32),
        compiler_params=pltpu.CompilerParams(dimension_semantics=("parallel",), vmem_limit_bytes=VMEM_LIMIT_BYTES),
        name="nsa_decode",
    )(page_table.reshape(-1), qs, qf, kcvc, _pad_new_t(rows_new[..., 2 * HD:]), win_t,
      _pad_new_t(win_new), gates, jnp.asarray(e, bf16), *([pages] * N_PAGES))
    return o.reshape(DB, NSA_HEADS, DEC_SEQ, HD).transpose(0, 2, 1, 3).reshape(DB, DEC_SEQ, BRANCH_WIDTH)


def _copy_kernel(pt_ref, x_ref, o_ref):
    o_ref[...] = x_ref[...]


def _gather_pages(cache, page_table):
    db, n_pages = page_table.shape
    _, page, w = cache.shape
    out = pl.pallas_call(
        _copy_kernel,
        grid_spec=pltpu.PrefetchScalarGridSpec(
            num_scalar_prefetch=1, grid=(db * n_pages,),
            in_specs=[pl.BlockSpec((1, page, w), lambda i, pt: (pt[i], 0, 0))],
            out_specs=pl.BlockSpec((1, page, w), lambda i, pt: (i, 0, 0))),
        out_shape=jax.ShapeDtypeStruct((db * n_pages, page, w), cache.dtype),
        name="gather_pages",
    )(page_table.reshape(-1), cache)
    return out.reshape(db, n_pages * page, w)


def _split_points():
    return [int(v) for v in np.cumsum(IN_SPLITS)[:-1]]


def _layer_norm(x, w, b):
    xf = x.astype(jnp.float32)
    mu = jnp.mean(xf, axis=-1, keepdims=True)
    var = jnp.mean(jnp.square(xf - mu), axis=-1, keepdims=True)
    y = (xf - mu) * lax.rsqrt(var + LN_EPS) * w.astype(jnp.float32) + b.astype(jnp.float32)
    return y.astype(x.dtype)


def _masked_softmax(s, mask):
    s = jnp.where(mask, s.astype(jnp.float32), NEG_INF)
    p = jnp.exp(s - jnp.max(s, axis=-1, keepdims=True)) * mask
    return p / jnp.maximum(jnp.sum(p, axis=-1, keepdims=True), TINY)


def _alibi_slopes(n):
    return jnp.exp2(-8.0 * jnp.arange(1, n + 1, dtype=jnp.float32) / n)


def _project(x, w_in, b_in):
    B, L, _ = x.shape
    f32 = jnp.float32
    z = _pallas_linear(x.reshape(B * L, -1), w_in, b_in).reshape(B, L, -1)
    (nq, nkv, ng, mq, mk, mv, mo, mi, mf, fq, fk, fv, ff, gm) = jnp.split(z, _split_points(), axis=-1)
    heads = lambda t, h: t.reshape(B, L, h, HEAD_DIM)
    rows = nkv.reshape(B, L, 6, HEAD_DIM)
    return dict(
        nsa_q=heads(nq, NSA_HEADS),
        nsa_rows=rows[:, :, :NSA_ROWS],
        nsa_win=rows[:, :, NSA_ROWS:],
        nsa_g=jax.nn.sigmoid(ng.astype(f32)).reshape(B, L, NSA_HEADS, 3),
        nsa_g_raw=ng,
        ml_q=heads(mq, MLSTM_HEADS).astype(f32),
        ml_k=heads(mk, MLSTM_HEADS).astype(f32) * ATTN_SCALE,
        ml_v=heads(mv, MLSTM_HEADS).astype(f32),
        ml_o=jax.nn.sigmoid(mo.astype(f32)).reshape(B, L, MLSTM_HEADS, HEAD_DIM),
        ml_i=mi.astype(f32),
        ml_lf=jax.nn.log_sigmoid(mf.astype(f32)),
        fox_q=heads(fq, FOX_HEADS),
        fox_k=heads(fk, FOX_HEADS),
        fox_v=heads(fv, FOX_HEADS),
        fox_lf=jax.nn.log_sigmoid(ff.astype(f32)),
        merge=jax.nn.sigmoid(gm.astype(f32)).reshape(B, L, N_BRANCH, D_MODEL))


def _nsa_compress(k, w):
    B, L, _ = k.shape
    n_cb = L // NSA_BLOCK
    blocks = k[:, :n_cb * NSA_BLOCK].reshape(B, n_cb, NSA_BLOCK, HEAD_DIM)
    return jnp.einsum('bjpd,pde->bje', blocks, w)


def _nsa_cmp_attend(q, kc, vc, pos_q, slopes):
    n_cb = kc.shape[1]
    blk_end = jnp.arange(n_cb) * NSA_BLOCK + NSA_BLOCK - 1
    dist = pos_q[:, None] - blk_end[None, :]
    s = jnp.einsum('bqhd,bjd->bhqj', q, kc) * ATTN_SCALE - slopes[:, None, None] * dist.astype(jnp.float32)
    p = _masked_softmax(s, dist >= 0)
    o = jnp.einsum('bhqj,bjd->bqhd', p.astype(vc.dtype), vc)
    return o, jnp.sum(p, axis=1)


def _nsa_select_blocks(imp, pos_q, n_sb):
    n_cb = imp.shape[-1]
    imp = jnp.pad(imp, ((0, 0), (0, 0), (0, n_sb - n_cb)))
    j = jnp.arange(n_sb)[None, :]
    cur = (pos_q // NSA_BLOCK)[:, None]
    score = jnp.where(j > cur, NEG_INF, jnp.where((j == cur) | (j == 0), FORCED_SCORE, imp))
    vals, idx = lax.top_k(score, min(NSA_TOPK, n_sb))
    return idx, vals > 0.5 * NEG_INF


def _nsa_sel_attend(q, kb, vb, idx, valid, pos_q, slopes):
    B, Q = q.shape[:2]
    bi = jnp.arange(B)[:, None, None]
    kg = kb[bi, idx].reshape(B, Q, -1, HEAD_DIM)
    vg = vb[bi, idx].reshape(B, Q, -1, HEAD_DIM)
    pos_k = (idx[..., None] * NSA_BLOCK + jnp.arange(NSA_BLOCK)).reshape(B, Q, -1)
    ok = jnp.repeat(valid, NSA_BLOCK, axis=-1) & (pos_k <= pos_q[None, :, None])
    dist = (pos_q[None, :, None] - pos_k).astype(jnp.float32)
    s = jnp.einsum('bqhd,bqkd->bhqk', q, kg) * ATTN_SCALE - slopes[None, :, None, None] * dist[:, None]
    p = _masked_softmax(s, ok[:, None])
    return jnp.einsum('bhqk,bqkd->bqhd', p.astype(vg.dtype), vg)


def _nsa_cmp_sel(q, rows, pos_q, w_ck, w_cv, slopes):
    B, Q = q.shape[:2]
    L = rows.shape[1]
    kc = _nsa_compress(rows[:, :, 0], w_ck)
    vc = _nsa_compress(rows[:, :, 1], w_cv)
    o_cmp, imp = _nsa_cmp_attend(q, kc, vc, pos_q, slopes)
    n_sb = -(-L // NSA_BLOCK)
    idx, valid = _nsa_select_blocks(imp, pos_q, n_sb)
    pad = ((0, 0), (0, n_sb * NSA_BLOCK - L), (0, 0))
    kb = jnp.pad(rows[:, :, 2], pad).reshape(B, n_sb, NSA_BLOCK, HEAD_DIM)
    vb = jnp.pad(rows[:, :, 3], pad).reshape(B, n_sb, NSA_BLOCK, HEAD_DIM)
    qb = Q_BLOCK if Q % Q_BLOCK == 0 else Q
    nqb = Q // qb

    def block(args):
        qc, ic, okc, pc = args
        return _nsa_sel_attend(qc, kb, vb, ic, okc, pc, slopes)

    xs = (q.reshape(B, nqb, qb, NSA_HEADS, HEAD_DIM).swapaxes(0, 1),
          idx.reshape(B, nqb, qb, -1).swapaxes(0, 1),
          valid.reshape(B, nqb, qb, -1).swapaxes(0, 1),
          pos_q.reshape(nqb, qb))
    o_sel = lax.map(block, xs).swapaxes(0, 1).reshape(B, Q, NSA_HEADS, HEAD_DIM)
    return o_cmp, o_sel


def _nsa_win_attend(q, k, v, pos_q, pos_k, slopes):
    dist = pos_q[:, None] - pos_k[None, :]
    ok = (dist >= 0) & (dist < NSA_WINDOW) & (pos_k[None, :] >= 0)
    s = jnp.einsum('bqhd,bkd->bhqk', q, k) * ATTN_SCALE - slopes[:, None, None] * dist.astype(jnp.float32)
    p = _masked_softmax(s, ok)
    return jnp.einsum('bhqk,bkd->bqhd', p.astype(v.dtype), v)


def _nsa_win_prompt(q, k, v, slopes):
    B, S = q.shape[:2]
    nqb = S // Q_BLOCK
    nb = NSA_WINDOW // Q_BLOCK
    pad = ((0, 0), (NSA_WINDOW, 0), (0, 0))
    kp = jnp.pad(k, pad).reshape(B, nqb + nb, Q_BLOCK, HEAD_DIM)
    vp = jnp.pad(v, pad).reshape(B, nqb + nb, Q_BLOCK, HEAD_DIM)
    band = jnp.arange(nqb)[:, None] + jnp.arange(nb + 1)[None, :]
    kband = kp[:, band].reshape(B, nqb, (nb + 1) * Q_BLOCK, HEAD_DIM)
    vband = vp[:, band].reshape(B, nqb, (nb + 1) * Q_BLOCK, HEAD_DIM)
    qblk = q.reshape(B, nqb, Q_BLOCK, NSA_HEADS, HEAD_DIM)
    pos_q = jnp.arange(S).reshape(nqb, Q_BLOCK)
    pos_k = (jnp.arange(nqb) * Q_BLOCK - NSA_WINDOW)[:, None] + jnp.arange((nb + 1) * Q_BLOCK)[None, :]
    o = jax.vmap(_nsa_win_attend, in_axes=(1, 1, 1, 0, 0, None), out_axes=1)(qblk, kband, vband, pos_q, pos_k, slopes)
    return o.reshape(B, S, NSA_HEADS, HEAD_DIM)


def _nsa_combine(g, o_cmp, o_sel, o_win):
    return jnp.einsum('blhc,cblhd->blhd', g, jnp.stack([o_cmp, o_sel, o_win]))


def _mlstm_chunk(carry, xs):
    c, n, m = carry
    q, k, v, ig, lf = xs
    L = q.shape[1]
    b = jnp.cumsum(lf, axis=1)
    causal = jnp.tril(jnp.ones((L, L), bool))[None, :, :, None]
    dmat = jnp.where(causal, b[:, :, None, :] - b[:, None, :, :] + ig[:, None, :, :], NEG_INF)
    a = b + m[:, None, :]
    m_t = jnp.maximum(a, jnp.max(dmat, axis=2))
    wq = jnp.exp(dmat - m_t[:, :, None, :]) * jnp.einsum('bthd,bshd->btsh', q, k)
    inter = jnp.exp(a - m_t)
    num = inter[..., None] * jnp.einsum('bthd,bhde->bthe', q, c) + jnp.einsum('btsh,bshe->bthe', wq, v)
    den = inter * jnp.einsum('bthd,bhd->bth', q, n) + jnp.sum(wq, axis=2)
    h = num / jnp.maximum(jnp.abs(den), jnp.exp(-m_t))[..., None]
    bl = b[:, -1]
    g = bl[:, None, :] - b + ig
    m_new = jnp.maximum(bl + m, jnp.max(g, axis=1))
    ws = jnp.exp(g - m_new[:, None, :])
    decay = jnp.exp(bl + m - m_new)
    c_new = decay[..., None, None] * c + jnp.einsum('bsh,bshd,bshe->bhde', ws, k, v)
    n_new = decay[..., None] * n + jnp.einsum('bsh,bshd->bhd', ws, k)
    return (c_new, n_new, m_new), h


def _mlstm_prompt(q, k, v, ig, lf):
    B, S, H, _ = q.shape
    ch = MLSTM_CHUNK if S % MLSTM_CHUNK == 0 else S
    nc = S // ch
    to_chunks = lambda t: t.reshape((B, nc, ch) + t.shape[2:]).swapaxes(0, 1)
    f32 = jnp.float32
    init = (jnp.zeros((B, H, HEAD_DIM, HEAD_DIM), f32), jnp.zeros((B, H, HEAD_DIM), f32), jnp.zeros((B, H), f32))
    state, h = lax.scan(_mlstm_chunk, init, tuple(to_chunks(t) for t in (q, k, v, ig, lf)))
    return h.swapaxes(0, 1).reshape(B, S, H, HEAD_DIM), state


def _mlstm_readout(h, o_gate, norm_w):
    h = o_gate * h
    mu = jnp.mean(h, axis=-1, keepdims=True)
    var = jnp.mean(jnp.square(h - mu), axis=-1, keepdims=True)
    return (h - mu) * lax.rsqrt(var + LN_EPS) * norm_w.astype(jnp.float32).reshape(MLSTM_HEADS, HEAD_DIM)


def _fox_attend(q, k, v, fq, fk, pos_q, pos_k):
    s = jnp.einsum('bqhd,bkhd->bhqk', q, k) * ATTN_SCALE
    s = s + (fq.transpose(0, 2, 1)[..., None] - fk.transpose(0, 2, 1)[:, :, None, :])
    p = _masked_softmax(s, pos_k[None, :] <= pos_q[:, None])
    return jnp.einsum('bhqk,bkhd->bqhd', p.astype(v.dtype), v)


def _fox_prompt(q, k, v, F):
    B, S = q.shape[:2]
    qb = Q_BLOCK if S % Q_BLOCK == 0 else S
    nqb = S // qb
    pos = jnp.arange(S)

    def block(args):
        qc, fc, pc = args
        return _fox_attend(qc, k, v, fc, F, pc, pos)

    xs = (q.reshape(B, nqb, qb, FOX_HEADS, HEAD_DIM).swapaxes(0, 1),
          F.reshape(B, nqb, qb, FOX_HEADS).swapaxes(0, 1),
          pos.reshape(nqb, qb))
    return lax.map(block, xs).swapaxes(0, 1).reshape(B, S, FOX_HEADS, HEAD_DIM)


def _merge(gates, o_nsa, o_ml, o_fox, w_branch, w_out, dtype):
    B, L = o_nsa.shape[:2]
    br = jnp.stack([o.reshape(B, L, BRANCH_WIDTH).astype(dtype) for o in (o_nsa, o_ml, o_fox)], axis=2)
    proj = jnp.einsum('blmc,mcd->blmd', br, w_branch)
    y = jnp.einsum('blmd,blmd->bld', gates, proj).astype(dtype)
    return y @ w_out


def _moe(x, w_group, b_group, w_expert, b_expert, w_gate, w_up, w_down):
    B, L, D = x.shape
    t = x.reshape(B * L, D)
    pg = jax.nn.softmax((t @ w_group + b_group).astype(jnp.float32), axis=-1)
    g_val, g_idx = lax.top_k(pg, 1)
    le = (t @ w_expert + b_expert).astype(jnp.float32).reshape(-1, N_GROUPS, EXPERTS_PER_GROUP)
    le = jnp.take_along_axis(le, g_idx[:, :, None], axis=1)[:, 0]
    e_val, e_idx = lax.top_k(jax.nn.softmax(le, axis=-1), TOP_K_IN_GROUP)
    e_w = g_val * e_val / jnp.sum(e_val, axis=-1, keepdims=True)
    gate = jnp.sum(jax.nn.one_hot(g_idx * EXPERTS_PER_GROUP + e_idx, N_EXPERTS, dtype=jnp.float32) * e_w[..., None], axis=1)
    h = jax.nn.silu(jnp.einsum('td,edf->tef', t, w_gate)) * jnp.einsum('td,edf->tef', t, w_up)
    h = h * gate[:, :, None].astype(h.dtype)
    return jnp.einsum('tef,efd->td', h, w_down).reshape(B, L, D)


def _mixers_prompt(x, w_in, b_in, w_ck, w_cv, ml_norm_w, w_branch, w_out, slopes):
    B, S, _ = x.shape
    pr = _project(x, w_in, b_in)
    pos = jnp.arange(S)
    o_nsa = _nsa_prompt(pr['nsa_q'].reshape(B, S, -1), pr['nsa_rows'].reshape(B, S, -1),
                        pr['nsa_win'].reshape(B, S, -1), jnp.pad(pr['nsa_g_raw'], ((0, 0), (0, 0), (0, 116))),
                        w_ck, w_cv).reshape(B, S, NSA_HEADS, HEAD_DIM)
    h, (c, n, m) = _mlstm_prompt(pr['ml_q'], pr['ml_k'], pr['ml_v'], pr['ml_i'], pr['ml_lf'])
    o_ml = _mlstm_readout(h, pr['ml_o'], ml_norm_w)
    flat = lambda t: t.reshape(B, S, -1)
    o_fox = _fox_prompt_attn(flat(pr['fox_q']), flat(pr['fox_k']), flat(pr['fox_v']),
                             jnp.cumsum(pr['fox_lf'], axis=1)).reshape(B, S, FOX_HEADS, HEAD_DIM)
    out = _merge(pr['merge'], o_nsa, o_ml, o_fox, w_branch, w_out, x.dtype)
    w_keep = min(NSA_WINDOW, S)
    new = (pr['nsa_rows'], pr['nsa_win'][:, S - w_keep:], jnp.stack([pr['fox_k'], pr['fox_v']], axis=2),
           pr['fox_lf'], c, n, m)
    return out, new


def _mixers_sample(x, c_nsa, c_win, c_fox_kv, c_fox_lf, s_c, s_n, s_m, page_table,
                   w_in, b_in, w_ck, w_cv, ml_norm_w, w_branch, w_out, slopes):
    DB, T, _ = x.shape
    past = page_table.shape[1] * PAGE_SIZE
    f32 = jnp.float32
    pr = _project(x, w_in, b_in)
    pos_q = past + jnp.arange(T)
    rows_past = _gather_pages(c_nsa.reshape(-1, PAGE_SIZE, NSA_ROWS * HEAD_DIM), page_table)
    rows_past = rows_past.reshape(DB, past, NSA_ROWS, HEAD_DIM)
    rows_all = jnp.concatenate([rows_past, pr['nsa_rows'].astype(rows_past.dtype)], axis=1)
    o_cmp, o_sel = _nsa_cmp_sel(pr['nsa_q'], rows_all, pos_q, w_ck, w_cv, slopes)
    wb = c_win.shape[1]
    win_all = jnp.concatenate([c_win, pr['nsa_win'].astype(c_win.dtype)], axis=1)
    pos_k = past - wb + jnp.arange(wb + T)
    o_win = _nsa_win_attend(pr['nsa_q'], win_all[:, :, 0], win_all[:, :, 1], pos_q, pos_k, slopes)
    o_nsa = _nsa_combine(pr['nsa_g'], o_cmp, o_sel, o_win)
    (c, n, m), h = _mlstm_chunk((s_c.astype(f32), s_n.astype(f32), s_m.astype(f32)),
                                (pr['ml_q'], pr['ml_k'], pr['ml_v'], pr['ml_i'], pr['ml_lf']))
    o_ml = _mlstm_readout(h, pr['ml_o'], ml_norm_w)
    kv_past = _gather_pages(c_fox_kv.reshape(-1, PAGE_SIZE, 2 * FOX_HEADS * HEAD_DIM), page_table)
    kv_past = kv_past.reshape(DB, past, 2, FOX_HEADS, HEAD_DIM)
    k_all = jnp.concatenate([kv_past[:, :, 0], pr['fox_k'].astype(kv_past.dtype)], axis=1)
    v_all = jnp.concatenate([kv_past[:, :, 1], pr['fox_v'].astype(kv_past.dtype)], axis=1)
    lf_all = jnp.concatenate([c_fox_lf[page_table].reshape(DB, past, FOX_HEADS).astype(f32), pr['fox_lf']], axis=1)
    F = jnp.cumsum(lf_all, axis=1)
    o_fox = _fox_attend(pr['fox_q'], k_all, v_all, F[:, past:], F, pos_q, jnp.arange(past + T))
    out = _merge(pr['merge'], o_nsa, o_ml, o_fox, w_branch, w_out, x.dtype)
    new = (pr['nsa_rows'], win_all[:, T:], jnp.stack([pr['fox_k'], pr['fox_v']], axis=2),
           pr['fox_lf'], c, n, m)
    return out, new


def _stack_layers(states):
    return tuple(jnp.stack(list(a)) for a in zip(*states))


def kernel(x_prompt, x_sample, cache_nsa, cache_nsa_win, cache_fox_kv, cache_fox_logf,
           state_mlstm_c, state_mlstm_n, state_mlstm_m, page_table,
           ln_in_w, ln_in_b, w_in, b_in, nsa_w_ck, nsa_w_cv, mlstm_norm_w, w_branch, w_out,
           ln1_w, ln1_b, moe_w_group, moe_b_group, moe_w_expert, moe_b_expert,
           moe_w_gate, moe_w_up, moe_w_down, ln2_w, ln2_b):
    f32, bf16 = jnp.float32, jnp.bfloat16
    alpha = (2.0 * DEPTH) ** 0.25
    B, S, D = x_prompt.shape
    DB, T, _ = x_sample.shape
    TP, TS = B * S, DB * T
    HD = HEAD_DIM
    x = jnp.concatenate([x_prompt.reshape(TP, D), x_sample.reshape(TS, D)], axis=0)
    x = _layer_norm_rows(x, ln_in_w, ln_in_b)
    cat = lambda p, s: jnp.concatenate([p.reshape(TP, -1), s.reshape(TS, -1)], axis=0)
    cols = lambda a, c0, w: a[..., c0:c0 + w]
    zero_state = (jnp.zeros((B, MLSTM_HEADS, HD, HD), f32), jnp.zeros((B, MLSTM_HEADS, HD), f32),
                  jnp.zeros((B, MLSTM_HEADS), f32))
    new_p, new_s = [], []
    for l in range(DEPTH):
        z = _in_projection(x, w_in[l], b_in[l])
        zp = z[:TP].reshape(B, S, Z_WIDTH)
        zs = z[TP:].reshape(DB, T, Z_WIDTH)
        small_p, small_s = cols(zp, Z_SMALL, 128), cols(zs, Z_SMALL, 128)

        o_nsa_p = _nsa_prompt(cols(zp, Z_NQ, 256), cols(zp, Z_ROWS, 256), cols(zp, Z_WIN, 128), small_p,
                              nsa_w_ck[l], nsa_w_cv[l])
        o_nsa_s = _nsa_decode(cols(zs, Z_NQ, 256), cols(zs, Z_ROWS, 256), cols(zs, Z_WIN, 128), small_s,
                              cache_nsa, cache_nsa_win, page_table, nsa_w_ck[l], nsa_w_cv[l], l)

        def mlstm(zz, small, state, L, nb, dt):
            return _mlstm(cols(zz, Z_MQ, 256), cols(zz, Z_MK, 256) * ATTN_SCALE, cols(zz, Z_MV, 256),
                          cols(zz, Z_MO, 256), cols(small, SMALL_MI, MLSTM_HEADS), cols(small, SMALL_MF, MLSTM_HEADS),
                          mlstm_norm_w[l], *state, L, nb, dt)

        o_ml_p, st_ml_p = mlstm(zp, small_p, zero_state, 128, B, bf16)
        o_ml_s, st_ml_s = mlstm(zs, small_s, (state_mlstm_c[l], state_mlstm_n[l], state_mlstm_m[l]), T, 4, f32)

        lf_p = jax.nn.log_sigmoid(cols(small_p, SMALL_FF, FOX_HEADS))
        lf_s = jax.nn.log_sigmoid(cols(small_s, SMALL_FF, FOX_HEADS))
        o_fox_p = _fox_prompt_attn(cols(zp, Z_FQ, 256), cols(zp, Z_FK, 256), cols(zp, Z_FV, 256),
                                   jnp.cumsum(lf_p, axis=1))
        o_fox_s = _fox_decode(cols(zs, Z_FQ, 256), cols(zs, Z_FK, 256), cols(zs, Z_FV, 256), lf_s,
                              cache_fox_kv, cache_fox_logf, page_table, l)

        x = _merge_ln(x, z, cat(o_nsa_p, o_nsa_s), cat(o_ml_p, o_ml_s), cat(o_fox_p, o_fox_s),
                      w_branch[l], w_out[l], ln1_w[l], ln1_b[l], alpha)
        x = _moe_ln(x, moe_w_group[l], moe_b_group[l], moe_w_expert[l], moe_b_expert[l],
                    moe_w_gate[l], moe_w_up[l], moe_w_down[l], ln2_w[l], ln2_b[l], alpha)

        w_keep = min(NSA_WINDOW, S)
        new_p.append((cols(zp, Z_ROWS, 256).reshape(B, S, NSA_ROWS, HD),
                      cols(zp, Z_WIN, 128)[:, S - w_keep:].reshape(B, w_keep, 2, HD),
                      cols(zp, Z_FK, 512).reshape(B, S, 2, FOX_HEADS, HD), lf_p) + tuple(st_ml_p))
        win_new = cols(zs, Z_WIN, 128).reshape(DB, T, 2, HD).astype(cache_nsa_win.dtype)
        new_s.append((cols(zs, Z_ROWS, 256).reshape(DB, T, NSA_ROWS, HD),
                      jnp.concatenate([cache_nsa_win[l][:, T:], win_new], axis=1),
                      cols(zs, Z_FK, 512).reshape(DB, T, 2, FOX_HEADS, HD), lf_s) + tuple(st_ml_s))
    (p_nsa_rows, p_nsa_win, p_fox_kv, p_fox_logf, p_mlstm_c, p_mlstm_n, p_mlstm_m) = _stack_layers(new_p)
    (s_nsa_rows, s_nsa_win, s_fox_kv, s_fox_logf, s_mlstm_c, s_mlstm_n, s_mlstm_m) = _stack_layers(new_s)
    return (x[:TP].reshape(B, S, D), x[TP:].reshape(DB, T, D),
            p_nsa_rows, p_nsa_win, p_fox_kv, p_fox_logf, p_mlstm_c, p_mlstm_n, p_mlstm_m,
            s_nsa_rows, s_nsa_win, s_fox_kv, s_fox_logf, s_mlstm_c, s_mlstm_n, s_mlstm_m)
```

```python
import functools

import jax
import jax.numpy as jnp
import numpy as np
from jax import lax
from jax.experimental import pallas as pl
from jax.experimental.pallas import tpu as pltpu

D_MODEL = 1024
BATCH = 2
SEQ = 8192
DEPTH = 2
DEC_BATCH = 128
DEC_SEQ = 8
PAST_LEN = 2048
PAGE_SIZE = 128

HEAD_DIM = 64
NSA_HEADS = 4
NSA_BLOCK = 64
NSA_TOPK = 16
NSA_WINDOW = 512
NSA_ROWS = 4
MLSTM_HEADS = 4
MLSTM_CHUNK = 64
FOX_HEADS = 4
Q_BLOCK = 128
N_BRANCH = 3
BRANCH_WIDTH = NSA_HEADS * HEAD_DIM
N_GROUPS = 4
EXPERTS_PER_GROUP = 4
N_EXPERTS = N_GROUPS * EXPERTS_PER_GROUP
TOP_K_IN_GROUP = 2
D_EXPERT = 256
LN_EPS = 1e-5
NEG_INF = -1e30
TINY = 1e-30
FORCED_SCORE = 1e9
ATTN_SCALE = HEAD_DIM ** -0.5

IN_SPLITS = (
    NSA_HEADS * HEAD_DIM,
    6 * HEAD_DIM,
    NSA_HEADS * 3,
    MLSTM_HEADS * HEAD_DIM,
    MLSTM_HEADS * HEAD_DIM,
    MLSTM_HEADS * HEAD_DIM,
    MLSTM_HEADS * HEAD_DIM,
    MLSTM_HEADS,
    MLSTM_HEADS,
    FOX_HEADS * HEAD_DIM,
    FOX_HEADS * HEAD_DIM,
    FOX_HEADS * HEAD_DIM,
    FOX_HEADS,
    N_BRANCH * D_MODEL,
)


def _linear_kernel(x_ref, w_ref, b_ref, o_ref):
    x = x_ref[...].astype(jnp.bfloat16)
    o_ref[...] = jnp.dot(x, w_ref[...], preferred_element_type=jnp.float32) + b_ref[...]


def _pallas_linear(x, w, b, tm=512, tn=512):
    T, K = x.shape
    N = w.shape[1]
    n_pad = -(-N // tn) * tn
    wp = jnp.pad(w.astype(jnp.bfloat16), ((0, 0), (0, n_pad - N)))
    bp = jnp.pad(b.astype(jnp.float32), (0, n_pad - N)).reshape(1, n_pad)
    out = pl.pallas_call(
        _linear_kernel,
        grid=(T // tm, n_pad // tn),
        in_specs=[pl.BlockSpec((tm, K), lambda i, j: (i, 0)),
                  pl.BlockSpec((K, tn), lambda i, j: (0, j)),
                  pl.BlockSpec((1, tn), lambda i, j: (0, j))],
        out_specs=pl.BlockSpec((tm, tn), lambda i, j: (i, j)),
        out_shape=jax.ShapeDtypeStruct((T, n_pad), jnp.float32),
        name="linear",
    )(x, wp, bp)
    return out[:, :N]


VMEM_LIMIT_BYTES = 48 * 1024 * 1024
NSA_SLOPES = tuple(2.0 ** (-8.0 * (h + 1) / NSA_HEADS) for h in range(NSA_HEADS))
_NT = (((1,), (1,)), ((), ()))
_HI = lax.Precision.HIGHEST


def _sigmoid(x):
    return 1.0 / (1.0 + jnp.exp(-x))


def _tile_rows(x, n):
    return jnp.concatenate([x] * n, axis=0)


def _compress_kernel(x_ref, w_ref, o_ref, acc_ref):
    k = pl.program_id(1)

    @pl.when(k == 0)
    def _():
        acc_ref[...] = jnp.zeros_like(acc_ref)

    acc_ref[...] += jnp.dot(x_ref[...], w_ref[...], preferred_element_type=jnp.float32, precision=_HI)

    @pl.when(k == pl.num_programs(1) - 1)
    def _():
        o_ref[...] = acc_ref[...]


def _compress_weights(w_ck, w_cv):
    z = jnp.zeros_like(w_ck)
    wk = jnp.stack([w_ck, z, z, z], axis=1)
    wv = jnp.stack([z, w_cv, z, z], axis=1)
    return jnp.concatenate([wk, wv], axis=-1).reshape(NSA_BLOCK * NSA_ROWS * HEAD_DIM, 2 * HEAD_DIM)


def _nsa_compress_blocks(blocks, w_big, tm=256, tk=2048):
    n, kdim = blocks.shape
    tm = min(tm, n)
    return pl.pallas_call(
        _compress_kernel,
        grid=(n // tm, kdim // tk),
        in_specs=[pl.BlockSpec((tm, tk), lambda i, k: (i, k)),
                  pl.BlockSpec((tk, 2 * HEAD_DIM), lambda i, k: (k, 0))],
        out_specs=pl.BlockSpec((tm, 2 * HEAD_DIM), lambda i, k: (i, 0)),
        out_shape=jax.ShapeDtypeStruct((n, 2 * HEAD_DIM), jnp.float32),
        scratch_shapes=[pltpu.VMEM((tm, 2 * HEAD_DIM), jnp.float32)],
        name="nsa_compress",
    )(blocks, w_big)


def _softmax_step(carry, s, v):
    m, l, acc = carry
    m_new = jnp.maximum(m, jnp.max(s, axis=-1, keepdims=True))
    p = jnp.exp(s - m_new)
    a = jnp.exp(m - m_new)
    l = a * l + jnp.sum(p, axis=-1, keepdims=True)
    acc = a * acc + jnp.dot(p.astype(jnp.bfloat16), v, preferred_element_type=jnp.float32)
    return m_new, l, acc


def _select_blocks(imp, cur, n_pick):
    q, n_sb = imp.shape
    jq = lax.broadcasted_iota(jnp.int32, (q, n_sb), 1)
    jf = jq.astype(jnp.float32)
    score = jnp.where(jq > cur, NEG_INF, jnp.where((jq == cur) | (jq == 0), FORCED_SCORE, imp))

    valid = jq <= cur

    def pick(_, carry):
        work, sel = carry
        hit = jq == jnp.argmax(work, axis=-1, keepdims=True).astype(jnp.int32)
        sel = jnp.where(hit & valid, 1.0, sel)
        work = jnp.where(hit, -jnp.inf, work)
        return work, sel

    _, sel = lax.fori_loop(0, n_pick, pick, (score, jnp.zeros((q, n_sb), jnp.float32)))
    return sel


def _nsa_prompt_kernel(qs_ref, qf_ref, kc_ref, vc_ref, ksel_ref, vsel_ref, kwin_ref, vwin_ref, g_ref,
                       o_ref, flag_ref, *, tq, tk, tw):
    i = pl.program_id(1)
    f32, bf16 = jnp.float32, jnp.bfloat16
    H = NSA_HEADS
    R = H * tq
    qs = qs_ref[0].reshape(R, HEAD_DIM)
    qf = qf_ref[0].reshape(R, HEAD_DIM)
    row = lax.broadcasted_iota(jnp.int32, (R, 1), 0)
    head = row // tq
    slope = jnp.where(head == 0, NSA_SLOPES[0], jnp.where(head == 1, NSA_SLOPES[1],
                      jnp.where(head == 2, NSA_SLOPES[2], NSA_SLOPES[3]))).astype(f32)
    posq = i * tq + (row - head * tq)

    n_cb = kc_ref.shape[1]
    sc = lax.dot_general(qf, kc_ref[0], _NT, precision=_HI, preferred_element_type=f32)
    jb = lax.broadcasted_iota(jnp.int32, (R, n_cb), 1)
    distc = posq - (jb * NSA_BLOCK + NSA_BLOCK - 1)
    okc = distc >= 0
    sc = jnp.where(okc, sc - slope * distc.astype(f32), NEG_INF)
    pc = jnp.exp(sc - jnp.max(sc, axis=-1, keepdims=True)) * okc.astype(f32)
    pc = pc / jnp.maximum(jnp.sum(pc, axis=-1, keepdims=True), TINY)
    o_cmp = jnp.dot(pc.astype(bf16), vc_ref[0].astype(bf16), preferred_element_type=f32)
    imp = pc[0:tq] + pc[tq:2 * tq] + pc[2 * tq:3 * tq] + pc[3 * tq:4 * tq]

    pq = i * tq + lax.broadcasted_iota(jnp.int32, (tq, 1), 0)
    msel = _select_blocks(imp, pq // NSA_BLOCK, NSA_TOPK)

    bpt = tk // NSA_BLOCK
    blk_any = jnp.max(msel, axis=0, keepdims=True)
    for j in range(n_cb // bpt):
        flag_ref[j] = (jnp.max(blk_any[:, j * bpt:(j + 1) * bpt]) > 0.5).astype(jnp.int32)
    msel = msel.astype(bf16)

    rowpos = i * tq + lax.broadcasted_iota(jnp.int32, (tq, 1), 0)

    def attend(carry, s, v1):
        m, acc = carry
        m_new = jnp.maximum(m, jnp.max(s, axis=-1, keepdims=True))
        p = jnp.exp(s - m_new).astype(bf16)
        return m_new, jnp.exp(m - m_new) * acc + jnp.dot(p, v1, preferred_element_type=f32)

    init = (jnp.full((R, 1), NEG_INF, f32), jnp.zeros((R, 2 * HEAD_DIM), f32))

    def sel_tile(j, carry):
        k0 = pl.multiple_of(j * tk, tk)
        k = ksel_ref[0, pl.ds(k0, tk), :]
        s = lax.dot_general(qs, k, _NT, preferred_element_type=f32)
        eb = (lax.broadcasted_iota(jnp.int32, (n_cb, tk), 0)
              == j * bpt + lax.broadcasted_iota(jnp.int32, (n_cb, tk), 1) // NSA_BLOCK)
        mexp = jnp.dot(msel, eb.astype(bf16), preferred_element_type=f32)
        d = rowpos - (k0 + lax.broadcasted_iota(jnp.int32, (tq, tk), 1))
        ok = (mexp > 0.5) & (d >= 0)
        s = jnp.where(_tile_rows(ok, H), s - slope * _tile_rows(d.astype(f32), H), NEG_INF)
        return attend(carry, s, vsel_ref[0, pl.ds(k0, tk), :])

    def sel_body(j, carry):
        return lax.cond(flag_ref[j] > 0, lambda c: sel_tile(j, c), lambda c: c, carry)

    n_sel = (i * tq + tq - 1) // tk + 1
    _, a_sel = lax.fori_loop(0, n_sel, sel_body, init)
    o_sel = a_sel[:, :HEAD_DIM] / a_sel[:, HEAD_DIM:]

    nw = NSA_WINDOW + tq
    w0 = pl.multiple_of(jnp.maximum(i * tq - NSA_WINDOW, 0), tw)
    sw = lax.dot_general(qs, kwin_ref[0, pl.ds(w0, nw), :], _NT, preferred_element_type=f32)
    dw = rowpos - (w0 + lax.broadcasted_iota(jnp.int32, (tq, nw), 1))
    okw = (dw >= 0) & (dw < NSA_WINDOW)
    sw = jnp.where(_tile_rows(okw, H), sw - slope * _tile_rows(dw.astype(f32), H), NEG_INF)
    _, a_win = attend(init, sw, vwin_ref[0, pl.ds(w0, nw), :])
    o_win = a_win[:, :HEAD_DIM] / a_win[:, HEAD_DIM:]

    g = _sigmoid(g_ref[...])
    gate = lambda c: jnp.concatenate([g[:, 3 * h + c:3 * h + c + 1] for h in range(H)], axis=0)
    o = gate(0) * o_cmp + gate(1) * o_sel + gate(2) * o_win
    o_ref[0] = o.reshape(H, tq, HEAD_DIM)


def _heads_major(x, dtype):
    B, S, _ = x.shape
    return x.reshape(B, S, -1, HEAD_DIM).transpose(0, 2, 1, 3).astype(dtype)


def _nsa_prompt(nq, nqb, rows, rowsb, winb, gates, w_ck, w_cv, tq=256, tk=512, tw=128):
    B, S, _ = nq.shape
    bf16 = jnp.bfloat16
    kcvc = _nsa_compress_blocks(rows.reshape(B * S // NSA_BLOCK, -1), _compress_weights(w_ck, w_cv))
    kcvc = kcvc.reshape(B, S // NSA_BLOCK, 2 * HEAD_DIM)
    kc, vc = kcvc[..., :HEAD_DIM], kcvc[..., HEAD_DIM:]
    qs = _heads_major(nqb, bf16)
    qf = _heads_major(nq, jnp.float32)
    rows, win = rowsb, winb
    col = lambda a, c: a[..., c * HEAD_DIM:(c + 1) * HEAD_DIM]
    with_ones = lambda v: jnp.concatenate([v, jnp.ones_like(v)], axis=-1)
    full = lambda n, w: pl.BlockSpec((1, n, w), lambda b, i: (b, 0, 0))
    qspec = pl.BlockSpec((1, NSA_HEADS, tq, HEAD_DIM), lambda b, i: (b, 0, i, 0))
    n_cb = S // NSA_BLOCK
    o = pl.pallas_call(
        functools.partial(_nsa_prompt_kernel, tq=tq, tk=tk, tw=tw),
        grid=(B, S // tq),
        in_specs=[qspec, qspec, full(n_cb, HEAD_DIM), full(n_cb, HEAD_DIM), full(S, HEAD_DIM),
                  full(S, 2 * HEAD_DIM), full(S, HEAD_DIM), full(S, 2 * HEAD_DIM),
                  pl.BlockSpec((None, tq, 128), lambda b, i: (b, i, 0))],
        out_specs=qspec,
        out_shape=jax.ShapeDtypeStruct((B, NSA_HEADS, S, HEAD_DIM), jnp.float32),
        scratch_shapes=[pltpu.SMEM((S // tk,), jnp.int32)],
        compiler_params=pltpu.CompilerParams(dimension_semantics=("parallel", "arbitrary"),
                                             vmem_limit_bytes=VMEM_LIMIT_BYTES),
        name="nsa_prompt",
    )(qs, qf, kc, vc, col(rows, 2), with_ones(col(rows, 3)), col(win, 0), with_ones(col(win, 1)), gates)
    return o.transpose(0, 2, 1, 3).reshape(B, S, BRANCH_WIDTH)


def _fox_prompt_kernel(q_ref, k_ref, v_ref, fq_ref, fk_ref, o_ref, *, tq, t):
    i = pl.program_id(1)
    f32, bf16 = jnp.float32, jnp.bfloat16
    lane = lax.broadcasted_iota(jnp.int32, (1, 2 * HEAD_DIM), 1)
    low = lane < HEAD_DIM
    rc = lax.broadcasted_iota(jnp.int32, (tq, t), 0) - lax.broadcasted_iota(jnp.int32, (tq, t), 1)
    n_pairs = FOX_HEADS // 2
    pair_cols = [slice(hp * 2 * HEAD_DIM, (hp + 1) * 2 * HEAD_DIM) for hp in range(n_pairs)]
    q_h, fq_h = [], []
    for hp in range(n_pairs):
        q2 = q_ref[0, :, pair_cols[hp]]
        zq = jnp.zeros_like(q2)
        q_h += [jnp.where(low, q2, zq), jnp.where(low, zq, q2)]
        fq_h += [fq_ref[0, :, 2 * hp:2 * hp + 1], fq_ref[0, :, 2 * hp + 1:2 * hp + 2]]

    def body(j, carry, diag_offset):
        k0 = pl.multiple_of(j * t, t)
        fk = fk_ref[0, j]
        out = []
        for hp in range(n_pairs):
            k2 = k_ref[0, pl.ds(k0, t), pair_cols[hp]]
            v2 = v_ref[0, pl.ds(k0, t), pair_cols[hp]]
            one = jnp.ones_like(v2)
            v1 = (jnp.where(low, v2, one), jnp.where(low, one, v2))
            for x in range(2):
                h = 2 * hp + x
                m, acc = carry[h]
                s = lax.dot_general(q_h[h], k2, _NT, preferred_element_type=f32) + (fq_h[h] - fk[h:h + 1])
                if diag_offset is not None:
                    s = jnp.where(rc >= diag_offset, s, NEG_INF)
                m_new = jnp.maximum(m, jnp.max(s, axis=-1, keepdims=True))
                p = jnp.exp(s - m_new).astype(bf16)
                acc = jnp.exp(m - m_new) * acc + jnp.dot(p, v1[x], preferred_element_type=f32)
                out.append((m_new, acc))
        return tuple(out)

    init = tuple((jnp.full((tq, 1), NEG_INF, f32), jnp.zeros((tq, 2 * HEAD_DIM), f32)) for _ in range(FOX_HEADS))
    n_full = i * (tq // t)
    carry = lax.fori_loop(0, n_full, functools.partial(body, diag_offset=None), init)
    for jj in range(tq // t):
        carry = body(n_full + jj, carry, jj * t)
    for hp in range(n_pairs):
        acc_a, acc_b = carry[2 * hp][1], carry[2 * hp + 1][1]
        den = pltpu.roll(jnp.where(low, acc_b, acc_a), HEAD_DIM, axis=1)
        o_ref[0, :, pair_cols[hp]] = jnp.where(low, acc_a, acc_b) / den


def _fox_prompt_attn(zb, F, tq=1024, t=1024):
    B, S, _ = zb.shape
    W = FOX_HEADS * HEAD_DIM
    cq, ck, cv = ((c - Z_NQ) // W for c in (Z_FQ, Z_FK, Z_FV))
    fk = jnp.pad(F.transpose(0, 2, 1), ((0, 0), (0, 8 - FOX_HEADS), (0, 0)))
    fk = fk.reshape(B, 8, S // t, t).transpose(0, 2, 1, 3)
    return pl.pallas_call(
        functools.partial(_fox_prompt_kernel, tq=tq, t=t),
        grid=(B, S // tq),
        in_specs=[pl.BlockSpec((1, tq, W), lambda b, i: (b, i, cq)),
                  pl.BlockSpec((1, S, W), lambda b, i: (b, 0, ck)),
                  pl.BlockSpec((1, S, W), lambda b, i: (b, 0, cv)),
                  pl.BlockSpec((1, tq, FOX_HEADS), lambda b, i: (b, i, 0)),
                  pl.BlockSpec((1, S // t, 8, t), lambda b, i: (b, 0, 0, 0))],
        out_specs=pl.BlockSpec((1, tq, W), lambda b, i: (b, i, 0)),
        out_shape=jax.ShapeDtypeStruct((B, S, W), jnp.float32),
        compiler_params=pltpu.CompilerParams(dimension_semantics=("parallel", "arbitrary"),
                                             vmem_limit_bytes=VMEM_LIMIT_BYTES),
        name="fox_prompt",
    )(zb, zb, zb, F, fk)


def _log_sigmoid(x):
    return jnp.minimum(x, 0.0) - jnp.log1p(jnp.exp(-jnp.abs(x)))


def _mlstm_kernel(q_ref, k_ref, kt_ref, v_ref, og_ref, gc_ref, gr_ref, nw_ref, cn0_ref, m0_ref,
                  h_ref, cn_ref, m_ref, cn_s, m_s, *, nb, L):
    c = pl.program_id(1)
    f32 = jnp.float32
    W = 2 * HEAD_DIM
    n_pairs = MLSTM_HEADS // 2

    @pl.when(c == 0)
    def _():
        cn_s[...] = cn0_ref[...]
        m_s[...] = m0_ref[...]

    lane = lax.broadcasted_iota(jnp.int32, (1, W), 1)
    low = lane < HEAD_DIM
    ti = lax.broadcasted_iota(jnp.int32, (L, L), 0)
    si = lax.broadcasted_iota(jnp.int32, (L, L), 1)
    causal = si <= ti
    tri = causal.astype(f32)
    tri_t = (ti <= si).astype(f32)
    srow = lax.broadcasted_iota(jnp.int32, (W, 2 * W), 0)
    slane = lax.broadcasted_iota(jnp.int32, (W, 2 * W), 1)
    top = srow < HEAD_DIM
    keep_a = top & ((slane < HEAD_DIM) | (slane == W))
    keep_b = (~top) & (((slane >= HEAD_DIM) & (slane < W)) | (slane == W + 1))
    lane_w = lax.broadcasted_iota(jnp.int32, (1, W), 1)
    mdt = k_ref.dtype

    for b in range(nb):
        gcol = gc_ref[b]
        grow = gr_ref[b]
        bcol = jnp.dot(tri, _log_sigmoid(gcol), precision=_HI, preferred_element_type=f32)
        brow = jnp.dot(_log_sigmoid(grow), tri_t, precision=_HI, preferred_element_type=f32)
        for hp in range(n_pairs):
            cols = slice(hp * W, (hp + 1) * W)
            q2 = q_ref[b, :, cols]
            k2 = k_ref[b, :, cols]
            v2 = v_ref[b, :, cols]
            kt2 = kt_ref[b, cols, :]
            cn = cn_s[b, hp]
            r = jnp.dot(q2, cn.astype(mdt), preferred_element_type=f32)
            zq = jnp.zeros_like(q2)
            per_head = []
            for x in range(2):
                h = 2 * hp + x
                qx = jnp.where(low, q2, zq) if x == 0 else jnp.where(low, zq, q2)
                b_c = bcol[:, MLSTM_HEADS + h:MLSTM_HEADS + h + 1]
                b_r = brow[MLSTM_HEADS + h:MLSTM_HEADS + h + 1, :]
                ig_c = gcol[:, h:h + 1]
                ig_r = grow[h:h + 1, :]
                m_prev = m_s[b, 0:1, h:h + 1]
                dmat = jnp.where(causal, b_c - b_r + ig_r, NEG_INF)
                a_c = b_c + m_prev
                m_t = jnp.maximum(a_c, jnp.max(dmat, axis=-1, keepdims=True))
                wq = jnp.exp(dmat - m_t) * lax.dot_general(qx, k2, _NT, preferred_element_type=f32)
                inter = jnp.exp(a_c - m_t)
                wv = jnp.dot(wq.astype(mdt), v2, preferred_element_type=f32)
                den = inter * r[:, W + x:W + x + 1] + jnp.sum(wq, axis=-1, keepdims=True)
                den = jnp.maximum(jnp.abs(den), jnp.exp(-m_t))
                bl = b_c[L - 1:L, :]
                g_c = bl - b_c + ig_c
                m_new = jnp.maximum(bl + m_prev, jnp.max(g_c, axis=0, keepdims=True))
                ws = jnp.exp(g_c - m_new)
                decay = jnp.exp(bl + m_prev - m_new)
                aug = jnp.concatenate([v2.astype(f32) * ws, jnp.where(lane_w == x, ws, 0.0)], axis=1)
                u = jnp.dot(kt2, aug.astype(mdt), preferred_element_type=f32)
                per_head.append((inter, wv, den, decay, u))
                m_s[b, 0:1, h:h + 1] = m_new
            (ia, wva, dena, deca, ua), (ib, wvb, denb, decb, ub) = per_head
            num = jnp.where(low, ia * r[:, :W] + wva, ib * r[:, :W] + wvb)
            hid = num / jnp.where(low, dena, denb)
            cn_s[b, hp] = (jnp.where(top, deca, decb) * cn + jnp.where(keep_a, ua, 0.0)
                           + jnp.where(keep_b, ub, 0.0))
            hid = _sigmoid(og_ref[b, :, cols]) * hid
            mean = lambda t: jnp.where(low, jnp.sum(jnp.where(low, t, 0.0), axis=-1, keepdims=True),
                                       jnp.sum(jnp.where(low, 0.0, t), axis=-1, keepdims=True)) / HEAD_DIM
            mu = mean(hid)
            var = mean(jnp.square(hid - mu))
            h_ref[b, :, cols] = (hid - mu) * lax.rsqrt(var + LN_EPS) * nw_ref[:, cols]

    @pl.when(c == pl.num_programs(1) - 1)
    def _():
        cn_ref[...] = cn_s[...]
        m_ref[...] = m_s[...]


def _mlstm_state_pack(c, n, m):
    Bx = c.shape[0]
    HD, W = HEAD_DIM, 2 * HEAD_DIM
    cp = c.reshape(Bx, MLSTM_HEADS // 2, 2, HD, HD)
    np_ = n.reshape(Bx, MLSTM_HEADS // 2, 2, HD, 1)
    z = lambda w: jnp.zeros((Bx, MLSTM_HEADS // 2, HD, w), jnp.float32)
    top = jnp.concatenate([cp[:, :, 0], z(HD), np_[:, :, 0], z(W - 1)], axis=-1)
    bot = jnp.concatenate([z(HD), cp[:, :, 1], z(1), np_[:, :, 1], z(W - 2)], axis=-1)
    cn = jnp.concatenate([top, bot], axis=-2)
    m8 = jnp.pad(m[:, None, :], ((0, 0), (0, 7), (0, 128 - MLSTM_HEADS)))
    return cn, m8


def _mlstm_state_unpack(cn, m8):
    W = 2 * HEAD_DIM
    c = jnp.stack([cn[:, h // 2, (h % 2) * HEAD_DIM:(h % 2 + 1) * HEAD_DIM,
                      (h % 2) * HEAD_DIM:(h % 2 + 1) * HEAD_DIM] for h in range(MLSTM_HEADS)], axis=1)
    n = jnp.stack([cn[:, h // 2, (h % 2) * HEAD_DIM:(h % 2 + 1) * HEAD_DIM, W + h % 2]
                   for h in range(MLSTM_HEADS)], axis=1)
    return c, n, m8[:, 0, :MLSTM_HEADS]


def _mlstm(q, k, v, og, mi, mf, norm_w, c0, n0, m0, L, nb, mxu_dtype):
    Bx, S, W4 = q.shape
    f32 = jnp.float32
    gates = jnp.concatenate([mi, mf], axis=-1).astype(f32)
    gcol = jnp.pad(gates, ((0, 0), (0, 0), (0, 128 - 2 * MLSTM_HEADS)))
    grow = gates.transpose(0, 2, 1)
    cn0, m8 = _mlstm_state_pack(c0.astype(f32), n0.astype(f32), m0.astype(f32))
    tok = lambda w: pl.BlockSpec((nb, L, w), lambda b, c: (b, c, 0))
    st_cn = pl.BlockSpec((nb, MLSTM_HEADS // 2, 128, 256), lambda b, c: (b, 0, 0, 0))
    st_m = pl.BlockSpec((nb, 8, 128), lambda b, c: (b, 0, 0))
    h, cn, m8 = pl.pallas_call(
        functools.partial(_mlstm_kernel, nb=nb, L=L),
        grid=(Bx // nb, S // L),
        in_specs=[tok(W4), tok(W4), pl.BlockSpec((nb, W4, L), lambda b, c: (b, 0, c)), tok(W4), tok(W4),
                  tok(128), pl.BlockSpec((nb, 8, L), lambda b, c: (b, 0, c)),
                  pl.BlockSpec((1, W4), lambda b, c: (0, 0)), st_cn, st_m],
        out_specs=[tok(W4), st_cn, st_m],
        out_shape=[jax.ShapeDtypeStruct((Bx, S, W4), f32),
                   jax.ShapeDtypeStruct(cn0.shape, f32), jax.ShapeDtypeStruct(m8.shape, f32)],
        scratch_shapes=[pltpu.VMEM((nb, MLSTM_HEADS // 2, 128, 256), f32), pltpu.VMEM((nb, 8, 128), f32)],
        compiler_params=pltpu.CompilerParams(dimension_semantics=("parallel", "arbitrary"),
                                             vmem_limit_bytes=VMEM_LIMIT_BYTES),
        name="mlstm",
    )(q.astype(mxu_dtype), k.astype(mxu_dtype), k.astype(mxu_dtype).transpose(0, 2, 1), v.astype(mxu_dtype),
      og.astype(f32), gcol, grow, norm_w.astype(f32).reshape(1, W4), cn0, m8)
    return h, _mlstm_state_unpack(cn, m8)


def _ln(x, w, b):
    mu = jnp.mean(x, axis=-1, keepdims=True)
    var = jnp.mean(jnp.square(x - mu), axis=-1, keepdims=True)
    return (x - mu) * lax.rsqrt(var + LN_EPS) * w + b


def _ln_kernel(x_ref, w_ref, b_ref, o_ref):
    o_ref[...] = _ln(x_ref[...], w_ref[...], b_ref[...])


def _layer_norm_rows(x, w, b, tm=1024):
    T, D = x.shape
    vec = pl.BlockSpec((1, D), lambda i: (0, 0))
    return pl.pallas_call(
        _ln_kernel, grid=(T // tm,),
        in_specs=[pl.BlockSpec((tm, D), lambda i: (i, 0)), vec, vec],
        out_specs=pl.BlockSpec((tm, D), lambda i: (i, 0)),
        out_shape=jax.ShapeDtypeStruct((T, D), jnp.float32),
        name="layer_norm",
    )(x, w.reshape(1, D), b.reshape(1, D))


Z_GM = 0
Z_NQ = 3072
Z_ROWS = Z_NQ + 256
Z_MQ = 3584
Z_MK = Z_MQ + 256
Z_MV = 4096
Z_MO = Z_MV + 256
Z_FQ = 4608
Z_WIN = Z_FQ + 256
Z_SMALL = Z_WIN + 128
Z_FK = 5120
Z_FV = Z_FK + 256
Z_WIDTH = 5632
Z_TILE = 512
ZB_FIRST_TILE = Z_NQ // Z_TILE
ZB_WIDTH = Z_WIDTH - Z_NQ
SMALL_MI, SMALL_MF, SMALL_FF = 12, 16, 20


def _z_column_order():
    starts = np.concatenate([[0], np.cumsum(IN_SPLITS)])
    seg = lambda i, lo=0, hi=None: np.arange(starts[i] + lo, starts[i] + (IN_SPLITS[i] if hi is None else hi))
    pad = lambda n: np.full((n,), -1)
    order = np.concatenate([
        seg(13), seg(0), seg(1, 0, 256), seg(3), seg(4), seg(5), seg(6),
        seg(9), seg(1, 256, 384), seg(2), seg(7), seg(8), seg(12), pad(128 - 24), seg(10), seg(11)])
    assert order.shape == (Z_WIDTH,)
    return order


def _permute_in_proj(w_in, b_in):
    order = _z_column_order()
    valid = jnp.asarray(order >= 0)
    idx = jnp.asarray(np.maximum(order, 0))
    scale = np.ones((Z_WIDTH,), np.float32)
    for c0 in (Z_NQ, Z_MK, Z_FQ):
        scale[c0:c0 + 256] = ATTN_SCALE
    scale = jnp.asarray(scale)
    w = jnp.where(valid[None, :], w_in[:, idx], 0.0) * scale[None, :]
    b = jnp.where(valid, b_in[idx], 0.0) * scale
    return w, b


def _proj_kernel(x_ref, w_ref, wlo_ref, b_ref, o_ref, ob_ref, xh_s, xl_s, *, hi_tile):
    j = pl.program_id(1)
    f32 = jnp.float32
    _proj_tile(x_ref, w_ref, wlo_ref, b_ref, o_ref, xh_s, xl_s, j, hi_tile)

    @pl.when(j >= ZB_FIRST_TILE)
    def _():
        ob_ref[...] = o_ref[...].astype(jnp.bfloat16)


def _proj_tile(x_ref, w_ref, wlo_ref, b_ref, o_ref, xh_s, xl_s, j, hi_tile):
    f32 = jnp.float32

    @pl.when(j == 0)
    def _():
        x = x_ref[...]
        xh = x.astype(jnp.bfloat16)
        xh_s[...] = xh
        xl_s[...] = (x - xh.astype(f32)).astype(jnp.bfloat16)

    @pl.when(j != hi_tile)
    def _():
        o_ref[...] = jnp.dot(xh_s[...], w_ref[...], preferred_element_type=f32) + b_ref[...]

    @pl.when(j == hi_tile)
    def _():
        acc = jnp.dot(xh_s[...], wlo_ref[...], preferred_element_type=f32)
        acc += jnp.dot(xl_s[...], w_ref[...], preferred_element_type=f32)
        acc += jnp.dot(xh_s[...], w_ref[...], preferred_element_type=f32)
        o_ref[...] = acc + b_ref[...]


def _in_projection(x, w_in, b_in, tm=1024):
    T, D = x.shape
    w, b = _permute_in_proj(w_in, b_in)
    wh = w.astype(jnp.bfloat16)
    hi_tile = Z_NQ // Z_TILE
    wlo = (w[:, Z_NQ:Z_NQ + Z_TILE] - wh[:, Z_NQ:Z_NQ + Z_TILE].astype(jnp.float32)).astype(jnp.bfloat16)
    return pl.pallas_call(
        functools.partial(_proj_kernel, hi_tile=hi_tile),
        grid=(T // tm, Z_WIDTH // Z_TILE),
        in_specs=[pl.BlockSpec((tm, D), lambda i, j: (i, 0)),
                  pl.BlockSpec((D, Z_TILE), lambda i, j: (0, j)),
                  pl.BlockSpec((D, Z_TILE), lambda i, j: (0, 0)),
                  pl.BlockSpec((1, Z_TILE), lambda i, j: (0, j))],
        out_specs=[pl.BlockSpec((tm, Z_TILE), lambda i, j: (i, j)),
                   pl.BlockSpec((tm, Z_TILE), lambda i, j: (i, jnp.maximum(j - ZB_FIRST_TILE, 0)))],
        out_shape=[jax.ShapeDtypeStruct((T, Z_WIDTH), jnp.float32),
                   jax.ShapeDtypeStruct((T, ZB_WIDTH), jnp.bfloat16)],
        scratch_shapes=[pltpu.VMEM((tm, D), jnp.bfloat16), pltpu.VMEM((tm, D), jnp.bfloat16)],
        compiler_params=pltpu.CompilerParams(dimension_semantics=("parallel", "arbitrary"),
                                             vmem_limit_bytes=VMEM_LIMIT_BYTES),
        name="in_projection",
    )(x, wh, wlo, b.reshape(1, Z_WIDTH))


def _merge_kernel(x_ref, gm_ref, on_ref, om_ref, of_ref, wb_ref, wo_ref, lw_ref, lb_ref, o_ref, *, alpha):
    f32, bf16 = jnp.float32, jnp.bfloat16
    y = None
    for m, br in enumerate((on_ref, om_ref, of_ref)):
        proj = jnp.dot(br[...].astype(bf16), wb_ref[m], preferred_element_type=f32)
        term = _sigmoid(gm_ref[:, m * D_MODEL:(m + 1) * D_MODEL]) * proj
        y = term if y is None else y + term
    mix = jnp.dot(y.astype(bf16), wo_ref[...], preferred_element_type=f32)
    o_ref[...] = _ln(alpha * x_ref[...] + mix, lw_ref[...], lb_ref[...])


def _merge_ln(x, z, o_nsa, o_ml, o_fox, w_branch, w_out, ln_w, ln_b, alpha, tm=512):
    T, D = x.shape
    tok = lambda w: pl.BlockSpec((tm, w), lambda i: (i, 0))
    vec = pl.BlockSpec((1, D), lambda i: (0, 0))
    return pl.pallas_call(
        functools.partial(_merge_kernel, alpha=alpha),
        grid=(T // tm,),
        in_specs=[tok(D), tok(N_BRANCH * D), tok(BRANCH_WIDTH), tok(BRANCH_WIDTH), tok(BRANCH_WIDTH),
                  pl.BlockSpec((N_BRANCH, BRANCH_WIDTH, D), lambda i: (0, 0, 0)),
                  pl.BlockSpec((D, D), lambda i: (0, 0)), vec, vec],
        out_specs=tok(D),
        out_shape=jax.ShapeDtypeStruct((T, D), jnp.float32),
        compiler_params=pltpu.CompilerParams(dimension_semantics=("parallel",),
                                             vmem_limit_bytes=VMEM_LIMIT_BYTES),
        name="merge_ln",
    )(x, z, o_nsa, o_ml, o_fox, w_branch.astype(jnp.bfloat16), w_out.astype(jnp.bfloat16),
      ln_w.reshape(1, D), ln_b.reshape(1, D))


def _route(logits):
    tm = logits.shape[0]
    lane = lax.broadcasted_iota(jnp.int32, (tm, 128), 1)
    lanef = lane.astype(jnp.float32)
    big = 1e9
    is_g = lane < N_GROUPS
    lg = jnp.where(is_g, logits, -jnp.inf)
    eg = jnp.exp(lg - jnp.max(lg, axis=-1, keepdims=True))
    pg = eg / jnp.sum(eg, axis=-1, keepdims=True)
    g_val = jnp.max(pg, axis=-1, keepdims=True)
    g_idx = jnp.min(jnp.where(is_g & (pg == g_val), lanef, big), axis=-1, keepdims=True)
    e_lo = N_GROUPS + EXPERTS_PER_GROUP * g_idx
    in_grp = (lanef >= e_lo) & (lanef < e_lo + EXPERTS_PER_GROUP)
    le = jnp.where(in_grp, logits, -jnp.inf)
    ee = jnp.exp(le - jnp.max(le, axis=-1, keepdims=True))
    pe = ee / jnp.sum(ee, axis=-1, keepdims=True)
    v1 = jnp.max(pe, axis=-1, keepdims=True)
    i1 = jnp.min(jnp.where(in_grp & (pe == v1), lanef, big), axis=-1, keepdims=True)
    rest = in_grp & (lanef != i1)
    pe2 = jnp.where(rest, pe, -1.0)
    v2 = jnp.max(pe2, axis=-1, keepdims=True)
    i2 = jnp.min(jnp.where(rest & (pe2 == v2), lanef, big), axis=-1, keepdims=True)
    tot = v1 + v2
    return jnp.where(lanef == i1, g_val * v1 / tot, jnp.where(lanef == i2, g_val * v2 / tot, 0.0))


def _moe_kernel(x_ref, wr_ref, br_ref, wgu_ref, wd_ref, lw_ref, lb_ref, o_ref, xb_s, gate_s, acc_s, *, alpha):
    e = pl.program_id(1)
    f32, bf16 = jnp.float32, jnp.bfloat16

    @pl.when(e == 0)
    def _():
        x = x_ref[...]
        xb_s[...] = x.astype(bf16)
        logits = jnp.dot(x, wr_ref[...], precision=_HI, preferred_element_type=f32) + br_ref[...]
        gate_s[...] = _route(logits)
        acc_s[...] = jnp.zeros_like(acc_s)

    lane = lax.broadcasted_iota(jnp.int32, gate_s.shape, 1)
    gate = jnp.sum(jnp.where(lane == N_GROUPS + e, gate_s[...], 0.0), axis=-1, keepdims=True)
    gu = jnp.dot(xb_s[...], wgu_ref[0], preferred_element_type=f32)
    g, u = gu[:, :D_EXPERT], gu[:, D_EXPERT:]
    h = (g * _sigmoid(g)) * u * gate
    acc_s[...] += jnp.dot(h.astype(bf16), wd_ref[0], preferred_element_type=f32)

    @pl.when(e == pl.num_programs(1) - 1)
    def _():
        o_ref[...] = _ln(alpha * x_ref[...] + acc_s[...], lw_ref[...], lb_ref[...])


def _moe_ln(x, w_group, b_group, w_expert, b_expert, w_gate, w_up, w_down, ln_w, ln_b, alpha, tm=1024):
    T, D = x.shape
    bf16 = jnp.bfloat16
    n_r = N_GROUPS + N_EXPERTS
    wr = jnp.pad(jnp.concatenate([w_group, w_expert], axis=1), ((0, 0), (0, 128 - n_r)))
    br = jnp.pad(jnp.concatenate([b_group, b_expert]), (0, 128 - n_r)).reshape(1, 128)
    wgu = jnp.concatenate([w_gate, w_up], axis=-1).astype(bf16)
    vec = pl.BlockSpec((1, D), lambda i, e: (0, 0))
    return pl.pallas_call(
        functools.partial(_moe_kernel, alpha=alpha),
        grid=(T // tm, N_EXPERTS),
        in_specs=[pl.BlockSpec((tm, D), lambda i, e: (i, 0)),
                  pl.BlockSpec((D, 128), lambda i, e: (0, 0)),
                  pl.BlockSpec((1, 128), lambda i, e: (0, 0)),
                  pl.BlockSpec((1, D, 2 * D_EXPERT), lambda i, e: (e, 0, 0)),
                  pl.BlockSpec((1, D_EXPERT, D), lambda i, e: (e, 0, 0)), vec, vec],
        out_specs=pl.BlockSpec((tm, D), lambda i, e: (i, 0)),
        out_shape=jax.ShapeDtypeStruct((T, D), jnp.float32),
        scratch_shapes=[pltpu.VMEM((tm, D), bf16), pltpu.VMEM((tm, 128), jnp.float32),
                        pltpu.VMEM((tm, D), jnp.float32)],
        compiler_params=pltpu.CompilerParams(dimension_semantics=("parallel", "arbitrary"),
                                             vmem_limit_bytes=VMEM_LIMIT_BYTES),
        name="moe_ln",
    )(x, wr, br, wgu, w_down.astype(bf16), ln_w.reshape(1, D), ln_b.reshape(1, D))


N_PAGES = PAST_LEN // PAGE_SIZE
NEW_PAD = 128
DEC_KEYS = PAST_LEN + NEW_PAD
DEC_ROWS = NSA_HEADS * DEC_SEQ


def _pages_token_minor(cache):
    nd = cache.ndim
    t = cache.transpose((0, 1) + tuple(range(3, nd)) + (2,))
    return t.reshape(cache.shape[0] * cache.shape[1], -1, cache.shape[2])


def _page_specs(rows, row_block, layer, n_phys):
    def spec(p):
        return pl.BlockSpec((1, rows, PAGE_SIZE),
                            lambda b, pt: (layer * n_phys + pt[b * N_PAGES + p], row_block, 0))
    return [spec(p) for p in range(N_PAGES)]


def _softmax_rows(s):
    m = jnp.max(s, axis=-1, keepdims=True)
    p = jnp.exp(s - m)
    return p, jnp.sum(p, axis=-1, keepdims=True)


def _fox_decode_kernel(pt_ref, qbd_ref, knew_ref, vnew_ref, fk_ref, fq_ref, *refs):
    pages, o_ref = refs[:N_PAGES], refs[N_PAGES]
    f32, bf16 = jnp.float32, jnp.bfloat16
    W = FOX_HEADS * HEAD_DIM
    qbd = qbd_ref[0]
    s = [jnp.dot(qbd, pg[0, :W, :].astype(bf16), preferred_element_type=f32) for pg in pages]
    s.append(jnp.dot(qbd, knew_ref[0], preferred_element_type=f32))
    s = jnp.concatenate(s, axis=1)
    rowh = lax.broadcasted_iota(jnp.int32, (DEC_ROWS, 1), 0) // DEC_SEQ
    fk = fk_ref[0]
    fk_rows = jnp.where(rowh == 0, fk[0:1], jnp.where(rowh == 1, fk[1:2], jnp.where(rowh == 2, fk[2:3], fk[3:4])))
    col = lax.broadcasted_iota(jnp.int32, (DEC_ROWS, DEC_KEYS), 1)
    t = lax.broadcasted_iota(jnp.int32, (DEC_ROWS, DEC_KEYS), 0) % DEC_SEQ
    ok = (col < PAST_LEN) | (col - PAST_LEN <= t)
    s = jnp.where(ok, s + (fq_ref[0] - fk_rows), NEG_INF)
    p, l = _softmax_rows(s)
    pb = p.astype(bf16)
    o = lax.dot_general(pb[:, PAST_LEN:], vnew_ref[0], _NT, preferred_element_type=f32)
    for i, pg in enumerate(pages):
        o += lax.dot_general(pb[:, i * PAGE_SIZE:(i + 1) * PAGE_SIZE], pg[0, W:, :].astype(bf16), _NT,
                             preferred_element_type=f32)
    o = o / l
    lane_h = lax.broadcasted_iota(jnp.int32, (DEC_ROWS, W), 1) // HEAD_DIM
    o = jnp.where(lane_h == rowh, o, 0.0)
    o_ref[0] = o[0:8] + o[8:16] + o[16:24] + o[24:32]


def _pad_new_t(x):
    return jnp.pad(x.transpose(0, 2, 1), ((0, 0), (0, 0), (0, NEW_PAD - DEC_SEQ))).astype(jnp.bfloat16)


def _fox_decode(q, k_new, v_new, lf_new, cache_kv, cache_lf, page_table, layer):
    DB = q.shape[0]
    f32, bf16 = jnp.float32, jnp.bfloat16
    W = FOX_HEADS * HEAD_DIM
    eye = jnp.eye(FOX_HEADS, dtype=f32)
    qh = q.reshape(DB, DEC_SEQ, FOX_HEADS, HEAD_DIM).transpose(0, 2, 1, 3)
    qbd = (qh[:, :, :, None, :] * eye[None, :, None, :, None]).reshape(DB, DEC_ROWS, W).astype(bf16)
    lf_all = jnp.concatenate([cache_lf[layer][page_table].reshape(DB, PAST_LEN, FOX_HEADS).astype(f32), lf_new], axis=1)
    F = jnp.cumsum(lf_all, axis=1)
    fk = jnp.pad(F.transpose(0, 2, 1), ((0, 0), (0, 8 - FOX_HEADS), (0, DEC_KEYS - PAST_LEN - DEC_SEQ)))
    fq = F[:, PAST_LEN:].transpose(0, 2, 1).reshape(DB, DEC_ROWS, 1)
    per_seq = lambda r, w: pl.BlockSpec((1, r, w), lambda b, pt: (b, 0, 0))
    pages = _pages_token_minor(cache_kv)
    return pl.pallas_call(
        _fox_decode_kernel,
        grid_spec=pltpu.PrefetchScalarGridSpec(
            num_scalar_prefetch=1, grid=(DB,),
            in_specs=[per_seq(DEC_ROWS, W), per_seq(W, NEW_PAD), per_seq(W, NEW_PAD), per_seq(8, DEC_KEYS),
                      per_seq(DEC_ROWS, 1)] + _page_specs(2 * W, 0, layer, cache_kv.shape[1]),
            out_specs=per_seq(DEC_SEQ, W)),
        out_shape=jax.ShapeDtypeStruct((DB, DEC_SEQ, W), f32),
        compiler_params=pltpu.CompilerParams(dimension_semantics=("parallel",), vmem_limit_bytes=VMEM_LIMIT_BYTES),
        name="fox_decode",
    )(page_table.reshape(-1), qbd, _pad_new_t(k_new), _pad_new_t(v_new), fk, fq, *([pages] * N_PAGES))


def _nsa_decode_kernel(pt_ref, qs_ref, qf_ref, kcvc_ref, rnew_ref, wbuf_ref, wnew_ref, g_ref, e_ref, *refs):
    pages, o_ref = refs[:N_PAGES], refs[N_PAGES]
    f32, bf16 = jnp.float32, jnp.bfloat16
    H, T, HD = NSA_HEADS, DEC_SEQ, HEAD_DIM
    qs = qs_ref[0]
    row = lax.broadcasted_iota(jnp.int32, (DEC_ROWS, 1), 0)
    head = row // T
    slope = jnp.where(head == 0, NSA_SLOPES[0], jnp.where(head == 1, NSA_SLOPES[1],
                      jnp.where(head == 2, NSA_SLOPES[2], NSA_SLOPES[3]))).astype(f32)
    posq = PAST_LEN + row % T

    kcvc = kcvc_ref[0]
    n_cb = kcvc.shape[0]
    sc = lax.dot_general(qf_ref[0], kcvc, _NT, precision=_HI, preferred_element_type=f32)
    jb = lax.broadcasted_iota(jnp.int32, (DEC_ROWS, n_cb), 1)
    distc = posq - (jb * NSA_BLOCK + NSA_BLOCK - 1)
    okc = distc >= 0
    sc = jnp.where(okc, sc - slope * distc.astype(f32), NEG_INF)
    pc = jnp.exp(sc - jnp.max(sc, axis=-1, keepdims=True)) * okc.astype(f32)
    pc = pc / jnp.maximum(jnp.sum(pc, axis=-1, keepdims=True), TINY)
    o_cmp = jnp.dot(pc.astype(bf16), kcvc.astype(bf16), preferred_element_type=f32)[:, HD:]
    imp = pc[0:T] + pc[T:2 * T] + pc[2 * T:3 * T] + pc[3 * T:4 * T]
    imp = jnp.concatenate([imp, jnp.zeros((T, 128 - n_cb), f32)], axis=1)
    cur = (PAST_LEN + lax.broadcasted_iota(jnp.int32, (T, 1), 0)) // NSA_BLOCK
    msel = _select_blocks(imp, cur, NSA_TOPK).astype(bf16)

    mexp = jnp.dot(msel, e_ref[...], preferred_element_type=f32)
    col = lax.broadcasted_iota(jnp.int32, (T, DEC_KEYS), 1)
    tq = lax.broadcasted_iota(jnp.int32, (T, DEC_KEYS), 0)
    d = PAST_LEN + tq - col
    ok = (mexp > 0.5) & (d >= 0)
    kv = [pg[0].astype(bf16) for pg in pages] + [rnew_ref[0]]
    s = jnp.concatenate([jnp.dot(qs, x[:HD], preferred_element_type=f32) for x in kv], axis=1)
    s = jnp.where(_tile_rows(ok, H), s - slope * _tile_rows(d.astype(f32), H), NEG_INF)
    p, l = _softmax_rows(s)
    pb = p.astype(bf16)
    acc = jnp.zeros((DEC_ROWS, HD), f32)
    for i, x in enumerate(kv):
        acc += lax.dot_general(pb[:, i * PAGE_SIZE:(i + 1) * PAGE_SIZE], x[HD:], _NT, preferred_element_type=f32)
    o_sel = acc / l

    wb = wbuf_ref.shape[2]
    kvw = [wbuf_ref[0].astype(bf16), wnew_ref[0]]
    sw = jnp.concatenate([jnp.dot(qs, x[:HD], preferred_element_type=f32) for x in kvw], axis=1)
    colw = lax.broadcasted_iota(jnp.int32, (T, wb + NEW_PAD), 1)
    tw = lax.broadcasted_iota(jnp.int32, (T, wb + NEW_PAD), 0)
    dw = wb + tw - colw
    okw = (dw >= 0) & (dw < NSA_WINDOW)
    sw = jnp.where(_tile_rows(okw, H), sw - slope * _tile_rows(dw.astype(f32), H), NEG_INF)
    pw, lw = _softmax_rows(sw)
    pwb = pw.astype(bf16)
    accw = (lax.dot_general(pwb[:, :wb], kvw[0][HD:], _NT, preferred_element_type=f32)
            + lax.dot_general(pwb[:, wb:], kvw[1][HD:], _NT, preferred_element_type=f32))
    o_win = accw / lw

    g = _sigmoid(g_ref[0])
    gate = lambda c: jnp.concatenate([g[:, 3 * h + c:3 * h + c + 1] for h in range(H)], axis=0)
    o_ref[0] = gate(0) * o_cmp + gate(1) * o_sel + gate(2) * o_win


def _compress_pages_kernel(x_ref, wk_ref, wv_ref, o_ref):
    tm = x_ref.shape[0]
    f32 = jnp.float32
    acc_k = jnp.zeros((tm, 2 * HEAD_DIM), f32)
    acc_v = jnp.zeros((tm, 2 * HEAD_DIM), f32)
    for d in range(HEAD_DIM):
        acc_k += jnp.dot(x_ref[:, d, :], wk_ref[d], precision=_HI, preferred_element_type=f32)
        acc_v += jnp.dot(x_ref[:, HEAD_DIM + d, :], wv_ref[d], preferred_element_type=f32)
    o_ref[...] = jnp.concatenate([acc_k, acc_v], axis=1)


def _nsa_compress_pages(pages, w_ck, w_cv, layer, n_phys, tm=64):
    eye = jnp.eye(PAGE_SIZE // NSA_BLOCK, dtype=jnp.float32)
    big = lambda w: jnp.einsum('pde,bc->dbpce', w, eye).reshape(HEAD_DIM, PAGE_SIZE, 2 * HEAD_DIM)
    wspec = pl.BlockSpec((HEAD_DIM, PAGE_SIZE, 2 * HEAD_DIM), lambda i: (0, 0, 0))
    return pl.pallas_call(
        _compress_pages_kernel,
        grid=(n_phys // tm,),
        in_specs=[pl.BlockSpec((tm, 2 * HEAD_DIM, PAGE_SIZE), lambda i: (layer * (n_phys // tm) + i, 0, 0)),
                  wspec, wspec],
        out_specs=pl.BlockSpec((tm, 4 * HEAD_DIM), lambda i: (i, 0)),
        out_shape=jax.ShapeDtypeStruct((n_phys, 4 * HEAD_DIM), jnp.float32),
        compiler_params=pltpu.CompilerParams(dimension_semantics=("parallel",), vmem_limit_bytes=VMEM_LIMIT_BYTES),
        name="nsa_compress_pages",
    )(pages, big(w_ck), big(w_cv))


def _nsa_decode(nq, rows_new, win_new, gates, cache_rows, cache_win, page_table, w_ck, w_cv, layer):
    DB = nq.shape[0]
    f32, bf16 = jnp.float32, jnp.bfloat16
    HD = HEAD_DIM
    n_phys = cache_rows.shape[1]
    n_blk = PAGE_SIZE // NSA_BLOCK
    pages = _pages_token_minor(cache_rows)
    kcvc = _nsa_compress_pages(pages, w_ck, w_cv, layer, n_phys)[page_table]
    kcvc = kcvc.reshape(DB, N_PAGES, 2, n_blk, HD).transpose(0, 1, 3, 2, 4).reshape(DB, N_PAGES * n_blk, 2 * HD)
    stack = lambda x: x.reshape(DB, DEC_SEQ, NSA_HEADS, HD).transpose(0, 2, 1, 3).reshape(DB, DEC_ROWS, HD)
    qs = stack(nq).astype(bf16)
    qf = jnp.pad(stack(nq), ((0, 0), (0, 0), (0, HD))).astype(f32)
    win_t = cache_win.transpose(0, 1, 3, 4, 2).reshape(cache_win.shape[0] * DB, 2 * HD, cache_win.shape[2])
    colk = np.arange(DEC_KEYS)
    e = (np.arange(128)[:, None] == colk[None, :] // NSA_BLOCK) & (colk[None, :] < PAST_LEN + NSA_BLOCK)
    wb = cache_win.shape[2]
    per_seq = lambda r, w: pl.BlockSpec((1, r, w), lambda b, pt: (b, 0, 0))
    o = pl.pallas_call(
        _nsa_decode_kernel,
        grid_spec=pltpu.PrefetchScalarGridSpec(
            num_scalar_prefetch=1, grid=(DB,),
            in_specs=[per_seq(DEC_ROWS, HD), per_seq(DEC_ROWS, 2 * HD), per_seq(PAST_LEN // NSA_BLOCK, 2 * HD),
                      per_seq(2 * HD, NEW_PAD),
                      pl.BlockSpec((1, 2 * HD, wb), lambda b, pt: (layer * DB + b, 0, 0)),
                      per_seq(2 * HD, NEW_PAD), per_seq(DEC_SEQ, 128),
                      pl.BlockSpec((128, DEC_KEYS), lambda b, pt: (0, 0))]
                     + _page_specs(2 * HD, 1, layer, n_phys),
            out_specs=per_seq(DEC_ROWS, HD)),
        out_shape=jax.ShapeDtypeStruct((DB, DEC_ROWS, HD), f32),
        compiler_params=pltpu.CompilerParams(dimension_semantics=("parallel",), vmem_limit_bytes=VMEM_LIMIT_BYTES),
        name="nsa_decode",
    )(page_table.reshape(-1), qs, qf, kcvc, _pad_new_t(rows_new[..., 2 * HD:]), win_t,
      _pad_new_t(win_new), gates, jnp.asarray(e, bf16), *([pages] * N_PAGES))
    return o.reshape(DB, NSA_HEADS, DEC_SEQ, HD).transpose(0, 2, 1, 3).reshape(DB, DEC_SEQ, BRANCH_WIDTH)


def _copy_kernel(pt_ref, x_ref, o_ref):
    o_ref[...] = x_ref[...]


def _gather_pages(cache, page_table):
    db, n_pages = page_table.shape
    _, page, w = cache.shape
    out = pl.pallas_call(
        _copy_kernel,
        grid_spec=pltpu.PrefetchScalarGridSpec(
            num_scalar_prefetch=1, grid=(db * n_pages,),
            in_specs=[pl.BlockSpec((1, page, w), lambda i, pt: (pt[i], 0, 0))],
            out_specs=pl.BlockSpec((1, page, w), lambda i, pt: (i, 0, 0))),
        out_shape=jax.ShapeDtypeStruct((db * n_pages, page, w), cache.dtype),
        name="gather_pages",
    )(page_table.reshape(-1), cache)
    return out.reshape(db, n_pages * page, w)


def _split_points():
    return [int(v) for v in np.cumsum(IN_SPLITS)[:-1]]


def _layer_norm(x, w, b):
    xf = x.astype(jnp.float32)
    mu = jnp.mean(xf, axis=-1, keepdims=True)
    var = jnp.mean(jnp.square(xf - mu), axis=-1, keepdims=True)
    y = (xf - mu) * lax.rsqrt(var + LN_EPS) * w.astype(jnp.float32) + b.astype(jnp.float32)
    return y.astype(x.dtype)


def _masked_softmax(s, mask):
    s = jnp.where(mask, s.astype(jnp.float32), NEG_INF)
    p = jnp.exp(s - jnp.max(s, axis=-1, keepdims=True)) * mask
    return p / jnp.maximum(jnp.sum(p, axis=-1, keepdims=True), TINY)


def _alibi_slopes(n):
    return jnp.exp2(-8.0 * jnp.arange(1, n + 1, dtype=jnp.float32) / n)


def _project(x, w_in, b_in):
    B, L, _ = x.shape
    f32 = jnp.float32
    z = _pallas_linear(x.reshape(B * L, -1), w_in, b_in).reshape(B, L, -1)
    (nq, nkv, ng, mq, mk, mv, mo, mi, mf, fq, fk, fv, ff, gm) = jnp.split(z, _split_points(), axis=-1)
    heads = lambda t, h: t.reshape(B, L, h, HEAD_DIM)
    rows = nkv.reshape(B, L, 6, HEAD_DIM)
    return dict(
        nsa_q=heads(nq, NSA_HEADS),
        nsa_rows=rows[:, :, :NSA_ROWS],
        nsa_win=rows[:, :, NSA_ROWS:],
        nsa_g=jax.nn.sigmoid(ng.astype(f32)).reshape(B, L, NSA_HEADS, 3),
        nsa_g_raw=ng,
        ml_q=heads(mq, MLSTM_HEADS).astype(f32),
        ml_k=heads(mk, MLSTM_HEADS).astype(f32) * ATTN_SCALE,
        ml_v=heads(mv, MLSTM_HEADS).astype(f32),
        ml_o=jax.nn.sigmoid(mo.astype(f32)).reshape(B, L, MLSTM_HEADS, HEAD_DIM),
        ml_i=mi.astype(f32),
        ml_lf=jax.nn.log_sigmoid(mf.astype(f32)),
        fox_q=heads(fq, FOX_HEADS),
        fox_k=heads(fk, FOX_HEADS),
        fox_v=heads(fv, FOX_HEADS),
        fox_lf=jax.nn.log_sigmoid(ff.astype(f32)),
        merge=jax.nn.sigmoid(gm.astype(f32)).reshape(B, L, N_BRANCH, D_MODEL))


def _nsa_compress(k, w):
    B, L, _ = k.shape
    n_cb = L // NSA_BLOCK
    blocks = k[:, :n_cb * NSA_BLOCK].reshape(B, n_cb, NSA_BLOCK, HEAD_DIM)
    return jnp.einsum('bjpd,pde->bje', blocks, w)


def _nsa_cmp_attend(q, kc, vc, pos_q, slopes):
    n_cb = kc.shape[1]
    blk_end = jnp.arange(n_cb) * NSA_BLOCK + NSA_BLOCK - 1
    dist = pos_q[:, None] - blk_end[None, :]
    s = jnp.einsum('bqhd,bjd->bhqj', q, kc) * ATTN_SCALE - slopes[:, None, None] * dist.astype(jnp.float32)
    p = _masked_softmax(s, dist >= 0)
    o = jnp.einsum('bhqj,bjd->bqhd', p.astype(vc.dtype), vc)
    return o, jnp.sum(p, axis=1)


def _nsa_select_blocks(imp, pos_q, n_sb):
    n_cb = imp.shape[-1]
    imp = jnp.pad(imp, ((0, 0), (0, 0), (0, n_sb - n_cb)))
    j = jnp.arange(n_sb)[None, :]
    cur = (pos_q // NSA_BLOCK)[:, None]
    score = jnp.where(j > cur, NEG_INF, jnp.where((j == cur) | (j == 0), FORCED_SCORE, imp))
    vals, idx = lax.top_k(score, min(NSA_TOPK, n_sb))
    return idx, vals > 0.5 * NEG_INF


def _nsa_sel_attend(q, kb, vb, idx, valid, pos_q, slopes):
    B, Q = q.shape[:2]
    bi = jnp.arange(B)[:, None, None]
    kg = kb[bi, idx].reshape(B, Q, -1, HEAD_DIM)
    vg = vb[bi, idx].reshape(B, Q, -1, HEAD_DIM)
    pos_k = (idx[..., None] * NSA_BLOCK + jnp.arange(NSA_BLOCK)).reshape(B, Q, -1)
    ok = jnp.repeat(valid, NSA_BLOCK, axis=-1) & (pos_k <= pos_q[None, :, None])
    dist = (pos_q[None, :, None] - pos_k).astype(jnp.float32)
    s = jnp.einsum('bqhd,bqkd->bhqk', q, kg) * ATTN_SCALE - slopes[None, :, None, None] * dist[:, None]
    p = _masked_softmax(s, ok[:, None])
    return jnp.einsum('bhqk,bqkd->bqhd', p.astype(vg.dtype), vg)


def _nsa_cmp_sel(q, rows, pos_q, w_ck, w_cv, slopes):
    B, Q = q.shape[:2]
    L = rows.shape[1]
    kc = _nsa_compress(rows[:, :, 0], w_ck)
    vc = _nsa_compress(rows[:, :, 1], w_cv)
    o_cmp, imp = _nsa_cmp_attend(q, kc, vc, pos_q, slopes)
    n_sb = -(-L // NSA_BLOCK)
    idx, valid = _nsa_select_blocks(imp, pos_q, n_sb)
    pad = ((0, 0), (0, n_sb * NSA_BLOCK - L), (0, 0))
    kb = jnp.pad(rows[:, :, 2], pad).reshape(B, n_sb, NSA_BLOCK, HEAD_DIM)
    vb = jnp.pad(rows[:, :, 3], pad).reshape(B, n_sb, NSA_BLOCK, HEAD_DIM)
    qb = Q_BLOCK if Q % Q_BLOCK == 0 else Q
    nqb = Q // qb

    def block(args):
        qc, ic, okc, pc = args
        return _nsa_sel_attend(qc, kb, vb, ic, okc, pc, slopes)

    xs = (q.reshape(B, nqb, qb, NSA_HEADS, HEAD_DIM).swapaxes(0, 1),
          idx.reshape(B, nqb, qb, -1).swapaxes(0, 1),
          valid.reshape(B, nqb, qb, -1).swapaxes(0, 1),
          pos_q.reshape(nqb, qb))
    o_sel = lax.map(block, xs).swapaxes(0, 1).reshape(B, Q, NSA_HEADS, HEAD_DIM)
    return o_cmp, o_sel


def _nsa_win_attend(q, k, v, pos_q, pos_k, slopes):
    dist = pos_q[:, None] - pos_k[None, :]
    ok = (dist >= 0) & (dist < NSA_WINDOW) & (pos_k[None, :] >= 0)
    s = jnp.einsum('bqhd,bkd->bhqk', q, k) * ATTN_SCALE - slopes[:, None, None] * dist.astype(jnp.float32)
    p = _masked_softmax(s, ok)
    return jnp.einsum('bhqk,bkd->bqhd', p.astype(v.dtype), v)


def _nsa_win_prompt(q, k, v, slopes):
    B, S = q.shape[:2]
    nqb = S // Q_BLOCK
    nb = NSA_WINDOW // Q_BLOCK
    pad = ((0, 0), (NSA_WINDOW, 0), (0, 0))
    kp = jnp.pad(k, pad).reshape(B, nqb + nb, Q_BLOCK, HEAD_DIM)
    vp = jnp.pad(v, pad).reshape(B, nqb + nb, Q_BLOCK, HEAD_DIM)
    band = jnp.arange(nqb)[:, None] + jnp.arange(nb + 1)[None, :]
    kband = kp[:, band].reshape(B, nqb, (nb + 1) * Q_BLOCK, HEAD_DIM)
    vband = vp[:, band].reshape(B, nqb, (nb + 1) * Q_BLOCK, HEAD_DIM)
    qblk = q.reshape(B, nqb, Q_BLOCK, NSA_HEADS, HEAD_DIM)
    pos_q = jnp.arange(S).reshape(nqb, Q_BLOCK)
    pos_k = (jnp.arange(nqb) * Q_BLOCK - NSA_WINDOW)[:, None] + jnp.arange((nb + 1) * Q_BLOCK)[None, :]
    o = jax.vmap(_nsa_win_attend, in_axes=(1, 1, 1, 0, 0, None), out_axes=1)(qblk, kband, vband, pos_q, pos_k, slopes)
    return o.reshape(B, S, NSA_HEADS, HEAD_DIM)


def _nsa_combine(g, o_cmp, o_sel, o_win):
    return jnp.einsum('blhc,cblhd->blhd', g, jnp.stack([o_cmp, o_sel, o_win]))


def _mlstm_chunk(carry, xs):
    c, n, m = carry
    q, k, v, ig, lf = xs
    L = q.shape[1]
    b = jnp.cumsum(lf, axis=1)
    causal = jnp.tril(jnp.ones((L, L), bool))[None, :, :, None]
    dmat = jnp.where(causal, b[:, :, None, :] - b[:, None, :, :] + ig[:, None, :, :], NEG_INF)
    a = b + m[:, None, :]
    m_t = jnp.maximum(a, jnp.max(dmat, axis=2))
    wq = jnp.exp(dmat - m_t[:, :, None, :]) * jnp.einsum('bthd,bshd->btsh', q, k)
    inter = jnp.exp(a - m_t)
    num = inter[..., None] * jnp.einsum('bthd,bhde->bthe', q, c) + jnp.einsum('btsh,bshe->bthe', wq, v)
    den = inter * jnp.einsum('bthd,bhd->bth', q, n) + jnp.sum(wq, axis=2)
    h = num / jnp.maximum(jnp.abs(den), jnp.exp(-m_t))[..., None]
    bl = b[:, -1]
    g = bl[:, None, :] - b + ig
    m_new = jnp.maximum(bl + m, jnp.max(g, axis=1))
    ws = jnp.exp(g - m_new[:, None, :])
    decay = jnp.exp(bl + m - m_new)
    c_new = decay[..., None, None] * c + jnp.einsum('bsh,bshd,bshe->bhde', ws, k, v)
    n_new = decay[..., None] * n + jnp.einsum('bsh,bshd->bhd', ws, k)
    return (c_new, n_new, m_new), h


def _mlstm_prompt(q, k, v, ig, lf):
    B, S, H, _ = q.shape
    ch = MLSTM_CHUNK if S % MLSTM_CHUNK == 0 else S
    nc = S // ch
    to_chunks = lambda t: t.reshape((B, nc, ch) + t.shape[2:]).swapaxes(0, 1)
    f32 = jnp.float32
    init = (jnp.zeros((B, H, HEAD_DIM, HEAD_DIM), f32), jnp.zeros((B, H, HEAD_DIM), f32), jnp.zeros((B, H), f32))
    state, h = lax.scan(_mlstm_chunk, init, tuple(to_chunks(t) for t in (q, k, v, ig, lf)))
    return h.swapaxes(0, 1).reshape(B, S, H, HEAD_DIM), state


def _mlstm_readout(h, o_gate, norm_w):
    h = o_gate * h
    mu = jnp.mean(h, axis=-1, keepdims=True)
    var = jnp.mean(jnp.square(h - mu), axis=-1, keepdims=True)
    return (h - mu) * lax.rsqrt(var + LN_EPS) * norm_w.astype(jnp.float32).reshape(MLSTM_HEADS, HEAD_DIM)


def _fox_attend(q, k, v, fq, fk, pos_q, pos_k):
    s = jnp.einsum('bqhd,bkhd->bhqk', q, k) * ATTN_SCALE
    s = s + (fq.transpose(0, 2, 1)[..., None] - fk.transpose(0, 2, 1)[:, :, None, :])
    p = _masked_softmax(s, pos_k[None, :] <= pos_q[:, None])
    return jnp.einsum('bhqk,bkhd->bqhd', p.astype(v.dtype), v)


def _fox_prompt(q, k, v, F):
    B, S = q.shape[:2]
    qb = Q_BLOCK if S % Q_BLOCK == 0 else S
    nqb = S // qb
    pos = jnp.arange(S)

    def block(args):
        qc, fc, pc = args
        return _fox_attend(qc, k, v, fc, F, pc, pos)

    xs = (q.reshape(B, nqb, qb, FOX_HEADS, HEAD_DIM).swapaxes(0, 1),
          F.reshape(B, nqb, qb, FOX_HEADS).swapaxes(0, 1),
          pos.reshape(nqb, qb))
    return lax.map(block, xs).swapaxes(0, 1).reshape(B, S, FOX_HEADS, HEAD_DIM)


def _merge(gates, o_nsa, o_ml, o_fox, w_branch, w_out, dtype):
    B, L = o_nsa.shape[:2]
    br = jnp.stack([o.reshape(B, L, BRANCH_WIDTH).astype(dtype) for o in (o_nsa, o_ml, o_fox)], axis=2)
    proj = jnp.einsum('blmc,mcd->blmd', br, w_branch)
    y = jnp.einsum('blmd,blmd->bld', gates, proj).astype(dtype)
    return y @ w_out


def _moe(x, w_group, b_group, w_expert, b_expert, w_gate, w_up, w_down):
    B, L, D = x.shape
    t = x.reshape(B * L, D)
    pg = jax.nn.softmax((t @ w_group + b_group).astype(jnp.float32), axis=-1)
    g_val, g_idx = lax.top_k(pg, 1)
    le = (t @ w_expert + b_expert).astype(jnp.float32).reshape(-1, N_GROUPS, EXPERTS_PER_GROUP)
    le = jnp.take_along_axis(le, g_idx[:, :, None], axis=1)[:, 0]
    e_val, e_idx = lax.top_k(jax.nn.softmax(le, axis=-1), TOP_K_IN_GROUP)
    e_w = g_val * e_val / jnp.sum(e_val, axis=-1, keepdims=True)
    gate = jnp.sum(jax.nn.one_hot(g_idx * EXPERTS_PER_GROUP + e_idx, N_EXPERTS, dtype=jnp.float32) * e_w[..., None], axis=1)
    h = jax.nn.silu(jnp.einsum('td,edf->tef', t, w_gate)) * jnp.einsum('td,edf->tef', t, w_up)
    h = h * gate[:, :, None].astype(h.dtype)
    return jnp.einsum('tef,efd->td', h, w_down).reshape(B, L, D)


def _mixers_prompt(x, w_in, b_in, w_ck, w_cv, ml_norm_w, w_branch, w_out, slopes):
    B, S, _ = x.shape
    pr = _project(x, w_in, b_in)
    pos = jnp.arange(S)
    o_nsa = _nsa_prompt(pr['nsa_q'].reshape(B, S, -1), pr['nsa_rows'].reshape(B, S, -1),
                        pr['nsa_win'].reshape(B, S, -1), jnp.pad(pr['nsa_g_raw'], ((0, 0), (0, 0), (0, 116))),
                        w_ck, w_cv).reshape(B, S, NSA_HEADS, HEAD_DIM)
    h, (c, n, m) = _mlstm_prompt(pr['ml_q'], pr['ml_k'], pr['ml_v'], pr['ml_i'], pr['ml_lf'])
    o_ml = _mlstm_readout(h, pr['ml_o'], ml_norm_w)
    flat = lambda t: t.reshape(B, S, -1)
    o_fox = _fox_prompt_attn(flat(pr['fox_q']), flat(pr['fox_k']), flat(pr['fox_v']),
                             jnp.cumsum(pr['fox_lf'], axis=1)).reshape(B, S, FOX_HEADS, HEAD_DIM)
    out = _merge(pr['merge'], o_nsa, o_ml, o_fox, w_branch, w_out, x.dtype)
    w_keep = min(NSA_WINDOW, S)
    new = (pr['nsa_rows'], pr['nsa_win'][:, S - w_keep:], jnp.stack([pr['fox_k'], pr['fox_v']], axis=2),
           pr['fox_lf'], c, n, m)
    return out, new


def _mixers_sample(x, c_nsa, c_win, c_fox_kv, c_fox_lf, s_c, s_n, s_m, page_table,
                   w_in, b_in, w_ck, w_cv, ml_norm_w, w_branch, w_out, slopes):
    DB, T, _ = x.shape
    past = page_table.shape[1] * PAGE_SIZE
    f32 = jnp.float32
    pr = _project(x, w_in, b_in)
    pos_q = past + jnp.arange(T)
    rows_past = _gather_pages(c_nsa.reshape(-1, PAGE_SIZE, NSA_ROWS * HEAD_DIM), page_table)
    rows_past = rows_past.reshape(DB, past, NSA_ROWS, HEAD_DIM)
    rows_all = jnp.concatenate([rows_past, pr['nsa_rows'].astype(rows_past.dtype)], axis=1)
    o_cmp, o_sel = _nsa_cmp_sel(pr['nsa_q'], rows_all, pos_q, w_ck, w_cv, slopes)
    wb = c_win.shape[1]
    win_all = jnp.concatenate([c_win, pr['nsa_win'].astype(c_win.dtype)], axis=1)
    pos_k = past - wb + jnp.arange(wb + T)
    o_win = _nsa_win_attend(pr['nsa_q'], win_all[:, :, 0], win_all[:, :, 1], pos_q, pos_k, slopes)
    o_nsa = _nsa_combine(pr['nsa_g'], o_cmp, o_sel, o_win)
    (c, n, m), h = _mlstm_chunk((s_c.astype(f32), s_n.astype(f32), s_m.astype(f32)),
                                (pr['ml_q'], pr['ml_k'], pr['ml_v'], pr['ml_i'], pr['ml_lf']))
    o_ml = _mlstm_readout(h, pr['ml_o'], ml_norm_w)
    kv_past = _gather_pages(c_fox_kv.reshape(-1, PAGE_SIZE, 2 * FOX_HEADS * HEAD_DIM), page_table)
    kv_past = kv_past.reshape(DB, past, 2, FOX_HEADS, HEAD_DIM)
    k_all = jnp.concatenate([kv_past[:, :, 0], pr['fox_k'].astype(kv_past.dtype)], axis=1)
    v_all = jnp.concatenate([kv_past[:, :, 1], pr['fox_v'].astype(kv_past.dtype)], axis=1)
    lf_all = jnp.concatenate([c_fox_lf[page_table].reshape(DB, past, FOX_HEADS).astype(f32), pr['fox_lf']], axis=1)
    F = jnp.cumsum(lf_all, axis=1)
    o_fox = _fox_attend(pr['fox_q'], k_all, v_all, F[:, past:], F, pos_q, jnp.arange(past + T))
    out = _merge(pr['merge'], o_nsa, o_ml, o_fox, w_branch, w_out, x.dtype)
    new = (pr['nsa_rows'], win_all[:, T:], jnp.stack([pr['fox_k'], pr['fox_v']], axis=2),
           pr['fox_lf'], c, n, m)
    return out, new


def _stack_layers(states):
    return tuple(jnp.stack(list(a)) for a in zip(*states))


def kernel(x_prompt, x_sample, cache_nsa, cache_nsa_win, cache_fox_kv, cache_fox_logf,
           state_mlstm_c, state_mlstm_n, state_mlstm_m, page_table,
           ln_in_w, ln_in_b, w_in, b_in, nsa_w_ck, nsa_w_cv, mlstm_norm_w, w_branch, w_out,
           ln1_w, ln1_b, moe_w_group, moe_b_group, moe_w_expert, moe_b_expert,
           moe_w_gate, moe_w_up, moe_w_down, ln2_w, ln2_b):
    f32, bf16 = jnp.float32, jnp.bfloat16
    alpha = (2.0 * DEPTH) ** 0.25
    B, S, D = x_prompt.shape
    DB, T, _ = x_sample.shape
    TP, TS = B * S, DB * T
    HD = HEAD_DIM
    xp = _layer_norm_rows(x_prompt.reshape(TP, D), ln_in_w, ln_in_b)
    xs = _layer_norm_rows(x_sample.reshape(TS, D), ln_in_w, ln_in_b)
    cols = lambda a, c0, w: a[..., c0:c0 + w]
    colsb = lambda a, c0, w: a[..., c0 - Z_NQ:c0 - Z_NQ + w]
    zero_state = (jnp.zeros((B, MLSTM_HEADS, HD, HD), f32), jnp.zeros((B, MLSTM_HEADS, HD), f32),
                  jnp.zeros((B, MLSTM_HEADS), f32))
    new_p, new_s = [], []
    for l in range(DEPTH):
        zp2, zbp2 = _in_projection(xp, w_in[l], b_in[l])
        zs2, _ = _in_projection(xs, w_in[l], b_in[l])
        zp, zbp = zp2.reshape(B, S, Z_WIDTH), zbp2.reshape(B, S, ZB_WIDTH)
        zs = zs2.reshape(DB, T, Z_WIDTH)
        small_p, small_s = cols(zp, Z_SMALL, 128), cols(zs, Z_SMALL, 128)

        o_nsa_p = _nsa_prompt(cols(zp, Z_NQ, 256), colsb(zbp, Z_NQ, 256), cols(zp, Z_ROWS, 256),
                              colsb(zbp, Z_ROWS, 256), colsb(zbp, Z_WIN, 128), small_p, nsa_w_ck[l], nsa_w_cv[l])
        o_nsa_s = _nsa_decode(cols(zs, Z_NQ, 256), cols(zs, Z_ROWS, 256), cols(zs, Z_WIN, 128), small_s,
                              cache_nsa, cache_nsa_win, page_table, nsa_w_ck[l], nsa_w_cv[l], l)

        def mlstm(q, k, v, og, small, state, L, nb, dt):
            return _mlstm(q, k, v, og, cols(small, SMALL_MI, MLSTM_HEADS), cols(small, SMALL_MF, MLSTM_HEADS),
                          mlstm_norm_w[l], *state, L, nb, dt)

        o_ml_p, st_ml_p = mlstm(colsb(zbp, Z_MQ, 256), colsb(zbp, Z_MK, 256), colsb(zbp, Z_MV, 256),
                                cols(zp, Z_MO, 256), small_p, zero_state, 128, B, bf16)
        o_ml_s, st_ml_s = mlstm(cols(zs, Z_MQ, 256), cols(zs, Z_MK, 256), cols(zs, Z_MV, 256), cols(zs, Z_MO, 256),
                                small_s, (state_mlstm_c[l], state_mlstm_n[l], state_mlstm_m[l]), T, 4, f32)

        lf_p = jax.nn.log_sigmoid(cols(small_p, SMALL_FF, FOX_HEADS))
        lf_s = jax.nn.log_sigmoid(cols(small_s, SMALL_FF, FOX_HEADS))
        o_fox_p = _fox_prompt_attn(zbp, jnp.cumsum(lf_p, axis=1))
        o_fox_s = _fox_decode(cols(zs, Z_FQ, 256), cols(zs, Z_FK, 256), cols(zs, Z_FV, 256), lf_s,
                              cache_fox_kv, cache_fox_logf, page_table, l)

        flat = lambda a: a.reshape(-1, a.shape[-1])
        moe_w = (moe_w_group[l], moe_b_group[l], moe_w_expert[l], moe_b_expert[l],
                 moe_w_gate[l], moe_w_up[l], moe_w_down[l], ln2_w[l], ln2_b[l], alpha)
        xp = _merge_ln(xp, zp2, flat(o_nsa_p), flat(o_ml_p), flat(o_fox_p),
                       w_branch[l], w_out[l], ln1_w[l], ln1_b[l], alpha)
        xp = _moe_ln(xp, *moe_w)
        xs = _merge_ln(xs, zs2, flat(o_nsa_s), flat(o_ml_s), flat(o_fox_s),
                       w_branch[l], w_out[l], ln1_w[l], ln1_b[l], alpha)
        xs = _moe_ln(xs, *moe_w)

        w_keep = min(NSA_WINDOW, S)
        new_p.append((cols(zp, Z_ROWS, 256).reshape(B, S, NSA_ROWS, HD),
                      cols(zp, Z_WIN, 128)[:, S - w_keep:].reshape(B, w_keep, 2, HD),
                      cols(zp, Z_FK, 512).reshape(B, S, 2, FOX_HEADS, HD), lf_p) + tuple(st_ml_p))
        win_new = cols(zs, Z_WIN, 128).reshape(DB, T, 2, HD).astype(cache_nsa_win.dtype)
        new_s.append((cols(zs, Z_ROWS, 256).reshape(DB, T, NSA_ROWS, HD),
                      jnp.concatenate([cache_nsa_win[l][:, T:], win_new], axis=1),
                      cols(zs, Z_FK, 512).reshape(DB, T, 2, FOX_HEADS, HD), lf_s) + tuple(st_ml_s))
    (p_nsa_rows, p_nsa_win, p_fox_kv, p_fox_logf, p_mlstm_c, p_mlstm_n, p_mlstm_m) = _stack_layers(new_p)
    (s_nsa_rows, s_nsa_win, s_fox_kv, s_fox_logf, s_mlstm_c, s_mlstm_n, s_mlstm_m) = _stack_layers(new_s)
    return (xp.reshape(B, S, D), xs.reshape(DB, T, D),
            p_nsa_rows, p_nsa_win, p_fox_kv, p_fox_logf, p_mlstm_c, p_mlstm_n, p_mlstm_m,
            s_nsa_rows, s_nsa_win, s_fox_kv, s_fox_logf, s_mlstm_c, s_mlstm_n, s_mlstm_m)
```

```python
import functools

import jax
import jax.numpy as jnp
import numpy as np
from jax import lax
from jax.experimental import pallas as pl
from jax.experimental.pallas import tpu as pltpu

D_MODEL = 1024
BATCH = 2
SEQ = 8192
DEPTH = 2
DEC_BATCH = 128
DEC_SEQ = 8
PAST_LEN = 2048
PAGE_SIZE = 128

HEAD_DIM = 64
NSA_HEADS = 4
NSA_BLOCK = 64
NSA_TOPK = 16
NSA_WINDOW = 512
NSA_ROWS = 4
MLSTM_HEADS = 4
MLSTM_CHUNK = 64
FOX_HEADS = 4
Q_BLOCK = 128
N_BRANCH = 3
BRANCH_WIDTH = NSA_HEADS * HEAD_DIM
N_GROUPS = 4
EXPERTS_PER_GROUP = 4
N_EXPERTS = N_GROUPS * EXPERTS_PER_GROUP
TOP_K_IN_GROUP = 2
D_EXPERT = 256
LN_EPS = 1e-5
NEG_INF = -1e30
TINY = 1e-30
FORCED_SCORE = 1e9
ATTN_SCALE = HEAD_DIM ** -0.5

IN_SPLITS = (
    NSA_HEADS * HEAD_DIM,
    6 * HEAD_DIM,
    NSA_HEADS * 3,
    MLSTM_HEADS * HEAD_DIM,
    MLSTM_HEADS * HEAD_DIM,
    MLSTM_HEADS * HEAD_DIM,
    MLSTM_HEADS * HEAD_DIM,
    MLSTM_HEADS,
    MLSTM_HEADS,
    FOX_HEADS * HEAD_DIM,
    FOX_HEADS * HEAD_DIM,
    FOX_HEADS * HEAD_DIM,
    FOX_HEADS,
    N_BRANCH * D_MODEL,
)


def _linear_kernel(x_ref, w_ref, b_ref, o_ref):
    x = x_ref[...].astype(jnp.bfloat16)
    o_ref[...] = jnp.dot(x, w_ref[...], preferred_element_type=jnp.float32) + b_ref[...]


def _pallas_linear(x, w, b, tm=512, tn=512):
    T, K = x.shape
    N = w.shape[1]
    n_pad = -(-N // tn) * tn
    wp = jnp.pad(w.astype(jnp.bfloat16), ((0, 0), (0, n_pad - N)))
    bp = jnp.pad(b.astype(jnp.float32), (0, n_pad - N)).reshape(1, n_pad)
    out = pl.pallas_call(
        _linear_kernel,
        grid=(T // tm, n_pad // tn),
        in_specs=[pl.BlockSpec((tm, K), lambda i, j: (i, 0)),
                  pl.BlockSpec((K, tn), lambda i, j: (0, j)),
                  pl.BlockSpec((1, tn), lambda i, j: (0, j))],
        out_specs=pl.BlockSpec((tm, tn), lambda i, j: (i, j)),
        out_shape=jax.ShapeDtypeStruct((T, n_pad), jnp.float32),
        name="linear",
    )(x, wp, bp)
    return out[:, :N]


VMEM_LIMIT_BYTES = 48 * 1024 * 1024
NSA_SLOPES = tuple(2.0 ** (-8.0 * (h + 1) / NSA_HEADS) for h in range(NSA_HEADS))
_NT = (((1,), (1,)), ((), ()))
_HI = lax.Precision.HIGHEST


def _sigmoid(x):
    return 1.0 / (1.0 + jnp.exp(-x))


def _tile_rows(x, n):
    return jnp.concatenate([x] * n, axis=0)


def _compress_kernel(x_ref, w_ref, o_ref, acc_ref):
    k = pl.program_id(1)

    @pl.when(k == 0)
    def _():
        acc_ref[...] = jnp.zeros_like(acc_ref)

    acc_ref[...] += jnp.dot(x_ref[...], w_ref[...], preferred_element_type=jnp.float32, precision=_HI)

    @pl.when(k == pl.num_programs(1) - 1)
    def _():
        o_ref[...] = acc_ref[...]


def _compress_weights(w_ck, w_cv):
    z = jnp.zeros_like(w_ck)
    wk = jnp.stack([w_ck, z, z, z], axis=1)
    wv = jnp.stack([z, w_cv, z, z], axis=1)
    return jnp.concatenate([wk, wv], axis=-1).reshape(NSA_BLOCK * NSA_ROWS * HEAD_DIM, 2 * HEAD_DIM)


def _nsa_compress_blocks(blocks, w_big, tm=256, tk=2048):
    n, kdim = blocks.shape
    tm = min(tm, n)
    return pl.pallas_call(
        _compress_kernel,
        grid=(n // tm, kdim // tk),
        in_specs=[pl.BlockSpec((tm, tk), lambda i, k: (i, k)),
                  pl.BlockSpec((tk, 2 * HEAD_DIM), lambda i, k: (k, 0))],
        out_specs=pl.BlockSpec((tm, 2 * HEAD_DIM), lambda i, k: (i, 0)),
        out_shape=jax.ShapeDtypeStruct((n, 2 * HEAD_DIM), jnp.float32),
        scratch_shapes=[pltpu.VMEM((tm, 2 * HEAD_DIM), jnp.float32)],
        name="nsa_compress",
    )(blocks, w_big)


def _softmax_step(carry, s, v):
    m, l, acc = carry
    m_new = jnp.maximum(m, jnp.max(s, axis=-1, keepdims=True))
    p = jnp.exp(s - m_new)
    a = jnp.exp(m - m_new)
    l = a * l + jnp.sum(p, axis=-1, keepdims=True)
    acc = a * acc + jnp.dot(p.astype(jnp.bfloat16), v, preferred_element_type=jnp.float32)
    return m_new, l, acc


def _select_blocks(imp, cur, n_pick):
    q, n_sb = imp.shape
    jq = lax.broadcasted_iota(jnp.int32, (q, n_sb), 1)
    valid = jq <= cur
    forced = valid & ((jq == cur) | (jq == 0))
    work0 = jnp.where(forced, -jnp.inf, jnp.where(valid, imp, NEG_INF))

    def pick(_, carry):
        work, sel = carry
        hit = jq == jnp.argmax(work, axis=-1, keepdims=True).astype(jnp.int32)
        sel = jnp.where(hit & valid, 1.0, sel)
        work = jnp.where(hit, -jnp.inf, work)
        return work, sel

    _, sel = lax.fori_loop(0, n_pick - 2, pick, (work0, forced.astype(jnp.float32)))
    return sel


def _nsa_prompt_kernel(qs_ref, qf_ref, kc_ref, vc_ref, ksel_ref, vsel_ref, kwin_ref, vwin_ref, g_ref,
                       o_ref, flag_ref, *, tq, tk, tw):
    i = pl.program_id(1)
    f32, bf16 = jnp.float32, jnp.bfloat16
    H = NSA_HEADS
    R = H * tq
    qs = qs_ref[0].reshape(R, 2 * HEAD_DIM)
    qf = qf_ref[0].reshape(R, HEAD_DIM)
    row = lax.broadcasted_iota(jnp.int32, (R, 1), 0)
    head = row // tq
    slope = jnp.where(head == 0, NSA_SLOPES[0], jnp.where(head == 1, NSA_SLOPES[1],
                      jnp.where(head == 2, NSA_SLOPES[2], NSA_SLOPES[3]))).astype(f32)
    posq = i * tq + (row - head * tq)

    n_cb = kc_ref.shape[1]
    sc = lax.dot_general(qf, kc_ref[0], _NT, precision=_HI, preferred_element_type=f32)
    jb = lax.broadcasted_iota(jnp.int32, (R, n_cb), 1)
    distc = posq - (jb * NSA_BLOCK + NSA_BLOCK - 1)
    okc = distc >= 0
    sc = jnp.where(okc, sc - slope * distc.astype(f32), NEG_INF)
    pc = jnp.exp(sc - jnp.max(sc, axis=-1, keepdims=True)) * okc.astype(f32)
    pc = pc / jnp.maximum(jnp.sum(pc, axis=-1, keepdims=True), TINY)
    o_cmp = jnp.dot(pc.astype(bf16), vc_ref[0].astype(bf16), preferred_element_type=f32)
    imp = pc[0:tq] + pc[tq:2 * tq] + pc[2 * tq:3 * tq] + pc[3 * tq:4 * tq]

    pq = i * tq + lax.broadcasted_iota(jnp.int32, (tq, 1), 0)
    msel = _select_blocks(imp, pq // NSA_BLOCK, NSA_TOPK)

    bpt = tk // NSA_BLOCK
    blk_any = jnp.max(msel, axis=0, keepdims=True)
    for j in range(n_cb // bpt):
        flag_ref[j] = (jnp.max(blk_any[:, j * bpt:(j + 1) * bpt]) > 0.5).astype(jnp.int32)
    mbias = (NEG_INF * (1.0 - msel)).astype(bf16)

    rowpos = i * tq + lax.broadcasted_iota(jnp.int32, (tq, 1), 0)

    def attend(carry, s, v1):
        m, acc = carry
        m_new = jnp.maximum(m, jnp.max(s, axis=-1, keepdims=True))
        p = jnp.exp(s - m_new).astype(bf16)
        return m_new, jnp.exp(m - m_new) * acc + jnp.dot(p, v1, preferred_element_type=f32)

    init = (jnp.full((R, 1), NEG_INF, f32), jnp.zeros((R, 2 * HEAD_DIM), f32))

    def sel_tile(j, carry, causal):
        k0 = pl.multiple_of(j * tk, tk)
        s = lax.dot_general(qs, ksel_ref[0, pl.ds(k0, tk), :], _NT, preferred_element_type=f32)
        eb = (lax.broadcasted_iota(jnp.int32, (n_cb, tk), 0)
              == j * bpt + lax.broadcasted_iota(jnp.int32, (n_cb, tk), 1) // NSA_BLOCK)
        bias = jnp.dot(mbias, eb.astype(bf16), preferred_element_type=f32)
        if causal:
            d = rowpos - (k0 + lax.broadcasted_iota(jnp.int32, (tq, tk), 1))
            bias = jnp.where(d >= 0, bias, NEG_INF)
        return attend(carry, s + _tile_rows(bias, H), vsel_ref[0, pl.ds(k0, tk), :])

    def sel_body(j, carry):
        return lax.cond(flag_ref[j] > 0, lambda c: sel_tile(j, c, False), lambda c: c, carry)

    n_sel = (i * tq + tq - 1) // tk + 1
    carry = lax.fori_loop(0, n_sel - 1, sel_body, init)
    _, a_sel = sel_tile(n_sel - 1, carry, True)
    o_sel = a_sel[:, :HEAD_DIM] / a_sel[:, HEAD_DIM:]

    nw = NSA_WINDOW + tq
    w0 = pl.multiple_of(jnp.maximum(i * tq - NSA_WINDOW, 0), tw)
    sw = lax.dot_general(qs, kwin_ref[0, pl.ds(w0, nw), :], _NT, preferred_element_type=f32)
    dw = rowpos - (w0 + lax.broadcasted_iota(jnp.int32, (tq, nw), 1))
    okw = (dw >= 0) & (dw < NSA_WINDOW)
    sw = jnp.where(_tile_rows(okw, H), sw, NEG_INF)
    _, a_win = attend(init, sw, vwin_ref[0, pl.ds(w0, nw), :])
    o_win = a_win[:, :HEAD_DIM] / a_win[:, HEAD_DIM:]

    g = _sigmoid(g_ref[...])
    gate = lambda c: jnp.concatenate([g[:, 3 * h + c:3 * h + c + 1] for h in range(H)], axis=0)
    o = gate(0) * o_cmp + gate(1) * o_sel + gate(2) * o_win
    o_ref[0] = o.reshape(H, tq, HEAD_DIM)


def _heads_major(x, dtype):
    B, S, _ = x.shape
    return x.reshape(B, S, -1, HEAD_DIM).transpose(0, 2, 1, 3).astype(dtype)


def _nsa_prompt(nq, nqb, rows, rowsb, winb, gates, w_ck, w_cv, tq=256, tk=512, tw=128):
    B, S, _ = nq.shape
    bf16 = jnp.bfloat16
    kcvc = _nsa_compress_blocks(rows.reshape(B * S // NSA_BLOCK, -1), _compress_weights(w_ck, w_cv))
    kcvc = kcvc.reshape(B, S // NSA_BLOCK, 2 * HEAD_DIM)
    kc, vc = kcvc[..., :HEAD_DIM], kcvc[..., HEAD_DIM:]
    pos = jnp.arange(S)
    blk, off = (pos // NSA_BLOCK).astype(jnp.float32), (pos % NSA_BLOCK).astype(jnp.float32)
    one = jnp.ones((S,), jnp.float32)
    slopes = jnp.asarray(NSA_SLOPES, jnp.float32)[:, None]
    q_terms = jnp.stack([-slopes * NSA_BLOCK * blk, -slopes * off, slopes * NSA_BLOCK * one, slopes * one], axis=-1)
    k_terms = jnp.stack([one, one, blk, off], axis=-1)
    pad_terms = lambda t: jnp.pad(t, [(0, 0)] * (t.ndim - 1) + [(0, HEAD_DIM - 4)]).astype(bf16)
    with_k_terms = lambda k: jnp.concatenate([k, jnp.broadcast_to(pad_terms(k_terms), (B, S, HEAD_DIM))], axis=-1)
    qs = jnp.concatenate([_heads_major(nqb, bf16),
                          jnp.broadcast_to(pad_terms(q_terms), (B, NSA_HEADS, S, HEAD_DIM))], axis=-1)
    qf = _heads_major(nq, jnp.float32)
    rows, win = rowsb, winb
    col = lambda a, c: a[..., c * HEAD_DIM:(c + 1) * HEAD_DIM]
    with_ones = lambda v: jnp.concatenate([v, jnp.ones_like(v)], axis=-1)
    full = lambda n, w: pl.BlockSpec((1, n, w), lambda b, i: (b, 0, 0))
    qspec = pl.BlockSpec((1, NSA_HEADS, tq, HEAD_DIM), lambda b, i: (b, 0, i, 0))
    n_cb = S // NSA_BLOCK
    o = pl.pallas_call(
        functools.partial(_nsa_prompt_kernel, tq=tq, tk=tk, tw=tw),
        grid=(B, S // tq),
        in_specs=[pl.BlockSpec((1, NSA_HEADS, tq, 2 * HEAD_DIM), lambda b, i: (b, 0, i, 0)), qspec,
                  full(n_cb, HEAD_DIM), full(n_cb, HEAD_DIM), full(S, 2 * HEAD_DIM),
                  full(S, 2 * HEAD_DIM), full(S, 2 * HEAD_DIM), full(S, 2 * HEAD_DIM),
                  pl.BlockSpec((None, tq, 128), lambda b, i: (b, i, 0))],
        out_specs=qspec,
        out_shape=jax.ShapeDtypeStruct((B, NSA_HEADS, S, HEAD_DIM), jnp.float32),
        scratch_shapes=[pltpu.SMEM((S // tk,), jnp.int32)],
        compiler_params=pltpu.CompilerParams(dimension_semantics=("parallel", "arbitrary"),
                                             vmem_limit_bytes=VMEM_LIMIT_BYTES),
        name="nsa_prompt",
    )(qs, qf, kc, vc, with_k_terms(col(rows, 2)), with_ones(col(rows, 3)),
      with_k_terms(col(win, 0)), with_ones(col(win, 1)), gates)
    return o.transpose(0, 2, 1, 3).reshape(B, S, BRANCH_WIDTH)


def _fox_prompt_kernel(q_ref, k_ref, v_ref, fq_ref, fk_ref, o_ref, *, tq, t):
    i = pl.program_id(1)
    f32, bf16 = jnp.float32, jnp.bfloat16
    lane = lax.broadcasted_iota(jnp.int32, (1, 2 * HEAD_DIM), 1)
    low = lane < HEAD_DIM
    rc = lax.broadcasted_iota(jnp.int32, (tq, t), 0) - lax.broadcasted_iota(jnp.int32, (tq, t), 1)
    n_pairs = FOX_HEADS // 2
    pair_cols = [slice(hp * 2 * HEAD_DIM, (hp + 1) * 2 * HEAD_DIM) for hp in range(n_pairs)]
    q_h, fq_h = [], []
    for hp in range(n_pairs):
        q2 = q_ref[0, :, pair_cols[hp]]
        zq = jnp.zeros_like(q2)
        q_h += [jnp.where(low, q2, zq), jnp.where(low, zq, q2)]
        fq_h += [fq_ref[0, :, 2 * hp:2 * hp + 1], fq_ref[0, :, 2 * hp + 1:2 * hp + 2]]

    def body(j, carry, diag_offset):
        k0 = pl.multiple_of(j * t, t)
        fk = fk_ref[0, j]
        out = []
        for hp in range(n_pairs):
            k2 = k_ref[0, pl.ds(k0, t), pair_cols[hp]]
            v2 = v_ref[0, pl.ds(k0, t), pair_cols[hp]]
            one = jnp.ones_like(v2)
            v1 = (jnp.where(low, v2, one), jnp.where(low, one, v2))
            for x in range(2):
                h = 2 * hp + x
                m, acc = carry[h]
                s = lax.dot_general(q_h[h], k2, _NT, preferred_element_type=f32) + (fq_h[h] - fk[h:h + 1])
                if diag_offset is not None:
                    s = jnp.where(rc >= diag_offset, s, NEG_INF)
                m_new = jnp.maximum(m, jnp.max(s, axis=-1, keepdims=True))
                p = jnp.exp(s - m_new).astype(bf16)
                acc = jnp.exp(m - m_new) * acc + jnp.dot(p, v1[x], preferred_element_type=f32)
                out.append((m_new, acc))
        return tuple(out)

    init = tuple((jnp.full((tq, 1), NEG_INF, f32), jnp.zeros((tq, 2 * HEAD_DIM), f32)) for _ in range(FOX_HEADS))
    n_full = i * (tq // t)
    carry = lax.fori_loop(0, n_full, functools.partial(body, diag_offset=None), init)
    for jj in range(tq // t):
        carry = body(n_full + jj, carry, jj * t)
    for hp in range(n_pairs):
        acc_a, acc_b = carry[2 * hp][1], carry[2 * hp + 1][1]
        den = pltpu.roll(jnp.where(low, acc_b, acc_a), HEAD_DIM, axis=1)
        o_ref[0, :, pair_cols[hp]] = jnp.where(low, acc_a, acc_b) / den


def _fox_prompt_attn(zb, F, tq=1024, t=1024):
    B, S, _ = zb.shape
    W = FOX_HEADS * HEAD_DIM
    cq, ck, cv = ((c - Z_NQ) // W for c in (Z_FQ, Z_FK, Z_FV))
    fk = jnp.pad(F.transpose(0, 2, 1), ((0, 0), (0, 8 - FOX_HEADS), (0, 0)))
    fk = fk.reshape(B, 8, S // t, t).transpose(0, 2, 1, 3)
    return pl.pallas_call(
        functools.partial(_fox_prompt_kernel, tq=tq, t=t),
        grid=(B, S // tq),
        in_specs=[pl.BlockSpec((1, tq, W), lambda b, i: (b, i, cq)),
                  pl.BlockSpec((1, S, W), lambda b, i: (b, 0, ck)),
                  pl.BlockSpec((1, S, W), lambda b, i: (b, 0, cv)),
                  pl.BlockSpec((1, tq, FOX_HEADS), lambda b, i: (b, i, 0)),
                  pl.BlockSpec((1, S // t, 8, t), lambda b, i: (b, 0, 0, 0))],
        out_specs=pl.BlockSpec((1, tq, W), lambda b, i: (b, i, 0)),
        out_shape=jax.ShapeDtypeStruct((B, S, W), jnp.float32),
        compiler_params=pltpu.CompilerParams(dimension_semantics=("parallel", "arbitrary"),
                                             vmem_limit_bytes=VMEM_LIMIT_BYTES),
        name="fox_prompt",
    )(zb, zb, zb, F, fk)


def _log_sigmoid(x):
    return jnp.minimum(x, 0.0) - jnp.log1p(jnp.exp(-jnp.abs(x)))


def _mlstm_kernel(q_ref, k_ref, kt_ref, v_ref, og_ref, gc_ref, gr_ref, nw_ref, cn0_ref, m0_ref,
                  h_ref, cn_ref, m_ref, cn_s, m_s, *, nb, L):
    c = pl.program_id(1)
    f32 = jnp.float32
    W = 2 * HEAD_DIM
    n_pairs = MLSTM_HEADS // 2

    @pl.when(c == 0)
    def _():
        cn_s[...] = cn0_ref[...]
        m_s[...] = m0_ref[...]

    lane = lax.broadcasted_iota(jnp.int32, (1, W), 1)
    low = lane < HEAD_DIM
    ti = lax.broadcasted_iota(jnp.int32, (L, L), 0)
    si = lax.broadcasted_iota(jnp.int32, (L, L), 1)
    causal = si <= ti
    tri = causal.astype(f32)
    tri_t = (ti <= si).astype(f32)
    srow = lax.broadcasted_iota(jnp.int32, (W, 2 * W), 0)
    slane = lax.broadcasted_iota(jnp.int32, (W, 2 * W), 1)
    top = srow < HEAD_DIM
    keep_a = top & ((slane < HEAD_DIM) | (slane == W))
    keep_b = (~top) & (((slane >= HEAD_DIM) & (slane < W)) | (slane == W + 1))
    lane_w = lax.broadcasted_iota(jnp.int32, (1, W), 1)
    mdt = k_ref.dtype

    for b in range(nb):
        gcol = gc_ref[b]
        grow = gr_ref[b]
        bcol = jnp.dot(tri, _log_sigmoid(gcol), precision=_HI, preferred_element_type=f32)
        brow = jnp.dot(_log_sigmoid(grow), tri_t, precision=_HI, preferred_element_type=f32)
        for hp in range(n_pairs):
            cols = slice(hp * W, (hp + 1) * W)
            q2 = q_ref[b, :, cols]
            k2 = k_ref[b, :, cols]
            v2 = v_ref[b, :, cols]
            kt2 = kt_ref[b, cols, :]
            cn = cn_s[b, hp]
            r = jnp.dot(q2, cn.astype(mdt), preferred_element_type=f32)
            zq = jnp.zeros_like(q2)
            per_head = []
            for x in range(2):
                h = 2 * hp + x
                qx = jnp.where(low, q2, zq) if x == 0 else jnp.where(low, zq, q2)
                b_c = bcol[:, MLSTM_HEADS + h:MLSTM_HEADS + h + 1]
                b_r = brow[MLSTM_HEADS + h:MLSTM_HEADS + h + 1, :]
                ig_c = gcol[:, h:h + 1]
                ig_r = grow[h:h + 1, :]
                m_prev = m_s[b, 0:1, h:h + 1]
                dmat = jnp.where(causal, b_c - b_r + ig_r, NEG_INF)
                a_c = b_c + m_prev
                m_t = jnp.maximum(a_c, jnp.max(dmat, axis=-1, keepdims=True))
                wq = jnp.exp(dmat - m_t) * lax.dot_general(qx, k2, _NT, preferred_element_type=f32)
                inter = jnp.exp(a_c - m_t)
                wv = jnp.dot(wq.astype(mdt), v2, preferred_element_type=f32)
                den = inter * r[:, W + x:W + x + 1] + jnp.sum(wq, axis=-1, keepdims=True)
                den = jnp.maximum(jnp.abs(den), jnp.exp(-m_t))
                bl = b_c[L - 1:L, :]
                g_c = bl - b_c + ig_c
                m_new = jnp.maximum(bl + m_prev, jnp.max(g_c, axis=0, keepdims=True))
                ws = jnp.exp(g_c - m_new)
                decay = jnp.exp(bl + m_prev - m_new)
                aug = jnp.concatenate([v2.astype(f32) * ws, jnp.where(lane_w == x, ws, 0.0)], axis=1)
                u = jnp.dot(kt2, aug.astype(mdt), preferred_element_type=f32)
                per_head.append((inter, wv, den, decay, u))
                m_s[b, 0:1, h:h + 1] = m_new
            (ia, wva, dena, deca, ua), (ib, wvb, denb, decb, ub) = per_head
            num = jnp.where(low, ia * r[:, :W] + wva, ib * r[:, :W] + wvb)
            hid = num / jnp.where(low, dena, denb)
            cn_s[b, hp] = (jnp.where(top, deca, decb) * cn + jnp.where(keep_a, ua, 0.0)
                           + jnp.where(keep_b, ub, 0.0))
            hid = _sigmoid(og_ref[b, :, cols]) * hid
            mean = lambda t: jnp.where(low, jnp.sum(jnp.where(low, t, 0.0), axis=-1, keepdims=True),
                                       jnp.sum(jnp.where(low, 0.0, t), axis=-1, keepdims=True)) / HEAD_DIM
            mu = mean(hid)
            var = mean(jnp.square(hid - mu))
            h_ref[b, :, cols] = (hid - mu) * lax.rsqrt(var + LN_EPS) * nw_ref[:, cols]

    @pl.when(c == pl.num_programs(1) - 1)
    def _():
        cn_ref[...] = cn_s[...]
        m_ref[...] = m_s[...]


def _mlstm_state_pack(c, n, m):
    Bx = c.shape[0]
    HD, W = HEAD_DIM, 2 * HEAD_DIM
    cp = c.reshape(Bx, MLSTM_HEADS // 2, 2, HD, HD)
    np_ = n.reshape(Bx, MLSTM_HEADS // 2, 2, HD, 1)
    z = lambda w: jnp.zeros((Bx, MLSTM_HEADS // 2, HD, w), jnp.float32)
    top = jnp.concatenate([cp[:, :, 0], z(HD), np_[:, :, 0], z(W - 1)], axis=-1)
    bot = jnp.concatenate([z(HD), cp[:, :, 1], z(1), np_[:, :, 1], z(W - 2)], axis=-1)
    cn = jnp.concatenate([top, bot], axis=-2)
    m8 = jnp.pad(m[:, None, :], ((0, 0), (0, 7), (0, 128 - MLSTM_HEADS)))
    return cn, m8


def _mlstm_state_unpack(cn, m8):
    W = 2 * HEAD_DIM
    c = jnp.stack([cn[:, h // 2, (h % 2) * HEAD_DIM:(h % 2 + 1) * HEAD_DIM,
                      (h % 2) * HEAD_DIM:(h % 2 + 1) * HEAD_DIM] for h in range(MLSTM_HEADS)], axis=1)
    n = jnp.stack([cn[:, h // 2, (h % 2) * HEAD_DIM:(h % 2 + 1) * HEAD_DIM, W + h % 2]
                   for h in range(MLSTM_HEADS)], axis=1)
    return c, n, m8[:, 0, :MLSTM_HEADS]


def _mlstm(q, k, v, og, mi, mf, norm_w, c0, n0, m0, L, nb, mxu_dtype):
    Bx, S, W4 = q.shape
    f32 = jnp.float32
    gates = jnp.concatenate([mi, mf], axis=-1).astype(f32)
    gcol = jnp.pad(gates, ((0, 0), (0, 0), (0, 128 - 2 * MLSTM_HEADS)))
    grow = gates.transpose(0, 2, 1)
    cn0, m8 = _mlstm_state_pack(c0.astype(f32), n0.astype(f32), m0.astype(f32))
    tok = lambda w: pl.BlockSpec((nb, L, w), lambda b, c: (b, c, 0))
    st_cn = pl.BlockSpec((nb, MLSTM_HEADS // 2, 128, 256), lambda b, c: (b, 0, 0, 0))
    st_m = pl.BlockSpec((nb, 8, 128), lambda b, c: (b, 0, 0))
    h, cn, m8 = pl.pallas_call(
        functools.partial(_mlstm_kernel, nb=nb, L=L),
        grid=(Bx // nb, S // L),
        in_specs=[tok(W4), tok(W4), pl.BlockSpec((nb, W4, L), lambda b, c: (b, 0, c)), tok(W4), tok(W4),
                  tok(128), pl.BlockSpec((nb, 8, L), lambda b, c: (b, 0, c)),
                  pl.BlockSpec((1, W4), lambda b, c: (0, 0)), st_cn, st_m],
        out_specs=[tok(W4), st_cn, st_m],
        out_shape=[jax.ShapeDtypeStruct((Bx, S, W4), f32),
                   jax.ShapeDtypeStruct(cn0.shape, f32), jax.ShapeDtypeStruct(m8.shape, f32)],
        scratch_shapes=[pltpu.VMEM((nb, MLSTM_HEADS // 2, 128, 256), f32), pltpu.VMEM((nb, 8, 128), f32)],
        compiler_params=pltpu.CompilerParams(dimension_semantics=("parallel", "arbitrary"),
                                             vmem_limit_bytes=VMEM_LIMIT_BYTES),
        name="mlstm",
    )(q.astype(mxu_dtype), k.astype(mxu_dtype), k.astype(mxu_dtype).transpose(0, 2, 1), v.astype(mxu_dtype),
      og.astype(f32), gcol, grow, norm_w.astype(f32).reshape(1, W4), cn0, m8)
    return h, _mlstm_state_unpack(cn, m8)


def _ln(x, w, b):
    mu = jnp.mean(x, axis=-1, keepdims=True)
    var = jnp.mean(jnp.square(x - mu), axis=-1, keepdims=True)
    return (x - mu) * lax.rsqrt(var + LN_EPS) * w + b


def _ln_kernel(x_ref, w_ref, b_ref, o_ref):
    o_ref[...] = _ln(x_ref[...], w_ref[...], b_ref[...])


def _layer_norm_rows(x, w, b, tm=1024):
    T, D = x.shape
    vec = pl.BlockSpec((1, D), lambda i: (0, 0))
    return pl.pallas_call(
        _ln_kernel, grid=(T // tm,),
        in_specs=[pl.BlockSpec((tm, D), lambda i: (i, 0)), vec, vec],
        out_specs=pl.BlockSpec((tm, D), lambda i: (i, 0)),
        out_shape=jax.ShapeDtypeStruct((T, D), jnp.float32),
        name="layer_norm",
    )(x, w.reshape(1, D), b.reshape(1, D))


Z_GM = 0
Z_NQ = 3072
Z_ROWS = Z_NQ + 256
Z_MQ = 3584
Z_MK = Z_MQ + 256
Z_MV = 4096
Z_MO = Z_MV + 256
Z_FQ = 4608
Z_WIN = Z_FQ + 256
Z_SMALL = Z_WIN + 128
Z_FK = 5120
Z_FV = Z_FK + 256
Z_WIDTH = 5632
Z_TILE = 512
ZB_FIRST_TILE = Z_NQ // Z_TILE
ZB_WIDTH = Z_WIDTH - Z_NQ
SMALL_MI, SMALL_MF, SMALL_FF = 12, 16, 20


def _z_column_order():
    starts = np.concatenate([[0], np.cumsum(IN_SPLITS)])
    seg = lambda i, lo=0, hi=None: np.arange(starts[i] + lo, starts[i] + (IN_SPLITS[i] if hi is None else hi))
    pad = lambda n: np.full((n,), -1)
    order = np.concatenate([
        seg(13), seg(0), seg(1, 0, 256), seg(3), seg(4), seg(5), seg(6),
        seg(9), seg(1, 256, 384), seg(2), seg(7), seg(8), seg(12), pad(128 - 24), seg(10), seg(11)])
    assert order.shape == (Z_WIDTH,)
    return order


def _permute_in_proj(w_in, b_in):
    order = _z_column_order()
    valid = jnp.asarray(order >= 0)
    idx = jnp.asarray(np.maximum(order, 0))
    scale = np.ones((Z_WIDTH,), np.float32)
    for c0 in (Z_NQ, Z_MK, Z_FQ):
        scale[c0:c0 + 256] = ATTN_SCALE
    scale = jnp.asarray(scale)
    w = jnp.where(valid[None, :], w_in[:, idx], 0.0) * scale[None, :]
    b = jnp.where(valid, b_in[idx], 0.0) * scale
    return w, b


def _proj_kernel(x_ref, w_ref, wlo_ref, b_ref, o_ref, ob_ref, xh_s, xl_s, *, hi_tile):
    j = pl.program_id(1)
    f32 = jnp.float32
    _proj_tile(x_ref, w_ref, wlo_ref, b_ref, o_ref, xh_s, xl_s, j, hi_tile)

    @pl.when(j >= ZB_FIRST_TILE)
    def _():
        ob_ref[...] = o_ref[...].astype(jnp.bfloat16)


def _proj_tile(x_ref, w_ref, wlo_ref, b_ref, o_ref, xh_s, xl_s, j, hi_tile):
    f32 = jnp.float32

    @pl.when(j == 0)
    def _():
        x = x_ref[...]
        xh = x.astype(jnp.bfloat16)
        xh_s[...] = xh
        xl_s[...] = (x - xh.astype(f32)).astype(jnp.bfloat16)

    @pl.when(j != hi_tile)
    def _():
        o_ref[...] = jnp.dot(xh_s[...], w_ref[...], preferred_element_type=f32) + b_ref[...]

    @pl.when(j == hi_tile)
    def _():
        acc = jnp.dot(xh_s[...], wlo_ref[...], preferred_element_type=f32)
        acc += jnp.dot(xl_s[...], w_ref[...], preferred_element_type=f32)
        acc += jnp.dot(xh_s[...], w_ref[...], preferred_element_type=f32)
        o_ref[...] = acc + b_ref[...]


def _in_projection(x, w_in, b_in, tm=1024):
    T, D = x.shape
    w, b = _permute_in_proj(w_in, b_in)
    wh = w.astype(jnp.bfloat16)
    hi_tile = Z_NQ // Z_TILE
    wlo = (w[:, Z_NQ:Z_NQ + Z_TILE] - wh[:, Z_NQ:Z_NQ + Z_TILE].astype(jnp.float32)).astype(jnp.bfloat16)
    return pl.pallas_call(
        functools.partial(_proj_kernel, hi_tile=hi_tile),
        grid=(T // tm, Z_WIDTH // Z_TILE),
        in_specs=[pl.BlockSpec((tm, D), lambda i, j: (i, 0)),
                  pl.BlockSpec((D, Z_TILE), lambda i, j: (0, j)),
                  pl.BlockSpec((D, Z_TILE), lambda i, j: (0, 0)),
                  pl.BlockSpec((1, Z_TILE), lambda i, j: (0, j))],
        out_specs=[pl.BlockSpec((tm, Z_TILE), lambda i, j: (i, j)),
                   pl.BlockSpec((tm, Z_TILE), lambda i, j: (i, jnp.maximum(j - ZB_FIRST_TILE, 0)))],
        out_shape=[jax.ShapeDtypeStruct((T, Z_WIDTH), jnp.float32),
                   jax.ShapeDtypeStruct((T, ZB_WIDTH), jnp.bfloat16)],
        scratch_shapes=[pltpu.VMEM((tm, D), jnp.bfloat16), pltpu.VMEM((tm, D), jnp.bfloat16)],
        compiler_params=pltpu.CompilerParams(dimension_semantics=("parallel", "arbitrary"),
                                             vmem_limit_bytes=VMEM_LIMIT_BYTES),
        name="in_projection",
    )(x, wh, wlo, b.reshape(1, Z_WIDTH))


def _merge_kernel(x_ref, gm_ref, on_ref, om_ref, of_ref, wb_ref, wo_ref, lw_ref, lb_ref, o_ref, *, alpha):
    f32, bf16 = jnp.float32, jnp.bfloat16
    y = None
    for m, br in enumerate((on_ref, om_ref, of_ref)):
        proj = jnp.dot(br[...].astype(bf16), wb_ref[m], preferred_element_type=f32)
        term = _sigmoid(gm_ref[:, m * D_MODEL:(m + 1) * D_MODEL]) * proj
        y = term if y is None else y + term
    mix = jnp.dot(y.astype(bf16), wo_ref[...], preferred_element_type=f32)
    o_ref[...] = _ln(alpha * x_ref[...] + mix, lw_ref[...], lb_ref[...])


def _merge_ln(x, z, o_nsa, o_ml, o_fox, w_branch, w_out, ln_w, ln_b, alpha, tm=512):
    T, D = x.shape
    tok = lambda w: pl.BlockSpec((tm, w), lambda i: (i, 0))
    vec = pl.BlockSpec((1, D), lambda i: (0, 0))
    return pl.pallas_call(
        functools.partial(_merge_kernel, alpha=alpha),
        grid=(T // tm,),
        in_specs=[tok(D), tok(N_BRANCH * D), tok(BRANCH_WIDTH), tok(BRANCH_WIDTH), tok(BRANCH_WIDTH),
                  pl.BlockSpec((N_BRANCH, BRANCH_WIDTH, D), lambda i: (0, 0, 0)),
                  pl.BlockSpec((D, D), lambda i: (0, 0)), vec, vec],
        out_specs=tok(D),
        out_shape=jax.ShapeDtypeStruct((T, D), jnp.float32),
        compiler_params=pltpu.CompilerParams(dimension_semantics=("parallel",),
                                             vmem_limit_bytes=VMEM_LIMIT_BYTES),
        name="merge_ln",
    )(x, z, o_nsa, o_ml, o_fox, w_branch.astype(jnp.bfloat16), w_out.astype(jnp.bfloat16),
      ln_w.reshape(1, D), ln_b.reshape(1, D))


def _route(logits):
    tm = logits.shape[0]
    lane = lax.broadcasted_iota(jnp.int32, (tm, 128), 1)
    lanef = lane.astype(jnp.float32)
    big = 1e9
    is_g = lane < N_GROUPS
    lg = jnp.where(is_g, logits, -jnp.inf)
    eg = jnp.exp(lg - jnp.max(lg, axis=-1, keepdims=True))
    pg = eg / jnp.sum(eg, axis=-1, keepdims=True)
    g_val = jnp.max(pg, axis=-1, keepdims=True)
    g_idx = jnp.min(jnp.where(is_g & (pg == g_val), lanef, big), axis=-1, keepdims=True)
    e_lo = N_GROUPS + EXPERTS_PER_GROUP * g_idx
    in_grp = (lanef >= e_lo) & (lanef < e_lo + EXPERTS_PER_GROUP)
    le = jnp.where(in_grp, logits, -jnp.inf)
    ee = jnp.exp(le - jnp.max(le, axis=-1, keepdims=True))
    pe = ee / jnp.sum(ee, axis=-1, keepdims=True)
    v1 = jnp.max(pe, axis=-1, keepdims=True)
    i1 = jnp.min(jnp.where(in_grp & (pe == v1), lanef, big), axis=-1, keepdims=True)
    rest = in_grp & (lanef != i1)
    pe2 = jnp.where(rest, pe, -1.0)
    v2 = jnp.max(pe2, axis=-1, keepdims=True)
    i2 = jnp.min(jnp.where(rest & (pe2 == v2), lanef, big), axis=-1, keepdims=True)
    tot = v1 + v2
    return jnp.where(lanef == i1, g_val * v1 / tot, jnp.where(lanef == i2, g_val * v2 / tot, 0.0))


def _moe_kernel(x_ref, wr_ref, br_ref, wg_ref, wu_ref, wd_ref, lw_ref, lb_ref, o_ref, xb_s, gate_s, acc_s, *, alpha):
    e = pl.program_id(1)
    f32, bf16 = jnp.float32, jnp.bfloat16

    @pl.when(e == 0)
    def _():
        x = x_ref[...]
        xb_s[...] = x.astype(bf16)
        logits = jnp.dot(x, wr_ref[...], precision=_HI, preferred_element_type=f32) + br_ref[...]
        gate_s[...] = _route(logits)
        acc_s[...] = jnp.zeros_like(acc_s)

    lane = lax.broadcasted_iota(jnp.int32, gate_s.shape, 1)
    gate = jnp.sum(jnp.where(lane == N_GROUPS + e, gate_s[...], 0.0), axis=-1, keepdims=True)
    g = jnp.dot(xb_s[...], wg_ref[0].astype(bf16), preferred_element_type=f32)
    u = jnp.dot(xb_s[...], wu_ref[0].astype(bf16), preferred_element_type=f32)
    h = (g * _sigmoid(g)) * u * gate
    acc_s[...] += jnp.dot(h.astype(bf16), wd_ref[0].astype(bf16), preferred_element_type=f32)

    @pl.when(e == pl.num_programs(1) - 1)
    def _():
        o_ref[...] = _ln(alpha * x_ref[...] + acc_s[...], lw_ref[...], lb_ref[...])


def _moe_ln(x, w_group, b_group, w_expert, b_expert, w_gate, w_up, w_down, ln_w, ln_b, alpha, tm=1024):
    T, D = x.shape
    bf16 = jnp.bfloat16
    n_r = N_GROUPS + N_EXPERTS
    wr = jnp.pad(jnp.concatenate([w_group, w_expert], axis=1), ((0, 0), (0, 128 - n_r)))
    br = jnp.pad(jnp.concatenate([b_group, b_expert]), (0, 128 - n_r)).reshape(1, 128)
    vec = pl.BlockSpec((1, D), lambda i, e: (0, 0))
    return pl.pallas_call(
        functools.partial(_moe_kernel, alpha=alpha),
        grid=(T // tm, N_EXPERTS),
        in_specs=[pl.BlockSpec((tm, D), lambda i, e: (i, 0)),
                  pl.BlockSpec((D, 128), lambda i, e: (0, 0)),
                  pl.BlockSpec((1, 128), lambda i, e: (0, 0)),
                  pl.BlockSpec((1, D, D_EXPERT), lambda i, e: (e, 0, 0)),
                  pl.BlockSpec((1, D, D_EXPERT), lambda i, e: (e, 0, 0)),
                  pl.BlockSpec((1, D_EXPERT, D), lambda i, e: (e, 0, 0)), vec, vec],
        out_specs=pl.BlockSpec((tm, D), lambda i, e: (i, 0)),
        out_shape=jax.ShapeDtypeStruct((T, D), jnp.float32),
        scratch_shapes=[pltpu.VMEM((tm, D), bf16), pltpu.VMEM((tm, 128), jnp.float32),
                        pltpu.VMEM((tm, D), jnp.float32)],
        compiler_params=pltpu.CompilerParams(dimension_semantics=("parallel", "arbitrary"),
                                             vmem_limit_bytes=VMEM_LIMIT_BYTES),
        name="moe_ln",
    )(x, wr, br, w_gate, w_up, w_down, ln_w.reshape(1, D), ln_b.reshape(1, D))


N_PAGES = PAST_LEN // PAGE_SIZE
NEW_PAD = 128
DEC_KEYS = PAST_LEN + NEW_PAD
DEC_ROWS = NSA_HEADS * DEC_SEQ


def _pages_token_minor(cache):
    nd = cache.ndim
    t = cache.transpose((0, 1) + tuple(range(3, nd)) + (2,))
    return t.reshape(cache.shape[0] * cache.shape[1], -1, cache.shape[2])


def _page_specs(rows, row_block, layer, n_phys):
    def spec(p):
        return pl.BlockSpec((1, rows, PAGE_SIZE),
                            lambda b, pt: (layer * n_phys + pt[b * N_PAGES + p], row_block, 0))
    return [spec(p) for p in range(N_PAGES)]


def _softmax_rows(s):
    m = jnp.max(s, axis=-1, keepdims=True)
    p = jnp.exp(s - m)
    return p, jnp.sum(p, axis=-1, keepdims=True)


def _fox_decode_kernel(pt_ref, qbd_ref, knew_ref, vnew_ref, fk_ref, fq_ref, *refs):
    pages, o_ref = refs[:N_PAGES], refs[N_PAGES]
    f32, bf16 = jnp.float32, jnp.bfloat16
    W = FOX_HEADS * HEAD_DIM
    qbd = qbd_ref[0]
    s = [jnp.dot(qbd, pg[0, :W, :].astype(bf16), preferred_element_type=f32) for pg in pages]
    s.append(jnp.dot(qbd, knew_ref[0], preferred_element_type=f32))
    s = jnp.concatenate(s, axis=1)
    rowh = lax.broadcasted_iota(jnp.int32, (DEC_ROWS, 1), 0) // DEC_SEQ
    fk = fk_ref[0]
    fk_rows = jnp.where(rowh == 0, fk[0:1], jnp.where(rowh == 1, fk[1:2], jnp.where(rowh == 2, fk[2:3], fk[3:4])))
    col = lax.broadcasted_iota(jnp.int32, (DEC_ROWS, DEC_KEYS), 1)
    t = lax.broadcasted_iota(jnp.int32, (DEC_ROWS, DEC_KEYS), 0) % DEC_SEQ
    ok = (col < PAST_LEN) | (col - PAST_LEN <= t)
    s = jnp.where(ok, s + (fq_ref[0] - fk_rows), NEG_INF)
    p, l = _softmax_rows(s)
    pb = p.astype(bf16)
    o = lax.dot_general(pb[:, PAST_LEN:], vnew_ref[0], _NT, preferred_element_type=f32)
    for i, pg in enumerate(pages):
        o += lax.dot_general(pb[:, i * PAGE_SIZE:(i + 1) * PAGE_SIZE], pg[0, W:, :].astype(bf16), _NT,
                             preferred_element_type=f32)
    o = o / l
    lane_h = lax.broadcasted_iota(jnp.int32, (DEC_ROWS, W), 1) // HEAD_DIM
    o = jnp.where(lane_h == rowh, o, 0.0)
    o_ref[0] = o[0:8] + o[8:16] + o[16:24] + o[24:32]


def _pad_new_t(x):
    return jnp.pad(x.transpose(0, 2, 1), ((0, 0), (0, 0), (0, NEW_PAD - DEC_SEQ))).astype(jnp.bfloat16)


def _fox_decode(q, k_new, v_new, lf_new, cache_kv, cache_lf, page_table, layer):
    DB = q.shape[0]
    f32, bf16 = jnp.float32, jnp.bfloat16
    W = FOX_HEADS * HEAD_DIM
    eye = jnp.eye(FOX_HEADS, dtype=f32)
    qh = q.reshape(DB, DEC_SEQ, FOX_HEADS, HEAD_DIM).transpose(0, 2, 1, 3)
    qbd = (qh[:, :, :, None, :] * eye[None, :, None, :, None]).reshape(DB, DEC_ROWS, W).astype(bf16)
    lf_all = jnp.concatenate([cache_lf[layer][page_table].reshape(DB, PAST_LEN, FOX_HEADS).astype(f32), lf_new], axis=1)
    F = jnp.cumsum(lf_all, axis=1)
    fk = jnp.pad(F.transpose(0, 2, 1), ((0, 0), (0, 8 - FOX_HEADS), (0, DEC_KEYS - PAST_LEN - DEC_SEQ)))
    fq = F[:, PAST_LEN:].transpose(0, 2, 1).reshape(DB, DEC_ROWS, 1)
    per_seq = lambda r, w: pl.BlockSpec((1, r, w), lambda b, pt: (b, 0, 0))
    pages = _pages_token_minor(cache_kv)
    return pl.pallas_call(
        _fox_decode_kernel,
        grid_spec=pltpu.PrefetchScalarGridSpec(
            num_scalar_prefetch=1, grid=(DB,),
            in_specs=[per_seq(DEC_ROWS, W), per_seq(W, NEW_PAD), per_seq(W, NEW_PAD), per_seq(8, DEC_KEYS),
                      per_seq(DEC_ROWS, 1)] + _page_specs(2 * W, 0, layer, cache_kv.shape[1]),
            out_specs=per_seq(DEC_SEQ, W)),
        out_shape=jax.ShapeDtypeStruct((DB, DEC_SEQ, W), f32),
        compiler_params=pltpu.CompilerParams(dimension_semantics=("parallel",), vmem_limit_bytes=VMEM_LIMIT_BYTES),
        name="fox_decode",
    )(page_table.reshape(-1), qbd, _pad_new_t(k_new), _pad_new_t(v_new), fk, fq, *([pages] * N_PAGES))


def _nsa_decode_kernel(pt_ref, qs_ref, qf_ref, kcvc_ref, rnew_ref, wbuf_ref, wnew_ref, g_ref, e_ref, *refs):
    pages, o_ref = refs[:N_PAGES], refs[N_PAGES]
    f32, bf16 = jnp.float32, jnp.bfloat16
    H, T, HD = NSA_HEADS, DEC_SEQ, HEAD_DIM
    qs = qs_ref[0]
    row = lax.broadcasted_iota(jnp.int32, (DEC_ROWS, 1), 0)
    head = row // T
    slope = jnp.where(head == 0, NSA_SLOPES[0], jnp.where(head == 1, NSA_SLOPES[1],
                      jnp.where(head == 2, NSA_SLOPES[2], NSA_SLOPES[3]))).astype(f32)
    posq = PAST_LEN + row % T

    kcvc = kcvc_ref[0]
    n_cb = kcvc.shape[0]
    sc = lax.dot_general(qf_ref[0], kcvc, _NT, precision=_HI, preferred_element_type=f32)
    jb = lax.broadcasted_iota(jnp.int32, (DEC_ROWS, n_cb), 1)
    distc = posq - (jb * NSA_BLOCK + NSA_BLOCK - 1)
    okc = distc >= 0
    sc = jnp.where(okc, sc - slope * distc.astype(f32), NEG_INF)
    pc = jnp.exp(sc - jnp.max(sc, axis=-1, keepdims=True)) * okc.astype(f32)
    pc = pc / jnp.maximum(jnp.sum(pc, axis=-1, keepdims=True), TINY)
    o_cmp = jnp.dot(pc.astype(bf16), kcvc.astype(bf16), preferred_element_type=f32)[:, HD:]
    imp = pc[0:T] + pc[T:2 * T] + pc[2 * T:3 * T] + pc[3 * T:4 * T]
    imp = jnp.concatenate([imp, jnp.zeros((T, 128 - n_cb), f32)], axis=1)
    cur = (PAST_LEN + lax.broadcasted_iota(jnp.int32, (T, 1), 0)) // NSA_BLOCK
    msel = _select_blocks(imp, cur, NSA_TOPK).astype(bf16)

    mexp = jnp.dot(msel, e_ref[...], preferred_element_type=f32)
    col = lax.broadcasted_iota(jnp.int32, (T, DEC_KEYS), 1)
    tq = lax.broadcasted_iota(jnp.int32, (T, DEC_KEYS), 0)
    d = PAST_LEN + tq - col
    ok = (mexp > 0.5) & (d >= 0)
    kv = [pg[0].astype(bf16) for pg in pages] + [rnew_ref[0]]
    s = jnp.concatenate([jnp.dot(qs, x[:HD], preferred_element_type=f32) for x in kv], axis=1)
    s = jnp.where(_tile_rows(ok, H), s - slope * _tile_rows(d.astype(f32), H), NEG_INF)
    p, l = _softmax_rows(s)
    pb = p.astype(bf16)
    acc = jnp.zeros((DEC_ROWS, HD), f32)
    for i, x in enumerate(kv):
        acc += lax.dot_general(pb[:, i * PAGE_SIZE:(i + 1) * PAGE_SIZE], x[HD:], _NT, preferred_element_type=f32)
    o_sel = acc / l

    wb = wbuf_ref.shape[2]
    kvw = [wbuf_ref[0].astype(bf16), wnew_ref[0]]
    sw = jnp.concatenate([jnp.dot(qs, x[:HD], preferred_element_type=f32) for x in kvw], axis=1)
    colw = lax.broadcasted_iota(jnp.int32, (T, wb + NEW_PAD), 1)
    tw = lax.broadcasted_iota(jnp.int32, (T, wb + NEW_PAD), 0)
    dw = wb + tw - colw
    okw = (dw >= 0) & (dw < NSA_WINDOW)
    sw = jnp.where(_tile_rows(okw, H), sw - slope * _tile_rows(dw.astype(f32), H), NEG_INF)
    pw, lw = _softmax_rows(sw)
    pwb = pw.astype(bf16)
    accw = (lax.dot_general(pwb[:, :wb], kvw[0][HD:], _NT, preferred_element_type=f32)
            + lax.dot_general(pwb[:, wb:], kvw[1][HD:], _NT, preferred_element_type=f32))
    o_win = accw / lw

    g = _sigmoid(g_ref[0])
    gate = lambda c: jnp.concatenate([g[:, 3 * h + c:3 * h + c + 1] for h in range(H)], axis=0)
    o_ref[0] = gate(0) * o_cmp + gate(1) * o_sel + gate(2) * o_win


def _compress_pages_kernel(x_ref, wk_ref, wv_ref, o_ref):
    tm = x_ref.shape[0]
    f32 = jnp.float32
    acc_k = jnp.zeros((tm, 2 * HEAD_DIM), f32)
    acc_v = jnp.zeros((tm, 2 * HEAD_DIM), f32)
    for d in range(HEAD_DIM):
        acc_k += jnp.dot(x_ref[:, d, :], wk_ref[d], precision=_HI, preferred_element_type=f32)
        acc_v += jnp.dot(x_ref[:, HEAD_DIM + d, :], wv_ref[d], preferred_element_type=f32)
    o_ref[...] = jnp.concatenate([acc_k, acc_v], axis=1)


def _nsa_compress_pages(pages, w_ck, w_cv, layer, n_phys, tm=128):
    eye = jnp.eye(PAGE_SIZE // NSA_BLOCK, dtype=jnp.float32)
    big = lambda w: jnp.einsum('pde,bc->dbpce', w, eye).reshape(HEAD_DIM, PAGE_SIZE, 2 * HEAD_DIM)
    wspec = pl.BlockSpec((HEAD_DIM, PAGE_SIZE, 2 * HEAD_DIM), lambda i: (0, 0, 0))
    return pl.pallas_call(
        _compress_pages_kernel,
        grid=(n_phys // tm,),
        in_specs=[pl.BlockSpec((tm, 2 * HEAD_DIM, PAGE_SIZE), lambda i: (layer * (n_phys // tm) + i, 0, 0)),
                  wspec, wspec],
        out_specs=pl.BlockSpec((tm, 4 * HEAD_DIM), lambda i: (i, 0)),
        out_shape=jax.ShapeDtypeStruct((n_phys, 4 * HEAD_DIM), jnp.float32),
        compiler_params=pltpu.CompilerParams(dimension_semantics=("parallel",), vmem_limit_bytes=VMEM_LIMIT_BYTES),
        name="nsa_compress_pages",
    )(pages, big(w_ck), big(w_cv))


def _nsa_decode(nq, rows_new, win_new, gates, cache_rows, cache_win, page_table, w_ck, w_cv, layer):
    DB = nq.shape[0]
    f32, bf16 = jnp.float32, jnp.bfloat16
    HD = HEAD_DIM
    n_phys = cache_rows.shape[1]
    n_blk = PAGE_SIZE // NSA_BLOCK
    pages = _pages_token_minor(cache_rows)
    kcvc = _nsa_compress_pages(pages, w_ck, w_cv, layer, n_phys)[page_table]
    kcvc = kcvc.reshape(DB, N_PAGES, 2, n_blk, HD).transpose(0, 1, 3, 2, 4).reshape(DB, N_PAGES * n_blk, 2 * HD)
    stack = lambda x: x.reshape(DB, DEC_SEQ, NSA_HEADS, HD).transpose(0, 2, 1, 3).reshape(DB, DEC_ROWS, HD)
    qs = stack(nq).astype(bf16)
    qf = jnp.pad(stack(nq), ((0, 0), (0, 0), (0, HD))).astype(f32)
    win_t = cache_win.transpose(0, 1, 3, 4, 2).reshape(cache_win.shape[0] * DB, 2 * HD, cache_win.shape[2])
    colk = np.arange(DEC_KEYS)
    e = (np.arange(128)[:, None] == colk[None, :] // NSA_BLOCK) & (colk[None, :] < PAST_LEN + NSA_BLOCK)
    wb = cache_win.shape[2]
    per_seq = lambda r, w: pl.BlockSpec((1, r, w), lambda b, pt: (b, 0, 0))
    o = pl.pallas_call(
        _nsa_decode_kernel,
        grid_spec=pltpu.PrefetchScalarGridSpec(
            num_scalar_prefetch=1, grid=(DB,),
            in_specs=[per_seq(DEC_ROWS, HD), per_seq(DEC_ROWS, 2 * HD), per_seq(PAST_LEN // NSA_BLOCK, 2 * HD),
                      per_seq(2 * HD, NEW_PAD),
                      pl.BlockSpec((1, 2 * HD, wb), lambda b, pt: (layer * DB + b, 0, 0)),
                      per_seq(2 * HD, NEW_PAD), per_seq(DEC_SEQ, 128),
                      pl.BlockSpec((128, DEC_KEYS), lambda b, pt: (0, 0))]
                     + _page_specs(2 * HD, 1, layer, n_phys),
            out_specs=per_seq(DEC_ROWS, HD)),
        out_shape=jax.ShapeDtypeStruct((DB, DEC_ROWS, HD), f32),
        compiler_params=pltpu.CompilerParams(dimension_semantics=("parallel",), vmem_limit_bytes=VMEM_LIMIT_BYTES),
        name="nsa_decode",
    )(page_table.reshape(-1), qs, qf, kcvc, _pad_new_t(rows_new[..., 2 * HD:]), win_t,
      _pad_new_t(win_new), gates, jnp.asarray(e, bf16), *([pages] * N_PAGES))
    return o.reshape(DB, NSA_HEADS, DEC_SEQ, HD).transpose(0, 2, 1, 3).reshape(DB, DEC_SEQ, BRANCH_WIDTH)


def _copy_kernel(pt_ref, x_ref, o_ref):
    o_ref[...] = x_ref[...]


def _gather_pages(cache, page_table):
    db, n_pages = page_table.shape
    _, page, w = cache.shape
    out = pl.pallas_call(
        _copy_kernel,
        grid_spec=pltpu.PrefetchScalarGridSpec(
            num_scalar_prefetch=1, grid=(db * n_pages,),
            in_specs=[pl.BlockSpec((1, page, w), lambda i, pt: (pt[i], 0, 0))],
            out_specs=pl.BlockSpec((1, page, w), lambda i, pt: (i, 0, 0))),
        out_shape=jax.ShapeDtypeStruct((db * n_pages, page, w), cache.dtype),
        name="gather_pages",
    )(page_table.reshape(-1), cache)
    return out.reshape(db, n_pages * page, w)


def _split_points():
    return [int(v) for v in np.cumsum(IN_SPLITS)[:-1]]


def _layer_norm(x, w, b):
    xf = x.astype(jnp.float32)
    mu = jnp.mean(xf, axis=-1, keepdims=True)
    var = jnp.mean(jnp.square(xf - mu), axis=-1, keepdims=True)
    y = (xf - mu) * lax.rsqrt(var + LN_EPS) * w.astype(jnp.float32) + b.astype(jnp.float32)
    return y.astype(x.dtype)


def _masked_softmax(s, mask):
    s = jnp.where(mask, s.astype(jnp.float32), NEG_INF)
    p = jnp.exp(s - jnp.max(s, axis=-1, keepdims=True)) * mask
    return p / jnp.maximum(jnp.sum(p, axis=-1, keepdims=True), TINY)


def _alibi_slopes(n):
    return jnp.exp2(-8.0 * jnp.arange(1, n + 1, dtype=jnp.float32) / n)


def _project(x, w_in, b_in):
    B, L, _ = x.shape
    f32 = jnp.float32
    z = _pallas_linear(x.reshape(B * L, -1), w_in, b_in).reshape(B, L, -1)
    (nq, nkv, ng, mq, mk, mv, mo, mi, mf, fq, fk, fv, ff, gm) = jnp.split(z, _split_points(), axis=-1)
    heads = lambda t, h: t.reshape(B, L, h, HEAD_DIM)
    rows = nkv.reshape(B, L, 6, HEAD_DIM)
    return dict(
        nsa_q=heads(nq, NSA_HEADS),
        nsa_rows=rows[:, :, :NSA_ROWS],
        nsa_win=rows[:, :, NSA_ROWS:],
        nsa_g=jax.nn.sigmoid(ng.astype(f32)).reshape(B, L, NSA_HEADS, 3),
        nsa_g_raw=ng,
        ml_q=heads(mq, MLSTM_HEADS).astype(f32),
        ml_k=heads(mk, MLSTM_HEADS).astype(f32) * ATTN_SCALE,
        ml_v=heads(mv, MLSTM_HEADS).astype(f32),
        ml_o=jax.nn.sigmoid(mo.astype(f32)).reshape(B, L, MLSTM_HEADS, HEAD_DIM),
        ml_i=mi.astype(f32),
        ml_lf=jax.nn.log_sigmoid(mf.astype(f32)),
        fox_q=heads(fq, FOX_HEADS),
        fox_k=heads(fk, FOX_HEADS),
        fox_v=heads(fv, FOX_HEADS),
        fox_lf=jax.nn.log_sigmoid(ff.astype(f32)),
        merge=jax.nn.sigmoid(gm.astype(f32)).reshape(B, L, N_BRANCH, D_MODEL))


def _nsa_compress(k, w):
    B, L, _ = k.shape
    n_cb = L // NSA_BLOCK
    blocks = k[:, :n_cb * NSA_BLOCK].reshape(B, n_cb, NSA_BLOCK, HEAD_DIM)
    return jnp.einsum('bjpd,pde->bje', blocks, w)


def _nsa_cmp_attend(q, kc, vc, pos_q, slopes):
    n_cb = kc.shape[1]
    blk_end = jnp.arange(n_cb) * NSA_BLOCK + NSA_BLOCK - 1
    dist = pos_q[:, None] - blk_end[None, :]
    s = jnp.einsum('bqhd,bjd->bhqj', q, kc) * ATTN_SCALE - slopes[:, None, None] * dist.astype(jnp.float32)
    p = _masked_softmax(s, dist >= 0)
    o = jnp.einsum('bhqj,bjd->bqhd', p.astype(vc.dtype), vc)
    return o, jnp.sum(p, axis=1)


def _nsa_select_blocks(imp, pos_q, n_sb):
    n_cb = imp.shape[-1]
    imp = jnp.pad(imp, ((0, 0), (0, 0), (0, n_sb - n_cb)))
    j = jnp.arange(n_sb)[None, :]
    cur = (pos_q // NSA_BLOCK)[:, None]
    score = jnp.where(j > cur, NEG_INF, jnp.where((j == cur) | (j == 0), FORCED_SCORE, imp))
    vals, idx = lax.top_k(score, min(NSA_TOPK, n_sb))
    return idx, vals > 0.5 * NEG_INF


def _nsa_sel_attend(q, kb, vb, idx, valid, pos_q, slopes):
    B, Q = q.shape[:2]
    bi = jnp.arange(B)[:, None, None]
    kg = kb[bi, idx].reshape(B, Q, -1, HEAD_DIM)
    vg = vb[bi, idx].reshape(B, Q, -1, HEAD_DIM)
    pos_k = (idx[..., None] * NSA_BLOCK + jnp.arange(NSA_BLOCK)).reshape(B, Q, -1)
    ok = jnp.repeat(valid, NSA_BLOCK, axis=-1) & (pos_k <= pos_q[None, :, None])
    dist = (pos_q[None, :, None] - pos_k).astype(jnp.float32)
    s = jnp.einsum('bqhd,bqkd->bhqk', q, kg) * ATTN_SCALE - slopes[None, :, None, None] * dist[:, None]
    p = _masked_softmax(s, ok[:, None])
    return jnp.einsum('bhqk,bqkd->bqhd', p.astype(vg.dtype), vg)


def _nsa_cmp_sel(q, rows, pos_q, w_ck, w_cv, slopes):
    B, Q = q.shape[:2]
    L = rows.shape[1]
    kc = _nsa_compress(rows[:, :, 0], w_ck)
    vc = _nsa_compress(rows[:, :, 1], w_cv)
    o_cmp, imp = _nsa_cmp_attend(q, kc, vc, pos_q, slopes)
    n_sb = -(-L // NSA_BLOCK)
    idx, valid = _nsa_select_blocks(imp, pos_q, n_sb)
    pad = ((0, 0), (0, n_sb * NSA_BLOCK - L), (0, 0))
    kb = jnp.pad(rows[:, :, 2], pad).reshape(B, n_sb, NSA_BLOCK, HEAD_DIM)
    vb = jnp.pad(rows[:, :, 3], pad).reshape(B, n_sb, NSA_BLOCK, HEAD_DIM)
    qb = Q_BLOCK if Q % Q_BLOCK == 0 else Q
    nqb = Q // qb

    def block(args):
        qc, ic, okc, pc = args
        return _nsa_sel_attend(qc, kb, vb, ic, okc, pc, slopes)

    xs = (q.reshape(B, nqb, qb, NSA_HEADS, HEAD_DIM).swapaxes(0, 1),
          idx.reshape(B, nqb, qb, -1).swapaxes(0, 1),
          valid.reshape(B, nqb, qb, -1).swapaxes(0, 1),
          pos_q.reshape(nqb, qb))
    o_sel = lax.map(block, xs).swapaxes(0, 1).reshape(B, Q, NSA_HEADS, HEAD_DIM)
    return o_cmp, o_sel


def _nsa_win_attend(q, k, v, pos_q, pos_k, slopes):
    dist = pos_q[:, None] - pos_k[None, :]
    ok = (dist >= 0) & (dist < NSA_WINDOW) & (pos_k[None, :] >= 0)
    s = jnp.einsum('bqhd,bkd->bhqk', q, k) * ATTN_SCALE - slopes[:, None, None] * dist.astype(jnp.float32)
    p = _masked_softmax(s, ok)
    return jnp.einsum('bhqk,bkd->bqhd', p.astype(v.dtype), v)


def _nsa_win_prompt(q, k, v, slopes):
    B, S = q.shape[:2]
    nqb = S // Q_BLOCK
    nb = NSA_WINDOW // Q_BLOCK
    pad = ((0, 0), (NSA_WINDOW, 0), (0, 0))
    kp = jnp.pad(k, pad).reshape(B, nqb + nb, Q_BLOCK, HEAD_DIM)
    vp = jnp.pad(v, pad).reshape(B, nqb + nb, Q_BLOCK, HEAD_DIM)
    band = jnp.arange(nqb)[:, None] + jnp.arange(nb + 1)[None, :]
    kband = kp[:, band].reshape(B, nqb, (nb + 1) * Q_BLOCK, HEAD_DIM)
    vband = vp[:, band].reshape(B, nqb, (nb + 1) * Q_BLOCK, HEAD_DIM)
    qblk = q.reshape(B, nqb, Q_BLOCK, NSA_HEADS, HEAD_DIM)
    pos_q = jnp.arange(S).reshape(nqb, Q_BLOCK)
    pos_k = (jnp.arange(nqb) * Q_BLOCK - NSA_WINDOW)[:, None] + jnp.arange((nb + 1) * Q_BLOCK)[None, :]
    o = jax.vmap(_nsa_win_attend, in_axes=(1, 1, 1, 0, 0, None), out_axes=1)(qblk, kband, vband, pos_q, pos_k, slopes)
    return o.reshape(B, S, NSA_HEADS, HEAD_DIM)


def _nsa_combine(g, o_cmp, o_sel, o_win):
    return jnp.einsum('blhc,cblhd->blhd', g, jnp.stack([o_cmp, o_sel, o_win]))


def _mlstm_chunk(carry, xs):
    c, n, m = carry
    q, k, v, ig, lf = xs
    L = q.shape[1]
    b = jnp.cumsum(lf, axis=1)
    causal = jnp.tril(jnp.ones((L, L), bool))[None, :, :, None]
    dmat = jnp.where(causal, b[:, :, None, :] - b[:, None, :, :] + ig[:, None, :, :], NEG_INF)
    a = b + m[:, None, :]
    m_t = jnp.maximum(a, jnp.max(dmat, axis=2))
    wq = jnp.exp(dmat - m_t[:, :, None, :]) * jnp.einsum('bthd,bshd->btsh', q, k)
    inter = jnp.exp(a - m_t)
    num = inter[..., None] * jnp.einsum('bthd,bhde->bthe', q, c) + jnp.einsum('btsh,bshe->bthe', wq, v)
    den = inter * jnp.einsum('bthd,bhd->bth', q, n) + jnp.sum(wq, axis=2)
    h = num / jnp.maximum(jnp.abs(den), jnp.exp(-m_t))[..., None]
    bl = b[:, -1]
    g = bl[:, None, :] - b + ig
    m_new = jnp.maximum(bl + m, jnp.max(g, axis=1))
    ws = jnp.exp(g - m_new[:, None, :])
    decay = jnp.exp(bl + m - m_new)
    c_new = decay[..., None, None] * c + jnp.einsum('bsh,bshd,bshe->bhde', ws, k, v)
    n_new = decay[..., None] * n + jnp.einsum('bsh,bshd->bhd', ws, k)
    return (c_new, n_new, m_new), h


def _mlstm_prompt(q, k, v, ig, lf):
    B, S, H, _ = q.shape
    ch = MLSTM_CHUNK if S % MLSTM_CHUNK == 0 else S
    nc = S // ch
    to_chunks = lambda t: t.reshape((B, nc, ch) + t.shape[2:]).swapaxes(0, 1)
    f32 = jnp.float32
    init = (jnp.zeros((B, H, HEAD_DIM, HEAD_DIM), f32), jnp.zeros((B, H, HEAD_DIM), f32), jnp.zeros((B, H), f32))
    state, h = lax.scan(_mlstm_chunk, init, tuple(to_chunks(t) for t in (q, k, v, ig, lf)))
    return h.swapaxes(0, 1).reshape(B, S, H, HEAD_DIM), state


def _mlstm_readout(h, o_gate, norm_w):
    h = o_gate * h
    mu = jnp.mean(h, axis=-1, keepdims=True)
    var = jnp.mean(jnp.square(h - mu), axis=-1, keepdims=True)
    return (h - mu) * lax.rsqrt(var + LN_EPS) * norm_w.astype(jnp.float32).reshape(MLSTM_HEADS, HEAD_DIM)


def _fox_attend(q, k, v, fq, fk, pos_q, pos_k):
    s = jnp.einsum('bqhd,bkhd->bhqk', q, k) * ATTN_SCALE
    s = s + (fq.transpose(0, 2, 1)[..., None] - fk.transpose(0, 2, 1)[:, :, None, :])
    p = _masked_softmax(s, pos_k[None, :] <= pos_q[:, None])
    return jnp.einsum('bhqk,bkhd->bqhd', p.astype(v.dtype), v)


def _fox_prompt(q, k, v, F):
    B, S = q.shape[:2]
    qb = Q_BLOCK if S % Q_BLOCK == 0 else S
    nqb = S // qb
    pos = jnp.arange(S)

    def block(args):
        qc, fc, pc = args
        return _fox_attend(qc, k, v, fc, F, pc, pos)

    xs = (q.reshape(B, nqb, qb, FOX_HEADS, HEAD_DIM).swapaxes(0, 1),
          F.reshape(B, nqb, qb, FOX_HEADS).swapaxes(0, 1),
          pos.reshape(nqb, qb))
    return lax.map(block, xs).swapaxes(0, 1).reshape(B, S, FOX_HEADS, HEAD_DIM)


def _merge(gates, o_nsa, o_ml, o_fox, w_branch, w_out, dtype):
    B, L = o_nsa.shape[:2]
    br = jnp.stack([o.reshape(B, L, BRANCH_WIDTH).astype(dtype) for o in (o_nsa, o_ml, o_fox)], axis=2)
    proj = jnp.einsum('blmc,mcd->blmd', br, w_branch)
    y = jnp.einsum('blmd,blmd->bld', gates, proj).astype(dtype)
    return y @ w_out


def _moe(x, w_group, b_group, w_expert, b_expert, w_gate, w_up, w_down):
    B, L, D = x.shape
    t = x.reshape(B * L, D)
    pg = jax.nn.softmax((t @ w_group + b_group).astype(jnp.float32), axis=-1)
    g_val, g_idx = lax.top_k(pg, 1)
    le = (t @ w_expert + b_expert).astype(jnp.float32).reshape(-1, N_GROUPS, EXPERTS_PER_GROUP)
    le = jnp.take_along_axis(le, g_idx[:, :, None], axis=1)[:, 0]
    e_val, e_idx = lax.top_k(jax.nn.softmax(le, axis=-1), TOP_K_IN_GROUP)
    e_w = g_val * e_val / jnp.sum(e_val, axis=-1, keepdims=True)
    gate = jnp.sum(jax.nn.one_hot(g_idx * EXPERTS_PER_GROUP + e_idx, N_EXPERTS, dtype=jnp.float32) * e_w[..., None], axis=1)
    h = jax.nn.silu(jnp.einsum('td,edf->tef', t, w_gate)) * jnp.einsum('td,edf->tef', t, w_up)
    h = h * gate[:, :, None].astype(h.dtype)
    return jnp.einsum('tef,efd->td', h, w_down).reshape(B, L, D)


def _mixers_prompt(x, w_in, b_in, w_ck, w_cv, ml_norm_w, w_branch, w_out, slopes):
    B, S, _ = x.shape
    pr = _project(x, w_in, b_in)
    pos = jnp.arange(S)
    o_nsa = _nsa_prompt(pr['nsa_q'].reshape(B, S, -1), pr['nsa_rows'].reshape(B, S, -1),
                        pr['nsa_win'].reshape(B, S, -1), jnp.pad(pr['nsa_g_raw'], ((0, 0), (0, 0), (0, 116))),
                        w_ck, w_cv).reshape(B, S, NSA_HEADS, HEAD_DIM)
    h, (c, n, m) = _mlstm_prompt(pr['ml_q'], pr['ml_k'], pr['ml_v'], pr['ml_i'], pr['ml_lf'])
    o_ml = _mlstm_readout(h, pr['ml_o'], ml_norm_w)
    flat = lambda t: t.reshape(B, S, -1)
    o_fox = _fox_prompt_attn(flat(pr['fox_q']), flat(pr['fox_k']), flat(pr['fox_v']),
                             jnp.cumsum(pr['fox_lf'], axis=1)).reshape(B, S, FOX_HEADS, HEAD_DIM)
    out = _merge(pr['merge'], o_nsa, o_ml, o_fox, w_branch, w_out, x.dtype)
    w_keep = min(NSA_WINDOW, S)
    new = (pr['nsa_rows'], pr['nsa_win'][:, S - w_keep:], jnp.stack([pr['fox_k'], pr['fox_v']], axis=2),
           pr['fox_lf'], c, n, m)
    return out, new


def _mixers_sample(x, c_nsa, c_win, c_fox_kv, c_fox_lf, s_c, s_n, s_m, page_table,
                   w_in, b_in, w_ck, w_cv, ml_norm_w, w_branch, w_out, slopes):
    DB, T, _ = x.shape
    past = page_table.shape[1] * PAGE_SIZE
    f32 = jnp.float32
    pr = _project(x, w_in, b_in)
    pos_q = past + jnp.arange(T)
    rows_past = _gather_pages(c_nsa.reshape(-1, PAGE_SIZE, NSA_ROWS * HEAD_DIM), page_table)
    rows_past = rows_past.reshape(DB, past, NSA_ROWS, HEAD_DIM)
    rows_all = jnp.concatenate([rows_past, pr['nsa_rows'].astype(rows_past.dtype)], axis=1)
    o_cmp, o_sel = _nsa_cmp_sel(pr['nsa_q'], rows_all, pos_q, w_ck, w_cv, slopes)
    wb = c_win.shape[1]
    win_all = jnp.concatenate([c_win, pr['nsa_win'].astype(c_win.dtype)], axis=1)
    pos_k = past - wb + jnp.arange(wb + T)
    o_win = _nsa_win_attend(pr['nsa_q'], win_all[:, :, 0], win_all[:, :, 1], pos_q, pos_k, slopes)
    o_nsa = _nsa_combine(pr['nsa_g'], o_cmp, o_sel, o_win)
    (c, n, m), h = _mlstm_chunk((s_c.astype(f32), s_n.astype(f32), s_m.astype(f32)),
                                (pr['ml_q'], pr['ml_k'], pr['ml_v'], pr['ml_i'], pr['ml_lf']))
    o_ml = _mlstm_readout(h, pr['ml_o'], ml_norm_w)
    kv_past = _gather_pages(c_fox_kv.reshape(-1, PAGE_SIZE, 2 * FOX_HEADS * HEAD_DIM), page_table)
    kv_past = kv_past.reshape(DB, past, 2, FOX_HEADS, HEAD_DIM)
    k_all = jnp.concatenate([kv_past[:, :, 0], pr['fox_k'].astype(kv_past.dtype)], axis=1)
    v_all = jnp.concatenate([kv_past[:, :, 1], pr['fox_v'].astype(kv_past.dtype)], axis=1)
    lf_all = jnp.concatenate([c_fox_lf[page_table].reshape(DB, past, FOX_HEADS).astype(f32), pr['fox_lf']], axis=1)
    F = jnp.cumsum(lf_all, axis=1)
    o_fox = _fox_attend(pr['fox_q'], k_all, v_all, F[:, past:], F, pos_q, jnp.arange(past + T))
    out = _merge(pr['merge'], o_nsa, o_ml, o_fox, w_branch, w_out, x.dtype)
    new = (pr['nsa_rows'], win_all[:, T:], jnp.stack([pr['fox_k'], pr['fox_v']], axis=2),
           pr['fox_lf'], c, n, m)
    return out, new


def _stack_layers(states):
    return tuple(jnp.stack(list(a)) for a in zip(*states))


def kernel(x_prompt, x_sample, cache_nsa, cache_nsa_win, cache_fox_kv, cache_fox_logf,
           state_mlstm_c, state_mlstm_n, state_mlstm_m, page_table,
           ln_in_w, ln_in_b, w_in, b_in, nsa_w_ck, nsa_w_cv, mlstm_norm_w, w_branch, w_out,
           ln1_w, ln1_b, moe_w_group, moe_b_group, moe_w_expert, moe_b_expert,
           moe_w_gate, moe_w_up, moe_w_down, ln2_w, ln2_b):
    f32, bf16 = jnp.float32, jnp.bfloat16
    alpha = (2.0 * DEPTH) ** 0.25
    B, S, D = x_prompt.shape
    DB, T, _ = x_sample.shape
    TP, TS = B * S, DB * T
    HD = HEAD_DIM
    xp = _layer_norm_rows(x_prompt.reshape(TP, D), ln_in_w, ln_in_b)
    xs = _layer_norm_rows(x_sample.reshape(TS, D), ln_in_w, ln_in_b)
    cols = lambda a, c0, w: a[..., c0:c0 + w]
    colsb = lambda a, c0, w: a[..., c0 - Z_NQ:c0 - Z_NQ + w]
    zero_state = (jnp.zeros((B, MLSTM_HEADS, HD, HD), f32), jnp.zeros((B, MLSTM_HEADS, HD), f32),
                  jnp.zeros((B, MLSTM_HEADS), f32))
    new_p, new_s = [], []
    for l in range(DEPTH):
        zp2, zbp2 = _in_projection(xp, w_in[l], b_in[l])
        zs2, zbs2 = _in_projection(xs, w_in[l], b_in[l])
        zp, zbp = zp2.reshape(B, S, Z_WIDTH), zbp2.reshape(B, S, ZB_WIDTH)
        zs, zbs = zs2.reshape(DB, T, Z_WIDTH), zbs2.reshape(DB, T, ZB_WIDTH)
        small_p, small_s = cols(zp, Z_SMALL, 128), cols(zs, Z_SMALL, 128)

        o_nsa_p = _nsa_prompt(cols(zp, Z_NQ, 256), colsb(zbp, Z_NQ, 256), cols(zp, Z_ROWS, 256),
                              colsb(zbp, Z_ROWS, 256), colsb(zbp, Z_WIN, 128), small_p, nsa_w_ck[l], nsa_w_cv[l])
        o_nsa_s = _nsa_decode(cols(zs, Z_NQ, 256), colsb(zbs, Z_ROWS, 256), colsb(zbs, Z_WIN, 128), small_s,
                              cache_nsa, cache_nsa_win, page_table, nsa_w_ck[l], nsa_w_cv[l], l)

        def mlstm(q, k, v, og, small, state, L, nb, dt):
            return _mlstm(q, k, v, og, cols(small, SMALL_MI, MLSTM_HEADS), cols(small, SMALL_MF, MLSTM_HEADS),
                          mlstm_norm_w[l], *state, L, nb, dt)

        o_ml_p, st_ml_p = mlstm(colsb(zbp, Z_MQ, 256), colsb(zbp, Z_MK, 256), colsb(zbp, Z_MV, 256),
                                cols(zp, Z_MO, 256), small_p, zero_state, 128, B, bf16)
        o_ml_s, st_ml_s = mlstm(cols(zs, Z_MQ, 256), cols(zs, Z_MK, 256), cols(zs, Z_MV, 256), cols(zs, Z_MO, 256),
                                small_s, (state_mlstm_c[l], state_mlstm_n[l], state_mlstm_m[l]), T, 4, f32)

        lf_p = jax.nn.log_sigmoid(cols(small_p, SMALL_FF, FOX_HEADS))
        lf_s = jax.nn.log_sigmoid(cols(small_s, SMALL_FF, FOX_HEADS))
        o_fox_p = _fox_prompt_attn(zbp, jnp.cumsum(lf_p, axis=1))
        o_fox_s = _fox_decode(colsb(zbs, Z_FQ, 256), colsb(zbs, Z_FK, 256), colsb(zbs, Z_FV, 256), lf_s,
                              cache_fox_kv, cache_fox_logf, page_table, l)

        flat = lambda a: a.reshape(-1, a.shape[-1])
        moe_w = (moe_w_group[l], moe_b_group[l], moe_w_expert[l], moe_b_expert[l],
                 moe_w_gate[l], moe_w_up[l], moe_w_down[l], ln2_w[l], ln2_b[l], alpha)
        xp = _merge_ln(xp, zp2, flat(o_nsa_p), flat(o_ml_p), flat(o_fox_p),
                       w_branch[l], w_out[l], ln1_w[l], ln1_b[l], alpha)
        xp = _moe_ln(xp, *moe_w)
        xs = _merge_ln(xs, zs2, flat(o_nsa_s), flat(o_ml_s), flat(o_fox_s),
                       w_branch[l], w_out[l], ln1_w[l], ln1_b[l], alpha)
        xs = _moe_ln(xs, *moe_w)

        w_keep = min(NSA_WINDOW, S)
        new_p.append((cols(zp, Z_ROWS, 256).reshape(B, S, NSA_ROWS, HD),
                      cols(zp, Z_WIN, 128)[:, S - w_keep:].reshape(B, w_keep, 2, HD),
                      cols(zp, Z_FK, 512).reshape(B, S, 2, FOX_HEADS, HD), lf_p) + tuple(st_ml_p))
        win_new = cols(zs, Z_WIN, 128).reshape(DB, T, 2, HD).astype(cache_nsa_win.dtype)
        new_s.append((cols(zs, Z_ROWS, 256).reshape(DB, T, NSA_ROWS, HD),
                      jnp.concatenate([cache_nsa_win[l][:, T:], win_new], axis=1),
                      cols(zs, Z_FK, 512).reshape(DB, T, 2, FOX_HEADS, HD), lf_s) + tuple(st_ml_s))
    (p_nsa_rows, p_nsa_win, p_fox_kv, p_fox_logf, p_mlstm_c, p_mlstm_n, p_mlstm_m) = _stack_layers(new_p)
    (s_nsa_rows, s_nsa_win, s_fox_kv, s_fox_logf, s_mlstm_c, s_mlstm_n, s_mlstm_m) = _stack_layers(new_s)
    return (xp.reshape(B, S, D), xs.reshape(DB, T, D),
            p_nsa_rows, p_nsa_win, p_fox_kv, p_fox_logf, p_mlstm_c, p_mlstm_n, p_mlstm_m,
            s_nsa_rows, s_nsa_win, s_fox_kv, s_fox_logf, s_mlstm_c, s_mlstm_n, s_mlstm_m)
```

```python
import functools

import jax
import jax.numpy as jnp
import numpy as np
from jax import lax
from jax.experimental import pallas as pl
from jax.experimental.pallas import tpu as pltpu

D_MODEL = 1024
DEPTH = 2
DEC_SEQ = 8
PAST_LEN = 2048
PAGE_SIZE = 128

HEAD_DIM = 64
NSA_HEADS = 4
NSA_BLOCK = 64
NSA_TOPK = 16
NSA_WINDOW = 512
NSA_ROWS = 4
MLSTM_HEADS = 4
FOX_HEADS = 4
N_BRANCH = 3
BRANCH_WIDTH = NSA_HEADS * HEAD_DIM
N_GROUPS = 4
EXPERTS_PER_GROUP = 4
N_EXPERTS = N_GROUPS * EXPERTS_PER_GROUP
D_EXPERT = 256
LN_EPS = 1e-5
NEG_INF = -1e30
TINY = 1e-30
ATTN_SCALE = HEAD_DIM ** -0.5

IN_SPLITS = (
    NSA_HEADS * HEAD_DIM,
    6 * HEAD_DIM,
    NSA_HEADS * 3,
    MLSTM_HEADS * HEAD_DIM,
    MLSTM_HEADS * HEAD_DIM,
    MLSTM_HEADS * HEAD_DIM,
    MLSTM_HEADS * HEAD_DIM,
    MLSTM_HEADS,
    MLSTM_HEADS,
    FOX_HEADS * HEAD_DIM,
    FOX_HEADS * HEAD_DIM,
    FOX_HEADS * HEAD_DIM,
    FOX_HEADS,
    N_BRANCH * D_MODEL,
)


VMEM_LIMIT_BYTES = 48 * 1024 * 1024
NSA_SLOPES = tuple(2.0 ** (-8.0 * (h + 1) / NSA_HEADS) for h in range(NSA_HEADS))
_NT = (((1,), (1,)), ((), ()))
_HI = lax.Precision.HIGHEST


def _sigmoid(x):
    return 1.0 / (1.0 + jnp.exp(-x))


def _tile_rows(x, n):
    return jnp.concatenate([x] * n, axis=0)


def _compress_kernel(x_ref, w_ref, o_ref, acc_ref):
    k = pl.program_id(1)

    @pl.when(k == 0)
    def _():
        acc_ref[...] = jnp.zeros_like(acc_ref)

    acc_ref[...] += jnp.dot(x_ref[...], w_ref[...], preferred_element_type=jnp.float32, precision=_HI)

    @pl.when(k == pl.num_programs(1) - 1)
    def _():
        o_ref[...] = acc_ref[...]


def _compress_weights(w_ck, w_cv):
    z = jnp.zeros_like(w_ck)
    wk = jnp.stack([w_ck, z, z, z], axis=1)
    wv = jnp.stack([z, w_cv, z, z], axis=1)
    return jnp.concatenate([wk, wv], axis=-1).reshape(NSA_BLOCK * NSA_ROWS * HEAD_DIM, 2 * HEAD_DIM)


def _nsa_compress_blocks(blocks, w_big, tm=256, tk=2048):
    n, kdim = blocks.shape
    tm = min(tm, n)
    return pl.pallas_call(
        _compress_kernel,
        grid=(n // tm, kdim // tk),
        in_specs=[pl.BlockSpec((tm, tk), lambda i, k: (i, k)),
                  pl.BlockSpec((tk, 2 * HEAD_DIM), lambda i, k: (k, 0))],
        out_specs=pl.BlockSpec((tm, 2 * HEAD_DIM), lambda i, k: (i, 0)),
        out_shape=jax.ShapeDtypeStruct((n, 2 * HEAD_DIM), jnp.float32),
        scratch_shapes=[pltpu.VMEM((tm, 2 * HEAD_DIM), jnp.float32)],
        name="nsa_compress",
    )(blocks, w_big)


def _select_blocks(imp, cur, n_pick):
    q, n_sb = imp.shape
    jq = lax.broadcasted_iota(jnp.int32, (q, n_sb), 1)
    valid = jq <= cur
    forced = valid & ((jq == cur) | (jq == 0))
    work0 = jnp.where(forced, -jnp.inf, jnp.where(valid, imp, NEG_INF))

    def pick(_, carry):
        work, sel = carry
        hit = jq == jnp.argmax(work, axis=-1, keepdims=True).astype(jnp.int32)
        sel = jnp.where(hit & valid, 1.0, sel)
        work = jnp.where(hit, -jnp.inf, work)
        return work, sel

    _, sel = lax.fori_loop(0, n_pick - 2, pick, (work0, forced.astype(jnp.float32)), unroll=True)
    return sel


def _nsa_prompt_kernel(qs_ref, qf_ref, kc_ref, vc_ref, ksel_ref, vsel_ref, kwin_ref, vwin_ref, g_ref,
                       o_ref, flag_ref, *, tq, tk, tw):
    i = pl.program_id(1)
    f32, bf16 = jnp.float32, jnp.bfloat16
    H = NSA_HEADS
    R = H * tq
    qs = qs_ref[0].reshape(R, 2 * HEAD_DIM)
    qf = qf_ref[0].reshape(R, HEAD_DIM)
    row = lax.broadcasted_iota(jnp.int32, (R, 1), 0)
    head = row // tq
    slope = jnp.where(head == 0, NSA_SLOPES[0], jnp.where(head == 1, NSA_SLOPES[1],
                      jnp.where(head == 2, NSA_SLOPES[2], NSA_SLOPES[3]))).astype(f32)
    posq = i * tq + (row - head * tq)

    n_cb = kc_ref.shape[1]
    sc = lax.dot_general(qf, kc_ref[0], _NT, precision=_HI, preferred_element_type=f32)
    jb = lax.broadcasted_iota(jnp.int32, (R, n_cb), 1)
    distc = posq - (jb * NSA_BLOCK + NSA_BLOCK - 1)
    okc = distc >= 0
    sc = jnp.where(okc, sc - slope * distc.astype(f32), NEG_INF)
    pc = jnp.exp(sc - jnp.max(sc, axis=-1, keepdims=True)) * okc.astype(f32)
    pc = pc / jnp.maximum(jnp.sum(pc, axis=-1, keepdims=True), TINY)
    o_cmp = jnp.dot(pc.astype(bf16), vc_ref[0].astype(bf16), preferred_element_type=f32)
    imp = pc[0:tq] + pc[tq:2 * tq] + pc[2 * tq:3 * tq] + pc[3 * tq:4 * tq]

    pq = i * tq + lax.broadcasted_iota(jnp.int32, (tq, 1), 0)
    msel = _select_blocks(imp, pq // NSA_BLOCK, NSA_TOPK)

    bpt = tk // NSA_BLOCK
    blk_any = jnp.max(msel, axis=0, keepdims=True)
    for j in range(n_cb // bpt):
        flag_ref[j] = (jnp.max(blk_any[:, j * bpt:(j + 1) * bpt]) > 0.5).astype(jnp.int32)
    mbias = (NEG_INF * (1.0 - msel)).astype(bf16)

    rowpos = i * tq + lax.broadcasted_iota(jnp.int32, (tq, 1), 0)

    def attend(carry, s, v1):
        m, acc = carry
        m_new = jnp.maximum(m, jnp.max(s, axis=-1, keepdims=True))
        p = jnp.exp(s - m_new).astype(bf16)
        return m_new, jnp.exp(m - m_new) * acc + jnp.dot(p, v1, preferred_element_type=f32)

    init = (jnp.full((R, 1), NEG_INF, f32), jnp.zeros((R, 2 * HEAD_DIM), f32))

    def sel_tile(j, carry, causal):
        k0 = pl.multiple_of(j * tk, tk)
        s = lax.dot_general(qs, ksel_ref[0, pl.ds(k0, tk), :], _NT, preferred_element_type=f32)
        eb = (lax.broadcasted_iota(jnp.int32, (n_cb, tk), 0)
              == j * bpt + lax.broadcasted_iota(jnp.int32, (n_cb, tk), 1) // NSA_BLOCK)
        bias = jnp.dot(mbias, eb.astype(bf16), preferred_element_type=f32)
        if causal:
            d = rowpos - (k0 + lax.broadcasted_iota(jnp.int32, (tq, tk), 1))
            bias = jnp.where(d >= 0, bias, NEG_INF)
        return attend(carry, s + _tile_rows(bias, H), vsel_ref[0, pl.ds(k0, tk), :])

    def sel_body(j, carry):
        return lax.cond(flag_ref[j] > 0, lambda c: sel_tile(j, c, False), lambda c: c, carry)

    n_sel = (i * tq + tq - 1) // tk + 1
    carry = lax.fori_loop(0, n_sel - 1, sel_body, init)
    _, a_sel = sel_tile(n_sel - 1, carry, True)
    o_sel = a_sel[:, :HEAD_DIM] / a_sel[:, HEAD_DIM:]

    nw = NSA_WINDOW + tq
    w0 = pl.multiple_of(jnp.maximum(i * tq - NSA_WINDOW, 0), tw)
    sw = lax.dot_general(qs, kwin_ref[0, pl.ds(w0, nw), :], _NT, preferred_element_type=f32)
    dw = rowpos - (w0 + lax.broadcasted_iota(jnp.int32, (tq, nw), 1))
    okw = (dw >= 0) & (dw < NSA_WINDOW)
    sw = jnp.where(_tile_rows(okw, H), sw, NEG_INF)
    _, a_win = attend(init, sw, vwin_ref[0, pl.ds(w0, nw), :])
    o_win = a_win[:, :HEAD_DIM] / a_win[:, HEAD_DIM:]

    g = _sigmoid(g_ref[...])
    gate = lambda c: jnp.concatenate([g[:, 3 * h + c:3 * h + c + 1] for h in range(H)], axis=0)
    o = gate(0) * o_cmp + gate(1) * o_sel + gate(2) * o_win
    o_ref[0] = o.reshape(H, tq, HEAD_DIM)


def _heads_major(x, dtype):
    B, S, _ = x.shape
    return x.reshape(B, S, -1, HEAD_DIM).transpose(0, 2, 1, 3).astype(dtype)


def _nsa_prompt(nq, nqb, rows, rowsb, winb, gates, w_ck, w_cv, tq=256, tk=512, tw=128):
    B, S, _ = nq.shape
    bf16 = jnp.bfloat16
    kcvc = _nsa_compress_blocks(rows.reshape(B * S // NSA_BLOCK, -1), _compress_weights(w_ck, w_cv))
    kcvc = kcvc.reshape(B, S // NSA_BLOCK, 2 * HEAD_DIM)
    kc, vc = kcvc[..., :HEAD_DIM], kcvc[..., HEAD_DIM:]
    pos = jnp.arange(S)
    blk, off = (pos // NSA_BLOCK).astype(jnp.float32), (pos % NSA_BLOCK).astype(jnp.float32)
    one = jnp.ones((S,), jnp.float32)
    slopes = jnp.asarray(NSA_SLOPES, jnp.float32)[:, None]
    q_terms = jnp.stack([-slopes * NSA_BLOCK * blk, -slopes * off, slopes * NSA_BLOCK * one, slopes * one], axis=-1)
    k_terms = jnp.stack([one, one, blk, off], axis=-1)
    pad_terms = lambda t: jnp.pad(t, [(0, 0)] * (t.ndim - 1) + [(0, HEAD_DIM - 4)]).astype(bf16)
    with_k_terms = lambda k: jnp.concatenate([k, jnp.broadcast_to(pad_terms(k_terms), (B, S, HEAD_DIM))], axis=-1)
    qs = jnp.concatenate([_heads_major(nqb, bf16),
                          jnp.broadcast_to(pad_terms(q_terms), (B, NSA_HEADS, S, HEAD_DIM))], axis=-1)
    qf = _heads_major(nq, jnp.float32)
    rows, win = rowsb, winb
    col = lambda a, c: a[..., c * HEAD_DIM:(c + 1) * HEAD_DIM]
    with_ones = lambda v: jnp.concatenate([v, jnp.ones_like(v)], axis=-1)
    full = lambda n, w: pl.BlockSpec((1, n, w), lambda b, i: (b, 0, 0))
    qspec = pl.BlockSpec((1, NSA_HEADS, tq, HEAD_DIM), lambda b, i: (b, 0, i, 0))
    n_cb = S // NSA_BLOCK
    o = pl.pallas_call(
        functools.partial(_nsa_prompt_kernel, tq=tq, tk=tk, tw=tw),
        grid=(B, S // tq),
        in_specs=[pl.BlockSpec((1, NSA_HEADS, tq, 2 * HEAD_DIM), lambda b, i: (b, 0, i, 0)), qspec,
                  full(n_cb, HEAD_DIM), full(n_cb, HEAD_DIM), full(S, 2 * HEAD_DIM),
                  full(S, 2 * HEAD_DIM), full(S, 2 * HEAD_DIM), full(S, 2 * HEAD_DIM),
                  pl.BlockSpec((None, tq, 128), lambda b, i: (b, i, 0))],
        out_specs=qspec,
        out_shape=jax.ShapeDtypeStruct((B, NSA_HEADS, S, HEAD_DIM), jnp.float32),
        scratch_shapes=[pltpu.SMEM((S // tk,), jnp.int32)],
        compiler_params=pltpu.CompilerParams(dimension_semantics=("parallel", "arbitrary"),
                                             vmem_limit_bytes=VMEM_LIMIT_BYTES),
        name="nsa_prompt",
    )(qs, qf, kc, vc, with_k_terms(col(rows, 2)), with_ones(col(rows, 3)),
      with_k_terms(col(win, 0)), with_ones(col(win, 1)), gates)
    return o.transpose(0, 2, 1, 3).reshape(B, S, BRANCH_WIDTH)


def _fox_prompt_kernel(q_ref, k_ref, v_ref, fq_ref, fk_ref, o_ref, *, tq, t):
    i = pl.program_id(1)
    f32, bf16 = jnp.float32, jnp.bfloat16
    lane = lax.broadcasted_iota(jnp.int32, (1, 2 * HEAD_DIM), 1)
    low = lane < HEAD_DIM
    rc = lax.broadcasted_iota(jnp.int32, (tq, t), 0) - lax.broadcasted_iota(jnp.int32, (tq, t), 1)
    n_pairs = FOX_HEADS // 2
    pair_cols = [slice(hp * 2 * HEAD_DIM, (hp + 1) * 2 * HEAD_DIM) for hp in range(n_pairs)]
    q_h, fq_h = [], []
    for hp in range(n_pairs):
        q2 = q_ref[0, :, pair_cols[hp]]
        zq = jnp.zeros_like(q2)
        q_h += [jnp.where(low, q2, zq), jnp.where(low, zq, q2)]
        fq_h += [fq_ref[0, :, 2 * hp:2 * hp + 1], fq_ref[0, :, 2 * hp + 1:2 * hp + 2]]

    def body(j, carry, diag_offset):
        k0 = pl.multiple_of(j * t, t)
        fk = fk_ref[0, j]
        out = []
        for hp in range(n_pairs):
            k2 = k_ref[0, pl.ds(k0, t), pair_cols[hp]]
            v2 = v_ref[0, pl.ds(k0, t), pair_cols[hp]]
            one = jnp.ones_like(v2)
            v1 = (jnp.where(low, v2, one), jnp.where(low, one, v2))
            for x in range(2):
                h = 2 * hp + x
                m, acc = carry[h]
                s = lax.dot_general(q_h[h], k2, _NT, preferred_element_type=f32) + (fq_h[h] - fk[h:h + 1])
                if diag_offset is not None:
                    s = jnp.where(rc >= diag_offset, s, NEG_INF)
                m_new = jnp.maximum(m, jnp.max(s, axis=-1, keepdims=True))
                p = jnp.exp(s - m_new).astype(bf16)
                acc = jnp.exp(m - m_new) * acc + jnp.dot(p, v1[x], preferred_element_type=f32)
                out.append((m_new, acc))
        return tuple(out)

    init = tuple((jnp.full((tq, 1), NEG_INF, f32), jnp.zeros((tq, 2 * HEAD_DIM), f32)) for _ in range(FOX_HEADS))
    n_full = i * (tq // t)
    carry = lax.fori_loop(0, n_full, functools.partial(body, diag_offset=None), init)
    for jj in range(tq // t):
        carry = body(n_full + jj, carry, jj * t)
    for hp in range(n_pairs):
        acc_a, acc_b = carry[2 * hp][1], carry[2 * hp + 1][1]
        den = pltpu.roll(jnp.where(low, acc_b, acc_a), HEAD_DIM, axis=1)
        o_ref[0, :, pair_cols[hp]] = jnp.where(low, acc_a, acc_b) / den


def _fox_prompt_attn(zb, F, tq=1024, t=1024):
    B, S, _ = zb.shape
    W = FOX_HEADS * HEAD_DIM
    cq, ck, cv = ((c - Z_NQ) // W for c in (Z_FQ, Z_FK, Z_FV))
    fk = jnp.pad(F.transpose(0, 2, 1), ((0, 0), (0, 8 - FOX_HEADS), (0, 0)))
    fk = fk.reshape(B, 8, S // t, t).transpose(0, 2, 1, 3)
    return pl.pallas_call(
        functools.partial(_fox_prompt_kernel, tq=tq, t=t),
        grid=(B, S // tq),
        in_specs=[pl.BlockSpec((1, tq, W), lambda b, i: (b, i, cq)),
                  pl.BlockSpec((1, S, W), lambda b, i: (b, 0, ck)),
                  pl.BlockSpec((1, S, W), lambda b, i: (b, 0, cv)),
                  pl.BlockSpec((1, tq, FOX_HEADS), lambda b, i: (b, i, 0)),
                  pl.BlockSpec((1, S // t, 8, t), lambda b, i: (b, 0, 0, 0))],
        out_specs=pl.BlockSpec((1, tq, W), lambda b, i: (b, i, 0)),
        out_shape=jax.ShapeDtypeStruct((B, S, W), jnp.float32),
        compiler_params=pltpu.CompilerParams(dimension_semantics=("parallel", "arbitrary"),
                                             vmem_limit_bytes=VMEM_LIMIT_BYTES),
        name="fox_prompt",
    )(zb, zb, zb, F, fk)


def _log_sigmoid(x):
    return jnp.minimum(x, 0.0) - jnp.log1p(jnp.exp(-jnp.abs(x)))


def _mlstm_kernel(q_ref, k_ref, kt_ref, v_ref, og_ref, gc_ref, gr_ref, nw_ref, cn0_ref, m0_ref,
                  h_ref, cn_ref, m_ref, cn_s, m_s, *, nb, L):
    c = pl.program_id(1)
    f32 = jnp.float32
    W = 2 * HEAD_DIM
    n_pairs = MLSTM_HEADS // 2

    @pl.when(c == 0)
    def _():
        cn_s[...] = cn0_ref[...]
        m_s[...] = m0_ref[...]

    lane = lax.broadcasted_iota(jnp.int32, (1, W), 1)
    low = lane < HEAD_DIM
    ti = lax.broadcasted_iota(jnp.int32, (L, L), 0)
    si = lax.broadcasted_iota(jnp.int32, (L, L), 1)
    causal = si <= ti
    tri = causal.astype(f32)
    tri_t = (ti <= si).astype(f32)
    srow = lax.broadcasted_iota(jnp.int32, (W, 2 * W), 0)
    slane = lax.broadcasted_iota(jnp.int32, (W, 2 * W), 1)
    top = srow < HEAD_DIM
    keep_a = top & ((slane < HEAD_DIM) | (slane == W))
    keep_b = (~top) & (((slane >= HEAD_DIM) & (slane < W)) | (slane == W + 1))
    lane_w = lax.broadcasted_iota(jnp.int32, (1, W), 1)
    mdt = k_ref.dtype

    for b in range(nb):
        gcol = gc_ref[b]
        grow = gr_ref[b]
        bcol = jnp.dot(tri, _log_sigmoid(gcol), precision=_HI, preferred_element_type=f32)
        brow = jnp.dot(_log_sigmoid(grow), tri_t, precision=_HI, preferred_element_type=f32)
        for hp in range(n_pairs):
            cols = slice(hp * W, (hp + 1) * W)
            q2 = q_ref[b, :, cols]
            k2 = k_ref[b, :, cols]
            v2 = v_ref[b, :, cols]
            kt2 = kt_ref[b, cols, :]
            cn = cn_s[b, hp]
            r = jnp.dot(q2, cn.astype(mdt), preferred_element_type=f32)
            zq = jnp.zeros_like(q2)
            per_head = []
            for x in range(2):
                h = 2 * hp + x
                qx = jnp.where(low, q2, zq) if x == 0 else jnp.where(low, zq, q2)
                b_c = bcol[:, MLSTM_HEADS + h:MLSTM_HEADS + h + 1]
                b_r = brow[MLSTM_HEADS + h:MLSTM_HEADS + h + 1, :]
                ig_c = gcol[:, h:h + 1]
                ig_r = grow[h:h + 1, :]
                m_prev = m_s[b, 0:1, h:h + 1]
                dmat = jnp.where(causal, b_c - b_r + ig_r, NEG_INF)
                a_c = b_c + m_prev
                m_t = jnp.maximum(a_c, jnp.max(dmat, axis=-1, keepdims=True))
                wq = jnp.exp(dmat - m_t) * lax.dot_general(qx, k2, _NT, preferred_element_type=f32)
                inter = jnp.exp(a_c - m_t)
                wv = jnp.dot(wq.astype(mdt), v2, preferred_element_type=f32)
                den = inter * r[:, W + x:W + x + 1] + jnp.sum(wq, axis=-1, keepdims=True)
                den = jnp.maximum(jnp.abs(den), jnp.exp(-m_t))
                bl = b_c[L - 1:L, :]
                g_c = bl - b_c + ig_c
                m_new = jnp.maximum(bl + m_prev, jnp.max(g_c, axis=0, keepdims=True))
                ws = jnp.exp(g_c - m_new)
                decay = jnp.exp(bl + m_prev - m_new)
                aug = jnp.concatenate([v2.astype(f32) * ws, jnp.where(lane_w == x, ws, 0.0)], axis=1)
                u = jnp.dot(kt2, aug.astype(mdt), preferred_element_type=f32)
                per_head.append((inter, wv, den, decay, u))
                m_s[b, 0:1, h:h + 1] = m_new
            (ia, wva, dena, deca, ua), (ib, wvb, denb, decb, ub) = per_head
            num = jnp.where(low, ia * r[:, :W] + wva, ib * r[:, :W] + wvb)
            hid = num / jnp.where(low, dena, denb)
            cn_s[b, hp] = (jnp.where(top, deca, decb) * cn + jnp.where(keep_a, ua, 0.0)
                           + jnp.where(keep_b, ub, 0.0))
            hid = _sigmoid(og_ref[b, :, cols]) * hid
            mean = lambda t: jnp.where(low, jnp.sum(jnp.where(low, t, 0.0), axis=-1, keepdims=True),
                                       jnp.sum(jnp.where(low, 0.0, t), axis=-1, keepdims=True)) / HEAD_DIM
            mu = mean(hid)
            var = mean(jnp.square(hid - mu))
            h_ref[b, :, cols] = (hid - mu) * lax.rsqrt(var + LN_EPS) * nw_ref[:, cols]

    @pl.when(c == pl.num_programs(1) - 1)
    def _():
        cn_ref[...] = cn_s[...]
        m_ref[...] = m_s[...]


def _mlstm_state_pack(c, n, m):
    Bx = c.shape[0]
    HD, W = HEAD_DIM, 2 * HEAD_DIM
    cp = c.reshape(Bx, MLSTM_HEADS // 2, 2, HD, HD)
    np_ = n.reshape(Bx, MLSTM_HEADS // 2, 2, HD, 1)
    z = lambda w: jnp.zeros((Bx, MLSTM_HEADS // 2, HD, w), jnp.float32)
    top = jnp.concatenate([cp[:, :, 0], z(HD), np_[:, :, 0], z(W - 1)], axis=-1)
    bot = jnp.concatenate([z(HD), cp[:, :, 1], z(1), np_[:, :, 1], z(W - 2)], axis=-1)
    cn = jnp.concatenate([top, bot], axis=-2)
    m8 = jnp.pad(m[:, None, :], ((0, 0), (0, 7), (0, 128 - MLSTM_HEADS)))
    return cn, m8


def _mlstm_state_unpack(cn, m8):
    W = 2 * HEAD_DIM
    c = jnp.stack([cn[:, h // 2, (h % 2) * HEAD_DIM:(h % 2 + 1) * HEAD_DIM,
                      (h % 2) * HEAD_DIM:(h % 2 + 1) * HEAD_DIM] for h in range(MLSTM_HEADS)], axis=1)
    n = jnp.stack([cn[:, h // 2, (h % 2) * HEAD_DIM:(h % 2 + 1) * HEAD_DIM, W + h % 2]
                   for h in range(MLSTM_HEADS)], axis=1)
    return c, n, m8[:, 0, :MLSTM_HEADS]


def _mlstm(q, k, v, og, mi, mf, norm_w, c0, n0, m0, L, nb, mxu_dtype):
    Bx, S, W4 = q.shape
    f32 = jnp.float32
    gates = jnp.concatenate([mi, mf], axis=-1).astype(f32)
    gcol = jnp.pad(gates, ((0, 0), (0, 0), (0, 128 - 2 * MLSTM_HEADS)))
    grow = gates.transpose(0, 2, 1)
    cn0, m8 = _mlstm_state_pack(c0.astype(f32), n0.astype(f32), m0.astype(f32))
    tok = lambda w: pl.BlockSpec((nb, L, w), lambda b, c: (b, c, 0))
    st_cn = pl.BlockSpec((nb, MLSTM_HEADS // 2, 128, 256), lambda b, c: (b, 0, 0, 0))
    st_m = pl.BlockSpec((nb, 8, 128), lambda b, c: (b, 0, 0))
    h, cn, m8 = pl.pallas_call(
        functools.partial(_mlstm_kernel, nb=nb, L=L),
        grid=(Bx // nb, S // L),
        in_specs=[tok(W4), tok(W4), pl.BlockSpec((nb, W4, L), lambda b, c: (b, 0, c)), tok(W4), tok(W4),
                  tok(128), pl.BlockSpec((nb, 8, L), lambda b, c: (b, 0, c)),
                  pl.BlockSpec((1, W4), lambda b, c: (0, 0)), st_cn, st_m],
        out_specs=[tok(W4), st_cn, st_m],
        out_shape=[jax.ShapeDtypeStruct((Bx, S, W4), f32),
                   jax.ShapeDtypeStruct(cn0.shape, f32), jax.ShapeDtypeStruct(m8.shape, f32)],
        scratch_shapes=[pltpu.VMEM((nb, MLSTM_HEADS // 2, 128, 256), f32), pltpu.VMEM((nb, 8, 128), f32)],
        compiler_params=pltpu.CompilerParams(dimension_semantics=("parallel", "arbitrary"),
                                             vmem_limit_bytes=VMEM_LIMIT_BYTES),
        name="mlstm",
    )(q.astype(mxu_dtype), k.astype(mxu_dtype), k.astype(mxu_dtype).transpose(0, 2, 1), v.astype(mxu_dtype),
      og.astype(f32), gcol, grow, norm_w.astype(f32).reshape(1, W4), cn0, m8)
    return h, _mlstm_state_unpack(cn, m8)


def _ln(x, w, b):
    mu = jnp.mean(x, axis=-1, keepdims=True)
    var = jnp.mean(jnp.square(x - mu), axis=-1, keepdims=True)
    return (x - mu) * lax.rsqrt(var + LN_EPS) * w + b


def _ln_kernel(x_ref, w_ref, b_ref, o_ref):
    o_ref[...] = _ln(x_ref[...], w_ref[...], b_ref[...])


def _layer_norm_rows(x, w, b, tm=1024):
    T, D = x.shape
    vec = pl.BlockSpec((1, D), lambda i: (0, 0))
    return pl.pallas_call(
        _ln_kernel, grid=(T // tm,),
        in_specs=[pl.BlockSpec((tm, D), lambda i: (i, 0)), vec, vec],
        out_specs=pl.BlockSpec((tm, D), lambda i: (i, 0)),
        out_shape=jax.ShapeDtypeStruct((T, D), jnp.float32),
        name="layer_norm",
    )(x, w.reshape(1, D), b.reshape(1, D))


Z_GM = 0
Z_NQ = 3072
Z_ROWS = Z_NQ + 256
Z_MQ = 3584
Z_MK = Z_MQ + 256
Z_MV = 4096
Z_MO = Z_MV + 256
Z_FQ = 4608
Z_WIN = Z_FQ + 256
Z_SMALL = Z_WIN + 128
Z_FK = 5120
Z_FV = Z_FK + 256
Z_WIDTH = 5632
Z_TILE = 512
ZB_FIRST_TILE = Z_NQ // Z_TILE
ZB_WIDTH = Z_WIDTH - Z_NQ
SMALL_MI, SMALL_MF, SMALL_FF = 12, 16, 20


def _z_column_order():
    starts = np.concatenate([[0], np.cumsum(IN_SPLITS)])
    seg = lambda i, lo=0, hi=None: np.arange(starts[i] + lo, starts[i] + (IN_SPLITS[i] if hi is None else hi))
    pad = lambda n: np.full((n,), -1)
    order = np.concatenate([
        seg(13), seg(0), seg(1, 0, 256), seg(3), seg(4), seg(5), seg(6),
        seg(9), seg(1, 256, 384), seg(2), seg(7), seg(8), seg(12), pad(128 - 24), seg(10), seg(11)])
    assert order.shape == (Z_WIDTH,)
    return order


def _permute_in_proj(w_in, b_in):
    order = _z_column_order()
    valid = jnp.asarray(order >= 0)
    idx = jnp.asarray(np.maximum(order, 0))
    scale = np.ones((Z_WIDTH,), np.float32)
    for c0 in (Z_NQ, Z_MK, Z_FQ):
        scale[c0:c0 + 256] = ATTN_SCALE
    scale = jnp.asarray(scale)
    w = jnp.where(valid[None, :], w_in[:, idx], 0.0) * scale[None, :]
    b = jnp.where(valid, b_in[idx], 0.0) * scale
    return w, b


def _proj_kernel(x_ref, w_ref, wlo_ref, b_ref, o_ref, ob_ref, xh_s, xl_s, *, hi_tile):
    j = pl.program_id(1)
    f32 = jnp.float32
    _proj_tile(x_ref, w_ref, wlo_ref, b_ref, o_ref, xh_s, xl_s, j, hi_tile)

    @pl.when(j >= ZB_FIRST_TILE)
    def _():
        ob_ref[...] = o_ref[...].astype(jnp.bfloat16)


def _proj_tile(x_ref, w_ref, wlo_ref, b_ref, o_ref, xh_s, xl_s, j, hi_tile):
    f32 = jnp.float32

    @pl.when(j == 0)
    def _():
        x = x_ref[...]
        xh = x.astype(jnp.bfloat16)
        xh_s[...] = xh
        xl_s[...] = (x - xh.astype(f32)).astype(jnp.bfloat16)

    @pl.when(j != hi_tile)
    def _():
        o_ref[...] = jnp.dot(xh_s[...], w_ref[...], preferred_element_type=f32) + b_ref[...]

    @pl.when(j == hi_tile)
    def _():
        acc = jnp.dot(xh_s[...], wlo_ref[...], preferred_element_type=f32)
        acc += jnp.dot(xl_s[...], w_ref[...], preferred_element_type=f32)
        acc += jnp.dot(xh_s[...], w_ref[...], preferred_element_type=f32)
        o_ref[...] = acc + b_ref[...]


def _in_projection(x, w_in, b_in, tm=1024):
    T, D = x.shape
    w, b = _permute_in_proj(w_in, b_in)
    wh = w.astype(jnp.bfloat16)
    hi_tile = Z_NQ // Z_TILE
    wlo = (w[:, Z_NQ:Z_NQ + Z_TILE] - wh[:, Z_NQ:Z_NQ + Z_TILE].astype(jnp.float32)).astype(jnp.bfloat16)
    return pl.pallas_call(
        functools.partial(_proj_kernel, hi_tile=hi_tile),
        grid=(T // tm, Z_WIDTH // Z_TILE),
        in_specs=[pl.BlockSpec((tm, D), lambda i, j: (i, 0)),
                  pl.BlockSpec((D, Z_TILE), lambda i, j: (0, j)),
                  pl.BlockSpec((D, Z_TILE), lambda i, j: (0, 0)),
                  pl.BlockSpec((1, Z_TILE), lambda i, j: (0, j))],
        out_specs=[pl.BlockSpec((tm, Z_TILE), lambda i, j: (i, j)),
                   pl.BlockSpec((tm, Z_TILE), lambda i, j: (i, jnp.maximum(j - ZB_FIRST_TILE, 0)))],
        out_shape=[jax.ShapeDtypeStruct((T, Z_WIDTH), jnp.float32),
                   jax.ShapeDtypeStruct((T, ZB_WIDTH), jnp.bfloat16)],
        scratch_shapes=[pltpu.VMEM((tm, D), jnp.bfloat16), pltpu.VMEM((tm, D), jnp.bfloat16)],
        compiler_params=pltpu.CompilerParams(dimension_semantics=("parallel", "arbitrary"),
                                             vmem_limit_bytes=VMEM_LIMIT_BYTES),
        name="in_projection",
    )(x, wh, wlo, b.reshape(1, Z_WIDTH))


def _merge_kernel(x_ref, gm_ref, on_ref, om_ref, of_ref, wb_ref, wo_ref, lw_ref, lb_ref, o_ref, *, alpha):
    f32, bf16 = jnp.float32, jnp.bfloat16
    y = None
    for m, br in enumerate((on_ref, om_ref, of_ref)):
        proj = jnp.dot(br[...].astype(bf16), wb_ref[m], preferred_element_type=f32)
        term = _sigmoid(gm_ref[:, m * D_MODEL:(m + 1) * D_MODEL]) * proj
        y = term if y is None else y + term
    mix = jnp.dot(y.astype(bf16), wo_ref[...], preferred_element_type=f32)
    o_ref[...] = _ln(alpha * x_ref[...] + mix, lw_ref[...], lb_ref[...])


def _merge_ln(x, z, o_nsa, o_ml, o_fox, w_branch, w_out, ln_w, ln_b, alpha, tm=512):
    T, D = x.shape
    tok = lambda w: pl.BlockSpec((tm, w), lambda i: (i, 0))
    vec = pl.BlockSpec((1, D), lambda i: (0, 0))
    return pl.pallas_call(
        functools.partial(_merge_kernel, alpha=alpha),
        grid=(T // tm,),
        in_specs=[tok(D), tok(N_BRANCH * D), tok(BRANCH_WIDTH), tok(BRANCH_WIDTH), tok(BRANCH_WIDTH),
                  pl.BlockSpec((N_BRANCH, BRANCH_WIDTH, D), lambda i: (0, 0, 0)),
                  pl.BlockSpec((D, D), lambda i: (0, 0)), vec, vec],
        out_specs=tok(D),
        out_shape=jax.ShapeDtypeStruct((T, D), jnp.float32),
        compiler_params=pltpu.CompilerParams(dimension_semantics=("parallel",),
                                             vmem_limit_bytes=VMEM_LIMIT_BYTES),
        name="merge_ln",
    )(x, z, o_nsa, o_ml, o_fox, w_branch.astype(jnp.bfloat16), w_out.astype(jnp.bfloat16),
      ln_w.reshape(1, D), ln_b.reshape(1, D))


def _route(logits):
    tm = logits.shape[0]
    lane = lax.broadcasted_iota(jnp.int32, (tm, 128), 1)
    lanef = lane.astype(jnp.float32)
    big = 1e9
    is_g = lane < N_GROUPS
    lg = jnp.where(is_g, logits, -jnp.inf)
    eg = jnp.exp(lg - jnp.max(lg, axis=-1, keepdims=True))
    pg = eg / jnp.sum(eg, axis=-1, keepdims=True)
    g_val = jnp.max(pg, axis=-1, keepdims=True)
    g_idx = jnp.min(jnp.where(is_g & (pg == g_val), lanef, big), axis=-1, keepdims=True)
    e_lo = N_GROUPS + EXPERTS_PER_GROUP * g_idx
    in_grp = (lanef >= e_lo) & (lanef < e_lo + EXPERTS_PER_GROUP)
    le = jnp.where(in_grp, logits, -jnp.inf)
    ee = jnp.exp(le - jnp.max(le, axis=-1, keepdims=True))
    pe = ee / jnp.sum(ee, axis=-1, keepdims=True)
    v1 = jnp.max(pe, axis=-1, keepdims=True)
    i1 = jnp.min(jnp.where(in_grp & (pe == v1), lanef, big), axis=-1, keepdims=True)
    rest = in_grp & (lanef != i1)
    pe2 = jnp.where(rest, pe, -1.0)
    v2 = jnp.max(pe2, axis=-1, keepdims=True)
    i2 = jnp.min(jnp.where(rest & (pe2 == v2), lanef, big), axis=-1, keepdims=True)
    tot = v1 + v2
    return jnp.where(lanef == i1, g_val * v1 / tot, jnp.where(lanef == i2, g_val * v2 / tot, 0.0))


def _moe_kernel(x_ref, wr_ref, br_ref, wg_ref, wu_ref, wd_ref, lw_ref, lb_ref, o_ref, xb_s, gate_s, acc_s, *, alpha):
    e = pl.program_id(1)
    f32, bf16 = jnp.float32, jnp.bfloat16

    @pl.when(e == 0)
    def _():
        x = x_ref[...]
        xb_s[...] = x.astype(bf16)
        logits = jnp.dot(x, wr_ref[...], precision=_HI, preferred_element_type=f32) + br_ref[...]
        gate_s[...] = _route(logits)
        acc_s[...] = jnp.zeros_like(acc_s)

    lane = lax.broadcasted_iota(jnp.int32, gate_s.shape, 1)
    gate = jnp.sum(jnp.where(lane == N_GROUPS + e, gate_s[...], 0.0), axis=-1, keepdims=True)
    g = jnp.dot(xb_s[...], wg_ref[0].astype(bf16), preferred_element_type=f32)
    u = jnp.dot(xb_s[...], wu_ref[0].astype(bf16), preferred_element_type=f32)
    h = (g * _sigmoid(g)) * u * gate
    acc_s[...] += jnp.dot(h.astype(bf16), wd_ref[0].astype(bf16), preferred_element_type=f32)

    @pl.when(e == pl.num_programs(1) - 1)
    def _():
        o_ref[...] = _ln(alpha * x_ref[...] + acc_s[...], lw_ref[...], lb_ref[...])


def _moe_ln(x, w_group, b_group, w_expert, b_expert, w_gate, w_up, w_down, ln_w, ln_b, alpha, tm=1024):
    T, D = x.shape
    bf16 = jnp.bfloat16
    n_r = N_GROUPS + N_EXPERTS
    wr = jnp.pad(jnp.concatenate([w_group, w_expert], axis=1), ((0, 0), (0, 128 - n_r)))
    br = jnp.pad(jnp.concatenate([b_group, b_expert]), (0, 128 - n_r)).reshape(1, 128)
    vec = pl.BlockSpec((1, D), lambda i, e: (0, 0))
    return pl.pallas_call(
        functools.partial(_moe_kernel, alpha=alpha),
        grid=(T // tm, N_EXPERTS),
        in_specs=[pl.BlockSpec((tm, D), lambda i, e: (i, 0)),
                  pl.BlockSpec((D, 128), lambda i, e: (0, 0)),
                  pl.BlockSpec((1, 128), lambda i, e: (0, 0)),
                  pl.BlockSpec((1, D, D_EXPERT), lambda i, e: (e, 0, 0)),
                  pl.BlockSpec((1, D, D_EXPERT), lambda i, e: (e, 0, 0)),
                  pl.BlockSpec((1, D_EXPERT, D), lambda i, e: (e, 0, 0)), vec, vec],
        out_specs=pl.BlockSpec((tm, D), lambda i, e: (i, 0)),
        out_shape=jax.ShapeDtypeStruct((T, D), jnp.float32),
        scratch_shapes=[pltpu.VMEM((tm, D), bf16), pltpu.VMEM((tm, 128), jnp.float32),
                        pltpu.VMEM((tm, D), jnp.float32)],
        compiler_params=pltpu.CompilerParams(dimension_semantics=("parallel", "arbitrary"),
                                             vmem_limit_bytes=VMEM_LIMIT_BYTES),
        name="moe_ln",
    )(x, wr, br, w_gate, w_up, w_down, ln_w.reshape(1, D), ln_b.reshape(1, D))


N_PAGES = PAST_LEN // PAGE_SIZE
NEW_PAD = 128
DEC_KEYS = PAST_LEN + NEW_PAD
DEC_ROWS = NSA_HEADS * DEC_SEQ


def _pages_token_minor(cache):
    nd = cache.ndim
    t = cache.transpose((0, 1) + tuple(range(3, nd)) + (2,))
    return t.reshape(cache.shape[0] * cache.shape[1], -1, cache.shape[2])


def _page_specs(rows, row_block, layer, n_phys):
    def spec(p):
        return pl.BlockSpec((1, rows, PAGE_SIZE),
                            lambda b, pt: (layer * n_phys + pt[b * N_PAGES + p], row_block, 0))
    return [spec(p) for p in range(N_PAGES)]


def _softmax_rows(s):
    m = jnp.max(s, axis=-1, keepdims=True)
    p = jnp.exp(s - m)
    return p, jnp.sum(p, axis=-1, keepdims=True)


def _fox_decode_kernel(pt_ref, qbd_ref, knew_ref, vnew_ref, fk_ref, fq_ref, *refs):
    pages, o_ref = refs[:N_PAGES], refs[N_PAGES]
    f32, bf16 = jnp.float32, jnp.bfloat16
    W = FOX_HEADS * HEAD_DIM
    qbd = qbd_ref[0]
    s = [jnp.dot(qbd, pg[0, :W, :].astype(bf16), preferred_element_type=f32) for pg in pages]
    s.append(jnp.dot(qbd, knew_ref[0], preferred_element_type=f32))
    s = jnp.concatenate(s, axis=1)
    rowh = lax.broadcasted_iota(jnp.int32, (DEC_ROWS, 1), 0) // DEC_SEQ
    fk = fk_ref[0]
    fk_rows = jnp.where(rowh == 0, fk[0:1], jnp.where(rowh == 1, fk[1:2], jnp.where(rowh == 2, fk[2:3], fk[3:4])))
    col = lax.broadcasted_iota(jnp.int32, (DEC_ROWS, DEC_KEYS), 1)
    t = lax.broadcasted_iota(jnp.int32, (DEC_ROWS, DEC_KEYS), 0) % DEC_SEQ
    ok = (col < PAST_LEN) | (col - PAST_LEN <= t)
    s = jnp.where(ok, s + (fq_ref[0] - fk_rows), NEG_INF)
    p, l = _softmax_rows(s)
    pb = p.astype(bf16)
    o = lax.dot_general(pb[:, PAST_LEN:], vnew_ref[0], _NT, preferred_element_type=f32)
    for i, pg in enumerate(pages):
        o += lax.dot_general(pb[:, i * PAGE_SIZE:(i + 1) * PAGE_SIZE], pg[0, W:, :].astype(bf16), _NT,
                             preferred_element_type=f32)
    o = o / l
    lane_h = lax.broadcasted_iota(jnp.int32, (DEC_ROWS, W), 1) // HEAD_DIM
    o = jnp.where(lane_h == rowh, o, 0.0)
    o_ref[0] = o[0:8] + o[8:16] + o[16:24] + o[24:32]


def _pad_new_t(x):
    return jnp.pad(x.transpose(0, 2, 1), ((0, 0), (0, 0), (0, NEW_PAD - DEC_SEQ))).astype(jnp.bfloat16)


def _fox_decode(q, k_new, v_new, lf_new, cache_kv, cache_lf, page_table, layer):
    DB = q.shape[0]
    f32, bf16 = jnp.float32, jnp.bfloat16
    W = FOX_HEADS * HEAD_DIM
    eye = jnp.eye(FOX_HEADS, dtype=f32)
    qh = q.reshape(DB, DEC_SEQ, FOX_HEADS, HEAD_DIM).transpose(0, 2, 1, 3)
    qbd = (qh[:, :, :, None, :] * eye[None, :, None, :, None]).reshape(DB, DEC_ROWS, W).astype(bf16)
    lf_all = jnp.concatenate([cache_lf[layer][page_table].reshape(DB, PAST_LEN, FOX_HEADS).astype(f32), lf_new], axis=1)
    F = jnp.cumsum(lf_all, axis=1)
    fk = jnp.pad(F.transpose(0, 2, 1), ((0, 0), (0, 8 - FOX_HEADS), (0, DEC_KEYS - PAST_LEN - DEC_SEQ)))
    fq = F[:, PAST_LEN:].transpose(0, 2, 1).reshape(DB, DEC_ROWS, 1)
    per_seq = lambda r, w: pl.BlockSpec((1, r, w), lambda b, pt: (b, 0, 0))
    pages = _pages_token_minor(cache_kv)
    return pl.pallas_call(
        _fox_decode_kernel,
        grid_spec=pltpu.PrefetchScalarGridSpec(
            num_scalar_prefetch=1, grid=(DB,),
            in_specs=[per_seq(DEC_ROWS, W), per_seq(W, NEW_PAD), per_seq(W, NEW_PAD), per_seq(8, DEC_KEYS),
                      per_seq(DEC_ROWS, 1)] + _page_specs(2 * W, 0, layer, cache_kv.shape[1]),
            out_specs=per_seq(DEC_SEQ, W)),
        out_shape=jax.ShapeDtypeStruct((DB, DEC_SEQ, W), f32),
        compiler_params=pltpu.CompilerParams(dimension_semantics=("parallel",), vmem_limit_bytes=VMEM_LIMIT_BYTES),
        name="fox_decode",
    )(page_table.reshape(-1), qbd, _pad_new_t(k_new), _pad_new_t(v_new), fk, fq, *([pages] * N_PAGES))


def _nsa_decode_kernel(pt_ref, qs_ref, qf_ref, kcvc_ref, rnew_ref, wbuf_ref, wnew_ref, g_ref, e_ref, *refs):
    pages, o_ref = refs[:N_PAGES], refs[N_PAGES]
    f32, bf16 = jnp.float32, jnp.bfloat16
    H, T, HD = NSA_HEADS, DEC_SEQ, HEAD_DIM
    qs = qs_ref[0]
    row = lax.broadcasted_iota(jnp.int32, (DEC_ROWS, 1), 0)
    head = row // T
    slope = jnp.where(head == 0, NSA_SLOPES[0], jnp.where(head == 1, NSA_SLOPES[1],
                      jnp.where(head == 2, NSA_SLOPES[2], NSA_SLOPES[3]))).astype(f32)
    posq = PAST_LEN + row % T

    kcvc = kcvc_ref[0]
    n_cb = kcvc.shape[0]
    sc = lax.dot_general(qf_ref[0], kcvc, _NT, precision=_HI, preferred_element_type=f32)
    jb = lax.broadcasted_iota(jnp.int32, (DEC_ROWS, n_cb), 1)
    distc = posq - (jb * NSA_BLOCK + NSA_BLOCK - 1)
    okc = distc >= 0
    sc = jnp.where(okc, sc - slope * distc.astype(f32), NEG_INF)
    pc = jnp.exp(sc - jnp.max(sc, axis=-1, keepdims=True)) * okc.astype(f32)
    pc = pc / jnp.maximum(jnp.sum(pc, axis=-1, keepdims=True), TINY)
    o_cmp = jnp.dot(pc.astype(bf16), kcvc.astype(bf16), preferred_element_type=f32)[:, HD:]
    imp = pc[0:T] + pc[T:2 * T] + pc[2 * T:3 * T] + pc[3 * T:4 * T]
    imp = jnp.concatenate([imp, jnp.zeros((T, 128 - n_cb), f32)], axis=1)
    cur = (PAST_LEN + lax.broadcasted_iota(jnp.int32, (T, 1), 0)) // NSA_BLOCK
    msel = _select_blocks(imp, cur, NSA_TOPK).astype(bf16)

    mexp = jnp.dot(msel, e_ref[...], preferred_element_type=f32)
    col = lax.broadcasted_iota(jnp.int32, (T, DEC_KEYS), 1)
    tq = lax.broadcasted_iota(jnp.int32, (T, DEC_KEYS), 0)
    d = PAST_LEN + tq - col
    ok = (mexp > 0.5) & (d >= 0)
    kv = [pg[0].astype(bf16) for pg in pages] + [rnew_ref[0]]
    s = jnp.concatenate([jnp.dot(qs, x[:HD], preferred_element_type=f32) for x in kv], axis=1)
    s = jnp.where(_tile_rows(ok, H), s - slope * _tile_rows(d.astype(f32), H), NEG_INF)
    p, l = _softmax_rows(s)
    pb = p.astype(bf16)
    acc = jnp.zeros((DEC_ROWS, HD), f32)
    for i, x in enumerate(kv):
        acc += lax.dot_general(pb[:, i * PAGE_SIZE:(i + 1) * PAGE_SIZE], x[HD:], _NT, preferred_element_type=f32)
    o_sel = acc / l

    wb = wbuf_ref.shape[2]
    kvw = [wbuf_ref[0].astype(bf16), wnew_ref[0]]
    sw = jnp.concatenate([jnp.dot(qs, x[:HD], preferred_element_type=f32) for x in kvw], axis=1)
    colw = lax.broadcasted_iota(jnp.int32, (T, wb + NEW_PAD), 1)
    tw = lax.broadcasted_iota(jnp.int32, (T, wb + NEW_PAD), 0)
    dw = wb + tw - colw
    okw = (dw >= 0) & (dw < NSA_WINDOW)
    sw = jnp.where(_tile_rows(okw, H), sw - slope * _tile_rows(dw.astype(f32), H), NEG_INF)
    pw, lw = _softmax_rows(sw)
    pwb = pw.astype(bf16)
    accw = (lax.dot_general(pwb[:, :wb], kvw[0][HD:], _NT, preferred_element_type=f32)
            + lax.dot_general(pwb[:, wb:], kvw[1][HD:], _NT, preferred_element_type=f32))
    o_win = accw / lw

    g = _sigmoid(g_ref[0])
    gate = lambda c: jnp.concatenate([g[:, 3 * h + c:3 * h + c + 1] for h in range(H)], axis=0)
    o_ref[0] = gate(0) * o_cmp + gate(1) * o_sel + gate(2) * o_win


def _compress_pages_kernel(x_ref, wkh_ref, wkl_ref, wv_ref, o_ref):
    tm = x_ref.shape[0]
    f32, bf16 = jnp.float32, jnp.bfloat16
    acc_k = jnp.zeros((tm, 2 * HEAD_DIM), f32)
    acc_v = jnp.zeros((tm, 2 * HEAD_DIM), f32)
    for d in range(HEAD_DIM):
        xk = x_ref[:, d, :]
        xh = xk.astype(bf16)
        xl = (xk - xh.astype(f32)).astype(bf16)
        acc_k += (jnp.dot(xh, wkl_ref[d], preferred_element_type=f32)
                  + jnp.dot(xl, wkh_ref[d], preferred_element_type=f32)
                  + jnp.dot(xh, wkh_ref[d], preferred_element_type=f32))
        acc_v += jnp.dot(x_ref[:, HEAD_DIM + d, :].astype(bf16), wv_ref[d], preferred_element_type=f32)
    o_ref[...] = jnp.concatenate([acc_k, acc_v], axis=1)


def _nsa_compress_pages(pages, w_ck, w_cv, layer, n_phys, tm=128):
    eye = jnp.eye(PAGE_SIZE // NSA_BLOCK, dtype=jnp.float32)
    big = lambda w: jnp.einsum('pde,bc->dbpce', w, eye).reshape(HEAD_DIM, PAGE_SIZE, 2 * HEAD_DIM)
    wspec = pl.BlockSpec((HEAD_DIM, PAGE_SIZE, 2 * HEAD_DIM), lambda i: (0, 0, 0))
    wk = big(w_ck)
    wk_hi = wk.astype(jnp.bfloat16)
    return pl.pallas_call(
        _compress_pages_kernel,
        grid=(n_phys // tm,),
        in_specs=[pl.BlockSpec((tm, 2 * HEAD_DIM, PAGE_SIZE), lambda i: (layer * (n_phys // tm) + i, 0, 0)),
                  wspec, wspec, wspec],
        out_specs=pl.BlockSpec((tm, 4 * HEAD_DIM), lambda i: (i, 0)),
        out_shape=jax.ShapeDtypeStruct((n_phys, 4 * HEAD_DIM), jnp.float32),
        compiler_params=pltpu.CompilerParams(dimension_semantics=("parallel",), vmem_limit_bytes=VMEM_LIMIT_BYTES),
        name="nsa_compress_pages",
    )(pages, wk_hi, (wk - wk_hi.astype(jnp.float32)).astype(jnp.bfloat16), big(w_cv).astype(jnp.bfloat16))


def _nsa_decode(nq, rows_new, win_new, gates, cache_rows, cache_win, page_table, w_ck, w_cv, layer):
    DB = nq.shape[0]
    f32, bf16 = jnp.float32, jnp.bfloat16
    HD = HEAD_DIM
    n_phys = cache_rows.shape[1]
    n_blk = PAGE_SIZE // NSA_BLOCK
    pages = _pages_token_minor(cache_rows)
    kcvc = _nsa_compress_pages(pages, w_ck, w_cv, layer, n_phys)[page_table]
    kcvc = kcvc.reshape(DB, N_PAGES, 2, n_blk, HD).transpose(0, 1, 3, 2, 4).reshape(DB, N_PAGES * n_blk, 2 * HD)
    stack = lambda x: x.reshape(DB, DEC_SEQ, NSA_HEADS, HD).transpose(0, 2, 1, 3).reshape(DB, DEC_ROWS, HD)
    qs = stack(nq).astype(bf16)
    qf = jnp.pad(stack(nq), ((0, 0), (0, 0), (0, HD))).astype(f32)
    win_t = cache_win.transpose(0, 1, 3, 4, 2).reshape(cache_win.shape[0] * DB, 2 * HD, cache_win.shape[2])
    colk = np.arange(DEC_KEYS)
    e = (np.arange(128)[:, None] == colk[None, :] // NSA_BLOCK) & (colk[None, :] < PAST_LEN + NSA_BLOCK)
    wb = cache_win.shape[2]
    per_seq = lambda r, w: pl.BlockSpec((1, r, w), lambda b, pt: (b, 0, 0))
    o = pl.pallas_call(
        _nsa_decode_kernel,
        grid_spec=pltpu.PrefetchScalarGridSpec(
            num_scalar_prefetch=1, grid=(DB,),
            in_specs=[per_seq(DEC_ROWS, HD), per_seq(DEC_ROWS, 2 * HD), per_seq(PAST_LEN // NSA_BLOCK, 2 * HD),
                      per_seq(2 * HD, NEW_PAD),
                      pl.BlockSpec((1, 2 * HD, wb), lambda b, pt: (layer * DB + b, 0, 0)),
                      per_seq(2 * HD, NEW_PAD), per_seq(DEC_SEQ, 128),
                      pl.BlockSpec((128, DEC_KEYS), lambda b, pt: (0, 0))]
                     + _page_specs(2 * HD, 1, layer, n_phys),
            out_specs=per_seq(DEC_ROWS, HD)),
        out_shape=jax.ShapeDtypeStruct((DB, DEC_ROWS, HD), f32),
        compiler_params=pltpu.CompilerParams(dimension_semantics=("parallel",), vmem_limit_bytes=VMEM_LIMIT_BYTES),
        name="nsa_decode",
    )(page_table.reshape(-1), qs, qf, kcvc, _pad_new_t(rows_new[..., 2 * HD:]), win_t,
      _pad_new_t(win_new), gates, jnp.asarray(e, bf16), *([pages] * N_PAGES))
    return o.reshape(DB, NSA_HEADS, DEC_SEQ, HD).transpose(0, 2, 1, 3).reshape(DB, DEC_SEQ, BRANCH_WIDTH)


def _stack_layers(states):
    return tuple(jnp.stack(list(a)) for a in zip(*states))


def kernel(x_prompt, x_sample, cache_nsa, cache_nsa_win, cache_fox_kv, cache_fox_logf,
           state_mlstm_c, state_mlstm_n, state_mlstm_m, page_table,
           ln_in_w, ln_in_b, w_in, b_in, nsa_w_ck, nsa_w_cv, mlstm_norm_w, w_branch, w_out,
           ln1_w, ln1_b, moe_w_group, moe_b_group, moe_w_expert, moe_b_expert,
           moe_w_gate, moe_w_up, moe_w_down, ln2_w, ln2_b):
    f32, bf16 = jnp.float32, jnp.bfloat16
    alpha = (2.0 * DEPTH) ** 0.25
    B, S, D = x_prompt.shape
    DB, T, _ = x_sample.shape
    TP, TS = B * S, DB * T
    HD = HEAD_DIM
    xp = _layer_norm_rows(x_prompt.reshape(TP, D), ln_in_w, ln_in_b)
    xs = _layer_norm_rows(x_sample.reshape(TS, D), ln_in_w, ln_in_b)
    cols = lambda a, c0, w: a[..., c0:c0 + w]
    colsb = lambda a, c0, w: a[..., c0 - Z_NQ:c0 - Z_NQ + w]
    zero_state = (jnp.zeros((B, MLSTM_HEADS, HD, HD), f32), jnp.zeros((B, MLSTM_HEADS, HD), f32),
                  jnp.zeros((B, MLSTM_HEADS), f32))
    new_p, new_s = [], []
    for l in range(DEPTH):
        zp2, zbp2 = _in_projection(xp, w_in[l], b_in[l])
        zs2, zbs2 = _in_projection(xs, w_in[l], b_in[l])
        zp, zbp = zp2.reshape(B, S, Z_WIDTH), zbp2.reshape(B, S, ZB_WIDTH)
        zs, zbs = zs2.reshape(DB, T, Z_WIDTH), zbs2.reshape(DB, T, ZB_WIDTH)
        small_p, small_s = cols(zp, Z_SMALL, 128), cols(zs, Z_SMALL, 128)

        o_nsa_p = _nsa_prompt(cols(zp, Z_NQ, 256), colsb(zbp, Z_NQ, 256), cols(zp, Z_ROWS, 256),
                              colsb(zbp, Z_ROWS, 256), colsb(zbp, Z_WIN, 128), small_p, nsa_w_ck[l], nsa_w_cv[l])
        o_nsa_s = _nsa_decode(cols(zs, Z_NQ, 256), colsb(zbs, Z_ROWS, 256), colsb(zbs, Z_WIN, 128), small_s,
                              cache_nsa, cache_nsa_win, page_table, nsa_w_ck[l], nsa_w_cv[l], l)

        def mlstm(q, k, v, og, small, state, L, nb, dt):
            return _mlstm(q, k, v, og, cols(small, SMALL_MI, MLSTM_HEADS), cols(small, SMALL_MF, MLSTM_HEADS),
                          mlstm_norm_w[l], *state, L, nb, dt)

        o_ml_p, st_ml_p = mlstm(colsb(zbp, Z_MQ, 256), colsb(zbp, Z_MK, 256), colsb(zbp, Z_MV, 256),
                                cols(zp, Z_MO, 256), small_p, zero_state, 128, B, bf16)
        o_ml_s, st_ml_s = mlstm(cols(zs, Z_MQ, 256), cols(zs, Z_MK, 256), cols(zs, Z_MV, 256), cols(zs, Z_MO, 256),
                                small_s, (state_mlstm_c[l], state_mlstm_n[l], state_mlstm_m[l]), T, 4, f32)

        lf_p = jax.nn.log_sigmoid(cols(small_p, SMALL_FF, FOX_HEADS))
        lf_s = jax.nn.log_sigmoid(cols(small_s, SMALL_FF, FOX_HEADS))
        o_fox_p = _fox_prompt_attn(zbp, jnp.cumsum(lf_p, axis=1))
        o_fox_s = _fox_decode(colsb(zbs, Z_FQ, 256), colsb(zbs, Z_FK, 256), colsb(zbs, Z_FV, 256), lf_s,
                              cache_fox_kv, cache_fox_logf, page_table, l)

        flat = lambda a: a.reshape(-1, a.shape[-1])
        moe_w = (moe_w_group[l], moe_b_group[l], moe_w_expert[l], moe_b_expert[l],
                 moe_w_gate[l], moe_w_up[l], moe_w_down[l], ln2_w[l], ln2_b[l], alpha)
        xp = _merge_ln(xp, zp2, flat(o_nsa_p), flat(o_ml_p), flat(o_fox_p),
                       w_branch[l], w_out[l], ln1_w[l], ln1_b[l], alpha)
        xp = _moe_ln(xp, *moe_w)
        xs = _merge_ln(xs, zs2, flat(o_nsa_s), flat(o_ml_s), flat(o_fox_s),
                       w_branch[l], w_out[l], ln1_w[l], ln1_b[l], alpha)
        xs = _moe_ln(xs, *moe_w)

        w_keep = min(NSA_WINDOW, S)
        new_p.append((cols(zp, Z_ROWS, 256).reshape(B, S, NSA_ROWS, HD),
                      cols(zp, Z_WIN, 128)[:, S - w_keep:].reshape(B, w_keep, 2, HD),
                      cols(zp, Z_FK, 512).reshape(B, S, 2, FOX_HEADS, HD), lf_p) + tuple(st_ml_p))
        win_new = cols(zs, Z_WIN, 128).reshape(DB, T, 2, HD).astype(cache_nsa_win.dtype)
        new_s.append((cols(zs, Z_ROWS, 256).reshape(DB, T, NSA_ROWS, HD),
                      win_new,
                      cols(zs, Z_FK, 512).reshape(DB, T, 2, FOX_HEADS, HD), lf_s) + tuple(st_ml_s))
    (p_nsa_rows, p_nsa_win, p_fox_kv, p_fox_logf, p_mlstm_c, p_mlstm_n, p_mlstm_m) = _stack_layers(new_p)
    (s_nsa_rows, s_win_new, s_fox_kv, s_fox_logf, s_mlstm_c, s_mlstm_n, s_mlstm_m) = _stack_layers(new_s)
    s_nsa_win = jnp.concatenate([cache_nsa_win[:, :, T:], s_win_new], axis=2)
    return (xp.reshape(B, S, D), xs.reshape(DB, T, D),
            p_nsa_rows, p_nsa_win, p_fox_kv, p_fox_logf, p_mlstm_c, p_mlstm_n, p_mlstm_m,
            s_nsa_rows, s_nsa_win, s_fox_kv, s_fox_logf, s_mlstm_c, s_mlstm_n, s_mlstm_m)
```

```python
import functools

import jax
import jax.numpy as jnp
import numpy as np
from jax import lax
from jax.experimental import pallas as pl
from jax.experimental.pallas import tpu as pltpu

D_MODEL = 1024
DEPTH = 2
DEC_SEQ = 8
PAST_LEN = 2048
PAGE_SIZE = 128

HEAD_DIM = 64
NSA_HEADS = 4
NSA_BLOCK = 64
NSA_TOPK = 16
NSA_WINDOW = 512
NSA_ROWS = 4
MLSTM_HEADS = 4
FOX_HEADS = 4
N_BRANCH = 3
BRANCH_WIDTH = NSA_HEADS * HEAD_DIM
N_GROUPS = 4
EXPERTS_PER_GROUP = 4
N_EXPERTS = N_GROUPS * EXPERTS_PER_GROUP
D_EXPERT = 256
LN_EPS = 1e-5
NEG_INF = -1e30
TINY = 1e-30
ATTN_SCALE = HEAD_DIM ** -0.5

IN_SPLITS = (
    NSA_HEADS * HEAD_DIM,
    6 * HEAD_DIM,
    NSA_HEADS * 3,
    MLSTM_HEADS * HEAD_DIM,
    MLSTM_HEADS * HEAD_DIM,
    MLSTM_HEADS * HEAD_DIM,
    MLSTM_HEADS * HEAD_DIM,
    MLSTM_HEADS,
    MLSTM_HEADS,
    FOX_HEADS * HEAD_DIM,
    FOX_HEADS * HEAD_DIM,
    FOX_HEADS * HEAD_DIM,
    FOX_HEADS,
    N_BRANCH * D_MODEL,
)


VMEM_LIMIT_BYTES = 48 * 1024 * 1024
NSA_SLOPES = tuple(2.0 ** (-8.0 * (h + 1) / NSA_HEADS) for h in range(NSA_HEADS))
_NT = (((1,), (1,)), ((), ()))
_HI = lax.Precision.HIGHEST


def _sigmoid(x):
    return 1.0 / (1.0 + jnp.exp(-x))


def _tile_rows(x, n):
    return jnp.concatenate([x] * n, axis=0)


def _compress_kernel(x_ref, w_ref, o_ref, acc_ref):
    k = pl.program_id(1)

    @pl.when(k == 0)
    def _():
        acc_ref[...] = jnp.zeros_like(acc_ref)

    acc_ref[...] += jnp.dot(x_ref[...], w_ref[...], preferred_element_type=jnp.float32, precision=_HI)

    @pl.when(k == pl.num_programs(1) - 1)
    def _():
        o_ref[...] = acc_ref[...]


def _compress_weights(w_ck, w_cv):
    z = jnp.zeros_like(w_ck)
    wk = jnp.stack([w_ck, z, z, z], axis=1)
    wv = jnp.stack([z, w_cv, z, z], axis=1)
    return jnp.concatenate([wk, wv], axis=-1).reshape(NSA_BLOCK * NSA_ROWS * HEAD_DIM, 2 * HEAD_DIM)


def _nsa_compress_blocks(blocks, w_big, tm=256, tk=2048):
    n, kdim = blocks.shape
    tm = min(tm, n)
    return pl.pallas_call(
        _compress_kernel,
        grid=(n // tm, kdim // tk),
        in_specs=[pl.BlockSpec((tm, tk), lambda i, k: (i, k)),
                  pl.BlockSpec((tk, 2 * HEAD_DIM), lambda i, k: (k, 0))],
        out_specs=pl.BlockSpec((tm, 2 * HEAD_DIM), lambda i, k: (i, 0)),
        out_shape=jax.ShapeDtypeStruct((n, 2 * HEAD_DIM), jnp.float32),
        scratch_shapes=[pltpu.VMEM((tm, 2 * HEAD_DIM), jnp.float32)],
        name="nsa_compress",
    )(blocks, w_big)


def _select_blocks(imp, cur, n_pick):
    q, n_sb = imp.shape
    jq = lax.broadcasted_iota(jnp.int32, (q, n_sb), 1)
    valid = jq <= cur
    forced = valid & ((jq == cur) | (jq == 0))
    work0 = jnp.where(forced, -jnp.inf, jnp.where(valid, imp, NEG_INF))

    def pick(_, carry):
        work, sel = carry
        hit = jq == jnp.argmax(work, axis=-1, keepdims=True).astype(jnp.int32)
        sel = jnp.where(hit & valid, 1.0, sel)
        work = jnp.where(hit, -jnp.inf, work)
        return work, sel

    _, sel = lax.fori_loop(0, n_pick - 2, pick, (work0, forced.astype(jnp.float32)), unroll=True)
    return sel


def _nsa_prompt_kernel(qs_ref, qf_ref, kc_ref, vc_ref, ksel_ref, vsel_ref, kwin_ref, vwin_ref, g_ref,
                       o_ref, flag_ref, *, tq, tk, tw):
    i = pl.program_id(1)
    f32, bf16 = jnp.float32, jnp.bfloat16
    H = NSA_HEADS
    R = H * tq
    qs = qs_ref[0].reshape(R, 2 * HEAD_DIM)
    qf = qf_ref[0].reshape(R, HEAD_DIM)
    row = lax.broadcasted_iota(jnp.int32, (R, 1), 0)
    head = row // tq
    slope = jnp.where(head == 0, NSA_SLOPES[0], jnp.where(head == 1, NSA_SLOPES[1],
                      jnp.where(head == 2, NSA_SLOPES[2], NSA_SLOPES[3]))).astype(f32)
    posq = i * tq + (row - head * tq)

    n_cb = kc_ref.shape[1]
    sc = lax.dot_general(qf, kc_ref[0], _NT, precision=_HI, preferred_element_type=f32)
    jb = lax.broadcasted_iota(jnp.int32, (R, n_cb), 1)
    distc = posq - (jb * NSA_BLOCK + NSA_BLOCK - 1)
    okc = distc >= 0
    sc = jnp.where(okc, sc - slope * distc.astype(f32), NEG_INF)
    pc = jnp.exp(sc - jnp.max(sc, axis=-1, keepdims=True)) * okc.astype(f32)
    pc = pc / jnp.maximum(jnp.sum(pc, axis=-1, keepdims=True), TINY)
    o_cmp = jnp.dot(pc.astype(bf16), vc_ref[0].astype(bf16), preferred_element_type=f32)
    imp = pc[0:tq] + pc[tq:2 * tq] + pc[2 * tq:3 * tq] + pc[3 * tq:4 * tq]

    pq = i * tq + lax.broadcasted_iota(jnp.int32, (tq, 1), 0)
    msel = _select_blocks(imp, pq // NSA_BLOCK, NSA_TOPK)

    bpt = tk // NSA_BLOCK
    blk_any = jnp.max(msel, axis=0, keepdims=True)
    for j in range(n_cb // bpt):
        flag_ref[j] = (jnp.max(blk_any[:, j * bpt:(j + 1) * bpt]) > 0.5).astype(jnp.int32)
    mbias = (NEG_INF * (1.0 - msel)).astype(bf16)

    rowpos = i * tq + lax.broadcasted_iota(jnp.int32, (tq, 1), 0)

    def attend(carry, s, v1):
        m, acc = carry
        m_new = jnp.maximum(m, jnp.max(s, axis=-1, keepdims=True))
        p = jnp.exp(s - m_new).astype(bf16)
        return m_new, jnp.exp(m - m_new) * acc + jnp.dot(p, v1, preferred_element_type=f32)

    init = (jnp.full((R, 1), NEG_INF, f32), jnp.zeros((R, 2 * HEAD_DIM), f32))

    def sel_tile(j, carry, causal):
        k0 = pl.multiple_of(j * tk, tk)
        s = lax.dot_general(qs, ksel_ref[0, pl.ds(k0, tk), :], _NT, preferred_element_type=f32)
        eb = (lax.broadcasted_iota(jnp.int32, (n_cb, tk), 0)
              == j * bpt + lax.broadcasted_iota(jnp.int32, (n_cb, tk), 1) // NSA_BLOCK)
        bias = jnp.dot(mbias, eb.astype(bf16), preferred_element_type=f32)
        if causal:
            d = rowpos - (k0 + lax.broadcasted_iota(jnp.int32, (tq, tk), 1))
            bias = jnp.where(d >= 0, bias, NEG_INF)
        return attend(carry, s + _tile_rows(bias, H), vsel_ref[0, pl.ds(k0, tk), :])

    def sel_body(j, carry):
        return lax.cond(flag_ref[j] > 0, lambda c: sel_tile(j, c, False), lambda c: c, carry)

    n_sel = (i * tq + tq - 1) // tk + 1
    carry = lax.fori_loop(0, n_sel - 1, sel_body, init)
    _, a_sel = sel_tile(n_sel - 1, carry, True)
    o_sel = a_sel[:, :HEAD_DIM] / a_sel[:, HEAD_DIM:]

    nw = NSA_WINDOW + tq
    w0 = pl.multiple_of(jnp.maximum(i * tq - NSA_WINDOW, 0), tw)
    sw = lax.dot_general(qs, kwin_ref[0, pl.ds(w0, nw), :], _NT, preferred_element_type=f32)
    dw = rowpos - (w0 + lax.broadcasted_iota(jnp.int32, (tq, nw), 1))
    okw = (dw >= 0) & (dw < NSA_WINDOW)
    sw = jnp.where(_tile_rows(okw, H), sw, NEG_INF)
    _, a_win = attend(init, sw, vwin_ref[0, pl.ds(w0, nw), :])
    o_win = a_win[:, :HEAD_DIM] / a_win[:, HEAD_DIM:]

    g = _sigmoid(g_ref[...])
    gate = lambda c: jnp.concatenate([g[:, 3 * h + c:3 * h + c + 1] for h in range(H)], axis=0)
    o = gate(0) * o_cmp + gate(1) * o_sel + gate(2) * o_win
    o_ref[0] = o.reshape(H, tq, HEAD_DIM)


def _heads_major(x, dtype):
    B, S, _ = x.shape
    return x.reshape(B, S, -1, HEAD_DIM).transpose(0, 2, 1, 3).astype(dtype)


def _nsa_prompt(nq, nqb, rows, rowsb, winb, gates, w_ck, w_cv, tq=256, tk=512, tw=128):
    B, S, _ = nq.shape
    bf16 = jnp.bfloat16
    kcvc = _nsa_compress_blocks(rows.reshape(B * S // NSA_BLOCK, -1), _compress_weights(w_ck, w_cv))
    kcvc = kcvc.reshape(B, S // NSA_BLOCK, 2 * HEAD_DIM)
    kc, vc = kcvc[..., :HEAD_DIM], kcvc[..., HEAD_DIM:]
    pos = jnp.arange(S)
    blk, off = (pos // NSA_BLOCK).astype(jnp.float32), (pos % NSA_BLOCK).astype(jnp.float32)
    one = jnp.ones((S,), jnp.float32)
    slopes = jnp.asarray(NSA_SLOPES, jnp.float32)[:, None]
    q_terms = jnp.stack([-slopes * NSA_BLOCK * blk, -slopes * off, slopes * NSA_BLOCK * one, slopes * one], axis=-1)
    k_terms = jnp.stack([one, one, blk, off], axis=-1)
    pad_terms = lambda t: jnp.pad(t, [(0, 0)] * (t.ndim - 1) + [(0, HEAD_DIM - 4)]).astype(bf16)
    with_k_terms = lambda k: jnp.concatenate([k, jnp.broadcast_to(pad_terms(k_terms), (B, S, HEAD_DIM))], axis=-1)
    qs = jnp.concatenate([_heads_major(nqb, bf16),
                          jnp.broadcast_to(pad_terms(q_terms), (B, NSA_HEADS, S, HEAD_DIM))], axis=-1)
    qf = _heads_major(nq, jnp.float32)
    rows, win = rowsb, winb
    col = lambda a, c: a[..., c * HEAD_DIM:(c + 1) * HEAD_DIM]
    with_ones = lambda v: jnp.concatenate([v, jnp.ones_like(v)], axis=-1)
    full = lambda n, w: pl.BlockSpec((1, n, w), lambda b, i: (b, 0, 0))
    qspec = pl.BlockSpec((1, NSA_HEADS, tq, HEAD_DIM), lambda b, i: (b, 0, i, 0))
    n_cb = S // NSA_BLOCK
    o = pl.pallas_call(
        functools.partial(_nsa_prompt_kernel, tq=tq, tk=tk, tw=tw),
        grid=(B, S // tq),
        in_specs=[pl.BlockSpec((1, NSA_HEADS, tq, 2 * HEAD_DIM), lambda b, i: (b, 0, i, 0)), qspec,
                  full(n_cb, HEAD_DIM), full(n_cb, HEAD_DIM), full(S, 2 * HEAD_DIM),
                  full(S, 2 * HEAD_DIM), full(S, 2 * HEAD_DIM), full(S, 2 * HEAD_DIM),
                  pl.BlockSpec((None, tq, 128), lambda b, i: (b, i, 0))],
        out_specs=qspec,
        out_shape=jax.ShapeDtypeStruct((B, NSA_HEADS, S, HEAD_DIM), jnp.float32),
        scratch_shapes=[pltpu.SMEM((S // tk,), jnp.int32)],
        compiler_params=pltpu.CompilerParams(dimension_semantics=("parallel", "arbitrary"),
                                             vmem_limit_bytes=VMEM_LIMIT_BYTES),
        name="nsa_prompt",
    )(qs, qf, kc, vc, with_k_terms(col(rows, 2)), with_ones(col(rows, 3)),
      with_k_terms(col(win, 0)), with_ones(col(win, 1)), gates)
    return o.transpose(0, 2, 1, 3).reshape(B, S, BRANCH_WIDTH)


def _fox_prompt_kernel(q_ref, k_ref, v_ref, fq_ref, fk_ref, o_ref, *, tq, t):
    i = pl.program_id(1)
    f32, bf16 = jnp.float32, jnp.bfloat16
    lane = lax.broadcasted_iota(jnp.int32, (1, 2 * HEAD_DIM), 1)
    low = lane < HEAD_DIM
    rc = lax.broadcasted_iota(jnp.int32, (tq, t), 0) - lax.broadcasted_iota(jnp.int32, (tq, t), 1)
    n_pairs = FOX_HEADS // 2
    pair_cols = [slice(hp * 2 * HEAD_DIM, (hp + 1) * 2 * HEAD_DIM) for hp in range(n_pairs)]
    q_h, fq_h = [], []
    for hp in range(n_pairs):
        q2 = q_ref[0, :, pair_cols[hp]]
        zq = jnp.zeros_like(q2)
        q_h += [jnp.where(low, q2, zq), jnp.where(low, zq, q2)]
        fq_h += [fq_ref[0, :, 2 * hp:2 * hp + 1], fq_ref[0, :, 2 * hp + 1:2 * hp + 2]]

    def body(j, carry, diag_offset):
        k0 = pl.multiple_of(j * t, t)
        fk = fk_ref[0, j]
        out = []
        for hp in range(n_pairs):
            k2 = k_ref[0, pl.ds(k0, t), pair_cols[hp]]
            v2 = v_ref[0, pl.ds(k0, t), pair_cols[hp]]
            one = jnp.ones_like(v2)
            v1 = (jnp.where(low, v2, one), jnp.where(low, one, v2))
            for x in range(2):
                h = 2 * hp + x
                m, acc = carry[h]
                s = lax.dot_general(q_h[h], k2, _NT, preferred_element_type=f32) + (fq_h[h] - fk[h:h + 1])
                if diag_offset is not None:
                    s = jnp.where(rc >= diag_offset, s, NEG_INF)
                m_new = jnp.maximum(m, jnp.max(s, axis=-1, keepdims=True))
                p = jnp.exp(s - m_new).astype(bf16)
                acc = jnp.exp(m - m_new) * acc + jnp.dot(p, v1[x], preferred_element_type=f32)
                out.append((m_new, acc))
        return tuple(out)

    init = tuple((jnp.full((tq, 1), NEG_INF, f32), jnp.zeros((tq, 2 * HEAD_DIM), f32)) for _ in range(FOX_HEADS))
    n_full = i * (tq // t)
    carry = lax.fori_loop(0, n_full, functools.partial(body, diag_offset=None), init)
    for jj in range(tq // t):
        carry = body(n_full + jj, carry, jj * t)
    for hp in range(n_pairs):
        acc_a, acc_b = carry[2 * hp][1], carry[2 * hp + 1][1]
        den = pltpu.roll(jnp.where(low, acc_b, acc_a), HEAD_DIM, axis=1)
        o_ref[0, :, pair_cols[hp]] = jnp.where(low, acc_a, acc_b) / den


def _fox_prompt_attn(zb, F, tq=1024, t=1024):
    B, S, _ = zb.shape
    W = FOX_HEADS * HEAD_DIM
    cq, ck, cv = ((c - Z_NQ) // W for c in (Z_FQ, Z_FK, Z_FV))
    fk = jnp.pad(F.transpose(0, 2, 1), ((0, 0), (0, 8 - FOX_HEADS), (0, 0)))
    fk = fk.reshape(B, 8, S // t, t).transpose(0, 2, 1, 3)
    return pl.pallas_call(
        functools.partial(_fox_prompt_kernel, tq=tq, t=t),
        grid=(B, S // tq),
        in_specs=[pl.BlockSpec((1, tq, W), lambda b, i: (b, i, cq)),
                  pl.BlockSpec((1, S, W), lambda b, i: (b, 0, ck)),
                  pl.BlockSpec((1, S, W), lambda b, i: (b, 0, cv)),
                  pl.BlockSpec((1, tq, FOX_HEADS), lambda b, i: (b, i, 0)),
                  pl.BlockSpec((1, S // t, 8, t), lambda b, i: (b, 0, 0, 0))],
        out_specs=pl.BlockSpec((1, tq, W), lambda b, i: (b, i, 0)),
        out_shape=jax.ShapeDtypeStruct((B, S, W), jnp.float32),
        compiler_params=pltpu.CompilerParams(dimension_semantics=("parallel", "arbitrary"),
                                             vmem_limit_bytes=VMEM_LIMIT_BYTES),
        name="fox_prompt",
    )(zb, zb, zb, F, fk)


def _log_sigmoid(x):
    return jnp.minimum(x, 0.0) - jnp.log1p(jnp.exp(-jnp.abs(x)))


def _mlstm_kernel(q_ref, k_ref, kt_ref, v_ref, og_ref, gc_ref, gr_ref, nw_ref, cn0_ref, m0_ref,
                  h_ref, cn_ref, m_ref, cn_s, m_s, *, nb, L):
    c = pl.program_id(1)
    f32 = jnp.float32
    W = 2 * HEAD_DIM
    n_pairs = MLSTM_HEADS // 2

    @pl.when(c == 0)
    def _():
        cn_s[...] = cn0_ref[...]
        m_s[...] = m0_ref[...]

    lane = lax.broadcasted_iota(jnp.int32, (1, W), 1)
    low = lane < HEAD_DIM
    ti = lax.broadcasted_iota(jnp.int32, (L, L), 0)
    si = lax.broadcasted_iota(jnp.int32, (L, L), 1)
    causal = si <= ti
    tri = causal.astype(f32)
    tri_t = (ti <= si).astype(f32)
    srow = lax.broadcasted_iota(jnp.int32, (W, 2 * W), 0)
    slane = lax.broadcasted_iota(jnp.int32, (W, 2 * W), 1)
    top = srow < HEAD_DIM
    keep_a = top & ((slane < HEAD_DIM) | (slane == W))
    keep_b = (~top) & (((slane >= HEAD_DIM) & (slane < W)) | (slane == W + 1))
    lane_w = lax.broadcasted_iota(jnp.int32, (1, W), 1)
    mdt = k_ref.dtype

    for b in range(nb):
        gcol = gc_ref[b]
        grow = gr_ref[b]
        bcol = jnp.dot(tri, _log_sigmoid(gcol), precision=_HI, preferred_element_type=f32)
        brow = jnp.dot(_log_sigmoid(grow), tri_t, precision=_HI, preferred_element_type=f32)
        for hp in range(n_pairs):
            cols = slice(hp * W, (hp + 1) * W)
            q2 = q_ref[b, :, cols]
            k2 = k_ref[b, :, cols]
            v2 = v_ref[b, :, cols]
            kt2 = kt_ref[b, cols, :]
            cn = cn_s[b, hp]
            r = jnp.dot(q2, cn.astype(mdt), preferred_element_type=f32)
            zq = jnp.zeros_like(q2)
            per_head = []
            for x in range(2):
                h = 2 * hp + x
                qx = jnp.where(low, q2, zq) if x == 0 else jnp.where(low, zq, q2)
                b_c = bcol[:, MLSTM_HEADS + h:MLSTM_HEADS + h + 1]
                b_r = brow[MLSTM_HEADS + h:MLSTM_HEADS + h + 1, :]
                ig_c = gcol[:, h:h + 1]
                ig_r = grow[h:h + 1, :]
                m_prev = m_s[b, 0:1, h:h + 1]
                dmat = jnp.where(causal, b_c - b_r + ig_r, NEG_INF)
                a_c = b_c + m_prev
                m_t = jnp.maximum(a_c, jnp.max(dmat, axis=-1, keepdims=True))
                wq = jnp.exp(dmat - m_t) * lax.dot_general(qx, k2, _NT, preferred_element_type=f32)
                inter = jnp.exp(a_c - m_t)
                wv = jnp.dot(wq.astype(mdt), v2, preferred_element_type=f32)
                den = inter * r[:, W + x:W + x + 1] + jnp.sum(wq, axis=-1, keepdims=True)
                den = jnp.maximum(jnp.abs(den), jnp.exp(-m_t))
                bl = b_c[L - 1:L, :]
                g_c = bl - b_c + ig_c
                m_new = jnp.maximum(bl + m_prev, jnp.max(g_c, axis=0, keepdims=True))
                ws = jnp.exp(g_c - m_new)
                decay = jnp.exp(bl + m_prev - m_new)
                aug = jnp.concatenate([v2.astype(f32) * ws, jnp.where(lane_w == x, ws, 0.0)], axis=1)
                u = jnp.dot(kt2, aug.astype(mdt), preferred_element_type=f32)
                per_head.append((inter, wv, den, decay, u))
                m_s[b, 0:1, h:h + 1] = m_new
            (ia, wva, dena, deca, ua), (ib, wvb, denb, decb, ub) = per_head
            num = jnp.where(low, ia * r[:, :W] + wva, ib * r[:, :W] + wvb)
            hid = num / jnp.where(low, dena, denb)
            cn_s[b, hp] = (jnp.where(top, deca, decb) * cn + jnp.where(keep_a, ua, 0.0)
                           + jnp.where(keep_b, ub, 0.0))
            hid = _sigmoid(og_ref[b, :, cols]) * hid
            mean = lambda t: jnp.where(low, jnp.sum(jnp.where(low, t, 0.0), axis=-1, keepdims=True),
                                       jnp.sum(jnp.where(low, 0.0, t), axis=-1, keepdims=True)) / HEAD_DIM
            mu = mean(hid)
            var = mean(jnp.square(hid - mu))
            h_ref[b, :, cols] = (hid - mu) * lax.rsqrt(var + LN_EPS) * nw_ref[:, cols]

    @pl.when(c == pl.num_programs(1) - 1)
    def _():
        cn_ref[...] = cn_s[...]
        m_ref[...] = m_s[...]


def _mlstm_state_pack(c, n, m):
    Bx = c.shape[0]
    HD, W = HEAD_DIM, 2 * HEAD_DIM
    cp = c.reshape(Bx, MLSTM_HEADS // 2, 2, HD, HD)
    np_ = n.reshape(Bx, MLSTM_HEADS // 2, 2, HD, 1)
    z = lambda w: jnp.zeros((Bx, MLSTM_HEADS // 2, HD, w), jnp.float32)
    top = jnp.concatenate([cp[:, :, 0], z(HD), np_[:, :, 0], z(W - 1)], axis=-1)
    bot = jnp.concatenate([z(HD), cp[:, :, 1], z(1), np_[:, :, 1], z(W - 2)], axis=-1)
    cn = jnp.concatenate([top, bot], axis=-2)
    m8 = jnp.pad(m[:, None, :], ((0, 0), (0, 7), (0, 128 - MLSTM_HEADS)))
    return cn, m8


def _mlstm_state_unpack(cn, m8):
    W = 2 * HEAD_DIM
    c = jnp.stack([cn[:, h // 2, (h % 2) * HEAD_DIM:(h % 2 + 1) * HEAD_DIM,
                      (h % 2) * HEAD_DIM:(h % 2 + 1) * HEAD_DIM] for h in range(MLSTM_HEADS)], axis=1)
    n = jnp.stack([cn[:, h // 2, (h % 2) * HEAD_DIM:(h % 2 + 1) * HEAD_DIM, W + h % 2]
                   for h in range(MLSTM_HEADS)], axis=1)
    return c, n, m8[:, 0, :MLSTM_HEADS]


def _mlstm(q, k, v, og, mi, mf, norm_w, c0, n0, m0, L, nb, mxu_dtype):
    Bx, S, W4 = q.shape
    f32 = jnp.float32
    gates = jnp.concatenate([mi, mf], axis=-1).astype(f32)
    gcol = jnp.pad(gates, ((0, 0), (0, 0), (0, 128 - 2 * MLSTM_HEADS)))
    grow = gates.transpose(0, 2, 1)
    cn0, m8 = _mlstm_state_pack(c0.astype(f32), n0.astype(f32), m0.astype(f32))
    tok = lambda w: pl.BlockSpec((nb, L, w), lambda b, c: (b, c, 0))
    st_cn = pl.BlockSpec((nb, MLSTM_HEADS // 2, 128, 256), lambda b, c: (b, 0, 0, 0))
    st_m = pl.BlockSpec((nb, 8, 128), lambda b, c: (b, 0, 0))
    h, cn, m8 = pl.pallas_call(
        functools.partial(_mlstm_kernel, nb=nb, L=L),
        grid=(Bx // nb, S // L),
        in_specs=[tok(W4), tok(W4), pl.BlockSpec((nb, W4, L), lambda b, c: (b, 0, c)), tok(W4), tok(W4),
                  tok(128), pl.BlockSpec((nb, 8, L), lambda b, c: (b, 0, c)),
                  pl.BlockSpec((1, W4), lambda b, c: (0, 0)), st_cn, st_m],
        out_specs=[tok(W4), st_cn, st_m],
        out_shape=[jax.ShapeDtypeStruct((Bx, S, W4), f32),
                   jax.ShapeDtypeStruct(cn0.shape, f32), jax.ShapeDtypeStruct(m8.shape, f32)],
        scratch_shapes=[pltpu.VMEM((nb, MLSTM_HEADS // 2, 128, 256), f32), pltpu.VMEM((nb, 8, 128), f32)],
        compiler_params=pltpu.CompilerParams(dimension_semantics=("parallel", "arbitrary"),
                                             vmem_limit_bytes=VMEM_LIMIT_BYTES),
        name="mlstm",
    )(q.astype(mxu_dtype), k.astype(mxu_dtype), k.astype(mxu_dtype).transpose(0, 2, 1), v.astype(mxu_dtype),
      og.astype(f32), gcol, grow, norm_w.astype(f32).reshape(1, W4), cn0, m8)
    return h, _mlstm_state_unpack(cn, m8)


def _ln(x, w, b):
    mu = jnp.mean(x, axis=-1, keepdims=True)
    var = jnp.mean(jnp.square(x - mu), axis=-1, keepdims=True)
    return (x - mu) * lax.rsqrt(var + LN_EPS) * w + b


def _ln_kernel(x_ref, w_ref, b_ref, o_ref):
    o_ref[...] = _ln(x_ref[...], w_ref[...], b_ref[...])


def _layer_norm_rows(x, w, b, tm=1024):
    T, D = x.shape
    vec = pl.BlockSpec((1, D), lambda i: (0, 0))
    return pl.pallas_call(
        _ln_kernel, grid=(T // tm,),
        in_specs=[pl.BlockSpec((tm, D), lambda i: (i, 0)), vec, vec],
        out_specs=pl.BlockSpec((tm, D), lambda i: (i, 0)),
        out_shape=jax.ShapeDtypeStruct((T, D), jnp.float32),
        name="layer_norm",
    )(x, w.reshape(1, D), b.reshape(1, D))


Z_GM = 0
Z_NQ = 3072
Z_ROWS = Z_NQ + 256
Z_MQ = 3584
Z_MK = Z_MQ + 256
Z_MV = 4096
Z_MO = Z_MV + 256
Z_FQ = 4608
Z_WIN = Z_FQ + 256
Z_SMALL = Z_WIN + 128
Z_FK = 5120
Z_FV = Z_FK + 256
Z_WIDTH = 5632
Z_TILE = 512
ZB_FIRST_TILE = Z_NQ // Z_TILE
ZB_WIDTH = Z_WIDTH - Z_NQ
SMALL_MI, SMALL_MF, SMALL_FF = 12, 16, 20


def _z_column_order():
    starts = np.concatenate([[0], np.cumsum(IN_SPLITS)])
    seg = lambda i, lo=0, hi=None: np.arange(starts[i] + lo, starts[i] + (IN_SPLITS[i] if hi is None else hi))
    pad = lambda n: np.full((n,), -1)
    order = np.concatenate([
        seg(13), seg(0), seg(1, 0, 256), seg(3), seg(4), seg(5), seg(6),
        seg(9), seg(1, 256, 384), seg(2), seg(7), seg(8), seg(12), pad(128 - 24), seg(10), seg(11)])
    assert order.shape == (Z_WIDTH,)
    return order


def _permute_in_proj(w_in, b_in):
    order = _z_column_order()
    valid = jnp.asarray(order >= 0)
    idx = jnp.asarray(np.maximum(order, 0))
    scale = np.ones((Z_WIDTH,), np.float32)
    for c0 in (Z_NQ, Z_MK, Z_FQ):
        scale[c0:c0 + 256] = ATTN_SCALE
    scale = jnp.asarray(scale)
    w = jnp.where(valid[None, :], w_in[:, idx], 0.0) * scale[None, :]
    b = jnp.where(valid, b_in[idx], 0.0) * scale
    return w, b


def _proj_kernel(x_ref, w_ref, wlo_ref, b_ref, o_ref, ob_ref, xh_s, xl_s, *, hi_tile):
    j = pl.program_id(1)
    f32 = jnp.float32
    _proj_tile(x_ref, w_ref, wlo_ref, b_ref, o_ref, xh_s, xl_s, j, hi_tile)

    @pl.when(j >= ZB_FIRST_TILE)
    def _():
        ob_ref[...] = o_ref[...].astype(jnp.bfloat16)


def _proj_tile(x_ref, w_ref, wlo_ref, b_ref, o_ref, xh_s, xl_s, j, hi_tile):
    f32 = jnp.float32

    @pl.when(j == 0)
    def _():
        x = x_ref[...]
        xh = x.astype(jnp.bfloat16)
        xh_s[...] = xh
        xl_s[...] = (x - xh.astype(f32)).astype(jnp.bfloat16)

    @pl.when(j != hi_tile)
    def _():
        o_ref[...] = jnp.dot(xh_s[...], w_ref[...], preferred_element_type=f32) + b_ref[...]

    @pl.when(j == hi_tile)
    def _():
        acc = jnp.dot(xh_s[...], wlo_ref[...], preferred_element_type=f32)
        acc += jnp.dot(xl_s[...], w_ref[...], preferred_element_type=f32)
        acc += jnp.dot(xh_s[...], w_ref[...], preferred_element_type=f32)
        o_ref[...] = acc + b_ref[...]


def _in_projection(x, w_in, b_in, tm=1024):
    T, D = x.shape
    w, b = _permute_in_proj(w_in, b_in)
    wh = w.astype(jnp.bfloat16)
    hi_tile = Z_NQ // Z_TILE
    wlo = (w[:, Z_NQ:Z_NQ + Z_TILE] - wh[:, Z_NQ:Z_NQ + Z_TILE].astype(jnp.float32)).astype(jnp.bfloat16)
    return pl.pallas_call(
        functools.partial(_proj_kernel, hi_tile=hi_tile),
        grid=(T // tm, Z_WIDTH // Z_TILE),
        in_specs=[pl.BlockSpec((tm, D), lambda i, j: (i, 0)),
                  pl.BlockSpec((D, Z_TILE), lambda i, j: (0, j)),
                  pl.BlockSpec((D, Z_TILE), lambda i, j: (0, 0)),
                  pl.BlockSpec((1, Z_TILE), lambda i, j: (0, j))],
        out_specs=[pl.BlockSpec((tm, Z_TILE), lambda i, j: (i, j)),
                   pl.BlockSpec((tm, Z_TILE), lambda i, j: (i, jnp.maximum(j - ZB_FIRST_TILE, 0)))],
        out_shape=[jax.ShapeDtypeStruct((T, Z_WIDTH), jnp.float32),
                   jax.ShapeDtypeStruct((T, ZB_WIDTH), jnp.bfloat16)],
        scratch_shapes=[pltpu.VMEM((tm, D), jnp.bfloat16), pltpu.VMEM((tm, D), jnp.bfloat16)],
        compiler_params=pltpu.CompilerParams(dimension_semantics=("parallel", "arbitrary"),
                                             vmem_limit_bytes=VMEM_LIMIT_BYTES),
        name="in_projection",
    )(x, wh, wlo, b.reshape(1, Z_WIDTH))


def _merge_kernel(x_ref, gm_ref, on_ref, om_ref, of_ref, wb_ref, wo_ref, lw_ref, lb_ref, o_ref, *, alpha):
    f32, bf16 = jnp.float32, jnp.bfloat16
    y = None
    for m, br in enumerate((on_ref, om_ref, of_ref)):
        proj = jnp.dot(br[...].astype(bf16), wb_ref[m], preferred_element_type=f32)
        term = _sigmoid(gm_ref[:, m * D_MODEL:(m + 1) * D_MODEL]) * proj
        y = term if y is None else y + term
    mix = jnp.dot(y.astype(bf16), wo_ref[...], preferred_element_type=f32)
    o_ref[...] = _ln(alpha * x_ref[...] + mix, lw_ref[...], lb_ref[...])


def _merge_ln(x, z, o_nsa, o_ml, o_fox, w_branch, w_out, ln_w, ln_b, alpha, tm=512):
    T, D = x.shape
    tok = lambda w: pl.BlockSpec((tm, w), lambda i: (i, 0))
    vec = pl.BlockSpec((1, D), lambda i: (0, 0))
    return pl.pallas_call(
        functools.partial(_merge_kernel, alpha=alpha),
        grid=(T // tm,),
        in_specs=[tok(D), tok(N_BRANCH * D), tok(BRANCH_WIDTH), tok(BRANCH_WIDTH), tok(BRANCH_WIDTH),
                  pl.BlockSpec((N_BRANCH, BRANCH_WIDTH, D), lambda i: (0, 0, 0)),
                  pl.BlockSpec((D, D), lambda i: (0, 0)), vec, vec],
        out_specs=tok(D),
        out_shape=jax.ShapeDtypeStruct((T, D), jnp.float32),
        compiler_params=pltpu.CompilerParams(dimension_semantics=("parallel",),
                                             vmem_limit_bytes=VMEM_LIMIT_BYTES),
        name="merge_ln",
    )(x, z, o_nsa, o_ml, o_fox, w_branch.astype(jnp.bfloat16), w_out.astype(jnp.bfloat16),
      ln_w.reshape(1, D), ln_b.reshape(1, D))


def _route(logits):
    tm = logits.shape[0]
    lane = lax.broadcasted_iota(jnp.int32, (tm, 128), 1)
    lanef = lane.astype(jnp.float32)
    big = 1e9
    is_g = lane < N_GROUPS
    lg = jnp.where(is_g, logits, -jnp.inf)
    eg = jnp.exp(lg - jnp.max(lg, axis=-1, keepdims=True))
    pg = eg / jnp.sum(eg, axis=-1, keepdims=True)
    g_val = jnp.max(pg, axis=-1, keepdims=True)
    g_idx = jnp.min(jnp.where(is_g & (pg == g_val), lanef, big), axis=-1, keepdims=True)
    e_lo = N_GROUPS + EXPERTS_PER_GROUP * g_idx
    in_grp = (lanef >= e_lo) & (lanef < e_lo + EXPERTS_PER_GROUP)
    le = jnp.where(in_grp, logits, -jnp.inf)
    ee = jnp.exp(le - jnp.max(le, axis=-1, keepdims=True))
    pe = ee / jnp.sum(ee, axis=-1, keepdims=True)
    v1 = jnp.max(pe, axis=-1, keepdims=True)
    i1 = jnp.min(jnp.where(in_grp & (pe == v1), lanef, big), axis=-1, keepdims=True)
    rest = in_grp & (lanef != i1)
    pe2 = jnp.where(rest, pe, -1.0)
    v2 = jnp.max(pe2, axis=-1, keepdims=True)
    i2 = jnp.min(jnp.where(rest & (pe2 == v2), lanef, big), axis=-1, keepdims=True)
    tot = v1 + v2
    return jnp.where(lanef == i1, g_val * v1 / tot, jnp.where(lanef == i2, g_val * v2 / tot, 0.0))


def _moe_kernel(x_ref, wr_ref, br_ref, wg_ref, wu_ref, wd_ref, lw_ref, lb_ref, o_ref, xb_s, gate_s, acc_s, *, alpha):
    e = pl.program_id(1)
    f32, bf16 = jnp.float32, jnp.bfloat16

    @pl.when(e == 0)
    def _():
        x = x_ref[...]
        xb_s[...] = x.astype(bf16)
        logits = jnp.dot(x, wr_ref[...], precision=_HI, preferred_element_type=f32) + br_ref[...]
        gate_s[...] = _route(logits)
        acc_s[...] = jnp.zeros_like(acc_s)

    lane = lax.broadcasted_iota(jnp.int32, gate_s.shape, 1)
    y = None
    for k in range(wg_ref.shape[0]):
        ex = e * wg_ref.shape[0] + k
        gate = jnp.sum(jnp.where(lane == N_GROUPS + ex, gate_s[...], 0.0), axis=-1, keepdims=True)
        g = jnp.dot(xb_s[...], wg_ref[k].astype(bf16), preferred_element_type=f32)
        u = jnp.dot(xb_s[...], wu_ref[k].astype(bf16), preferred_element_type=f32)
        h = (g * _sigmoid(g)) * u * gate
        yk = jnp.dot(h.astype(bf16), wd_ref[k].astype(bf16), preferred_element_type=f32)
        y = yk if y is None else y + yk
    acc_s[...] += y

    @pl.when(e == pl.num_programs(1) - 1)
    def _():
        o_ref[...] = _ln(alpha * x_ref[...] + acc_s[...], lw_ref[...], lb_ref[...])


def _moe_ln(x, w_group, b_group, w_expert, b_expert, w_gate, w_up, w_down, ln_w, ln_b, alpha, tm=1024, eps=2):
    T, D = x.shape
    bf16 = jnp.bfloat16
    n_r = N_GROUPS + N_EXPERTS
    wr = jnp.pad(jnp.concatenate([w_group, w_expert], axis=1), ((0, 0), (0, 128 - n_r)))
    br = jnp.pad(jnp.concatenate([b_group, b_expert]), (0, 128 - n_r)).reshape(1, 128)
    vec = pl.BlockSpec((1, D), lambda i, e: (0, 0))
    return pl.pallas_call(
        functools.partial(_moe_kernel, alpha=alpha),
        grid=(T // tm, N_EXPERTS // eps),
        in_specs=[pl.BlockSpec((tm, D), lambda i, e: (i, 0)),
                  pl.BlockSpec((D, 128), lambda i, e: (0, 0)),
                  pl.BlockSpec((1, 128), lambda i, e: (0, 0)),
                  pl.BlockSpec((eps, D, D_EXPERT), lambda i, e: (e, 0, 0)),
                  pl.BlockSpec((eps, D, D_EXPERT), lambda i, e: (e, 0, 0)),
                  pl.BlockSpec((eps, D_EXPERT, D), lambda i, e: (e, 0, 0)), vec, vec],
        out_specs=pl.BlockSpec((tm, D), lambda i, e: (i, 0)),
        out_shape=jax.ShapeDtypeStruct((T, D), jnp.float32),
        scratch_shapes=[pltpu.VMEM((tm, D), bf16), pltpu.VMEM((tm, 128), jnp.float32),
                        pltpu.VMEM((tm, D), jnp.float32)],
        compiler_params=pltpu.CompilerParams(dimension_semantics=("parallel", "arbitrary"),
                                             vmem_limit_bytes=VMEM_LIMIT_BYTES),
        name="moe_ln",
    )(x, wr, br, w_gate, w_up, w_down, ln_w.reshape(1, D), ln_b.reshape(1, D))


N_PAGES = PAST_LEN // PAGE_SIZE
NEW_PAD = 128
DEC_KEYS = PAST_LEN + NEW_PAD
DEC_ROWS = NSA_HEADS * DEC_SEQ


def _pages_token_minor(cache):
    nd = cache.ndim
    t = cache.transpose((0, 1) + tuple(range(3, nd)) + (2,))
    return t.reshape(cache.shape[0] * cache.shape[1], -1, cache.shape[2])


def _page_specs(rows, row_block, layer, n_phys):
    def spec(p):
        return pl.BlockSpec((1, rows, PAGE_SIZE),
                            lambda b, pt: (layer * n_phys + pt[b * N_PAGES + p], row_block, 0))
    return [spec(p) for p in range(N_PAGES)]


def _softmax_rows(s):
    m = jnp.max(s, axis=-1, keepdims=True)
    p = jnp.exp(s - m)
    return p, jnp.sum(p, axis=-1, keepdims=True)


def _fox_decode_kernel(pt_ref, qbd_ref, knew_ref, vnew_ref, fk_ref, fq_ref, *refs):
    pages, o_ref = refs[:N_PAGES], refs[N_PAGES]
    f32, bf16 = jnp.float32, jnp.bfloat16
    W = FOX_HEADS * HEAD_DIM
    qbd = qbd_ref[0]
    s = [jnp.dot(qbd, pg[0, :W, :].astype(bf16), preferred_element_type=f32) for pg in pages]
    s.append(jnp.dot(qbd, knew_ref[0], preferred_element_type=f32))
    s = jnp.concatenate(s, axis=1)
    rowh = lax.broadcasted_iota(jnp.int32, (DEC_ROWS, 1), 0) // DEC_SEQ
    fk = fk_ref[0]
    fk_rows = jnp.where(rowh == 0, fk[0:1], jnp.where(rowh == 1, fk[1:2], jnp.where(rowh == 2, fk[2:3], fk[3:4])))
    col = lax.broadcasted_iota(jnp.int32, (DEC_ROWS, DEC_KEYS), 1)
    t = lax.broadcasted_iota(jnp.int32, (DEC_ROWS, DEC_KEYS), 0) % DEC_SEQ
    ok = (col < PAST_LEN) | (col - PAST_LEN <= t)
    s = jnp.where(ok, s + (fq_ref[0] - fk_rows), NEG_INF)
    p, l = _softmax_rows(s)
    pb = p.astype(bf16)
    o = lax.dot_general(pb[:, PAST_LEN:], vnew_ref[0], _NT, preferred_element_type=f32)
    for i, pg in enumerate(pages):
        o += lax.dot_general(pb[:, i * PAGE_SIZE:(i + 1) * PAGE_SIZE], pg[0, W:, :].astype(bf16), _NT,
                             preferred_element_type=f32)
    o = o / l
    lane_h = lax.broadcasted_iota(jnp.int32, (DEC_ROWS, W), 1) // HEAD_DIM
    o = jnp.where(lane_h == rowh, o, 0.0)
    o_ref[0] = o[0:8] + o[8:16] + o[16:24] + o[24:32]


def _pad_new_t(x):
    return jnp.pad(x.transpose(0, 2, 1), ((0, 0), (0, 0), (0, NEW_PAD - DEC_SEQ))).astype(jnp.bfloat16)


def _fox_decode(q, k_new, v_new, lf_new, cache_kv, cache_lf, page_table, layer):
    DB = q.shape[0]
    f32, bf16 = jnp.float32, jnp.bfloat16
    W = FOX_HEADS * HEAD_DIM
    eye = jnp.eye(FOX_HEADS, dtype=f32)
    qh = q.reshape(DB, DEC_SEQ, FOX_HEADS, HEAD_DIM).transpose(0, 2, 1, 3)
    qbd = (qh[:, :, :, None, :] * eye[None, :, None, :, None]).reshape(DB, DEC_ROWS, W).astype(bf16)
    lf_all = jnp.concatenate([cache_lf[layer][page_table].reshape(DB, PAST_LEN, FOX_HEADS).astype(f32), lf_new], axis=1)
    F = jnp.cumsum(lf_all, axis=1)
    fk = jnp.pad(F.transpose(0, 2, 1), ((0, 0), (0, 8 - FOX_HEADS), (0, DEC_KEYS - PAST_LEN - DEC_SEQ)))
    fq = F[:, PAST_LEN:].transpose(0, 2, 1).reshape(DB, DEC_ROWS, 1)
    per_seq = lambda r, w: pl.BlockSpec((1, r, w), lambda b, pt: (b, 0, 0))
    pages = _pages_token_minor(cache_kv)
    return pl.pallas_call(
        _fox_decode_kernel,
        grid_spec=pltpu.PrefetchScalarGridSpec(
            num_scalar_prefetch=1, grid=(DB,),
            in_specs=[per_seq(DEC_ROWS, W), per_seq(W, NEW_PAD), per_seq(W, NEW_PAD), per_seq(8, DEC_KEYS),
                      per_seq(DEC_ROWS, 1)] + _page_specs(2 * W, 0, layer, cache_kv.shape[1]),
            out_specs=per_seq(DEC_SEQ, W)),
        out_shape=jax.ShapeDtypeStruct((DB, DEC_SEQ, W), f32),
        compiler_params=pltpu.CompilerParams(dimension_semantics=("parallel",), vmem_limit_bytes=VMEM_LIMIT_BYTES),
        name="fox_decode",
    )(page_table.reshape(-1), qbd, _pad_new_t(k_new), _pad_new_t(v_new), fk, fq, *([pages] * N_PAGES))


def _nsa_decode_kernel(pt_ref, qs_ref, qf_ref, kcvc_ref, rnew_ref, wbuf_ref, wnew_ref, g_ref, e_ref, *refs):
    pages, o_ref = refs[:N_PAGES], refs[N_PAGES]
    f32, bf16 = jnp.float32, jnp.bfloat16
    H, T, HD = NSA_HEADS, DEC_SEQ, HEAD_DIM
    qs = qs_ref[0]
    row = lax.broadcasted_iota(jnp.int32, (DEC_ROWS, 1), 0)
    head = row // T
    slope = jnp.where(head == 0, NSA_SLOPES[0], jnp.where(head == 1, NSA_SLOPES[1],
                      jnp.where(head == 2, NSA_SLOPES[2], NSA_SLOPES[3]))).astype(f32)
    posq = PAST_LEN + row % T

    kcvc = kcvc_ref[0]
    n_cb = kcvc.shape[0]
    sc = lax.dot_general(qf_ref[0], kcvc, _NT, precision=_HI, preferred_element_type=f32)
    jb = lax.broadcasted_iota(jnp.int32, (DEC_ROWS, n_cb), 1)
    distc = posq - (jb * NSA_BLOCK + NSA_BLOCK - 1)
    okc = distc >= 0
    sc = jnp.where(okc, sc - slope * distc.astype(f32), NEG_INF)
    pc = jnp.exp(sc - jnp.max(sc, axis=-1, keepdims=True)) * okc.astype(f32)
    pc = pc / jnp.maximum(jnp.sum(pc, axis=-1, keepdims=True), TINY)
    o_cmp = jnp.dot(pc.astype(bf16), kcvc.astype(bf16), preferred_element_type=f32)[:, HD:]
    imp = pc[0:T] + pc[T:2 * T] + pc[2 * T:3 * T] + pc[3 * T:4 * T]
    imp = jnp.concatenate([imp, jnp.zeros((T, 128 - n_cb), f32)], axis=1)
    cur = (PAST_LEN + lax.broadcasted_iota(jnp.int32, (T, 1), 0)) // NSA_BLOCK
    msel = _select_blocks(imp, cur, NSA_TOPK).astype(bf16)

    mexp = jnp.dot(msel, e_ref[...], preferred_element_type=f32)
    col = lax.broadcasted_iota(jnp.int32, (T, DEC_KEYS), 1)
    tq = lax.broadcasted_iota(jnp.int32, (T, DEC_KEYS), 0)
    d = PAST_LEN + tq - col
    ok = (mexp > 0.5) & (d >= 0)
    kv = [pg[0].astype(bf16) for pg in pages] + [rnew_ref[0]]
    s = jnp.concatenate([jnp.dot(qs, x[:HD], preferred_element_type=f32) for x in kv], axis=1)
    s = jnp.where(_tile_rows(ok, H), s - slope * _tile_rows(d.astype(f32), H), NEG_INF)
    p, l = _softmax_rows(s)
    pb = p.astype(bf16)
    acc = jnp.zeros((DEC_ROWS, HD), f32)
    for i, x in enumerate(kv):
        acc += lax.dot_general(pb[:, i * PAGE_SIZE:(i + 1) * PAGE_SIZE], x[HD:], _NT, preferred_element_type=f32)
    o_sel = acc / l

    wb = wbuf_ref.shape[2]
    kvw = [wbuf_ref[0].astype(bf16), wnew_ref[0]]
    sw = jnp.concatenate([jnp.dot(qs, x[:HD], preferred_element_type=f32) for x in kvw], axis=1)
    colw = lax.broadcasted_iota(jnp.int32, (T, wb + NEW_PAD), 1)
    tw = lax.broadcasted_iota(jnp.int32, (T, wb + NEW_PAD), 0)
    dw = wb + tw - colw
    okw = (dw >= 0) & (dw < NSA_WINDOW)
    sw = jnp.where(_tile_rows(okw, H), sw - slope * _tile_rows(dw.astype(f32), H), NEG_INF)
    pw, lw = _softmax_rows(sw)
    pwb = pw.astype(bf16)
    accw = (lax.dot_general(pwb[:, :wb], kvw[0][HD:], _NT, preferred_element_type=f32)
            + lax.dot_general(pwb[:, wb:], kvw[1][HD:], _NT, preferred_element_type=f32))
    o_win = accw / lw

    g = _sigmoid(g_ref[0])
    gate = lambda c: jnp.concatenate([g[:, 3 * h + c:3 * h + c + 1] for h in range(H)], axis=0)
    o_ref[0] = gate(0) * o_cmp + gate(1) * o_sel + gate(2) * o_win


def _compress_pages_kernel(x_ref, wk_ref, wv_ref, o_ref):
    tm = x_ref.shape[0]
    f32 = jnp.float32
    acc_k = jnp.zeros((tm, 2 * HEAD_DIM), f32)
    acc_v = jnp.zeros((tm, 2 * HEAD_DIM), f32)
    pair = lambda r: jnp.concatenate([x_ref[:, r, :], x_ref[:, r + 1, :]], axis=1)
    for i in range(HEAD_DIM // 2):
        acc_k += jnp.dot(pair(2 * i), wk_ref[i], precision=_HI, preferred_element_type=f32)
        acc_v += jnp.dot(pair(HEAD_DIM + 2 * i), wv_ref[i], preferred_element_type=f32)
    o_ref[...] = jnp.concatenate([acc_k, acc_v], axis=1)


def _nsa_compress_pages(pages, w_ck, w_cv, layer, n_phys, tm=128):
    eye = jnp.eye(PAGE_SIZE // NSA_BLOCK, dtype=jnp.float32)
    big = lambda w: jnp.einsum('pde,bc->dbpce', w, eye).reshape(HEAD_DIM // 2, 2 * PAGE_SIZE, 2 * HEAD_DIM)
    wspec = pl.BlockSpec((HEAD_DIM // 2, 2 * PAGE_SIZE, 2 * HEAD_DIM), lambda i: (0, 0, 0))
    return pl.pallas_call(
        _compress_pages_kernel,
        grid=(n_phys // tm,),
        in_specs=[pl.BlockSpec((tm, 2 * HEAD_DIM, PAGE_SIZE), lambda i: (layer * (n_phys // tm) + i, 0, 0)),
                  wspec, wspec],
        out_specs=pl.BlockSpec((tm, 4 * HEAD_DIM), lambda i: (i, 0)),
        out_shape=jax.ShapeDtypeStruct((n_phys, 4 * HEAD_DIM), jnp.float32),
        compiler_params=pltpu.CompilerParams(dimension_semantics=("parallel",), vmem_limit_bytes=VMEM_LIMIT_BYTES),
        name="nsa_compress_pages",
    )(pages, big(w_ck), big(w_cv))


def _nsa_decode(nq, rows_new, win_new, gates, cache_rows, cache_win, page_table, w_ck, w_cv, layer):
    DB = nq.shape[0]
    f32, bf16 = jnp.float32, jnp.bfloat16
    HD = HEAD_DIM
    n_phys = cache_rows.shape[1]
    n_blk = PAGE_SIZE // NSA_BLOCK
    pages = _pages_token_minor(cache_rows)
    kcvc = _nsa_compress_pages(pages, w_ck, w_cv, layer, n_phys)[page_table]
    kcvc = kcvc.reshape(DB, N_PAGES, 2, n_blk, HD).transpose(0, 1, 3, 2, 4).reshape(DB, N_PAGES * n_blk, 2 * HD)
    stack = lambda x: x.reshape(DB, DEC_SEQ, NSA_HEADS, HD).transpose(0, 2, 1, 3).reshape(DB, DEC_ROWS, HD)
    qs = stack(nq).astype(bf16)
    qf = jnp.pad(stack(nq), ((0, 0), (0, 0), (0, HD))).astype(f32)
    win_t = cache_win.transpose(0, 1, 3, 4, 2).reshape(cache_win.shape[0] * DB, 2 * HD, cache_win.shape[2])
    colk = np.arange(DEC_KEYS)
    e = (np.arange(128)[:, None] == colk[None, :] // NSA_BLOCK) & (colk[None, :] < PAST_LEN + NSA_BLOCK)
    wb = cache_win.shape[2]
    per_seq = lambda r, w: pl.BlockSpec((1, r, w), lambda b, pt: (b, 0, 0))
    o = pl.pallas_call(
        _nsa_decode_kernel,
        grid_spec=pltpu.PrefetchScalarGridSpec(
            num_scalar_prefetch=1, grid=(DB,),
            in_specs=[per_seq(DEC_ROWS, HD), per_seq(DEC_ROWS, 2 * HD), per_seq(PAST_LEN // NSA_BLOCK, 2 * HD),
                      per_seq(2 * HD, NEW_PAD),
                      pl.BlockSpec((1, 2 * HD, wb), lambda b, pt: (layer * DB + b, 0, 0)),
                      per_seq(2 * HD, NEW_PAD), per_seq(DEC_SEQ, 128),
                      pl.BlockSpec((128, DEC_KEYS), lambda b, pt: (0, 0))]
                     + _page_specs(2 * HD, 1, layer, n_phys),
            out_specs=per_seq(DEC_ROWS, HD)),
        out_shape=jax.ShapeDtypeStruct((DB, DEC_ROWS, HD), f32),
        compiler_params=pltpu.CompilerParams(dimension_semantics=("parallel",), vmem_limit_bytes=VMEM_LIMIT_BYTES),
        name="nsa_decode",
    )(page_table.reshape(-1), qs, qf, kcvc, _pad_new_t(rows_new[..., 2 * HD:]), win_t,
      _pad_new_t(win_new), gates, jnp.asarray(e, bf16), *([pages] * N_PAGES))
    return o.reshape(DB, NSA_HEADS, DEC_SEQ, HD).transpose(0, 2, 1, 3).reshape(DB, DEC_SEQ, BRANCH_WIDTH)


def _stack_layers(states):
    return tuple(jnp.stack(list(a)) for a in zip(*states))


def kernel(x_prompt, x_sample, cache_nsa, cache_nsa_win, cache_fox_kv, cache_fox_logf,
           state_mlstm_c, state_mlstm_n, state_mlstm_m, page_table,
           ln_in_w, ln_in_b, w_in, b_in, nsa_w_ck, nsa_w_cv, mlstm_norm_w, w_branch, w_out,
           ln1_w, ln1_b, moe_w_group, moe_b_group, moe_w_expert, moe_b_expert,
           moe_w_gate, moe_w_up, moe_w_down, ln2_w, ln2_b):
    f32, bf16 = jnp.float32, jnp.bfloat16
    alpha = (2.0 * DEPTH) ** 0.25
    B, S, D = x_prompt.shape
    DB, T, _ = x_sample.shape
    TP, TS = B * S, DB * T
    HD = HEAD_DIM
    xp = _layer_norm_rows(x_prompt.reshape(TP, D), ln_in_w, ln_in_b)
    xs = _layer_norm_rows(x_sample.reshape(TS, D), ln_in_w, ln_in_b)
    cols = lambda a, c0, w: a[..., c0:c0 + w]
    colsb = lambda a, c0, w: a[..., c0 - Z_NQ:c0 - Z_NQ + w]
    zero_state = (jnp.zeros((B, MLSTM_HEADS, HD, HD), f32), jnp.zeros((B, MLSTM_HEADS, HD), f32),
                  jnp.zeros((B, MLSTM_HEADS), f32))
    new_p, new_s = [], []
    for l in range(DEPTH):
        zp2, zbp2 = _in_projection(xp, w_in[l], b_in[l])
        zs2, zbs2 = _in_projection(xs, w_in[l], b_in[l])
        zp, zbp = zp2.reshape(B, S, Z_WIDTH), zbp2.reshape(B, S, ZB_WIDTH)
        zs, zbs = zs2.reshape(DB, T, Z_WIDTH), zbs2.reshape(DB, T, ZB_WIDTH)
        small_p, small_s = cols(zp, Z_SMALL, 128), cols(zs, Z_SMALL, 128)

        o_nsa_p = _nsa_prompt(cols(zp, Z_NQ, 256), colsb(zbp, Z_NQ, 256), cols(zp, Z_ROWS, 256),
                              colsb(zbp, Z_ROWS, 256), colsb(zbp, Z_WIN, 128), small_p, nsa_w_ck[l], nsa_w_cv[l])
        o_nsa_s = _nsa_decode(cols(zs, Z_NQ, 256), colsb(zbs, Z_ROWS, 256), colsb(zbs, Z_WIN, 128), small_s,
                              cache_nsa, cache_nsa_win, page_table, nsa_w_ck[l], nsa_w_cv[l], l)

        def mlstm(q, k, v, og, small, state, L, nb, dt):
            return _mlstm(q, k, v, og, cols(small, SMALL_MI, MLSTM_HEADS), cols(small, SMALL_MF, MLSTM_HEADS),
                          mlstm_norm_w[l], *state, L, nb, dt)

        o_ml_p, st_ml_p = mlstm(colsb(zbp, Z_MQ, 256), colsb(zbp, Z_MK, 256), colsb(zbp, Z_MV, 256),
                                cols(zp, Z_MO, 256), small_p, zero_state, 128, B, bf16)
        o_ml_s, st_ml_s = mlstm(cols(zs, Z_MQ, 256), cols(zs, Z_MK, 256), cols(zs, Z_MV, 256), cols(zs, Z_MO, 256),
                                small_s, (state_mlstm_c[l], state_mlstm_n[l], state_mlstm_m[l]), T, 4, f32)

        lf_p = jax.nn.log_sigmoid(cols(small_p, SMALL_FF, FOX_HEADS))
        lf_s = jax.nn.log_sigmoid(cols(small_s, SMALL_FF, FOX_HEADS))
        o_fox_p = _fox_prompt_attn(zbp, jnp.cumsum(lf_p, axis=1))
        o_fox_s = _fox_decode(colsb(zbs, Z_FQ, 256), colsb(zbs, Z_FK, 256), colsb(zbs, Z_FV, 256), lf_s,
                              cache_fox_kv, cache_fox_logf, page_table, l)

        flat = lambda a: a.reshape(-1, a.shape[-1])
        moe_w = (moe_w_group[l], moe_b_group[l], moe_w_expert[l], moe_b_expert[l],
                 moe_w_gate[l], moe_w_up[l], moe_w_down[l], ln2_w[l], ln2_b[l], alpha)
        xp = _merge_ln(xp, zp2, flat(o_nsa_p), flat(o_ml_p), flat(o_fox_p),
                       w_branch[l], w_out[l], ln1_w[l], ln1_b[l], alpha)
        xp = _moe_ln(xp, *moe_w)
        xs = _merge_ln(xs, zs2, flat(o_nsa_s), flat(o_ml_s), flat(o_fox_s),
                       w_branch[l], w_out[l], ln1_w[l], ln1_b[l], alpha)
        xs = _moe_ln(xs, *moe_w)

        w_keep = min(NSA_WINDOW, S)
        new_p.append((cols(zp, Z_ROWS, 256).reshape(B, S, NSA_ROWS, HD),
                      cols(zp, Z_WIN, 128)[:, S - w_keep:].reshape(B, w_keep, 2, HD),
                      cols(zp, Z_FK, 512).reshape(B, S, 2, FOX_HEADS, HD), lf_p) + tuple(st_ml_p))
        win_new = cols(zs, Z_WIN, 128).reshape(DB, T, 2, HD).astype(cache_nsa_win.dtype)
        new_s.append((cols(zs, Z_ROWS, 256).reshape(DB, T, NSA_ROWS, HD),
                      win_new,
                      cols(zs, Z_FK, 512).reshape(DB, T, 2, FOX_HEADS, HD), lf_s) + tuple(st_ml_s))
    (p_nsa_rows, p_nsa_win, p_fox_kv, p_fox_logf, p_mlstm_c, p_mlstm_n, p_mlstm_m) = _stack_layers(new_p)
    (s_nsa_rows, s_win_new, s_fox_kv, s_fox_logf, s_mlstm_c, s_mlstm_n, s_mlstm_m) = _stack_layers(new_s)
    s_nsa_win = jnp.concatenate([cache_nsa_win[:, :, T:], s_win_new], axis=2)
    return (xp.reshape(B, S, D), xs.reshape(DB, T, D),
            p_nsa_rows, p_nsa_win, p_fox_kv, p_fox_logf, p_mlstm_c, p_mlstm_n, p_mlstm_m,
            s_nsa_rows, s_nsa_win, s_fox_kv, s_fox_logf, s_mlstm_c, s_mlstm_n, s_mlstm_m)
```

```python
import functools

import jax
import jax.numpy as jnp
import numpy as np
from jax import lax
from jax.experimental import pallas as pl
from jax.experimental.pallas import tpu as pltpu

D_MODEL = 1024
DEPTH = 2
DEC_SEQ = 8
PAST_LEN = 2048
PAGE_SIZE = 128

HEAD_DIM = 64
NSA_HEADS = 4
NSA_BLOCK = 64
NSA_TOPK = 16
NSA_WINDOW = 512
NSA_ROWS = 4
MLSTM_HEADS = 4
FOX_HEADS = 4
N_BRANCH = 3
BRANCH_WIDTH = NSA_HEADS * HEAD_DIM
N_GROUPS = 4
EXPERTS_PER_GROUP = 4
N_EXPERTS = N_GROUPS * EXPERTS_PER_GROUP
D_EXPERT = 256
LN_EPS = 1e-5
NEG_INF = -1e30
TINY = 1e-30
ATTN_SCALE = HEAD_DIM ** -0.5

IN_SPLITS = (
    NSA_HEADS * HEAD_DIM,
    6 * HEAD_DIM,
    NSA_HEADS * 3,
    MLSTM_HEADS * HEAD_DIM,
    MLSTM_HEADS * HEAD_DIM,
    MLSTM_HEADS * HEAD_DIM,
    MLSTM_HEADS * HEAD_DIM,
    MLSTM_HEADS,
    MLSTM_HEADS,
    FOX_HEADS * HEAD_DIM,
    FOX_HEADS * HEAD_DIM,
    FOX_HEADS * HEAD_DIM,
    FOX_HEADS,
    N_BRANCH * D_MODEL,
)


VMEM_LIMIT_BYTES = 48 * 1024 * 1024
NSA_SLOPES = tuple(2.0 ** (-8.0 * (h + 1) / NSA_HEADS) for h in range(NSA_HEADS))
_NT = (((1,), (1,)), ((), ()))
_HI = lax.Precision.HIGHEST


def _sigmoid(x):
    return 1.0 / (1.0 + jnp.exp(-x))


def _tile_rows(x, n):
    return jnp.concatenate([x] * n, axis=0)


def _compress_kernel(x_ref, w_ref, o_ref, acc_ref):
    k = pl.program_id(1)

    @pl.when(k == 0)
    def _():
        acc_ref[...] = jnp.zeros_like(acc_ref)

    acc_ref[...] += jnp.dot(x_ref[...], w_ref[...], preferred_element_type=jnp.float32, precision=_HI)

    @pl.when(k == pl.num_programs(1) - 1)
    def _():
        o_ref[...] = acc_ref[...]


def _compress_weights(w_ck, w_cv):
    z = jnp.zeros_like(w_ck)
    wk = jnp.stack([w_ck, z, z, z], axis=1)
    wv = jnp.stack([z, w_cv, z, z], axis=1)
    return jnp.concatenate([wk, wv], axis=-1).reshape(NSA_BLOCK * NSA_ROWS * HEAD_DIM, 2 * HEAD_DIM)


def _nsa_compress_blocks(blocks, w_big, tm=256, tk=2048):
    n, kdim = blocks.shape
    tm = min(tm, n)
    return pl.pallas_call(
        _compress_kernel,
        grid=(n // tm, kdim // tk),
        in_specs=[pl.BlockSpec((tm, tk), lambda i, k: (i, k)),
                  pl.BlockSpec((tk, 2 * HEAD_DIM), lambda i, k: (k, 0))],
        out_specs=pl.BlockSpec((tm, 2 * HEAD_DIM), lambda i, k: (i, 0)),
        out_shape=jax.ShapeDtypeStruct((n, 2 * HEAD_DIM), jnp.float32),
        scratch_shapes=[pltpu.VMEM((tm, 2 * HEAD_DIM), jnp.float32)],
        name="nsa_compress",
    )(blocks, w_big)


def _select_blocks(imp, cur, n_pick):
    q, n_sb = imp.shape
    jq = lax.broadcasted_iota(jnp.int32, (q, n_sb), 1)
    valid = jq <= cur
    forced = valid & ((jq == cur) | (jq == 0))
    work0 = jnp.where(forced, -jnp.inf, jnp.where(valid, imp, NEG_INF))

    def pick(_, carry):
        work, sel = carry
        hit = jq == jnp.argmax(work, axis=-1, keepdims=True).astype(jnp.int32)
        sel = jnp.where(hit & valid, 1.0, sel)
        work = jnp.where(hit, -jnp.inf, work)
        return work, sel

    _, sel = lax.fori_loop(0, n_pick - 2, pick, (work0, forced.astype(jnp.float32)), unroll=True)
    return sel


def _nsa_prompt_kernel(qs_ref, qf_ref, kc_ref, vc_ref, ksel_ref, vsel_ref, kwin_ref, vwin_ref, g_ref,
                       o_ref, flag_ref, *, tq, tk, tw):
    i = pl.program_id(1)
    f32, bf16 = jnp.float32, jnp.bfloat16
    H = NSA_HEADS
    R = H * tq
    qs = qs_ref[0].reshape(R, 2 * HEAD_DIM)
    qf = qf_ref[0].reshape(R, HEAD_DIM)
    row = lax.broadcasted_iota(jnp.int32, (R, 1), 0)
    head = row // tq
    slope = jnp.where(head == 0, NSA_SLOPES[0], jnp.where(head == 1, NSA_SLOPES[1],
                      jnp.where(head == 2, NSA_SLOPES[2], NSA_SLOPES[3]))).astype(f32)
    posq = i * tq + (row - head * tq)

    n_cb = kc_ref.shape[1]
    sc = lax.dot_general(qf, kc_ref[0], _NT, precision=_HI, preferred_element_type=f32)
    jb = lax.broadcasted_iota(jnp.int32, (R, n_cb), 1)
    distc = posq - (jb * NSA_BLOCK + NSA_BLOCK - 1)
    okc = distc >= 0
    sc = jnp.where(okc, sc - slope * distc.astype(f32), NEG_INF)
    pc = jnp.exp(sc - jnp.max(sc, axis=-1, keepdims=True)) * okc.astype(f32)
    pc = pc / jnp.maximum(jnp.sum(pc, axis=-1, keepdims=True), TINY)
    o_cmp = jnp.dot(pc.astype(bf16), vc_ref[0].astype(bf16), preferred_element_type=f32)
    imp = pc[0:tq] + pc[tq:2 * tq] + pc[2 * tq:3 * tq] + pc[3 * tq:4 * tq]

    pq = i * tq + lax.broadcasted_iota(jnp.int32, (tq, 1), 0)
    msel = _select_blocks(imp, pq // NSA_BLOCK, NSA_TOPK)

    bpt = tk // NSA_BLOCK
    blk_any = jnp.max(msel, axis=0, keepdims=True)
    for j in range(n_cb // bpt):
        flag_ref[j] = (jnp.max(blk_any[:, j * bpt:(j + 1) * bpt]) > 0.5).astype(jnp.int32)
    mbias = (NEG_INF * (1.0 - msel)).astype(bf16)

    rowpos = i * tq + lax.broadcasted_iota(jnp.int32, (tq, 1), 0)

    def attend(carry, s, v1):
        m, acc = carry
        m_new = jnp.maximum(m, jnp.max(s, axis=-1, keepdims=True))
        p = jnp.exp(s - m_new).astype(bf16)
        return m_new, jnp.exp(m - m_new) * acc + jnp.dot(p, v1, preferred_element_type=f32)

    init = (jnp.full((R, 1), NEG_INF, f32), jnp.zeros((R, 2 * HEAD_DIM), f32))

    def sel_tile(j, carry, causal):
        k0 = pl.multiple_of(j * tk, tk)
        s = lax.dot_general(qs, ksel_ref[0, pl.ds(k0, tk), :], _NT, preferred_element_type=f32)
        eb = (lax.broadcasted_iota(jnp.int32, (n_cb, tk), 0)
              == j * bpt + lax.broadcasted_iota(jnp.int32, (n_cb, tk), 1) // NSA_BLOCK)
        bias = jnp.dot(mbias, eb.astype(bf16), preferred_element_type=f32)
        if causal:
            d = rowpos - (k0 + lax.broadcasted_iota(jnp.int32, (tq, tk), 1))
            bias = jnp.where(d >= 0, bias, NEG_INF)
        return attend(carry, s + _tile_rows(bias, H), vsel_ref[0, pl.ds(k0, tk), :])

    def sel_body(j, carry):
        return lax.cond(flag_ref[j] > 0, lambda c: sel_tile(j, c, False), lambda c: c, carry)

    n_sel = (i * tq + tq - 1) // tk + 1
    carry = lax.fori_loop(0, n_sel - 1, sel_body, init)
    _, a_sel = sel_tile(n_sel - 1, carry, True)
    o_sel = a_sel[:, :HEAD_DIM] / a_sel[:, HEAD_DIM:]

    nw = NSA_WINDOW + tq
    w0 = pl.multiple_of(jnp.maximum(i * tq - NSA_WINDOW, 0), tw)
    sw = lax.dot_general(qs, kwin_ref[0, pl.ds(w0, nw), :], _NT, preferred_element_type=f32)
    dw = rowpos - (w0 + lax.broadcasted_iota(jnp.int32, (tq, nw), 1))
    okw = (dw >= 0) & (dw < NSA_WINDOW)
    sw = jnp.where(_tile_rows(okw, H), sw, NEG_INF)
    _, a_win = attend(init, sw, vwin_ref[0, pl.ds(w0, nw), :])
    o_win = a_win[:, :HEAD_DIM] / a_win[:, HEAD_DIM:]

    g = _sigmoid(g_ref[...])
    gate = lambda c: jnp.concatenate([g[:, 3 * h + c:3 * h + c + 1] for h in range(H)], axis=0)
    o = gate(0) * o_cmp + gate(1) * o_sel + gate(2) * o_win
    o_ref[0] = o.reshape(H, tq, HEAD_DIM)


def _heads_major(x, dtype):
    B, S, _ = x.shape
    return x.reshape(B, S, -1, HEAD_DIM).transpose(0, 2, 1, 3).astype(dtype)


def _nsa_prompt(nq, nqb, rows, rowsb, winb, gates, w_ck, w_cv, tq=256, tk=512, tw=128):
    B, S, _ = nq.shape
    bf16 = jnp.bfloat16
    kcvc = _nsa_compress_blocks(rows.reshape(B * S // NSA_BLOCK, -1), _compress_weights(w_ck, w_cv))
    kcvc = kcvc.reshape(B, S // NSA_BLOCK, 2 * HEAD_DIM)
    kc, vc = kcvc[..., :HEAD_DIM], kcvc[..., HEAD_DIM:]
    pos = jnp.arange(S)
    blk, off = (pos // NSA_BLOCK).astype(jnp.float32), (pos % NSA_BLOCK).astype(jnp.float32)
    one = jnp.ones((S,), jnp.float32)
    slopes = jnp.asarray(NSA_SLOPES, jnp.float32)[:, None]
    q_terms = jnp.stack([-slopes * NSA_BLOCK * blk, -slopes * off, slopes * NSA_BLOCK * one, slopes * one], axis=-1)
    k_terms = jnp.stack([one, one, blk, off], axis=-1)
    pad_terms = lambda t: jnp.pad(t, [(0, 0)] * (t.ndim - 1) + [(0, HEAD_DIM - 4)]).astype(bf16)
    with_k_terms = lambda k: jnp.concatenate([k, jnp.broadcast_to(pad_terms(k_terms), (B, S, HEAD_DIM))], axis=-1)
    qs = jnp.concatenate([_heads_major(nqb, bf16),
                          jnp.broadcast_to(pad_terms(q_terms), (B, NSA_HEADS, S, HEAD_DIM))], axis=-1)
    qf = _heads_major(nq, jnp.float32)
    rows, win = rowsb, winb
    col = lambda a, c: a[..., c * HEAD_DIM:(c + 1) * HEAD_DIM]
    with_ones = lambda v: jnp.concatenate([v, jnp.ones_like(v)], axis=-1)
    full = lambda n, w: pl.BlockSpec((1, n, w), lambda b, i: (b, 0, 0))
    qspec = pl.BlockSpec((1, NSA_HEADS, tq, HEAD_DIM), lambda b, i: (b, 0, i, 0))
    n_cb = S // NSA_BLOCK
    o = pl.pallas_call(
        functools.partial(_nsa_prompt_kernel, tq=tq, tk=tk, tw=tw),
        grid=(B, S // tq),
        in_specs=[pl.BlockSpec((1, NSA_HEADS, tq, 2 * HEAD_DIM), lambda b, i: (b, 0, i, 0)), qspec,
                  full(n_cb, HEAD_DIM), full(n_cb, HEAD_DIM), full(S, 2 * HEAD_DIM),
                  full(S, 2 * HEAD_DIM), full(S, 2 * HEAD_DIM), full(S, 2 * HEAD_DIM),
                  pl.BlockSpec((None, tq, 128), lambda b, i: (b, i, 0))],
        out_specs=qspec,
        out_shape=jax.ShapeDtypeStruct((B, NSA_HEADS, S, HEAD_DIM), jnp.float32),
        scratch_shapes=[pltpu.SMEM((S // tk,), jnp.int32)],
        compiler_params=pltpu.CompilerParams(dimension_semantics=("parallel", "arbitrary"),
                                             vmem_limit_bytes=VMEM_LIMIT_BYTES),
        name="nsa_prompt",
    )(qs, qf, kc, vc, with_k_terms(col(rows, 2)), with_ones(col(rows, 3)),
      with_k_terms(col(win, 0)), with_ones(col(win, 1)), gates)
    return o.transpose(0, 2, 1, 3).reshape(B, S, BRANCH_WIDTH)


def _fox_prompt_kernel(q_ref, k_ref, v_ref, fq_ref, fk_ref, o_ref, *, tq, t):
    i = pl.program_id(1)
    f32, bf16 = jnp.float32, jnp.bfloat16
    lane = lax.broadcasted_iota(jnp.int32, (1, 2 * HEAD_DIM), 1)
    low = lane < HEAD_DIM
    rc = lax.broadcasted_iota(jnp.int32, (tq, t), 0) - lax.broadcasted_iota(jnp.int32, (tq, t), 1)
    n_pairs = FOX_HEADS // 2
    pair_cols = [slice(hp * 2 * HEAD_DIM, (hp + 1) * 2 * HEAD_DIM) for hp in range(n_pairs)]
    q_h, fq_h = [], []
    for hp in range(n_pairs):
        q2 = q_ref[0, :, pair_cols[hp]]
        zq = jnp.zeros_like(q2)
        q_h += [jnp.where(low, q2, zq), jnp.where(low, zq, q2)]
        fq_h += [fq_ref[0, :, 2 * hp:2 * hp + 1], fq_ref[0, :, 2 * hp + 1:2 * hp + 2]]

    def body(j, carry, diag_offset):
        k0 = pl.multiple_of(j * t, t)
        fk = fk_ref[0, j]
        out = []
        for hp in range(n_pairs):
            k2 = k_ref[0, pl.ds(k0, t), pair_cols[hp]]
            v2 = v_ref[0, pl.ds(k0, t), pair_cols[hp]]
            one = jnp.ones_like(v2)
            v1 = (jnp.where(low, v2, one), jnp.where(low, one, v2))
            for x in range(2):
                h = 2 * hp + x
                m, acc = carry[h]
                s = lax.dot_general(q_h[h], k2, _NT, preferred_element_type=f32) + (fq_h[h] - fk[h:h + 1])
                if diag_offset is not None:
                    s = jnp.where(rc >= diag_offset, s, NEG_INF)
                m_new = jnp.maximum(m, jnp.max(s, axis=-1, keepdims=True))
                p = jnp.exp(s - m_new).astype(bf16)
                acc = jnp.exp(m - m_new) * acc + jnp.dot(p, v1[x], preferred_element_type=f32)
                out.append((m_new, acc))
        return tuple(out)

    init = tuple((jnp.full((tq, 1), NEG_INF, f32), jnp.zeros((tq, 2 * HEAD_DIM), f32)) for _ in range(FOX_HEADS))
    n_full = i * (tq // t)
    carry = lax.fori_loop(0, n_full, functools.partial(body, diag_offset=None), init)
    for jj in range(tq // t):
        carry = body(n_full + jj, carry, jj * t)
    for hp in range(n_pairs):
        acc_a, acc_b = carry[2 * hp][1], carry[2 * hp + 1][1]
        den = pltpu.roll(jnp.where(low, acc_b, acc_a), HEAD_DIM, axis=1)
        o_ref[0, :, pair_cols[hp]] = jnp.where(low, acc_a, acc_b) / den


def _fox_prompt_attn(zb, F, tq=1024, t=1024):
    B, S, _ = zb.shape
    W = FOX_HEADS * HEAD_DIM
    cq, ck, cv = ((c - Z_NQ) // W for c in (Z_FQ, Z_FK, Z_FV))
    fk = jnp.pad(F.transpose(0, 2, 1), ((0, 0), (0, 8 - FOX_HEADS), (0, 0)))
    fk = fk.reshape(B, 8, S // t, t).transpose(0, 2, 1, 3)
    return pl.pallas_call(
        functools.partial(_fox_prompt_kernel, tq=tq, t=t),
        grid=(B, S // tq),
        in_specs=[pl.BlockSpec((1, tq, W), lambda b, i: (b, i, cq)),
                  pl.BlockSpec((1, S, W), lambda b, i: (b, 0, ck)),
                  pl.BlockSpec((1, S, W), lambda b, i: (b, 0, cv)),
                  pl.BlockSpec((1, tq, FOX_HEADS), lambda b, i: (b, i, 0)),
                  pl.BlockSpec((1, S // t, 8, t), lambda b, i: (b, 0, 0, 0))],
        out_specs=pl.BlockSpec((1, tq, W), lambda b, i: (b, i, 0)),
        out_shape=jax.ShapeDtypeStruct((B, S, W), jnp.float32),
        compiler_params=pltpu.CompilerParams(dimension_semantics=("parallel", "arbitrary"),
                                             vmem_limit_bytes=VMEM_LIMIT_BYTES),
        name="fox_prompt",
    )(zb, zb, zb, F, fk)


def _log_sigmoid(x):
    return jnp.minimum(x, 0.0) - jnp.log1p(jnp.exp(-jnp.abs(x)))


def _mlstm_kernel(q_ref, k_ref, kt_ref, v_ref, og_ref, gc_ref, gr_ref, nw_ref, cn0_ref, m0_ref,
                  h_ref, cn_ref, m_ref, cn_s, m_s, *, nb, L):
    c = pl.program_id(1)
    f32 = jnp.float32
    W = 2 * HEAD_DIM
    n_pairs = MLSTM_HEADS // 2

    @pl.when(c == 0)
    def _():
        cn_s[...] = cn0_ref[...]
        m_s[...] = m0_ref[...]

    lane = lax.broadcasted_iota(jnp.int32, (1, W), 1)
    low = lane < HEAD_DIM
    ti = lax.broadcasted_iota(jnp.int32, (L, L), 0)
    si = lax.broadcasted_iota(jnp.int32, (L, L), 1)
    causal = si <= ti
    tri = causal.astype(f32)
    tri_t = (ti <= si).astype(f32)
    srow = lax.broadcasted_iota(jnp.int32, (W, 2 * W), 0)
    slane = lax.broadcasted_iota(jnp.int32, (W, 2 * W), 1)
    top = srow < HEAD_DIM
    keep_a = top & ((slane < HEAD_DIM) | (slane == W))
    keep_b = (~top) & (((slane >= HEAD_DIM) & (slane < W)) | (slane == W + 1))
    lane_w = lax.broadcasted_iota(jnp.int32, (1, W), 1)
    mdt = k_ref.dtype

    for b in range(nb):
        gcol = gc_ref[b]
        grow = gr_ref[b]
        bcol = jnp.dot(tri, _log_sigmoid(gcol), precision=_HI, preferred_element_type=f32)
        brow = jnp.dot(_log_sigmoid(grow), tri_t, precision=_HI, preferred_element_type=f32)
        for hp in range(n_pairs):
            cols = slice(hp * W, (hp + 1) * W)
            q2 = q_ref[b, :, cols]
            k2 = k_ref[b, :, cols]
            v2 = v_ref[b, :, cols]
            kt2 = kt_ref[b, cols, :]
            cn = cn_s[b, hp]
            r = jnp.dot(q2, cn.astype(mdt), preferred_element_type=f32)
            zq = jnp.zeros_like(q2)
            per_head = []
            for x in range(2):
                h = 2 * hp + x
                qx = jnp.where(low, q2, zq) if x == 0 else jnp.where(low, zq, q2)
                b_c = bcol[:, MLSTM_HEADS + h:MLSTM_HEADS + h + 1]
                b_r = brow[MLSTM_HEADS + h:MLSTM_HEADS + h + 1, :]
                ig_c = gcol[:, h:h + 1]
                ig_r = grow[h:h + 1, :]
                m_prev = m_s[b, 0:1, h:h + 1]
                dmat = jnp.where(causal, b_c - b_r + ig_r, NEG_INF)
                a_c = b_c + m_prev
                m_t = jnp.maximum(a_c, jnp.max(dmat, axis=-1, keepdims=True))
                wq = jnp.exp(dmat - m_t) * lax.dot_general(qx, k2, _NT, preferred_element_type=f32)
                inter = jnp.exp(a_c - m_t)
                wv = jnp.dot(wq.astype(mdt), v2, preferred_element_type=f32)
                den = inter * r[:, W + x:W + x + 1] + jnp.sum(wq, axis=-1, keepdims=True)
                den = jnp.maximum(jnp.abs(den), jnp.exp(-m_t))
                bl = b_c[L - 1:L, :]
                g_c = bl - b_c + ig_c
                m_new = jnp.maximum(bl + m_prev, jnp.max(g_c, axis=0, keepdims=True))
                ws = jnp.exp(g_c - m_new)
                decay = jnp.exp(bl + m_prev - m_new)
                aug = jnp.concatenate([v2.astype(f32) * ws, jnp.where(lane_w == x, ws, 0.0)], axis=1)
                u = jnp.dot(kt2, aug.astype(mdt), preferred_element_type=f32)
                per_head.append((inter, wv, den, decay, u))
                m_s[b, 0:1, h:h + 1] = m_new
            (ia, wva, dena, deca, ua), (ib, wvb, denb, decb, ub) = per_head
            num = jnp.where(low, ia * r[:, :W] + wva, ib * r[:, :W] + wvb)
            hid = num / jnp.where(low, dena, denb)
            cn_s[b, hp] = (jnp.where(top, deca, decb) * cn + jnp.where(keep_a, ua, 0.0)
                           + jnp.where(keep_b, ub, 0.0))
            hid = _sigmoid(og_ref[b, :, cols]) * hid
            mean = lambda t: jnp.where(low, jnp.sum(jnp.where(low, t, 0.0), axis=-1, keepdims=True),
                                       jnp.sum(jnp.where(low, 0.0, t), axis=-1, keepdims=True)) / HEAD_DIM
            mu = mean(hid)
            var = mean(jnp.square(hid - mu))
            h_ref[b, :, cols] = (hid - mu) * lax.rsqrt(var + LN_EPS) * nw_ref[:, cols]

    @pl.when(c == pl.num_programs(1) - 1)
    def _():
        cn_ref[...] = cn_s[...]
        m_ref[...] = m_s[...]


def _mlstm_state_pack(c, n, m):
    Bx = c.shape[0]
    HD, W = HEAD_DIM, 2 * HEAD_DIM
    cp = c.reshape(Bx, MLSTM_HEADS // 2, 2, HD, HD)
    np_ = n.reshape(Bx, MLSTM_HEADS // 2, 2, HD, 1)
    z = lambda w: jnp.zeros((Bx, MLSTM_HEADS // 2, HD, w), jnp.float32)
    top = jnp.concatenate([cp[:, :, 0], z(HD), np_[:, :, 0], z(W - 1)], axis=-1)
    bot = jnp.concatenate([z(HD), cp[:, :, 1], z(1), np_[:, :, 1], z(W - 2)], axis=-1)
    cn = jnp.concatenate([top, bot], axis=-2)
    m8 = jnp.pad(m[:, None, :], ((0, 0), (0, 7), (0, 128 - MLSTM_HEADS)))
    return cn, m8


def _mlstm_state_unpack(cn, m8):
    W = 2 * HEAD_DIM
    c = jnp.stack([cn[:, h // 2, (h % 2) * HEAD_DIM:(h % 2 + 1) * HEAD_DIM,
                      (h % 2) * HEAD_DIM:(h % 2 + 1) * HEAD_DIM] for h in range(MLSTM_HEADS)], axis=1)
    n = jnp.stack([cn[:, h // 2, (h % 2) * HEAD_DIM:(h % 2 + 1) * HEAD_DIM, W + h % 2]
                   for h in range(MLSTM_HEADS)], axis=1)
    return c, n, m8[:, 0, :MLSTM_HEADS]


def _mlstm(q, k, v, og, mi, mf, norm_w, c0, n0, m0, L, nb, mxu_dtype):
    Bx, S, W4 = q.shape
    f32 = jnp.float32
    gates = jnp.concatenate([mi, mf], axis=-1).astype(f32)
    gcol = jnp.pad(gates, ((0, 0), (0, 0), (0, 128 - 2 * MLSTM_HEADS)))
    grow = gates.transpose(0, 2, 1)
    cn0, m8 = _mlstm_state_pack(c0.astype(f32), n0.astype(f32), m0.astype(f32))
    tok = lambda w: pl.BlockSpec((nb, L, w), lambda b, c: (b, c, 0))
    st_cn = pl.BlockSpec((nb, MLSTM_HEADS // 2, 128, 256), lambda b, c: (b, 0, 0, 0))
    st_m = pl.BlockSpec((nb, 8, 128), lambda b, c: (b, 0, 0))
    h, cn, m8 = pl.pallas_call(
        functools.partial(_mlstm_kernel, nb=nb, L=L),
        grid=(Bx // nb, S // L),
        in_specs=[tok(W4), tok(W4), pl.BlockSpec((nb, W4, L), lambda b, c: (b, 0, c)), tok(W4), tok(W4),
                  tok(128), pl.BlockSpec((nb, 8, L), lambda b, c: (b, 0, c)),
                  pl.BlockSpec((1, W4), lambda b, c: (0, 0)), st_cn, st_m],
        out_specs=[tok(W4), st_cn, st_m],
        out_shape=[jax.ShapeDtypeStruct((Bx, S, W4), f32),
                   jax.ShapeDtypeStruct(cn0.shape, f32), jax.ShapeDtypeStruct(m8.shape, f32)],
        scratch_shapes=[pltpu.VMEM((nb, MLSTM_HEADS // 2, 128, 256), f32), pltpu.VMEM((nb, 8, 128), f32)],
        compiler_params=pltpu.CompilerParams(dimension_semantics=("parallel", "arbitrary"),
                                             vmem_limit_bytes=VMEM_LIMIT_BYTES),
        name="mlstm",
    )(q.astype(mxu_dtype), k.astype(mxu_dtype), k.astype(mxu_dtype).transpose(0, 2, 1), v.astype(mxu_dtype),
      og.astype(f32), gcol, grow, norm_w.astype(f32).reshape(1, W4), cn0, m8)
    return h, _mlstm_state_unpack(cn, m8)


def _ln(x, w, b):
    mu = jnp.mean(x, axis=-1, keepdims=True)
    var = jnp.mean(jnp.square(x - mu), axis=-1, keepdims=True)
    return (x - mu) * lax.rsqrt(var + LN_EPS) * w + b


def _ln_kernel(x_ref, w_ref, b_ref, o_ref):
    o_ref[...] = _ln(x_ref[...], w_ref[...], b_ref[...])


def _layer_norm_rows(x, w, b, tm=1024):
    T, D = x.shape
    vec = pl.BlockSpec((1, D), lambda i: (0, 0))
    return pl.pallas_call(
        _ln_kernel, grid=(T // tm,),
        in_specs=[pl.BlockSpec((tm, D), lambda i: (i, 0)), vec, vec],
        out_specs=pl.BlockSpec((tm, D), lambda i: (i, 0)),
        out_shape=jax.ShapeDtypeStruct((T, D), jnp.float32),
        name="layer_norm",
    )(x, w.reshape(1, D), b.reshape(1, D))


Z_NQ = 0
Z_ROWS = Z_NQ + 256
Z_MQ = 512
Z_MK = Z_MQ + 256
Z_MV = 1024
Z_MO = Z_MV + 256
Z_FQ = 1536
Z_WIN = Z_FQ + 256
Z_SMALL = Z_WIN + 128
Z_FK = 2048
Z_FV = Z_FK + 256
Z_WIDTH = 2560
Z_TILE = 512
ZB_WIDTH = Z_WIDTH
GM_SPLIT = 13
SMALL_MI, SMALL_MF, SMALL_FF = 12, 16, 20


def _z_column_order():
    starts = np.concatenate([[0], np.cumsum(IN_SPLITS)])
    seg = lambda i, lo=0, hi=None: np.arange(starts[i] + lo, starts[i] + (IN_SPLITS[i] if hi is None else hi))
    pad = lambda n: np.full((n,), -1)
    order = np.concatenate([
        seg(0), seg(1, 0, 256), seg(3), seg(4), seg(5), seg(6),
        seg(9), seg(1, 256, 384), seg(2), seg(7), seg(8), seg(12), pad(128 - 24), seg(10), seg(11)])
    assert order.shape == (Z_WIDTH,)
    return order


def _permute_in_proj(w_in, b_in):
    order = _z_column_order()
    valid = jnp.asarray(order >= 0)
    idx = jnp.asarray(np.maximum(order, 0))
    scale = np.ones((Z_WIDTH,), np.float32)
    for c0 in (Z_NQ, Z_MK, Z_FQ):
        scale[c0:c0 + 256] = ATTN_SCALE
    scale = jnp.asarray(scale)
    w = jnp.where(valid[None, :], w_in[:, idx], 0.0) * scale[None, :]
    b = jnp.where(valid, b_in[idx], 0.0) * scale
    return w, b


def _proj_kernel(x_ref, w_ref, wlo_ref, b_ref, o_ref, ob_ref, xh_s, xl_s, *, hi_tile):
    j = pl.program_id(1)
    f32 = jnp.float32
    _proj_tile(x_ref, w_ref, wlo_ref, b_ref, o_ref, xh_s, xl_s, j, hi_tile)
    ob_ref[...] = o_ref[...].astype(jnp.bfloat16)


def _proj_tile(x_ref, w_ref, wlo_ref, b_ref, o_ref, xh_s, xl_s, j, hi_tile):
    f32 = jnp.float32

    @pl.when(j == 0)
    def _():
        x = x_ref[...]
        xh = x.astype(jnp.bfloat16)
        xh_s[...] = xh
        xl_s[...] = (x - xh.astype(f32)).astype(jnp.bfloat16)

    @pl.when(j != hi_tile)
    def _():
        o_ref[...] = jnp.dot(xh_s[...], w_ref[...], preferred_element_type=f32) + b_ref[...]

    @pl.when(j == hi_tile)
    def _():
        acc = jnp.dot(xh_s[...], wlo_ref[...], preferred_element_type=f32)
        acc += jnp.dot(xl_s[...], w_ref[...], preferred_element_type=f32)
        acc += jnp.dot(xh_s[...], w_ref[...], preferred_element_type=f32)
        o_ref[...] = acc + b_ref[...]


def _in_projection(x, w_in, b_in, tm=1024):
    T, D = x.shape
    w, b = _permute_in_proj(w_in, b_in)
    wh = w.astype(jnp.bfloat16)
    hi_tile = Z_NQ // Z_TILE
    wlo = (w[:, Z_NQ:Z_NQ + Z_TILE] - wh[:, Z_NQ:Z_NQ + Z_TILE].astype(jnp.float32)).astype(jnp.bfloat16)
    return pl.pallas_call(
        functools.partial(_proj_kernel, hi_tile=hi_tile),
        grid=(T // tm, Z_WIDTH // Z_TILE),
        in_specs=[pl.BlockSpec((tm, D), lambda i, j: (i, 0)),
                  pl.BlockSpec((D, Z_TILE), lambda i, j: (0, j)),
                  pl.BlockSpec((D, Z_TILE), lambda i, j: (0, 0)),
                  pl.BlockSpec((1, Z_TILE), lambda i, j: (0, j))],
        out_specs=[pl.BlockSpec((tm, Z_TILE), lambda i, j: (i, j)),
                   pl.BlockSpec((tm, Z_TILE), lambda i, j: (i, j))],
        out_shape=[jax.ShapeDtypeStruct((T, Z_WIDTH), jnp.float32),
                   jax.ShapeDtypeStruct((T, ZB_WIDTH), jnp.bfloat16)],
        scratch_shapes=[pltpu.VMEM((tm, D), jnp.bfloat16), pltpu.VMEM((tm, D), jnp.bfloat16)],
        compiler_params=pltpu.CompilerParams(dimension_semantics=("parallel", "arbitrary"),
                                             vmem_limit_bytes=VMEM_LIMIT_BYTES),
        name="in_projection",
    )(x, wh, wlo, b.reshape(1, Z_WIDTH))


def _merge_kernel(x_ref, on_ref, om_ref, of_ref, wg_ref, bg_ref, wb_ref, wo_ref, lw_ref, lb_ref, o_ref, *, alpha):
    f32, bf16 = jnp.float32, jnp.bfloat16
    x = x_ref[...]
    xb = x.astype(bf16)
    y = None
    for m, br in enumerate((on_ref, om_ref, of_ref)):
        gcols = slice(m * D_MODEL, (m + 1) * D_MODEL)
        gm = jnp.dot(xb, wg_ref[:, gcols], preferred_element_type=f32) + bg_ref[:, gcols]
        proj = jnp.dot(br[...].astype(bf16), wb_ref[m], preferred_element_type=f32)
        term = _sigmoid(gm) * proj
        y = term if y is None else y + term
    mix = jnp.dot(y.astype(bf16), wo_ref[...], preferred_element_type=f32)
    o_ref[...] = _ln(alpha * x + mix, lw_ref[...], lb_ref[...])


def _merge_ln(x, o_nsa, o_ml, o_fox, w_in, b_in, w_branch, w_out, ln_w, ln_b, alpha, tm=512):
    T, D = x.shape
    g0 = int(np.sum(IN_SPLITS[:GM_SPLIT]))
    w_gm = w_in[:, g0:g0 + N_BRANCH * D].astype(jnp.bfloat16)
    b_gm = b_in[g0:g0 + N_BRANCH * D].reshape(1, N_BRANCH * D)
    tok = lambda w: pl.BlockSpec((tm, w), lambda i: (i, 0))
    vec = pl.BlockSpec((1, D), lambda i: (0, 0))
    return pl.pallas_call(
        functools.partial(_merge_kernel, alpha=alpha),
        grid=(T // tm,),
        in_specs=[tok(D), tok(BRANCH_WIDTH), tok(BRANCH_WIDTH), tok(BRANCH_WIDTH),
                  pl.BlockSpec((D, N_BRANCH * D), lambda i: (0, 0)),
                  pl.BlockSpec((1, N_BRANCH * D), lambda i: (0, 0)),
                  pl.BlockSpec((N_BRANCH, BRANCH_WIDTH, D), lambda i: (0, 0, 0)),
                  pl.BlockSpec((D, D), lambda i: (0, 0)), vec, vec],
        out_specs=tok(D),
        out_shape=jax.ShapeDtypeStruct((T, D), jnp.float32),
        compiler_params=pltpu.CompilerParams(dimension_semantics=("parallel",),
                                             vmem_limit_bytes=VMEM_LIMIT_BYTES),
        name="merge_ln",
    )(x, o_nsa, o_ml, o_fox, w_gm, b_gm, w_branch.astype(jnp.bfloat16), w_out.astype(jnp.bfloat16),
      ln_w.reshape(1, D), ln_b.reshape(1, D))


def _route(logits):
    tm = logits.shape[0]
    lane = lax.broadcasted_iota(jnp.int32, (tm, 128), 1)
    lanef = lane.astype(jnp.float32)
    big = 1e9
    is_g = lane < N_GROUPS
    lg = jnp.where(is_g, logits, -jnp.inf)
    eg = jnp.exp(lg - jnp.max(lg, axis=-1, keepdims=True))
    pg = eg / jnp.sum(eg, axis=-1, keepdims=True)
    g_val = jnp.max(pg, axis=-1, keepdims=True)
    g_idx = jnp.min(jnp.where(is_g & (pg == g_val), lanef, big), axis=-1, keepdims=True)
    e_lo = N_GROUPS + EXPERTS_PER_GROUP * g_idx
    in_grp = (lanef >= e_lo) & (lanef < e_lo + EXPERTS_PER_GROUP)
    le = jnp.where(in_grp, logits, -jnp.inf)
    ee = jnp.exp(le - jnp.max(le, axis=-1, keepdims=True))
    pe = ee / jnp.sum(ee, axis=-1, keepdims=True)
    v1 = jnp.max(pe, axis=-1, keepdims=True)
    i1 = jnp.min(jnp.where(in_grp & (pe == v1), lanef, big), axis=-1, keepdims=True)
    rest = in_grp & (lanef != i1)
    pe2 = jnp.where(rest, pe, -1.0)
    v2 = jnp.max(pe2, axis=-1, keepdims=True)
    i2 = jnp.min(jnp.where(rest & (pe2 == v2), lanef, big), axis=-1, keepdims=True)
    tot = v1 + v2
    return jnp.where(lanef == i1, g_val * v1 / tot, jnp.where(lanef == i2, g_val * v2 / tot, 0.0))


def _moe_kernel(x_ref, wr_ref, br_ref, wg_ref, wu_ref, wd_ref, lw_ref, lb_ref, o_ref, xb_s, gate_s, acc_s, *, alpha):
    e = pl.program_id(1)
    f32, bf16 = jnp.float32, jnp.bfloat16

    @pl.when(e == 0)
    def _():
        x = x_ref[...]
        xb_s[...] = x.astype(bf16)
        logits = jnp.dot(x, wr_ref[...], precision=_HI, preferred_element_type=f32) + br_ref[...]
        gate_s[...] = _route(logits)
        acc_s[...] = jnp.zeros_like(acc_s)

    lane = lax.broadcasted_iota(jnp.int32, gate_s.shape, 1)
    y = None
    for k in range(wg_ref.shape[0]):
        ex = e * wg_ref.shape[0] + k
        gate = jnp.sum(jnp.where(lane == N_GROUPS + ex, gate_s[...], 0.0), axis=-1, keepdims=True)
        g = jnp.dot(xb_s[...], wg_ref[k].astype(bf16), preferred_element_type=f32)
        u = jnp.dot(xb_s[...], wu_ref[k].astype(bf16), preferred_element_type=f32)
        h = (g * _sigmoid(g)) * u * gate
        yk = jnp.dot(h.astype(bf16), wd_ref[k].astype(bf16), preferred_element_type=f32)
        y = yk if y is None else y + yk
    acc_s[...] += y

    @pl.when(e == pl.num_programs(1) - 1)
    def _():
        o_ref[...] = _ln(alpha * x_ref[...] + acc_s[...], lw_ref[...], lb_ref[...])


def _moe_ln(x, w_group, b_group, w_expert, b_expert, w_gate, w_up, w_down, ln_w, ln_b, alpha, tm=1024, eps=2):
    T, D = x.shape
    bf16 = jnp.bfloat16
    n_r = N_GROUPS + N_EXPERTS
    wr = jnp.pad(jnp.concatenate([w_group, w_expert], axis=1), ((0, 0), (0, 128 - n_r)))
    br = jnp.pad(jnp.concatenate([b_group, b_expert]), (0, 128 - n_r)).reshape(1, 128)
    vec = pl.BlockSpec((1, D), lambda i, e: (0, 0))
    return pl.pallas_call(
        functools.partial(_moe_kernel, alpha=alpha),
        grid=(T // tm, N_EXPERTS // eps),
        in_specs=[pl.BlockSpec((tm, D), lambda i, e: (i, 0)),
                  pl.BlockSpec((D, 128), lambda i, e: (0, 0)),
                  pl.BlockSpec((1, 128), lambda i, e: (0, 0)),
                  pl.BlockSpec((eps, D, D_EXPERT), lambda i, e: (e, 0, 0)),
                  pl.BlockSpec((eps, D, D_EXPERT), lambda i, e: (e, 0, 0)),
                  pl.BlockSpec((eps, D_EXPERT, D), lambda i, e: (e, 0, 0)), vec, vec],
        out_specs=pl.BlockSpec((tm, D), lambda i, e: (i, 0)),
        out_shape=jax.ShapeDtypeStruct((T, D), jnp.float32),
        scratch_shapes=[pltpu.VMEM((tm, D), bf16), pltpu.VMEM((tm, 128), jnp.float32),
                        pltpu.VMEM((tm, D), jnp.float32)],
        compiler_params=pltpu.CompilerParams(dimension_semantics=("parallel", "arbitrary"),
                                             vmem_limit_bytes=VMEM_LIMIT_BYTES),
        name="moe_ln",
    )(x, wr, br, w_gate, w_up, w_down, ln_w.reshape(1, D), ln_b.reshape(1, D))


N_PAGES = PAST_LEN // PAGE_SIZE
NEW_PAD = 128
DEC_KEYS = PAST_LEN + NEW_PAD
DEC_ROWS = NSA_HEADS * DEC_SEQ


def _pages_token_minor(cache):
    nd = cache.ndim
    t = cache.transpose((0, 1) + tuple(range(3, nd)) + (2,))
    return t.reshape(cache.shape[0] * cache.shape[1], -1, cache.shape[2])


def _page_specs(rows, row_block, layer, n_phys):
    def spec(p):
        return pl.BlockSpec((1, rows, PAGE_SIZE),
                            lambda b, pt: (layer * n_phys + pt[b * N_PAGES + p], row_block, 0))
    return [spec(p) for p in range(N_PAGES)]


def _softmax_rows(s):
    m = jnp.max(s, axis=-1, keepdims=True)
    p = jnp.exp(s - m)
    return p, jnp.sum(p, axis=-1, keepdims=True)


def _fox_decode_kernel(pt_ref, qbd_ref, knew_ref, vnew_ref, fk_ref, fq_ref, *refs):
    pages, o_ref = refs[:N_PAGES], refs[N_PAGES]
    f32, bf16 = jnp.float32, jnp.bfloat16
    W = FOX_HEADS * HEAD_DIM
    qbd = qbd_ref[0]
    s = [jnp.dot(qbd, pg[0, :W, :].astype(bf16), preferred_element_type=f32) for pg in pages]
    s.append(jnp.dot(qbd, knew_ref[0], preferred_element_type=f32))
    s = jnp.concatenate(s, axis=1)
    rowh = lax.broadcasted_iota(jnp.int32, (DEC_ROWS, 1), 0) // DEC_SEQ
    fk = fk_ref[0]
    fk_rows = jnp.where(rowh == 0, fk[0:1], jnp.where(rowh == 1, fk[1:2], jnp.where(rowh == 2, fk[2:3], fk[3:4])))
    col = lax.broadcasted_iota(jnp.int32, (DEC_ROWS, DEC_KEYS), 1)
    t = lax.broadcasted_iota(jnp.int32, (DEC_ROWS, DEC_KEYS), 0) % DEC_SEQ
    ok = (col < PAST_LEN) | (col - PAST_LEN <= t)
    s = jnp.where(ok, s + (fq_ref[0] - fk_rows), NEG_INF)
    p, l = _softmax_rows(s)
    pb = p.astype(bf16)
    o = lax.dot_general(pb[:, PAST_LEN:], vnew_ref[0], _NT, preferred_element_type=f32)
    for i, pg in enumerate(pages):
        o += lax.dot_general(pb[:, i * PAGE_SIZE:(i + 1) * PAGE_SIZE], pg[0, W:, :].astype(bf16), _NT,
                             preferred_element_type=f32)
    o = o / l
    lane_h = lax.broadcasted_iota(jnp.int32, (DEC_ROWS, W), 1) // HEAD_DIM
    o = jnp.where(lane_h == rowh, o, 0.0)
    o_ref[0] = o[0:8] + o[8:16] + o[16:24] + o[24:32]


def _pad_new_t(x):
    return jnp.pad(x.transpose(0, 2, 1), ((0, 0), (0, 0), (0, NEW_PAD - DEC_SEQ))).astype(jnp.bfloat16)


def _fox_decode(q, k_new, v_new, lf_new, cache_kv, cache_lf, page_table, layer):
    DB = q.shape[0]
    f32, bf16 = jnp.float32, jnp.bfloat16
    W = FOX_HEADS * HEAD_DIM
    eye = jnp.eye(FOX_HEADS, dtype=f32)
    qh = q.reshape(DB, DEC_SEQ, FOX_HEADS, HEAD_DIM).transpose(0, 2, 1, 3)
    qbd = (qh[:, :, :, None, :] * eye[None, :, None, :, None]).reshape(DB, DEC_ROWS, W).astype(bf16)
    lf_all = jnp.concatenate([cache_lf[layer][page_table].reshape(DB, PAST_LEN, FOX_HEADS).astype(f32), lf_new], axis=1)
    F = jnp.cumsum(lf_all, axis=1)
    fk = jnp.pad(F.transpose(0, 2, 1), ((0, 0), (0, 8 - FOX_HEADS), (0, DEC_KEYS - PAST_LEN - DEC_SEQ)))
    fq = F[:, PAST_LEN:].transpose(0, 2, 1).reshape(DB, DEC_ROWS, 1)
    per_seq = lambda r, w: pl.BlockSpec((1, r, w), lambda b, pt: (b, 0, 0))
    pages = _pages_token_minor(cache_kv)
    return pl.pallas_call(
        _fox_decode_kernel,
        grid_spec=pltpu.PrefetchScalarGridSpec(
            num_scalar_prefetch=1, grid=(DB,),
            in_specs=[per_seq(DEC_ROWS, W), per_seq(W, NEW_PAD), per_seq(W, NEW_PAD), per_seq(8, DEC_KEYS),
                      per_seq(DEC_ROWS, 1)] + _page_specs(2 * W, 0, layer, cache_kv.shape[1]),
            out_specs=per_seq(DEC_SEQ, W)),
        out_shape=jax.ShapeDtypeStruct((DB, DEC_SEQ, W), f32),
        compiler_params=pltpu.CompilerParams(dimension_semantics=("parallel",), vmem_limit_bytes=VMEM_LIMIT_BYTES),
        name="fox_decode",
    )(page_table.reshape(-1), qbd, _pad_new_t(k_new), _pad_new_t(v_new), fk, fq, *([pages] * N_PAGES))


def _nsa_decode_kernel(pt_ref, qs_ref, qf_ref, kcvc_ref, rnew_ref, wbuf_ref, wnew_ref, g_ref, e_ref, *refs):
    pages, o_ref = refs[:N_PAGES], refs[N_PAGES]
    f32, bf16 = jnp.float32, jnp.bfloat16
    H, T, HD = NSA_HEADS, DEC_SEQ, HEAD_DIM
    qs = qs_ref[0]
    row = lax.broadcasted_iota(jnp.int32, (DEC_ROWS, 1), 0)
    head = row // T
    slope = jnp.where(head == 0, NSA_SLOPES[0], jnp.where(head == 1, NSA_SLOPES[1],
                      jnp.where(head == 2, NSA_SLOPES[2], NSA_SLOPES[3]))).astype(f32)
    posq = PAST_LEN + row % T

    kcvc = kcvc_ref[0]
    n_cb = kcvc.shape[0]
    sc = lax.dot_general(qf_ref[0], kcvc, _NT, precision=_HI, preferred_element_type=f32)
    jb = lax.broadcasted_iota(jnp.int32, (DEC_ROWS, n_cb), 1)
    distc = posq - (jb * NSA_BLOCK + NSA_BLOCK - 1)
    okc = distc >= 0
    sc = jnp.where(okc, sc - slope * distc.astype(f32), NEG_INF)
    pc = jnp.exp(sc - jnp.max(sc, axis=-1, keepdims=True)) * okc.astype(f32)
    pc = pc / jnp.maximum(jnp.sum(pc, axis=-1, keepdims=True), TINY)
    o_cmp = jnp.dot(pc.astype(bf16), kcvc.astype(bf16), preferred_element_type=f32)[:, HD:]
    imp = pc[0:T] + pc[T:2 * T] + pc[2 * T:3 * T] + pc[3 * T:4 * T]
    imp = jnp.concatenate([imp, jnp.zeros((T, 128 - n_cb), f32)], axis=1)
    cur = (PAST_LEN + lax.broadcasted_iota(jnp.int32, (T, 1), 0)) // NSA_BLOCK
    msel = _select_blocks(imp, cur, NSA_TOPK).astype(bf16)

    mexp = jnp.dot(msel, e_ref[...], preferred_element_type=f32)
    col = lax.broadcasted_iota(jnp.int32, (T, DEC_KEYS), 1)
    tq = lax.broadcasted_iota(jnp.int32, (T, DEC_KEYS), 0)
    d = PAST_LEN + tq - col
    ok = (mexp > 0.5) & (d >= 0)
    kv = [pg[0].astype(bf16) for pg in pages] + [rnew_ref[0]]
    s = jnp.concatenate([jnp.dot(qs, x[:HD], preferred_element_type=f32) for x in kv], axis=1)
    s = jnp.where(_tile_rows(ok, H), s - slope * _tile_rows(d.astype(f32), H), NEG_INF)
    p, l = _softmax_rows(s)
    pb = p.astype(bf16)
    acc = jnp.zeros((DEC_ROWS, HD), f32)
    for i, x in enumerate(kv):
        acc += lax.dot_general(pb[:, i * PAGE_SIZE:(i + 1) * PAGE_SIZE], x[HD:], _NT, preferred_element_type=f32)
    o_sel = acc / l

    wb = wbuf_ref.shape[2]
    kvw = [wbuf_ref[0].astype(bf16), wnew_ref[0]]
    sw = jnp.concatenate([jnp.dot(qs, x[:HD], preferred_element_type=f32) for x in kvw], axis=1)
    colw = lax.broadcasted_iota(jnp.int32, (T, wb + NEW_PAD), 1)
    tw = lax.broadcasted_iota(jnp.int32, (T, wb + NEW_PAD), 0)
    dw = wb + tw - colw
    okw = (dw >= 0) & (dw < NSA_WINDOW)
    sw = jnp.where(_tile_rows(okw, H), sw - slope * _tile_rows(dw.astype(f32), H), NEG_INF)
    pw, lw = _softmax_rows(sw)
    pwb = pw.astype(bf16)
    accw = (lax.dot_general(pwb[:, :wb], kvw[0][HD:], _NT, preferred_element_type=f32)
            + lax.dot_general(pwb[:, wb:], kvw[1][HD:], _NT, preferred_element_type=f32))
    o_win = accw / lw

    g = _sigmoid(g_ref[0])
    gate = lambda c: jnp.concatenate([g[:, 3 * h + c:3 * h + c + 1] for h in range(H)], axis=0)
    o_ref[0] = gate(0) * o_cmp + gate(1) * o_sel + gate(2) * o_win


def _compress_pages_kernel(x_ref, wk_ref, wv_ref, o_ref):
    tm = x_ref.shape[0]
    f32 = jnp.float32
    acc_k = jnp.zeros((tm, 2 * HEAD_DIM), f32)
    acc_v = jnp.zeros((tm, 2 * HEAD_DIM), f32)
    pair = lambda r: jnp.concatenate([x_ref[:, r, :], x_ref[:, r + 1, :]], axis=1)
    for i in range(HEAD_DIM // 2):
        acc_k += jnp.dot(pair(2 * i), wk_ref[i], precision=_HI, preferred_element_type=f32)
        acc_v += jnp.dot(pair(HEAD_DIM + 2 * i), wv_ref[i], preferred_element_type=f32)
    o_ref[...] = jnp.concatenate([acc_k, acc_v], axis=1)


def _nsa_compress_pages(pages, w_ck, w_cv, layer, n_phys, tm=128):
    eye = jnp.eye(PAGE_SIZE // NSA_BLOCK, dtype=jnp.float32)
    big = lambda w: jnp.einsum('pde,bc->dbpce', w, eye).reshape(HEAD_DIM // 2, 2 * PAGE_SIZE, 2 * HEAD_DIM)
    wspec = pl.BlockSpec((HEAD_DIM // 2, 2 * PAGE_SIZE, 2 * HEAD_DIM), lambda i: (0, 0, 0))
    return pl.pallas_call(
        _compress_pages_kernel,
        grid=(n_phys // tm,),
        in_specs=[pl.BlockSpec((tm, 2 * HEAD_DIM, PAGE_SIZE), lambda i: (layer * (n_phys // tm) + i, 0, 0)),
                  wspec, wspec],
        out_specs=pl.BlockSpec((tm, 4 * HEAD_DIM), lambda i: (i, 0)),
        out_shape=jax.ShapeDtypeStruct((n_phys, 4 * HEAD_DIM), jnp.float32),
        compiler_params=pltpu.CompilerParams(dimension_semantics=("parallel",), vmem_limit_bytes=VMEM_LIMIT_BYTES),
        name="nsa_compress_pages",
    )(pages, big(w_ck), big(w_cv))


def _nsa_decode(nq, rows_new, win_new, gates, cache_rows, cache_win, page_table, w_ck, w_cv, layer):
    DB = nq.shape[0]
    f32, bf16 = jnp.float32, jnp.bfloat16
    HD = HEAD_DIM
    n_phys = cache_rows.shape[1]
    n_blk = PAGE_SIZE // NSA_BLOCK
    pages = _pages_token_minor(cache_rows)
    kcvc = _nsa_compress_pages(pages, w_ck, w_cv, layer, n_phys)[page_table]
    kcvc = kcvc.reshape(DB, N_PAGES, 2, n_blk, HD).transpose(0, 1, 3, 2, 4).reshape(DB, N_PAGES * n_blk, 2 * HD)
    stack = lambda x: x.reshape(DB, DEC_SEQ, NSA_HEADS, HD).transpose(0, 2, 1, 3).reshape(DB, DEC_ROWS, HD)
    qs = stack(nq).astype(bf16)
    qf = jnp.pad(stack(nq), ((0, 0), (0, 0), (0, HD))).astype(f32)
    win_t = cache_win.transpose(0, 1, 3, 4, 2).reshape(cache_win.shape[0] * DB, 2 * HD, cache_win.shape[2])
    colk = np.arange(DEC_KEYS)
    e = (np.arange(128)[:, None] == colk[None, :] // NSA_BLOCK) & (colk[None, :] < PAST_LEN + NSA_BLOCK)
    wb = cache_win.shape[2]
    per_seq = lambda r, w: pl.BlockSpec((1, r, w), lambda b, pt: (b, 0, 0))
    o = pl.pallas_call(
        _nsa_decode_kernel,
        grid_spec=pltpu.PrefetchScalarGridSpec(
            num_scalar_prefetch=1, grid=(DB,),
            in_specs=[per_seq(DEC_ROWS, HD), per_seq(DEC_ROWS, 2 * HD), per_seq(PAST_LEN // NSA_BLOCK, 2 * HD),
                      per_seq(2 * HD, NEW_PAD),
                      pl.BlockSpec((1, 2 * HD, wb), lambda b, pt: (layer * DB + b, 0, 0)),
                      per_seq(2 * HD, NEW_PAD), per_seq(DEC_SEQ, 128),
                      pl.BlockSpec((128, DEC_KEYS), lambda b, pt: (0, 0))]
                     + _page_specs(2 * HD, 1, layer, n_phys),
            out_specs=per_seq(DEC_ROWS, HD)),
        out_shape=jax.ShapeDtypeStruct((DB, DEC_ROWS, HD), f32),
        compiler_params=pltpu.CompilerParams(dimension_semantics=("parallel",), vmem_limit_bytes=VMEM_LIMIT_BYTES),
        name="nsa_decode",
    )(page_table.reshape(-1), qs, qf, kcvc, _pad_new_t(rows_new[..., 2 * HD:]), win_t,
      _pad_new_t(win_new), gates, jnp.asarray(e, bf16), *([pages] * N_PAGES))
    return o.reshape(DB, NSA_HEADS, DEC_SEQ, HD).transpose(0, 2, 1, 3).reshape(DB, DEC_SEQ, BRANCH_WIDTH)


def _stack_layers(states):
    return tuple(jnp.stack(list(a)) for a in zip(*states))


def kernel(x_prompt, x_sample, cache_nsa, cache_nsa_win, cache_fox_kv, cache_fox_logf,
           state_mlstm_c, state_mlstm_n, state_mlstm_m, page_table,
           ln_in_w, ln_in_b, w_in, b_in, nsa_w_ck, nsa_w_cv, mlstm_norm_w, w_branch, w_out,
           ln1_w, ln1_b, moe_w_group, moe_b_group, moe_w_expert, moe_b_expert,
           moe_w_gate, moe_w_up, moe_w_down, ln2_w, ln2_b):
    f32, bf16 = jnp.float32, jnp.bfloat16
    alpha = (2.0 * DEPTH) ** 0.25
    B, S, D = x_prompt.shape
    DB, T, _ = x_sample.shape
    TP, TS = B * S, DB * T
    HD = HEAD_DIM
    xp = _layer_norm_rows(x_prompt.reshape(TP, D), ln_in_w, ln_in_b)
    xs = _layer_norm_rows(x_sample.reshape(TS, D), ln_in_w, ln_in_b)
    cols = lambda a, c0, w: a[..., c0:c0 + w]
    colsb = lambda a, c0, w: a[..., c0 - Z_NQ:c0 - Z_NQ + w]
    zero_state = (jnp.zeros((B, MLSTM_HEADS, HD, HD), f32), jnp.zeros((B, MLSTM_HEADS, HD), f32),
                  jnp.zeros((B, MLSTM_HEADS), f32))
    new_p, new_s = [], []
    for l in range(DEPTH):
        zp2, zbp2 = _in_projection(xp, w_in[l], b_in[l])
        zs2, zbs2 = _in_projection(xs, w_in[l], b_in[l])
        zp, zbp = zp2.reshape(B, S, Z_WIDTH), zbp2.reshape(B, S, ZB_WIDTH)
        zs, zbs = zs2.reshape(DB, T, Z_WIDTH), zbs2.reshape(DB, T, ZB_WIDTH)
        small_p, small_s = cols(zp, Z_SMALL, 128), cols(zs, Z_SMALL, 128)

        o_nsa_p = _nsa_prompt(cols(zp, Z_NQ, 256), colsb(zbp, Z_NQ, 256), cols(zp, Z_ROWS, 256),
                              colsb(zbp, Z_ROWS, 256), colsb(zbp, Z_WIN, 128), small_p, nsa_w_ck[l], nsa_w_cv[l])
        o_nsa_s = _nsa_decode(cols(zs, Z_NQ, 256), colsb(zbs, Z_ROWS, 256), colsb(zbs, Z_WIN, 128), small_s,
                              cache_nsa, cache_nsa_win, page_table, nsa_w_ck[l], nsa_w_cv[l], l)

        def mlstm(q, k, v, og, small, state, L, nb, dt):
            return _mlstm(q, k, v, og, cols(small, SMALL_MI, MLSTM_HEADS), cols(small, SMALL_MF, MLSTM_HEADS),
                          mlstm_norm_w[l], *state, L, nb, dt)

        o_ml_p, st_ml_p = mlstm(colsb(zbp, Z_MQ, 256), colsb(zbp, Z_MK, 256), colsb(zbp, Z_MV, 256),
                                cols(zp, Z_MO, 256), small_p, zero_state, 128, B, bf16)
        o_ml_s, st_ml_s = mlstm(cols(zs, Z_MQ, 256), cols(zs, Z_MK, 256), cols(zs, Z_MV, 256), cols(zs, Z_MO, 256),
                                small_s, (state_mlstm_c[l], state_mlstm_n[l], state_mlstm_m[l]), T, 4, f32)

        lf_p = jax.nn.log_sigmoid(cols(small_p, SMALL_FF, FOX_HEADS))
        lf_s = jax.nn.log_sigmoid(cols(small_s, SMALL_FF, FOX_HEADS))
        o_fox_p = _fox_prompt_attn(zbp, jnp.cumsum(lf_p, axis=1))
        o_fox_s = _fox_decode(colsb(zbs, Z_FQ, 256), colsb(zbs, Z_FK, 256), colsb(zbs, Z_FV, 256), lf_s,
                              cache_fox_kv, cache_fox_logf, page_table, l)

        flat = lambda a: a.reshape(-1, a.shape[-1])
        moe_w = (moe_w_group[l], moe_b_group[l], moe_w_expert[l], moe_b_expert[l],
                 moe_w_gate[l], moe_w_up[l], moe_w_down[l], ln2_w[l], ln2_b[l], alpha)
        xp = _merge_ln(xp, flat(o_nsa_p), flat(o_ml_p), flat(o_fox_p), w_in[l], b_in[l],
                       w_branch[l], w_out[l], ln1_w[l], ln1_b[l], alpha)
        xp = _moe_ln(xp, *moe_w)
        xs = _merge_ln(xs, flat(o_nsa_s), flat(o_ml_s), flat(o_fox_s), w_in[l], b_in[l],
                       w_branch[l], w_out[l], ln1_w[l], ln1_b[l], alpha)
        xs = _moe_ln(xs, *moe_w)

        w_keep = min(NSA_WINDOW, S)
        new_p.append((cols(zp, Z_ROWS, 256).reshape(B, S, NSA_ROWS, HD),
                      cols(zp, Z_WIN, 128)[:, S - w_keep:].reshape(B, w_keep, 2, HD),
                      cols(zp, Z_FK, 512).reshape(B, S, 2, FOX_HEADS, HD), lf_p) + tuple(st_ml_p))
        win_new = cols(zs, Z_WIN, 128).reshape(DB, T, 2, HD).astype(cache_nsa_win.dtype)
        new_s.append((cols(zs, Z_ROWS, 256).reshape(DB, T, NSA_ROWS, HD),
                      win_new,
                      cols(zs, Z_FK, 512).reshape(DB, T, 2, FOX_HEADS, HD), lf_s) + tuple(st_ml_s))
    (p_nsa_rows, p_nsa_win, p_fox_kv, p_fox_logf, p_mlstm_c, p_mlstm_n, p_mlstm_m) = _stack_layers(new_p)
    (s_nsa_rows, s_win_new, s_fox_kv, s_fox_logf, s_mlstm_c, s_mlstm_n, s_mlstm_m) = _stack_layers(new_s)
    s_nsa_win = jnp.concatenate([cache_nsa_win[:, :, T:], s_win_new], axis=2)
    return (xp.reshape(B, S, D), xs.reshape(DB, T, D),
            p_nsa_rows, p_nsa_win, p_fox_kv, p_fox_logf, p_mlstm_c, p_mlstm_n, p_mlstm_m,
            s_nsa_rows, s_nsa_win, s_fox_kv, s_fox_logf, s_mlstm_c, s_mlstm_n, s_mlstm_m)
```

```python
import functools

import jax
import jax.numpy as jnp
import numpy as np
from jax import lax
from jax.experimental import pallas as pl
from jax.experimental.pallas import tpu as pltpu

D_MODEL = 1024
DEPTH = 2
DEC_SEQ = 8
PAST_LEN = 2048
PAGE_SIZE = 128

HEAD_DIM = 64
NSA_HEADS = 4
NSA_BLOCK = 64
NSA_TOPK = 16
NSA_WINDOW = 512
NSA_ROWS = 4
MLSTM_HEADS = 4
FOX_HEADS = 4
N_BRANCH = 3
BRANCH_WIDTH = NSA_HEADS * HEAD_DIM
N_GROUPS = 4
EXPERTS_PER_GROUP = 4
N_EXPERTS = N_GROUPS * EXPERTS_PER_GROUP
D_EXPERT = 256
LN_EPS = 1e-5
NEG_INF = -1e30
TINY = 1e-30
ATTN_SCALE = HEAD_DIM ** -0.5

IN_SPLITS = (
    NSA_HEADS * HEAD_DIM,
    6 * HEAD_DIM,
    NSA_HEADS * 3,
    MLSTM_HEADS * HEAD_DIM,
    MLSTM_HEADS * HEAD_DIM,
    MLSTM_HEADS * HEAD_DIM,
    MLSTM_HEADS * HEAD_DIM,
    MLSTM_HEADS,
    MLSTM_HEADS,
    FOX_HEADS * HEAD_DIM,
    FOX_HEADS * HEAD_DIM,
    FOX_HEADS * HEAD_DIM,
    FOX_HEADS,
    N_BRANCH * D_MODEL,
)


VMEM_LIMIT_BYTES = 48 * 1024 * 1024
NSA_SLOPES = tuple(2.0 ** (-8.0 * (h + 1) / NSA_HEADS) for h in range(NSA_HEADS))
_NT = (((1,), (1,)), ((), ()))
_HI = lax.Precision.HIGHEST


def _sigmoid(x):
    return 1.0 / (1.0 + jnp.exp(-x))


def _tile_rows(x, n):
    return jnp.concatenate([x] * n, axis=0)


def _compress_kernel(x_ref, w_ref, o_ref, acc_ref):
    k = pl.program_id(1)

    @pl.when(k == 0)
    def _():
        acc_ref[...] = jnp.zeros_like(acc_ref)

    acc_ref[...] += jnp.dot(x_ref[...], w_ref[...], preferred_element_type=jnp.float32, precision=_HI)

    @pl.when(k == pl.num_programs(1) - 1)
    def _():
        o_ref[...] = acc_ref[...]


def _compress_weights(w_ck, w_cv):
    z = jnp.zeros_like(w_ck)
    wk = jnp.stack([w_ck, z, z, z], axis=1)
    wv = jnp.stack([z, w_cv, z, z], axis=1)
    return jnp.concatenate([wk, wv], axis=-1).reshape(NSA_BLOCK * NSA_ROWS * HEAD_DIM, 2 * HEAD_DIM)


def _nsa_compress_blocks(blocks, w_big, tm=256, tk=2048):
    n, kdim = blocks.shape
    tm = min(tm, n)
    return pl.pallas_call(
        _compress_kernel,
        grid=(n // tm, kdim // tk),
        in_specs=[pl.BlockSpec((tm, tk), lambda i, k: (i, k)),
                  pl.BlockSpec((tk, 2 * HEAD_DIM), lambda i, k: (k, 0))],
        out_specs=pl.BlockSpec((tm, 2 * HEAD_DIM), lambda i, k: (i, 0)),
        out_shape=jax.ShapeDtypeStruct((n, 2 * HEAD_DIM), jnp.float32),
        scratch_shapes=[pltpu.VMEM((tm, 2 * HEAD_DIM), jnp.float32)],
        name="nsa_compress",
    )(blocks, w_big)


def _select_blocks(imp, cur, n_pick):
    q, n_sb = imp.shape
    jq = lax.broadcasted_iota(jnp.int32, (q, n_sb), 1)
    valid = jq <= cur
    forced = valid & ((jq == cur) | (jq == 0))
    work0 = jnp.where(forced, -jnp.inf, jnp.where(valid, imp, NEG_INF))

    jf = jq.astype(jnp.float32)

    def pick(_, carry):
        work, sel = carry
        mx = jnp.max(work, axis=-1, keepdims=True)
        hit = jf == jnp.min(jnp.where(work == mx, jf, float(n_sb)), axis=-1, keepdims=True)
        sel = jnp.where(hit & valid, 1.0, sel)
        work = jnp.where(hit, -jnp.inf, work)
        return work, sel

    _, sel = lax.fori_loop(0, n_pick - 2, pick, (work0, forced.astype(jnp.float32)), unroll=True)
    return sel


def _nsa_prompt_kernel(qs_ref, qf_ref, kc_ref, vc_ref, ksel_ref, vsel_ref, kwin_ref, vwin_ref, g_ref,
                       o_ref, flag_ref, *, tq, tk, tw):
    i = pl.program_id(1)
    f32, bf16 = jnp.float32, jnp.bfloat16
    H = NSA_HEADS
    R = H * tq
    qs = qs_ref[0].reshape(R, 2 * HEAD_DIM)
    qf = qf_ref[0].reshape(R, HEAD_DIM)
    row = lax.broadcasted_iota(jnp.int32, (R, 1), 0)
    head = row // tq
    slope = jnp.where(head == 0, NSA_SLOPES[0], jnp.where(head == 1, NSA_SLOPES[1],
                      jnp.where(head == 2, NSA_SLOPES[2], NSA_SLOPES[3]))).astype(f32)
    posq = i * tq + (row - head * tq)

    n_cb = kc_ref.shape[1]
    sc = lax.dot_general(qf, kc_ref[0], _NT, precision=_HI, preferred_element_type=f32)
    jb = lax.broadcasted_iota(jnp.int32, (R, n_cb), 1)
    distc = posq - (jb * NSA_BLOCK + NSA_BLOCK - 1)
    okc = distc >= 0
    sc = jnp.where(okc, sc - slope * distc.astype(f32), NEG_INF)
    pc = jnp.exp(sc - jnp.max(sc, axis=-1, keepdims=True)) * okc.astype(f32)
    pc = pc / jnp.maximum(jnp.sum(pc, axis=-1, keepdims=True), TINY)
    o_cmp = jnp.dot(pc.astype(bf16), vc_ref[0].astype(bf16), preferred_element_type=f32)
    imp = pc[0:tq] + pc[tq:2 * tq] + pc[2 * tq:3 * tq] + pc[3 * tq:4 * tq]

    pq = i * tq + lax.broadcasted_iota(jnp.int32, (tq, 1), 0)
    msel = _select_blocks(imp, pq // NSA_BLOCK, NSA_TOPK)

    bpt = tk // NSA_BLOCK
    blk_any = jnp.max(msel, axis=0, keepdims=True)
    for j in range(n_cb // bpt):
        flag_ref[j] = (jnp.max(blk_any[:, j * bpt:(j + 1) * bpt]) > 0.5).astype(jnp.int32)
    mbias = (NEG_INF * (1.0 - msel)).astype(bf16)

    rowpos = i * tq + lax.broadcasted_iota(jnp.int32, (tq, 1), 0)

    def attend(carry, s, v1):
        m, acc = carry
        m_new = jnp.maximum(m, jnp.max(s, axis=-1, keepdims=True))
        p = jnp.exp(s - m_new).astype(bf16)
        return m_new, jnp.exp(m - m_new) * acc + jnp.dot(p, v1, preferred_element_type=f32)

    init = (jnp.full((R, 1), NEG_INF, f32), jnp.zeros((R, 2 * HEAD_DIM), f32))

    def sel_tile(j, carry, causal):
        k0 = pl.multiple_of(j * tk, tk)
        s = lax.dot_general(qs, ksel_ref[0, pl.ds(k0, tk), :], _NT, preferred_element_type=f32)
        eb = (lax.broadcasted_iota(jnp.int32, (n_cb, tk), 0)
              == j * bpt + lax.broadcasted_iota(jnp.int32, (n_cb, tk), 1) // NSA_BLOCK)
        bias = jnp.dot(mbias, eb.astype(bf16), preferred_element_type=f32)
        if causal:
            d = rowpos - (k0 + lax.broadcasted_iota(jnp.int32, (tq, tk), 1))
            bias = jnp.where(d >= 0, bias, NEG_INF)
        return attend(carry, s + _tile_rows(bias, H), vsel_ref[0, pl.ds(k0, tk), :])

    def sel_body(j, carry):
        return lax.cond(flag_ref[j] > 0, lambda c: sel_tile(j, c, False), lambda c: c, carry)

    n_sel = (i * tq + tq - 1) // tk + 1
    carry = lax.fori_loop(0, n_sel - 1, sel_body, init)
    _, a_sel = sel_tile(n_sel - 1, carry, True)
    o_sel = a_sel[:, :HEAD_DIM] / a_sel[:, HEAD_DIM:]

    nw = NSA_WINDOW + tq
    w0 = pl.multiple_of(jnp.maximum(i * tq - NSA_WINDOW, 0), tw)
    sw = lax.dot_general(qs, kwin_ref[0, pl.ds(w0, nw), :], _NT, preferred_element_type=f32)
    dw = rowpos - (w0 + lax.broadcasted_iota(jnp.int32, (tq, nw), 1))
    okw = (dw >= 0) & (dw < NSA_WINDOW)
    sw = jnp.where(_tile_rows(okw, H), sw, NEG_INF)
    _, a_win = attend(init, sw, vwin_ref[0, pl.ds(w0, nw), :])
    o_win = a_win[:, :HEAD_DIM] / a_win[:, HEAD_DIM:]

    g = _sigmoid(g_ref[...])
    gate = lambda c: jnp.concatenate([g[:, 3 * h + c:3 * h + c + 1] for h in range(H)], axis=0)
    o = gate(0) * o_cmp + gate(1) * o_sel + gate(2) * o_win
    o_ref[0] = o.reshape(H, tq, HEAD_DIM)


def _heads_major(x, dtype):
    B, S, _ = x.shape
    return x.reshape(B, S, -1, HEAD_DIM).transpose(0, 2, 1, 3).astype(dtype)


def _nsa_prompt(nq, nqb, rows, rowsb, winb, gates, w_ck, w_cv, tq=256, tk=512, tw=128):
    B, S, _ = nq.shape
    bf16 = jnp.bfloat16
    kcvc = _nsa_compress_blocks(rows.reshape(B * S // NSA_BLOCK, -1), _compress_weights(w_ck, w_cv))
    kcvc = kcvc.reshape(B, S // NSA_BLOCK, 2 * HEAD_DIM)
    kc, vc = kcvc[..., :HEAD_DIM], kcvc[..., HEAD_DIM:]
    pos = jnp.arange(S)
    blk, off = (pos // NSA_BLOCK).astype(jnp.float32), (pos % NSA_BLOCK).astype(jnp.float32)
    one = jnp.ones((S,), jnp.float32)
    slopes = jnp.asarray(NSA_SLOPES, jnp.float32)[:, None]
    q_terms = jnp.stack([-slopes * NSA_BLOCK * blk, -slopes * off, slopes * NSA_BLOCK * one, slopes * one], axis=-1)
    k_terms = jnp.stack([one, one, blk, off], axis=-1)
    pad_terms = lambda t: jnp.pad(t, [(0, 0)] * (t.ndim - 1) + [(0, HEAD_DIM - 4)]).astype(bf16)
    with_k_terms = lambda k: jnp.concatenate([k, jnp.broadcast_to(pad_terms(k_terms), (B, S, HEAD_DIM))], axis=-1)
    qs = jnp.concatenate([_heads_major(nqb, bf16),
                          jnp.broadcast_to(pad_terms(q_terms), (B, NSA_HEADS, S, HEAD_DIM))], axis=-1)
    qf = _heads_major(nq, jnp.float32)
    rows, win = rowsb, winb
    col = lambda a, c: a[..., c * HEAD_DIM:(c + 1) * HEAD_DIM]
    with_ones = lambda v: jnp.concatenate([v, jnp.ones_like(v)], axis=-1)
    full = lambda n, w: pl.BlockSpec((1, n, w), lambda b, i: (b, 0, 0))
    qspec = pl.BlockSpec((1, NSA_HEADS, tq, HEAD_DIM), lambda b, i: (b, 0, i, 0))
    n_cb = S // NSA_BLOCK
    o = pl.pallas_call(
        functools.partial(_nsa_prompt_kernel, tq=tq, tk=tk, tw=tw),
        grid=(B, S // tq),
        in_specs=[pl.BlockSpec((1, NSA_HEADS, tq, 2 * HEAD_DIM), lambda b, i: (b, 0, i, 0)), qspec,
                  full(n_cb, HEAD_DIM), full(n_cb, HEAD_DIM), full(S, 2 * HEAD_DIM),
                  full(S, 2 * HEAD_DIM), full(S, 2 * HEAD_DIM), full(S, 2 * HEAD_DIM),
                  pl.BlockSpec((None, tq, 128), lambda b, i: (b, i, 0))],
        out_specs=qspec,
        out_shape=jax.ShapeDtypeStruct((B, NSA_HEADS, S, HEAD_DIM), jnp.float32),
        scratch_shapes=[pltpu.SMEM((S // tk,), jnp.int32)],
        compiler_params=pltpu.CompilerParams(dimension_semantics=("parallel", "arbitrary"),
                                             vmem_limit_bytes=VMEM_LIMIT_BYTES),
        name="nsa_prompt",
    )(qs, qf, kc, vc, with_k_terms(col(rows, 2)), with_ones(col(rows, 3)),
      with_k_terms(col(win, 0)), with_ones(col(win, 1)), gates)
    return o.transpose(0, 2, 1, 3).reshape(B, S, BRANCH_WIDTH)


def _fox_prompt_kernel(q_ref, k_ref, v_ref, fq_ref, fk_ref, o_ref, *, tq, t):
    i = pl.program_id(1)
    f32, bf16 = jnp.float32, jnp.bfloat16
    lane = lax.broadcasted_iota(jnp.int32, (1, 2 * HEAD_DIM), 1)
    low = lane < HEAD_DIM
    rc = lax.broadcasted_iota(jnp.int32, (tq, t), 0) - lax.broadcasted_iota(jnp.int32, (tq, t), 1)
    n_pairs = FOX_HEADS // 2
    pair_cols = [slice(hp * 2 * HEAD_DIM, (hp + 1) * 2 * HEAD_DIM) for hp in range(n_pairs)]
    q_h, fq_h = [], []
    for hp in range(n_pairs):
        q2 = q_ref[0, :, pair_cols[hp]]
        zq = jnp.zeros_like(q2)
        q_h += [jnp.where(low, q2, zq), jnp.where(low, zq, q2)]
        fq_h += [fq_ref[0, :, 2 * hp:2 * hp + 1], fq_ref[0, :, 2 * hp + 1:2 * hp + 2]]

    def body(j, carry, diag_offset):
        k0 = pl.multiple_of(j * t, t)
        fk = fk_ref[0, j]
        out = []
        for hp in range(n_pairs):
            k2 = k_ref[0, pl.ds(k0, t), pair_cols[hp]]
            v2 = v_ref[0, pl.ds(k0, t), pair_cols[hp]]
            one = jnp.ones_like(v2)
            v1 = (jnp.where(low, v2, one), jnp.where(low, one, v2))
            for x in range(2):
                h = 2 * hp + x
                m, acc = carry[h]
                s = lax.dot_general(q_h[h], k2, _NT, preferred_element_type=f32) + (fq_h[h] - fk[h:h + 1])
                if diag_offset is not None:
                    s = jnp.where(rc >= diag_offset, s, NEG_INF)
                m_new = jnp.maximum(m, jnp.max(s, axis=-1, keepdims=True))
                p = jnp.exp(s - m_new).astype(bf16)
                acc = jnp.exp(m - m_new) * acc + jnp.dot(p, v1[x], preferred_element_type=f32)
                out.append((m_new, acc))
        return tuple(out)

    init = tuple((jnp.full((tq, 1), NEG_INF, f32), jnp.zeros((tq, 2 * HEAD_DIM), f32)) for _ in range(FOX_HEADS))
    n_full = i * (tq // t)
    carry = lax.fori_loop(0, n_full, functools.partial(body, diag_offset=None), init)
    for jj in range(tq // t):
        carry = body(n_full + jj, carry, jj * t)
    for hp in range(n_pairs):
        acc_a, acc_b = carry[2 * hp][1], carry[2 * hp + 1][1]
        den = pltpu.roll(jnp.where(low, acc_b, acc_a), HEAD_DIM, axis=1)
        o_ref[0, :, pair_cols[hp]] = jnp.where(low, acc_a, acc_b) / den


def _fox_prompt_attn(zb, F, tq=1024, t=1024):
    B, S, _ = zb.shape
    W = FOX_HEADS * HEAD_DIM
    cq, ck, cv = ((c - Z_NQ) // W for c in (Z_FQ, Z_FK, Z_FV))
    fk = jnp.pad(F.transpose(0, 2, 1), ((0, 0), (0, 8 - FOX_HEADS), (0, 0)))
    fk = fk.reshape(B, 8, S // t, t).transpose(0, 2, 1, 3)
    return pl.pallas_call(
        functools.partial(_fox_prompt_kernel, tq=tq, t=t),
        grid=(B, S // tq),
        in_specs=[pl.BlockSpec((1, tq, W), lambda b, i: (b, i, cq)),
                  pl.BlockSpec((1, S, W), lambda b, i: (b, 0, ck)),
                  pl.BlockSpec((1, S, W), lambda b, i: (b, 0, cv)),
                  pl.BlockSpec((1, tq, FOX_HEADS), lambda b, i: (b, i, 0)),
                  pl.BlockSpec((1, S // t, 8, t), lambda b, i: (b, 0, 0, 0))],
        out_specs=pl.BlockSpec((1, tq, W), lambda b, i: (b, i, 0)),
        out_shape=jax.ShapeDtypeStruct((B, S, W), jnp.float32),
        compiler_params=pltpu.CompilerParams(dimension_semantics=("parallel", "arbitrary"),
                                             vmem_limit_bytes=VMEM_LIMIT_BYTES),
        name="fox_prompt",
    )(zb, zb, zb, F, fk)


def _log_sigmoid(x):
    return jnp.minimum(x, 0.0) - jnp.log1p(jnp.exp(-jnp.abs(x)))


def _mlstm_kernel(q_ref, k_ref, kt_ref, v_ref, og_ref, gc_ref, gr_ref, nw_ref, cn0_ref, m0_ref,
                  h_ref, cn_ref, m_ref, cn_s, m_s, *, nb, L):
    c = pl.program_id(1)
    f32 = jnp.float32
    W = 2 * HEAD_DIM
    n_pairs = MLSTM_HEADS // 2

    @pl.when(c == 0)
    def _():
        cn_s[...] = cn0_ref[...]
        m_s[...] = m0_ref[...]

    lane = lax.broadcasted_iota(jnp.int32, (1, W), 1)
    low = lane < HEAD_DIM
    ti = lax.broadcasted_iota(jnp.int32, (L, L), 0)
    si = lax.broadcasted_iota(jnp.int32, (L, L), 1)
    causal = si <= ti
    tri = causal.astype(f32)
    tri_t = (ti <= si).astype(f32)
    srow = lax.broadcasted_iota(jnp.int32, (W, 2 * W), 0)
    slane = lax.broadcasted_iota(jnp.int32, (W, 2 * W), 1)
    top = srow < HEAD_DIM
    keep_a = top & ((slane < HEAD_DIM) | (slane == W))
    keep_b = (~top) & (((slane >= HEAD_DIM) & (slane < W)) | (slane == W + 1))
    lane_w = lax.broadcasted_iota(jnp.int32, (1, W), 1)
    mdt = k_ref.dtype

    for b in range(nb):
        gcol = gc_ref[b]
        grow = gr_ref[b]
        bcol = jnp.dot(tri, _log_sigmoid(gcol), precision=_HI, preferred_element_type=f32)
        brow = jnp.dot(_log_sigmoid(grow), tri_t, precision=_HI, preferred_element_type=f32)
        for hp in range(n_pairs):
            cols = slice(hp * W, (hp + 1) * W)
            q2 = q_ref[b, :, cols]
            k2 = k_ref[b, :, cols]
            v2 = v_ref[b, :, cols]
            kt2 = kt_ref[b, cols, :]
            cn = cn_s[b, hp]
            r = jnp.dot(q2, cn.astype(mdt), preferred_element_type=f32)
            zq = jnp.zeros_like(q2)
            per_head = []
            for x in range(2):
                h = 2 * hp + x
                qx = jnp.where(low, q2, zq) if x == 0 else jnp.where(low, zq, q2)
                b_c = bcol[:, MLSTM_HEADS + h:MLSTM_HEADS + h + 1]
                b_r = brow[MLSTM_HEADS + h:MLSTM_HEADS + h + 1, :]
                ig_c = gcol[:, h:h + 1]
                ig_r = grow[h:h + 1, :]
                m_prev = m_s[b, 0:1, h:h + 1]
                dmat = jnp.where(causal, b_c - b_r + ig_r, NEG_INF)
                a_c = b_c + m_prev
                m_t = jnp.maximum(a_c, jnp.max(dmat, axis=-1, keepdims=True))
                wq = jnp.exp(dmat - m_t) * lax.dot_general(qx, k2, _NT, preferred_element_type=f32)
                inter = jnp.exp(a_c - m_t)
                wv = jnp.dot(wq.astype(mdt), v2, preferred_element_type=f32)
                den = inter * r[:, W + x:W + x + 1] + jnp.sum(wq, axis=-1, keepdims=True)
                den = jnp.maximum(jnp.abs(den), jnp.exp(-m_t))
                bl = b_c[L - 1:L, :]
                g_c = bl - b_c + ig_c
                m_new = jnp.maximum(bl + m_prev, jnp.max(g_c, axis=0, keepdims=True))
                ws = jnp.exp(g_c - m_new)
                decay = jnp.exp(bl + m_prev - m_new)
                aug = jnp.concatenate([v2.astype(f32) * ws, jnp.where(lane_w == x, ws, 0.0)], axis=1)
                u = jnp.dot(kt2, aug.astype(mdt), preferred_element_type=f32)
                per_head.append((inter, wv, den, decay, u))
                m_s[b, 0:1, h:h + 1] = m_new
            (ia, wva, dena, deca, ua), (ib, wvb, denb, decb, ub) = per_head
            num = jnp.where(low, ia * r[:, :W] + wva, ib * r[:, :W] + wvb)
            hid = num / jnp.where(low, dena, denb)
            cn_s[b, hp] = (jnp.where(top, deca, decb) * cn + jnp.where(keep_a, ua, 0.0)
                           + jnp.where(keep_b, ub, 0.0))
            hid = _sigmoid(og_ref[b, :, cols]) * hid
            mean = lambda t: jnp.where(low, jnp.sum(jnp.where(low, t, 0.0), axis=-1, keepdims=True),
                                       jnp.sum(jnp.where(low, 0.0, t), axis=-1, keepdims=True)) / HEAD_DIM
            mu = mean(hid)
            var = mean(jnp.square(hid - mu))
            h_ref[b, :, cols] = (hid - mu) * lax.rsqrt(var + LN_EPS) * nw_ref[:, cols]

    @pl.when(c == pl.num_programs(1) - 1)
    def _():
        cn_ref[...] = cn_s[...]
        m_ref[...] = m_s[...]


def _mlstm_state_pack(c, n, m):
    Bx = c.shape[0]
    HD, W = HEAD_DIM, 2 * HEAD_DIM
    cp = c.reshape(Bx, MLSTM_HEADS // 2, 2, HD, HD)
    np_ = n.reshape(Bx, MLSTM_HEADS // 2, 2, HD, 1)
    z = lambda w: jnp.zeros((Bx, MLSTM_HEADS // 2, HD, w), jnp.float32)
    top = jnp.concatenate([cp[:, :, 0], z(HD), np_[:, :, 0], z(W - 1)], axis=-1)
    bot = jnp.concatenate([z(HD), cp[:, :, 1], z(1), np_[:, :, 1], z(W - 2)], axis=-1)
    cn = jnp.concatenate([top, bot], axis=-2)
    m8 = jnp.pad(m[:, None, :], ((0, 0), (0, 7), (0, 128 - MLSTM_HEADS)))
    return cn, m8


def _mlstm_state_unpack(cn, m8):
    W = 2 * HEAD_DIM
    c = jnp.stack([cn[:, h // 2, (h % 2) * HEAD_DIM:(h % 2 + 1) * HEAD_DIM,
                      (h % 2) * HEAD_DIM:(h % 2 + 1) * HEAD_DIM] for h in range(MLSTM_HEADS)], axis=1)
    n = jnp.stack([cn[:, h // 2, (h % 2) * HEAD_DIM:(h % 2 + 1) * HEAD_DIM, W + h % 2]
                   for h in range(MLSTM_HEADS)], axis=1)
    return c, n, m8[:, 0, :MLSTM_HEADS]


def _mlstm(q, k, v, og, mi, mf, norm_w, c0, n0, m0, L, nb, mxu_dtype):
    Bx, S, W4 = q.shape
    f32 = jnp.float32
    gates = jnp.concatenate([mi, mf], axis=-1).astype(f32)
    gcol = jnp.pad(gates, ((0, 0), (0, 0), (0, 128 - 2 * MLSTM_HEADS)))
    grow = gates.transpose(0, 2, 1)
    cn0, m8 = _mlstm_state_pack(c0.astype(f32), n0.astype(f32), m0.astype(f32))
    tok = lambda w: pl.BlockSpec((nb, L, w), lambda b, c: (b, c, 0))
    st_cn = pl.BlockSpec((nb, MLSTM_HEADS // 2, 128, 256), lambda b, c: (b, 0, 0, 0))
    st_m = pl.BlockSpec((nb, 8, 128), lambda b, c: (b, 0, 0))
    h, cn, m8 = pl.pallas_call(
        functools.partial(_mlstm_kernel, nb=nb, L=L),
        grid=(Bx // nb, S // L),
        in_specs=[tok(W4), tok(W4), pl.BlockSpec((nb, W4, L), lambda b, c: (b, 0, c)), tok(W4), tok(W4),
                  tok(128), pl.BlockSpec((nb, 8, L), lambda b, c: (b, 0, c)),
                  pl.BlockSpec((1, W4), lambda b, c: (0, 0)), st_cn, st_m],
        out_specs=[tok(W4), st_cn, st_m],
        out_shape=[jax.ShapeDtypeStruct((Bx, S, W4), f32),
                   jax.ShapeDtypeStruct(cn0.shape, f32), jax.ShapeDtypeStruct(m8.shape, f32)],
        scratch_shapes=[pltpu.VMEM((nb, MLSTM_HEADS // 2, 128, 256), f32), pltpu.VMEM((nb, 8, 128), f32)],
        compiler_params=pltpu.CompilerParams(dimension_semantics=("parallel", "arbitrary"),
                                             vmem_limit_bytes=VMEM_LIMIT_BYTES),
        name="mlstm",
    )(q.astype(mxu_dtype), k.astype(mxu_dtype), k.astype(mxu_dtype).transpose(0, 2, 1), v.astype(mxu_dtype),
      og.astype(f32), gcol, grow, norm_w.astype(f32).reshape(1, W4), cn0, m8)
    return h, _mlstm_state_unpack(cn, m8)


def _ln(x, w, b):
    mu = jnp.mean(x, axis=-1, keepdims=True)
    var = jnp.mean(jnp.square(x - mu), axis=-1, keepdims=True)
    return (x - mu) * lax.rsqrt(var + LN_EPS) * w + b


def _ln_kernel(x_ref, w_ref, b_ref, o_ref):
    o_ref[...] = _ln(x_ref[...], w_ref[...], b_ref[...])


def _layer_norm_rows(x, w, b, tm=1024):
    T, D = x.shape
    vec = pl.BlockSpec((1, D), lambda i: (0, 0))
    return pl.pallas_call(
        _ln_kernel, grid=(T // tm,),
        in_specs=[pl.BlockSpec((tm, D), lambda i: (i, 0)), vec, vec],
        out_specs=pl.BlockSpec((tm, D), lambda i: (i, 0)),
        out_shape=jax.ShapeDtypeStruct((T, D), jnp.float32),
        name="layer_norm",
    )(x, w.reshape(1, D), b.reshape(1, D))


Z_NQ = 0
Z_ROWS = Z_NQ + 256
Z_MQ = 512
Z_MK = Z_MQ + 256
Z_MV = 1024
Z_MO = Z_MV + 256
Z_FQ = 1536
Z_WIN = Z_FQ + 256
Z_SMALL = Z_WIN + 128
Z_FK = 2048
Z_FV = Z_FK + 256
Z_WIDTH = 2560
Z_TILE = 512
ZB_WIDTH = Z_WIDTH
GM_SPLIT = 13
SMALL_MI, SMALL_MF, SMALL_FF = 12, 16, 20


def _z_column_order():
    starts = np.concatenate([[0], np.cumsum(IN_SPLITS)])
    seg = lambda i, lo=0, hi=None: np.arange(starts[i] + lo, starts[i] + (IN_SPLITS[i] if hi is None else hi))
    pad = lambda n: np.full((n,), -1)
    order = np.concatenate([
        seg(0), seg(1, 0, 256), seg(3), seg(4), seg(5), seg(6),
        seg(9), seg(1, 256, 384), seg(2), seg(7), seg(8), seg(12), pad(128 - 24), seg(10), seg(11)])
    assert order.shape == (Z_WIDTH,)
    return order


def _permute_in_proj(w_in, b_in):
    order = _z_column_order()
    valid = jnp.asarray(order >= 0)
    idx = jnp.asarray(np.maximum(order, 0))
    scale = np.ones((Z_WIDTH,), np.float32)
    for c0 in (Z_NQ, Z_MK, Z_FQ):
        scale[c0:c0 + 256] = ATTN_SCALE
    scale = jnp.asarray(scale)
    w = jnp.where(valid[None, :], w_in[:, idx], 0.0) * scale[None, :]
    b = jnp.where(valid, b_in[idx], 0.0) * scale
    return w, b


def _proj_kernel(x_ref, w_ref, wlo_ref, b_ref, o_ref, ob_ref, xh_s, xl_s, *, hi_tile):
    j = pl.program_id(1)
    f32 = jnp.float32
    _proj_tile(x_ref, w_ref, wlo_ref, b_ref, o_ref, xh_s, xl_s, j, hi_tile)
    ob_ref[...] = o_ref[...].astype(jnp.bfloat16)


def _proj_tile(x_ref, w_ref, wlo_ref, b_ref, o_ref, xh_s, xl_s, j, hi_tile):
    f32 = jnp.float32

    @pl.when(j == 0)
    def _():
        x = x_ref[...]
        xh = x.astype(jnp.bfloat16)
        xh_s[...] = xh
        xl_s[...] = (x - xh.astype(f32)).astype(jnp.bfloat16)

    @pl.when(j != hi_tile)
    def _():
        o_ref[...] = jnp.dot(xh_s[...], w_ref[...], preferred_element_type=f32) + b_ref[...]

    @pl.when(j == hi_tile)
    def _():
        acc = jnp.dot(xh_s[...], wlo_ref[...], preferred_element_type=f32)
        acc += jnp.dot(xl_s[...], w_ref[...], preferred_element_type=f32)
        acc += jnp.dot(xh_s[...], w_ref[...], preferred_element_type=f32)
        o_ref[...] = acc + b_ref[...]


def _in_projection(x, w_in, b_in, tm=1024):
    T, D = x.shape
    w, b = _permute_in_proj(w_in, b_in)
    wh = w.astype(jnp.bfloat16)
    hi_tile = Z_NQ // Z_TILE
    wlo = (w[:, Z_NQ:Z_NQ + Z_TILE] - wh[:, Z_NQ:Z_NQ + Z_TILE].astype(jnp.float32)).astype(jnp.bfloat16)
    return pl.pallas_call(
        functools.partial(_proj_kernel, hi_tile=hi_tile),
        grid=(T // tm, Z_WIDTH // Z_TILE),
        in_specs=[pl.BlockSpec((tm, D), lambda i, j: (i, 0)),
                  pl.BlockSpec((D, Z_TILE), lambda i, j: (0, j)),
                  pl.BlockSpec((D, Z_TILE), lambda i, j: (0, 0)),
                  pl.BlockSpec((1, Z_TILE), lambda i, j: (0, j))],
        out_specs=[pl.BlockSpec((tm, Z_TILE), lambda i, j: (i, j)),
                   pl.BlockSpec((tm, Z_TILE), lambda i, j: (i, j))],
        out_shape=[jax.ShapeDtypeStruct((T, Z_WIDTH), jnp.float32),
                   jax.ShapeDtypeStruct((T, ZB_WIDTH), jnp.bfloat16)],
        scratch_shapes=[pltpu.VMEM((tm, D), jnp.bfloat16), pltpu.VMEM((tm, D), jnp.bfloat16)],
        compiler_params=pltpu.CompilerParams(dimension_semantics=("parallel", "arbitrary"),
                                             vmem_limit_bytes=VMEM_LIMIT_BYTES),
        name="in_projection",
    )(x, wh, wlo, b.reshape(1, Z_WIDTH))


def _merge_kernel(x_ref, on_ref, om_ref, of_ref, wg_ref, bg_ref, wb_ref, wo_ref, lw_ref, lb_ref, o_ref, *, alpha):
    f32, bf16 = jnp.float32, jnp.bfloat16
    x = x_ref[...]
    xb = x.astype(bf16)
    y = None
    for m, br in enumerate((on_ref, om_ref, of_ref)):
        gcols = slice(m * D_MODEL, (m + 1) * D_MODEL)
        gm = jnp.dot(xb, wg_ref[:, gcols], preferred_element_type=f32) + bg_ref[:, gcols]
        proj = jnp.dot(br[...].astype(bf16), wb_ref[m], preferred_element_type=f32)
        term = _sigmoid(gm) * proj
        y = term if y is None else y + term
    mix = jnp.dot(y.astype(bf16), wo_ref[...], preferred_element_type=f32)
    o_ref[...] = _ln(alpha * x + mix, lw_ref[...], lb_ref[...])


def _merge_ln(x, o_nsa, o_ml, o_fox, w_in, b_in, w_branch, w_out, ln_w, ln_b, alpha, tm=512):
    T, D = x.shape
    g0 = int(np.sum(IN_SPLITS[:GM_SPLIT]))
    w_gm = w_in[:, g0:g0 + N_BRANCH * D].astype(jnp.bfloat16)
    b_gm = b_in[g0:g0 + N_BRANCH * D].reshape(1, N_BRANCH * D)
    tok = lambda w: pl.BlockSpec((tm, w), lambda i: (i, 0))
    vec = pl.BlockSpec((1, D), lambda i: (0, 0))
    return pl.pallas_call(
        functools.partial(_merge_kernel, alpha=alpha),
        grid=(T // tm,),
        in_specs=[tok(D), tok(BRANCH_WIDTH), tok(BRANCH_WIDTH), tok(BRANCH_WIDTH),
                  pl.BlockSpec((D, N_BRANCH * D), lambda i: (0, 0)),
                  pl.BlockSpec((1, N_BRANCH * D), lambda i: (0, 0)),
                  pl.BlockSpec((N_BRANCH, BRANCH_WIDTH, D), lambda i: (0, 0, 0)),
                  pl.BlockSpec((D, D), lambda i: (0, 0)), vec, vec],
        out_specs=tok(D),
        out_shape=jax.ShapeDtypeStruct((T, D), jnp.float32),
        compiler_params=pltpu.CompilerParams(dimension_semantics=("parallel",),
                                             vmem_limit_bytes=VMEM_LIMIT_BYTES),
        name="merge_ln",
    )(x, o_nsa, o_ml, o_fox, w_gm, b_gm, w_branch.astype(jnp.bfloat16), w_out.astype(jnp.bfloat16),
      ln_w.reshape(1, D), ln_b.reshape(1, D))


def _route(logits):
    tm = logits.shape[0]
    lane = lax.broadcasted_iota(jnp.int32, (tm, 128), 1)
    lanef = lane.astype(jnp.float32)
    big = 1e9
    is_g = lane < N_GROUPS
    lg = jnp.where(is_g, logits, -jnp.inf)
    eg = jnp.exp(lg - jnp.max(lg, axis=-1, keepdims=True))
    pg = eg / jnp.sum(eg, axis=-1, keepdims=True)
    g_val = jnp.max(pg, axis=-1, keepdims=True)
    g_idx = jnp.min(jnp.where(is_g & (pg == g_val), lanef, big), axis=-1, keepdims=True)
    e_lo = N_GROUPS + EXPERTS_PER_GROUP * g_idx
    in_grp = (lanef >= e_lo) & (lanef < e_lo + EXPERTS_PER_GROUP)
    le = jnp.where(in_grp, logits, -jnp.inf)
    ee = jnp.exp(le - jnp.max(le, axis=-1, keepdims=True))
    pe = ee / jnp.sum(ee, axis=-1, keepdims=True)
    v1 = jnp.max(pe, axis=-1, keepdims=True)
    i1 = jnp.min(jnp.where(in_grp & (pe == v1), lanef, big), axis=-1, keepdims=True)
    rest = in_grp & (lanef != i1)
    pe2 = jnp.where(rest, pe, -1.0)
    v2 = jnp.max(pe2, axis=-1, keepdims=True)
    i2 = jnp.min(jnp.where(rest & (pe2 == v2), lanef, big), axis=-1, keepdims=True)
    tot = v1 + v2
    return jnp.where(lanef == i1, g_val * v1 / tot, jnp.where(lanef == i2, g_val * v2 / tot, 0.0))


def _moe_kernel(x_ref, wr_ref, br_ref, wg_ref, wu_ref, wd_ref, lw_ref, lb_ref, o_ref, xb_s, gate_s, acc_s, *, alpha):
    e = pl.program_id(1)
    f32, bf16 = jnp.float32, jnp.bfloat16

    @pl.when(e == 0)
    def _():
        x = x_ref[...]
        xb_s[...] = x.astype(bf16)
        logits = jnp.dot(x, wr_ref[...], precision=_HI, preferred_element_type=f32) + br_ref[...]
        gate_s[...] = _route(logits)
        acc_s[...] = jnp.zeros_like(acc_s)

    lane = lax.broadcasted_iota(jnp.int32, gate_s.shape, 1)
    y = None
    for k in range(wg_ref.shape[0]):
        ex = e * wg_ref.shape[0] + k
        gate = jnp.sum(jnp.where(lane == N_GROUPS + ex, gate_s[...], 0.0), axis=-1, keepdims=True)
        g = jnp.dot(xb_s[...], wg_ref[k].astype(bf16), preferred_element_type=f32)
        u = jnp.dot(xb_s[...], wu_ref[k].astype(bf16), preferred_element_type=f32)
        h = (g * _sigmoid(g)) * u * gate
        yk = jnp.dot(h.astype(bf16), wd_ref[k].astype(bf16), preferred_element_type=f32)
        y = yk if y is None else y + yk
    acc_s[...] += y

    @pl.when(e == pl.num_programs(1) - 1)
    def _():
        o_ref[...] = _ln(alpha * x_ref[...] + acc_s[...], lw_ref[...], lb_ref[...])


def _moe_ln(x, w_group, b_group, w_expert, b_expert, w_gate, w_up, w_down, ln_w, ln_b, alpha, tm=1024, eps=2):
    T, D = x.shape
    bf16 = jnp.bfloat16
    n_r = N_GROUPS + N_EXPERTS
    wr = jnp.pad(jnp.concatenate([w_group, w_expert], axis=1), ((0, 0), (0, 128 - n_r)))
    br = jnp.pad(jnp.concatenate([b_group, b_expert]), (0, 128 - n_r)).reshape(1, 128)
    vec = pl.BlockSpec((1, D), lambda i, e: (0, 0))
    return pl.pallas_call(
        functools.partial(_moe_kernel, alpha=alpha),
        grid=(T // tm, N_EXPERTS // eps),
        in_specs=[pl.BlockSpec((tm, D), lambda i, e: (i, 0)),
                  pl.BlockSpec((D, 128), lambda i, e: (0, 0)),
                  pl.BlockSpec((1, 128), lambda i, e: (0, 0)),
                  pl.BlockSpec((eps, D, D_EXPERT), lambda i, e: (e, 0, 0)),
                  pl.BlockSpec((eps, D, D_EXPERT), lambda i, e: (e, 0, 0)),
                  pl.BlockSpec((eps, D_EXPERT, D), lambda i, e: (e, 0, 0)), vec, vec],
        out_specs=pl.BlockSpec((tm, D), lambda i, e: (i, 0)),
        out_shape=jax.ShapeDtypeStruct((T, D), jnp.float32),
        scratch_shapes=[pltpu.VMEM((tm, D), bf16), pltpu.VMEM((tm, 128), jnp.float32),
                        pltpu.VMEM((tm, D), jnp.float32)],
        compiler_params=pltpu.CompilerParams(dimension_semantics=("parallel", "arbitrary"),
                                             vmem_limit_bytes=VMEM_LIMIT_BYTES),
        name="moe_ln",
    )(x, wr, br, w_gate, w_up, w_down, ln_w.reshape(1, D), ln_b.reshape(1, D))


N_PAGES = PAST_LEN // PAGE_SIZE
NEW_PAD = 128
DEC_KEYS = PAST_LEN + NEW_PAD
DEC_ROWS = NSA_HEADS * DEC_SEQ


def _pages_token_minor(cache):
    nd = cache.ndim
    t = cache.transpose((0, 1) + tuple(range(3, nd)) + (2,))
    return t.reshape(cache.shape[0] * cache.shape[1], -1, cache.shape[2])


def _page_specs(rows, row_block, layer, n_phys):
    def spec(p):
        return pl.BlockSpec((1, rows, PAGE_SIZE),
                            lambda b, pt: (layer * n_phys + pt[b * N_PAGES + p], row_block, 0))
    return [spec(p) for p in range(N_PAGES)]


def _softmax_rows(s):
    m = jnp.max(s, axis=-1, keepdims=True)
    p = jnp.exp(s - m)
    return p, jnp.sum(p, axis=-1, keepdims=True)


def _fox_decode_kernel(pt_ref, qbd_ref, knew_ref, vnew_ref, fk_ref, fq_ref, *refs):
    pages, o_ref = refs[:N_PAGES], refs[N_PAGES]
    f32, bf16 = jnp.float32, jnp.bfloat16
    W = FOX_HEADS * HEAD_DIM
    qbd = qbd_ref[0]
    s = [jnp.dot(qbd, pg[0, :W, :].astype(bf16), preferred_element_type=f32) for pg in pages]
    s.append(jnp.dot(qbd, knew_ref[0], preferred_element_type=f32))
    s = jnp.concatenate(s, axis=1)
    rowh = lax.broadcasted_iota(jnp.int32, (DEC_ROWS, 1), 0) // DEC_SEQ
    fk = fk_ref[0]
    fk_rows = jnp.where(rowh == 0, fk[0:1], jnp.where(rowh == 1, fk[1:2], jnp.where(rowh == 2, fk[2:3], fk[3:4])))
    col = lax.broadcasted_iota(jnp.int32, (DEC_ROWS, DEC_KEYS), 1)
    t = lax.broadcasted_iota(jnp.int32, (DEC_ROWS, DEC_KEYS), 0) % DEC_SEQ
    ok = (col < PAST_LEN) | (col - PAST_LEN <= t)
    s = jnp.where(ok, s + (fq_ref[0] - fk_rows), NEG_INF)
    p, l = _softmax_rows(s)
    pb = p.astype(bf16)
    o = lax.dot_general(pb[:, PAST_LEN:], vnew_ref[0], _NT, preferred_element_type=f32)
    for i, pg in enumerate(pages):
        o += lax.dot_general(pb[:, i * PAGE_SIZE:(i + 1) * PAGE_SIZE], pg[0, W:, :].astype(bf16), _NT,
                             preferred_element_type=f32)
    o = o / l
    lane_h = lax.broadcasted_iota(jnp.int32, (DEC_ROWS, W), 1) // HEAD_DIM
    o = jnp.where(lane_h == rowh, o, 0.0)
    o_ref[0] = o[0:8] + o[8:16] + o[16:24] + o[24:32]


def _pad_new_t(x):
    return jnp.pad(x.transpose(0, 2, 1), ((0, 0), (0, 0), (0, NEW_PAD - DEC_SEQ))).astype(jnp.bfloat16)


def _fox_decode(q, k_new, v_new, lf_new, cache_kv, cache_lf, page_table, layer):
    DB = q.shape[0]
    f32, bf16 = jnp.float32, jnp.bfloat16
    W = FOX_HEADS * HEAD_DIM
    eye = jnp.eye(FOX_HEADS, dtype=f32)
    qh = q.reshape(DB, DEC_SEQ, FOX_HEADS, HEAD_DIM).transpose(0, 2, 1, 3)
    qbd = (qh[:, :, :, None, :] * eye[None, :, None, :, None]).reshape(DB, DEC_ROWS, W).astype(bf16)
    lf_all = jnp.concatenate([cache_lf[layer][page_table].reshape(DB, PAST_LEN, FOX_HEADS).astype(f32), lf_new], axis=1)
    F = jnp.cumsum(lf_all, axis=1)
    fk = jnp.pad(F.transpose(0, 2, 1), ((0, 0), (0, 8 - FOX_HEADS), (0, DEC_KEYS - PAST_LEN - DEC_SEQ)))
    fq = F[:, PAST_LEN:].transpose(0, 2, 1).reshape(DB, DEC_ROWS, 1)
    per_seq = lambda r, w: pl.BlockSpec((1, r, w), lambda b, pt: (b, 0, 0))
    pages = _pages_token_minor(cache_kv)
    return pl.pallas_call(
        _fox_decode_kernel,
        grid_spec=pltpu.PrefetchScalarGridSpec(
            num_scalar_prefetch=1, grid=(DB,),
            in_specs=[per_seq(DEC_ROWS, W), per_seq(W, NEW_PAD), per_seq(W, NEW_PAD), per_seq(8, DEC_KEYS),
                      per_seq(DEC_ROWS, 1)] + _page_specs(2 * W, 0, layer, cache_kv.shape[1]),
            out_specs=per_seq(DEC_SEQ, W)),
        out_shape=jax.ShapeDtypeStruct((DB, DEC_SEQ, W), f32),
        compiler_params=pltpu.CompilerParams(dimension_semantics=("parallel",), vmem_limit_bytes=VMEM_LIMIT_BYTES),
        name="fox_decode",
    )(page_table.reshape(-1), qbd, _pad_new_t(k_new), _pad_new_t(v_new), fk, fq, *([pages] * N_PAGES))


def _nsa_decode_kernel(pt_ref, qs_ref, qf_ref, kcvc_ref, rnew_ref, wbuf_ref, wnew_ref, g_ref, e_ref, *refs):
    pages, o_ref = refs[:N_PAGES], refs[N_PAGES]
    f32, bf16 = jnp.float32, jnp.bfloat16
    H, T, HD = NSA_HEADS, DEC_SEQ, HEAD_DIM
    qs = qs_ref[0]
    row = lax.broadcasted_iota(jnp.int32, (DEC_ROWS, 1), 0)
    head = row // T
    slope = jnp.where(head == 0, NSA_SLOPES[0], jnp.where(head == 1, NSA_SLOPES[1],
                      jnp.where(head == 2, NSA_SLOPES[2], NSA_SLOPES[3]))).astype(f32)
    posq = PAST_LEN + row % T

    kcvc = kcvc_ref[0]
    n_cb = kcvc.shape[0]
    sc = lax.dot_general(qf_ref[0], kcvc, _NT, precision=_HI, preferred_element_type=f32)
    jb = lax.broadcasted_iota(jnp.int32, (DEC_ROWS, n_cb), 1)
    distc = posq - (jb * NSA_BLOCK + NSA_BLOCK - 1)
    okc = distc >= 0
    sc = jnp.where(okc, sc - slope * distc.astype(f32), NEG_INF)
    pc = jnp.exp(sc - jnp.max(sc, axis=-1, keepdims=True)) * okc.astype(f32)
    pc = pc / jnp.maximum(jnp.sum(pc, axis=-1, keepdims=True), TINY)
    o_cmp = jnp.dot(pc.astype(bf16), kcvc.astype(bf16), preferred_element_type=f32)[:, HD:]
    imp = pc[0:T] + pc[T:2 * T] + pc[2 * T:3 * T] + pc[3 * T:4 * T]
    imp = jnp.concatenate([imp, jnp.zeros((T, 128 - n_cb), f32)], axis=1)
    cur = (PAST_LEN + lax.broadcasted_iota(jnp.int32, (T, 1), 0)) // NSA_BLOCK
    msel = _select_blocks(imp, cur, NSA_TOPK).astype(bf16)

    mexp = jnp.dot(msel, e_ref[...], preferred_element_type=f32)
    col = lax.broadcasted_iota(jnp.int32, (T, DEC_KEYS), 1)
    tq = lax.broadcasted_iota(jnp.int32, (T, DEC_KEYS), 0)
    d = PAST_LEN + tq - col
    ok = (mexp > 0.5) & (d >= 0)
    kv = [pg[0].astype(bf16) for pg in pages] + [rnew_ref[0]]
    s = jnp.concatenate([jnp.dot(qs, x[:HD], preferred_element_type=f32) for x in kv], axis=1)
    s = jnp.where(_tile_rows(ok, H), s - slope * _tile_rows(d.astype(f32), H), NEG_INF)
    p, l = _softmax_rows(s)
    pb = p.astype(bf16)
    acc = jnp.zeros((DEC_ROWS, HD), f32)
    for i, x in enumerate(kv):
        acc += lax.dot_general(pb[:, i * PAGE_SIZE:(i + 1) * PAGE_SIZE], x[HD:], _NT, preferred_element_type=f32)
    o_sel = acc / l

    wb = wbuf_ref.shape[2]
    kvw = [wbuf_ref[0].astype(bf16), wnew_ref[0]]
    sw = jnp.concatenate([jnp.dot(qs, x[:HD], preferred_element_type=f32) for x in kvw], axis=1)
    colw = lax.broadcasted_iota(jnp.int32, (T, wb + NEW_PAD), 1)
    tw = lax.broadcasted_iota(jnp.int32, (T, wb + NEW_PAD), 0)
    dw = wb + tw - colw
    okw = (dw >= 0) & (dw < NSA_WINDOW)
    sw = jnp.where(_tile_rows(okw, H), sw - slope * _tile_rows(dw.astype(f32), H), NEG_INF)
    pw, lw = _softmax_rows(sw)
    pwb = pw.astype(bf16)
    accw = (lax.dot_general(pwb[:, :wb], kvw[0][HD:], _NT, preferred_element_type=f32)
            + lax.dot_general(pwb[:, wb:], kvw[1][HD:], _NT, preferred_element_type=f32))
    o_win = accw / lw

    g = _sigmoid(g_ref[0])
    gate = lambda c: jnp.concatenate([g[:, 3 * h + c:3 * h + c + 1] for h in range(H)], axis=0)
    o_ref[0] = gate(0) * o_cmp + gate(1) * o_sel + gate(2) * o_win


def _compress_pages_kernel(x_ref, wk_ref, wv_ref, o_ref):
    tm = x_ref.shape[0]
    f32 = jnp.float32
    acc_k = jnp.zeros((tm, 2 * HEAD_DIM), f32)
    acc_v = jnp.zeros((tm, 2 * HEAD_DIM), f32)
    pair = lambda r: jnp.concatenate([x_ref[:, r, :], x_ref[:, r + 1, :]], axis=1)
    for i in range(HEAD_DIM // 2):
        acc_k += jnp.dot(pair(2 * i), wk_ref[i], precision=_HI, preferred_element_type=f32)
        acc_v += jnp.dot(pair(HEAD_DIM + 2 * i), wv_ref[i], preferred_element_type=f32)
    o_ref[...] = jnp.concatenate([acc_k, acc_v], axis=1)


def _nsa_compress_pages(pages, w_ck, w_cv, layer, n_phys, tm=128):
    eye = jnp.eye(PAGE_SIZE // NSA_BLOCK, dtype=jnp.float32)
    big = lambda w: jnp.einsum('pde,bc->dbpce', w, eye).reshape(HEAD_DIM // 2, 2 * PAGE_SIZE, 2 * HEAD_DIM)
    wspec = pl.BlockSpec((HEAD_DIM // 2, 2 * PAGE_SIZE, 2 * HEAD_DIM), lambda i: (0, 0, 0))
    return pl.pallas_call(
        _compress_pages_kernel,
        grid=(n_phys // tm,),
        in_specs=[pl.BlockSpec((tm, 2 * HEAD_DIM, PAGE_SIZE), lambda i: (layer * (n_phys // tm) + i, 0, 0)),
                  wspec, wspec],
        out_specs=pl.BlockSpec((tm, 4 * HEAD_DIM), lambda i: (i, 0)),
        out_shape=jax.ShapeDtypeStruct((n_phys, 4 * HEAD_DIM), jnp.float32),
        compiler_params=pltpu.CompilerParams(dimension_semantics=("parallel",), vmem_limit_bytes=VMEM_LIMIT_BYTES),
        name="nsa_compress_pages",
    )(pages, big(w_ck), big(w_cv))


def _nsa_decode(nq, rows_new, win_new, gates, cache_rows, cache_win, page_table, w_ck, w_cv, layer):
    DB = nq.shape[0]
    f32, bf16 = jnp.float32, jnp.bfloat16
    HD = HEAD_DIM
    n_phys = cache_rows.shape[1]
    n_blk = PAGE_SIZE // NSA_BLOCK
    pages = _pages_token_minor(cache_rows)
    kcvc = _nsa_compress_pages(pages, w_ck, w_cv, layer, n_phys)[page_table]
    kcvc = kcvc.reshape(DB, N_PAGES, 2, n_blk, HD).transpose(0, 1, 3, 2, 4).reshape(DB, N_PAGES * n_blk, 2 * HD)
    stack = lambda x: x.reshape(DB, DEC_SEQ, NSA_HEADS, HD).transpose(0, 2, 1, 3).reshape(DB, DEC_ROWS, HD)
    qs = stack(nq).astype(bf16)
    qf = jnp.pad(stack(nq), ((0, 0), (0, 0), (0, HD))).astype(f32)
    win_t = cache_win.transpose(0, 1, 3, 4, 2).reshape(cache_win.shape[0] * DB, 2 * HD, cache_win.shape[2])
    colk = np.arange(DEC_KEYS)
    e = (np.arange(128)[:, None] == colk[None, :] // NSA_BLOCK) & (colk[None, :] < PAST_LEN + NSA_BLOCK)
    wb = cache_win.shape[2]
    per_seq = lambda r, w: pl.BlockSpec((1, r, w), lambda b, pt: (b, 0, 0))
    o = pl.pallas_call(
        _nsa_decode_kernel,
        grid_spec=pltpu.PrefetchScalarGridSpec(
            num_scalar_prefetch=1, grid=(DB,),
            in_specs=[per_seq(DEC_ROWS, HD), per_seq(DEC_ROWS, 2 * HD), per_seq(PAST_LEN // NSA_BLOCK, 2 * HD),
                      per_seq(2 * HD, NEW_PAD),
                      pl.BlockSpec((1, 2 * HD, wb), lambda b, pt: (layer * DB + b, 0, 0)),
                      per_seq(2 * HD, NEW_PAD), per_seq(DEC_SEQ, 128),
                      pl.BlockSpec((128, DEC_KEYS), lambda b, pt: (0, 0))]
                     + _page_specs(2 * HD, 1, layer, n_phys),
            out_specs=per_seq(DEC_ROWS, HD)),
        out_shape=jax.ShapeDtypeStruct((DB, DEC_ROWS, HD), f32),
        compiler_params=pltpu.CompilerParams(dimension_semantics=("parallel",), vmem_limit_bytes=VMEM_LIMIT_BYTES),
        name="nsa_decode",
    )(page_table.reshape(-1), qs, qf, kcvc, _pad_new_t(rows_new[..., 2 * HD:]), win_t,
      _pad_new_t(win_new), gates, jnp.asarray(e, bf16), *([pages] * N_PAGES))
    return o.reshape(DB, NSA_HEADS, DEC_SEQ, HD).transpose(0, 2, 1, 3).reshape(DB, DEC_SEQ, BRANCH_WIDTH)


def _stack_layers(states):
    return tuple(jnp.stack(list(a)) for a in zip(*states))


def kernel(x_prompt, x_sample, cache_nsa, cache_nsa_win, cache_fox_kv, cache_fox_logf,
           state_mlstm_c, state_mlstm_n, state_mlstm_m, page_table,
           ln_in_w, ln_in_b, w_in, b_in, nsa_w_ck, nsa_w_cv, mlstm_norm_w, w_branch, w_out,
           ln1_w, ln1_b, moe_w_group, moe_b_group, moe_w_expert, moe_b_expert,
           moe_w_gate, moe_w_up, moe_w_down, ln2_w, ln2_b):
    f32, bf16 = jnp.float32, jnp.bfloat16
    alpha = (2.0 * DEPTH) ** 0.25
    B, S, D = x_prompt.shape
    DB, T, _ = x_sample.shape
    TP, TS = B * S, DB * T
    HD = HEAD_DIM
    xp = _layer_norm_rows(x_prompt.reshape(TP, D), ln_in_w, ln_in_b)
    xs = _layer_norm_rows(x_sample.reshape(TS, D), ln_in_w, ln_in_b)
    cols = lambda a, c0, w: a[..., c0:c0 + w]
    colsb = lambda a, c0, w: a[..., c0 - Z_NQ:c0 - Z_NQ + w]
    zero_state = (jnp.zeros((B, MLSTM_HEADS, HD, HD), f32), jnp.zeros((B, MLSTM_HEADS, HD), f32),
                  jnp.zeros((B, MLSTM_HEADS), f32))
    new_p, new_s = [], []
    for l in range(DEPTH):
        zp2, zbp2 = _in_projection(xp, w_in[l], b_in[l])
        zs2, zbs2 = _in_projection(xs, w_in[l], b_in[l])
        zp, zbp = zp2.reshape(B, S, Z_WIDTH), zbp2.reshape(B, S, ZB_WIDTH)
        zs, zbs = zs2.reshape(DB, T, Z_WIDTH), zbs2.reshape(DB, T, ZB_WIDTH)
        small_p, small_s = cols(zp, Z_SMALL, 128), cols(zs, Z_SMALL, 128)

        o_nsa_p = _nsa_prompt(cols(zp, Z_NQ, 256), colsb(zbp, Z_NQ, 256), cols(zp, Z_ROWS, 256),
                              colsb(zbp, Z_ROWS, 256), colsb(zbp, Z_WIN, 128), small_p, nsa_w_ck[l], nsa_w_cv[l])
        o_nsa_s = _nsa_decode(cols(zs, Z_NQ, 256), colsb(zbs, Z_ROWS, 256), colsb(zbs, Z_WIN, 128), small_s,
                              cache_nsa, cache_nsa_win, page_table, nsa_w_ck[l], nsa_w_cv[l], l)

        def mlstm(q, k, v, og, small, state, L, nb, dt):
            return _mlstm(q, k, v, og, cols(small, SMALL_MI, MLSTM_HEADS), cols(small, SMALL_MF, MLSTM_HEADS),
                          mlstm_norm_w[l], *state, L, nb, dt)

        o_ml_p, st_ml_p = mlstm(colsb(zbp, Z_MQ, 256), colsb(zbp, Z_MK, 256), colsb(zbp, Z_MV, 256),
                                cols(zp, Z_MO, 256), small_p, zero_state, 128, B, bf16)
        o_ml_s, st_ml_s = mlstm(cols(zs, Z_MQ, 256), cols(zs, Z_MK, 256), cols(zs, Z_MV, 256), cols(zs, Z_MO, 256),
                                small_s, (state_mlstm_c[l], state_mlstm_n[l], state_mlstm_m[l]), T, 4, f32)

        lf_p = jax.nn.log_sigmoid(cols(small_p, SMALL_FF, FOX_HEADS))
        lf_s = jax.nn.log_sigmoid(cols(small_s, SMALL_FF, FOX_HEADS))
        o_fox_p = _fox_prompt_attn(zbp, jnp.cumsum(lf_p, axis=1))
        o_fox_s = _fox_decode(colsb(zbs, Z_FQ, 256), colsb(zbs, Z_FK, 256), colsb(zbs, Z_FV, 256), lf_s,
                              cache_fox_kv, cache_fox_logf, page_table, l)

        flat = lambda a: a.reshape(-1, a.shape[-1])
        moe_w = (moe_w_group[l], moe_b_group[l], moe_w_expert[l], moe_b_expert[l],
                 moe_w_gate[l], moe_w_up[l], moe_w_down[l], ln2_w[l], ln2_b[l], alpha)
        xp = _merge_ln(xp, flat(o_nsa_p), flat(o_ml_p), flat(o_fox_p), w_in[l], b_in[l],
                       w_branch[l], w_out[l], ln1_w[l], ln1_b[l], alpha)
        xp = _moe_ln(xp, *moe_w)
        xs = _merge_ln(xs, flat(o_nsa_s), flat(o_ml_s), flat(o_fox_s), w_in[l], b_in[l],
                       w_branch[l], w_out[l], ln1_w[l], ln1_b[l], alpha)
        xs = _moe_ln(xs, *moe_w)

        w_keep = min(NSA_WINDOW, S)
        new_p.append((cols(zp, Z_ROWS, 256).reshape(B, S, NSA_ROWS, HD),
                      cols(zp, Z_WIN, 128)[:, S - w_keep:].reshape(B, w_keep, 2, HD),
                      cols(zp, Z_FK, 512).reshape(B, S, 2, FOX_HEADS, HD), lf_p) + tuple(st_ml_p))
        win_new = cols(zs, Z_WIN, 128).reshape(DB, T, 2, HD).astype(cache_nsa_win.dtype)
        new_s.append((cols(zs, Z_ROWS, 256).reshape(DB, T, NSA_ROWS, HD),
                      win_new,
                      cols(zs, Z_FK, 512).reshape(DB, T, 2, FOX_HEADS, HD), lf_s) + tuple(st_ml_s))
    (p_nsa_rows, p_nsa_win, p_fox_kv, p_fox_logf, p_mlstm_c, p_mlstm_n, p_mlstm_m) = _stack_layers(new_p)
    (s_nsa_rows, s_win_new, s_fox_kv, s_fox_logf, s_mlstm_c, s_mlstm_n, s_mlstm_m) = _stack_layers(new_s)
    s_nsa_win = jnp.concatenate([cache_nsa_win[:, :, T:], s_win_new], axis=2)
    return (xp.reshape(B, S, D), xs.reshape(DB, T, D),
            p_nsa_rows, p_nsa_win, p_fox_kv, p_fox_logf, p_mlstm_c, p_mlstm_n, p_mlstm_m,
            s_nsa_rows, s_nsa_win, s_fox_kv, s_fox_logf, s_mlstm_c, s_mlstm_n, s_mlstm_m)
```

```python
import functools

import jax
import jax.numpy as jnp
import numpy as np
from jax import lax
from jax.experimental import pallas as pl
from jax.experimental.pallas import tpu as pltpu

D_MODEL = 1024
DEPTH = 2
DEC_SEQ = 8
PAST_LEN = 2048
PAGE_SIZE = 128

HEAD_DIM = 64
NSA_HEADS = 4
NSA_BLOCK = 64
NSA_TOPK = 16
NSA_WINDOW = 512
NSA_ROWS = 4
MLSTM_HEADS = 4
FOX_HEADS = 4
N_BRANCH = 3
BRANCH_WIDTH = NSA_HEADS * HEAD_DIM
N_GROUPS = 4
EXPERTS_PER_GROUP = 4
N_EXPERTS = N_GROUPS * EXPERTS_PER_GROUP
D_EXPERT = 256
LN_EPS = 1e-5
NEG_INF = -1e30
TINY = 1e-30
ATTN_SCALE = HEAD_DIM ** -0.5

IN_SPLITS = (
    NSA_HEADS * HEAD_DIM,
    6 * HEAD_DIM,
    NSA_HEADS * 3,
    MLSTM_HEADS * HEAD_DIM,
    MLSTM_HEADS * HEAD_DIM,
    MLSTM_HEADS * HEAD_DIM,
    MLSTM_HEADS * HEAD_DIM,
    MLSTM_HEADS,
    MLSTM_HEADS,
    FOX_HEADS * HEAD_DIM,
    FOX_HEADS * HEAD_DIM,
    FOX_HEADS * HEAD_DIM,
    FOX_HEADS,
    N_BRANCH * D_MODEL,
)


VMEM_LIMIT_BYTES = 48 * 1024 * 1024
NSA_SLOPES = tuple(2.0 ** (-8.0 * (h + 1) / NSA_HEADS) for h in range(NSA_HEADS))
_NT = (((1,), (1,)), ((), ()))
_HI = lax.Precision.HIGHEST


def _sigmoid(x):
    return 1.0 / (1.0 + jnp.exp(-x))


def _tile_rows(x, n):
    return jnp.concatenate([x] * n, axis=0)


def _compress_kernel(x_ref, w_ref, o_ref, acc_ref):
    k = pl.program_id(1)

    @pl.when(k == 0)
    def _():
        acc_ref[...] = jnp.zeros_like(acc_ref)

    acc_ref[...] += jnp.dot(x_ref[...], w_ref[...], preferred_element_type=jnp.float32, precision=_HI)

    @pl.when(k == pl.num_programs(1) - 1)
    def _():
        o_ref[...] = acc_ref[...]


def _compress_weights(w_ck, w_cv):
    z = jnp.zeros_like(w_ck)
    wk = jnp.stack([w_ck, z, z, z], axis=1)
    wv = jnp.stack([z, w_cv, z, z], axis=1)
    return jnp.concatenate([wk, wv], axis=-1).reshape(NSA_BLOCK * NSA_ROWS * HEAD_DIM, 2 * HEAD_DIM)


def _nsa_compress_blocks(blocks, w_big, tm=256, tk=2048):
    n, kdim = blocks.shape
    tm = min(tm, n)
    return pl.pallas_call(
        _compress_kernel,
        grid=(n // tm, kdim // tk),
        in_specs=[pl.BlockSpec((tm, tk), lambda i, k: (i, k)),
                  pl.BlockSpec((tk, 2 * HEAD_DIM), lambda i, k: (k, 0))],
        out_specs=pl.BlockSpec((tm, 2 * HEAD_DIM), lambda i, k: (i, 0)),
        out_shape=jax.ShapeDtypeStruct((n, 2 * HEAD_DIM), jnp.float32),
        scratch_shapes=[pltpu.VMEM((tm, 2 * HEAD_DIM), jnp.float32)],
        name="nsa_compress",
    )(blocks, w_big)


def _select_blocks(imp, cur, n_pick):
    q, n_sb = imp.shape
    jq = lax.broadcasted_iota(jnp.int32, (q, n_sb), 1)
    valid = jq <= cur
    forced = valid & ((jq == cur) | (jq == 0))
    work0 = jnp.where(forced, -jnp.inf, jnp.where(valid, imp, NEG_INF))

    jf = jq.astype(jnp.float32)

    def pick(_, carry):
        work, sel = carry
        mx = jnp.max(work, axis=-1, keepdims=True)
        hit = jf == jnp.min(jnp.where(work == mx, jf, float(n_sb)), axis=-1, keepdims=True)
        sel = jnp.where(hit & valid, 1.0, sel)
        work = jnp.where(hit, -jnp.inf, work)
        return work, sel

    _, sel = lax.fori_loop(0, n_pick - 2, pick, (work0, forced.astype(jnp.float32)), unroll=True)
    return sel


def _select_blocks_by_rank(imp, cur, n_pick, n_cand):
    q, n_sb = imp.shape
    jq = lax.broadcasted_iota(jnp.int32, (q, n_sb), 1)
    valid = jq <= cur
    forced = valid & ((jq == cur) | (jq == 0))
    work = jnp.where(forced, -jnp.inf, jnp.where(valid, imp, NEG_INF))
    ahead = jnp.zeros((q, n_sb), jnp.float32)
    for i in range(n_cand):
        wi = work[:, i:i + 1]
        ahead += ((wi > work) | ((wi == work) & (jq > i))).astype(jnp.float32)
    return (forced | (valid & (ahead < n_pick - 2))).astype(jnp.float32)


def _nsa_prompt_kernel(qs_ref, qf_ref, kc_ref, vc_ref, ksel_ref, vsel_ref, kwin_ref, vwin_ref, g_ref,
                       o_ref, flag_ref, *, tq, tk, tw):
    i = pl.program_id(1)
    f32, bf16 = jnp.float32, jnp.bfloat16
    H = NSA_HEADS
    R = H * tq
    qs = qs_ref[0].reshape(R, 2 * HEAD_DIM)
    qf = qf_ref[0].reshape(R, HEAD_DIM)
    row = lax.broadcasted_iota(jnp.int32, (R, 1), 0)
    head = row // tq
    slope = jnp.where(head == 0, NSA_SLOPES[0], jnp.where(head == 1, NSA_SLOPES[1],
                      jnp.where(head == 2, NSA_SLOPES[2], NSA_SLOPES[3]))).astype(f32)
    posq = i * tq + (row - head * tq)

    n_cb = kc_ref.shape[1]
    sc = lax.dot_general(qf, kc_ref[0], _NT, precision=_HI, preferred_element_type=f32)
    jb = lax.broadcasted_iota(jnp.int32, (R, n_cb), 1)
    distc = posq - (jb * NSA_BLOCK + NSA_BLOCK - 1)
    okc = distc >= 0
    sc = jnp.where(okc, sc - slope * distc.astype(f32), NEG_INF)
    pc = jnp.exp(sc - jnp.max(sc, axis=-1, keepdims=True)) * okc.astype(f32)
    pc = pc / jnp.maximum(jnp.sum(pc, axis=-1, keepdims=True), TINY)
    o_cmp = jnp.dot(pc.astype(bf16), vc_ref[0].astype(bf16), preferred_element_type=f32)
    imp = pc[0:tq] + pc[tq:2 * tq] + pc[2 * tq:3 * tq] + pc[3 * tq:4 * tq]

    pq = i * tq + lax.broadcasted_iota(jnp.int32, (tq, 1), 0)
    msel = _select_blocks(imp, pq // NSA_BLOCK, NSA_TOPK)

    bpt = tk // NSA_BLOCK
    blk_any = jnp.max(msel, axis=0, keepdims=True)
    for j in range(n_cb // bpt):
        flag_ref[j] = (jnp.max(blk_any[:, j * bpt:(j + 1) * bpt]) > 0.5).astype(jnp.int32)
    mbias = (NEG_INF * (1.0 - msel)).astype(bf16)

    rowpos = i * tq + lax.broadcasted_iota(jnp.int32, (tq, 1), 0)

    def attend(carry, s, v1):
        m, acc = carry
        m_new = jnp.maximum(m, jnp.max(s, axis=-1, keepdims=True))
        p = jnp.exp(s - m_new).astype(bf16)
        return m_new, jnp.exp(m - m_new) * acc + jnp.dot(p, v1, preferred_element_type=f32)

    init = (jnp.full((R, 1), NEG_INF, f32), jnp.zeros((R, 2 * HEAD_DIM), f32))

    def sel_tile(j, carry, causal):
        k0 = pl.multiple_of(j * tk, tk)
        s = lax.dot_general(qs, ksel_ref[0, pl.ds(k0, tk), :], _NT, preferred_element_type=f32)
        eb = (lax.broadcasted_iota(jnp.int32, (n_cb, tk), 0)
              == j * bpt + lax.broadcasted_iota(jnp.int32, (n_cb, tk), 1) // NSA_BLOCK)
        bias = jnp.dot(mbias, eb.astype(bf16), preferred_element_type=f32)
        if causal:
            d = rowpos - (k0 + lax.broadcasted_iota(jnp.int32, (tq, tk), 1))
            bias = jnp.where(d >= 0, bias, NEG_INF)
        return attend(carry, s + _tile_rows(bias, H), vsel_ref[0, pl.ds(k0, tk), :])

    def sel_body(j, carry):
        return lax.cond(flag_ref[j] > 0, lambda c: sel_tile(j, c, False), lambda c: c, carry)

    n_sel = (i * tq + tq - 1) // tk + 1
    carry = lax.fori_loop(0, n_sel - 1, sel_body, init)
    _, a_sel = sel_tile(n_sel - 1, carry, True)
    o_sel = a_sel[:, :HEAD_DIM] / a_sel[:, HEAD_DIM:]

    nw = NSA_WINDOW + tq
    w0 = pl.multiple_of(jnp.maximum(i * tq - NSA_WINDOW, 0), tw)
    sw = lax.dot_general(qs, kwin_ref[0, pl.ds(w0, nw), :], _NT, preferred_element_type=f32)
    dw = rowpos - (w0 + lax.broadcasted_iota(jnp.int32, (tq, nw), 1))
    okw = (dw >= 0) & (dw < NSA_WINDOW)
    sw = jnp.where(_tile_rows(okw, H), sw, NEG_INF)
    _, a_win = attend(init, sw, vwin_ref[0, pl.ds(w0, nw), :])
    o_win = a_win[:, :HEAD_DIM] / a_win[:, HEAD_DIM:]

    g = _sigmoid(g_ref[...])
    gate = lambda c: jnp.concatenate([g[:, 3 * h + c:3 * h + c + 1] for h in range(H)], axis=0)
    o = gate(0) * o_cmp + gate(1) * o_sel + gate(2) * o_win
    o_ref[0] = o.reshape(H, tq, HEAD_DIM)


def _heads_major(x, dtype):
    B, S, _ = x.shape
    return x.reshape(B, S, -1, HEAD_DIM).transpose(0, 2, 1, 3).astype(dtype)


def _nsa_prompt(nq, nqb, rows, rowsb, winb, gates, w_ck, w_cv, tq=256, tk=512, tw=128):
    B, S, _ = nq.shape
    bf16 = jnp.bfloat16
    kcvc = _nsa_compress_blocks(rows.reshape(B * S // NSA_BLOCK, -1), _compress_weights(w_ck, w_cv))
    kcvc = kcvc.reshape(B, S // NSA_BLOCK, 2 * HEAD_DIM)
    kc, vc = kcvc[..., :HEAD_DIM], kcvc[..., HEAD_DIM:]
    pos = jnp.arange(S)
    blk, off = (pos // NSA_BLOCK).astype(jnp.float32), (pos % NSA_BLOCK).astype(jnp.float32)
    one = jnp.ones((S,), jnp.float32)
    slopes = jnp.asarray(NSA_SLOPES, jnp.float32)[:, None]
    q_terms = jnp.stack([-slopes * NSA_BLOCK * blk, -slopes * off, slopes * NSA_BLOCK * one, slopes * one], axis=-1)
    k_terms = jnp.stack([one, one, blk, off], axis=-1)
    pad_terms = lambda t: jnp.pad(t, [(0, 0)] * (t.ndim - 1) + [(0, HEAD_DIM - 4)]).astype(bf16)
    with_k_terms = lambda k: jnp.concatenate([k, jnp.broadcast_to(pad_terms(k_terms), (B, S, HEAD_DIM))], axis=-1)
    qs = jnp.concatenate([_heads_major(nqb, bf16),
                          jnp.broadcast_to(pad_terms(q_terms), (B, NSA_HEADS, S, HEAD_DIM))], axis=-1)
    qf = _heads_major(nq, jnp.float32)
    rows, win = rowsb, winb
    col = lambda a, c: a[..., c * HEAD_DIM:(c + 1) * HEAD_DIM]
    with_ones = lambda v: jnp.concatenate([v, jnp.ones_like(v)], axis=-1)
    full = lambda n, w: pl.BlockSpec((1, n, w), lambda b, i: (b, 0, 0))
    qspec = pl.BlockSpec((1, NSA_HEADS, tq, HEAD_DIM), lambda b, i: (b, 0, i, 0))
    n_cb = S // NSA_BLOCK
    o = pl.pallas_call(
        functools.partial(_nsa_prompt_kernel, tq=tq, tk=tk, tw=tw),
        grid=(B, S // tq),
        in_specs=[pl.BlockSpec((1, NSA_HEADS, tq, 2 * HEAD_DIM), lambda b, i: (b, 0, i, 0)), qspec,
                  full(n_cb, HEAD_DIM), full(n_cb, HEAD_DIM), full(S, 2 * HEAD_DIM),
                  full(S, 2 * HEAD_DIM), full(S, 2 * HEAD_DIM), full(S, 2 * HEAD_DIM),
                  pl.BlockSpec((None, tq, 128), lambda b, i: (b, i, 0))],
        out_specs=qspec,
        out_shape=jax.ShapeDtypeStruct((B, NSA_HEADS, S, HEAD_DIM), jnp.float32),
        scratch_shapes=[pltpu.SMEM((S // tk,), jnp.int32)],
        compiler_params=pltpu.CompilerParams(dimension_semantics=("parallel", "arbitrary"),
                                             vmem_limit_bytes=VMEM_LIMIT_BYTES),
        name="nsa_prompt",
    )(qs, qf, kc, vc, with_k_terms(col(rows, 2)), with_ones(col(rows, 3)),
      with_k_terms(col(win, 0)), with_ones(col(win, 1)), gates)
    return o.transpose(0, 2, 1, 3).reshape(B, S, BRANCH_WIDTH)


def _fox_prompt_kernel(q_ref, k_ref, v_ref, fq_ref, fk_ref, o_ref, *, tq, t):
    i = pl.program_id(1)
    f32, bf16 = jnp.float32, jnp.bfloat16
    lane = lax.broadcasted_iota(jnp.int32, (1, 2 * HEAD_DIM), 1)
    low = lane < HEAD_DIM
    rc = lax.broadcasted_iota(jnp.int32, (tq, t), 0) - lax.broadcasted_iota(jnp.int32, (tq, t), 1)
    n_pairs = FOX_HEADS // 2
    pair_cols = [slice(hp * 2 * HEAD_DIM, (hp + 1) * 2 * HEAD_DIM) for hp in range(n_pairs)]
    q_h, fq_h = [], []
    for hp in range(n_pairs):
        q2 = q_ref[0, :, pair_cols[hp]]
        zq = jnp.zeros_like(q2)
        q_h += [jnp.where(low, q2, zq), jnp.where(low, zq, q2)]
        fq_h += [fq_ref[0, :, 2 * hp:2 * hp + 1], fq_ref[0, :, 2 * hp + 1:2 * hp + 2]]

    def body(j, carry, diag_offset):
        k0 = pl.multiple_of(j * t, t)
        fk = fk_ref[0, j]
        out = []
        for hp in range(n_pairs):
            k2 = k_ref[0, pl.ds(k0, t), pair_cols[hp]]
            v2 = v_ref[0, pl.ds(k0, t), pair_cols[hp]]
            one = jnp.ones_like(v2)
            v1 = (jnp.where(low, v2, one), jnp.where(low, one, v2))
            for x in range(2):
                h = 2 * hp + x
                m, acc = carry[h]
                s = lax.dot_general(q_h[h], k2, _NT, preferred_element_type=f32) + (fq_h[h] - fk[h:h + 1])
                if diag_offset is not None:
                    s = jnp.where(rc >= diag_offset, s, NEG_INF)
                m_new = jnp.maximum(m, jnp.max(s, axis=-1, keepdims=True))
                p = jnp.exp(s - m_new).astype(bf16)
                acc = jnp.exp(m - m_new) * acc + jnp.dot(p, v1[x], preferred_element_type=f32)
                out.append((m_new, acc))
        return tuple(out)

    init = tuple((jnp.full((tq, 1), NEG_INF, f32), jnp.zeros((tq, 2 * HEAD_DIM), f32)) for _ in range(FOX_HEADS))
    n_full = i * (tq // t)
    carry = lax.fori_loop(0, n_full, functools.partial(body, diag_offset=None), init)
    for jj in range(tq // t):
        carry = body(n_full + jj, carry, jj * t)
    for hp in range(n_pairs):
        acc_a, acc_b = carry[2 * hp][1], carry[2 * hp + 1][1]
        den = pltpu.roll(jnp.where(low, acc_b, acc_a), HEAD_DIM, axis=1)
        o_ref[0, :, pair_cols[hp]] = jnp.where(low, acc_a, acc_b) / den


def _fox_prompt_attn(zb, F, tq=1024, t=1024):
    B, S, _ = zb.shape
    W = FOX_HEADS * HEAD_DIM
    cq, ck, cv = ((c - Z_NQ) // W for c in (Z_FQ, Z_FK, Z_FV))
    fk = jnp.pad(F.transpose(0, 2, 1), ((0, 0), (0, 8 - FOX_HEADS), (0, 0)))
    fk = fk.reshape(B, 8, S // t, t).transpose(0, 2, 1, 3)
    return pl.pallas_call(
        functools.partial(_fox_prompt_kernel, tq=tq, t=t),
        grid=(B, S // tq),
        in_specs=[pl.BlockSpec((1, tq, W), lambda b, i: (b, i, cq)),
                  pl.BlockSpec((1, S, W), lambda b, i: (b, 0, ck)),
                  pl.BlockSpec((1, S, W), lambda b, i: (b, 0, cv)),
                  pl.BlockSpec((1, tq, FOX_HEADS), lambda b, i: (b, i, 0)),
                  pl.BlockSpec((1, S // t, 8, t), lambda b, i: (b, 0, 0, 0))],
        out_specs=pl.BlockSpec((1, tq, W), lambda b, i: (b, i, 0)),
        out_shape=jax.ShapeDtypeStruct((B, S, W), jnp.float32),
        compiler_params=pltpu.CompilerParams(dimension_semantics=("parallel", "arbitrary"),
                                             vmem_limit_bytes=VMEM_LIMIT_BYTES),
        name="fox_prompt",
    )(zb, zb, zb, F, fk)


def _log_sigmoid(x):
    return jnp.minimum(x, 0.0) - jnp.log1p(jnp.exp(-jnp.abs(x)))


def _mlstm_kernel(q_ref, k_ref, kt_ref, v_ref, og_ref, gc_ref, gr_ref, nw_ref, cn0_ref, m0_ref,
                  h_ref, cn_ref, m_ref, cn_s, m_s, *, nb, L):
    c = pl.program_id(1)
    f32 = jnp.float32
    W = 2 * HEAD_DIM
    n_pairs = MLSTM_HEADS // 2

    @pl.when(c == 0)
    def _():
        cn_s[...] = cn0_ref[...]
        m_s[...] = m0_ref[...]

    lane = lax.broadcasted_iota(jnp.int32, (1, W), 1)
    low = lane < HEAD_DIM
    ti = lax.broadcasted_iota(jnp.int32, (L, L), 0)
    si = lax.broadcasted_iota(jnp.int32, (L, L), 1)
    causal = si <= ti
    tri = causal.astype(f32)
    tri_t = (ti <= si).astype(f32)
    srow = lax.broadcasted_iota(jnp.int32, (W, 2 * W), 0)
    slane = lax.broadcasted_iota(jnp.int32, (W, 2 * W), 1)
    top = srow < HEAD_DIM
    keep_a = top & ((slane < HEAD_DIM) | (slane == W))
    keep_b = (~top) & (((slane >= HEAD_DIM) & (slane < W)) | (slane == W + 1))
    lane_w = lax.broadcasted_iota(jnp.int32, (1, W), 1)
    mdt = k_ref.dtype

    for b in range(nb):
        gcol = gc_ref[b]
        grow = gr_ref[b]
        bcol = jnp.dot(tri, _log_sigmoid(gcol), precision=_HI, preferred_element_type=f32)
        brow = jnp.dot(_log_sigmoid(grow), tri_t, precision=_HI, preferred_element_type=f32)
        for hp in range(n_pairs):
            cols = slice(hp * W, (hp + 1) * W)
            q2 = q_ref[b, :, cols]
            k2 = k_ref[b, :, cols]
            v2 = v_ref[b, :, cols]
            kt2 = kt_ref[b, cols, :]
            cn = cn_s[b, hp]
            r = jnp.dot(q2, cn.astype(mdt), preferred_element_type=f32)
            zq = jnp.zeros_like(q2)
            per_head = []
            for x in range(2):
                h = 2 * hp + x
                qx = jnp.where(low, q2, zq) if x == 0 else jnp.where(low, zq, q2)
                b_c = bcol[:, MLSTM_HEADS + h:MLSTM_HEADS + h + 1]
                b_r = brow[MLSTM_HEADS + h:MLSTM_HEADS + h + 1, :]
                ig_c = gcol[:, h:h + 1]
                ig_r = grow[h:h + 1, :]
                m_prev = m_s[b, 0:1, h:h + 1]
                dmat = jnp.where(causal, b_c - b_r + ig_r, NEG_INF)
                a_c = b_c + m_prev
                m_t = jnp.maximum(a_c, jnp.max(dmat, axis=-1, keepdims=True))
                wq = jnp.exp(dmat - m_t) * lax.dot_general(qx, k2, _NT, preferred_element_type=f32)
                inter = jnp.exp(a_c - m_t)
                wv = jnp.dot(wq.astype(mdt), v2, preferred_element_type=f32)
                den = inter * r[:, W + x:W + x + 1] + jnp.sum(wq, axis=-1, keepdims=True)
                den = jnp.maximum(jnp.abs(den), jnp.exp(-m_t))
                bl = b_c[L - 1:L, :]
                g_c = bl - b_c + ig_c
                m_new = jnp.maximum(bl + m_prev, jnp.max(g_c, axis=0, keepdims=True))
                ws = jnp.exp(g_c - m_new)
                decay = jnp.exp(bl + m_prev - m_new)
                aug = jnp.concatenate([v2.astype(f32) * ws, jnp.where(lane_w == x, ws, 0.0)], axis=1)
                u = jnp.dot(kt2, aug.astype(mdt), preferred_element_type=f32)
                per_head.append((inter, wv, den, decay, u))
                m_s[b, 0:1, h:h + 1] = m_new
            (ia, wva, dena, deca, ua), (ib, wvb, denb, decb, ub) = per_head
            num = jnp.where(low, ia * r[:, :W] + wva, ib * r[:, :W] + wvb)
            hid = num / jnp.where(low, dena, denb)
            cn_s[b, hp] = (jnp.where(top, deca, decb) * cn + jnp.where(keep_a, ua, 0.0)
                           + jnp.where(keep_b, ub, 0.0))
            hid = _sigmoid(og_ref[b, :, cols]) * hid
            mean = lambda t: jnp.where(low, jnp.sum(jnp.where(low, t, 0.0), axis=-1, keepdims=True),
                                       jnp.sum(jnp.where(low, 0.0, t), axis=-1, keepdims=True)) / HEAD_DIM
            mu = mean(hid)
            var = mean(jnp.square(hid - mu))
            h_ref[b, :, cols] = (hid - mu) * lax.rsqrt(var + LN_EPS) * nw_ref[:, cols]

    @pl.when(c == pl.num_programs(1) - 1)
    def _():
        cn_ref[...] = cn_s[...]
        m_ref[...] = m_s[...]


def _mlstm_state_pack(c, n, m):
    Bx = c.shape[0]
    HD, W = HEAD_DIM, 2 * HEAD_DIM
    cp = c.reshape(Bx, MLSTM_HEADS // 2, 2, HD, HD)
    np_ = n.reshape(Bx, MLSTM_HEADS // 2, 2, HD, 1)
    z = lambda w: jnp.zeros((Bx, MLSTM_HEADS // 2, HD, w), jnp.float32)
    top = jnp.concatenate([cp[:, :, 0], z(HD), np_[:, :, 0], z(W - 1)], axis=-1)
    bot = jnp.concatenate([z(HD), cp[:, :, 1], z(1), np_[:, :, 1], z(W - 2)], axis=-1)
    cn = jnp.concatenate([top, bot], axis=-2)
    m8 = jnp.pad(m[:, None, :], ((0, 0), (0, 7), (0, 128 - MLSTM_HEADS)))
    return cn, m8


def _mlstm_state_unpack(cn, m8):
    W = 2 * HEAD_DIM
    c = jnp.stack([cn[:, h // 2, (h % 2) * HEAD_DIM:(h % 2 + 1) * HEAD_DIM,
                      (h % 2) * HEAD_DIM:(h % 2 + 1) * HEAD_DIM] for h in range(MLSTM_HEADS)], axis=1)
    n = jnp.stack([cn[:, h // 2, (h % 2) * HEAD_DIM:(h % 2 + 1) * HEAD_DIM, W + h % 2]
                   for h in range(MLSTM_HEADS)], axis=1)
    return c, n, m8[:, 0, :MLSTM_HEADS]


def _mlstm(q, k, v, og, mi, mf, norm_w, c0, n0, m0, L, nb, mxu_dtype):
    Bx, S, W4 = q.shape
    f32 = jnp.float32
    gates = jnp.concatenate([mi, mf], axis=-1).astype(f32)
    gcol = jnp.pad(gates, ((0, 0), (0, 0), (0, 128 - 2 * MLSTM_HEADS)))
    grow = gates.transpose(0, 2, 1)
    cn0, m8 = _mlstm_state_pack(c0.astype(f32), n0.astype(f32), m0.astype(f32))
    tok = lambda w: pl.BlockSpec((nb, L, w), lambda b, c: (b, c, 0))
    st_cn = pl.BlockSpec((nb, MLSTM_HEADS // 2, 128, 256), lambda b, c: (b, 0, 0, 0))
    st_m = pl.BlockSpec((nb, 8, 128), lambda b, c: (b, 0, 0))
    h, cn, m8 = pl.pallas_call(
        functools.partial(_mlstm_kernel, nb=nb, L=L),
        grid=(Bx // nb, S // L),
        in_specs=[tok(W4), tok(W4), pl.BlockSpec((nb, W4, L), lambda b, c: (b, 0, c)), tok(W4), tok(W4),
                  tok(128), pl.BlockSpec((nb, 8, L), lambda b, c: (b, 0, c)),
                  pl.BlockSpec((1, W4), lambda b, c: (0, 0)), st_cn, st_m],
        out_specs=[tok(W4), st_cn, st_m],
        out_shape=[jax.ShapeDtypeStruct((Bx, S, W4), f32),
                   jax.ShapeDtypeStruct(cn0.shape, f32), jax.ShapeDtypeStruct(m8.shape, f32)],
        scratch_shapes=[pltpu.VMEM((nb, MLSTM_HEADS // 2, 128, 256), f32), pltpu.VMEM((nb, 8, 128), f32)],
        compiler_params=pltpu.CompilerParams(dimension_semantics=("parallel", "arbitrary"),
                                             vmem_limit_bytes=VMEM_LIMIT_BYTES),
        name="mlstm",
    )(q.astype(mxu_dtype), k.astype(mxu_dtype), k.astype(mxu_dtype).transpose(0, 2, 1), v.astype(mxu_dtype),
      og.astype(f32), gcol, grow, norm_w.astype(f32).reshape(1, W4), cn0, m8)
    return h, _mlstm_state_unpack(cn, m8)


def _ln(x, w, b):
    mu = jnp.mean(x, axis=-1, keepdims=True)
    var = jnp.mean(jnp.square(x - mu), axis=-1, keepdims=True)
    return (x - mu) * lax.rsqrt(var + LN_EPS) * w + b


def _ln_kernel(x_ref, w_ref, b_ref, o_ref):
    o_ref[...] = _ln(x_ref[...], w_ref[...], b_ref[...])


def _layer_norm_rows(x, w, b, tm=1024):
    T, D = x.shape
    vec = pl.BlockSpec((1, D), lambda i: (0, 0))
    return pl.pallas_call(
        _ln_kernel, grid=(T // tm,),
        in_specs=[pl.BlockSpec((tm, D), lambda i: (i, 0)), vec, vec],
        out_specs=pl.BlockSpec((tm, D), lambda i: (i, 0)),
        out_shape=jax.ShapeDtypeStruct((T, D), jnp.float32),
        name="layer_norm",
    )(x, w.reshape(1, D), b.reshape(1, D))


Z_NQ = 0
Z_ROWS = Z_NQ + 256
Z_MQ = 512
Z_MK = Z_MQ + 256
Z_MV = 1024
Z_MO = Z_MV + 256
Z_FQ = 1536
Z_WIN = Z_FQ + 256
Z_SMALL = Z_WIN + 128
Z_FK = 2048
Z_FV = Z_FK + 256
Z_WIDTH = 2560
Z_TILE = 512
ZB_WIDTH = Z_WIDTH
GM_SPLIT = 13
SMALL_MI, SMALL_MF, SMALL_FF = 12, 16, 20


def _z_column_order():
    starts = np.concatenate([[0], np.cumsum(IN_SPLITS)])
    seg = lambda i, lo=0, hi=None: np.arange(starts[i] + lo, starts[i] + (IN_SPLITS[i] if hi is None else hi))
    pad = lambda n: np.full((n,), -1)
    order = np.concatenate([
        seg(0), seg(1, 0, 256), seg(3), seg(4), seg(5), seg(6),
        seg(9), seg(1, 256, 384), seg(2), seg(7), seg(8), seg(12), pad(128 - 24), seg(10), seg(11)])
    assert order.shape == (Z_WIDTH,)
    return order


def _permute_in_proj(w_in, b_in):
    order = _z_column_order()
    valid = jnp.asarray(order >= 0)
    idx = jnp.asarray(np.maximum(order, 0))
    scale = np.ones((Z_WIDTH,), np.float32)
    for c0 in (Z_NQ, Z_MK, Z_FQ):
        scale[c0:c0 + 256] = ATTN_SCALE
    scale = jnp.asarray(scale)
    w = jnp.where(valid[None, :], w_in[:, idx], 0.0) * scale[None, :]
    b = jnp.where(valid, b_in[idx], 0.0) * scale
    return w, b


def _proj_kernel(x_ref, w_ref, wlo_ref, b_ref, o_ref, ob_ref, xh_s, xl_s, *, hi_tile):
    j = pl.program_id(1)
    f32 = jnp.float32
    _proj_tile(x_ref, w_ref, wlo_ref, b_ref, o_ref, xh_s, xl_s, j, hi_tile)
    ob_ref[...] = o_ref[...].astype(jnp.bfloat16)


def _proj_tile(x_ref, w_ref, wlo_ref, b_ref, o_ref, xh_s, xl_s, j, hi_tile):
    f32 = jnp.float32

    @pl.when(j == 0)
    def _():
        x = x_ref[...]
        xh = x.astype(jnp.bfloat16)
        xh_s[...] = xh
        xl_s[...] = (x - xh.astype(f32)).astype(jnp.bfloat16)

    @pl.when(j != hi_tile)
    def _():
        o_ref[...] = jnp.dot(xh_s[...], w_ref[...], preferred_element_type=f32) + b_ref[...]

    @pl.when(j == hi_tile)
    def _():
        acc = jnp.dot(xh_s[...], wlo_ref[...], preferred_element_type=f32)
        acc += jnp.dot(xl_s[...], w_ref[...], preferred_element_type=f32)
        acc += jnp.dot(xh_s[...], w_ref[...], preferred_element_type=f32)
        o_ref[...] = acc + b_ref[...]


def _in_projection(x, w_in, b_in, tm=1024):
    T, D = x.shape
    w, b = _permute_in_proj(w_in, b_in)
    wh = w.astype(jnp.bfloat16)
    hi_tile = Z_NQ // Z_TILE
    wlo = (w[:, Z_NQ:Z_NQ + Z_TILE] - wh[:, Z_NQ:Z_NQ + Z_TILE].astype(jnp.float32)).astype(jnp.bfloat16)
    return pl.pallas_call(
        functools.partial(_proj_kernel, hi_tile=hi_tile),
        grid=(T // tm, Z_WIDTH // Z_TILE),
        in_specs=[pl.BlockSpec((tm, D), lambda i, j: (i, 0)),
                  pl.BlockSpec((D, Z_TILE), lambda i, j: (0, j)),
                  pl.BlockSpec((D, Z_TILE), lambda i, j: (0, 0)),
                  pl.BlockSpec((1, Z_TILE), lambda i, j: (0, j))],
        out_specs=[pl.BlockSpec((tm, Z_TILE), lambda i, j: (i, j)),
                   pl.BlockSpec((tm, Z_TILE), lambda i, j: (i, j))],
        out_shape=[jax.ShapeDtypeStruct((T, Z_WIDTH), jnp.float32),
                   jax.ShapeDtypeStruct((T, ZB_WIDTH), jnp.bfloat16)],
        scratch_shapes=[pltpu.VMEM((tm, D), jnp.bfloat16), pltpu.VMEM((tm, D), jnp.bfloat16)],
        compiler_params=pltpu.CompilerParams(dimension_semantics=("parallel", "arbitrary"),
                                             vmem_limit_bytes=VMEM_LIMIT_BYTES),
        name="in_projection",
    )(x, wh, wlo, b.reshape(1, Z_WIDTH))


def _merge_kernel(x_ref, on_ref, om_ref, of_ref, wg_ref, bg_ref, wb_ref, wo_ref, lw_ref, lb_ref, o_ref, *, alpha):
    f32, bf16 = jnp.float32, jnp.bfloat16
    x = x_ref[...]
    xb = x.astype(bf16)
    y = None
    for m, br in enumerate((on_ref, om_ref, of_ref)):
        gcols = slice(m * D_MODEL, (m + 1) * D_MODEL)
        gm = jnp.dot(xb, wg_ref[:, gcols], preferred_element_type=f32) + bg_ref[:, gcols]
        proj = jnp.dot(br[...].astype(bf16), wb_ref[m], preferred_element_type=f32)
        term = _sigmoid(gm) * proj
        y = term if y is None else y + term
    mix = jnp.dot(y.astype(bf16), wo_ref[...], preferred_element_type=f32)
    o_ref[...] = _ln(alpha * x + mix, lw_ref[...], lb_ref[...])


def _merge_ln(x, o_nsa, o_ml, o_fox, w_in, b_in, w_branch, w_out, ln_w, ln_b, alpha, tm=512):
    T, D = x.shape
    g0 = int(np.sum(IN_SPLITS[:GM_SPLIT]))
    w_gm = w_in[:, g0:g0 + N_BRANCH * D].astype(jnp.bfloat16)
    b_gm = b_in[g0:g0 + N_BRANCH * D].reshape(1, N_BRANCH * D)
    tok = lambda w: pl.BlockSpec((tm, w), lambda i: (i, 0))
    vec = pl.BlockSpec((1, D), lambda i: (0, 0))
    return pl.pallas_call(
        functools.partial(_merge_kernel, alpha=alpha),
        grid=(T // tm,),
        in_specs=[tok(D), tok(BRANCH_WIDTH), tok(BRANCH_WIDTH), tok(BRANCH_WIDTH),
                  pl.BlockSpec((D, N_BRANCH * D), lambda i: (0, 0)),
                  pl.BlockSpec((1, N_BRANCH * D), lambda i: (0, 0)),
                  pl.BlockSpec((N_BRANCH, BRANCH_WIDTH, D), lambda i: (0, 0, 0)),
                  pl.BlockSpec((D, D), lambda i: (0, 0)), vec, vec],
        out_specs=tok(D),
        out_shape=jax.ShapeDtypeStruct((T, D), jnp.float32),
        compiler_params=pltpu.CompilerParams(dimension_semantics=("parallel",),
                                             vmem_limit_bytes=VMEM_LIMIT_BYTES),
        name="merge_ln",
    )(x, o_nsa, o_ml, o_fox, w_gm, b_gm, w_branch.astype(jnp.bfloat16), w_out.astype(jnp.bfloat16),
      ln_w.reshape(1, D), ln_b.reshape(1, D))


def _route(logits):
    tm = logits.shape[0]
    lane = lax.broadcasted_iota(jnp.int32, (tm, 128), 1)
    lanef = lane.astype(jnp.float32)
    big = 1e9
    is_g = lane < N_GROUPS
    lg = jnp.where(is_g, logits, -jnp.inf)
    eg = jnp.exp(lg - jnp.max(lg, axis=-1, keepdims=True))
    pg = eg / jnp.sum(eg, axis=-1, keepdims=True)
    g_val = jnp.max(pg, axis=-1, keepdims=True)
    g_idx = jnp.min(jnp.where(is_g & (pg == g_val), lanef, big), axis=-1, keepdims=True)
    e_lo = N_GROUPS + EXPERTS_PER_GROUP * g_idx
    in_grp = (lanef >= e_lo) & (lanef < e_lo + EXPERTS_PER_GROUP)
    le = jnp.where(in_grp, logits, -jnp.inf)
    ee = jnp.exp(le - jnp.max(le, axis=-1, keepdims=True))
    pe = ee / jnp.sum(ee, axis=-1, keepdims=True)
    v1 = jnp.max(pe, axis=-1, keepdims=True)
    i1 = jnp.min(jnp.where(in_grp & (pe == v1), lanef, big), axis=-1, keepdims=True)
    rest = in_grp & (lanef != i1)
    pe2 = jnp.where(rest, pe, -1.0)
    v2 = jnp.max(pe2, axis=-1, keepdims=True)
    i2 = jnp.min(jnp.where(rest & (pe2 == v2), lanef, big), axis=-1, keepdims=True)
    tot = v1 + v2
    return jnp.where(lanef == i1, g_val * v1 / tot, jnp.where(lanef == i2, g_val * v2 / tot, 0.0))


def _moe_kernel(x_ref, wr_ref, br_ref, wg_ref, wu_ref, wd_ref, lw_ref, lb_ref, o_ref, xb_s, gate_s, acc_s, *, alpha):
    e = pl.program_id(1)
    f32, bf16 = jnp.float32, jnp.bfloat16

    @pl.when(e == 0)
    def _():
        x = x_ref[...]
        xb_s[...] = x.astype(bf16)
        logits = jnp.dot(x, wr_ref[...], precision=_HI, preferred_element_type=f32) + br_ref[...]
        gate_s[...] = _route(logits)
        acc_s[...] = jnp.zeros_like(acc_s)

    lane = lax.broadcasted_iota(jnp.int32, gate_s.shape, 1)
    y = None
    for k in range(wg_ref.shape[0]):
        ex = e * wg_ref.shape[0] + k
        gate = jnp.sum(jnp.where(lane == N_GROUPS + ex, gate_s[...], 0.0), axis=-1, keepdims=True)
        g = jnp.dot(xb_s[...], wg_ref[k].astype(bf16), preferred_element_type=f32)
        u = jnp.dot(xb_s[...], wu_ref[k].astype(bf16), preferred_element_type=f32)
        h = (g * _sigmoid(g)) * u * gate
        yk = jnp.dot(h.astype(bf16), wd_ref[k].astype(bf16), preferred_element_type=f32)
        y = yk if y is None else y + yk
    acc_s[...] += y

    @pl.when(e == pl.num_programs(1) - 1)
    def _():
        o_ref[...] = _ln(alpha * x_ref[...] + acc_s[...], lw_ref[...], lb_ref[...])


def _moe_ln(x, w_group, b_group, w_expert, b_expert, w_gate, w_up, w_down, ln_w, ln_b, alpha, tm=1024, eps=2):
    T, D = x.shape
    bf16 = jnp.bfloat16
    n_r = N_GROUPS + N_EXPERTS
    wr = jnp.pad(jnp.concatenate([w_group, w_expert], axis=1), ((0, 0), (0, 128 - n_r)))
    br = jnp.pad(jnp.concatenate([b_group, b_expert]), (0, 128 - n_r)).reshape(1, 128)
    vec = pl.BlockSpec((1, D), lambda i, e: (0, 0))
    return pl.pallas_call(
        functools.partial(_moe_kernel, alpha=alpha),
        grid=(T // tm, N_EXPERTS // eps),
        in_specs=[pl.BlockSpec((tm, D), lambda i, e: (i, 0)),
                  pl.BlockSpec((D, 128), lambda i, e: (0, 0)),
                  pl.BlockSpec((1, 128), lambda i, e: (0, 0)),
                  pl.BlockSpec((eps, D, D_EXPERT), lambda i, e: (e, 0, 0)),
                  pl.BlockSpec((eps, D, D_EXPERT), lambda i, e: (e, 0, 0)),
                  pl.BlockSpec((eps, D_EXPERT, D), lambda i, e: (e, 0, 0)), vec, vec],
        out_specs=pl.BlockSpec((tm, D), lambda i, e: (i, 0)),
        out_shape=jax.ShapeDtypeStruct((T, D), jnp.float32),
        scratch_shapes=[pltpu.VMEM((tm, D), bf16), pltpu.VMEM((tm, 128), jnp.float32),
                        pltpu.VMEM((tm, D), jnp.float32)],
        compiler_params=pltpu.CompilerParams(dimension_semantics=("parallel", "arbitrary"),
                                             vmem_limit_bytes=VMEM_LIMIT_BYTES),
        name="moe_ln",
    )(x, wr, br, w_gate, w_up, w_down, ln_w.reshape(1, D), ln_b.reshape(1, D))


N_PAGES = PAST_LEN // PAGE_SIZE
NEW_PAD = 128
DEC_KEYS = PAST_LEN + NEW_PAD
DEC_ROWS = NSA_HEADS * DEC_SEQ


def _pages_token_minor(cache):
    nd = cache.ndim
    t = cache.transpose((0, 1) + tuple(range(3, nd)) + (2,))
    return t.reshape(cache.shape[0] * cache.shape[1], -1, cache.shape[2])


def _page_specs(rows, row_block, layer, n_phys):
    def spec(p):
        return pl.BlockSpec((1, rows, PAGE_SIZE),
                            lambda b, pt: (layer * n_phys + pt[b * N_PAGES + p], row_block, 0))
    return [spec(p) for p in range(N_PAGES)]


def _softmax_rows(s):
    m = jnp.max(s, axis=-1, keepdims=True)
    p = jnp.exp(s - m)
    return p, jnp.sum(p, axis=-1, keepdims=True)


def _fox_decode_kernel(pt_ref, qbd_ref, knew_ref, vnew_ref, fk_ref, fq_ref, *refs):
    pages, o_ref = refs[:N_PAGES], refs[N_PAGES]
    f32, bf16 = jnp.float32, jnp.bfloat16
    W = FOX_HEADS * HEAD_DIM
    qbd = qbd_ref[0]
    s = [jnp.dot(qbd, pg[0, :W, :].astype(bf16), preferred_element_type=f32) for pg in pages]
    s.append(jnp.dot(qbd, knew_ref[0], preferred_element_type=f32))
    s = jnp.concatenate(s, axis=1)
    rowh = lax.broadcasted_iota(jnp.int32, (DEC_ROWS, 1), 0) // DEC_SEQ
    fk = fk_ref[0]
    fk_rows = jnp.where(rowh == 0, fk[0:1], jnp.where(rowh == 1, fk[1:2], jnp.where(rowh == 2, fk[2:3], fk[3:4])))
    col = lax.broadcasted_iota(jnp.int32, (DEC_ROWS, DEC_KEYS), 1)
    t = lax.broadcasted_iota(jnp.int32, (DEC_ROWS, DEC_KEYS), 0) % DEC_SEQ
    ok = (col < PAST_LEN) | (col - PAST_LEN <= t)
    s = jnp.where(ok, s + (fq_ref[0] - fk_rows), NEG_INF)
    p, l = _softmax_rows(s)
    pb = p.astype(bf16)
    o = lax.dot_general(pb[:, PAST_LEN:], vnew_ref[0], _NT, preferred_element_type=f32)
    for i, pg in enumerate(pages):
        o += lax.dot_general(pb[:, i * PAGE_SIZE:(i + 1) * PAGE_SIZE], pg[0, W:, :].astype(bf16), _NT,
                             preferred_element_type=f32)
    o = o / l
    lane_h = lax.broadcasted_iota(jnp.int32, (DEC_ROWS, W), 1) // HEAD_DIM
    o = jnp.where(lane_h == rowh, o, 0.0)
    o_ref[0] = o[0:8] + o[8:16] + o[16:24] + o[24:32]


def _pad_new_t(x):
    return jnp.pad(x.transpose(0, 2, 1), ((0, 0), (0, 0), (0, NEW_PAD - DEC_SEQ))).astype(jnp.bfloat16)


def _fox_decode(q, k_new, v_new, lf_new, cache_kv, cache_lf, page_table, layer):
    DB = q.shape[0]
    f32, bf16 = jnp.float32, jnp.bfloat16
    W = FOX_HEADS * HEAD_DIM
    eye = jnp.eye(FOX_HEADS, dtype=f32)
    qh = q.reshape(DB, DEC_SEQ, FOX_HEADS, HEAD_DIM).transpose(0, 2, 1, 3)
    qbd = (qh[:, :, :, None, :] * eye[None, :, None, :, None]).reshape(DB, DEC_ROWS, W).astype(bf16)
    lf_all = jnp.concatenate([cache_lf[layer][page_table].reshape(DB, PAST_LEN, FOX_HEADS).astype(f32), lf_new], axis=1)
    F = jnp.cumsum(lf_all, axis=1)
    fk = jnp.pad(F.transpose(0, 2, 1), ((0, 0), (0, 8 - FOX_HEADS), (0, DEC_KEYS - PAST_LEN - DEC_SEQ)))
    fq = F[:, PAST_LEN:].transpose(0, 2, 1).reshape(DB, DEC_ROWS, 1)
    per_seq = lambda r, w: pl.BlockSpec((1, r, w), lambda b, pt: (b, 0, 0))
    pages = _pages_token_minor(cache_kv)
    return pl.pallas_call(
        _fox_decode_kernel,
        grid_spec=pltpu.PrefetchScalarGridSpec(
            num_scalar_prefetch=1, grid=(DB,),
            in_specs=[per_seq(DEC_ROWS, W), per_seq(W, NEW_PAD), per_seq(W, NEW_PAD), per_seq(8, DEC_KEYS),
                      per_seq(DEC_ROWS, 1)] + _page_specs(2 * W, 0, layer, cache_kv.shape[1]),
            out_specs=per_seq(DEC_SEQ, W)),
        out_shape=jax.ShapeDtypeStruct((DB, DEC_SEQ, W), f32),
        compiler_params=pltpu.CompilerParams(dimension_semantics=("parallel",), vmem_limit_bytes=VMEM_LIMIT_BYTES),
        name="fox_decode",
    )(page_table.reshape(-1), qbd, _pad_new_t(k_new), _pad_new_t(v_new), fk, fq, *([pages] * N_PAGES))


def _nsa_decode_kernel(pt_ref, qs_ref, qf_ref, kcvc_ref, rnew_ref, wbuf_ref, wnew_ref, g_ref, e_ref, *refs):
    pages, o_ref = refs[:N_PAGES], refs[N_PAGES]
    f32, bf16 = jnp.float32, jnp.bfloat16
    H, T, HD = NSA_HEADS, DEC_SEQ, HEAD_DIM
    qs = qs_ref[0]
    row = lax.broadcasted_iota(jnp.int32, (DEC_ROWS, 1), 0)
    head = row // T
    slope = jnp.where(head == 0, NSA_SLOPES[0], jnp.where(head == 1, NSA_SLOPES[1],
                      jnp.where(head == 2, NSA_SLOPES[2], NSA_SLOPES[3]))).astype(f32)
    posq = PAST_LEN + row % T

    kcvc = kcvc_ref[0]
    n_cb = kcvc.shape[0]
    sc = lax.dot_general(qf_ref[0], kcvc, _NT, precision=_HI, preferred_element_type=f32)
    jb = lax.broadcasted_iota(jnp.int32, (DEC_ROWS, n_cb), 1)
    distc = posq - (jb * NSA_BLOCK + NSA_BLOCK - 1)
    okc = distc >= 0
    sc = jnp.where(okc, sc - slope * distc.astype(f32), NEG_INF)
    pc = jnp.exp(sc - jnp.max(sc, axis=-1, keepdims=True)) * okc.astype(f32)
    pc = pc / jnp.maximum(jnp.sum(pc, axis=-1, keepdims=True), TINY)
    o_cmp = jnp.dot(pc.astype(bf16), kcvc.astype(bf16), preferred_element_type=f32)[:, HD:]
    imp = pc[0:T] + pc[T:2 * T] + pc[2 * T:3 * T] + pc[3 * T:4 * T]
    imp = jnp.concatenate([imp, jnp.zeros((T, 128 - n_cb), f32)], axis=1)
    cur = (PAST_LEN + lax.broadcasted_iota(jnp.int32, (T, 1), 0)) // NSA_BLOCK
    msel = _select_blocks_by_rank(imp, cur, NSA_TOPK, n_cb + 1).astype(bf16)

    mexp = jnp.dot(msel, e_ref[...], preferred_element_type=f32)
    col = lax.broadcasted_iota(jnp.int32, (T, DEC_KEYS), 1)
    tq = lax.broadcasted_iota(jnp.int32, (T, DEC_KEYS), 0)
    d = PAST_LEN + tq - col
    ok = (mexp > 0.5) & (d >= 0)
    kv = [pg[0].astype(bf16) for pg in pages] + [rnew_ref[0]]
    s = jnp.concatenate([jnp.dot(qs, x[:HD], preferred_element_type=f32) for x in kv], axis=1)
    s = jnp.where(_tile_rows(ok, H), s - slope * _tile_rows(d.astype(f32), H), NEG_INF)
    p, l = _softmax_rows(s)
    pb = p.astype(bf16)
    acc = jnp.zeros((DEC_ROWS, HD), f32)
    for i, x in enumerate(kv):
        acc += lax.dot_general(pb[:, i * PAGE_SIZE:(i + 1) * PAGE_SIZE], x[HD:], _NT, preferred_element_type=f32)
    o_sel = acc / l

    wb = wbuf_ref.shape[2]
    kvw = [wbuf_ref[0].astype(bf16), wnew_ref[0]]
    sw = jnp.concatenate([jnp.dot(qs, x[:HD], preferred_element_type=f32) for x in kvw], axis=1)
    colw = lax.broadcasted_iota(jnp.int32, (T, wb + NEW_PAD), 1)
    tw = lax.broadcasted_iota(jnp.int32, (T, wb + NEW_PAD), 0)
    dw = wb + tw - colw
    okw = (dw >= 0) & (dw < NSA_WINDOW)
    sw = jnp.where(_tile_rows(okw, H), sw - slope * _tile_rows(dw.astype(f32), H), NEG_INF)
    pw, lw = _softmax_rows(sw)
    pwb = pw.astype(bf16)
    accw = (lax.dot_general(pwb[:, :wb], kvw[0][HD:], _NT, preferred_element_type=f32)
            + lax.dot_general(pwb[:, wb:], kvw[1][HD:], _NT, preferred_element_type=f32))
    o_win = accw / lw

    g = _sigmoid(g_ref[0])
    gate = lambda c: jnp.concatenate([g[:, 3 * h + c:3 * h + c + 1] for h in range(H)], axis=0)
    o_ref[0] = gate(0) * o_cmp + gate(1) * o_sel + gate(2) * o_win


def _compress_pages_kernel(x_ref, wk_ref, wv_ref, o_ref):
    tm = x_ref.shape[0]
    f32 = jnp.float32
    acc_k = jnp.zeros((tm, 2 * HEAD_DIM), f32)
    acc_v = jnp.zeros((tm, 2 * HEAD_DIM), f32)
    pair = lambda r: jnp.concatenate([x_ref[:, r, :], x_ref[:, r + 1, :]], axis=1)
    for i in range(HEAD_DIM // 2):
        acc_k += jnp.dot(pair(2 * i), wk_ref[i], precision=_HI, preferred_element_type=f32)
        acc_v += jnp.dot(pair(HEAD_DIM + 2 * i), wv_ref[i], preferred_element_type=f32)
    o_ref[...] = jnp.concatenate([acc_k, acc_v], axis=1)


def _nsa_compress_pages(pages, w_ck, w_cv, layer, n_phys, tm=128):
    eye = jnp.eye(PAGE_SIZE // NSA_BLOCK, dtype=jnp.float32)
    big = lambda w: jnp.einsum('pde,bc->dbpce', w, eye).reshape(HEAD_DIM // 2, 2 * PAGE_SIZE, 2 * HEAD_DIM)
    wspec = pl.BlockSpec((HEAD_DIM // 2, 2 * PAGE_SIZE, 2 * HEAD_DIM), lambda i: (0, 0, 0))
    return pl.pallas_call(
        _compress_pages_kernel,
        grid=(n_phys // tm,),
        in_specs=[pl.BlockSpec((tm, 2 * HEAD_DIM, PAGE_SIZE), lambda i: (layer * (n_phys // tm) + i, 0, 0)),
                  wspec, wspec],
        out_specs=pl.BlockSpec((tm, 4 * HEAD_DIM), lambda i: (i, 0)),
        out_shape=jax.ShapeDtypeStruct((n_phys, 4 * HEAD_DIM), jnp.float32),
        compiler_params=pltpu.CompilerParams(dimension_semantics=("parallel",), vmem_limit_bytes=VMEM_LIMIT_BYTES),
        name="nsa_compress_pages",
    )(pages, big(w_ck), big(w_cv))


def _nsa_decode(nq, rows_new, win_new, gates, cache_rows, cache_win, page_table, w_ck, w_cv, layer):
    DB = nq.shape[0]
    f32, bf16 = jnp.float32, jnp.bfloat16
    HD = HEAD_DIM
    n_phys = cache_rows.shape[1]
    n_blk = PAGE_SIZE // NSA_BLOCK
    pages = _pages_token_minor(cache_rows)
    kcvc = _nsa_compress_pages(pages, w_ck, w_cv, layer, n_phys)[page_table]
    kcvc = kcvc.reshape(DB, N_PAGES, 2, n_blk, HD).transpose(0, 1, 3, 2, 4).reshape(DB, N_PAGES * n_blk, 2 * HD)
    stack = lambda x: x.reshape(DB, DEC_SEQ, NSA_HEADS, HD).transpose(0, 2, 1, 3).reshape(DB, DEC_ROWS, HD)
    qs = stack(nq).astype(bf16)
    qf = jnp.pad(stack(nq), ((0, 0), (0, 0), (0, HD))).astype(f32)
    win_t = cache_win.transpose(0, 1, 3, 4, 2).reshape(cache_win.shape[0] * DB, 2 * HD, cache_win.shape[2])
    colk = np.arange(DEC_KEYS)
    e = (np.arange(128)[:, None] == colk[None, :] // NSA_BLOCK) & (colk[None, :] < PAST_LEN + NSA_BLOCK)
    wb = cache_win.shape[2]
    per_seq = lambda r, w: pl.BlockSpec((1, r, w), lambda b, pt: (b, 0, 0))
    o = pl.pallas_call(
        _nsa_decode_kernel,
        grid_spec=pltpu.PrefetchScalarGridSpec(
            num_scalar_prefetch=1, grid=(DB,),
            in_specs=[per_seq(DEC_ROWS, HD), per_seq(DEC_ROWS, 2 * HD), per_seq(PAST_LEN // NSA_BLOCK, 2 * HD),
                      per_seq(2 * HD, NEW_PAD),
                      pl.BlockSpec((1, 2 * HD, wb), lambda b, pt: (layer * DB + b, 0, 0)),
                      per_seq(2 * HD, NEW_PAD), per_seq(DEC_SEQ, 128),
                      pl.BlockSpec((128, DEC_KEYS), lambda b, pt: (0, 0))]
                     + _page_specs(2 * HD, 1, layer, n_phys),
            out_specs=per_seq(DEC_ROWS, HD)),
        out_shape=jax.ShapeDtypeStruct((DB, DEC_ROWS, HD), f32),
        compiler_params=pltpu.CompilerParams(dimension_semantics=("parallel",), vmem_limit_bytes=VMEM_LIMIT_BYTES),
        name="nsa_decode",
    )(page_table.reshape(-1), qs, qf, kcvc, _pad_new_t(rows_new[..., 2 * HD:]), win_t,
      _pad_new_t(win_new), gates, jnp.asarray(e, bf16), *([pages] * N_PAGES))
    return o.reshape(DB, NSA_HEADS, DEC_SEQ, HD).transpose(0, 2, 1, 3).reshape(DB, DEC_SEQ, BRANCH_WIDTH)


def _stack_layers(states):
    return tuple(jnp.stack(list(a)) for a in zip(*states))


def kernel(x_prompt, x_sample, cache_nsa, cache_nsa_win, cache_fox_kv, cache_fox_logf,
           state_mlstm_c, state_mlstm_n, state_mlstm_m, page_table,
           ln_in_w, ln_in_b, w_in, b_in, nsa_w_ck, nsa_w_cv, mlstm_norm_w, w_branch, w_out,
           ln1_w, ln1_b, moe_w_group, moe_b_group, moe_w_expert, moe_b_expert,
           moe_w_gate, moe_w_up, moe_w_down, ln2_w, ln2_b):
    f32, bf16 = jnp.float32, jnp.bfloat16
    alpha = (2.0 * DEPTH) ** 0.25
    B, S, D = x_prompt.shape
    DB, T, _ = x_sample.shape
    TP, TS = B * S, DB * T
    HD = HEAD_DIM
    xp = _layer_norm_rows(x_prompt.reshape(TP, D), ln_in_w, ln_in_b)
    xs = _layer_norm_rows(x_sample.reshape(TS, D), ln_in_w, ln_in_b)
    cols = lambda a, c0, w: a[..., c0:c0 + w]
    colsb = lambda a, c0, w: a[..., c0 - Z_NQ:c0 - Z_NQ + w]
    zero_state = (jnp.zeros((B, MLSTM_HEADS, HD, HD), f32), jnp.zeros((B, MLSTM_HEADS, HD), f32),
                  jnp.zeros((B, MLSTM_HEADS), f32))
    new_p, new_s = [], []
    for l in range(DEPTH):
        zp2, zbp2 = _in_projection(xp, w_in[l], b_in[l])
        zs2, zbs2 = _in_projection(xs, w_in[l], b_in[l])
        zp, zbp = zp2.reshape(B, S, Z_WIDTH), zbp2.reshape(B, S, ZB_WIDTH)
        zs, zbs = zs2.reshape(DB, T, Z_WIDTH), zbs2.reshape(DB, T, ZB_WIDTH)
        small_p, small_s = cols(zp, Z_SMALL, 128), cols(zs, Z_SMALL, 128)

        o_nsa_p = _nsa_prompt(cols(zp, Z_NQ, 256), colsb(zbp, Z_NQ, 256), cols(zp, Z_ROWS, 256),
                              colsb(zbp, Z_ROWS, 256), colsb(zbp, Z_WIN, 128), small_p, nsa_w_ck[l], nsa_w_cv[l])
        o_nsa_s = _nsa_decode(cols(zs, Z_NQ, 256), colsb(zbs, Z_ROWS, 256), colsb(zbs, Z_WIN, 128), small_s,
                              cache_nsa, cache_nsa_win, page_table, nsa_w_ck[l], nsa_w_cv[l], l)

        def mlstm(q, k, v, og, small, state, L, nb, dt):
            return _mlstm(q, k, v, og, cols(small, SMALL_MI, MLSTM_HEADS), cols(small, SMALL_MF, MLSTM_HEADS),
                          mlstm_norm_w[l], *state, L, nb, dt)

        o_ml_p, st_ml_p = mlstm(colsb(zbp, Z_MQ, 256), colsb(zbp, Z_MK, 256), colsb(zbp, Z_MV, 256),
                                cols(zp, Z_MO, 256), small_p, zero_state, 128, B, bf16)
        o_ml_s, st_ml_s = mlstm(cols(zs, Z_MQ, 256), cols(zs, Z_MK, 256), cols(zs, Z_MV, 256), cols(zs, Z_MO, 256),
                                small_s, (state_mlstm_c[l], state_mlstm_n[l], state_mlstm_m[l]), T, 4, f32)

        lf_p = jax.nn.log_sigmoid(cols(small_p, SMALL_FF, FOX_HEADS))
        lf_s = jax.nn.log_sigmoid(cols(small_s, SMALL_FF, FOX_HEADS))
        o_fox_p = _fox_prompt_attn(zbp, jnp.cumsum(lf_p, axis=1))
        o_fox_s = _fox_decode(colsb(zbs, Z_FQ, 256), colsb(zbs, Z_FK, 256), colsb(zbs, Z_FV, 256), lf_s,
                              cache_fox_kv, cache_fox_logf, page_table, l)

        flat = lambda a: a.reshape(-1, a.shape[-1])
        moe_w = (moe_w_group[l], moe_b_group[l], moe_w_expert[l], moe_b_expert[l],
                 moe_w_gate[l], moe_w_up[l], moe_w_down[l], ln2_w[l], ln2_b[l], alpha)
        xp = _merge_ln(xp, flat(o_nsa_p), flat(o_ml_p), flat(o_fox_p), w_in[l], b_in[l],
                       w_branch[l], w_out[l], ln1_w[l], ln1_b[l], alpha)
        xp = _moe_ln(xp, *moe_w)
        xs = _merge_ln(xs, flat(o_nsa_s), flat(o_ml_s), flat(o_fox_s), w_in[l], b_in[l],
                       w_branch[l], w_out[l], ln1_w[l], ln1_b[l], alpha)
        xs = _moe_ln(xs, *moe_w)

        w_keep = min(NSA_WINDOW, S)
        new_p.append((cols(zp, Z_ROWS, 256).reshape(B, S, NSA_ROWS, HD),
                      cols(zp, Z_WIN, 128)[:, S - w_keep:].reshape(B, w_keep, 2, HD),
                      cols(zp, Z_FK, 512).reshape(B, S, 2, FOX_HEADS, HD), lf_p) + tuple(st_ml_p))
        win_new = cols(zs, Z_WIN, 128).reshape(DB, T, 2, HD).astype(cache_nsa_win.dtype)
        new_s.append((cols(zs, Z_ROWS, 256).reshape(DB, T, NSA_ROWS, HD),
                      win_new,
                      cols(zs, Z_FK, 512).reshape(DB, T, 2, FOX_HEADS, HD), lf_s) + tuple(st_ml_s))
    (p_nsa_rows, p_nsa_win, p_fox_kv, p_fox_logf, p_mlstm_c, p_mlstm_n, p_mlstm_m) = _stack_layers(new_p)
    (s_nsa_rows, s_win_new, s_fox_kv, s_fox_logf, s_mlstm_c, s_mlstm_n, s_mlstm_m) = _stack_layers(new_s)
    s_nsa_win = jnp.concatenate([cache_nsa_win[:, :, T:], s_win_new], axis=2)
    return (xp.reshape(B, S, D), xs.reshape(DB, T, D),
            p_nsa_rows, p_nsa_win, p_fox_kv, p_fox_logf, p_mlstm_c, p_mlstm_n, p_mlstm_m,
            s_nsa_rows, s_nsa_win, s_fox_kv, s_fox_logf, s_mlstm_c, s_mlstm_n, s_mlstm_m)
```

```python
import functools

import jax
import jax.numpy as jnp
import numpy as np
from jax import lax
from jax.experimental import pallas as pl
from jax.experimental.pallas import tpu as pltpu

D_MODEL = 1024
DEPTH = 2
DEC_SEQ = 8
PAST_LEN = 2048
PAGE_SIZE = 128

HEAD_DIM = 64
NSA_HEADS = 4
NSA_BLOCK = 64
NSA_TOPK = 16
NSA_WINDOW = 512
NSA_ROWS = 4
MLSTM_HEADS = 4
FOX_HEADS = 4
N_BRANCH = 3
BRANCH_WIDTH = NSA_HEADS * HEAD_DIM
N_GROUPS = 4
EXPERTS_PER_GROUP = 4
N_EXPERTS = N_GROUPS * EXPERTS_PER_GROUP
D_EXPERT = 256
LN_EPS = 1e-5
NEG_INF = -1e30
TINY = 1e-30
ATTN_SCALE = HEAD_DIM ** -0.5

IN_SPLITS = (
    NSA_HEADS * HEAD_DIM,
    6 * HEAD_DIM,
    NSA_HEADS * 3,
    MLSTM_HEADS * HEAD_DIM,
    MLSTM_HEADS * HEAD_DIM,
    MLSTM_HEADS * HEAD_DIM,
    MLSTM_HEADS * HEAD_DIM,
    MLSTM_HEADS,
    MLSTM_HEADS,
    FOX_HEADS * HEAD_DIM,
    FOX_HEADS * HEAD_DIM,
    FOX_HEADS * HEAD_DIM,
    FOX_HEADS,
    N_BRANCH * D_MODEL,
)


VMEM_LIMIT_BYTES = 48 * 1024 * 1024
NSA_SLOPES = tuple(2.0 ** (-8.0 * (h + 1) / NSA_HEADS) for h in range(NSA_HEADS))
_NT = (((1,), (1,)), ((), ()))
_HI = lax.Precision.HIGHEST


def _sigmoid(x):
    return 1.0 / (1.0 + jnp.exp(-x))


def _tile_rows(x, n):
    return jnp.concatenate([x] * n, axis=0)


def _compress_kernel(x_ref, w_ref, o_ref, acc_ref):
    k = pl.program_id(1)

    @pl.when(k == 0)
    def _():
        acc_ref[...] = jnp.zeros_like(acc_ref)

    acc_ref[...] += jnp.dot(x_ref[...], w_ref[...], preferred_element_type=jnp.float32, precision=_HI)

    @pl.when(k == pl.num_programs(1) - 1)
    def _():
        o_ref[...] = acc_ref[...]


def _compress_weights(w_ck, w_cv):
    z = jnp.zeros_like(w_ck)
    wk = jnp.stack([w_ck, z, z, z], axis=1)
    wv = jnp.stack([z, w_cv, z, z], axis=1)
    return jnp.concatenate([wk, wv], axis=-1).reshape(NSA_BLOCK * NSA_ROWS * HEAD_DIM, 2 * HEAD_DIM)


def _nsa_compress_blocks(blocks, w_big, tm=256, tk=2048):
    n, kdim = blocks.shape
    tm = min(tm, n)
    return pl.pallas_call(
        _compress_kernel,
        grid=(n // tm, kdim // tk),
        in_specs=[pl.BlockSpec((tm, tk), lambda i, k: (i, k)),
                  pl.BlockSpec((tk, 2 * HEAD_DIM), lambda i, k: (k, 0))],
        out_specs=pl.BlockSpec((tm, 2 * HEAD_DIM), lambda i, k: (i, 0)),
        out_shape=jax.ShapeDtypeStruct((n, 2 * HEAD_DIM), jnp.float32),
        scratch_shapes=[pltpu.VMEM((tm, 2 * HEAD_DIM), jnp.float32)],
        name="nsa_compress",
    )(blocks, w_big)


def _select_blocks(imp, cur, n_pick):
    q, n_sb = imp.shape
    jq = lax.broadcasted_iota(jnp.int32, (q, n_sb), 1)
    valid = jq <= cur
    forced = valid & ((jq == cur) | (jq == 0))
    work0 = jnp.where(forced, -jnp.inf, jnp.where(valid, imp, NEG_INF))

    jf = jq.astype(jnp.float32)

    def pick(_, carry):
        work, sel = carry
        mx = jnp.max(work, axis=-1, keepdims=True)
        hit = jf == jnp.min(jnp.where(work == mx, jf, float(n_sb)), axis=-1, keepdims=True)
        sel = jnp.where(hit & valid, 1.0, sel)
        work = jnp.where(hit, -jnp.inf, work)
        return work, sel

    _, sel = lax.fori_loop(0, n_pick - 2, pick, (work0, forced.astype(jnp.float32)), unroll=True)
    return sel


def _select_blocks_by_rank(imp, cur, n_pick, n_cand):
    q, n_sb = imp.shape
    jq = lax.broadcasted_iota(jnp.int32, (q, n_sb), 1)
    valid = jq <= cur
    forced = valid & ((jq == cur) | (jq == 0))
    work = jnp.where(forced, -jnp.inf, jnp.where(valid, imp, NEG_INF))
    ahead = jnp.zeros((q, n_sb), jnp.float32)
    for i in range(n_cand):
        wi = work[:, i:i + 1]
        ahead += ((wi > work) | ((wi == work) & (jq > i))).astype(jnp.float32)
    return (forced | (valid & (ahead < n_pick - 2))).astype(jnp.float32)


def _nsa_prompt_kernel(qs_ref, qf_ref, kc_ref, vc_ref, ksel_ref, vsel_ref, kwin_ref, vwin_ref, g_ref,
                       o_ref, flag_ref, *, tq, tk, tw):
    i = pl.program_id(1)
    f32, bf16 = jnp.float32, jnp.bfloat16
    H = NSA_HEADS
    R = H * tq
    qs = qs_ref[0].reshape(R, 2 * HEAD_DIM)
    qf = qf_ref[0].reshape(R, HEAD_DIM)
    row = lax.broadcasted_iota(jnp.int32, (R, 1), 0)
    head = row // tq
    slope = jnp.where(head == 0, NSA_SLOPES[0], jnp.where(head == 1, NSA_SLOPES[1],
                      jnp.where(head == 2, NSA_SLOPES[2], NSA_SLOPES[3]))).astype(f32)
    posq = i * tq + (row - head * tq)

    n_cb = kc_ref.shape[1]
    sc = lax.dot_general(qf, kc_ref[0], _NT, precision=_HI, preferred_element_type=f32)
    jb = lax.broadcasted_iota(jnp.int32, (R, n_cb), 1)
    distc = posq - (jb * NSA_BLOCK + NSA_BLOCK - 1)
    okc = distc >= 0
    sc = jnp.where(okc, sc - slope * distc.astype(f32), NEG_INF)
    pc = jnp.exp(sc - jnp.max(sc, axis=-1, keepdims=True)) * okc.astype(f32)
    pc = pc / jnp.maximum(jnp.sum(pc, axis=-1, keepdims=True), TINY)
    o_cmp = jnp.dot(pc.astype(bf16), vc_ref[0].astype(bf16), preferred_element_type=f32)
    imp = pc[0:tq] + pc[tq:2 * tq] + pc[2 * tq:3 * tq] + pc[3 * tq:4 * tq]

    pq = i * tq + lax.broadcasted_iota(jnp.int32, (tq, 1), 0)
    msel = _select_blocks(imp, pq // NSA_BLOCK, NSA_TOPK)

    bpt = tk // NSA_BLOCK
    blk_any = jnp.max(msel, axis=0, keepdims=True)
    for j in range(n_cb // bpt):
        flag_ref[j] = (jnp.max(blk_any[:, j * bpt:(j + 1) * bpt]) > 0.5).astype(jnp.int32)
    mbias = (NEG_INF * (1.0 - msel)).astype(bf16)

    rowpos = i * tq + lax.broadcasted_iota(jnp.int32, (tq, 1), 0)

    def attend(carry, s, v1):
        m, acc = carry
        m_new = jnp.maximum(m, jnp.max(s, axis=-1, keepdims=True))
        p = jnp.exp(s - m_new).astype(bf16)
        return m_new, jnp.exp(m - m_new) * acc + jnp.dot(p, v1, preferred_element_type=f32)

    init = (jnp.full((R, 1), NEG_INF, f32), jnp.zeros((R, 2 * HEAD_DIM), f32))

    def sel_tile(j, carry, causal):
        k0 = pl.multiple_of(j * tk, tk)
        s = lax.dot_general(qs, ksel_ref[0, pl.ds(k0, tk), :], _NT, preferred_element_type=f32)
        eb = (lax.broadcasted_iota(jnp.int32, (n_cb, tk), 0)
              == j * bpt + lax.broadcasted_iota(jnp.int32, (n_cb, tk), 1) // NSA_BLOCK)
        bias = jnp.dot(mbias, eb.astype(bf16), preferred_element_type=f32)
        if causal:
            d = rowpos - (k0 + lax.broadcasted_iota(jnp.int32, (tq, tk), 1))
            bias = jnp.where(d >= 0, bias, NEG_INF)
        return attend(carry, s + _tile_rows(bias, H), vsel_ref[0, pl.ds(k0, tk), :])

    def sel_body(j, carry):
        return lax.cond(flag_ref[j] > 0, lambda c: sel_tile(j, c, False), lambda c: c, carry)

    n_sel = (i * tq + tq - 1) // tk + 1
    carry = lax.fori_loop(0, n_sel - 1, sel_body, init)
    _, a_sel = sel_tile(n_sel - 1, carry, True)
    o_sel = a_sel[:, :HEAD_DIM] / a_sel[:, HEAD_DIM:]

    nw = NSA_WINDOW + tq
    w0 = pl.multiple_of(jnp.maximum(i * tq - NSA_WINDOW, 0), tw)
    sw = lax.dot_general(qs, kwin_ref[0, pl.ds(w0, nw), :], _NT, preferred_element_type=f32)
    dw = rowpos - (w0 + lax.broadcasted_iota(jnp.int32, (tq, nw), 1))
    okw = (dw >= 0) & (dw < NSA_WINDOW)
    sw = jnp.where(_tile_rows(okw, H), sw, NEG_INF)
    _, a_win = attend(init, sw, vwin_ref[0, pl.ds(w0, nw), :])
    o_win = a_win[:, :HEAD_DIM] / a_win[:, HEAD_DIM:]

    g = _sigmoid(g_ref[...])
    gate = lambda c: jnp.concatenate([g[:, 3 * h + c:3 * h + c + 1] for h in range(H)], axis=0)
    o = gate(0) * o_cmp + gate(1) * o_sel + gate(2) * o_win
    o_ref[0] = o.reshape(H, tq, HEAD_DIM)


def _heads_major(x, dtype):
    B, S, _ = x.shape
    return x.reshape(B, S, -1, HEAD_DIM).transpose(0, 2, 1, 3).astype(dtype)


def _nsa_prompt(nq, nqb, rows, rowsb, winb, gates, w_ck, w_cv, tq=512, tk=512, tw=128):
    B, S, _ = nq.shape
    bf16 = jnp.bfloat16
    kcvc = _nsa_compress_blocks(rows.reshape(B * S // NSA_BLOCK, -1), _compress_weights(w_ck, w_cv))
    kcvc = kcvc.reshape(B, S // NSA_BLOCK, 2 * HEAD_DIM)
    kc, vc = kcvc[..., :HEAD_DIM], kcvc[..., HEAD_DIM:]
    pos = jnp.arange(S)
    blk, off = (pos // NSA_BLOCK).astype(jnp.float32), (pos % NSA_BLOCK).astype(jnp.float32)
    one = jnp.ones((S,), jnp.float32)
    slopes = jnp.asarray(NSA_SLOPES, jnp.float32)[:, None]
    q_terms = jnp.stack([-slopes * NSA_BLOCK * blk, -slopes * off, slopes * NSA_BLOCK * one, slopes * one], axis=-1)
    k_terms = jnp.stack([one, one, blk, off], axis=-1)
    pad_terms = lambda t: jnp.pad(t, [(0, 0)] * (t.ndim - 1) + [(0, HEAD_DIM - 4)]).astype(bf16)
    with_k_terms = lambda k: jnp.concatenate([k, jnp.broadcast_to(pad_terms(k_terms), (B, S, HEAD_DIM))], axis=-1)
    qs = jnp.concatenate([_heads_major(nqb, bf16),
                          jnp.broadcast_to(pad_terms(q_terms), (B, NSA_HEADS, S, HEAD_DIM))], axis=-1)
    qf = _heads_major(nq, jnp.float32)
    rows, win = rowsb, winb
    col = lambda a, c: a[..., c * HEAD_DIM:(c + 1) * HEAD_DIM]
    with_ones = lambda v: jnp.concatenate([v, jnp.ones_like(v)], axis=-1)
    full = lambda n, w: pl.BlockSpec((1, n, w), lambda b, i: (b, 0, 0))
    qspec = pl.BlockSpec((1, NSA_HEADS, tq, HEAD_DIM), lambda b, i: (b, 0, i, 0))
    n_cb = S // NSA_BLOCK
    o = pl.pallas_call(
        functools.partial(_nsa_prompt_kernel, tq=tq, tk=tk, tw=tw),
        grid=(B, S // tq),
        in_specs=[pl.BlockSpec((1, NSA_HEADS, tq, 2 * HEAD_DIM), lambda b, i: (b, 0, i, 0)), qspec,
                  full(n_cb, HEAD_DIM), full(n_cb, HEAD_DIM), full(S, 2 * HEAD_DIM),
                  full(S, 2 * HEAD_DIM), full(S, 2 * HEAD_DIM), full(S, 2 * HEAD_DIM),
                  pl.BlockSpec((None, tq, 128), lambda b, i: (b, i, 0))],
        out_specs=qspec,
        out_shape=jax.ShapeDtypeStruct((B, NSA_HEADS, S, HEAD_DIM), jnp.float32),
        scratch_shapes=[pltpu.SMEM((S // tk,), jnp.int32)],
        compiler_params=pltpu.CompilerParams(dimension_semantics=("parallel", "arbitrary"),
                                             vmem_limit_bytes=VMEM_LIMIT_BYTES),
        name="nsa_prompt",
    )(qs, qf, kc, vc, with_k_terms(col(rows, 2)), with_ones(col(rows, 3)),
      with_k_terms(col(win, 0)), with_ones(col(win, 1)), gates)
    return o.transpose(0, 2, 1, 3).reshape(B, S, BRANCH_WIDTH)


def _fox_prompt_kernel(q_ref, k_ref, v_ref, fq_ref, fk_ref, o_ref, *, tq, t):
    i = pl.program_id(1)
    f32, bf16 = jnp.float32, jnp.bfloat16
    lane = lax.broadcasted_iota(jnp.int32, (1, 2 * HEAD_DIM), 1)
    low = lane < HEAD_DIM
    rc = lax.broadcasted_iota(jnp.int32, (tq, t), 0) - lax.broadcasted_iota(jnp.int32, (tq, t), 1)
    n_pairs = FOX_HEADS // 2
    pair_cols = [slice(hp * 2 * HEAD_DIM, (hp + 1) * 2 * HEAD_DIM) for hp in range(n_pairs)]
    q_h, fq_h = [], []
    for hp in range(n_pairs):
        q2 = q_ref[0, :, pair_cols[hp]]
        zq = jnp.zeros_like(q2)
        q_h += [jnp.where(low, q2, zq), jnp.where(low, zq, q2)]
        fq_h += [fq_ref[0, :, 2 * hp:2 * hp + 1], fq_ref[0, :, 2 * hp + 1:2 * hp + 2]]

    def body(j, carry, diag_offset):
        k0 = pl.multiple_of(j * t, t)
        fk = fk_ref[0, j]
        out = []
        for hp in range(n_pairs):
            k2 = k_ref[0, pl.ds(k0, t), pair_cols[hp]]
            v2 = v_ref[0, pl.ds(k0, t), pair_cols[hp]]
            one = jnp.ones_like(v2)
            v1 = (jnp.where(low, v2, one), jnp.where(low, one, v2))
            for x in range(2):
                h = 2 * hp + x
                m, acc = carry[h]
                s = lax.dot_general(q_h[h], k2, _NT, preferred_element_type=f32) + (fq_h[h] - fk[h:h + 1])
                if diag_offset is not None:
                    s = jnp.where(rc >= diag_offset, s, NEG_INF)
                m_new = jnp.maximum(m, jnp.max(s, axis=-1, keepdims=True))
                p = jnp.exp(s - m_new).astype(bf16)
                acc = jnp.exp(m - m_new) * acc + jnp.dot(p, v1[x], preferred_element_type=f32)
                out.append((m_new, acc))
        return tuple(out)

    init = tuple((jnp.full((tq, 1), NEG_INF, f32), jnp.zeros((tq, 2 * HEAD_DIM), f32)) for _ in range(FOX_HEADS))
    n_full = i * (tq // t)
    carry = lax.fori_loop(0, n_full, functools.partial(body, diag_offset=None), init)
    for jj in range(tq // t):
        carry = body(n_full + jj, carry, jj * t)
    for hp in range(n_pairs):
        acc_a, acc_b = carry[2 * hp][1], carry[2 * hp + 1][1]
        den = pltpu.roll(jnp.where(low, acc_b, acc_a), HEAD_DIM, axis=1)
        o_ref[0, :, pair_cols[hp]] = jnp.where(low, acc_a, acc_b) / den


def _fox_prompt_attn(zb, F, tq=1024, t=1024):
    B, S, _ = zb.shape
    W = FOX_HEADS * HEAD_DIM
    cq, ck, cv = ((c - Z_NQ) // W for c in (Z_FQ, Z_FK, Z_FV))
    fk = jnp.pad(F.transpose(0, 2, 1), ((0, 0), (0, 8 - FOX_HEADS), (0, 0)))
    fk = fk.reshape(B, 8, S // t, t).transpose(0, 2, 1, 3)
    return pl.pallas_call(
        functools.partial(_fox_prompt_kernel, tq=tq, t=t),
        grid=(B, S // tq),
        in_specs=[pl.BlockSpec((1, tq, W), lambda b, i: (b, i, cq)),
                  pl.BlockSpec((1, S, W), lambda b, i: (b, 0, ck)),
                  pl.BlockSpec((1, S, W), lambda b, i: (b, 0, cv)),
                  pl.BlockSpec((1, tq, FOX_HEADS), lambda b, i: (b, i, 0)),
                  pl.BlockSpec((1, S // t, 8, t), lambda b, i: (b, 0, 0, 0))],
        out_specs=pl.BlockSpec((1, tq, W), lambda b, i: (b, i, 0)),
        out_shape=jax.ShapeDtypeStruct((B, S, W), jnp.float32),
        compiler_params=pltpu.CompilerParams(dimension_semantics=("parallel", "arbitrary"),
                                             vmem_limit_bytes=VMEM_LIMIT_BYTES),
        name="fox_prompt",
    )(zb, zb, zb, F, fk)


def _log_sigmoid(x):
    return jnp.minimum(x, 0.0) - jnp.log1p(jnp.exp(-jnp.abs(x)))


def _mlstm_kernel(q_ref, k_ref, kt_ref, v_ref, og_ref, gc_ref, gr_ref, nw_ref, cn0_ref, m0_ref,
                  h_ref, cn_ref, m_ref, cn_s, m_s, *, nb, L):
    c = pl.program_id(1)
    f32 = jnp.float32
    W = 2 * HEAD_DIM
    n_pairs = MLSTM_HEADS // 2

    @pl.when(c == 0)
    def _():
        cn_s[...] = cn0_ref[...]
        m_s[...] = m0_ref[...]

    lane = lax.broadcasted_iota(jnp.int32, (1, W), 1)
    low = lane < HEAD_DIM
    ti = lax.broadcasted_iota(jnp.int32, (L, L), 0)
    si = lax.broadcasted_iota(jnp.int32, (L, L), 1)
    causal = si <= ti
    tri = causal.astype(f32)
    tri_t = (ti <= si).astype(f32)
    srow = lax.broadcasted_iota(jnp.int32, (W, 2 * W), 0)
    slane = lax.broadcasted_iota(jnp.int32, (W, 2 * W), 1)
    top = srow < HEAD_DIM
    keep_a = top & ((slane < HEAD_DIM) | (slane == W))
    keep_b = (~top) & (((slane >= HEAD_DIM) & (slane < W)) | (slane == W + 1))
    lane_w = lax.broadcasted_iota(jnp.int32, (1, W), 1)
    mdt = k_ref.dtype

    for b in range(nb):
        gcol = gc_ref[b]
        grow = gr_ref[b]
        bcol = jnp.dot(tri, _log_sigmoid(gcol), precision=_HI, preferred_element_type=f32)
        brow = jnp.dot(_log_sigmoid(grow), tri_t, precision=_HI, preferred_element_type=f32)
        for hp in range(n_pairs):
            cols = slice(hp * W, (hp + 1) * W)
            q2 = q_ref[b, :, cols]
            k2 = k_ref[b, :, cols]
            v2 = v_ref[b, :, cols]
            kt2 = kt_ref[b, cols, :]
            cn = cn_s[b, hp]
            r = jnp.dot(q2, cn.astype(mdt), preferred_element_type=f32)
            zq = jnp.zeros_like(q2)
            per_head = []
            for x in range(2):
                h = 2 * hp + x
                qx = jnp.where(low, q2, zq) if x == 0 else jnp.where(low, zq, q2)
                b_c = bcol[:, MLSTM_HEADS + h:MLSTM_HEADS + h + 1]
                b_r = brow[MLSTM_HEADS + h:MLSTM_HEADS + h + 1, :]
                ig_c = gcol[:, h:h + 1]
                ig_r = grow[h:h + 1, :]
                m_prev = m_s[b, 0:1, h:h + 1]
                dmat = jnp.where(causal, b_c - b_r + ig_r, NEG_INF)
                a_c = b_c + m_prev
                m_t = jnp.maximum(a_c, jnp.max(dmat, axis=-1, keepdims=True))
                wq = jnp.exp(dmat - m_t) * lax.dot_general(qx, k2, _NT, preferred_element_type=f32)
                inter = jnp.exp(a_c - m_t)
                wv = jnp.dot(wq.astype(mdt), v2, preferred_element_type=f32)
                den = inter * r[:, W + x:W + x + 1] + jnp.sum(wq, axis=-1, keepdims=True)
                den = jnp.maximum(jnp.abs(den), jnp.exp(-m_t))
                bl = b_c[L - 1:L, :]
                g_c = bl - b_c + ig_c
                m_new = jnp.maximum(bl + m_prev, jnp.max(g_c, axis=0, keepdims=True))
                ws = jnp.exp(g_c - m_new)
                decay = jnp.exp(bl + m_prev - m_new)
                aug = jnp.concatenate([v2.astype(f32) * ws, jnp.where(lane_w == x, ws, 0.0)], axis=1)
                u = jnp.dot(kt2, aug.astype(mdt), preferred_element_type=f32)
                per_head.append((inter, wv, den, decay, u))
                m_s[b, 0:1, h:h + 1] = m_new
            (ia, wva, dena, deca, ua), (ib, wvb, denb, decb, ub) = per_head
            num = jnp.where(low, ia * r[:, :W] + wva, ib * r[:, :W] + wvb)
            hid = num / jnp.where(low, dena, denb)
            cn_s[b, hp] = (jnp.where(top, deca, decb) * cn + jnp.where(keep_a, ua, 0.0)
                           + jnp.where(keep_b, ub, 0.0))
            hid = _sigmoid(og_ref[b, :, cols]) * hid
            mean = lambda t: jnp.where(low, jnp.sum(jnp.where(low, t, 0.0), axis=-1, keepdims=True),
                                       jnp.sum(jnp.where(low, 0.0, t), axis=-1, keepdims=True)) / HEAD_DIM
            mu = mean(hid)
            var = mean(jnp.square(hid - mu))
            h_ref[b, :, cols] = (hid - mu) * lax.rsqrt(var + LN_EPS) * nw_ref[:, cols]

    @pl.when(c == pl.num_programs(1) - 1)
    def _():
        cn_ref[...] = cn_s[...]
        m_ref[...] = m_s[...]


def _mlstm_state_pack(c, n, m):
    Bx = c.shape[0]
    HD, W = HEAD_DIM, 2 * HEAD_DIM
    cp = c.reshape(Bx, MLSTM_HEADS // 2, 2, HD, HD)
    np_ = n.reshape(Bx, MLSTM_HEADS // 2, 2, HD, 1)
    z = lambda w: jnp.zeros((Bx, MLSTM_HEADS // 2, HD, w), jnp.float32)
    top = jnp.concatenate([cp[:, :, 0], z(HD), np_[:, :, 0], z(W - 1)], axis=-1)
    bot = jnp.concatenate([z(HD), cp[:, :, 1], z(1), np_[:, :, 1], z(W - 2)], axis=-1)
    cn = jnp.concatenate([top, bot], axis=-2)
    m8 = jnp.pad(m[:, None, :], ((0, 0), (0, 7), (0, 128 - MLSTM_HEADS)))
    return cn, m8


def _mlstm_state_unpack(cn, m8):
    W = 2 * HEAD_DIM
    c = jnp.stack([cn[:, h // 2, (h % 2) * HEAD_DIM:(h % 2 + 1) * HEAD_DIM,
                      (h % 2) * HEAD_DIM:(h % 2 + 1) * HEAD_DIM] for h in range(MLSTM_HEADS)], axis=1)
    n = jnp.stack([cn[:, h // 2, (h % 2) * HEAD_DIM:(h % 2 + 1) * HEAD_DIM, W + h % 2]
                   for h in range(MLSTM_HEADS)], axis=1)
    return c, n, m8[:, 0, :MLSTM_HEADS]


def _mlstm(q, k, v, og, mi, mf, norm_w, c0, n0, m0, L, nb, mxu_dtype):
    Bx, S, W4 = q.shape
    f32 = jnp.float32
    gates = jnp.concatenate([mi, mf], axis=-1).astype(f32)
    gcol = jnp.pad(gates, ((0, 0), (0, 0), (0, 128 - 2 * MLSTM_HEADS)))
    grow = gates.transpose(0, 2, 1)
    cn0, m8 = _mlstm_state_pack(c0.astype(f32), n0.astype(f32), m0.astype(f32))
    tok = lambda w: pl.BlockSpec((nb, L, w), lambda b, c: (b, c, 0))
    st_cn = pl.BlockSpec((nb, MLSTM_HEADS // 2, 128, 256), lambda b, c: (b, 0, 0, 0))
    st_m = pl.BlockSpec((nb, 8, 128), lambda b, c: (b, 0, 0))
    h, cn, m8 = pl.pallas_call(
        functools.partial(_mlstm_kernel, nb=nb, L=L),
        grid=(Bx // nb, S // L),
        in_specs=[tok(W4), tok(W4), pl.BlockSpec((nb, W4, L), lambda b, c: (b, 0, c)), tok(W4), tok(W4),
                  tok(128), pl.BlockSpec((nb, 8, L), lambda b, c: (b, 0, c)),
                  pl.BlockSpec((1, W4), lambda b, c: (0, 0)), st_cn, st_m],
        out_specs=[tok(W4), st_cn, st_m],
        out_shape=[jax.ShapeDtypeStruct((Bx, S, W4), f32),
                   jax.ShapeDtypeStruct(cn0.shape, f32), jax.ShapeDtypeStruct(m8.shape, f32)],
        scratch_shapes=[pltpu.VMEM((nb, MLSTM_HEADS // 2, 128, 256), f32), pltpu.VMEM((nb, 8, 128), f32)],
        compiler_params=pltpu.CompilerParams(dimension_semantics=("parallel", "arbitrary"),
                                             vmem_limit_bytes=VMEM_LIMIT_BYTES),
        name="mlstm",
    )(q.astype(mxu_dtype), k.astype(mxu_dtype), k.astype(mxu_dtype).transpose(0, 2, 1), v.astype(mxu_dtype),
      og.astype(f32), gcol, grow, norm_w.astype(f32).reshape(1, W4), cn0, m8)
    return h, _mlstm_state_unpack(cn, m8)


def _ln(x, w, b):
    mu = jnp.mean(x, axis=-1, keepdims=True)
    var = jnp.mean(jnp.square(x - mu), axis=-1, keepdims=True)
    return (x - mu) * lax.rsqrt(var + LN_EPS) * w + b


def _ln_kernel(x_ref, w_ref, b_ref, o_ref):
    o_ref[...] = _ln(x_ref[...], w_ref[...], b_ref[...])


def _layer_norm_rows(x, w, b, tm=1024):
    T, D = x.shape
    vec = pl.BlockSpec((1, D), lambda i: (0, 0))
    return pl.pallas_call(
        _ln_kernel, grid=(T // tm,),
        in_specs=[pl.BlockSpec((tm, D), lambda i: (i, 0)), vec, vec],
        out_specs=pl.BlockSpec((tm, D), lambda i: (i, 0)),
        out_shape=jax.ShapeDtypeStruct((T, D), jnp.float32),
        name="layer_norm",
    )(x, w.reshape(1, D), b.reshape(1, D))


Z_NQ = 0
Z_ROWS = Z_NQ + 256
Z_MQ = 512
Z_MK = Z_MQ + 256
Z_MV = 1024
Z_MO = Z_MV + 256
Z_FQ = 1536
Z_WIN = Z_FQ + 256
Z_SMALL = Z_WIN + 128
Z_FK = 2048
Z_FV = Z_FK + 256
Z_WIDTH = 2560
Z_TILE = 512
ZB_WIDTH = Z_WIDTH
GM_SPLIT = 13
SMALL_MI, SMALL_MF, SMALL_FF = 12, 16, 20


def _z_column_order():
    starts = np.concatenate([[0], np.cumsum(IN_SPLITS)])
    seg = lambda i, lo=0, hi=None: np.arange(starts[i] + lo, starts[i] + (IN_SPLITS[i] if hi is None else hi))
    pad = lambda n: np.full((n,), -1)
    order = np.concatenate([
        seg(0), seg(1, 0, 256), seg(3), seg(4), seg(5), seg(6),
        seg(9), seg(1, 256, 384), seg(2), seg(7), seg(8), seg(12), pad(128 - 24), seg(10), seg(11)])
    assert order.shape == (Z_WIDTH,)
    return order


def _permute_in_proj(w_in, b_in):
    order = _z_column_order()
    valid = jnp.asarray(order >= 0)
    idx = jnp.asarray(np.maximum(order, 0))
    scale = np.ones((Z_WIDTH,), np.float32)
    for c0 in (Z_NQ, Z_MK, Z_FQ):
        scale[c0:c0 + 256] = ATTN_SCALE
    scale = jnp.asarray(scale)
    w = jnp.where(valid[None, :], w_in[:, idx], 0.0) * scale[None, :]
    b = jnp.where(valid, b_in[idx], 0.0) * scale
    return w, b


def _proj_kernel(x_ref, w_ref, wlo_ref, b_ref, o_ref, ob_ref, xh_s, xl_s, *, hi_tile):
    j = pl.program_id(1)
    f32 = jnp.float32
    _proj_tile(x_ref, w_ref, wlo_ref, b_ref, o_ref, xh_s, xl_s, j, hi_tile)
    ob_ref[...] = o_ref[...].astype(jnp.bfloat16)


def _proj_tile(x_ref, w_ref, wlo_ref, b_ref, o_ref, xh_s, xl_s, j, hi_tile):
    f32 = jnp.float32

    @pl.when(j == 0)
    def _():
        x = x_ref[...]
        xh = x.astype(jnp.bfloat16)
        xh_s[...] = xh
        xl_s[...] = (x - xh.astype(f32)).astype(jnp.bfloat16)

    @pl.when(j != hi_tile)
    def _():
        o_ref[...] = jnp.dot(xh_s[...], w_ref[...], preferred_element_type=f32) + b_ref[...]

    @pl.when(j == hi_tile)
    def _():
        acc = jnp.dot(xh_s[...], wlo_ref[...], preferred_element_type=f32)
        acc += jnp.dot(xl_s[...], w_ref[...], preferred_element_type=f32)
        acc += jnp.dot(xh_s[...], w_ref[...], preferred_element_type=f32)
        o_ref[...] = acc + b_ref[...]


def _in_projection(x, w_in, b_in, tm=1024):
    T, D = x.shape
    w, b = _permute_in_proj(w_in, b_in)
    wh = w.astype(jnp.bfloat16)
    hi_tile = Z_NQ // Z_TILE
    wlo = (w[:, Z_NQ:Z_NQ + Z_TILE] - wh[:, Z_NQ:Z_NQ + Z_TILE].astype(jnp.float32)).astype(jnp.bfloat16)
    return pl.pallas_call(
        functools.partial(_proj_kernel, hi_tile=hi_tile),
        grid=(T // tm, Z_WIDTH // Z_TILE),
        in_specs=[pl.BlockSpec((tm, D), lambda i, j: (i, 0)),
                  pl.BlockSpec((D, Z_TILE), lambda i, j: (0, j)),
                  pl.BlockSpec((D, Z_TILE), lambda i, j: (0, 0)),
                  pl.BlockSpec((1, Z_TILE), lambda i, j: (0, j))],
        out_specs=[pl.BlockSpec((tm, Z_TILE), lambda i, j: (i, j)),
                   pl.BlockSpec((tm, Z_TILE), lambda i, j: (i, j))],
        out_shape=[jax.ShapeDtypeStruct((T, Z_WIDTH), jnp.float32),
                   jax.ShapeDtypeStruct((T, ZB_WIDTH), jnp.bfloat16)],
        scratch_shapes=[pltpu.VMEM((tm, D), jnp.bfloat16), pltpu.VMEM((tm, D), jnp.bfloat16)],
        compiler_params=pltpu.CompilerParams(dimension_semantics=("parallel", "arbitrary"),
                                             vmem_limit_bytes=VMEM_LIMIT_BYTES),
        name="in_projection",
    )(x, wh, wlo, b.reshape(1, Z_WIDTH))


def _merge_kernel(x_ref, on_ref, om_ref, of_ref, wg_ref, bg_ref, wb_ref, wo_ref, lw_ref, lb_ref, o_ref, *, alpha):
    f32, bf16 = jnp.float32, jnp.bfloat16
    x = x_ref[...]
    xb = x.astype(bf16)
    y = None
    for m, br in enumerate((on_ref, om_ref, of_ref)):
        gcols = slice(m * D_MODEL, (m + 1) * D_MODEL)
        gm = jnp.dot(xb, wg_ref[:, gcols], preferred_element_type=f32) + bg_ref[:, gcols]
        proj = jnp.dot(br[...].astype(bf16), wb_ref[m], preferred_element_type=f32)
        term = _sigmoid(gm) * proj
        y = term if y is None else y + term
    mix = jnp.dot(y.astype(bf16), wo_ref[...], preferred_element_type=f32)
    o_ref[...] = _ln(alpha * x + mix, lw_ref[...], lb_ref[...])


def _merge_ln(x, o_nsa, o_ml, o_fox, w_in, b_in, w_branch, w_out, ln_w, ln_b, alpha, tm=512):
    T, D = x.shape
    g0 = int(np.sum(IN_SPLITS[:GM_SPLIT]))
    w_gm = w_in[:, g0:g0 + N_BRANCH * D].astype(jnp.bfloat16)
    b_gm = b_in[g0:g0 + N_BRANCH * D].reshape(1, N_BRANCH * D)
    tok = lambda w: pl.BlockSpec((tm, w), lambda i: (i, 0))
    vec = pl.BlockSpec((1, D), lambda i: (0, 0))
    return pl.pallas_call(
        functools.partial(_merge_kernel, alpha=alpha),
        grid=(T // tm,),
        in_specs=[tok(D), tok(BRANCH_WIDTH), tok(BRANCH_WIDTH), tok(BRANCH_WIDTH),
                  pl.BlockSpec((D, N_BRANCH * D), lambda i: (0, 0)),
                  pl.BlockSpec((1, N_BRANCH * D), lambda i: (0, 0)),
                  pl.BlockSpec((N_BRANCH, BRANCH_WIDTH, D), lambda i: (0, 0, 0)),
                  pl.BlockSpec((D, D), lambda i: (0, 0)), vec, vec],
        out_specs=tok(D),
        out_shape=jax.ShapeDtypeStruct((T, D), jnp.float32),
        compiler_params=pltpu.CompilerParams(dimension_semantics=("parallel",),
                                             vmem_limit_bytes=VMEM_LIMIT_BYTES),
        name="merge_ln",
    )(x, o_nsa, o_ml, o_fox, w_gm, b_gm, w_branch.astype(jnp.bfloat16), w_out.astype(jnp.bfloat16),
      ln_w.reshape(1, D), ln_b.reshape(1, D))


def _route(logits):
    tm = logits.shape[0]
    lane = lax.broadcasted_iota(jnp.int32, (tm, 128), 1)
    lanef = lane.astype(jnp.float32)
    big = 1e9
    is_g = lane < N_GROUPS
    lg = jnp.where(is_g, logits, -jnp.inf)
    eg = jnp.exp(lg - jnp.max(lg, axis=-1, keepdims=True))
    pg = eg / jnp.sum(eg, axis=-1, keepdims=True)
    g_val = jnp.max(pg, axis=-1, keepdims=True)
    g_idx = jnp.min(jnp.where(is_g & (pg == g_val), lanef, big), axis=-1, keepdims=True)
    e_lo = N_GROUPS + EXPERTS_PER_GROUP * g_idx
    in_grp = (lanef >= e_lo) & (lanef < e_lo + EXPERTS_PER_GROUP)
    le = jnp.where(in_grp, logits, -jnp.inf)
    ee = jnp.exp(le - jnp.max(le, axis=-1, keepdims=True))
    pe = ee / jnp.sum(ee, axis=-1, keepdims=True)
    v1 = jnp.max(pe, axis=-1, keepdims=True)
    i1 = jnp.min(jnp.where(in_grp & (pe == v1), lanef, big), axis=-1, keepdims=True)
    rest = in_grp & (lanef != i1)
    pe2 = jnp.where(rest, pe, -1.0)
    v2 = jnp.max(pe2, axis=-1, keepdims=True)
    i2 = jnp.min(jnp.where(rest & (pe2 == v2), lanef, big), axis=-1, keepdims=True)
    tot = v1 + v2
    return jnp.where(lanef == i1, g_val * v1 / tot, jnp.where(lanef == i2, g_val * v2 / tot, 0.0))


def _moe_kernel(x_ref, wr_ref, br_ref, wg_ref, wu_ref, wd_ref, lw_ref, lb_ref, o_ref, xb_s, gate_s, acc_s, *, alpha):
    e = pl.program_id(1)
    f32, bf16 = jnp.float32, jnp.bfloat16

    @pl.when(e == 0)
    def _():
        x = x_ref[...]
        xb_s[...] = x.astype(bf16)
        logits = jnp.dot(x, wr_ref[...], precision=_HI, preferred_element_type=f32) + br_ref[...]
        gate_s[...] = _route(logits)
        acc_s[...] = jnp.zeros_like(acc_s)

    lane = lax.broadcasted_iota(jnp.int32, gate_s.shape, 1)
    y = None
    for k in range(wg_ref.shape[0]):
        ex = e * wg_ref.shape[0] + k
        gate = jnp.sum(jnp.where(lane == N_GROUPS + ex, gate_s[...], 0.0), axis=-1, keepdims=True)
        g = jnp.dot(xb_s[...], wg_ref[k].astype(bf16), preferred_element_type=f32)
        u = jnp.dot(xb_s[...], wu_ref[k].astype(bf16), preferred_element_type=f32)
        h = (g * _sigmoid(g)) * u * gate
        yk = jnp.dot(h.astype(bf16), wd_ref[k].astype(bf16), preferred_element_type=f32)
        y = yk if y is None else y + yk
    acc_s[...] += y

    @pl.when(e == pl.num_programs(1) - 1)
    def _():
        o_ref[...] = _ln(alpha * x_ref[...] + acc_s[...], lw_ref[...], lb_ref[...])


def _moe_ln(x, w_group, b_group, w_expert, b_expert, w_gate, w_up, w_down, ln_w, ln_b, alpha, tm=1024, eps=2):
    T, D = x.shape
    bf16 = jnp.bfloat16
    n_r = N_GROUPS + N_EXPERTS
    wr = jnp.pad(jnp.concatenate([w_group, w_expert], axis=1), ((0, 0), (0, 128 - n_r)))
    br = jnp.pad(jnp.concatenate([b_group, b_expert]), (0, 128 - n_r)).reshape(1, 128)
    vec = pl.BlockSpec((1, D), lambda i, e: (0, 0))
    return pl.pallas_call(
        functools.partial(_moe_kernel, alpha=alpha),
        grid=(T // tm, N_EXPERTS // eps),
        in_specs=[pl.BlockSpec((tm, D), lambda i, e: (i, 0)),
                  pl.BlockSpec((D, 128), lambda i, e: (0, 0)),
                  pl.BlockSpec((1, 128), lambda i, e: (0, 0)),
                  pl.BlockSpec((eps, D, D_EXPERT), lambda i, e: (e, 0, 0)),
                  pl.BlockSpec((eps, D, D_EXPERT), lambda i, e: (e, 0, 0)),
                  pl.BlockSpec((eps, D_EXPERT, D), lambda i, e: (e, 0, 0)), vec, vec],
        out_specs=pl.BlockSpec((tm, D), lambda i, e: (i, 0)),
        out_shape=jax.ShapeDtypeStruct((T, D), jnp.float32),
        scratch_shapes=[pltpu.VMEM((tm, D), bf16), pltpu.VMEM((tm, 128), jnp.float32),
                        pltpu.VMEM((tm, D), jnp.float32)],
        compiler_params=pltpu.CompilerParams(dimension_semantics=("parallel", "arbitrary"),
                                             vmem_limit_bytes=VMEM_LIMIT_BYTES),
        name="moe_ln",
    )(x, wr, br, w_gate, w_up, w_down, ln_w.reshape(1, D), ln_b.reshape(1, D))


N_PAGES = PAST_LEN // PAGE_SIZE
NEW_PAD = 128
DEC_KEYS = PAST_LEN + NEW_PAD
DEC_ROWS = NSA_HEADS * DEC_SEQ


def _pages_token_minor(cache):
    nd = cache.ndim
    t = cache.transpose((0, 1) + tuple(range(3, nd)) + (2,))
    return t.reshape(cache.shape[0] * cache.shape[1], -1, cache.shape[2])


def _page_specs(rows, row_block, layer, n_phys):
    def spec(p):
        return pl.BlockSpec((1, rows, PAGE_SIZE),
                            lambda b, pt: (layer * n_phys + pt[b * N_PAGES + p], row_block, 0))
    return [spec(p) for p in range(N_PAGES)]


def _softmax_rows(s):
    m = jnp.max(s, axis=-1, keepdims=True)
    p = jnp.exp(s - m)
    return p, jnp.sum(p, axis=-1, keepdims=True)


def _fox_decode_kernel(pt_ref, qbd_ref, knew_ref, vnew_ref, fk_ref, fq_ref, *refs):
    pages, o_ref = refs[:N_PAGES], refs[N_PAGES]
    f32, bf16 = jnp.float32, jnp.bfloat16
    W = FOX_HEADS * HEAD_DIM
    qbd = qbd_ref[0]
    s = [jnp.dot(qbd, pg[0, :W, :].astype(bf16), preferred_element_type=f32) for pg in pages]
    s.append(jnp.dot(qbd, knew_ref[0], preferred_element_type=f32))
    s = jnp.concatenate(s, axis=1)
    rowh = lax.broadcasted_iota(jnp.int32, (DEC_ROWS, 1), 0) // DEC_SEQ
    fk = fk_ref[0]
    fk_rows = jnp.where(rowh == 0, fk[0:1], jnp.where(rowh == 1, fk[1:2], jnp.where(rowh == 2, fk[2:3], fk[3:4])))
    col = lax.broadcasted_iota(jnp.int32, (DEC_ROWS, DEC_KEYS), 1)
    t = lax.broadcasted_iota(jnp.int32, (DEC_ROWS, DEC_KEYS), 0) % DEC_SEQ
    ok = (col < PAST_LEN) | (col - PAST_LEN <= t)
    s = jnp.where(ok, s + (fq_ref[0] - fk_rows), NEG_INF)
    p, l = _softmax_rows(s)
    pb = p.astype(bf16)
    o = lax.dot_general(pb[:, PAST_LEN:], vnew_ref[0], _NT, preferred_element_type=f32)
    for i, pg in enumerate(pages):
        o += lax.dot_general(pb[:, i * PAGE_SIZE:(i + 1) * PAGE_SIZE], pg[0, W:, :].astype(bf16), _NT,
                             preferred_element_type=f32)
    o = o / l
    lane_h = lax.broadcasted_iota(jnp.int32, (DEC_ROWS, W), 1) // HEAD_DIM
    o = jnp.where(lane_h == rowh, o, 0.0)
    o_ref[0] = o[0:8] + o[8:16] + o[16:24] + o[24:32]


def _pad_new_t(x):
    return jnp.pad(x.transpose(0, 2, 1), ((0, 0), (0, 0), (0, NEW_PAD - DEC_SEQ))).astype(jnp.bfloat16)


def _fox_decode(q, k_new, v_new, lf_new, cache_kv, cache_lf, page_table, layer):
    DB = q.shape[0]
    f32, bf16 = jnp.float32, jnp.bfloat16
    W = FOX_HEADS * HEAD_DIM
    eye = jnp.eye(FOX_HEADS, dtype=f32)
    qh = q.reshape(DB, DEC_SEQ, FOX_HEADS, HEAD_DIM).transpose(0, 2, 1, 3)
    qbd = (qh[:, :, :, None, :] * eye[None, :, None, :, None]).reshape(DB, DEC_ROWS, W).astype(bf16)
    lf_all = jnp.concatenate([cache_lf[layer][page_table].reshape(DB, PAST_LEN, FOX_HEADS).astype(f32), lf_new], axis=1)
    F = jnp.cumsum(lf_all, axis=1)
    fk = jnp.pad(F.transpose(0, 2, 1), ((0, 0), (0, 8 - FOX_HEADS), (0, DEC_KEYS - PAST_LEN - DEC_SEQ)))
    fq = F[:, PAST_LEN:].transpose(0, 2, 1).reshape(DB, DEC_ROWS, 1)
    per_seq = lambda r, w: pl.BlockSpec((1, r, w), lambda b, pt: (b, 0, 0))
    pages = _pages_token_minor(cache_kv)
    return pl.pallas_call(
        _fox_decode_kernel,
        grid_spec=pltpu.PrefetchScalarGridSpec(
            num_scalar_prefetch=1, grid=(DB,),
            in_specs=[per_seq(DEC_ROWS, W), per_seq(W, NEW_PAD), per_seq(W, NEW_PAD), per_seq(8, DEC_KEYS),
                      per_seq(DEC_ROWS, 1)] + _page_specs(2 * W, 0, layer, cache_kv.shape[1]),
            out_specs=per_seq(DEC_SEQ, W)),
        out_shape=jax.ShapeDtypeStruct((DB, DEC_SEQ, W), f32),
        compiler_params=pltpu.CompilerParams(dimension_semantics=("parallel",), vmem_limit_bytes=VMEM_LIMIT_BYTES),
        name="fox_decode",
    )(page_table.reshape(-1), qbd, _pad_new_t(k_new), _pad_new_t(v_new), fk, fq, *([pages] * N_PAGES))


def _nsa_decode_kernel(pt_ref, qs_ref, qf_ref, kcvc_ref, rnew_ref, wbuf_ref, wnew_ref, g_ref, e_ref, *refs):
    pages, o_ref = refs[:N_PAGES], refs[N_PAGES]
    f32, bf16 = jnp.float32, jnp.bfloat16
    H, T, HD = NSA_HEADS, DEC_SEQ, HEAD_DIM
    qs = qs_ref[0]
    row = lax.broadcasted_iota(jnp.int32, (DEC_ROWS, 1), 0)
    head = row // T
    slope = jnp.where(head == 0, NSA_SLOPES[0], jnp.where(head == 1, NSA_SLOPES[1],
                      jnp.where(head == 2, NSA_SLOPES[2], NSA_SLOPES[3]))).astype(f32)
    posq = PAST_LEN + row % T

    kcvc = kcvc_ref[0]
    n_cb = kcvc.shape[0]
    sc = lax.dot_general(qf_ref[0], kcvc, _NT, precision=_HI, preferred_element_type=f32)
    jb = lax.broadcasted_iota(jnp.int32, (DEC_ROWS, n_cb), 1)
    distc = posq - (jb * NSA_BLOCK + NSA_BLOCK - 1)
    okc = distc >= 0
    sc = jnp.where(okc, sc - slope * distc.astype(f32), NEG_INF)
    pc = jnp.exp(sc - jnp.max(sc, axis=-1, keepdims=True)) * okc.astype(f32)
    pc = pc / jnp.maximum(jnp.sum(pc, axis=-1, keepdims=True), TINY)
    o_cmp = jnp.dot(pc.astype(bf16), kcvc.astype(bf16), preferred_element_type=f32)[:, HD:]
    imp = pc[0:T] + pc[T:2 * T] + pc[2 * T:3 * T] + pc[3 * T:4 * T]
    imp = jnp.concatenate([imp, jnp.zeros((T, 128 - n_cb), f32)], axis=1)
    cur = (PAST_LEN + lax.broadcasted_iota(jnp.int32, (T, 1), 0)) // NSA_BLOCK
    msel = _select_blocks_by_rank(imp, cur, NSA_TOPK, n_cb + 1).astype(bf16)

    mexp = jnp.dot(msel, e_ref[...], preferred_element_type=f32)
    col = lax.broadcasted_iota(jnp.int32, (T, DEC_KEYS), 1)
    tq = lax.broadcasted_iota(jnp.int32, (T, DEC_KEYS), 0)
    d = PAST_LEN + tq - col
    ok = (mexp > 0.5) & (d >= 0)
    kv = [pg[0].astype(bf16) for pg in pages] + [rnew_ref[0]]
    s = jnp.concatenate([jnp.dot(qs, x[:HD], preferred_element_type=f32) for x in kv], axis=1)
    s = jnp.where(_tile_rows(ok, H), s - slope * _tile_rows(d.astype(f32), H), NEG_INF)
    p, l = _softmax_rows(s)
    pb = p.astype(bf16)
    acc = jnp.zeros((DEC_ROWS, HD), f32)
    for i, x in enumerate(kv):
        acc += lax.dot_general(pb[:, i * PAGE_SIZE:(i + 1) * PAGE_SIZE], x[HD:], _NT, preferred_element_type=f32)
    o_sel = acc / l

    wb = wbuf_ref.shape[2]
    kvw = [wbuf_ref[0].astype(bf16), wnew_ref[0]]
    sw = jnp.concatenate([jnp.dot(qs, x[:HD], preferred_element_type=f32) for x in kvw], axis=1)
    colw = lax.broadcasted_iota(jnp.int32, (T, wb + NEW_PAD), 1)
    tw = lax.broadcasted_iota(jnp.int32, (T, wb + NEW_PAD), 0)
    dw = wb + tw - colw
    okw = (dw >= 0) & (dw < NSA_WINDOW)
    sw = jnp.where(_tile_rows(okw, H), sw - slope * _tile_rows(dw.astype(f32), H), NEG_INF)
    pw, lw = _softmax_rows(sw)
    pwb = pw.astype(bf16)
    accw = (lax.dot_general(pwb[:, :wb], kvw[0][HD:], _NT, preferred_element_type=f32)
            + lax.dot_general(pwb[:, wb:], kvw[1][HD:], _NT, preferred_element_type=f32))
    o_win = accw / lw

    g = _sigmoid(g_ref[0])
    gate = lambda c: jnp.concatenate([g[:, 3 * h + c:3 * h + c + 1] for h in range(H)], axis=0)
    o_ref[0] = gate(0) * o_cmp + gate(1) * o_sel + gate(2) * o_win


def _compress_pages_kernel(x_ref, wk_ref, wv_ref, o_ref):
    tm = x_ref.shape[0]
    f32 = jnp.float32
    acc_k = jnp.zeros((tm, 2 * HEAD_DIM), f32)
    acc_v = jnp.zeros((tm, 2 * HEAD_DIM), f32)
    pair = lambda r: jnp.concatenate([x_ref[:, r, :], x_ref[:, r + 1, :]], axis=1)
    for i in range(HEAD_DIM // 2):
        acc_k += jnp.dot(pair(2 * i), wk_ref[i], precision=_HI, preferred_element_type=f32)
        acc_v += jnp.dot(pair(HEAD_DIM + 2 * i), wv_ref[i], preferred_element_type=f32)
    o_ref[...] = jnp.concatenate([acc_k, acc_v], axis=1)


def _nsa_compress_pages(pages, w_ck, w_cv, layer, n_phys, tm=128):
    eye = jnp.eye(PAGE_SIZE // NSA_BLOCK, dtype=jnp.float32)
    big = lambda w: jnp.einsum('pde,bc->dbpce', w, eye).reshape(HEAD_DIM // 2, 2 * PAGE_SIZE, 2 * HEAD_DIM)
    wspec = pl.BlockSpec((HEAD_DIM // 2, 2 * PAGE_SIZE, 2 * HEAD_DIM), lambda i: (0, 0, 0))
    return pl.pallas_call(
        _compress_pages_kernel,
        grid=(n_phys // tm,),
        in_specs=[pl.BlockSpec((tm, 2 * HEAD_DIM, PAGE_SIZE), lambda i: (layer * (n_phys // tm) + i, 0, 0)),
                  wspec, wspec],
        out_specs=pl.BlockSpec((tm, 4 * HEAD_DIM), lambda i: (i, 0)),
        out_shape=jax.ShapeDtypeStruct((n_phys, 4 * HEAD_DIM), jnp.float32),
        compiler_params=pltpu.CompilerParams(dimension_semantics=("parallel",), vmem_limit_bytes=VMEM_LIMIT_BYTES),
        name="nsa_compress_pages",
    )(pages, big(w_ck), big(w_cv))


def _nsa_decode(nq, rows_new, win_new, gates, cache_rows, cache_win, page_table, w_ck, w_cv, layer):
    DB = nq.shape[0]
    f32, bf16 = jnp.float32, jnp.bfloat16
    HD = HEAD_DIM
    n_phys = cache_rows.shape[1]
    n_blk = PAGE_SIZE // NSA_BLOCK
    pages = _pages_token_minor(cache_rows)
    kcvc = _nsa_compress_pages(pages, w_ck, w_cv, layer, n_phys)[page_table]
    kcvc = kcvc.reshape(DB, N_PAGES, 2, n_blk, HD).transpose(0, 1, 3, 2, 4).reshape(DB, N_PAGES * n_blk, 2 * HD)
    stack = lambda x: x.reshape(DB, DEC_SEQ, NSA_HEADS, HD).transpose(0, 2, 1, 3).reshape(DB, DEC_ROWS, HD)
    qs = stack(nq).astype(bf16)
    qf = jnp.pad(stack(nq), ((0, 0), (0, 0), (0, HD))).astype(f32)
    win_t = cache_win.transpose(0, 1, 3, 4, 2).reshape(cache_win.shape[0] * DB, 2 * HD, cache_win.shape[2])
    colk = np.arange(DEC_KEYS)
    e = (np.arange(128)[:, None] == colk[None, :] // NSA_BLOCK) & (colk[None, :] < PAST_LEN + NSA_BLOCK)
    wb = cache_win.shape[2]
    per_seq = lambda r, w: pl.BlockSpec((1, r, w), lambda b, pt: (b, 0, 0))
    o = pl.pallas_call(
        _nsa_decode_kernel,
        grid_spec=pltpu.PrefetchScalarGridSpec(
            num_scalar_prefetch=1, grid=(DB,),
            in_specs=[per_seq(DEC_ROWS, HD), per_seq(DEC_ROWS, 2 * HD), per_seq(PAST_LEN // NSA_BLOCK, 2 * HD),
                      per_seq(2 * HD, NEW_PAD),
                      pl.BlockSpec((1, 2 * HD, wb), lambda b, pt: (layer * DB + b, 0, 0)),
                      per_seq(2 * HD, NEW_PAD), per_seq(DEC_SEQ, 128),
                      pl.BlockSpec((128, DEC_KEYS), lambda b, pt: (0, 0))]
                     + _page_specs(2 * HD, 1, layer, n_phys),
            out_specs=per_seq(DEC_ROWS, HD)),
        out_shape=jax.ShapeDtypeStruct((DB, DEC_ROWS, HD), f32),
        compiler_params=pltpu.CompilerParams(dimension_semantics=("parallel",), vmem_limit_bytes=VMEM_LIMIT_BYTES),
        name="nsa_decode",
    )(page_table.reshape(-1), qs, qf, kcvc, _pad_new_t(rows_new[..., 2 * HD:]), win_t,
      _pad_new_t(win_new), gates, jnp.asarray(e, bf16), *([pages] * N_PAGES))
    return o.reshape(DB, NSA_HEADS, DEC_SEQ, HD).transpose(0, 2, 1, 3).reshape(DB, DEC_SEQ, BRANCH_WIDTH)


def _stack_layers(states):
    return tuple(jnp.stack(list(a)) for a in zip(*states))


def kernel(x_prompt, x_sample, cache_nsa, cache_nsa_win, cache_fox_kv, cache_fox_logf,
           state_mlstm_c, state_mlstm_n, state_mlstm_m, page_table,
           ln_in_w, ln_in_b, w_in, b_in, nsa_w_ck, nsa_w_cv, mlstm_norm_w, w_branch, w_out,
           ln1_w, ln1_b, moe_w_group, moe_b_group, moe_w_expert, moe_b_expert,
           moe_w_gate, moe_w_up, moe_w_down, ln2_w, ln2_b):
    f32, bf16 = jnp.float32, jnp.bfloat16
    alpha = (2.0 * DEPTH) ** 0.25
    B, S, D = x_prompt.shape
    DB, T, _ = x_sample.shape
    TP, TS = B * S, DB * T
    HD = HEAD_DIM
    xp = _layer_norm_rows(x_prompt.reshape(TP, D), ln_in_w, ln_in_b)
    xs = _layer_norm_rows(x_sample.reshape(TS, D), ln_in_w, ln_in_b)
    cols = lambda a, c0, w: a[..., c0:c0 + w]
    colsb = lambda a, c0, w: a[..., c0 - Z_NQ:c0 - Z_NQ + w]
    zero_state = (jnp.zeros((B, MLSTM_HEADS, HD, HD), f32), jnp.zeros((B, MLSTM_HEADS, HD), f32),
                  jnp.zeros((B, MLSTM_HEADS), f32))
    new_p, new_s = [], []
    for l in range(DEPTH):
        zp2, zbp2 = _in_projection(xp, w_in[l], b_in[l])
        zs2, zbs2 = _in_projection(xs, w_in[l], b_in[l])
        zp, zbp = zp2.reshape(B, S, Z_WIDTH), zbp2.reshape(B, S, ZB_WIDTH)
        zs, zbs = zs2.reshape(DB, T, Z_WIDTH), zbs2.reshape(DB, T, ZB_WIDTH)
        small_p, small_s = cols(zp, Z_SMALL, 128), cols(zs, Z_SMALL, 128)

        o_nsa_p = _nsa_prompt(cols(zp, Z_NQ, 256), colsb(zbp, Z_NQ, 256), cols(zp, Z_ROWS, 256),
                              colsb(zbp, Z_ROWS, 256), colsb(zbp, Z_WIN, 128), small_p, nsa_w_ck[l], nsa_w_cv[l])
        o_nsa_s = _nsa_decode(cols(zs, Z_NQ, 256), colsb(zbs, Z_ROWS, 256), colsb(zbs, Z_WIN, 128), small_s,
                              cache_nsa, cache_nsa_win, page_table, nsa_w_ck[l], nsa_w_cv[l], l)

        def mlstm(q, k, v, og, small, state, L, nb, dt):
            return _mlstm(q, k, v, og, cols(small, SMALL_MI, MLSTM_HEADS), cols(small, SMALL_MF, MLSTM_HEADS),
                          mlstm_norm_w[l], *state, L, nb, dt)

        o_ml_p, st_ml_p = mlstm(colsb(zbp, Z_MQ, 256), colsb(zbp, Z_MK, 256), colsb(zbp, Z_MV, 256),
                                cols(zp, Z_MO, 256), small_p, zero_state, 128, B, bf16)
        o_ml_s, st_ml_s = mlstm(cols(zs, Z_MQ, 256), cols(zs, Z_MK, 256), cols(zs, Z_MV, 256), cols(zs, Z_MO, 256),
                                small_s, (state_mlstm_c[l], state_mlstm_n[l], state_mlstm_m[l]), T, 4, f32)

        lf_p = jax.nn.log_sigmoid(cols(small_p, SMALL_FF, FOX_HEADS))
        lf_s = jax.nn.log_sigmoid(cols(small_s, SMALL_FF, FOX_HEADS))
        o_fox_p = _fox_prompt_attn(zbp, jnp.cumsum(lf_p, axis=1))
        o_fox_s = _fox_decode(colsb(zbs, Z_FQ, 256), colsb(zbs, Z_FK, 256), colsb(zbs, Z_FV, 256), lf_s,
                              cache_fox_kv, cache_fox_logf, page_table, l)

        flat = lambda a: a.reshape(-1, a.shape[-1])
        moe_w = (moe_w_group[l], moe_b_group[l], moe_w_expert[l], moe_b_expert[l],
                 moe_w_gate[l], moe_w_up[l], moe_w_down[l], ln2_w[l], ln2_b[l], alpha)
        xp = _merge_ln(xp, flat(o_nsa_p), flat(o_ml_p), flat(o_fox_p), w_in[l], b_in[l],
                       w_branch[l], w_out[l], ln1_w[l], ln1_b[l], alpha)
        xp = _moe_ln(xp, *moe_w)
        xs = _merge_ln(xs, flat(o_nsa_s), flat(o_ml_s), flat(o_fox_s), w_in[l], b_in[l],
                       w_branch[l], w_out[l], ln1_w[l], ln1_b[l], alpha)
        xs = _moe_ln(xs, *moe_w)

        w_keep = min(NSA_WINDOW, S)
        new_p.append((cols(zp, Z_ROWS, 256).reshape(B, S, NSA_ROWS, HD),
                      cols(zp, Z_WIN, 128)[:, S - w_keep:].reshape(B, w_keep, 2, HD),
                      cols(zp, Z_FK, 512).reshape(B, S, 2, FOX_HEADS, HD), lf_p) + tuple(st_ml_p))
        win_new = cols(zs, Z_WIN, 128).reshape(DB, T, 2, HD).astype(cache_nsa_win.dtype)
        new_s.append((cols(zs, Z_ROWS, 256).reshape(DB, T, NSA_ROWS, HD),
                      win_new,
                      cols(zs, Z_FK, 512).reshape(DB, T, 2, FOX_HEADS, HD), lf_s) + tuple(st_ml_s))
    (p_nsa_rows, p_nsa_win, p_fox_kv, p_fox_logf, p_mlstm_c, p_mlstm_n, p_mlstm_m) = _stack_layers(new_p)
    (s_nsa_rows, s_win_new, s_fox_kv, s_fox_logf, s_mlstm_c, s_mlstm_n, s_mlstm_m) = _stack_layers(new_s)
    s_nsa_win = jnp.concatenate([cache_nsa_win[:, :, T:], s_win_new], axis=2)
    return (xp.reshape(B, S, D), xs.reshape(DB, T, D),
            p_nsa_rows, p_nsa_win, p_fox_kv, p_fox_logf, p_mlstm_c, p_mlstm_n, p_mlstm_m,
            s_nsa_rows, s_nsa_win, s_fox_kv, s_fox_logf, s_mlstm_c, s_mlstm_n, s_mlstm_m)
```

```python
import functools

import jax
import jax.numpy as jnp
import numpy as np
from jax import lax
from jax.experimental import pallas as pl
from jax.experimental.pallas import tpu as pltpu

D_MODEL = 1024
DEPTH = 2
DEC_SEQ = 8
PAST_LEN = 2048
PAGE_SIZE = 128

HEAD_DIM = 64
NSA_HEADS = 4
NSA_BLOCK = 64
NSA_TOPK = 16
NSA_WINDOW = 512
NSA_ROWS = 4
MLSTM_HEADS = 4
FOX_HEADS = 4
N_BRANCH = 3
BRANCH_WIDTH = NSA_HEADS * HEAD_DIM
N_GROUPS = 4
EXPERTS_PER_GROUP = 4
N_EXPERTS = N_GROUPS * EXPERTS_PER_GROUP
D_EXPERT = 256
LN_EPS = 1e-5
NEG_INF = -1e30
TINY = 1e-30
ATTN_SCALE = HEAD_DIM ** -0.5

IN_SPLITS = (
    NSA_HEADS * HEAD_DIM,
    6 * HEAD_DIM,
    NSA_HEADS * 3,
    MLSTM_HEADS * HEAD_DIM,
    MLSTM_HEADS * HEAD_DIM,
    MLSTM_HEADS * HEAD_DIM,
    MLSTM_HEADS * HEAD_DIM,
    MLSTM_HEADS,
    MLSTM_HEADS,
    FOX_HEADS * HEAD_DIM,
    FOX_HEADS * HEAD_DIM,
    FOX_HEADS * HEAD_DIM,
    FOX_HEADS,
    N_BRANCH * D_MODEL,
)


VMEM_LIMIT_BYTES = 48 * 1024 * 1024
NSA_SLOPES = tuple(2.0 ** (-8.0 * (h + 1) / NSA_HEADS) for h in range(NSA_HEADS))
_NT = (((1,), (1,)), ((), ()))
_HI = lax.Precision.HIGHEST


def _sigmoid(x):
    return 1.0 / (1.0 + jnp.exp(-x))


def _tile_rows(x, n):
    return jnp.concatenate([x] * n, axis=0)


def _compress_kernel(x_ref, w_ref, o_ref, acc_ref):
    k = pl.program_id(1)

    @pl.when(k == 0)
    def _():
        acc_ref[...] = jnp.zeros_like(acc_ref)

    acc_ref[...] += jnp.dot(x_ref[...], w_ref[...], preferred_element_type=jnp.float32, precision=_HI)

    @pl.when(k == pl.num_programs(1) - 1)
    def _():
        o_ref[...] = acc_ref[...]


def _compress_weights(w_ck, w_cv):
    z = jnp.zeros_like(w_ck)
    wk = jnp.stack([w_ck, z, z, z], axis=1)
    wv = jnp.stack([z, w_cv, z, z], axis=1)
    return jnp.concatenate([wk, wv], axis=-1).reshape(NSA_BLOCK * NSA_ROWS * HEAD_DIM, 2 * HEAD_DIM)


def _nsa_compress_blocks(blocks, w_big, tm=256, tk=2048):
    n, kdim = blocks.shape
    tm = min(tm, n)
    return pl.pallas_call(
        _compress_kernel,
        grid=(n // tm, kdim // tk),
        in_specs=[pl.BlockSpec((tm, tk), lambda i, k: (i, k)),
                  pl.BlockSpec((tk, 2 * HEAD_DIM), lambda i, k: (k, 0))],
        out_specs=pl.BlockSpec((tm, 2 * HEAD_DIM), lambda i, k: (i, 0)),
        out_shape=jax.ShapeDtypeStruct((n, 2 * HEAD_DIM), jnp.float32),
        scratch_shapes=[pltpu.VMEM((tm, 2 * HEAD_DIM), jnp.float32)],
        name="nsa_compress",
    )(blocks, w_big)


def _select_blocks(imp, cur, n_pick):
    q, n_sb = imp.shape
    jq = lax.broadcasted_iota(jnp.int32, (q, n_sb), 1)
    valid = jq <= cur
    forced = valid & ((jq == cur) | (jq == 0))
    work0 = jnp.where(forced, -jnp.inf, jnp.where(valid, imp, NEG_INF))

    jf = jq.astype(jnp.float32)

    def pick(_, carry):
        work, sel = carry
        mx = jnp.max(work, axis=-1, keepdims=True)
        hit = jf == jnp.min(jnp.where(work == mx, jf, float(n_sb)), axis=-1, keepdims=True)
        sel = jnp.where(hit & valid, 1.0, sel)
        work = jnp.where(hit, -jnp.inf, work)
        return work, sel

    _, sel = lax.fori_loop(0, n_pick - 2, pick, (work0, forced.astype(jnp.float32)), unroll=True)
    return sel


def _select_blocks_by_rank(imp, cur, n_pick, n_cand):
    q, n_sb = imp.shape
    jq = lax.broadcasted_iota(jnp.int32, (q, n_sb), 1)
    valid = jq <= cur
    forced = valid & ((jq == cur) | (jq == 0))
    work = jnp.where(forced, -jnp.inf, jnp.where(valid, imp, NEG_INF))
    ahead = jnp.zeros((q, n_sb), jnp.float32)
    for i in range(n_cand):
        wi = work[:, i:i + 1]
        ahead += ((wi > work) | ((wi == work) & (jq > i))).astype(jnp.float32)
    return (forced | (valid & (ahead < n_pick - 2))).astype(jnp.float32)


def _nsa_prompt_kernel(qs_ref, qf_ref, kc_ref, vc_ref, ksel_ref, vsel_ref, kwin_ref, vwin_ref, g_ref,
                       o_ref, flag_ref, *, tq, tk, tw):
    i = pl.program_id(1)
    f32, bf16 = jnp.float32, jnp.bfloat16
    H = NSA_HEADS
    R = H * tq
    qs = qs_ref[0].reshape(R, 2 * HEAD_DIM)
    qf = qf_ref[0].reshape(R, HEAD_DIM)
    row = lax.broadcasted_iota(jnp.int32, (R, 1), 0)
    head = row // tq
    slope = jnp.where(head == 0, NSA_SLOPES[0], jnp.where(head == 1, NSA_SLOPES[1],
                      jnp.where(head == 2, NSA_SLOPES[2], NSA_SLOPES[3]))).astype(f32)
    posq = i * tq + (row - head * tq)

    n_cb = kc_ref.shape[1]
    sc = lax.dot_general(qf, kc_ref[0], _NT, precision=_HI, preferred_element_type=f32)
    jb = lax.broadcasted_iota(jnp.int32, (R, n_cb), 1)
    distc = posq - (jb * NSA_BLOCK + NSA_BLOCK - 1)
    okc = distc >= 0
    sc = jnp.where(okc, sc - slope * distc.astype(f32), NEG_INF)
    pc = jnp.exp(sc - jnp.max(sc, axis=-1, keepdims=True)) * okc.astype(f32)
    pc = pc / jnp.maximum(jnp.sum(pc, axis=-1, keepdims=True), TINY)
    o_cmp = jnp.dot(pc.astype(bf16), vc_ref[0].astype(bf16), preferred_element_type=f32)
    imp = pc[0:tq] + pc[tq:2 * tq] + pc[2 * tq:3 * tq] + pc[3 * tq:4 * tq]

    pq = i * tq + lax.broadcasted_iota(jnp.int32, (tq, 1), 0)
    msel = _select_blocks(imp, pq // NSA_BLOCK, NSA_TOPK)

    bpt = tk // NSA_BLOCK
    blk_any = jnp.max(msel, axis=0, keepdims=True)
    for j in range(n_cb // bpt):
        flag_ref[j] = (jnp.max(blk_any[:, j * bpt:(j + 1) * bpt]) > 0.5).astype(jnp.int32)
    mbias = (NEG_INF * (1.0 - msel)).astype(bf16)

    rowpos = i * tq + lax.broadcasted_iota(jnp.int32, (tq, 1), 0)

    def attend(carry, s, v1):
        m, acc = carry
        m_new = jnp.maximum(m, jnp.max(s, axis=-1, keepdims=True))
        p = jnp.exp(s - m_new).astype(bf16)
        return m_new, jnp.exp(m - m_new) * acc + jnp.dot(p, v1, preferred_element_type=f32)

    init = (jnp.full((R, 1), NEG_INF, f32), jnp.zeros((R, 2 * HEAD_DIM), f32))

    def sel_tile(j, carry, causal):
        k0 = pl.multiple_of(j * tk, tk)
        s = lax.dot_general(qs, ksel_ref[0, pl.ds(k0, tk), :], _NT, preferred_element_type=f32)
        eb = (lax.broadcasted_iota(jnp.int32, (n_cb, tk), 0)
              == j * bpt + lax.broadcasted_iota(jnp.int32, (n_cb, tk), 1) // NSA_BLOCK)
        bias = jnp.dot(mbias, eb.astype(bf16), preferred_element_type=f32)
        if causal:
            d = rowpos - (k0 + lax.broadcasted_iota(jnp.int32, (tq, tk), 1))
            bias = jnp.where(d >= 0, bias, NEG_INF)
        return attend(carry, s + _tile_rows(bias, H), vsel_ref[0, pl.ds(k0, tk), :])

    def sel_body(j, carry):
        return lax.cond(flag_ref[j] > 0, lambda c: sel_tile(j, c, False), lambda c: c, carry)

    n_sel = (i * tq + tq - 1) // tk + 1
    carry = lax.fori_loop(0, n_sel - 1, sel_body, init)
    _, a_sel = sel_tile(n_sel - 1, carry, True)
    o_sel = a_sel[:, :HEAD_DIM] / a_sel[:, HEAD_DIM:]

    nw = NSA_WINDOW + tq
    w0 = pl.multiple_of(jnp.maximum(i * tq - NSA_WINDOW, 0), tw)
    sw = lax.dot_general(qs, kwin_ref[0, pl.ds(w0, nw), :], _NT, preferred_element_type=f32)
    dw = rowpos - (w0 + lax.broadcasted_iota(jnp.int32, (tq, nw), 1))
    okw = (dw >= 0) & (dw < NSA_WINDOW)
    sw = jnp.where(_tile_rows(okw, H), sw, NEG_INF)
    _, a_win = attend(init, sw, vwin_ref[0, pl.ds(w0, nw), :])
    o_win = a_win[:, :HEAD_DIM] / a_win[:, HEAD_DIM:]

    g = _sigmoid(g_ref[...])
    gate = lambda c: jnp.concatenate([g[:, 3 * h + c:3 * h + c + 1] for h in range(H)], axis=0)
    o = gate(0) * o_cmp + gate(1) * o_sel + gate(2) * o_win
    o_ref[0] = o.reshape(H, tq, HEAD_DIM)


def _heads_major(x, dtype):
    B, S, _ = x.shape
    return x.reshape(B, S, -1, HEAD_DIM).transpose(0, 2, 1, 3).astype(dtype)


def _nsa_prompt(nq, nqb, rows, rowsb, winb, gates, w_ck, w_cv, tq=512, tk=512, tw=128):
    B, S, _ = nq.shape
    bf16 = jnp.bfloat16
    kcvc = _nsa_compress_blocks(rows.reshape(B * S // NSA_BLOCK, -1), _compress_weights(w_ck, w_cv))
    kcvc = kcvc.reshape(B, S // NSA_BLOCK, 2 * HEAD_DIM)
    kc, vc = kcvc[..., :HEAD_DIM], kcvc[..., HEAD_DIM:]
    pos = jnp.arange(S)
    blk, off = (pos // NSA_BLOCK).astype(jnp.float32), (pos % NSA_BLOCK).astype(jnp.float32)
    one = jnp.ones((S,), jnp.float32)
    slopes = jnp.asarray(NSA_SLOPES, jnp.float32)[:, None]
    q_terms = jnp.stack([-slopes * NSA_BLOCK * blk, -slopes * off, slopes * NSA_BLOCK * one, slopes * one], axis=-1)
    k_terms = jnp.stack([one, one, blk, off], axis=-1)
    pad_terms = lambda t: jnp.pad(t, [(0, 0)] * (t.ndim - 1) + [(0, HEAD_DIM - 4)]).astype(bf16)
    with_k_terms = lambda k: jnp.concatenate([k, jnp.broadcast_to(pad_terms(k_terms), (B, S, HEAD_DIM))], axis=-1)
    qs = jnp.concatenate([_heads_major(nqb, bf16),
                          jnp.broadcast_to(pad_terms(q_terms), (B, NSA_HEADS, S, HEAD_DIM))], axis=-1)
    qf = _heads_major(nq, jnp.float32)
    rows, win = rowsb, winb
    col = lambda a, c: a[..., c * HEAD_DIM:(c + 1) * HEAD_DIM]
    with_ones = lambda v: jnp.concatenate([v, jnp.ones_like(v)], axis=-1)
    full = lambda n, w: pl.BlockSpec((1, n, w), lambda b, i: (b, 0, 0))
    qspec = pl.BlockSpec((1, NSA_HEADS, tq, HEAD_DIM), lambda b, i: (b, 0, i, 0))
    n_cb = S // NSA_BLOCK
    o = pl.pallas_call(
        functools.partial(_nsa_prompt_kernel, tq=tq, tk=tk, tw=tw),
        grid=(B, S // tq),
        in_specs=[pl.BlockSpec((1, NSA_HEADS, tq, 2 * HEAD_DIM), lambda b, i: (b, 0, i, 0)), qspec,
                  full(n_cb, HEAD_DIM), full(n_cb, HEAD_DIM), full(S, 2 * HEAD_DIM),
                  full(S, 2 * HEAD_DIM), full(S, 2 * HEAD_DIM), full(S, 2 * HEAD_DIM),
                  pl.BlockSpec((None, tq, 128), lambda b, i: (b, i, 0))],
        out_specs=qspec,
        out_shape=jax.ShapeDtypeStruct((B, NSA_HEADS, S, HEAD_DIM), jnp.float32),
        scratch_shapes=[pltpu.SMEM((S // tk,), jnp.int32)],
        compiler_params=pltpu.CompilerParams(dimension_semantics=("parallel", "arbitrary"),
                                             vmem_limit_bytes=VMEM_LIMIT_BYTES),
        name="nsa_prompt",
    )(qs, qf, kc, vc, with_k_terms(col(rows, 2)), with_ones(col(rows, 3)),
      with_k_terms(col(win, 0)), with_ones(col(win, 1)), gates)
    return o.transpose(0, 2, 1, 3).reshape(B, S, BRANCH_WIDTH)


def _fox_prompt_kernel(q_ref, k_ref, v_ref, fq_ref, fk_ref, o_ref, *, tq, t):
    i = pl.program_id(1)
    f32, bf16 = jnp.float32, jnp.bfloat16
    lane = lax.broadcasted_iota(jnp.int32, (1, 2 * HEAD_DIM), 1)
    low = lane < HEAD_DIM
    rc = lax.broadcasted_iota(jnp.int32, (tq, t), 0) - lax.broadcasted_iota(jnp.int32, (tq, t), 1)
    n_pairs = FOX_HEADS // 2
    pair_cols = [slice(hp * 2 * HEAD_DIM, (hp + 1) * 2 * HEAD_DIM) for hp in range(n_pairs)]
    q_h, fq_h = [], []
    for hp in range(n_pairs):
        q2 = q_ref[0, :, pair_cols[hp]]
        zq = jnp.zeros_like(q2)
        q_h += [jnp.where(low, q2, zq), jnp.where(low, zq, q2)]
        fq_h += [fq_ref[0, :, 2 * hp:2 * hp + 1], fq_ref[0, :, 2 * hp + 1:2 * hp + 2]]

    def body(j, carry, diag_offset):
        k0 = pl.multiple_of(j * t, t)
        fk = fk_ref[0, j]
        out = []
        for hp in range(n_pairs):
            k2 = k_ref[0, pl.ds(k0, t), pair_cols[hp]]
            v2 = v_ref[0, pl.ds(k0, t), pair_cols[hp]]
            one = jnp.ones_like(v2)
            v1 = (jnp.where(low, v2, one), jnp.where(low, one, v2))
            for x in range(2):
                h = 2 * hp + x
                m, acc = carry[h]
                s = lax.dot_general(q_h[h], k2, _NT, preferred_element_type=f32) + (fq_h[h] - fk[h:h + 1])
                if diag_offset is not None:
                    s = jnp.where(rc >= diag_offset, s, NEG_INF)
                m_new = jnp.maximum(m, jnp.max(s, axis=-1, keepdims=True))
                p = jnp.exp(s - m_new).astype(bf16)
                acc = jnp.exp(m - m_new) * acc + jnp.dot(p, v1[x], preferred_element_type=f32)
                out.append((m_new, acc))
        return tuple(out)

    init = tuple((jnp.full((tq, 1), NEG_INF, f32), jnp.zeros((tq, 2 * HEAD_DIM), f32)) for _ in range(FOX_HEADS))
    n_full = i * (tq // t)
    carry = lax.fori_loop(0, n_full, functools.partial(body, diag_offset=None), init)
    for jj in range(tq // t):
        carry = body(n_full + jj, carry, jj * t)
    for hp in range(n_pairs):
        acc_a, acc_b = carry[2 * hp][1], carry[2 * hp + 1][1]
        den = pltpu.roll(jnp.where(low, acc_b, acc_a), HEAD_DIM, axis=1)
        o_ref[0, :, pair_cols[hp]] = jnp.where(low, acc_a, acc_b) / den


def _logf_scan_kernel(x_ref, lf_ref, f_ref, carry_s, *, t):
    @pl.when(pl.program_id(1) == 0)
    def _():
        carry_s[...] = jnp.zeros_like(carry_s)

    lf = _log_sigmoid(x_ref[0])
    tri = (lax.broadcasted_iota(jnp.int32, (t, t), 1) <= lax.broadcasted_iota(jnp.int32, (t, t), 0))
    c = jnp.dot(tri.astype(jnp.float32), lf, precision=_HI, preferred_element_type=jnp.float32) + carry_s[0:1, :]
    lf_ref[0] = lf
    f_ref[0] = c
    carry_s[0:1, :] = c[t - 1:t, :]


def _logf_scan(logits, t=512):
    B, S, W = logits.shape
    spec = pl.BlockSpec((1, t, W), lambda b, j: (b, j, 0))
    return pl.pallas_call(
        functools.partial(_logf_scan_kernel, t=t),
        grid=(B, S // t),
        in_specs=[spec], out_specs=[spec, spec],
        out_shape=[jax.ShapeDtypeStruct((B, S, W), jnp.float32)] * 2,
        scratch_shapes=[pltpu.VMEM((8, W), jnp.float32)],
        compiler_params=pltpu.CompilerParams(dimension_semantics=("parallel", "arbitrary")),
        name="logf_scan",
    )(logits)


def _fox_prompt_attn(zb, F, tq=1024, t=1024):
    B, S, _ = zb.shape
    W = FOX_HEADS * HEAD_DIM
    cq, ck, cv = ((c - Z_NQ) // W for c in (Z_FQ, Z_FK, Z_FV))
    fk = jnp.pad(F.transpose(0, 2, 1), ((0, 0), (0, 8 - FOX_HEADS), (0, 0)))
    fk = fk.reshape(B, 8, S // t, t).transpose(0, 2, 1, 3)
    return pl.pallas_call(
        functools.partial(_fox_prompt_kernel, tq=tq, t=t),
        grid=(B, S // tq),
        in_specs=[pl.BlockSpec((1, tq, W), lambda b, i: (b, i, cq)),
                  pl.BlockSpec((1, S, W), lambda b, i: (b, 0, ck)),
                  pl.BlockSpec((1, S, W), lambda b, i: (b, 0, cv)),
                  pl.BlockSpec((1, tq, FOX_HEADS), lambda b, i: (b, i, 0)),
                  pl.BlockSpec((1, S // t, 8, t), lambda b, i: (b, 0, 0, 0))],
        out_specs=pl.BlockSpec((1, tq, W), lambda b, i: (b, i, 0)),
        out_shape=jax.ShapeDtypeStruct((B, S, W), jnp.float32),
        compiler_params=pltpu.CompilerParams(dimension_semantics=("parallel", "arbitrary"),
                                             vmem_limit_bytes=VMEM_LIMIT_BYTES),
        name="fox_prompt",
    )(zb, zb, zb, F, fk)


def _log_sigmoid(x):
    return jnp.minimum(x, 0.0) - jnp.log1p(jnp.exp(-jnp.abs(x)))


def _mlstm_kernel(q_ref, k_ref, kt_ref, v_ref, og_ref, gc_ref, gr_ref, nw_ref, cn0_ref, m0_ref,
                  h_ref, cn_ref, m_ref, cn_s, m_s, *, nb, L):
    c = pl.program_id(1)
    f32 = jnp.float32
    W = 2 * HEAD_DIM
    n_pairs = MLSTM_HEADS // 2

    @pl.when(c == 0)
    def _():
        cn_s[...] = cn0_ref[...]
        m_s[...] = m0_ref[...]

    lane = lax.broadcasted_iota(jnp.int32, (1, W), 1)
    low = lane < HEAD_DIM
    ti = lax.broadcasted_iota(jnp.int32, (L, L), 0)
    si = lax.broadcasted_iota(jnp.int32, (L, L), 1)
    causal = si <= ti
    tri = causal.astype(f32)
    tri_t = (ti <= si).astype(f32)
    srow = lax.broadcasted_iota(jnp.int32, (W, 2 * W), 0)
    slane = lax.broadcasted_iota(jnp.int32, (W, 2 * W), 1)
    top = srow < HEAD_DIM
    keep_a = top & ((slane < HEAD_DIM) | (slane == W))
    keep_b = (~top) & (((slane >= HEAD_DIM) & (slane < W)) | (slane == W + 1))
    lane_w = lax.broadcasted_iota(jnp.int32, (1, W), 1)
    mdt = k_ref.dtype

    for b in range(nb):
        gcol = gc_ref[b]
        grow = gr_ref[b]
        bcol = jnp.dot(tri, _log_sigmoid(gcol), precision=_HI, preferred_element_type=f32)
        brow = jnp.dot(_log_sigmoid(grow), tri_t, precision=_HI, preferred_element_type=f32)
        for hp in range(n_pairs):
            cols = slice(hp * W, (hp + 1) * W)
            q2 = q_ref[b, :, cols]
            k2 = k_ref[b, :, cols]
            v2 = v_ref[b, :, cols]
            kt2 = kt_ref[b, cols, :]
            cn = cn_s[b, hp]
            r = jnp.dot(q2, cn.astype(mdt), preferred_element_type=f32)
            zq = jnp.zeros_like(q2)
            per_head = []
            for x in range(2):
                h = 2 * hp + x
                qx = jnp.where(low, q2, zq) if x == 0 else jnp.where(low, zq, q2)
                b_c = bcol[:, MLSTM_HEADS + h:MLSTM_HEADS + h + 1]
                b_r = brow[MLSTM_HEADS + h:MLSTM_HEADS + h + 1, :]
                ig_c = gcol[:, h:h + 1]
                ig_r = grow[h:h + 1, :]
                m_prev = m_s[b, 0:1, h:h + 1]
                dmat = jnp.where(causal, b_c - b_r + ig_r, NEG_INF)
                a_c = b_c + m_prev
                m_t = jnp.maximum(a_c, jnp.max(dmat, axis=-1, keepdims=True))
                wq = jnp.exp(dmat - m_t) * lax.dot_general(qx, k2, _NT, preferred_element_type=f32)
                inter = jnp.exp(a_c - m_t)
                wv = jnp.dot(wq.astype(mdt), v2, preferred_element_type=f32)
                den = inter * r[:, W + x:W + x + 1] + jnp.sum(wq, axis=-1, keepdims=True)
                den = jnp.maximum(jnp.abs(den), jnp.exp(-m_t))
                bl = b_c[L - 1:L, :]
                g_c = bl - b_c + ig_c
                m_new = jnp.maximum(bl + m_prev, jnp.max(g_c, axis=0, keepdims=True))
                ws = jnp.exp(g_c - m_new)
                decay = jnp.exp(bl + m_prev - m_new)
                aug = jnp.concatenate([v2.astype(f32) * ws, jnp.where(lane_w == x, ws, 0.0)], axis=1)
                u = jnp.dot(kt2, aug.astype(mdt), preferred_element_type=f32)
                per_head.append((inter, wv, den, decay, u))
                m_s[b, 0:1, h:h + 1] = m_new
            (ia, wva, dena, deca, ua), (ib, wvb, denb, decb, ub) = per_head
            num = jnp.where(low, ia * r[:, :W] + wva, ib * r[:, :W] + wvb)
            hid = num / jnp.where(low, dena, denb)
            cn_s[b, hp] = (jnp.where(top, deca, decb) * cn + jnp.where(keep_a, ua, 0.0)
                           + jnp.where(keep_b, ub, 0.0))
            hid = _sigmoid(og_ref[b, :, cols]) * hid
            mean = lambda t: jnp.where(low, jnp.sum(jnp.where(low, t, 0.0), axis=-1, keepdims=True),
                                       jnp.sum(jnp.where(low, 0.0, t), axis=-1, keepdims=True)) / HEAD_DIM
            mu = mean(hid)
            var = mean(jnp.square(hid - mu))
            h_ref[b, :, cols] = (hid - mu) * lax.rsqrt(var + LN_EPS) * nw_ref[:, cols]

    @pl.when(c == pl.num_programs(1) - 1)
    def _():
        cn_ref[...] = cn_s[...]
        m_ref[...] = m_s[...]


def _mlstm_state_pack(c, n, m):
    Bx = c.shape[0]
    HD, W = HEAD_DIM, 2 * HEAD_DIM
    cp = c.reshape(Bx, MLSTM_HEADS // 2, 2, HD, HD)
    np_ = n.reshape(Bx, MLSTM_HEADS // 2, 2, HD, 1)
    z = lambda w: jnp.zeros((Bx, MLSTM_HEADS // 2, HD, w), jnp.float32)
    top = jnp.concatenate([cp[:, :, 0], z(HD), np_[:, :, 0], z(W - 1)], axis=-1)
    bot = jnp.concatenate([z(HD), cp[:, :, 1], z(1), np_[:, :, 1], z(W - 2)], axis=-1)
    cn = jnp.concatenate([top, bot], axis=-2)
    m8 = jnp.pad(m[:, None, :], ((0, 0), (0, 7), (0, 128 - MLSTM_HEADS)))
    return cn, m8


def _mlstm_state_unpack(cn, m8):
    W = 2 * HEAD_DIM
    c = jnp.stack([cn[:, h // 2, (h % 2) * HEAD_DIM:(h % 2 + 1) * HEAD_DIM,
                      (h % 2) * HEAD_DIM:(h % 2 + 1) * HEAD_DIM] for h in range(MLSTM_HEADS)], axis=1)
    n = jnp.stack([cn[:, h // 2, (h % 2) * HEAD_DIM:(h % 2 + 1) * HEAD_DIM, W + h % 2]
                   for h in range(MLSTM_HEADS)], axis=1)
    return c, n, m8[:, 0, :MLSTM_HEADS]


def _mlstm(q, k, v, og, mi, mf, norm_w, c0, n0, m0, L, nb, mxu_dtype):
    Bx, S, W4 = q.shape
    f32 = jnp.float32
    gates = jnp.concatenate([mi, mf], axis=-1).astype(f32)
    gcol = jnp.pad(gates, ((0, 0), (0, 0), (0, 128 - 2 * MLSTM_HEADS)))
    grow = gates.transpose(0, 2, 1)
    cn0, m8 = _mlstm_state_pack(c0.astype(f32), n0.astype(f32), m0.astype(f32))
    tok = lambda w: pl.BlockSpec((nb, L, w), lambda b, c: (b, c, 0))
    st_cn = pl.BlockSpec((nb, MLSTM_HEADS // 2, 128, 256), lambda b, c: (b, 0, 0, 0))
    st_m = pl.BlockSpec((nb, 8, 128), lambda b, c: (b, 0, 0))
    h, cn, m8 = pl.pallas_call(
        functools.partial(_mlstm_kernel, nb=nb, L=L),
        grid=(Bx // nb, S // L),
        in_specs=[tok(W4), tok(W4), pl.BlockSpec((nb, W4, L), lambda b, c: (b, 0, c)), tok(W4), tok(W4),
                  tok(128), pl.BlockSpec((nb, 8, L), lambda b, c: (b, 0, c)),
                  pl.BlockSpec((1, W4), lambda b, c: (0, 0)), st_cn, st_m],
        out_specs=[tok(W4), st_cn, st_m],
        out_shape=[jax.ShapeDtypeStruct((Bx, S, W4), f32),
                   jax.ShapeDtypeStruct(cn0.shape, f32), jax.ShapeDtypeStruct(m8.shape, f32)],
        scratch_shapes=[pltpu.VMEM((nb, MLSTM_HEADS // 2, 128, 256), f32), pltpu.VMEM((nb, 8, 128), f32)],
        compiler_params=pltpu.CompilerParams(dimension_semantics=("parallel", "arbitrary"),
                                             vmem_limit_bytes=VMEM_LIMIT_BYTES),
        name="mlstm",
    )(q.astype(mxu_dtype), k.astype(mxu_dtype), k.astype(mxu_dtype).transpose(0, 2, 1), v.astype(mxu_dtype),
      og.astype(f32), gcol, grow, norm_w.astype(f32).reshape(1, W4), cn0, m8)
    return h, _mlstm_state_unpack(cn, m8)


def _ln(x, w, b):
    mu = jnp.mean(x, axis=-1, keepdims=True)
    var = jnp.mean(jnp.square(x - mu), axis=-1, keepdims=True)
    return (x - mu) * lax.rsqrt(var + LN_EPS) * w + b


def _ln_kernel(x_ref, w_ref, b_ref, o_ref):
    o_ref[...] = _ln(x_ref[...], w_ref[...], b_ref[...])


def _layer_norm_rows(x, w, b, tm=1024):
    T, D = x.shape
    vec = pl.BlockSpec((1, D), lambda i: (0, 0))
    return pl.pallas_call(
        _ln_kernel, grid=(T // tm,),
        in_specs=[pl.BlockSpec((tm, D), lambda i: (i, 0)), vec, vec],
        out_specs=pl.BlockSpec((tm, D), lambda i: (i, 0)),
        out_shape=jax.ShapeDtypeStruct((T, D), jnp.float32),
        name="layer_norm",
    )(x, w.reshape(1, D), b.reshape(1, D))


Z_NQ = 0
Z_ROWS = Z_NQ + 256
Z_MQ = 512
Z_MK = Z_MQ + 256
Z_MV = 1024
Z_MO = Z_MV + 256
Z_FQ = 1536
Z_WIN = Z_FQ + 256
Z_SMALL = Z_WIN + 128
Z_FK = 2048
Z_FV = Z_FK + 256
Z_WIDTH = 2560
Z_TILE = 512
ZB_WIDTH = Z_WIDTH
GM_SPLIT = 13
SMALL_MI, SMALL_MF, SMALL_FF = 12, 16, 20


def _z_column_order():
    starts = np.concatenate([[0], np.cumsum(IN_SPLITS)])
    seg = lambda i, lo=0, hi=None: np.arange(starts[i] + lo, starts[i] + (IN_SPLITS[i] if hi is None else hi))
    pad = lambda n: np.full((n,), -1)
    order = np.concatenate([
        seg(0), seg(1, 0, 256), seg(3), seg(4), seg(5), seg(6),
        seg(9), seg(1, 256, 384), seg(2), seg(7), seg(8), seg(12), pad(128 - 24), seg(10), seg(11)])
    assert order.shape == (Z_WIDTH,)
    return order


def _permute_in_proj(w_in, b_in):
    order = _z_column_order()
    valid = jnp.asarray(order >= 0)
    idx = jnp.asarray(np.maximum(order, 0))
    scale = np.ones((Z_WIDTH,), np.float32)
    for c0 in (Z_NQ, Z_MK, Z_FQ):
        scale[c0:c0 + 256] = ATTN_SCALE
    scale = jnp.asarray(scale)
    w = jnp.where(valid[None, :], w_in[:, idx], 0.0) * scale[None, :]
    b = jnp.where(valid, b_in[idx], 0.0) * scale
    return w, b


def _proj_kernel(x_ref, w_ref, wlo_ref, b_ref, o_ref, ob_ref, xh_s, xl_s, *, hi_tile):
    j = pl.program_id(1)
    f32 = jnp.float32
    _proj_tile(x_ref, w_ref, wlo_ref, b_ref, o_ref, xh_s, xl_s, j, hi_tile)
    ob_ref[...] = o_ref[...].astype(jnp.bfloat16)


def _proj_tile(x_ref, w_ref, wlo_ref, b_ref, o_ref, xh_s, xl_s, j, hi_tile):
    f32 = jnp.float32

    @pl.when(j == 0)
    def _():
        x = x_ref[...]
        xh = x.astype(jnp.bfloat16)
        xh_s[...] = xh
        xl_s[...] = (x - xh.astype(f32)).astype(jnp.bfloat16)

    @pl.when(j != hi_tile)
    def _():
        o_ref[...] = jnp.dot(xh_s[...], w_ref[...], preferred_element_type=f32) + b_ref[...]

    @pl.when(j == hi_tile)
    def _():
        acc = jnp.dot(xh_s[...], wlo_ref[...], preferred_element_type=f32)
        acc += jnp.dot(xl_s[...], w_ref[...], preferred_element_type=f32)
        acc += jnp.dot(xh_s[...], w_ref[...], preferred_element_type=f32)
        o_ref[...] = acc + b_ref[...]


def _in_projection(x, w_in, b_in, tm=1024):
    T, D = x.shape
    w, b = _permute_in_proj(w_in, b_in)
    wh = w.astype(jnp.bfloat16)
    hi_tile = Z_NQ // Z_TILE
    wlo = (w[:, Z_NQ:Z_NQ + Z_TILE] - wh[:, Z_NQ:Z_NQ + Z_TILE].astype(jnp.float32)).astype(jnp.bfloat16)
    return pl.pallas_call(
        functools.partial(_proj_kernel, hi_tile=hi_tile),
        grid=(T // tm, Z_WIDTH // Z_TILE),
        in_specs=[pl.BlockSpec((tm, D), lambda i, j: (i, 0)),
                  pl.BlockSpec((D, Z_TILE), lambda i, j: (0, j)),
                  pl.BlockSpec((D, Z_TILE), lambda i, j: (0, 0)),
                  pl.BlockSpec((1, Z_TILE), lambda i, j: (0, j))],
        out_specs=[pl.BlockSpec((tm, Z_TILE), lambda i, j: (i, j)),
                   pl.BlockSpec((tm, Z_TILE), lambda i, j: (i, j))],
        out_shape=[jax.ShapeDtypeStruct((T, Z_WIDTH), jnp.float32),
                   jax.ShapeDtypeStruct((T, ZB_WIDTH), jnp.bfloat16)],
        scratch_shapes=[pltpu.VMEM((tm, D), jnp.bfloat16), pltpu.VMEM((tm, D), jnp.bfloat16)],
        compiler_params=pltpu.CompilerParams(dimension_semantics=("parallel", "arbitrary"),
                                             vmem_limit_bytes=VMEM_LIMIT_BYTES),
        name="in_projection",
    )(x, wh, wlo, b.reshape(1, Z_WIDTH))


def _merge_kernel(x_ref, on_ref, om_ref, of_ref, wg_ref, bg_ref, wb_ref, wo_ref, lw_ref, lb_ref, o_ref, *, alpha):
    f32, bf16 = jnp.float32, jnp.bfloat16
    x = x_ref[...]
    xb = x.astype(bf16)
    y = None
    for m, br in enumerate((on_ref, om_ref, of_ref)):
        gcols = slice(m * D_MODEL, (m + 1) * D_MODEL)
        gm = jnp.dot(xb, wg_ref[:, gcols], preferred_element_type=f32) + bg_ref[:, gcols]
        proj = jnp.dot(br[...].astype(bf16), wb_ref[m], preferred_element_type=f32)
        term = _sigmoid(gm) * proj
        y = term if y is None else y + term
    mix = jnp.dot(y.astype(bf16), wo_ref[...], preferred_element_type=f32)
    o_ref[...] = _ln(alpha * x + mix, lw_ref[...], lb_ref[...])


def _merge_ln(x, o_nsa, o_ml, o_fox, w_in, b_in, w_branch, w_out, ln_w, ln_b, alpha, tm=512):
    T, D = x.shape
    g0 = int(np.sum(IN_SPLITS[:GM_SPLIT]))
    w_gm = w_in[:, g0:g0 + N_BRANCH * D].astype(jnp.bfloat16)
    b_gm = b_in[g0:g0 + N_BRANCH * D].reshape(1, N_BRANCH * D)
    tok = lambda w: pl.BlockSpec((tm, w), lambda i: (i, 0))
    vec = pl.BlockSpec((1, D), lambda i: (0, 0))
    return pl.pallas_call(
        functools.partial(_merge_kernel, alpha=alpha),
        grid=(T // tm,),
        in_specs=[tok(D), tok(BRANCH_WIDTH), tok(BRANCH_WIDTH), tok(BRANCH_WIDTH),
                  pl.BlockSpec((D, N_BRANCH * D), lambda i: (0, 0)),
                  pl.BlockSpec((1, N_BRANCH * D), lambda i: (0, 0)),
                  pl.BlockSpec((N_BRANCH, BRANCH_WIDTH, D), lambda i: (0, 0, 0)),
                  pl.BlockSpec((D, D), lambda i: (0, 0)), vec, vec],
        out_specs=tok(D),
        out_shape=jax.ShapeDtypeStruct((T, D), jnp.float32),
        compiler_params=pltpu.CompilerParams(dimension_semantics=("parallel",),
                                             vmem_limit_bytes=VMEM_LIMIT_BYTES),
        name="merge_ln",
    )(x, o_nsa, o_ml, o_fox, w_gm, b_gm, w_branch.astype(jnp.bfloat16), w_out.astype(jnp.bfloat16),
      ln_w.reshape(1, D), ln_b.reshape(1, D))


def _route(logits):
    tm = logits.shape[0]
    lane = lax.broadcasted_iota(jnp.int32, (tm, 128), 1)
    lanef = lane.astype(jnp.float32)
    big = 1e9
    is_g = lane < N_GROUPS
    lg = jnp.where(is_g, logits, -jnp.inf)
    eg = jnp.exp(lg - jnp.max(lg, axis=-1, keepdims=True))
    pg = eg / jnp.sum(eg, axis=-1, keepdims=True)
    g_val = jnp.max(pg, axis=-1, keepdims=True)
    g_idx = jnp.min(jnp.where(is_g & (pg == g_val), lanef, big), axis=-1, keepdims=True)
    e_lo = N_GROUPS + EXPERTS_PER_GROUP * g_idx
    in_grp = (lanef >= e_lo) & (lanef < e_lo + EXPERTS_PER_GROUP)
    le = jnp.where(in_grp, logits, -jnp.inf)
    ee = jnp.exp(le - jnp.max(le, axis=-1, keepdims=True))
    pe = ee / jnp.sum(ee, axis=-1, keepdims=True)
    v1 = jnp.max(pe, axis=-1, keepdims=True)
    i1 = jnp.min(jnp.where(in_grp & (pe == v1), lanef, big), axis=-1, keepdims=True)
    rest = in_grp & (lanef != i1)
    pe2 = jnp.where(rest, pe, -1.0)
    v2 = jnp.max(pe2, axis=-1, keepdims=True)
    i2 = jnp.min(jnp.where(rest & (pe2 == v2), lanef, big), axis=-1, keepdims=True)
    tot = v1 + v2
    return jnp.where(lanef == i1, g_val * v1 / tot, jnp.where(lanef == i2, g_val * v2 / tot, 0.0))


def _moe_kernel(x_ref, wr_ref, br_ref, wg_ref, wu_ref, wd_ref, lw_ref, lb_ref, o_ref, xb_s, gate_s, acc_s, *, alpha):
    e = pl.program_id(1)
    f32, bf16 = jnp.float32, jnp.bfloat16

    @pl.when(e == 0)
    def _():
        x = x_ref[...]
        xb_s[...] = x.astype(bf16)
        logits = jnp.dot(x, wr_ref[...], precision=_HI, preferred_element_type=f32) + br_ref[...]
        gate_s[...] = _route(logits)
        acc_s[...] = jnp.zeros_like(acc_s)

    lane = lax.broadcasted_iota(jnp.int32, gate_s.shape, 1)
    y = None
    for k in range(wg_ref.shape[0]):
        ex = e * wg_ref.shape[0] + k
        gate = jnp.sum(jnp.where(lane == N_GROUPS + ex, gate_s[...], 0.0), axis=-1, keepdims=True)
        g = jnp.dot(xb_s[...], wg_ref[k].astype(bf16), preferred_element_type=f32)
        u = jnp.dot(xb_s[...], wu_ref[k].astype(bf16), preferred_element_type=f32)
        h = (g * _sigmoid(g)) * u * gate
        yk = jnp.dot(h.astype(bf16), wd_ref[k].astype(bf16), preferred_element_type=f32)
        y = yk if y is None else y + yk
    acc_s[...] += y

    @pl.when(e == pl.num_programs(1) - 1)
    def _():
        o_ref[...] = _ln(alpha * x_ref[...] + acc_s[...], lw_ref[...], lb_ref[...])


def _moe_ln(x, w_group, b_group, w_expert, b_expert, w_gate, w_up, w_down, ln_w, ln_b, alpha, tm=1024, eps=2):
    T, D = x.shape
    bf16 = jnp.bfloat16
    n_r = N_GROUPS + N_EXPERTS
    wr = jnp.pad(jnp.concatenate([w_group, w_expert], axis=1), ((0, 0), (0, 128 - n_r)))
    br = jnp.pad(jnp.concatenate([b_group, b_expert]), (0, 128 - n_r)).reshape(1, 128)
    vec = pl.BlockSpec((1, D), lambda i, e: (0, 0))
    return pl.pallas_call(
        functools.partial(_moe_kernel, alpha=alpha),
        grid=(T // tm, N_EXPERTS // eps),
        in_specs=[pl.BlockSpec((tm, D), lambda i, e: (i, 0)),
                  pl.BlockSpec((D, 128), lambda i, e: (0, 0)),
                  pl.BlockSpec((1, 128), lambda i, e: (0, 0)),
                  pl.BlockSpec((eps, D, D_EXPERT), lambda i, e: (e, 0, 0)),
                  pl.BlockSpec((eps, D, D_EXPERT), lambda i, e: (e, 0, 0)),
                  pl.BlockSpec((eps, D_EXPERT, D), lambda i, e: (e, 0, 0)), vec, vec],
        out_specs=pl.BlockSpec((tm, D), lambda i, e: (i, 0)),
        out_shape=jax.ShapeDtypeStruct((T, D), jnp.float32),
        scratch_shapes=[pltpu.VMEM((tm, D), bf16), pltpu.VMEM((tm, 128), jnp.float32),
                        pltpu.VMEM((tm, D), jnp.float32)],
        compiler_params=pltpu.CompilerParams(dimension_semantics=("parallel", "arbitrary"),
                                             vmem_limit_bytes=VMEM_LIMIT_BYTES),
        name="moe_ln",
    )(x, wr, br, w_gate, w_up, w_down, ln_w.reshape(1, D), ln_b.reshape(1, D))


N_PAGES = PAST_LEN // PAGE_SIZE
NEW_PAD = 128
DEC_KEYS = PAST_LEN + NEW_PAD
DEC_ROWS = NSA_HEADS * DEC_SEQ


def _pages_token_minor(cache):
    nd = cache.ndim
    t = cache.transpose((0, 1) + tuple(range(3, nd)) + (2,))
    return t.reshape(cache.shape[0] * cache.shape[1], -1, cache.shape[2])


def _page_specs(rows, row_block, layer, n_phys):
    def spec(p):
        return pl.BlockSpec((1, rows, PAGE_SIZE),
                            lambda b, pt: (layer * n_phys + pt[b * N_PAGES + p], row_block, 0))
    return [spec(p) for p in range(N_PAGES)]


def _softmax_rows(s):
    m = jnp.max(s, axis=-1, keepdims=True)
    p = jnp.exp(s - m)
    return p, jnp.sum(p, axis=-1, keepdims=True)


def _fox_decode_kernel(pt_ref, qbd_ref, knew_ref, vnew_ref, fk_ref, fq_ref, *refs):
    pages, o_ref = refs[:N_PAGES], refs[N_PAGES]
    f32, bf16 = jnp.float32, jnp.bfloat16
    W = FOX_HEADS * HEAD_DIM
    qbd = qbd_ref[0]
    s = [jnp.dot(qbd, pg[0, :W, :].astype(bf16), preferred_element_type=f32) for pg in pages]
    s.append(jnp.dot(qbd, knew_ref[0], preferred_element_type=f32))
    s = jnp.concatenate(s, axis=1)
    rowh = lax.broadcasted_iota(jnp.int32, (DEC_ROWS, 1), 0) // DEC_SEQ
    fk = fk_ref[0]
    fk_rows = jnp.where(rowh == 0, fk[0:1], jnp.where(rowh == 1, fk[1:2], jnp.where(rowh == 2, fk[2:3], fk[3:4])))
    col = lax.broadcasted_iota(jnp.int32, (DEC_ROWS, DEC_KEYS), 1)
    t = lax.broadcasted_iota(jnp.int32, (DEC_ROWS, DEC_KEYS), 0) % DEC_SEQ
    ok = (col < PAST_LEN) | (col - PAST_LEN <= t)
    s = jnp.where(ok, s + (fq_ref[0] - fk_rows), NEG_INF)
    p, l = _softmax_rows(s)
    pb = p.astype(bf16)
    o = lax.dot_general(pb[:, PAST_LEN:], vnew_ref[0], _NT, preferred_element_type=f32)
    for i, pg in enumerate(pages):
        o += lax.dot_general(pb[:, i * PAGE_SIZE:(i + 1) * PAGE_SIZE], pg[0, W:, :].astype(bf16), _NT,
                             preferred_element_type=f32)
    o = o / l
    lane_h = lax.broadcasted_iota(jnp.int32, (DEC_ROWS, W), 1) // HEAD_DIM
    o = jnp.where(lane_h == rowh, o, 0.0)
    o_ref[0] = o[0:8] + o[8:16] + o[16:24] + o[24:32]


def _pad_new_t(x):
    return jnp.pad(x.transpose(0, 2, 1), ((0, 0), (0, 0), (0, NEW_PAD - DEC_SEQ))).astype(jnp.bfloat16)


def _fox_decode(q, k_new, v_new, lf_new, cache_kv, cache_lf, page_table, layer):
    DB = q.shape[0]
    f32, bf16 = jnp.float32, jnp.bfloat16
    W = FOX_HEADS * HEAD_DIM
    eye = jnp.eye(FOX_HEADS, dtype=f32)
    qh = q.reshape(DB, DEC_SEQ, FOX_HEADS, HEAD_DIM).transpose(0, 2, 1, 3)
    qbd = (qh[:, :, :, None, :] * eye[None, :, None, :, None]).reshape(DB, DEC_ROWS, W).astype(bf16)
    lf_all = jnp.concatenate([cache_lf[layer][page_table].reshape(DB, PAST_LEN, FOX_HEADS).astype(f32), lf_new], axis=1)
    F = jnp.cumsum(lf_all, axis=1)
    fk = jnp.pad(F.transpose(0, 2, 1), ((0, 0), (0, 8 - FOX_HEADS), (0, DEC_KEYS - PAST_LEN - DEC_SEQ)))
    fq = F[:, PAST_LEN:].transpose(0, 2, 1).reshape(DB, DEC_ROWS, 1)
    per_seq = lambda r, w: pl.BlockSpec((1, r, w), lambda b, pt: (b, 0, 0))
    pages = _pages_token_minor(cache_kv)
    return pl.pallas_call(
        _fox_decode_kernel,
        grid_spec=pltpu.PrefetchScalarGridSpec(
            num_scalar_prefetch=1, grid=(DB,),
            in_specs=[per_seq(DEC_ROWS, W), per_seq(W, NEW_PAD), per_seq(W, NEW_PAD), per_seq(8, DEC_KEYS),
                      per_seq(DEC_ROWS, 1)] + _page_specs(2 * W, 0, layer, cache_kv.shape[1]),
            out_specs=per_seq(DEC_SEQ, W)),
        out_shape=jax.ShapeDtypeStruct((DB, DEC_SEQ, W), f32),
        compiler_params=pltpu.CompilerParams(dimension_semantics=("parallel",), vmem_limit_bytes=VMEM_LIMIT_BYTES),
        name="fox_decode",
    )(page_table.reshape(-1), qbd, _pad_new_t(k_new), _pad_new_t(v_new), fk, fq, *([pages] * N_PAGES))


def _nsa_decode_kernel(pt_ref, qs_ref, qf_ref, kcvc_ref, rnew_ref, wbuf_ref, wnew_ref, g_ref, e_ref, *refs):
    pages, o_ref = refs[:N_PAGES], refs[N_PAGES]
    f32, bf16 = jnp.float32, jnp.bfloat16
    H, T, HD = NSA_HEADS, DEC_SEQ, HEAD_DIM
    qs = qs_ref[0]
    row = lax.broadcasted_iota(jnp.int32, (DEC_ROWS, 1), 0)
    head = row // T
    slope = jnp.where(head == 0, NSA_SLOPES[0], jnp.where(head == 1, NSA_SLOPES[1],
                      jnp.where(head == 2, NSA_SLOPES[2], NSA_SLOPES[3]))).astype(f32)
    posq = PAST_LEN + row % T

    kcvc = kcvc_ref[0]
    n_cb = kcvc.shape[0]
    sc = lax.dot_general(qf_ref[0], kcvc, _NT, precision=_HI, preferred_element_type=f32)
    jb = lax.broadcasted_iota(jnp.int32, (DEC_ROWS, n_cb), 1)
    distc = posq - (jb * NSA_BLOCK + NSA_BLOCK - 1)
    okc = distc >= 0
    sc = jnp.where(okc, sc - slope * distc.astype(f32), NEG_INF)
    pc = jnp.exp(sc - jnp.max(sc, axis=-1, keepdims=True)) * okc.astype(f32)
    pc = pc / jnp.maximum(jnp.sum(pc, axis=-1, keepdims=True), TINY)
    o_cmp = jnp.dot(pc.astype(bf16), kcvc.astype(bf16), preferred_element_type=f32)[:, HD:]
    imp = pc[0:T] + pc[T:2 * T] + pc[2 * T:3 * T] + pc[3 * T:4 * T]
    imp = jnp.concatenate([imp, jnp.zeros((T, 128 - n_cb), f32)], axis=1)
    cur = (PAST_LEN + lax.broadcasted_iota(jnp.int32, (T, 1), 0)) // NSA_BLOCK
    msel = _select_blocks_by_rank(imp, cur, NSA_TOPK, n_cb + 1).astype(bf16)

    mexp = jnp.dot(msel, e_ref[...], preferred_element_type=f32)
    col = lax.broadcasted_iota(jnp.int32, (T, DEC_KEYS), 1)
    tq = lax.broadcasted_iota(jnp.int32, (T, DEC_KEYS), 0)
    d = PAST_LEN + tq - col
    ok = (mexp > 0.5) & (d >= 0)
    kv = [pg[0].astype(bf16) for pg in pages] + [rnew_ref[0]]
    s = jnp.concatenate([jnp.dot(qs, x[:HD], preferred_element_type=f32) for x in kv], axis=1)
    s = jnp.where(_tile_rows(ok, H), s - slope * _tile_rows(d.astype(f32), H), NEG_INF)
    p, l = _softmax_rows(s)
    pb = p.astype(bf16)
    acc = jnp.zeros((DEC_ROWS, HD), f32)
    for i, x in enumerate(kv):
        acc += lax.dot_general(pb[:, i * PAGE_SIZE:(i + 1) * PAGE_SIZE], x[HD:], _NT, preferred_element_type=f32)
    o_sel = acc / l

    wb = wbuf_ref.shape[2]
    kvw = [wbuf_ref[0].astype(bf16), wnew_ref[0]]
    sw = jnp.concatenate([jnp.dot(qs, x[:HD], preferred_element_type=f32) for x in kvw], axis=1)
    colw = lax.broadcasted_iota(jnp.int32, (T, wb + NEW_PAD), 1)
    tw = lax.broadcasted_iota(jnp.int32, (T, wb + NEW_PAD), 0)
    dw = wb + tw - colw
    okw = (dw >= 0) & (dw < NSA_WINDOW)
    sw = jnp.where(_tile_rows(okw, H), sw - slope * _tile_rows(dw.astype(f32), H), NEG_INF)
    pw, lw = _softmax_rows(sw)
    pwb = pw.astype(bf16)
    accw = (lax.dot_general(pwb[:, :wb], kvw[0][HD:], _NT, preferred_element_type=f32)
            + lax.dot_general(pwb[:, wb:], kvw[1][HD:], _NT, preferred_element_type=f32))
    o_win = accw / lw

    g = _sigmoid(g_ref[0])
    gate = lambda c: jnp.concatenate([g[:, 3 * h + c:3 * h + c + 1] for h in range(H)], axis=0)
    o_ref[0] = gate(0) * o_cmp + gate(1) * o_sel + gate(2) * o_win


def _compress_pages_kernel(x_ref, wk_ref, wv_ref, o_ref):
    tm = x_ref.shape[0]
    f32 = jnp.float32
    acc_k = jnp.zeros((tm, 2 * HEAD_DIM), f32)
    acc_v = jnp.zeros((tm, 2 * HEAD_DIM), f32)
    pair = lambda r: jnp.concatenate([x_ref[:, r, :], x_ref[:, r + 1, :]], axis=1)
    for i in range(HEAD_DIM // 2):
        acc_k += jnp.dot(pair(2 * i), wk_ref[i], precision=_HI, preferred_element_type=f32)
        acc_v += jnp.dot(pair(HEAD_DIM + 2 * i), wv_ref[i], preferred_element_type=f32)
    o_ref[...] = jnp.concatenate([acc_k, acc_v], axis=1)


def _nsa_compress_pages(pages, w_ck, w_cv, layer, n_phys, tm=128):
    eye = jnp.eye(PAGE_SIZE // NSA_BLOCK, dtype=jnp.float32)
    big = lambda w: jnp.einsum('pde,bc->dbpce', w, eye).reshape(HEAD_DIM // 2, 2 * PAGE_SIZE, 2 * HEAD_DIM)
    wspec = pl.BlockSpec((HEAD_DIM // 2, 2 * PAGE_SIZE, 2 * HEAD_DIM), lambda i: (0, 0, 0))
    return pl.pallas_call(
        _compress_pages_kernel,
        grid=(n_phys // tm,),
        in_specs=[pl.BlockSpec((tm, 2 * HEAD_DIM, PAGE_SIZE), lambda i: (layer * (n_phys // tm) + i, 0, 0)),
                  wspec, wspec],
        out_specs=pl.BlockSpec((tm, 4 * HEAD_DIM), lambda i: (i, 0)),
        out_shape=jax.ShapeDtypeStruct((n_phys, 4 * HEAD_DIM), jnp.float32),
        compiler_params=pltpu.CompilerParams(dimension_semantics=("parallel",), vmem_limit_bytes=VMEM_LIMIT_BYTES),
        name="nsa_compress_pages",
    )(pages, big(w_ck), big(w_cv))


def _nsa_decode(nq, rows_new, win_new, gates, cache_rows, cache_win, page_table, w_ck, w_cv, layer):
    DB = nq.shape[0]
    f32, bf16 = jnp.float32, jnp.bfloat16
    HD = HEAD_DIM
    n_phys = cache_rows.shape[1]
    n_blk = PAGE_SIZE // NSA_BLOCK
    pages = _pages_token_minor(cache_rows)
    kcvc = _nsa_compress_pages(pages, w_ck, w_cv, layer, n_phys)[page_table]
    kcvc = kcvc.reshape(DB, N_PAGES, 2, n_blk, HD).transpose(0, 1, 3, 2, 4).reshape(DB, N_PAGES * n_blk, 2 * HD)
    stack = lambda x: x.reshape(DB, DEC_SEQ, NSA_HEADS, HD).transpose(0, 2, 1, 3).reshape(DB, DEC_ROWS, HD)
    qs = stack(nq).astype(bf16)
    qf = jnp.pad(stack(nq), ((0, 0), (0, 0), (0, HD))).astype(f32)
    win_t = cache_win.transpose(0, 1, 3, 4, 2).reshape(cache_win.shape[0] * DB, 2 * HD, cache_win.shape[2])
    colk = np.arange(DEC_KEYS)
    e = (np.arange(128)[:, None] == colk[None, :] // NSA_BLOCK) & (colk[None, :] < PAST_LEN + NSA_BLOCK)
    wb = cache_win.shape[2]
    per_seq = lambda r, w: pl.BlockSpec((1, r, w), lambda b, pt: (b, 0, 0))
    o = pl.pallas_call(
        _nsa_decode_kernel,
        grid_spec=pltpu.PrefetchScalarGridSpec(
            num_scalar_prefetch=1, grid=(DB,),
            in_specs=[per_seq(DEC_ROWS, HD), per_seq(DEC_ROWS, 2 * HD), per_seq(PAST_LEN // NSA_BLOCK, 2 * HD),
                      per_seq(2 * HD, NEW_PAD),
                      pl.BlockSpec((1, 2 * HD, wb), lambda b, pt: (layer * DB + b, 0, 0)),
                      per_seq(2 * HD, NEW_PAD), per_seq(DEC_SEQ, 128),
                      pl.BlockSpec((128, DEC_KEYS), lambda b, pt: (0, 0))]
                     + _page_specs(2 * HD, 1, layer, n_phys),
            out_specs=per_seq(DEC_ROWS, HD)),
        out_shape=jax.ShapeDtypeStruct((DB, DEC_ROWS, HD), f32),
        compiler_params=pltpu.CompilerParams(dimension_semantics=("parallel",), vmem_limit_bytes=VMEM_LIMIT_BYTES),
        name="nsa_decode",
    )(page_table.reshape(-1), qs, qf, kcvc, _pad_new_t(rows_new[..., 2 * HD:]), win_t,
      _pad_new_t(win_new), gates, jnp.asarray(e, bf16), *([pages] * N_PAGES))
    return o.reshape(DB, NSA_HEADS, DEC_SEQ, HD).transpose(0, 2, 1, 3).reshape(DB, DEC_SEQ, BRANCH_WIDTH)


def _stack_layers(states):
    return tuple(jnp.stack(list(a)) for a in zip(*states))


def kernel(x_prompt, x_sample, cache_nsa, cache_nsa_win, cache_fox_kv, cache_fox_logf,
           state_mlstm_c, state_mlstm_n, state_mlstm_m, page_table,
           ln_in_w, ln_in_b, w_in, b_in, nsa_w_ck, nsa_w_cv, mlstm_norm_w, w_branch, w_out,
           ln1_w, ln1_b, moe_w_group, moe_b_group, moe_w_expert, moe_b_expert,
           moe_w_gate, moe_w_up, moe_w_down, ln2_w, ln2_b):
    f32, bf16 = jnp.float32, jnp.bfloat16
    alpha = (2.0 * DEPTH) ** 0.25
    B, S, D = x_prompt.shape
    DB, T, _ = x_sample.shape
    TP, TS = B * S, DB * T
    HD = HEAD_DIM
    xp = _layer_norm_rows(x_prompt.reshape(TP, D), ln_in_w, ln_in_b)
    xs = _layer_norm_rows(x_sample.reshape(TS, D), ln_in_w, ln_in_b)
    cols = lambda a, c0, w: a[..., c0:c0 + w]
    colsb = lambda a, c0, w: a[..., c0 - Z_NQ:c0 - Z_NQ + w]
    zero_state = (jnp.zeros((B, MLSTM_HEADS, HD, HD), f32), jnp.zeros((B, MLSTM_HEADS, HD), f32),
                  jnp.zeros((B, MLSTM_HEADS), f32))
    new_p, new_s = [], []
    for l in range(DEPTH):
        zp2, zbp2 = _in_projection(xp, w_in[l], b_in[l])
        zs2, zbs2 = _in_projection(xs, w_in[l], b_in[l])
        zp, zbp = zp2.reshape(B, S, Z_WIDTH), zbp2.reshape(B, S, ZB_WIDTH)
        zs, zbs = zs2.reshape(DB, T, Z_WIDTH), zbs2.reshape(DB, T, ZB_WIDTH)
        small_p, small_s = cols(zp, Z_SMALL, 128), cols(zs, Z_SMALL, 128)

        o_nsa_p = _nsa_prompt(cols(zp, Z_NQ, 256), colsb(zbp, Z_NQ, 256), cols(zp, Z_ROWS, 256),
                              colsb(zbp, Z_ROWS, 256), colsb(zbp, Z_WIN, 128), small_p, nsa_w_ck[l], nsa_w_cv[l])
        o_nsa_s = _nsa_decode(cols(zs, Z_NQ, 256), colsb(zbs, Z_ROWS, 256), colsb(zbs, Z_WIN, 128), small_s,
                              cache_nsa, cache_nsa_win, page_table, nsa_w_ck[l], nsa_w_cv[l], l)

        def mlstm(q, k, v, og, small, state, L, nb, dt):
            return _mlstm(q, k, v, og, cols(small, SMALL_MI, MLSTM_HEADS), cols(small, SMALL_MF, MLSTM_HEADS),
                          mlstm_norm_w[l], *state, L, nb, dt)

        o_ml_p, st_ml_p = mlstm(colsb(zbp, Z_MQ, 256), colsb(zbp, Z_MK, 256), colsb(zbp, Z_MV, 256),
                                cols(zp, Z_MO, 256), small_p, zero_state, 128, B, bf16)
        o_ml_s, st_ml_s = mlstm(cols(zs, Z_MQ, 256), cols(zs, Z_MK, 256), cols(zs, Z_MV, 256), cols(zs, Z_MO, 256),
                                small_s, (state_mlstm_c[l], state_mlstm_n[l], state_mlstm_m[l]), T, 4, f32)

        lf_all_p, cum_all_p = _logf_scan(small_p)
        lf_p, cum_p = cols(lf_all_p, SMALL_FF, FOX_HEADS), cols(cum_all_p, SMALL_FF, FOX_HEADS)
        lf_s = jax.nn.log_sigmoid(cols(small_s, SMALL_FF, FOX_HEADS))
        o_fox_p = _fox_prompt_attn(zbp, cum_p)
        o_fox_s = _fox_decode(colsb(zbs, Z_FQ, 256), colsb(zbs, Z_FK, 256), colsb(zbs, Z_FV, 256), lf_s,
                              cache_fox_kv, cache_fox_logf, page_table, l)

        flat = lambda a: a.reshape(-1, a.shape[-1])
        moe_w = (moe_w_group[l], moe_b_group[l], moe_w_expert[l], moe_b_expert[l],
                 moe_w_gate[l], moe_w_up[l], moe_w_down[l], ln2_w[l], ln2_b[l], alpha)
        xp = _merge_ln(xp, flat(o_nsa_p), flat(o_ml_p), flat(o_fox_p), w_in[l], b_in[l],
                       w_branch[l], w_out[l], ln1_w[l], ln1_b[l], alpha)
        xp = _moe_ln(xp, *moe_w)
        xs = _merge_ln(xs, flat(o_nsa_s), flat(o_ml_s), flat(o_fox_s), w_in[l], b_in[l],
                       w_branch[l], w_out[l], ln1_w[l], ln1_b[l], alpha)
        xs = _moe_ln(xs, *moe_w)

        w_keep = min(NSA_WINDOW, S)
        new_p.append((cols(zp, Z_ROWS, 256).reshape(B, S, NSA_ROWS, HD),
                      cols(zp, Z_WIN, 128)[:, S - w_keep:].reshape(B, w_keep, 2, HD),
                      cols(zp, Z_FK, 512).reshape(B, S, 2, FOX_HEADS, HD), lf_p) + tuple(st_ml_p))
        win_new = cols(zs, Z_WIN, 128).reshape(DB, T, 2, HD).astype(cache_nsa_win.dtype)
        new_s.append((cols(zs, Z_ROWS, 256).reshape(DB, T, NSA_ROWS, HD),
                      win_new,
                      cols(zs, Z_FK, 512).reshape(DB, T, 2, FOX_HEADS, HD), lf_s) + tuple(st_ml_s))
    (p_nsa_rows, p_nsa_win, p_fox_kv, p_fox_logf, p_mlstm_c, p_mlstm_n, p_mlstm_m) = _stack_layers(new_p)
    (s_nsa_rows, s_win_new, s_fox_kv, s_fox_logf, s_mlstm_c, s_mlstm_n, s_mlstm_m) = _stack_layers(new_s)
    s_nsa_win = jnp.concatenate([cache_nsa_win[:, :, T:], s_win_new], axis=2)
    return (xp.reshape(B, S, D), xs.reshape(DB, T, D),
            p_nsa_rows, p_nsa_win, p_fox_kv, p_fox_logf, p_mlstm_c, p_mlstm_n, p_mlstm_m,
            s_nsa_rows, s_nsa_win, s_fox_kv, s_fox_logf, s_mlstm_c, s_mlstm_n, s_mlstm_m)
```
